```python
import math
import jax, jax.numpy as jnp
from jax import lax
import numpy as np

D_MODEL = 1024
BATCH = 4
SEQ = 8192
DEPTH = 4

N_MIXERS = 2
EPS = 1e-6

ML_HEADS = 4
ML_QK_DIM = D_MODEL // 2 // ML_HEADS
ML_V_DIM = D_MODEL // ML_HEADS
ML_QK = ML_HEADS * ML_QK_DIM
ML_V = ML_HEADS * ML_V_DIM
ML_IN = 2 * ML_QK + 2 * ML_V + 2 * ML_HEADS
ML_CHUNK = 64
GATE_CAP = 15.0

SW_Q_HEADS = 16
SW_KV_HEADS = 4
SW_GROUP = SW_Q_HEADS // SW_KV_HEADS
SW_HEAD_DIM = 64
WINDOW = 128
SW_IN = (SW_Q_HEADS + 2 * SW_KV_HEADS) * SW_HEAD_DIM
SW_OUT_IN = SW_Q_HEADS * SW_HEAD_DIM

N_GROUPS = 8
EXPERTS_PER_GROUP = 8
N_EXPERTS = N_GROUPS * EXPERTS_PER_GROUP
TOP_K = 2
D_EXPERT = 384
MOE_BLOCK = 256

kernel_name = "hybrid_mlstm_swa_hmoe_adaln"


def rmsnorm(x, g):
    xf = x.astype(jnp.float32)
    y = xf * lax.rsqrt(jnp.mean(xf * xf, axis=-1, keepdims=True) + EPS)
    return y.astype(x.dtype) * g


def softcap(z):
    return GATE_CAP * jnp.tanh(z / GATE_CAP)


def mlstm_mixer(h, w_in, b_gate, g_out, w_out):
    B, S, _ = h.shape
    H, dk, dv, L = ML_HEADS, ML_QK_DIM, ML_V_DIM, ML_CHUNK
    NC = S // L
    proj = h @ w_in
    q, k, v, o, gates = jnp.split(proj, [ML_QK, 2 * ML_QK, 2 * ML_QK + ML_V, 2 * ML_QK + 2 * ML_V], axis=-1)
    gates = softcap(gates.astype(jnp.float32) + b_gate)
    i_pre = gates[..., :H]
    log_f = jax.nn.log_sigmoid(gates[..., H:])

    def to_chunks(t, d):
        return t.astype(jnp.float32).reshape(B, NC, L, H, d).transpose(1, 0, 3, 2, 4)

    def gate_chunks(t):
        return t.reshape(B, NC, L, H).transpose(1, 0, 3, 2)

    qc = to_chunks(q, dk)
    kc = to_chunks(k, dk) * (dk ** -0.5)
    vc = to_chunks(v, dv)
    ic = gate_chunks(i_pre)
    fc = gate_chunks(log_f)
    causal = jnp.tril(jnp.ones((L, L), dtype=bool))

    def step(carry, inp):
        C, n, m = carry
        qb, kb, vb, ib, fb = inp
        b = jnp.cumsum(fb, axis=-1)
        d_log = jnp.where(causal, b[..., :, None] - b[..., None, :] + ib[..., None, :], -jnp.inf)
        inter = b + m[..., None]
        m_t = jnp.maximum(inter, d_log.max(-1))
        w_intra = jnp.exp(d_log - m_t[..., None])
        w_inter = jnp.exp(inter - m_t)
        s = jnp.einsum('bhtd,bhsd->bhts', qb, kb) * w_intra
        num = w_inter[..., None] * jnp.einsum('bhtd,bhdv->bhtv', qb, C) + jnp.einsum('bhts,bhsv->bhtv', s, vb)
        den = w_inter * jnp.einsum('bhtd,bhd->bht', qb, n) + s.sum(-1)
        hb = num / jnp.maximum(jnp.abs(den), jnp.exp(-m_t))[..., None]
        g = b[..., -1]
        a = g[..., None] - b + ib
        m_new = jnp.maximum(g + m, a.max(-1))
        w_state = jnp.exp(a - m_new[..., None])
        decay = jnp.exp(g + m - m_new)
        C_new = decay[..., None, None] * C + jnp.einsum('bhs,bhsd,bhsv->bhdv', w_state, kb, vb)
        n_new = decay[..., None] * n + jnp.einsum('bhs,bhsd->bhd', w_state, kb)
        return (C_new, n_new, m_new), hb

    init = (jnp.zeros((B, H, dk, dv), jnp.float32), jnp.zeros((B, H, dk), jnp.float32), jnp.zeros((B, H), jnp.float32))
    _, hs = lax.scan(step, init, (qc, kc, vc, ic, fc))
    hs = hs.transpose(1, 0, 3, 2, 4).reshape(B, S, H, dv)
    hs = rmsnorm(hs, g_out.reshape(H, dv)).astype(h.dtype)
    hs = hs * jax.nn.sigmoid(o).reshape(B, S, H, dv)
    return hs.reshape(B, S, ML_V) @ w_out


def swa_mixer(h, w_in, g_q, g_k, sinks, w_out):
    B, S, _ = h.shape
    W, dh = WINDOW, SW_HEAD_DIM
    NB = S // W
    proj = h @ w_in
    q, k, v = jnp.split(proj, [SW_Q_HEADS * dh, (SW_Q_HEADS + SW_KV_HEADS) * dh], axis=-1)
    q = rmsnorm(q.reshape(B, S, SW_Q_HEADS, dh), g_q)
    k = rmsnorm(k.reshape(B, S, SW_KV_HEADS, dh), g_k)
    v = v.reshape(B, S, SW_KV_HEADS, dh)
    qb = q.reshape(B, NB, W, SW_KV_HEADS, SW_GROUP, dh)

    def band(t):
        tb = t.reshape(B, NB, W, SW_KV_HEADS, dh)
        prev = jnp.concatenate([jnp.zeros_like(tb[:, :1]), tb[:, :-1]], axis=1)
        return jnp.concatenate([prev, tb], axis=2)

    kb, vb = band(k), band(v)
    scores = jnp.einsum('bnqhgd,bnkhd->bnhgqk', qb, kb).astype(jnp.float32) * (dh ** -0.5)
    qi = jnp.arange(W)[:, None]
    ki = jnp.arange(2 * W)[None, :]
    rel = qi + W - ki
    in_win = (rel >= 0) & (rel < W)
    first = (jnp.arange(NB)[:, None, None] == 0) & (ki[None] < W)
    mask = in_win[None] & ~first
    scores = jnp.where(mask[None, :, None, None], scores, -jnp.inf)
    sink = sinks.astype(jnp.float32).reshape(SW_KV_HEADS, SW_GROUP)[None, None, :, :, None]
    m = jnp.maximum(scores.max(-1), sink)
    p = jnp.exp(scores - m[..., None])
    denom = p.sum(-1) + jnp.exp(sink - m)
    p = (p / denom[..., None]).astype(v.dtype)
    o = jnp.einsum('bnhgqk,bnkhd->bnqhgd', p, vb)
    return o.reshape(B, S, SW_OUT_IN) @ w_out


def hier_moe(h, w_group, b_group, w_router, b_router, w1, w3, w2):
    B, S, D = h.shape
    T = B * S
    A = T * TOP_K
    xt = h.reshape(T, D)
    grp_logits = (xt @ w_group + b_group).astype(jnp.float32)
    g_sel = jnp.argmax(grp_logits, axis=-1)
    p_grp = jnp.take_along_axis(jax.nn.softmax(grp_logits, axis=-1), g_sel[:, None], axis=-1)
    exp_logits = (xt @ w_router + b_router).astype(jnp.float32).reshape(T, N_GROUPS, EXPERTS_PER_GROUP)
    in_grp = jnp.take_along_axis(exp_logits, g_sel[:, None, None], axis=1)[:, 0]
    top_v, top_i = lax.top_k(in_grp, TOP_K)
    gate = jax.nn.softmax(top_v, axis=-1) * p_grp
    e_idx = (g_sel[:, None] * EXPERTS_PER_GROUP + top_i).reshape(A).astype(jnp.int32)
    tok = jnp.repeat(jnp.arange(T, dtype=jnp.int32), TOP_K)

    order = jnp.argsort(e_idx, stable=True)
    se = e_idx[order]
    counts = jnp.bincount(e_idx, length=N_EXPERTS)
    starts = jnp.cumsum(counts) - counts
    padded = (counts + MOE_BLOCK - 1) // MOE_BLOCK * MOE_BLOCK
    pad_ends = jnp.cumsum(padded)
    pad_starts = pad_ends - padded
    dest = pad_starts[se] + jnp.arange(A, dtype=jnp.int32) - starts[se]
    NB = -(-A // MOE_BLOCK) + N_EXPERTS
    P = NB * MOE_BLOCK
    slot_tok = jnp.full((P,), T, dtype=jnp.int32).at[dest].set(tok[order])
    slot_w = jnp.zeros((P,), jnp.float32).at[dest].set(gate.reshape(A)[order])
    blk_exp = jnp.minimum(jnp.searchsorted(pad_ends, jnp.arange(NB, dtype=jnp.int32) * MOE_BLOCK, side='right'), N_EXPERTS - 1)
    x_slots = jnp.concatenate([xt, jnp.zeros((1, D), xt.dtype)], axis=0)[slot_tok].reshape(NB, MOE_BLOCK, D)

    def expert_block(args):
        xb, e = args
        return (jax.nn.silu(xb @ w1[e]) * (xb @ w3[e])) @ w2[e]

    y_slots = lax.map(expert_block, (x_slots, blk_exp)).reshape(P, D)
    y = jnp.zeros((T + 1, D), h.dtype).at[slot_tok].add(y_slots * slot_w[:, None].astype(h.dtype))
    return y[:T].reshape(B, S, D)


def setup_inputs(seed: int = 0) -> dict:
    key = jax.random.key(seed)
    ks = jax.random.split(key, 24)
    n_ml = (DEPTH + 1) // N_MIXERS
    n_sw = DEPTH // N_MIXERS
    D = D_MODEL
    nrm = lambda k, shape, s: jax.random.normal(k, shape, jnp.float32) * s
    b_i = -1.0 + nrm(ks[6], (n_ml, ML_HEADS), 0.1)
    b_f = 3.0 + nrm(ks[7], (n_ml, ML_HEADS), 0.5)
    return {
        "x": nrm(ks[0], (BATCH, SEQ, D), 1.0),
        "c": nrm(ks[1], (BATCH, D), 1.0),
        "w_ada": nrm(ks[2], (DEPTH, D, 6 * D), 0.5 * D ** -0.5),
        "b_ada": nrm(ks[3], (DEPTH, 6 * D), 0.02),
        "norm1_g": 1.0 + nrm(ks[4], (DEPTH, D), 0.05),
        "norm2_g": 1.0 + nrm(ks[5], (DEPTH, D), 0.05),
        "ml_w_in": nrm(ks[8], (n_ml, D, ML_IN), D ** -0.5),
        "ml_b_gate": jnp.concatenate([b_i, b_f], axis=-1),
        "ml_g_out": 1.0 + nrm(ks[9], (n_ml, ML_V), 0.05),
        "ml_w_out": nrm(ks[10], (n_ml, ML_V, D), ML_V ** -0.5),
        "sw_w_in": nrm(ks[11], (n_sw, D, SW_IN), D ** -0.5),
        "sw_g_q": 1.0 + nrm(ks[12], (n_sw, SW_HEAD_DIM), 0.05),
        "sw_g_k": 1.0 + nrm(ks[13], (n_sw, SW_HEAD_DIM), 0.05),
        "sw_sinks": nrm(ks[14], (n_sw, SW_Q_HEADS), 0.5),
        "sw_w_out": nrm(ks[15], (n_sw, SW_OUT_IN, D), SW_OUT_IN ** -0.5),
        "moe_w_group": nrm(ks[16], (DEPTH, D, N_GROUPS), D ** -0.5),
        "moe_b_group": nrm(ks[17], (DEPTH, N_GROUPS), 0.01),
        "moe_w_router": nrm(ks[18], (DEPTH, D, N_EXPERTS), D ** -0.5),
        "moe_b_router": nrm(ks[19], (DEPTH, N_EXPERTS), 0.01),
        "moe_w1": nrm(ks[20], (DEPTH, N_EXPERTS, D, D_EXPERT), D ** -0.5),
        "moe_w3": nrm(ks[21], (DEPTH, N_EXPERTS, D, D_EXPERT), D ** -0.5),
        "moe_w2": nrm(ks[22], (DEPTH, N_EXPERTS, D_EXPERT, D), D_EXPERT ** -0.5),
    }


def reference(x, c, w_ada, b_ada, norm1_g, norm2_g, ml_w_in, ml_b_gate, ml_g_out, ml_w_out,
              sw_w_in, sw_g_q, sw_g_k, sw_sinks, sw_w_out, moe_w_group, moe_b_group,
              moe_w_router, moe_b_router, moe_w1, moe_w3, moe_w2):
    cond = jax.nn.silu(c)
    for layer in range(DEPTH):
        mod = (cond @ w_ada[layer] + b_ada[layer])[:, None, :]
        sh1, sc1, gt1, sh2, sc2, gt2 = jnp.split(mod, 6, axis=-1)
        hn = rmsnorm(x, norm1_g[layer]) * (1.0 + sc1) + sh1
        j = layer // N_MIXERS
        if layer % N_MIXERS == 0:
            y = mlstm_mixer(hn, ml_w_in[j], ml_b_gate[j], ml_g_out[j], ml_w_out[j])
        else:
            y = swa_mixer(hn, sw_w_in[j], sw_g_q[j], sw_g_k[j], sw_sinks[j], sw_w_out[j])
        x = x + gt1 * y
        hn = rmsnorm(x, norm2_g[layer]) * (1.0 + sc2) + sh2
        x = x + gt2 * hier_moe(hn, moe_w_group[layer], moe_b_group[layer], moe_w_router[layer],
                               moe_b_router[layer], moe_w1[layer], moe_w3[layer], moe_w2[layer])
    return x
```

```python
import functools

import jax
import jax.numpy as jnp
from jax import lax
from jax.experimental import pallas as pl
from jax.experimental.pallas import tpu as pltpu

EPS = 1e-6
GATE_CAP = 15.0

ML_HEADS = 4
ML_DK = 128
ML_DV = 256
ML_QK = ML_HEADS * ML_DK
ML_V = ML_HEADS * ML_DV
ML_GATE_ROWS = 16

SW_Q_HEADS = 16
SW_KV_HEADS = 4
SW_GROUP = SW_Q_HEADS // SW_KV_HEADS
SW_DH = 64
SW_WINDOW = 128
LANES = 128

N_GROUPS = 8
EXPERTS_PER_GROUP = 8
N_EXPERTS = N_GROUPS * EXPERTS_PER_GROUP
MOE_BLOCK = 256
ROUTER_LANES = 128

VMEM_LIMIT = 56 * 1024 * 1024

_NT = (((1,), (1,)), ((), ()))


def _bdot(a, b):
    return jnp.dot(a, b, preferred_element_type=jnp.float32)


def _bdot_nt(a, b):
    return lax.dot_general(a, b, _NT, preferred_element_type=jnp.float32)


def _split_hi_lo(a):
    hi = a.astype(jnp.bfloat16)
    lo = (a - hi.astype(jnp.float32)).astype(jnp.bfloat16)
    return hi, lo


def _params(sem):
    return pltpu.CompilerParams(dimension_semantics=sem, vmem_limit_bytes=VMEM_LIMIT)


def _ada_kernel(c_ref, w_ref, b_ref, o_ref):
    c = c_ref[...]
    cond = c * jax.nn.sigmoid(c)
    c_hi, c_lo = _split_hi_lo(cond)
    w_hi, w_lo = _split_hi_lo(w_ref[...])
    acc = _bdot(c_hi, w_hi) + (_bdot(c_lo, w_hi) + _bdot(c_hi, w_lo))
    o_ref[...] = acc + b_ref[...]


def _ada_mod(c, w_ada, b_ada):
    depth, d, n = w_ada.shape
    bsz = c.shape[0]
    rows = 8
    tn = 768
    c_pad = jnp.zeros((rows, d), jnp.float32).at[:bsz].set(c)
    out = pl.pallas_call(
        _ada_kernel,
        grid=(depth, n // tn),
        in_specs=[
            pl.BlockSpec((rows, d), lambda l, j: (0, 0)),
            pl.BlockSpec((None, d, tn), lambda l, j: (l, 0, j)),
            pl.BlockSpec((None, 1, tn), lambda l, j: (l, 0, j)),
        ],
        out_specs=pl.BlockSpec((None, rows, tn), lambda l, j: (l, 0, j)),
        out_shape=jax.ShapeDtypeStruct((depth, rows, n), jnp.float32),
        compiler_params=_params(("arbitrary", "arbitrary")),
        name="ada_mod",
    )(c_pad, w_ada, b_ada.reshape(depth, 1, n))
    return out[:, :bsz]


def _modulated_norm(x, g, sc, sh):
    y = x * lax.rsqrt(jnp.mean(x * x, axis=-1, keepdims=True) + EPS)
    return y * g * (1.0 + sc) + sh


def _inproj_kernel(*refs, n_main, chunk, q_cols, q_scale, with_ml):
    if with_ml:
        (x_ref, g_ref, sc_ref, sh_ref, w_ref, wk_ref, wgh_ref, wgl_ref,
         o_ref, kt_ref, gt_ref) = refs
    else:
        x_ref, g_ref, sc_ref, sh_ref, w_ref, o_ref = refs
    hn = _modulated_norm(x_ref[...], g_ref[...], sc_ref[...], sh_ref[...])
    hb = hn.astype(jnp.bfloat16)
    for c0 in range(0, n_main, chunk):
        acc = _bdot(hb, w_ref[:, c0:c0 + chunk])
        if q_cols is not None and c0 >= q_cols[0]:
            acc = acc * q_scale
        o_ref[:, c0:c0 + chunk] = acc.astype(o_ref.dtype)
    if with_ml:
        kt_ref[...] = _bdot_nt(wk_ref[...], hb).astype(kt_ref.dtype)
        h_lo = (hn - hb.astype(jnp.float32)).astype(jnp.bfloat16)
        gt_ref[...] = (_bdot_nt(wgh_ref[...], hb)
                       + (_bdot_nt(wgh_ref[...], h_lo) + _bdot_nt(wgl_ref[...], hb)))


def _inproj(x2d, g, sc, sh, w_main, seq, *, ml_extra=None, q_cols=None, q_scale=1.0):
    t, d = x2d.shape
    tm = 512
    n_main = w_main.shape[1]
    tiles_per_batch = seq // tm
    row = lambda i: (i, 0)
    per_batch = lambda i: (i // tiles_per_batch, 0, 0)
    const = lambda i: (0, 0)
    in_specs = [
        pl.BlockSpec((tm, d), row),
        pl.BlockSpec((1, d), const),
        pl.BlockSpec((None, 1, d), per_batch),
        pl.BlockSpec((None, 1, d), per_batch),
        pl.BlockSpec((d, n_main), const),
    ]
    args = [x2d, g.reshape(1, d), sc, sh, w_main]
    out_specs = [pl.BlockSpec((tm, n_main), row)]
    out_shape = [jax.ShapeDtypeStruct((t, n_main), jnp.bfloat16)]
    if ml_extra is not None:
        wk_t, wg_hi, wg_lo = ml_extra
        in_specs += [pl.BlockSpec(wk_t.shape, const),
                     pl.BlockSpec(wg_hi.shape, const),
                     pl.BlockSpec(wg_lo.shape, const)]
        args += [wk_t, wg_hi, wg_lo]
        out_specs += [pl.BlockSpec((wk_t.shape[0], tm), lambda i: (0, i)),
                      pl.BlockSpec((ML_GATE_ROWS, tm), lambda i: (0, i))]
        out_shape += [jax.ShapeDtypeStruct((wk_t.shape[0], t), jnp.bfloat16),
                      jax.ShapeDtypeStruct((ML_GATE_ROWS, t), jnp.float32)]
    kern = functools.partial(_inproj_kernel, n_main=n_main, chunk=512, q_cols=q_cols,
                             q_scale=q_scale, with_ml=ml_extra is not None)
    return pl.pallas_call(
        kern,
        grid=(t // tm,),
        in_specs=in_specs,
        out_specs=out_specs,
        out_shape=out_shape,
        compiler_params=_params(("arbitrary",)),
        name="inproj_ml" if ml_extra is not None else "inproj_sw",
    )(*args)


def _mlstm_kernel(v_ref, o_ref, q_ref, kt_ref, gt_ref, bg_ref, gout_ref, out_ref,
                  c_ref, n_ref, m_ref, *, chunk):
    L = chunk
    H, dk, dv = ML_HEADS, ML_DK, ML_DV

    @pl.when(pl.program_id(1) == 0)
    def _():
        c_ref[...] = jnp.zeros_like(c_ref)
        n_ref[...] = jnp.zeros_like(n_ref)
        m_ref[...] = jnp.zeros_like(m_ref)

    z = gt_ref[...] + bg_ref[...]
    gates = GATE_CAP * jnp.tanh(z / GATE_CAP)
    log_f = jnp.minimum(gates, 0.0) - jnp.log1p(jnp.exp(-jnp.abs(gates)))
    grow = lax.broadcasted_iota(jnp.int32, (ML_GATE_ROWS, L), 0)
    is_i = grow < H
    slab = jnp.where(is_i, gates, log_f)

    r_idx = lax.broadcasted_iota(jnp.int32, (L, L), 0)
    c_idx = lax.broadcasted_iota(jnp.int32, (L, L), 1)
    upper = jnp.where(r_idx <= c_idx, 1.0, 0.0).astype(jnp.bfloat16)
    a1 = slab.astype(jnp.bfloat16)
    r1 = slab - a1.astype(jnp.float32)
    a2 = r1.astype(jnp.bfloat16)
    a3 = (r1 - a2.astype(jnp.float32)).astype(jnp.bfloat16)
    cum = _bdot(a1, upper) + (_bdot(a2, upper) + _bdot(a3, upper))
    rows = jnp.where(is_i, gates, cum)
    cols = jnp.concatenate(
        [rows, jnp.zeros((LANES - ML_GATE_ROWS, L), jnp.float32)], axis=0).T

    causal = r_idx >= c_idx
    for h in range(H):
        i_r = rows[h:h + 1, :]
        b_r = rows[H + h:H + h + 1, :]
        b_c = cols[:, H + h:H + h + 1]
        m_prev = m_ref[h][0:1, 0:1]
        n_row = n_ref[h][0:1, :]
        c_st = c_ref[h]
        q = q_ref[:, h * dk:(h + 1) * dk]
        kt = kt_ref[h * dk:(h + 1) * dk, :]
        v = v_ref[:, h * dv:(h + 1) * dv]

        u_r = i_r - b_r
        d_log = jnp.where(causal, b_c + u_r, -jnp.inf)
        inter = b_c + m_prev
        m_t = jnp.maximum(inter, jnp.max(d_log, axis=-1, keepdims=True))
        w_intra = jnp.exp(d_log - m_t)
        w_inter = jnp.exp(inter - m_t)
        s = _bdot(q, kt) * w_intra
        num = w_inter * _bdot(q, c_st.astype(jnp.bfloat16)) + _bdot(s.astype(jnp.bfloat16), v)
        qn = jnp.sum(q.astype(jnp.float32) * n_row, axis=-1, keepdims=True)
        den = w_inter * qn + jnp.sum(s, axis=-1, keepdims=True)
        hb = num * (1.0 / jnp.maximum(jnp.abs(den), jnp.exp(-m_t)))

        g_last = b_r[:, L - 1:L]
        a_r = g_last - b_r + i_r
        m_new = jnp.maximum(g_last + m_prev, jnp.max(a_r, axis=-1, keepdims=True))
        w_state = jnp.exp(a_r - m_new)
        decay = jnp.exp(g_last + m_prev - m_new)
        kw = (kt.astype(jnp.float32) * w_state).astype(jnp.bfloat16)
        c_ref[h] = decay * c_st + _bdot(kw, v)
        w16 = jnp.broadcast_to(w_state, (16, L)).astype(jnp.bfloat16)
        dn = _bdot_nt(w16, kt)
        n_ref[h] = jnp.broadcast_to(decay * n_row + dn[0:1, :], (8, dk))
        m_ref[h] = jnp.broadcast_to(m_new, (8, LANES))

        y = hb * lax.rsqrt(jnp.mean(hb * hb, axis=-1, keepdims=True) + EPS)
        y = y * gout_ref[:, h * dv:(h + 1) * dv]
        og = jax.nn.sigmoid(o_ref[:, h * dv:(h + 1) * dv].astype(jnp.float32))
        out_ref[:, h * dv:(h + 1) * dv] = (y * og).astype(out_ref.dtype)


def _mlstm_core(main, k_t, g_t, b_gate, g_out, bsz, seq, chunk=256):
    t = main.shape[0]
    nc = seq // chunk
    blk = lambda b, c: b * nc + c
    bg = jnp.zeros((ML_GATE_ROWS, 1), jnp.float32).at[:2 * ML_HEADS, 0].set(b_gate)
    return pl.pallas_call(
        functools.partial(_mlstm_kernel, chunk=chunk),
        grid=(bsz, nc),
        in_specs=[
            pl.BlockSpec((chunk, ML_V), lambda b, c: (blk(b, c), 0)),
            pl.BlockSpec((chunk, ML_V), lambda b, c: (blk(b, c), 1)),
            pl.BlockSpec((chunk, ML_QK), lambda b, c: (blk(b, c), 4)),
            pl.BlockSpec((ML_QK, chunk), lambda b, c: (0, blk(b, c))),
            pl.BlockSpec((ML_GATE_ROWS, chunk), lambda b, c: (0, blk(b, c))),
            pl.BlockSpec((ML_GATE_ROWS, 1), lambda b, c: (0, 0)),
            pl.BlockSpec((1, ML_V), lambda b, c: (0, 0)),
        ],
        out_specs=pl.BlockSpec((chunk, ML_V), lambda b, c: (blk(b, c), 0)),
        out_shape=jax.ShapeDtypeStruct((t, ML_V), jnp.bfloat16),
        scratch_shapes=[
            pltpu.VMEM((ML_HEADS, ML_DK, ML_DV), jnp.float32),
            pltpu.VMEM((ML_HEADS, 8, ML_DK), jnp.float32),
            pltpu.VMEM((ML_HEADS, 8, LANES), jnp.float32),
        ],
        compiler_params=_params(("arbitrary", "arbitrary")),
        name="mlstm_core",
    )(main, main, main, k_t, g_t, bg, g_out.reshape(1, ML_V))


def _outproj_kernel(a_ref, w_ref, x_ref, gt_ref, o_ref):
    o_ref[...] = x_ref[...] + gt_ref[...] * _bdot(a_ref[...], w_ref[...])


def _outproj_residual(a, w, x2d, gt, seq):
    t, d = x2d.shape
    tm = 512
    tiles_per_batch = seq // tm
    return pl.pallas_call(
        _outproj_kernel,
        grid=(t // tm,),
        in_specs=[
            pl.BlockSpec((tm, a.shape[1]), lambda i: (i, 0)),
            pl.BlockSpec(w.shape, lambda i: (0, 0)),
            pl.BlockSpec((tm, d), lambda i: (i, 0)),
            pl.BlockSpec((None, 1, d), lambda i: (i // tiles_per_batch, 0, 0)),
        ],
        out_specs=pl.BlockSpec((tm, d), lambda i: (i, 0)),
        out_shape=jax.ShapeDtypeStruct((t, d), jnp.float32),
        compiler_params=_params(("arbitrary",)),
        name="outproj_residual",
    )(a, w, x2d, gt)


def _swa_kernel(sinks_ref, q_ref, kc_ref, kp_ref, vc_ref, vp_ref, gq_ref, gk_ref, o_ref):
    W = SW_WINDOW
    first = pl.program_id(1) == 0
    lane = lax.broadcasted_iota(jnp.int32, (W, LANES), 1)
    low = lane < SW_DH
    rows4 = lax.broadcasted_iota(jnp.int32, (SW_GROUP * W, 2 * W), 0)
    keys4 = lax.broadcasted_iota(jnp.int32, (SW_GROUP * W, 2 * W), 1)
    rel = (rows4 & (W - 1)) + W - keys4
    key_min = jnp.where(first, W, 0)
    mask = (rel >= 0) & (rel < W) & (keys4 >= key_min)
    row_col = lax.broadcasted_iota(jnp.int32, (SW_GROUP * W, 1), 0)
    gq = gq_ref[...]
    gk = gk_ref[...]
    for g in range(SW_KV_HEADS):
        sl = slice(g * LANES, (g + 1) * LANES)
        k2 = jnp.concatenate([kp_ref[:, sl], kc_ref[:, sl]], axis=0).astype(jnp.float32)
        kn = k2 * lax.rsqrt(jnp.mean(k2 * k2, axis=-1, keepdims=True) + EPS)
        kn = (kn * gk).astype(jnp.bfloat16)
        v2 = jnp.concatenate([vp_ref[:, sl], vc_ref[:, sl]], axis=0)
        parts = []
        for p in range(2):
            c0 = g * SW_GROUP * SW_DH + p * LANES
            qp = q_ref[:, c0:c0 + LANES].astype(jnp.float32)
            sq = qp * qp
            ss_lo = jnp.sum(jnp.where(low, sq, 0.0), axis=-1, keepdims=True)
            ss_hi = jnp.sum(jnp.where(low, 0.0, sq), axis=-1, keepdims=True)
            rs = jnp.where(low, lax.rsqrt(ss_lo / SW_DH + EPS), lax.rsqrt(ss_hi / SW_DH + EPS))
            qn = qp * rs * gq * (SW_DH ** -0.5)
            parts.append(jnp.where(low, qn, 0.0).astype(jnp.bfloat16))
            parts.append(jnp.where(low, 0.0, qn).astype(jnp.bfloat16))
        q4 = jnp.concatenate(parts, axis=0)
        scores = jnp.where(mask, _bdot_nt(q4, kn), -jnp.inf)
        sink = jnp.full((SW_GROUP * W, 1), sinks_ref[g * SW_GROUP + SW_GROUP - 1], jnp.float32)
        for j in range(SW_GROUP - 2, -1, -1):
            sink = jnp.where(row_col < (j + 1) * W, sinks_ref[g * SW_GROUP + j], sink)
        m = jnp.maximum(jnp.max(scores, axis=-1, keepdims=True), sink)
        pexp = jnp.exp(scores - m)
        denom = jnp.sum(pexp, axis=-1, keepdims=True) + jnp.exp(sink - m)
        prob = (pexp / denom).astype(jnp.bfloat16)
        o4 = _bdot(prob, v2)
        for p in range(2):
            oa = o4[(2 * p) * W:(2 * p + 1) * W]
            ob = o4[(2 * p + 1) * W:(2 * p + 2) * W]
            c0 = g * SW_GROUP * SW_DH + p * LANES
            o_ref[:, c0:c0 + LANES] = jnp.where(low, oa, ob).astype(o_ref.dtype)


def _swa_core(proj, sinks, g_q, g_k, bsz, seq):
    t = proj.shape[0]
    W = SW_WINDOW
    nb = seq // W
    dq = SW_Q_HEADS * SW_DH
    kv_w = SW_KV_HEADS * LANES
    k_blk = dq // kv_w
    v_blk = k_blk + 1
    cur = lambda b, n, s: b * nb + n
    prev = lambda b, n, s: b * nb + jnp.maximum(n - 1, 0)
    gq2 = jnp.concatenate([g_q, g_q]).reshape(1, LANES)
    gk2 = jnp.concatenate([g_k, g_k]).reshape(1, LANES)
    grid_spec = pltpu.PrefetchScalarGridSpec(
        num_scalar_prefetch=1,
        grid=(bsz, nb),
        in_specs=[
            pl.BlockSpec((W, dq), lambda b, n, s: (cur(b, n, s), 0)),
            pl.BlockSpec((W, kv_w), lambda b, n, s: (cur(b, n, s), k_blk)),
            pl.BlockSpec((W, kv_w), lambda b, n, s: (prev(b, n, s), k_blk)),
            pl.BlockSpec((W, kv_w), lambda b, n, s: (cur(b, n, s), v_blk)),
            pl.BlockSpec((W, kv_w), lambda b, n, s: (prev(b, n, s), v_blk)),
            pl.BlockSpec((1, LANES), lambda b, n, s: (0, 0)),
            pl.BlockSpec((1, LANES), lambda b, n, s: (0, 0)),
        ],
        out_specs=pl.BlockSpec((W, dq), lambda b, n, s: (cur(b, n, s), 0)),
    )
    return pl.pallas_call(
        _swa_kernel,
        grid_spec=grid_spec,
        out_shape=jax.ShapeDtypeStruct((t, dq), jnp.bfloat16),
        compiler_params=_params(("arbitrary", "arbitrary")),
        name="swa_core",
    )(sinks.astype(jnp.float32), proj, proj, proj, proj, proj, gq2, gk2)


def _router_kernel(x_ref, g_ref, sc_ref, sh_ref, wh_ref, wl_ref, b_ref,
                   hn_ref, info_ref, cnt_ref, carry_ref, *, tm):
    @pl.when(pl.program_id(0) == 0)
    def _():
        carry_ref[...] = jnp.zeros_like(carry_ref)

    hn = _modulated_norm(x_ref[...], g_ref[...], sc_ref[...], sh_ref[...])
    hn_ref[...] = hn
    h_hi, h_lo = _split_hi_lo(hn)
    logits = (_bdot(h_hi, wh_ref[...]) + (_bdot(h_lo, wh_ref[...]) + _bdot(h_hi, wl_ref[...]))
              + b_ref[...])
    lane = lax.broadcasted_iota(jnp.int32, (tm, ROUTER_LANES), 1)
    big = jnp.int32(4 * ROUTER_LANES)
    neg = -jnp.inf

    is_grp = (lane >= N_EXPERTS) & (lane < N_EXPERTS + N_GROUPS)
    gl = jnp.where(is_grp, logits, neg)
    gmax = jnp.max(gl, axis=-1, keepdims=True)
    gsel = jnp.min(jnp.where(gl == gmax, lane, big), axis=-1, keepdims=True) - N_EXPERTS
    p_grp = 1.0 / jnp.sum(jnp.exp(gl - gmax), axis=-1, keepdims=True)

    lo_lane = gsel * EXPERTS_PER_GROUP
    in_grp = (lane >= lo_lane) & (lane < lo_lane + EXPERTS_PER_GROUP)
    el = jnp.where(in_grp, logits, neg)
    v1 = jnp.max(el, axis=-1, keepdims=True)
    i1 = jnp.min(jnp.where(el == v1, lane, big), axis=-1, keepdims=True)
    el2 = jnp.where(lane == i1, neg, el)
    v2 = jnp.max(el2, axis=-1, keepdims=True)
    i2 = jnp.min(jnp.where(el2 == v2, lane, big), axis=-1, keepdims=True)
    e21 = jnp.exp(v2 - v1)
    gate1 = p_grp / (1.0 + e21)
    gate2 = p_grp * e21 / (1.0 + e21)

    hit1 = lane == i1
    hit2 = lane == i2
    onehot = jnp.where(hit1 | hit2, 1.0, 0.0)
    r_idx = lax.broadcasted_iota(jnp.int32, (tm, tm), 0)
    c_idx = lax.broadcasted_iota(jnp.int32, (tm, tm), 1)
    strict_lower = jnp.where(c_idx < r_idx, 1.0, 0.0).astype(jnp.bfloat16)
    before = _bdot(strict_lower, onehot.astype(jnp.bfloat16)) + carry_ref[0:1, :]
    rank1 = jnp.sum(jnp.where(hit1, before, 0.0), axis=-1, keepdims=True)
    rank2 = jnp.sum(jnp.where(hit2, before, 0.0), axis=-1, keepdims=True)
    total = carry_ref[0:1, :] + jnp.sum(onehot, axis=0, keepdims=True)
    carry_ref[...] = jnp.broadcast_to(total, carry_ref.shape)
    cnt_ref[...] = jnp.broadcast_to(total, cnt_ref.shape)

    info = jnp.where(lane == 0, i1.astype(jnp.float32), 0.0)
    info = jnp.where(lane == 1, i2.astype(jnp.float32), info)
    info = jnp.where(lane == 2, rank1, info)
    info = jnp.where(lane == 3, rank2, info)
    info = jnp.where(lane == 4, gate1, info)
    info = jnp.where(lane == 5, gate2, info)
    info_ref[...] = info


def _router(x2d, g, sc, sh, w_hi, w_lo, bias, seq):
    t, d = x2d.shape
    tm = 512
    tiles_per_batch = seq // tm
    per_batch = lambda i: (i // tiles_per_batch, 0, 0)
    const = lambda i: (0, 0)
    return pl.pallas_call(
        functools.partial(_router_kernel, tm=tm),
        grid=(t // tm,),
        in_specs=[
            pl.BlockSpec((tm, d), lambda i: (i, 0)),
            pl.BlockSpec((1, d), const),
            pl.BlockSpec((None, 1, d), per_batch),
            pl.BlockSpec((None, 1, d), per_batch),
            pl.BlockSpec((d, ROUTER_LANES), const),
            pl.BlockSpec((d, ROUTER_LANES), const),
            pl.BlockSpec((1, ROUTER_LANES), const),
        ],
        out_specs=[
            pl.BlockSpec((tm, d), lambda i: (i, 0)),
            pl.BlockSpec((tm, ROUTER_LANES), lambda i: (i, 0)),
            pl.BlockSpec((8, ROUTER_LANES), const),
        ],
        out_shape=[
            jax.ShapeDtypeStruct((t, d), jnp.float32),
            jax.ShapeDtypeStruct((t, ROUTER_LANES), jnp.float32),
            jax.ShapeDtypeStruct((8, ROUTER_LANES), jnp.float32),
        ],
        scratch_shapes=[pltpu.VMEM((8, ROUTER_LANES), jnp.float32)],
        compiler_params=_params(("arbitrary",)),
        name="moe_router",
    )(x2d, g.reshape(1, d), sc, sh, w_hi, w_lo, bias)


def _gather_kernel(idx_ref, src_ref, out_ref, sems, *, rows_per_step, n_steps):
    i = pl.program_id(0)
    base = i * rows_per_step

    def row_copy(r, slot):
        return pltpu.make_async_copy(src_ref.at[idx_ref[base + r]], out_ref.at[base + r],
                                     sems.at[slot])

    def start(r, carry):
        row_copy(r, 0).start()
        return carry

    def wait(r, carry):
        row_copy(r, 0).wait()
        return carry

    lax.fori_loop(0, rows_per_step, start, 0)
    lax.fori_loop(0, rows_per_step, wait, 0)


def _gather_rows(src, idx):
    n_out = idx.shape[0]
    rows_per_step = next(r for r in (1024, 512, 256) if n_out % r == 0)
    n_steps = n_out // rows_per_step
    grid_spec = pltpu.PrefetchScalarGridSpec(
        num_scalar_prefetch=1,
        grid=(n_steps,),
        in_specs=[pl.BlockSpec(memory_space=pl.ANY)],
        out_specs=pl.BlockSpec(memory_space=pl.ANY),
        scratch_shapes=[pltpu.SemaphoreType.DMA((1,))],
    )
    return pl.pallas_call(
        functools.partial(_gather_kernel, rows_per_step=rows_per_step, n_steps=n_steps),
        grid_spec=grid_spec,
        out_shape=jax.ShapeDtypeStruct((n_out, src.shape[1]), src.dtype),
        compiler_params=_params(("arbitrary",)),
        name="gather_rows",
    )(idx, src)


def _expert_kernel(blk_ref, x_ref, w1_ref, w3_ref, w2_ref, y_ref):
    used = pl.program_id(0) < blk_ref[blk_ref.shape[0] - 1]

    @pl.when(used)
    def _():
        xb = x_ref[...].astype(jnp.bfloat16)
        h1 = _bdot(xb, w1_ref[...].astype(jnp.bfloat16))
        h3 = _bdot(xb, w3_ref[...].astype(jnp.bfloat16))
        act = (h1 * jax.nn.sigmoid(h1) * h3).astype(jnp.bfloat16)
        y_ref[...] = _bdot(act, w2_ref[...].astype(jnp.bfloat16))

    @pl.when(jnp.logical_not(used))
    def _():
        y_ref[...] = jnp.zeros_like(y_ref)


def _experts(x_slots, blk_meta, w1, w3, w2, layer):
    p, d = x_slots.shape
    nb = p // MOE_BLOCK
    de = w1.shape[-1]
    n_used = lambda s: s[nb]
    rowblk = lambda i, s: (jnp.minimum(i, n_used(s) - 1), 0)
    grid_spec = pltpu.PrefetchScalarGridSpec(
        num_scalar_prefetch=1,
        grid=(nb,),
        in_specs=[
            pl.BlockSpec((MOE_BLOCK, d), rowblk),
            pl.BlockSpec((None, None, d, de), lambda i, s: (layer, s[i], 0, 0)),
            pl.BlockSpec((None, None, d, de), lambda i, s: (layer, s[i], 0, 0)),
            pl.BlockSpec((None, None, de, d), lambda i, s: (layer, s[i], 0, 0)),
        ],
        out_specs=pl.BlockSpec((MOE_BLOCK, d), lambda i, s: (i, 0)),
    )
    return pl.pallas_call(
        _expert_kernel,
        grid_spec=grid_spec,
        out_shape=jax.ShapeDtypeStruct((p, d), jnp.float32),
        compiler_params=_params(("arbitrary",)),
        name="moe_experts",
    )(blk_meta, x_slots, w1, w3, w2)


def _combine_kernel(x_ref, y_ref, info_ref, gt_ref, o_ref):
    d = x_ref.shape[1]
    info = info_ref[...]
    g1 = info[:, 4:5]
    g2 = info[:, 5:6]
    y = g1 * y_ref[:, :d] + g2 * y_ref[:, d:]
    o_ref[...] = x_ref[...] + gt_ref[...] * y


def _combine(x2d, y_pairs, info, gt, seq):
    t, d = x2d.shape
    tm = 512
    tiles_per_batch = seq // tm
    return pl.pallas_call(
        _combine_kernel,
        grid=(t // tm,),
        in_specs=[
            pl.BlockSpec((tm, d), lambda i: (i, 0)),
            pl.BlockSpec((tm, 2 * d), lambda i: (i, 0)),
            pl.BlockSpec((tm, ROUTER_LANES), lambda i: (i, 0)),
            pl.BlockSpec((None, 1, d), lambda i: (i // tiles_per_batch, 0, 0)),
        ],
        out_specs=pl.BlockSpec((tm, d), lambda i: (i, 0)),
        out_shape=jax.ShapeDtypeStruct((t, d), jnp.float32),
        compiler_params=_params(("arbitrary",)),
        name="moe_combine",
    )(x2d, y_pairs, info, gt)


def _moe_layer(x2d, g, sc, sh, gt, w_group, b_group, w_router, b_router, w1, w3, w2, layer, seq):
    t, d = x2d.shape
    w_cat = jnp.zeros((d, ROUTER_LANES), jnp.float32)
    w_cat = w_cat.at[:, :N_EXPERTS].set(w_router).at[:, N_EXPERTS:N_EXPERTS + N_GROUPS].set(w_group)
    b_cat = jnp.zeros((1, ROUTER_LANES), jnp.float32)
    b_cat = b_cat.at[0, :N_EXPERTS].set(b_router).at[0, N_EXPERTS:N_EXPERTS + N_GROUPS].set(b_group)
    w_hi, w_lo = _split_hi_lo(w_cat)
    hn, info, cnt = _router(x2d, g, sc, sh, w_hi, w_lo, b_cat, seq)

    counts = cnt[0, :N_EXPERTS].astype(jnp.int32)
    padded = (counts + MOE_BLOCK - 1) // MOE_BLOCK * MOE_BLOCK
    pad_ends = jnp.cumsum(padded)
    pad_starts = pad_ends - padded
    e_idx = info[:, 0:2].astype(jnp.int32)
    rank = info[:, 2:4].astype(jnp.int32)
    dest = (pad_starts[e_idx] + rank).reshape(-1)
    nb = -(-(2 * t) // MOE_BLOCK) + N_EXPERTS
    tok = jnp.repeat(jnp.arange(t, dtype=jnp.int32), 2)
    slot_tok = jnp.zeros((nb * MOE_BLOCK,), jnp.int32).at[dest].set(tok)
    blk_exp = jnp.minimum(
        jnp.searchsorted(pad_ends, jnp.arange(nb, dtype=jnp.int32) * MOE_BLOCK, side='right'),
        N_EXPERTS - 1).astype(jnp.int32)
    n_used = (pad_ends[-1] // MOE_BLOCK).astype(jnp.int32)
    blk_meta = jnp.concatenate([blk_exp, n_used[None]])

    x_slots = _gather_rows(hn, slot_tok)
    y_slots = _experts(x_slots, blk_meta, w1, w3, w2, layer)
    y_pairs = _gather_rows(y_slots, dest).reshape(t, 2 * d)
    return _combine(x2d, y_pairs, info, gt, seq)


def kernel(x, c, w_ada, b_ada, norm1_g, norm2_g, ml_w_in, ml_b_gate, ml_g_out, ml_w_out,
           sw_w_in, sw_g_q, sw_g_k, sw_sinks, sw_w_out, moe_w_group, moe_b_group,
           moe_w_router, moe_b_router, moe_w1, moe_w3, moe_w2):
    bsz, seq, d = x.shape
    depth = w_ada.shape[0]
    bf = jnp.bfloat16
    mod = _ada_mod(c, w_ada, b_ada)
    x2d = x.reshape(bsz * seq, d)
    for layer in range(depth):
        parts = [mod[layer, :, i * d:(i + 1) * d].reshape(bsz, 1, d) for i in range(6)]
        sh1, sc1, gt1, sh2, sc2, gt2 = parts
        j = layer // 2
        if layer % 2 == 0:
            w = ml_w_in[j]
            q_w, k_w = w[:, :ML_QK], w[:, ML_QK:2 * ML_QK]
            v_w = w[:, 2 * ML_QK:2 * ML_QK + ML_V]
            o_w = w[:, 2 * ML_QK + ML_V:2 * ML_QK + 2 * ML_V]
            g_w = w[:, 2 * ML_QK + 2 * ML_V:]
            w_main = jnp.concatenate([v_w, o_w, q_w], axis=1).astype(bf)
            wg_t = jnp.zeros((ML_GATE_ROWS, d), jnp.float32).at[:2 * ML_HEADS].set(g_w.T)
            wg_hi, wg_lo = _split_hi_lo(wg_t)
            main, k_t, g_t = _inproj(
                x2d, norm1_g[layer], sc1, sh1, w_main, seq,
                ml_extra=(k_w.T.astype(bf), wg_hi, wg_lo),
                q_cols=(2 * ML_V, 2 * ML_V + ML_QK), q_scale=ML_DK ** -0.5)
            a = _mlstm_core(main, k_t, g_t, ml_b_gate[j], ml_g_out[j], bsz, seq)
            x2d = _outproj_residual(a, ml_w_out[j].astype(bf), x2d, gt1, seq)
        else:
            w = sw_w_in[j]
            dq = SW_Q_HEADS * SW_DH
            dkv = SW_KV_HEADS * SW_DH
            dup = lambda m: jnp.concatenate(
                [m.reshape(d, SW_KV_HEADS, 1, SW_DH)] * 2, axis=2).reshape(d, 2 * dkv)
            w_main = jnp.concatenate(
                [w[:, :dq], dup(w[:, dq:dq + dkv]), dup(w[:, dq + dkv:])], axis=1).astype(bf)
            (proj,) = _inproj(x2d, norm1_g[layer], sc1, sh1, w_main, seq)
            a = _swa_core(proj, sw_sinks[j], sw_g_q[j], sw_g_k[j], bsz, seq)
            x2d = _outproj_residual(a, sw_w_out[j].astype(bf), x2d, gt1, seq)
        x2d = _moe_layer(x2d, norm2_g[layer], sc2, sh2, gt2, moe_w_group[layer],
                         moe_b_group[layer], moe_w_router[layer], moe_b_router[layer],
                         moe_w1, moe_w3, moe_w2, layer, seq)
    return x2d.reshape(bsz, seq, d)
```

```python
import functools

import jax
import jax.numpy as jnp
from jax import lax
from jax.experimental import pallas as pl
from jax.experimental.pallas import tpu as pltpu
from jax.experimental.pallas import tpu_sc as plsc

EPS = 1e-6
GATE_CAP = 15.0

ML_HEADS = 4
ML_DK = 128
ML_DV = 256
ML_QK = ML_HEADS * ML_DK
ML_V = ML_HEADS * ML_DV
ML_GATE_ROWS = 16

SW_Q_HEADS = 16
SW_KV_HEADS = 4
SW_GROUP = SW_Q_HEADS // SW_KV_HEADS
SW_DH = 64
SW_WINDOW = 128
LANES = 128

N_GROUPS = 8
EXPERTS_PER_GROUP = 8
N_EXPERTS = N_GROUPS * EXPERTS_PER_GROUP
MOE_BLOCK = 256
ROUTER_LANES = 128
SC_WINDOW = 128
SC_COLS = 256

VMEM_LIMIT = 56 * 1024 * 1024

_NT = (((1,), (1,)), ((), ()))


def _bdot(a, b):
    return jnp.dot(a, b, preferred_element_type=jnp.float32)


def _bdot_nt(a, b):
    return lax.dot_general(a, b, _NT, preferred_element_type=jnp.float32)


def _split_hi_lo(a):
    hi = a.astype(jnp.bfloat16)
    lo = (a - hi.astype(jnp.float32)).astype(jnp.bfloat16)
    return hi, lo


def _params(sem):
    return pltpu.CompilerParams(dimension_semantics=sem, vmem_limit_bytes=VMEM_LIMIT)


def _ada_kernel(c_ref, w_ref, b_ref, o_ref):
    c = c_ref[...]
    cond = c * jax.nn.sigmoid(c)
    c_hi, c_lo = _split_hi_lo(cond)
    w_hi, w_lo = _split_hi_lo(w_ref[...])
    acc = _bdot(c_hi, w_hi) + (_bdot(c_lo, w_hi) + _bdot(c_hi, w_lo))
    o_ref[...] = acc + b_ref[...]


def _ada_mod(c, w_ada, b_ada):
    depth, d, n = w_ada.shape
    bsz = c.shape[0]
    rows = 8
    tn = 768
    c_pad = jnp.zeros((rows, d), jnp.float32).at[:bsz].set(c)
    out = pl.pallas_call(
        _ada_kernel,
        grid=(depth, n // tn),
        in_specs=[
            pl.BlockSpec((rows, d), lambda l, j: (0, 0)),
            pl.BlockSpec((None, d, tn), lambda l, j: (l, 0, j)),
            pl.BlockSpec((None, 1, tn), lambda l, j: (l, 0, j)),
        ],
        out_specs=pl.BlockSpec((None, rows, tn), lambda l, j: (l, 0, j)),
        out_shape=jax.ShapeDtypeStruct((depth, rows, n), jnp.float32),
        compiler_params=_params(("arbitrary", "arbitrary")),
        name="ada_mod",
    )(c_pad, w_ada, b_ada.reshape(depth, 1, n))
    return out[:, :bsz]


def _modulated_norm(x, g, sc, sh):
    y = x * lax.rsqrt(jnp.mean(x * x, axis=-1, keepdims=True) + EPS)
    return y * g * (1.0 + sc) + sh


def _inproj_kernel(*refs, n_main, chunk, q_cols, q_scale, with_ml):
    if with_ml:
        (x_ref, g_ref, sc_ref, sh_ref, w_ref, wk_ref, wgh_ref, wgl_ref,
         o_ref, kt_ref, gt_ref) = refs
    else:
        x_ref, g_ref, sc_ref, sh_ref, w_ref, o_ref = refs
    hn = _modulated_norm(x_ref[...], g_ref[...], sc_ref[...], sh_ref[...])
    hb = hn.astype(jnp.bfloat16)
    for c0 in range(0, n_main, chunk):
        acc = _bdot(hb, w_ref[:, c0:c0 + chunk])
        if q_cols is not None and c0 >= q_cols[0]:
            acc = acc * q_scale
        o_ref[:, c0:c0 + chunk] = acc.astype(o_ref.dtype)
    if with_ml:
        kt_ref[...] = _bdot_nt(wk_ref[...], hb).astype(kt_ref.dtype)
        h_lo = (hn - hb.astype(jnp.float32)).astype(jnp.bfloat16)
        gt_ref[...] = (_bdot_nt(wgh_ref[...], hb)
                       + (_bdot_nt(wgh_ref[...], h_lo) + _bdot_nt(wgl_ref[...], hb)))


def _inproj(x2d, g, sc, sh, w_main, seq, *, ml_extra=None, q_cols=None, q_scale=1.0):
    t, d = x2d.shape
    tm = 512
    n_main = w_main.shape[1]
    tiles_per_batch = seq // tm
    row = lambda i: (i, 0)
    per_batch = lambda i: (i // tiles_per_batch, 0, 0)
    const = lambda i: (0, 0)
    in_specs = [
        pl.BlockSpec((tm, d), row),
        pl.BlockSpec((1, d), const),
        pl.BlockSpec((None, 1, d), per_batch),
        pl.BlockSpec((None, 1, d), per_batch),
        pl.BlockSpec((d, n_main), const),
    ]
    args = [x2d, g.reshape(1, d), sc, sh, w_main]
    out_specs = [pl.BlockSpec((tm, n_main), row)]
    out_shape = [jax.ShapeDtypeStruct((t, n_main), jnp.bfloat16)]
    if ml_extra is not None:
        wk_t, wg_hi, wg_lo = ml_extra
        in_specs += [pl.BlockSpec(wk_t.shape, const),
                     pl.BlockSpec(wg_hi.shape, const),
                     pl.BlockSpec(wg_lo.shape, const)]
        args += [wk_t, wg_hi, wg_lo]
        out_specs += [pl.BlockSpec((wk_t.shape[0], tm), lambda i: (0, i)),
                      pl.BlockSpec((ML_GATE_ROWS, tm), lambda i: (0, i))]
        out_shape += [jax.ShapeDtypeStruct((wk_t.shape[0], t), jnp.bfloat16),
                      jax.ShapeDtypeStruct((ML_GATE_ROWS, t), jnp.float32)]
    kern = functools.partial(_inproj_kernel, n_main=n_main, chunk=512, q_cols=q_cols,
                             q_scale=q_scale, with_ml=ml_extra is not None)
    return pl.pallas_call(
        kern,
        grid=(t // tm,),
        in_specs=in_specs,
        out_specs=out_specs,
        out_shape=out_shape,
        compiler_params=_params(("arbitrary",)),
        name="inproj_ml" if ml_extra is not None else "inproj_sw",
    )(*args)


def _mlstm_kernel(v_ref, o_ref, q_ref, kt_ref, gt_ref, bg_ref, gout_ref, out_ref,
                  c_ref, n_ref, m_ref, *, chunk):
    L = chunk
    H, dk, dv = ML_HEADS, ML_DK, ML_DV

    @pl.when(pl.program_id(1) == 0)
    def _():
        c_ref[...] = jnp.zeros_like(c_ref)
        n_ref[...] = jnp.zeros_like(n_ref)
        m_ref[...] = jnp.zeros_like(m_ref)

    z = gt_ref[...] + bg_ref[...]
    gates = GATE_CAP * jnp.tanh(z / GATE_CAP)
    log_f = jnp.minimum(gates, 0.0) - jnp.log1p(jnp.exp(-jnp.abs(gates)))
    grow = lax.broadcasted_iota(jnp.int32, (ML_GATE_ROWS, L), 0)
    is_i = grow < H
    slab = jnp.where(is_i, gates, log_f)

    r_idx = lax.broadcasted_iota(jnp.int32, (L, L), 0)
    c_idx = lax.broadcasted_iota(jnp.int32, (L, L), 1)
    upper = jnp.where(r_idx <= c_idx, 1.0, 0.0).astype(jnp.bfloat16)
    a1 = slab.astype(jnp.bfloat16)
    r1 = slab - a1.astype(jnp.float32)
    a2 = r1.astype(jnp.bfloat16)
    a3 = (r1 - a2.astype(jnp.float32)).astype(jnp.bfloat16)
    cum = _bdot(a1, upper) + (_bdot(a2, upper) + _bdot(a3, upper))
    rows = jnp.where(is_i, gates, cum)
    cols = jnp.concatenate(
        [rows, jnp.zeros((LANES - ML_GATE_ROWS, L), jnp.float32)], axis=0).T

    causal = r_idx >= c_idx
    for h in range(H):
        i_r = rows[h:h + 1, :]
        b_r = rows[H + h:H + h + 1, :]
        b_c = cols[:, H + h:H + h + 1]
        m_prev = m_ref[h][0:1, 0:1]
        n_row = n_ref[h][0:1, :]
        c_st = c_ref[h]
        q = q_ref[:, h * dk:(h + 1) * dk]
        kt = kt_ref[h * dk:(h + 1) * dk, :]
        v = v_ref[:, h * dv:(h + 1) * dv]

        u_r = i_r - b_r
        d_log = jnp.where(causal, b_c + u_r, -jnp.inf)
        inter = b_c + m_prev
        m_t = jnp.maximum(inter, jnp.max(d_log, axis=-1, keepdims=True))
        w_intra = jnp.exp(d_log - m_t)
        w_inter = jnp.exp(inter - m_t)
        s = _bdot(q, kt) * w_intra
        num = w_inter * _bdot(q, c_st.astype(jnp.bfloat16)) + _bdot(s.astype(jnp.bfloat16), v)
        qn = jnp.sum(q.astype(jnp.float32) * n_row, axis=-1, keepdims=True)
        den = w_inter * qn + jnp.sum(s, axis=-1, keepdims=True)
        hb = num * (1.0 / jnp.maximum(jnp.abs(den), jnp.exp(-m_t)))

        g_last = b_r[:, L - 1:L]
        a_r = g_last - b_r + i_r
        m_new = jnp.maximum(g_last + m_prev, jnp.max(a_r, axis=-1, keepdims=True))
        w_state = jnp.exp(a_r - m_new)
        decay = jnp.exp(g_last + m_prev - m_new)
        kw = (kt.astype(jnp.float32) * w_state).astype(jnp.bfloat16)
        c_ref[h] = decay * c_st + _bdot(kw, v)
        w16 = jnp.broadcast_to(w_state, (16, L)).astype(jnp.bfloat16)
        dn = _bdot_nt(w16, kt)
        n_ref[h] = jnp.broadcast_to(decay * n_row + dn[0:1, :], (8, dk))
        m_ref[h] = jnp.broadcast_to(m_new, (8, LANES))

        y = hb * lax.rsqrt(jnp.mean(hb * hb, axis=-1, keepdims=True) + EPS)
        y = y * gout_ref[:, h * dv:(h + 1) * dv]
        og = jax.nn.sigmoid(o_ref[:, h * dv:(h + 1) * dv].astype(jnp.float32))
        out_ref[:, h * dv:(h + 1) * dv] = (y * og).astype(out_ref.dtype)


def _mlstm_core(main, k_t, g_t, b_gate, g_out, bsz, seq, chunk=256):
    t = main.shape[0]
    nc = seq // chunk
    blk = lambda b, c: b * nc + c
    bg = jnp.zeros((ML_GATE_ROWS, 1), jnp.float32).at[:2 * ML_HEADS, 0].set(b_gate)
    return pl.pallas_call(
        functools.partial(_mlstm_kernel, chunk=chunk),
        grid=(bsz, nc),
        in_specs=[
            pl.BlockSpec((chunk, ML_V), lambda b, c: (blk(b, c), 0)),
            pl.BlockSpec((chunk, ML_V), lambda b, c: (blk(b, c), 1)),
            pl.BlockSpec((chunk, ML_QK), lambda b, c: (blk(b, c), 4)),
            pl.BlockSpec((ML_QK, chunk), lambda b, c: (0, blk(b, c))),
            pl.BlockSpec((ML_GATE_ROWS, chunk), lambda b, c: (0, blk(b, c))),
            pl.BlockSpec((ML_GATE_ROWS, 1), lambda b, c: (0, 0)),
            pl.BlockSpec((1, ML_V), lambda b, c: (0, 0)),
        ],
        out_specs=pl.BlockSpec((chunk, ML_V), lambda b, c: (blk(b, c), 0)),
        out_shape=jax.ShapeDtypeStruct((t, ML_V), jnp.bfloat16),
        scratch_shapes=[
            pltpu.VMEM((ML_HEADS, ML_DK, ML_DV), jnp.float32),
            pltpu.VMEM((ML_HEADS, 8, ML_DK), jnp.float32),
            pltpu.VMEM((ML_HEADS, 8, LANES), jnp.float32),
        ],
        compiler_params=_params(("arbitrary", "arbitrary")),
        name="mlstm_core",
    )(main, main, main, k_t, g_t, bg, g_out.reshape(1, ML_V))


def _outproj_kernel(a_ref, w_ref, x_ref, gt_ref, o_ref):
    o_ref[...] = x_ref[...] + gt_ref[...] * _bdot(a_ref[...], w_ref[...])


def _outproj_residual(a, w, x2d, gt, seq):
    t, d = x2d.shape
    tm = 512
    tiles_per_batch = seq // tm
    return pl.pallas_call(
        _outproj_kernel,
        grid=(t // tm,),
        in_specs=[
            pl.BlockSpec((tm, a.shape[1]), lambda i: (i, 0)),
            pl.BlockSpec(w.shape, lambda i: (0, 0)),
            pl.BlockSpec((tm, d), lambda i: (i, 0)),
            pl.BlockSpec((None, 1, d), lambda i: (i // tiles_per_batch, 0, 0)),
        ],
        out_specs=pl.BlockSpec((tm, d), lambda i: (i, 0)),
        out_shape=jax.ShapeDtypeStruct((t, d), jnp.float32),
        compiler_params=_params(("arbitrary",)),
        name="outproj_residual",
    )(a, w, x2d, gt)


def _swa_kernel(sinks_ref, q_ref, kc_ref, kp_ref, vc_ref, vp_ref, gq_ref, gk_ref, o_ref):
    W = SW_WINDOW
    first = pl.program_id(1) == 0
    lane = lax.broadcasted_iota(jnp.int32, (W, LANES), 1)
    low = lane < SW_DH
    rows4 = lax.broadcasted_iota(jnp.int32, (SW_GROUP * W, 2 * W), 0)
    keys4 = lax.broadcasted_iota(jnp.int32, (SW_GROUP * W, 2 * W), 1)
    rel = (rows4 & (W - 1)) + W - keys4
    key_min = jnp.where(first, W, 0)
    mask = (rel >= 0) & (rel < W) & (keys4 >= key_min)
    row_col = lax.broadcasted_iota(jnp.int32, (SW_GROUP * W, 1), 0)
    gq = gq_ref[...]
    gk = gk_ref[...]
    for g in range(SW_KV_HEADS):
        sl = slice(g * LANES, (g + 1) * LANES)
        k2 = jnp.concatenate([kp_ref[:, sl], kc_ref[:, sl]], axis=0).astype(jnp.float32)
        kn = k2 * lax.rsqrt(jnp.mean(k2 * k2, axis=-1, keepdims=True) + EPS)
        kn = (kn * gk).astype(jnp.bfloat16)
        v2 = jnp.concatenate([vp_ref[:, sl], vc_ref[:, sl]], axis=0)
        parts = []
        for p in range(2):
            c0 = g * SW_GROUP * SW_DH + p * LANES
            qp = q_ref[:, c0:c0 + LANES].astype(jnp.float32)
            sq = qp * qp
            ss_lo = jnp.sum(jnp.where(low, sq, 0.0), axis=-1, keepdims=True)
            ss_hi = jnp.sum(jnp.where(low, 0.0, sq), axis=-1, keepdims=True)
            rs = jnp.where(low, lax.rsqrt(ss_lo / SW_DH + EPS), lax.rsqrt(ss_hi / SW_DH + EPS))
            qn = qp * rs * gq * (SW_DH ** -0.5)
            parts.append(jnp.where(low, qn, 0.0).astype(jnp.bfloat16))
            parts.append(jnp.where(low, 0.0, qn).astype(jnp.bfloat16))
        q4 = jnp.concatenate(parts, axis=0)
        scores = jnp.where(mask, _bdot_nt(q4, kn), -jnp.inf)
        sink = jnp.full((SW_GROUP * W, 1), sinks_ref[g * SW_GROUP + SW_GROUP - 1], jnp.float32)
        for j in range(SW_GROUP - 2, -1, -1):
            sink = jnp.where(row_col < (j + 1) * W, sinks_ref[g * SW_GROUP + j], sink)
        m = jnp.maximum(jnp.max(scores, axis=-1, keepdims=True), sink)
        pexp = jnp.exp(scores - m)
        denom = jnp.sum(pexp, axis=-1, keepdims=True) + jnp.exp(sink - m)
        prob = (pexp / denom).astype(jnp.bfloat16)
        o4 = _bdot(prob, v2)
        for p in range(2):
            oa = o4[(2 * p) * W:(2 * p + 1) * W]
            ob = o4[(2 * p + 1) * W:(2 * p + 2) * W]
            c0 = g * SW_GROUP * SW_DH + p * LANES
            o_ref[:, c0:c0 + LANES] = jnp.where(low, oa, ob).astype(o_ref.dtype)


def _swa_core(proj, sinks, g_q, g_k, bsz, seq):
    t = proj.shape[0]
    W = SW_WINDOW
    nb = seq // W
    dq = SW_Q_HEADS * SW_DH
    kv_w = SW_KV_HEADS * LANES
    k_blk = dq // kv_w
    v_blk = k_blk + 1
    cur = lambda b, n, s: b * nb + n
    prev = lambda b, n, s: b * nb + jnp.maximum(n - 1, 0)
    gq2 = jnp.concatenate([g_q, g_q]).reshape(1, LANES)
    gk2 = jnp.concatenate([g_k, g_k]).reshape(1, LANES)
    grid_spec = pltpu.PrefetchScalarGridSpec(
        num_scalar_prefetch=1,
        grid=(bsz, nb),
        in_specs=[
            pl.BlockSpec((W, dq), lambda b, n, s: (cur(b, n, s), 0)),
            pl.BlockSpec((W, kv_w), lambda b, n, s: (cur(b, n, s), k_blk)),
            pl.BlockSpec((W, kv_w), lambda b, n, s: (prev(b, n, s), k_blk)),
            pl.BlockSpec((W, kv_w), lambda b, n, s: (cur(b, n, s), v_blk)),
            pl.BlockSpec((W, kv_w), lambda b, n, s: (prev(b, n, s), v_blk)),
            pl.BlockSpec((1, LANES), lambda b, n, s: (0, 0)),
            pl.BlockSpec((1, LANES), lambda b, n, s: (0, 0)),
        ],
        out_specs=pl.BlockSpec((W, dq), lambda b, n, s: (cur(b, n, s), 0)),
    )
    return pl.pallas_call(
        _swa_kernel,
        grid_spec=grid_spec,
        out_shape=jax.ShapeDtypeStruct((t, dq), jnp.bfloat16),
        compiler_params=_params(("arbitrary", "arbitrary")),
        name="swa_core",
    )(sinks.astype(jnp.float32), proj, proj, proj, proj, proj, gq2, gk2)


def _pack_bf16_pairs(a):
    k = a.shape[1] // 2
    hi = lax.bitcast_convert_type(a[:, :k].astype(jnp.bfloat16).astype(jnp.float32), jnp.uint32)
    lo = lax.bitcast_convert_type(a[:, k:].astype(jnp.bfloat16).astype(jnp.float32), jnp.uint32)
    return hi | (lo >> 16)


def _unpack_bf16_pairs(u):
    hi = lax.bitcast_convert_type(u & jnp.uint32(0xFFFF0000), jnp.float32)
    lo = lax.bitcast_convert_type(u << 16, jnp.float32)
    return hi, lo


def _router_kernel(x_ref, g_ref, sc_ref, sh_ref, wh_ref, wl_ref, b_ref,
                   hn_ref, info_ref, infot_ref, cnt_ref, carry_ref, *, tm):
    @pl.when(pl.program_id(0) == 0)
    def _():
        carry_ref[...] = jnp.zeros_like(carry_ref)

    hn = _modulated_norm(x_ref[...], g_ref[...], sc_ref[...], sh_ref[...])
    hn_ref[...] = _pack_bf16_pairs(hn)
    h_hi, h_lo = _split_hi_lo(hn)
    logits = (_bdot(h_hi, wh_ref[...]) + (_bdot(h_lo, wh_ref[...]) + _bdot(h_hi, wl_ref[...]))
              + b_ref[...])
    lane = lax.broadcasted_iota(jnp.int32, (tm, ROUTER_LANES), 1)
    big = jnp.int32(4 * ROUTER_LANES)
    neg = -jnp.inf

    is_grp = (lane >= N_EXPERTS) & (lane < N_EXPERTS + N_GROUPS)
    gl = jnp.where(is_grp, logits, neg)
    gmax = jnp.max(gl, axis=-1, keepdims=True)
    gsel = jnp.min(jnp.where(gl == gmax, lane, big), axis=-1, keepdims=True) - N_EXPERTS
    p_grp = 1.0 / jnp.sum(jnp.exp(gl - gmax), axis=-1, keepdims=True)

    lo_lane = gsel * EXPERTS_PER_GROUP
    in_grp = (lane >= lo_lane) & (lane < lo_lane + EXPERTS_PER_GROUP)
    el = jnp.where(in_grp, logits, neg)
    v1 = jnp.max(el, axis=-1, keepdims=True)
    i1 = jnp.min(jnp.where(el == v1, lane, big), axis=-1, keepdims=True)
    el2 = jnp.where(lane == i1, neg, el)
    v2 = jnp.max(el2, axis=-1, keepdims=True)
    i2 = jnp.min(jnp.where(el2 == v2, lane, big), axis=-1, keepdims=True)
    e21 = jnp.exp(v2 - v1)
    gate1 = p_grp / (1.0 + e21)
    gate2 = p_grp * e21 / (1.0 + e21)

    hit1 = lane == i1
    hit2 = lane == i2
    onehot = jnp.where(hit1 | hit2, 1.0, 0.0)
    r_idx = lax.broadcasted_iota(jnp.int32, (tm, tm), 0)
    c_idx = lax.broadcasted_iota(jnp.int32, (tm, tm), 1)
    strict_lower = jnp.where(c_idx < r_idx, 1.0, 0.0).astype(jnp.bfloat16)
    before = _bdot(strict_lower, onehot.astype(jnp.bfloat16)) + carry_ref[0:1, :]
    rank1 = jnp.sum(jnp.where(hit1, before, 0.0), axis=-1, keepdims=True)
    rank2 = jnp.sum(jnp.where(hit2, before, 0.0), axis=-1, keepdims=True)
    total = carry_ref[0:1, :] + jnp.sum(onehot, axis=0, keepdims=True)
    carry_ref[...] = jnp.broadcast_to(total, carry_ref.shape)
    cnt_ref[...] = jnp.broadcast_to(total, cnt_ref.shape)

    info = jnp.where(lane == 0, i1.astype(jnp.float32), 0.0)
    info = jnp.where(lane == 1, i2.astype(jnp.float32), info)
    info = jnp.where(lane == 2, rank1, info)
    info = jnp.where(lane == 3, rank2, info)
    info = jnp.where(lane == 4, gate1, info)
    info = jnp.where(lane == 5, gate2, info)
    info_ref[...] = info
    infot_ref[...] = info.T[0:8, :]


def _router(x2d, g, sc, sh, w_hi, w_lo, bias, seq):
    t, d = x2d.shape
    tm = 512
    tiles_per_batch = seq // tm
    per_batch = lambda i: (i // tiles_per_batch, 0, 0)
    const = lambda i: (0, 0)
    return pl.pallas_call(
        functools.partial(_router_kernel, tm=tm),
        grid=(t // tm,),
        in_specs=[
            pl.BlockSpec((tm, d), lambda i: (i, 0)),
            pl.BlockSpec((1, d), const),
            pl.BlockSpec((None, 1, d), per_batch),
            pl.BlockSpec((None, 1, d), per_batch),
            pl.BlockSpec((d, ROUTER_LANES), const),
            pl.BlockSpec((d, ROUTER_LANES), const),
            pl.BlockSpec((1, ROUTER_LANES), const),
        ],
        out_specs=[
            pl.BlockSpec((tm, d // 2), lambda i: (i, 0)),
            pl.BlockSpec((tm, ROUTER_LANES), lambda i: (i, 0)),
            pl.BlockSpec((8, tm), lambda i: (0, i)),
            pl.BlockSpec((8, ROUTER_LANES), const),
        ],
        out_shape=[
            jax.ShapeDtypeStruct((t, d // 2), jnp.uint32),
            jax.ShapeDtypeStruct((t, ROUTER_LANES), jnp.float32),
            jax.ShapeDtypeStruct((8, t), jnp.float32),
            jax.ShapeDtypeStruct((8, ROUTER_LANES), jnp.float32),
        ],
        scratch_shapes=[pltpu.VMEM((8, ROUTER_LANES), jnp.float32)],
        compiler_params=_params(("arbitrary",)),
        name="moe_router",
    )(x2d, g.reshape(1, d), sc, sh, w_hi, w_lo, bias)


def _sc_mesh():
    return plsc.VectorSubcoreMesh(core_axis_name="c", subcore_axis_name="s")


def _sc_dispatch(rows, d0, d1, pad_idx, n_slots):
    t, w = rows.shape
    n_pad = pad_idx.shape[1]
    zeros = jnp.zeros((SC_WINDOW, w), rows.dtype)
    sem = (pltpu.PARALLEL, pltpu.ARBITRARY)

    @pl.kernel(out_type=jax.ShapeDtypeStruct((n_slots + SC_WINDOW, w), rows.dtype), mesh=_sc_mesh())
    def dispatch(x_hbm, d0_hbm, d1_hbm, z_hbm, p_hbm, o_hbm):
        def scatter_rows(x_vmem, i0_vmem, i1_vmem):
            cols = pl.ds(pl.program_id(1) * SC_COLS, SC_COLS)
            pltpu.sync_copy(x_vmem, o_hbm.at[i0_vmem.at[0], cols])
            pltpu.sync_copy(x_vmem, o_hbm.at[i1_vmem.at[0], cols])

        pltpu.emit_pipeline(
            scatter_rows,
            grid=(t // SC_WINDOW, w // SC_COLS),
            in_specs=[pl.BlockSpec((SC_WINDOW, SC_COLS), lambda i, j: (i, j)),
                      pl.BlockSpec((1, SC_WINDOW), lambda i, j: (0, i)),
                      pl.BlockSpec((1, SC_WINDOW), lambda i, j: (0, i))],
            out_specs=[],
            core_axis_name=("c", "s"),
            dimension_semantics=sem,
        )(x_hbm, d0_hbm, d1_hbm)

        def scatter_zeros(z_vmem, p_vmem):
            cols = pl.ds(pl.program_id(1) * SC_COLS, SC_COLS)
            pltpu.sync_copy(z_vmem, o_hbm.at[p_vmem.at[0], cols])

        pltpu.emit_pipeline(
            scatter_zeros,
            grid=(n_pad // SC_WINDOW, w // SC_COLS),
            in_specs=[pl.BlockSpec((SC_WINDOW, SC_COLS), lambda i, j: (0, j)),
                      pl.BlockSpec((1, SC_WINDOW), lambda i, j: (0, i))],
            out_specs=[],
            core_axis_name=("c", "s"),
            dimension_semantics=sem,
        )(z_hbm, p_hbm)

    return dispatch(rows, d0, d1, zeros, pad_idx)


def _sc_gather(src, idx):
    n_out = idx.shape[1]
    w = src.shape[1]

    @pl.kernel(out_type=jax.ShapeDtypeStruct((n_out, w), src.dtype), mesh=_sc_mesh())
    def gather(x_hbm, i_hbm, o_hbm):
        def gather_rows(i_vmem, o_vmem):
            cols = pl.ds(pl.program_id(1) * SC_COLS, SC_COLS)
            pltpu.sync_copy(x_hbm.at[i_vmem.at[0], cols], o_vmem)

        pltpu.emit_pipeline(
            gather_rows,
            grid=(n_out // SC_WINDOW, w // SC_COLS),
            in_specs=[pl.BlockSpec((1, SC_WINDOW), lambda i, j: (0, i))],
            out_specs=[pl.BlockSpec((SC_WINDOW, SC_COLS), lambda i, j: (i, j))],
            core_axis_name=("c", "s"),
            dimension_semantics=(pltpu.PARALLEL, pltpu.ARBITRARY),
        )(i_hbm, o_hbm)

    return gather(src, idx)


def _expert_kernel(blk_ref, x_ref, w1_ref, w3_ref, w2_ref, y_ref):
    used = pl.program_id(0) < blk_ref[blk_ref.shape[0] - 1]

    @pl.when(used)
    def _():
        x_hi, x_lo = _unpack_bf16_pairs(x_ref[...])
        xb = jnp.concatenate([x_hi, x_lo], axis=-1).astype(jnp.bfloat16)
        h1 = _bdot(xb, w1_ref[...].astype(jnp.bfloat16))
        h3 = _bdot(xb, w3_ref[...].astype(jnp.bfloat16))
        act = (h1 * jax.nn.sigmoid(h1) * h3).astype(jnp.bfloat16)
        y_ref[...] = _pack_bf16_pairs(_bdot(act, w2_ref[...].astype(jnp.bfloat16)))

    @pl.when(jnp.logical_not(used))
    def _():
        y_ref[...] = jnp.zeros_like(y_ref)


def _experts(x_slots, blk_meta, w1, w3, w2, layer, nb):
    dp = x_slots.shape[1]
    d, de = w1.shape[-2:]
    n_used = lambda s: s[nb]
    rowblk = lambda i, s: (jnp.minimum(i, n_used(s) - 1), 0)
    grid_spec = pltpu.PrefetchScalarGridSpec(
        num_scalar_prefetch=1,
        grid=(nb,),
        in_specs=[
            pl.BlockSpec((MOE_BLOCK, dp), rowblk),
            pl.BlockSpec((None, None, d, de), lambda i, s: (layer, s[i], 0, 0)),
            pl.BlockSpec((None, None, d, de), lambda i, s: (layer, s[i], 0, 0)),
            pl.BlockSpec((None, None, de, d), lambda i, s: (layer, s[i], 0, 0)),
        ],
        out_specs=pl.BlockSpec((MOE_BLOCK, dp), lambda i, s: (i, 0)),
    )
    return pl.pallas_call(
        _expert_kernel,
        grid_spec=grid_spec,
        out_shape=jax.ShapeDtypeStruct((nb * MOE_BLOCK, dp), jnp.uint32),
        compiler_params=_params(("arbitrary",)),
        name="moe_experts",
    )(blk_meta, x_slots, w1, w3, w2)


def _combine_kernel(x_ref, y1_ref, y2_ref, info_ref, gt_ref, o_ref):
    k = x_ref.shape[1] // 2
    info = info_ref[...]
    g1 = info[:, 4:5]
    g2 = info[:, 5:6]
    y1_hi, y1_lo = _unpack_bf16_pairs(y1_ref[...])
    y2_hi, y2_lo = _unpack_bf16_pairs(y2_ref[...])
    o_ref[:, :k] = x_ref[:, :k] + gt_ref[:, :k] * (g1 * y1_hi + g2 * y2_hi)
    o_ref[:, k:] = x_ref[:, k:] + gt_ref[:, k:] * (g1 * y1_lo + g2 * y2_lo)


def _combine(x2d, y_pairs, info, gt, seq):
    t, d = x2d.shape
    tm = 512
    tiles_per_batch = seq // tm
    second = t // tm
    return pl.pallas_call(
        _combine_kernel,
        grid=(t // tm,),
        in_specs=[
            pl.BlockSpec((tm, d), lambda i: (i, 0)),
            pl.BlockSpec((tm, d // 2), lambda i: (i, 0)),
            pl.BlockSpec((tm, d // 2), lambda i: (i + second, 0)),
            pl.BlockSpec((tm, ROUTER_LANES), lambda i: (i, 0)),
            pl.BlockSpec((None, 1, d), lambda i: (i // tiles_per_batch, 0, 0)),
        ],
        out_specs=pl.BlockSpec((tm, d), lambda i: (i, 0)),
        out_shape=jax.ShapeDtypeStruct((t, d), jnp.float32),
        compiler_params=_params(("arbitrary",)),
        name="moe_combine",
    )(x2d, y_pairs, y_pairs, info, gt)


def _moe_layer(x2d, g, sc, sh, gt, w_group, b_group, w_router, b_router, w1, w3, w2, layer, seq):
    t, d = x2d.shape
    w_cat = jnp.zeros((d, ROUTER_LANES), jnp.float32)
    w_cat = w_cat.at[:, :N_EXPERTS].set(w_router).at[:, N_EXPERTS:N_EXPERTS + N_GROUPS].set(w_group)
    b_cat = jnp.zeros((1, ROUTER_LANES), jnp.float32)
    b_cat = b_cat.at[0, :N_EXPERTS].set(b_router).at[0, N_EXPERTS:N_EXPERTS + N_GROUPS].set(b_group)
    w_hi, w_lo = _split_hi_lo(w_cat)
    hn, info, info_t, cnt = _router(x2d, g, sc, sh, w_hi, w_lo, b_cat, seq)

    counts = cnt[0, :N_EXPERTS].astype(jnp.int32)
    padded = (counts + MOE_BLOCK - 1) // MOE_BLOCK * MOE_BLOCK
    pad_ends = jnp.cumsum(padded)
    pad_starts = pad_ends - padded
    nb = -(-(2 * t) // MOE_BLOCK) + N_EXPERTS
    n_slots = nb * MOE_BLOCK
    it = info_t.astype(jnp.int32)
    onehot_start = lambda e: jnp.sum(
        jnp.where(e[None, :] == jnp.arange(N_EXPERTS, dtype=jnp.int32)[:, None],
                  pad_starts[:, None], 0), axis=0)
    dest1 = (onehot_start(it[0]) + it[2]).reshape(1, t)
    dest2 = (onehot_start(it[1]) + it[3]).reshape(1, t)
    pad_pos = (pad_starts + counts)[:, None] + jnp.arange(MOE_BLOCK, dtype=jnp.int32)[None, :]
    pad_idx = jnp.where(pad_pos < pad_ends[:, None], pad_pos, n_slots).reshape(1, -1)
    blk_first = jnp.arange(nb, dtype=jnp.int32) * MOE_BLOCK
    blk_exp = jnp.minimum(jnp.sum((pad_ends[None, :] <= blk_first[:, None]).astype(jnp.int32), axis=1),
                          N_EXPERTS - 1)
    n_used = pad_ends[-1] // MOE_BLOCK
    blk_meta = jnp.concatenate([blk_exp, n_used[None]]).astype(jnp.int32)

    x_slots = _sc_dispatch(hn, dest1, dest2, pad_idx, n_slots)
    y_slots = _experts(x_slots, blk_meta, w1, w3, w2, layer, nb)
    y_pairs = _sc_gather(y_slots, jnp.concatenate([dest1, dest2], axis=1))
    return _combine(x2d, y_pairs, info, gt, seq)


def kernel(x, c, w_ada, b_ada, norm1_g, norm2_g, ml_w_in, ml_b_gate, ml_g_out, ml_w_out,
           sw_w_in, sw_g_q, sw_g_k, sw_sinks, sw_w_out, moe_w_group, moe_b_group,
           moe_w_router, moe_b_router, moe_w1, moe_w3, moe_w2):
    bsz, seq, d = x.shape
    depth = w_ada.shape[0]
    bf = jnp.bfloat16
    mod = _ada_mod(c, w_ada, b_ada)
    x2d = x.reshape(bsz * seq, d)
    for layer in range(depth):
        parts = [mod[layer, :, i * d:(i + 1) * d].reshape(bsz, 1, d) for i in range(6)]
        sh1, sc1, gt1, sh2, sc2, gt2 = parts
        j = layer // 2
        if layer % 2 == 0:
            w = ml_w_in[j]
            q_w, k_w = w[:, :ML_QK], w[:, ML_QK:2 * ML_QK]
            v_w = w[:, 2 * ML_QK:2 * ML_QK + ML_V]
            o_w = w[:, 2 * ML_QK + ML_V:2 * ML_QK + 2 * ML_V]
            g_w = w[:, 2 * ML_QK + 2 * ML_V:]
            w_main = jnp.concatenate([v_w, o_w, q_w], axis=1).astype(bf)
            wg_t = jnp.zeros((ML_GATE_ROWS, d), jnp.float32).at[:2 * ML_HEADS].set(g_w.T)
            wg_hi, wg_lo = _split_hi_lo(wg_t)
            main, k_t, g_t = _inproj(
                x2d, norm1_g[layer], sc1, sh1, w_main, seq,
                ml_extra=(k_w.T.astype(bf), wg_hi, wg_lo),
                q_cols=(2 * ML_V, 2 * ML_V + ML_QK), q_scale=ML_DK ** -0.5)
            a = _mlstm_core(main, k_t, g_t, ml_b_gate[j], ml_g_out[j], bsz, seq)
            x2d = _outproj_residual(a, ml_w_out[j].astype(bf), x2d, gt1, seq)
        else:
            w = sw_w_in[j]
            dq = SW_Q_HEADS * SW_DH
            dkv = SW_KV_HEADS * SW_DH
            dup = lambda m: jnp.concatenate(
                [m.reshape(d, SW_KV_HEADS, 1, SW_DH)] * 2, axis=2).reshape(d, 2 * dkv)
            w_main = jnp.concatenate(
                [w[:, :dq], dup(w[:, dq:dq + dkv]), dup(w[:, dq + dkv:])], axis=1).astype(bf)
            (proj,) = _inproj(x2d, norm1_g[layer], sc1, sh1, w_main, seq)
            a = _swa_core(proj, sw_sinks[j], sw_g_q[j], sw_g_k[j], bsz, seq)
            x2d = _outproj_residual(a, sw_w_out[j].astype(bf), x2d, gt1, seq)
        x2d = _moe_layer(x2d, norm2_g[layer], sc2, sh2, gt2, moe_w_group[layer],
                         moe_b_group[layer], moe_w_router[layer], moe_b_router[layer],
                         moe_w1, moe_w3, moe_w2, layer, seq)
    return x2d.reshape(bsz, seq, d)
```

```python
import functools

import jax
import jax.numpy as jnp
from jax import lax
from jax.experimental import pallas as pl
from jax.experimental.pallas import tpu as pltpu
from jax.experimental.pallas import tpu_sc as plsc

EPS = 1e-6
GATE_CAP = 15.0

ML_HEADS = 4
ML_DK = 128
ML_DV = 256
ML_QK = ML_HEADS * ML_DK
ML_V = ML_HEADS * ML_DV
ML_GATE_ROWS = 16

SW_Q_HEADS = 16
SW_KV_HEADS = 4
SW_GROUP = SW_Q_HEADS // SW_KV_HEADS
SW_DH = 64
SW_WINDOW = 128
LANES = 128

N_GROUPS = 8
EXPERTS_PER_GROUP = 8
N_EXPERTS = N_GROUPS * EXPERTS_PER_GROUP
MOE_BLOCK = 256
ROUTER_LANES = 128
SC_WINDOW = 128
SC_COLS = 256

VMEM_LIMIT = 56 * 1024 * 1024

_NT = (((1,), (1,)), ((), ()))


def _bdot(a, b):
    return jnp.dot(a, b, preferred_element_type=jnp.float32)


def _bdot_nt(a, b):
    return lax.dot_general(a, b, _NT, preferred_element_type=jnp.float32)


def _split_hi_lo(a):
    hi = a.astype(jnp.bfloat16)
    lo = (a - hi.astype(jnp.float32)).astype(jnp.bfloat16)
    return hi, lo


def _params(sem):
    return pltpu.CompilerParams(dimension_semantics=sem, vmem_limit_bytes=VMEM_LIMIT)


def _ada_kernel(c_ref, w_ref, b_ref, o_ref):
    c = c_ref[...]
    cond = c * jax.nn.sigmoid(c)
    c_hi, c_lo = _split_hi_lo(cond)
    w_hi, w_lo = _split_hi_lo(w_ref[...])
    acc = _bdot(c_hi, w_hi) + (_bdot(c_lo, w_hi) + _bdot(c_hi, w_lo))
    o_ref[...] = acc + b_ref[...]


def _ada_mod(c, w_ada, b_ada):
    depth, d, n = w_ada.shape
    bsz = c.shape[0]
    rows = 8
    tn = 768
    c_pad = jnp.zeros((rows, d), jnp.float32).at[:bsz].set(c)
    out = pl.pallas_call(
        _ada_kernel,
        grid=(depth, n // tn),
        in_specs=[
            pl.BlockSpec((rows, d), lambda l, j: (0, 0)),
            pl.BlockSpec((None, d, tn), lambda l, j: (l, 0, j)),
            pl.BlockSpec((None, 1, tn), lambda l, j: (l, 0, j)),
        ],
        out_specs=pl.BlockSpec((None, rows, tn), lambda l, j: (l, 0, j)),
        out_shape=jax.ShapeDtypeStruct((depth, rows, n), jnp.float32),
        compiler_params=_params(("arbitrary", "arbitrary")),
        name="ada_mod",
    )(c_pad, w_ada, b_ada.reshape(depth, 1, n))
    return out[:, :bsz]


def _modulated_norm(x, g, sc, sh):
    y = x * lax.rsqrt(jnp.mean(x * x, axis=-1, keepdims=True) + EPS)
    return y * g * (1.0 + sc) + sh


def _inproj_kernel(*refs, n_main, chunk, q_cols, q_scale, with_ml):
    if with_ml:
        (x_ref, g_ref, sc_ref, sh_ref, w_ref, wk_ref, wgh_ref, wgl_ref,
         o_ref, kt_ref, gt_ref) = refs
    else:
        x_ref, g_ref, sc_ref, sh_ref, w_ref, o_ref = refs
    hn = _modulated_norm(x_ref[...], g_ref[...], sc_ref[...], sh_ref[...])
    hb = hn.astype(jnp.bfloat16)
    for c0 in range(0, n_main, chunk):
        acc = _bdot(hb, w_ref[:, c0:c0 + chunk])
        if q_cols is not None and c0 >= q_cols[0]:
            acc = acc * q_scale
        o_ref[:, c0:c0 + chunk] = acc.astype(o_ref.dtype)
    if with_ml:
        kt_ref[...] = _bdot_nt(wk_ref[...], hb).astype(kt_ref.dtype)
        h_lo = (hn - hb.astype(jnp.float32)).astype(jnp.bfloat16)
        gt_ref[...] = (_bdot_nt(wgh_ref[...], hb)
                       + (_bdot_nt(wgh_ref[...], h_lo) + _bdot_nt(wgl_ref[...], hb)))


def _inproj(x2d, g, sc, sh, w_main, seq, *, ml_extra=None, q_cols=None, q_scale=1.0):
    t, d = x2d.shape
    tm = 512
    n_main = w_main.shape[1]
    tiles_per_batch = seq // tm
    row = lambda i: (i, 0)
    per_batch = lambda i: (i // tiles_per_batch, 0, 0)
    const = lambda i: (0, 0)
    in_specs = [
        pl.BlockSpec((tm, d), row),
        pl.BlockSpec((1, d), const),
        pl.BlockSpec((None, 1, d), per_batch),
        pl.BlockSpec((None, 1, d), per_batch),
        pl.BlockSpec((d, n_main), const),
    ]
    args = [x2d, g.reshape(1, d), sc, sh, w_main]
    out_specs = [pl.BlockSpec((tm, n_main), row)]
    out_shape = [jax.ShapeDtypeStruct((t, n_main), jnp.bfloat16)]
    if ml_extra is not None:
        wk_t, wg_hi, wg_lo = ml_extra
        in_specs += [pl.BlockSpec(wk_t.shape, const),
                     pl.BlockSpec(wg_hi.shape, const),
                     pl.BlockSpec(wg_lo.shape, const)]
        args += [wk_t, wg_hi, wg_lo]
        out_specs += [pl.BlockSpec((wk_t.shape[0], tm), lambda i: (0, i)),
                      pl.BlockSpec((ML_GATE_ROWS, tm), lambda i: (0, i))]
        out_shape += [jax.ShapeDtypeStruct((wk_t.shape[0], t), jnp.bfloat16),
                      jax.ShapeDtypeStruct((ML_GATE_ROWS, t), jnp.float32)]
    kern = functools.partial(_inproj_kernel, n_main=n_main, chunk=512, q_cols=q_cols,
                             q_scale=q_scale, with_ml=ml_extra is not None)
    return pl.pallas_call(
        kern,
        grid=(t // tm,),
        in_specs=in_specs,
        out_specs=out_specs,
        out_shape=out_shape,
        compiler_params=_params(("arbitrary",)),
        name="inproj_ml" if ml_extra is not None else "inproj_sw",
    )(*args)


def _mlstm_kernel(v_ref, o_ref, q_ref, kt_ref, gt_ref, bg_ref, gout_ref, out_ref,
                  c_ref, n_ref, m_ref, *, chunk):
    L = chunk
    H, dk, dv = ML_HEADS, ML_DK, ML_DV

    @pl.when(pl.program_id(1) == 0)
    def _():
        c_ref[...] = jnp.zeros_like(c_ref)
        n_ref[...] = jnp.zeros_like(n_ref)
        m_ref[...] = jnp.zeros_like(m_ref)

    z = gt_ref[...] + bg_ref[...]
    gates = GATE_CAP * jnp.tanh(z / GATE_CAP)
    log_f = jnp.minimum(gates, 0.0) - jnp.log1p(jnp.exp(-jnp.abs(gates)))
    grow = lax.broadcasted_iota(jnp.int32, (ML_GATE_ROWS, L), 0)
    is_i = grow < H
    slab = jnp.where(is_i, gates, log_f)

    r_idx = lax.broadcasted_iota(jnp.int32, (L, L), 0)
    c_idx = lax.broadcasted_iota(jnp.int32, (L, L), 1)
    upper = jnp.where(r_idx <= c_idx, 1.0, 0.0).astype(jnp.bfloat16)
    a1 = slab.astype(jnp.bfloat16)
    r1 = slab - a1.astype(jnp.float32)
    a2 = r1.astype(jnp.bfloat16)
    a3 = (r1 - a2.astype(jnp.float32)).astype(jnp.bfloat16)
    cum = _bdot(a1, upper) + (_bdot(a2, upper) + _bdot(a3, upper))
    rows = jnp.where(is_i, gates, cum)
    cols = jnp.concatenate(
        [rows, jnp.zeros((LANES - ML_GATE_ROWS, L), jnp.float32)], axis=0).T

    causal = r_idx >= c_idx
    for h in range(H):
        i_r = rows[h:h + 1, :]
        b_r = rows[H + h:H + h + 1, :]
        b_c = cols[:, H + h:H + h + 1]
        m_prev = m_ref[h][0:1, 0:1]
        n_row = n_ref[h][0:1, :]
        c_st = c_ref[h]
        q = q_ref[:, h * dk:(h + 1) * dk]
        kt = kt_ref[h * dk:(h + 1) * dk, :]
        v = v_ref[:, h * dv:(h + 1) * dv]

        u_r = i_r - b_r
        d_log = jnp.where(causal, b_c + u_r, -jnp.inf)
        inter = b_c + m_prev
        m_t = jnp.maximum(inter, jnp.max(d_log, axis=-1, keepdims=True))
        w_intra = jnp.exp(d_log - m_t)
        w_inter = jnp.exp(inter - m_t)
        s = _bdot(q, kt) * w_intra
        num = w_inter * _bdot(q, c_st.astype(jnp.bfloat16)) + _bdot(s.astype(jnp.bfloat16), v)
        qn = jnp.sum(q.astype(jnp.float32) * n_row, axis=-1, keepdims=True)
        den = w_inter * qn + jnp.sum(s, axis=-1, keepdims=True)
        hb = num * (1.0 / jnp.maximum(jnp.abs(den), jnp.exp(-m_t)))

        g_last = b_r[:, L - 1:L]
        a_r = g_last - b_r + i_r
        m_new = jnp.maximum(g_last + m_prev, jnp.max(a_r, axis=-1, keepdims=True))
        w_state = jnp.exp(a_r - m_new)
        decay = jnp.exp(g_last + m_prev - m_new)
        kw = (kt.astype(jnp.float32) * w_state).astype(jnp.bfloat16)
        c_ref[h] = decay * c_st + _bdot(kw, v)
        w16 = jnp.broadcast_to(w_state, (16, L)).astype(jnp.bfloat16)
        dn = _bdot_nt(w16, kt)
        n_ref[h] = jnp.broadcast_to(decay * n_row + dn[0:1, :], (8, dk))
        m_ref[h] = jnp.broadcast_to(m_new, (8, LANES))

        y = hb * lax.rsqrt(jnp.mean(hb * hb, axis=-1, keepdims=True) + EPS)
        y = y * gout_ref[:, h * dv:(h + 1) * dv]
        og = jax.nn.sigmoid(o_ref[:, h * dv:(h + 1) * dv].astype(jnp.float32))
        out_ref[:, h * dv:(h + 1) * dv] = (y * og).astype(out_ref.dtype)


def _mlstm_core(main, k_t, g_t, b_gate, g_out, bsz, seq, chunk=256):
    t = main.shape[0]
    nc = seq // chunk
    blk = lambda b, c: b * nc + c
    bg = jnp.zeros((ML_GATE_ROWS, 1), jnp.float32).at[:2 * ML_HEADS, 0].set(b_gate)
    return pl.pallas_call(
        functools.partial(_mlstm_kernel, chunk=chunk),
        grid=(bsz, nc),
        in_specs=[
            pl.BlockSpec((chunk, ML_V), lambda b, c: (blk(b, c), 0)),
            pl.BlockSpec((chunk, ML_V), lambda b, c: (blk(b, c), 1)),
            pl.BlockSpec((chunk, ML_QK), lambda b, c: (blk(b, c), 4)),
            pl.BlockSpec((ML_QK, chunk), lambda b, c: (0, blk(b, c))),
            pl.BlockSpec((ML_GATE_ROWS, chunk), lambda b, c: (0, blk(b, c))),
            pl.BlockSpec((ML_GATE_ROWS, 1), lambda b, c: (0, 0)),
            pl.BlockSpec((1, ML_V), lambda b, c: (0, 0)),
        ],
        out_specs=pl.BlockSpec((chunk, ML_V), lambda b, c: (blk(b, c), 0)),
        out_shape=jax.ShapeDtypeStruct((t, ML_V), jnp.bfloat16),
        scratch_shapes=[
            pltpu.VMEM((ML_HEADS, ML_DK, ML_DV), jnp.float32),
            pltpu.VMEM((ML_HEADS, 8, ML_DK), jnp.float32),
            pltpu.VMEM((ML_HEADS, 8, LANES), jnp.float32),
        ],
        compiler_params=_params(("arbitrary", "arbitrary")),
        name="mlstm_core",
    )(main, main, main, k_t, g_t, bg, g_out.reshape(1, ML_V))


def _outproj_kernel(a_ref, w_ref, x_ref, gt_ref, o_ref):
    o_ref[...] = x_ref[...] + gt_ref[...] * _bdot(a_ref[...], w_ref[...])


def _outproj_residual(a, w, x2d, gt, seq):
    t, d = x2d.shape
    tm = 512
    tiles_per_batch = seq // tm
    return pl.pallas_call(
        _outproj_kernel,
        grid=(t // tm,),
        in_specs=[
            pl.BlockSpec((tm, a.shape[1]), lambda i: (i, 0)),
            pl.BlockSpec(w.shape, lambda i: (0, 0)),
            pl.BlockSpec((tm, d), lambda i: (i, 0)),
            pl.BlockSpec((None, 1, d), lambda i: (i // tiles_per_batch, 0, 0)),
        ],
        out_specs=pl.BlockSpec((tm, d), lambda i: (i, 0)),
        out_shape=jax.ShapeDtypeStruct((t, d), jnp.float32),
        compiler_params=_params(("arbitrary",)),
        name="outproj_residual",
    )(a, w, x2d, gt)


def _swa_kernel(sinks_ref, q_ref, kc_ref, kp_ref, vc_ref, vp_ref, gq_ref, gk_ref, bias_ref, o_ref):
    W = SW_WINDOW
    lane = lax.broadcasted_iota(jnp.int32, (W, LANES), 1)
    low = lane < SW_DH
    row_col = lax.broadcasted_iota(jnp.int32, (SW_GROUP * W, 1), 0)
    gq = gq_ref[...]
    gk = gk_ref[...]
    for g in range(SW_KV_HEADS):
        sl = slice(g * LANES, (g + 1) * LANES)
        k2 = jnp.concatenate([kp_ref[:, sl], kc_ref[:, sl]], axis=0).astype(jnp.float32)
        kn = k2 * lax.rsqrt(jnp.mean(k2 * k2, axis=-1, keepdims=True) + EPS)
        kn = (kn * gk).astype(jnp.bfloat16)
        v2 = jnp.concatenate([vp_ref[:, sl], vc_ref[:, sl]], axis=0)
        parts = []
        for p in range(2):
            c0 = g * SW_GROUP * SW_DH + p * LANES
            qp = q_ref[:, c0:c0 + LANES].astype(jnp.float32)
            sq = qp * qp
            ss_lo = jnp.sum(jnp.where(low, sq, 0.0), axis=-1, keepdims=True)
            ss_hi = jnp.sum(jnp.where(low, 0.0, sq), axis=-1, keepdims=True)
            rs = jnp.where(low, lax.rsqrt(ss_lo / SW_DH + EPS), lax.rsqrt(ss_hi / SW_DH + EPS))
            qn = qp * rs * gq * (SW_DH ** -0.5)
            parts.append(jnp.where(low, qn, 0.0).astype(jnp.bfloat16))
            parts.append(jnp.where(low, 0.0, qn).astype(jnp.bfloat16))
        q4 = jnp.concatenate(parts, axis=0)
        scores = _bdot_nt(q4, kn) + bias_ref[...]
        sink = jnp.full((SW_GROUP * W, 1), sinks_ref[g * SW_GROUP + SW_GROUP - 1], jnp.float32)
        for j in range(SW_GROUP - 2, -1, -1):
            sink = jnp.where(row_col < (j + 1) * W, sinks_ref[g * SW_GROUP + j], sink)
        m = jnp.maximum(jnp.max(scores, axis=-1, keepdims=True), sink)
        pexp = jnp.exp(scores - m)
        denom = jnp.sum(pexp, axis=-1, keepdims=True) + jnp.exp(sink - m)
        o4 = _bdot(pexp.astype(jnp.bfloat16), v2) * (1.0 / denom)
        for p in range(2):
            oa = o4[(2 * p) * W:(2 * p + 1) * W]
            ob = o4[(2 * p + 1) * W:(2 * p + 2) * W]
            c0 = g * SW_GROUP * SW_DH + p * LANES
            o_ref[:, c0:c0 + LANES] = jnp.where(low, oa, ob).astype(o_ref.dtype)


def _swa_core(proj, sinks, g_q, g_k, bsz, seq):
    t = proj.shape[0]
    W = SW_WINDOW
    nb = seq // W
    dq = SW_Q_HEADS * SW_DH
    kv_w = SW_KV_HEADS * LANES
    k_blk = dq // kv_w
    v_blk = k_blk + 1
    cur = lambda b, n, s: b * nb + n
    prev = lambda b, n, s: b * nb + jnp.maximum(n - 1, 0)
    gq2 = jnp.concatenate([g_q, g_q]).reshape(1, LANES)
    gk2 = jnp.concatenate([g_k, g_k]).reshape(1, LANES)
    qi = (jnp.arange(SW_GROUP * W) % W)[:, None]
    ki = jnp.arange(2 * W)[None, :]
    rel = qi + W - ki
    in_win = (rel >= 0) & (rel < W)
    bias = jnp.stack([jnp.where(in_win & (ki >= W), 0.0, -jnp.inf),
                      jnp.where(in_win, 0.0, -jnp.inf)]).astype(jnp.float32)
    grid_spec = pltpu.PrefetchScalarGridSpec(
        num_scalar_prefetch=1,
        grid=(bsz, nb),
        in_specs=[
            pl.BlockSpec((W, dq), lambda b, n, s: (cur(b, n, s), 0)),
            pl.BlockSpec((W, kv_w), lambda b, n, s: (cur(b, n, s), k_blk)),
            pl.BlockSpec((W, kv_w), lambda b, n, s: (prev(b, n, s), k_blk)),
            pl.BlockSpec((W, kv_w), lambda b, n, s: (cur(b, n, s), v_blk)),
            pl.BlockSpec((W, kv_w), lambda b, n, s: (prev(b, n, s), v_blk)),
            pl.BlockSpec((1, LANES), lambda b, n, s: (0, 0)),
            pl.BlockSpec((1, LANES), lambda b, n, s: (0, 0)),
            pl.BlockSpec((None, SW_GROUP * W, 2 * W), lambda b, n, s: (jnp.minimum(n, 1), 0, 0)),
        ],
        out_specs=pl.BlockSpec((W, dq), lambda b, n, s: (cur(b, n, s), 0)),
    )
    return pl.pallas_call(
        _swa_kernel,
        grid_spec=grid_spec,
        out_shape=jax.ShapeDtypeStruct((t, dq), jnp.bfloat16),
        compiler_params=_params(("arbitrary", "arbitrary")),
        name="swa_core",
    )(sinks.astype(jnp.float32), proj, proj, proj, proj, proj, gq2, gk2, bias)


def _pack_bf16_pairs(a):
    k = a.shape[1] // 2
    hi = lax.bitcast_convert_type(a[:, :k].astype(jnp.bfloat16).astype(jnp.float32), jnp.uint32)
    lo = lax.bitcast_convert_type(a[:, k:].astype(jnp.bfloat16).astype(jnp.float32), jnp.uint32)
    return hi | (lo >> 16)


def _unpack_bf16_pairs(u):
    hi = lax.bitcast_convert_type(u & jnp.uint32(0xFFFF0000), jnp.float32)
    lo = lax.bitcast_convert_type(u << 16, jnp.float32)
    return hi, lo


def _router_kernel(x_ref, g_ref, sc_ref, sh_ref, wh_ref, wl_ref, b_ref,
                   hn_ref, info_ref, infot_ref, cnt_ref, carry_ref, *, tm):
    @pl.when(pl.program_id(0) == 0)
    def _():
        carry_ref[...] = jnp.zeros_like(carry_ref)

    hn = _modulated_norm(x_ref[...], g_ref[...], sc_ref[...], sh_ref[...])
    hn_ref[...] = _pack_bf16_pairs(hn)
    h_hi, h_lo = _split_hi_lo(hn)
    logits = (_bdot(h_hi, wh_ref[...]) + (_bdot(h_lo, wh_ref[...]) + _bdot(h_hi, wl_ref[...]))
              + b_ref[...])
    lane = lax.broadcasted_iota(jnp.int32, (tm, ROUTER_LANES), 1)
    big = jnp.int32(4 * ROUTER_LANES)
    neg = -jnp.inf

    is_grp = (lane >= N_EXPERTS) & (lane < N_EXPERTS + N_GROUPS)
    gl = jnp.where(is_grp, logits, neg)
    gmax = jnp.max(gl, axis=-1, keepdims=True)
    gsel = jnp.min(jnp.where(gl == gmax, lane, big), axis=-1, keepdims=True) - N_EXPERTS
    p_grp = 1.0 / jnp.sum(jnp.exp(gl - gmax), axis=-1, keepdims=True)

    lo_lane = gsel * EXPERTS_PER_GROUP
    in_grp = (lane >= lo_lane) & (lane < lo_lane + EXPERTS_PER_GROUP)
    el = jnp.where(in_grp, logits, neg)
    v1 = jnp.max(el, axis=-1, keepdims=True)
    i1 = jnp.min(jnp.where(el == v1, lane, big), axis=-1, keepdims=True)
    el2 = jnp.where(lane == i1, neg, el)
    v2 = jnp.max(el2, axis=-1, keepdims=True)
    i2 = jnp.min(jnp.where(el2 == v2, lane, big), axis=-1, keepdims=True)
    e21 = jnp.exp(v2 - v1)
    gate1 = p_grp / (1.0 + e21)
    gate2 = p_grp * e21 / (1.0 + e21)

    hit1 = lane == i1
    hit2 = lane == i2
    onehot = jnp.where(hit1 | hit2, 1.0, 0.0)
    r_idx = lax.broadcasted_iota(jnp.int32, (tm, tm), 0)
    c_idx = lax.broadcasted_iota(jnp.int32, (tm, tm), 1)
    strict_lower = jnp.where(c_idx < r_idx, 1.0, 0.0).astype(jnp.bfloat16)
    before = _bdot(strict_lower, onehot.astype(jnp.bfloat16)) + carry_ref[0:1, :]
    rank1 = jnp.sum(jnp.where(hit1, before, 0.0), axis=-1, keepdims=True)
    rank2 = jnp.sum(jnp.where(hit2, before, 0.0), axis=-1, keepdims=True)
    total = carry_ref[0:1, :] + jnp.sum(onehot, axis=0, keepdims=True)
    carry_ref[...] = jnp.broadcast_to(total, carry_ref.shape)
    cnt_ref[...] = jnp.broadcast_to(total, cnt_ref.shape)

    info = jnp.where(lane == 0, i1.astype(jnp.float32), 0.0)
    info = jnp.where(lane == 1, i2.astype(jnp.float32), info)
    info = jnp.where(lane == 2, rank1, info)
    info = jnp.where(lane == 3, rank2, info)
    info = jnp.where(lane == 4, gate1, info)
    info = jnp.where(lane == 5, gate2, info)
    info_ref[...] = info
    infot_ref[...] = info.T[0:8, :]


def _router(x2d, g, sc, sh, w_hi, w_lo, bias, seq):
    t, d = x2d.shape
    tm = 512
    tiles_per_batch = seq // tm
    per_batch = lambda i: (i // tiles_per_batch, 0, 0)
    const = lambda i: (0, 0)
    return pl.pallas_call(
        functools.partial(_router_kernel, tm=tm),
        grid=(t // tm,),
        in_specs=[
            pl.BlockSpec((tm, d), lambda i: (i, 0)),
            pl.BlockSpec((1, d), const),
            pl.BlockSpec((None, 1, d), per_batch),
            pl.BlockSpec((None, 1, d), per_batch),
            pl.BlockSpec((d, ROUTER_LANES), const),
            pl.BlockSpec((d, ROUTER_LANES), const),
            pl.BlockSpec((1, ROUTER_LANES), const),
        ],
        out_specs=[
            pl.BlockSpec((tm, d // 2), lambda i: (i, 0)),
            pl.BlockSpec((tm, ROUTER_LANES), lambda i: (i, 0)),
            pl.BlockSpec((8, tm), lambda i: (0, i)),
            pl.BlockSpec((8, ROUTER_LANES), const),
        ],
        out_shape=[
            jax.ShapeDtypeStruct((t, d // 2), jnp.uint32),
            jax.ShapeDtypeStruct((t, ROUTER_LANES), jnp.float32),
            jax.ShapeDtypeStruct((8, t), jnp.float32),
            jax.ShapeDtypeStruct((8, ROUTER_LANES), jnp.float32),
        ],
        scratch_shapes=[pltpu.VMEM((8, ROUTER_LANES), jnp.float32)],
        compiler_params=_params(("arbitrary",)),
        name="moe_router",
    )(x2d, g.reshape(1, d), sc, sh, w_hi, w_lo, bias)


def _sc_mesh():
    return plsc.VectorSubcoreMesh(core_axis_name="c", subcore_axis_name="s")


def _sc_dispatch(rows, d0, d1, pad_idx, n_slots):
    t, w = rows.shape
    n_pad = pad_idx.shape[1]
    zeros = jnp.zeros((SC_WINDOW, w), rows.dtype)
    sem = (pltpu.PARALLEL, pltpu.ARBITRARY)

    @pl.kernel(out_type=jax.ShapeDtypeStruct((n_slots + SC_WINDOW, w), rows.dtype), mesh=_sc_mesh())
    def dispatch(x_hbm, d0_hbm, d1_hbm, z_hbm, p_hbm, o_hbm):
        def scatter_rows(x_vmem, i0_vmem, i1_vmem):
            cols = pl.ds(pl.program_id(1) * SC_COLS, SC_COLS)
            pltpu.sync_copy(x_vmem, o_hbm.at[i0_vmem.at[0], cols])
            pltpu.sync_copy(x_vmem, o_hbm.at[i1_vmem.at[0], cols])

        pltpu.emit_pipeline(
            scatter_rows,
            grid=(t // SC_WINDOW, w // SC_COLS),
            in_specs=[pl.BlockSpec((SC_WINDOW, SC_COLS), lambda i, j: (i, j)),
                      pl.BlockSpec((1, SC_WINDOW), lambda i, j: (0, i)),
                      pl.BlockSpec((1, SC_WINDOW), lambda i, j: (0, i))],
            out_specs=[],
            core_axis_name=("c", "s"),
            dimension_semantics=sem,
        )(x_hbm, d0_hbm, d1_hbm)

        def scatter_zeros(z_vmem, p_vmem):
            cols = pl.ds(pl.program_id(1) * SC_COLS, SC_COLS)
            pltpu.sync_copy(z_vmem, o_hbm.at[p_vmem.at[0], cols])

        pltpu.emit_pipeline(
            scatter_zeros,
            grid=(n_pad // SC_WINDOW, w // SC_COLS),
            in_specs=[pl.BlockSpec((SC_WINDOW, SC_COLS), lambda i, j: (0, j)),
                      pl.BlockSpec((1, SC_WINDOW), lambda i, j: (0, i))],
            out_specs=[],
            core_axis_name=("c", "s"),
            dimension_semantics=sem,
        )(z_hbm, p_hbm)

    return dispatch(rows, d0, d1, zeros, pad_idx)


def _sc_gather(src, idx):
    n_out = idx.shape[1]
    w = src.shape[1]

    @pl.kernel(out_type=jax.ShapeDtypeStruct((n_out, w), src.dtype), mesh=_sc_mesh())
    def gather(x_hbm, i_hbm, o_hbm):
        def gather_rows(i_vmem, o_vmem):
            cols = pl.ds(pl.program_id(1) * SC_COLS, SC_COLS)
            pltpu.sync_copy(x_hbm.at[i_vmem.at[0], cols], o_vmem)

        pltpu.emit_pipeline(
            gather_rows,
            grid=(n_out // SC_WINDOW, w // SC_COLS),
            in_specs=[pl.BlockSpec((1, SC_WINDOW), lambda i, j: (0, i))],
            out_specs=[pl.BlockSpec((SC_WINDOW, SC_COLS), lambda i, j: (i, j))],
            core_axis_name=("c", "s"),
            dimension_semantics=(pltpu.PARALLEL, pltpu.ARBITRARY),
        )(i_hbm, o_hbm)

    return gather(src, idx)


def _expert_kernel(meta_ref, x_ref, w1_hbm, w3_hbm, w2_hbm, y_ref,
                   w1_buf, w3_buf, w2_buf, w1_c, w3_c, w2_c, sems, *, layer, nb):
    i = pl.program_id(0)
    e = meta_ref[i]
    slot = meta_ref[nb + i]
    nxt = meta_ref[2 * nb + i]
    used = i < meta_ref[3 * nb]
    first = used & ((i == 0) | (e != meta_ref[jnp.maximum(i - 1, 0)]))

    def weight_copies(expert, s):
        return (pltpu.make_async_copy(w1_hbm.at[layer, expert], w1_buf.at[s], sems.at[s, 0]),
                pltpu.make_async_copy(w3_hbm.at[layer, expert], w3_buf.at[s], sems.at[s, 1]),
                pltpu.make_async_copy(w2_hbm.at[layer, expert], w2_buf.at[s], sems.at[s, 2]))

    @pl.when(i == 0)
    def _():
        for cp in weight_copies(e, slot):
            cp.start()

    @pl.when(first)
    def _():
        for cp in weight_copies(e, slot):
            cp.wait()

        @pl.when(nxt >= 0)
        def _():
            for cp in weight_copies(nxt, 1 - slot):
                cp.start()

        w1_c[...] = w1_buf[slot].astype(jnp.bfloat16)
        w3_c[...] = w3_buf[slot].astype(jnp.bfloat16)
        w2_c[...] = w2_buf[slot].astype(jnp.bfloat16)

    @pl.when(used)
    def _():
        x_hi, x_lo = _unpack_bf16_pairs(x_ref[...])
        xb = jnp.concatenate([x_hi, x_lo], axis=-1).astype(jnp.bfloat16)
        h1 = _bdot(xb, w1_c[...])
        h3 = _bdot(xb, w3_c[...])
        act = (h1 * jax.nn.sigmoid(h1) * h3).astype(jnp.bfloat16)
        y_ref[...] = _pack_bf16_pairs(_bdot(act, w2_c[...]))

    @pl.when(jnp.logical_not(used))
    def _():
        y_ref[...] = jnp.zeros_like(y_ref)


def _experts(x_slots, blk_meta, w1, w3, w2, layer, nb):
    dp = x_slots.shape[1]
    d, de = w1.shape[-2:]
    rowblk = lambda i, s: (jnp.minimum(i, s[3 * nb] - 1), 0)
    grid_spec = pltpu.PrefetchScalarGridSpec(
        num_scalar_prefetch=1,
        grid=(nb,),
        in_specs=[
            pl.BlockSpec((MOE_BLOCK, dp), rowblk),
            pl.BlockSpec(memory_space=pl.ANY),
            pl.BlockSpec(memory_space=pl.ANY),
            pl.BlockSpec(memory_space=pl.ANY),
        ],
        out_specs=pl.BlockSpec((MOE_BLOCK, dp), lambda i, s: (i, 0)),
        scratch_shapes=[
            pltpu.VMEM((2, d, de), jnp.float32),
            pltpu.VMEM((2, d, de), jnp.float32),
            pltpu.VMEM((2, de, d), jnp.float32),
            pltpu.VMEM((d, de), jnp.bfloat16),
            pltpu.VMEM((d, de), jnp.bfloat16),
            pltpu.VMEM((de, d), jnp.bfloat16),
            pltpu.SemaphoreType.DMA((2, 3)),
        ],
    )
    return pl.pallas_call(
        functools.partial(_expert_kernel, layer=layer, nb=nb),
        grid_spec=grid_spec,
        out_shape=jax.ShapeDtypeStruct((nb * MOE_BLOCK, dp), jnp.uint32),
        compiler_params=_params(("arbitrary",)),
        name="moe_experts",
    )(blk_meta, x_slots, w1, w3, w2)


def _combine_kernel(x_ref, y1_ref, y2_ref, info_ref, gt_ref, o_ref):
    k = x_ref.shape[1] // 2
    info = info_ref[...]
    g1 = info[:, 4:5]
    g2 = info[:, 5:6]
    y1_hi, y1_lo = _unpack_bf16_pairs(y1_ref[...])
    y2_hi, y2_lo = _unpack_bf16_pairs(y2_ref[...])
    o_ref[:, :k] = x_ref[:, :k] + gt_ref[:, :k] * (g1 * y1_hi + g2 * y2_hi)
    o_ref[:, k:] = x_ref[:, k:] + gt_ref[:, k:] * (g1 * y1_lo + g2 * y2_lo)


def _combine(x2d, y_pairs, info, gt, seq):
    t, d = x2d.shape
    tm = 512
    tiles_per_batch = seq // tm
    second = t // tm
    return pl.pallas_call(
        _combine_kernel,
        grid=(t // tm,),
        in_specs=[
            pl.BlockSpec((tm, d), lambda i: (i, 0)),
            pl.BlockSpec((tm, d // 2), lambda i: (i, 0)),
            pl.BlockSpec((tm, d // 2), lambda i: (i + second, 0)),
            pl.BlockSpec((tm, ROUTER_LANES), lambda i: (i, 0)),
            pl.BlockSpec((None, 1, d), lambda i: (i // tiles_per_batch, 0, 0)),
        ],
        out_specs=pl.BlockSpec((tm, d), lambda i: (i, 0)),
        out_shape=jax.ShapeDtypeStruct((t, d), jnp.float32),
        compiler_params=_params(("arbitrary",)),
        name="moe_combine",
    )(x2d, y_pairs, y_pairs, info, gt)


def _moe_layer(x2d, g, sc, sh, gt, w_group, b_group, w_router, b_router, w1, w3, w2, layer, seq):
    t, d = x2d.shape
    w_cat = jnp.zeros((d, ROUTER_LANES), jnp.float32)
    w_cat = w_cat.at[:, :N_EXPERTS].set(w_router).at[:, N_EXPERTS:N_EXPERTS + N_GROUPS].set(w_group)
    b_cat = jnp.zeros((1, ROUTER_LANES), jnp.float32)
    b_cat = b_cat.at[0, :N_EXPERTS].set(b_router).at[0, N_EXPERTS:N_EXPERTS + N_GROUPS].set(b_group)
    w_hi, w_lo = _split_hi_lo(w_cat)
    hn, info, info_t, cnt = _router(x2d, g, sc, sh, w_hi, w_lo, b_cat, seq)

    counts = cnt[0, :N_EXPERTS].astype(jnp.int32)
    padded = (counts + MOE_BLOCK - 1) // MOE_BLOCK * MOE_BLOCK
    pad_ends = jnp.cumsum(padded)
    pad_starts = pad_ends - padded
    nb = -(-(2 * t) // MOE_BLOCK) + N_EXPERTS
    n_slots = nb * MOE_BLOCK
    it = info_t.astype(jnp.int32)
    onehot_start = lambda e: jnp.sum(
        jnp.where(e[None, :] == jnp.arange(N_EXPERTS, dtype=jnp.int32)[:, None],
                  pad_starts[:, None], 0), axis=0)
    dest1 = (onehot_start(it[0]) + it[2]).reshape(1, t)
    dest2 = (onehot_start(it[1]) + it[3]).reshape(1, t)
    lane = jnp.arange(MOE_BLOCK, dtype=jnp.int32)[None, :]
    n_padding = (padded - counts)[:, None]
    wrapped = (pad_starts + counts)[:, None] + lane % jnp.maximum(n_padding, 1)
    pad_idx = jnp.where(n_padding > 0, wrapped, n_slots + lane % SC_WINDOW).reshape(1, -1)
    experts = jnp.arange(N_EXPERTS, dtype=jnp.int32)
    blk = jnp.arange(nb, dtype=jnp.int32)
    blk_exp = jnp.minimum(
        jnp.sum((pad_ends[None, :] <= (blk * MOE_BLOCK)[:, None]).astype(jnp.int32), axis=1),
        N_EXPERTS - 1)
    n_used = pad_ends[-1] // MOE_BLOCK
    prev_exp = jnp.concatenate([jnp.full((1,), -1, jnp.int32), blk_exp[:-1]])
    is_first = (blk < n_used) & (blk_exp != prev_exp)
    blk_slot = (jnp.cumsum(is_first.astype(jnp.int32)) + 1) % 2
    later = (experts[None, :] > experts[:, None]) & (padded[None, :] > 0)
    nxt_of = jnp.min(jnp.where(later, experts[None, :], N_EXPERTS), axis=1)
    nxt_of = jnp.where(nxt_of == N_EXPERTS, -1, nxt_of)
    blk_nxt = jnp.sum(jnp.where(blk_exp[:, None] == experts[None, :], nxt_of[None, :], 0), axis=1)
    blk_meta = jnp.concatenate([blk_exp, blk_slot, blk_nxt, n_used[None]]).astype(jnp.int32)

    x_slots = _sc_dispatch(hn, dest1, dest2, pad_idx, n_slots)
    y_slots = _experts(x_slots, blk_meta, w1, w3, w2, layer, nb)
    y_pairs = _sc_gather(y_slots, jnp.concatenate([dest1, dest2], axis=1))
    return _combine(x2d, y_pairs, info, gt, seq)


def kernel(x, c, w_ada, b_ada, norm1_g, norm2_g, ml_w_in, ml_b_gate, ml_g_out, ml_w_out,
           sw_w_in, sw_g_q, sw_g_k, sw_sinks, sw_w_out, moe_w_group, moe_b_group,
           moe_w_router, moe_b_router, moe_w1, moe_w3, moe_w2):
    bsz, seq, d = x.shape
    depth = w_ada.shape[0]
    bf = jnp.bfloat16
    mod = _ada_mod(c, w_ada, b_ada)
    x2d = x.reshape(bsz * seq, d)
    for layer in range(depth):
        parts = [mod[layer, :, i * d:(i + 1) * d].reshape(bsz, 1, d) for i in range(6)]
        sh1, sc1, gt1, sh2, sc2, gt2 = parts
        j = layer // 2
        if layer % 2 == 0:
            w = ml_w_in[j]
            q_w, k_w = w[:, :ML_QK], w[:, ML_QK:2 * ML_QK]
            v_w = w[:, 2 * ML_QK:2 * ML_QK + ML_V]
            o_w = w[:, 2 * ML_QK + ML_V:2 * ML_QK + 2 * ML_V]
            g_w = w[:, 2 * ML_QK + 2 * ML_V:]
            w_main = jnp.concatenate([v_w, o_w, q_w], axis=1).astype(bf)
            wg_t = jnp.zeros((ML_GATE_ROWS, d), jnp.float32).at[:2 * ML_HEADS].set(g_w.T)
            wg_hi, wg_lo = _split_hi_lo(wg_t)
            main, k_t, g_t = _inproj(
                x2d, norm1_g[layer], sc1, sh1, w_main, seq,
                ml_extra=(k_w.T.astype(bf), wg_hi, wg_lo),
                q_cols=(2 * ML_V, 2 * ML_V + ML_QK), q_scale=ML_DK ** -0.5)
            a = _mlstm_core(main, k_t, g_t, ml_b_gate[j], ml_g_out[j], bsz, seq)
            x2d = _outproj_residual(a, ml_w_out[j].astype(bf), x2d, gt1, seq)
        else:
            w = sw_w_in[j]
            dq = SW_Q_HEADS * SW_DH
            dkv = SW_KV_HEADS * SW_DH
            dup = lambda m: jnp.concatenate(
                [m.reshape(d, SW_KV_HEADS, 1, SW_DH)] * 2, axis=2).reshape(d, 2 * dkv)
            w_main = jnp.concatenate(
                [w[:, :dq], dup(w[:, dq:dq + dkv]), dup(w[:, dq + dkv:])], axis=1).astype(bf)
            (proj,) = _inproj(x2d, norm1_g[layer], sc1, sh1, w_main, seq)
            a = _swa_core(proj, sw_sinks[j], sw_g_q[j], sw_g_k[j], bsz, seq)
            x2d = _outproj_residual(a, sw_w_out[j].astype(bf), x2d, gt1, seq)
        x2d = _moe_layer(x2d, norm2_g[layer], sc2, sh2, gt2, moe_w_group[layer],
                         moe_b_group[layer], moe_w_router[layer], moe_b_router[layer],
                         moe_w1, moe_w3, moe_w2, layer, seq)
    return x2d.reshape(bsz, seq, d)
```

```python
import functools

import jax
import jax.numpy as jnp
from jax import lax
from jax.experimental import pallas as pl
from jax.experimental.pallas import tpu as pltpu
from jax.experimental.pallas import tpu_sc as plsc

EPS = 1e-6
GATE_CAP = 15.0
LOG2E = 1.4426950408889634

ML_HEADS = 4
ML_DK = 128
ML_DV = 256
ML_QK = ML_HEADS * ML_DK
ML_V = ML_HEADS * ML_DV
ML_GATE_ROWS = 16

SW_Q_HEADS = 16
SW_KV_HEADS = 4
SW_GROUP = SW_Q_HEADS // SW_KV_HEADS
SW_DH = 64
SW_WINDOW = 128
LANES = 128

N_GROUPS = 8
EXPERTS_PER_GROUP = 8
N_EXPERTS = N_GROUPS * EXPERTS_PER_GROUP
MOE_BLOCK = 256
ROUTER_LANES = 128
SC_WINDOW = 128
SC_COLS = 256

VMEM_LIMIT = 56 * 1024 * 1024

_NT = (((1,), (1,)), ((), ()))


def _bdot(a, b):
    return jnp.dot(a, b, preferred_element_type=jnp.float32)


def _bdot_nt(a, b):
    return lax.dot_general(a, b, _NT, preferred_element_type=jnp.float32)


def _split_hi_lo(a):
    hi = a.astype(jnp.bfloat16)
    lo = (a - hi.astype(jnp.float32)).astype(jnp.bfloat16)
    return hi, lo


def _params(sem):
    return pltpu.CompilerParams(dimension_semantics=sem, vmem_limit_bytes=VMEM_LIMIT)


def _ada_kernel(c_ref, w_ref, b_ref, o_ref):
    c = c_ref[...]
    cond = c * jax.nn.sigmoid(c)
    c_hi, c_lo = _split_hi_lo(cond)
    w_hi, w_lo = _split_hi_lo(w_ref[...])
    acc = _bdot(c_hi, w_hi) + (_bdot(c_lo, w_hi) + _bdot(c_hi, w_lo))
    o_ref[...] = acc + b_ref[...]


def _ada_mod(c, w_ada, b_ada):
    depth, d, n = w_ada.shape
    bsz = c.shape[0]
    rows = 8
    tn = 768
    c_pad = jnp.zeros((rows, d), jnp.float32).at[:bsz].set(c)
    out = pl.pallas_call(
        _ada_kernel,
        grid=(depth, n // tn),
        in_specs=[
            pl.BlockSpec((rows, d), lambda l, j: (0, 0)),
            pl.BlockSpec((None, d, tn), lambda l, j: (l, 0, j)),
            pl.BlockSpec((None, 1, tn), lambda l, j: (l, 0, j)),
        ],
        out_specs=pl.BlockSpec((None, rows, tn), lambda l, j: (l, 0, j)),
        out_shape=jax.ShapeDtypeStruct((depth, rows, n), jnp.float32),
        compiler_params=_params(("arbitrary", "arbitrary")),
        name="ada_mod",
    )(c_pad, w_ada, b_ada.reshape(depth, 1, n))
    return out[:, :bsz]


def _modulated_norm(x, g, sc, sh):
    y = x * lax.rsqrt(jnp.mean(x * x, axis=-1, keepdims=True) + EPS)
    return y * (g * (1.0 + sc)) + sh


def _inproj_kernel(*refs, n_main, chunk, q_cols, q_scale, with_ml):
    if with_ml:
        (x_ref, g_ref, sc_ref, sh_ref, w_ref, wk_ref, wgh_ref, wgl_ref,
         o_ref, kt_ref, gt_ref) = refs
    else:
        x_ref, g_ref, sc_ref, sh_ref, w_ref, o_ref = refs
    hn = _modulated_norm(x_ref[...], g_ref[...], sc_ref[...], sh_ref[...])
    hb = hn.astype(jnp.bfloat16)
    for c0 in range(0, n_main, chunk):
        acc = _bdot(hb, w_ref[:, c0:c0 + chunk])
        if q_cols is not None and c0 >= q_cols[0]:
            acc = acc * q_scale
        o_ref[:, c0:c0 + chunk] = acc.astype(o_ref.dtype)
    if with_ml:
        kt_ref[...] = _bdot_nt(wk_ref[...], hb).astype(kt_ref.dtype)
        h_lo = (hn - hb.astype(jnp.float32)).astype(jnp.bfloat16)
        gt_ref[...] = (_bdot_nt(wgh_ref[...], hb)
                       + (_bdot_nt(wgh_ref[...], h_lo) + _bdot_nt(wgl_ref[...], hb)))


def _inproj(x2d, g, sc, sh, w_main, seq, *, ml_extra=None, q_cols=None, q_scale=1.0):
    t, d = x2d.shape
    tm = 512
    n_main = w_main.shape[1]
    tiles_per_batch = seq // tm
    row = lambda i: (i, 0)
    per_batch = lambda i: (i // tiles_per_batch, 0, 0)
    const = lambda i: (0, 0)
    in_specs = [
        pl.BlockSpec((tm, d), row),
        pl.BlockSpec((1, d), const),
        pl.BlockSpec((None, 1, d), per_batch),
        pl.BlockSpec((None, 1, d), per_batch),
        pl.BlockSpec((d, n_main), const),
    ]
    args = [x2d, g.reshape(1, d), sc, sh, w_main]
    out_specs = [pl.BlockSpec((tm, n_main), row)]
    out_shape = [jax.ShapeDtypeStruct((t, n_main), jnp.bfloat16)]
    if ml_extra is not None:
        wk_t, wg_hi, wg_lo = ml_extra
        in_specs += [pl.BlockSpec(wk_t.shape, const),
                     pl.BlockSpec(wg_hi.shape, const),
                     pl.BlockSpec(wg_lo.shape, const)]
        args += [wk_t, wg_hi, wg_lo]
        out_specs += [pl.BlockSpec((wk_t.shape[0], tm), lambda i: (0, i)),
                      pl.BlockSpec((ML_GATE_ROWS, tm), lambda i: (0, i))]
        out_shape += [jax.ShapeDtypeStruct((wk_t.shape[0], t), jnp.bfloat16),
                      jax.ShapeDtypeStruct((ML_GATE_ROWS, t), jnp.float32)]
    kern = functools.partial(_inproj_kernel, n_main=n_main, chunk=512, q_cols=q_cols,
                             q_scale=q_scale, with_ml=ml_extra is not None)
    return pl.pallas_call(
        kern,
        grid=(t // tm,),
        in_specs=in_specs,
        out_specs=out_specs,
        out_shape=out_shape,
        compiler_params=_params(("arbitrary",)),
        name="inproj_ml" if ml_extra is not None else "inproj_sw",
    )(*args)


def _mlstm_kernel(v_ref, o_ref, q_ref, kt_ref, gt_ref, bg_ref, gout_ref, out_ref,
                  c_ref, n_ref, m_ref, *, chunk):
    L = chunk
    H, dk, dv = ML_HEADS, ML_DK, ML_DV

    @pl.when(pl.program_id(1) == 0)
    def _():
        c_ref[...] = jnp.zeros_like(c_ref)
        n_ref[...] = jnp.zeros_like(n_ref)
        m_ref[...] = jnp.zeros_like(m_ref)

    z = gt_ref[...] + bg_ref[...]
    gates = GATE_CAP * jnp.tanh(z / GATE_CAP)
    log_f = jnp.minimum(gates, 0.0) - jnp.log1p(jnp.exp(-jnp.abs(gates)))
    grow = lax.broadcasted_iota(jnp.int32, (ML_GATE_ROWS, L), 0)
    is_i = grow < H
    slab = jnp.where(is_i, gates, log_f)

    r_idx = lax.broadcasted_iota(jnp.int32, (L, L), 0)
    c_idx = lax.broadcasted_iota(jnp.int32, (L, L), 1)
    upper = jnp.where(r_idx <= c_idx, 1.0, 0.0).astype(jnp.bfloat16)
    a1 = slab.astype(jnp.bfloat16)
    r1 = slab - a1.astype(jnp.float32)
    a2 = r1.astype(jnp.bfloat16)
    a3 = (r1 - a2.astype(jnp.float32)).astype(jnp.bfloat16)
    cum = _bdot(a1, upper) + (_bdot(a2, upper) + _bdot(a3, upper))
    rows = jnp.where(is_i, gates, cum)
    cols = jnp.concatenate(
        [rows, jnp.zeros((LANES - ML_GATE_ROWS, L), jnp.float32)], axis=0).T

    causal = r_idx >= c_idx
    for h in range(H):
        i_r = rows[h:h + 1, :]
        b_r = rows[H + h:H + h + 1, :]
        b_c = cols[:, H + h:H + h + 1]
        m_prev = m_ref[h][0:1, 0:1]
        n_row = n_ref[h][0:1, :]
        c_st = c_ref[h]
        q = q_ref[:, h * dk:(h + 1) * dk]
        kt = kt_ref[h * dk:(h + 1) * dk, :]
        v = v_ref[:, h * dv:(h + 1) * dv]

        u_r = i_r - b_r
        d_log = jnp.where(causal, b_c + u_r, -jnp.inf)
        inter = b_c + m_prev
        m_t = jnp.maximum(inter, jnp.max(d_log, axis=-1, keepdims=True))
        w_intra = jnp.exp(d_log - m_t)
        w_inter = jnp.exp(inter - m_t)
        s = _bdot(q, kt) * w_intra
        num = w_inter * _bdot(q, c_st.astype(jnp.bfloat16)) + _bdot(s.astype(jnp.bfloat16), v)
        qn = jnp.sum(q.astype(jnp.float32) * n_row, axis=-1, keepdims=True)
        den = w_inter * qn + jnp.sum(s, axis=-1, keepdims=True)
        hb = num * (1.0 / jnp.maximum(jnp.abs(den), jnp.exp(-m_t)))

        g_last = b_r[:, L - 1:L]
        a_r = g_last - b_r + i_r
        m_new = jnp.maximum(g_last + m_prev, jnp.max(a_r, axis=-1, keepdims=True))
        w_state = jnp.exp(a_r - m_new)
        decay = jnp.exp(g_last + m_prev - m_new)
        kw = (kt.astype(jnp.float32) * w_state).astype(jnp.bfloat16)
        c_ref[h] = decay * c_st + _bdot(kw, v)
        w16 = jnp.broadcast_to(w_state, (16, L)).astype(jnp.bfloat16)
        dn = _bdot_nt(w16, kt)
        n_ref[h] = jnp.broadcast_to(decay * n_row + dn[0:1, :], (8, dk))
        m_ref[h] = jnp.broadcast_to(m_new, (8, LANES))

        y = hb * lax.rsqrt(jnp.mean(hb * hb, axis=-1, keepdims=True) + EPS)
        y = y * gout_ref[:, h * dv:(h + 1) * dv]
        og = jax.nn.sigmoid(o_ref[:, h * dv:(h + 1) * dv].astype(jnp.float32))
        out_ref[:, h * dv:(h + 1) * dv] = (y * og).astype(out_ref.dtype)


def _mlstm_core(main, k_t, g_t, b_gate, g_out, bsz, seq, chunk=256):
    t = main.shape[0]
    nc = seq // chunk
    blk = lambda b, c: b * nc + c
    bg = jnp.zeros((ML_GATE_ROWS, 1), jnp.float32).at[:2 * ML_HEADS, 0].set(b_gate)
    return pl.pallas_call(
        functools.partial(_mlstm_kernel, chunk=chunk),
        grid=(bsz, nc),
        in_specs=[
            pl.BlockSpec((chunk, ML_V), lambda b, c: (blk(b, c), 0)),
            pl.BlockSpec((chunk, ML_V), lambda b, c: (blk(b, c), 1)),
            pl.BlockSpec((chunk, ML_QK), lambda b, c: (blk(b, c), 4)),
            pl.BlockSpec((ML_QK, chunk), lambda b, c: (0, blk(b, c))),
            pl.BlockSpec((ML_GATE_ROWS, chunk), lambda b, c: (0, blk(b, c))),
            pl.BlockSpec((ML_GATE_ROWS, 1), lambda b, c: (0, 0)),
            pl.BlockSpec((1, ML_V), lambda b, c: (0, 0)),
        ],
        out_specs=pl.BlockSpec((chunk, ML_V), lambda b, c: (blk(b, c), 0)),
        out_shape=jax.ShapeDtypeStruct((t, ML_V), jnp.bfloat16),
        scratch_shapes=[
            pltpu.VMEM((ML_HEADS, ML_DK, ML_DV), jnp.float32),
            pltpu.VMEM((ML_HEADS, 8, ML_DK), jnp.float32),
            pltpu.VMEM((ML_HEADS, 8, LANES), jnp.float32),
        ],
        compiler_params=_params(("arbitrary", "arbitrary")),
        name="mlstm_core",
    )(main, main, main, k_t, g_t, bg, g_out.reshape(1, ML_V))


def _outproj_kernel(a_ref, w_ref, x_ref, gt_ref, o_ref):
    o_ref[...] = x_ref[...] + gt_ref[...] * _bdot(a_ref[...], w_ref[...])


def _outproj_residual(a, w, x2d, gt, seq):
    t, d = x2d.shape
    tm = min(1024, seq)
    tiles_per_batch = seq // tm
    return pl.pallas_call(
        _outproj_kernel,
        grid=(t // tm,),
        in_specs=[
            pl.BlockSpec((tm, a.shape[1]), lambda i: (i, 0)),
            pl.BlockSpec(w.shape, lambda i: (0, 0)),
            pl.BlockSpec((tm, d), lambda i: (i, 0)),
            pl.BlockSpec((None, 1, d), lambda i: (i // tiles_per_batch, 0, 0)),
        ],
        out_specs=pl.BlockSpec((tm, d), lambda i: (i, 0)),
        out_shape=jax.ShapeDtypeStruct((t, d), jnp.float32),
        compiler_params=_params(("arbitrary",)),
        name="outproj_residual",
    )(a, w, x2d, gt)


def _swa_kernel(sinks_ref, q_ref, kc_ref, kp_ref, vc_ref, vp_ref, gq_ref, gk_ref, bias_ref, o_ref):
    W = SW_WINDOW
    lane = lax.broadcasted_iota(jnp.int32, (W, LANES), 1)
    low = lane < SW_DH
    row_col = lax.broadcasted_iota(jnp.int32, (SW_GROUP * W, 1), 0)
    gq = gq_ref[...]
    gk = gk_ref[...]
    for g in range(SW_KV_HEADS):
        sl = slice(g * LANES, (g + 1) * LANES)
        k2 = jnp.concatenate([kp_ref[:, sl], kc_ref[:, sl]], axis=0).astype(jnp.float32)
        kn = k2 * lax.rsqrt(jnp.mean(k2 * k2, axis=-1, keepdims=True) + EPS)
        kn = (kn * gk).astype(jnp.bfloat16)
        v2 = jnp.concatenate([vp_ref[:, sl], vc_ref[:, sl]], axis=0)
        parts = []
        for p in range(2):
            c0 = g * SW_GROUP * SW_DH + p * LANES
            qp = q_ref[:, c0:c0 + LANES].astype(jnp.float32)
            sq = qp * qp
            ss_lo = jnp.sum(jnp.where(low, sq, 0.0), axis=-1, keepdims=True)
            ss_hi = jnp.sum(jnp.where(low, 0.0, sq), axis=-1, keepdims=True)
            rs = jnp.where(low, lax.rsqrt(ss_lo / SW_DH + EPS), lax.rsqrt(ss_hi / SW_DH + EPS))
            qn = qp * rs * gq
            parts.append(jnp.where(low, qn, 0.0).astype(jnp.bfloat16))
            parts.append(jnp.where(low, 0.0, qn).astype(jnp.bfloat16))
        q4 = jnp.concatenate(parts, axis=0)
        scores = _bdot_nt(q4, kn) + bias_ref[...]
        sink = jnp.full((SW_GROUP * W, 1), sinks_ref[g * SW_GROUP + SW_GROUP - 1], jnp.float32)
        for j in range(SW_GROUP - 2, -1, -1):
            sink = jnp.where(row_col < (j + 1) * W, sinks_ref[g * SW_GROUP + j], sink)
        m = jnp.maximum(jnp.max(scores, axis=-1, keepdims=True), sink)
        pexp = jnp.exp2(scores - m)
        denom = jnp.sum(pexp, axis=-1, keepdims=True) + jnp.exp2(sink - m)
        o4 = _bdot(pexp.astype(jnp.bfloat16), v2) * (1.0 / denom)
        for p in range(2):
            oa = o4[(2 * p) * W:(2 * p + 1) * W]
            ob = o4[(2 * p + 1) * W:(2 * p + 2) * W]
            c0 = g * SW_GROUP * SW_DH + p * LANES
            o_ref[:, c0:c0 + LANES] = jnp.where(low, oa, ob).astype(o_ref.dtype)


def _swa_core(proj, sinks, g_q, g_k, bsz, seq):
    t = proj.shape[0]
    W = SW_WINDOW
    nb = seq // W
    dq = SW_Q_HEADS * SW_DH
    kv_w = SW_KV_HEADS * LANES
    k_blk = dq // kv_w
    v_blk = k_blk + 1
    cur = lambda b, n, s: b * nb + n
    prev = lambda b, n, s: b * nb + jnp.maximum(n - 1, 0)
    gq2 = jnp.concatenate([g_q, g_q]).reshape(1, LANES) * (SW_DH ** -0.5 * LOG2E)
    gk2 = jnp.concatenate([g_k, g_k]).reshape(1, LANES)
    sinks = sinks.astype(jnp.float32) * LOG2E
    qi = (jnp.arange(SW_GROUP * W) % W)[:, None]
    ki = jnp.arange(2 * W)[None, :]
    rel = qi + W - ki
    in_win = (rel >= 0) & (rel < W)
    bias = jnp.stack([jnp.where(in_win & (ki >= W), 0.0, -jnp.inf),
                      jnp.where(in_win, 0.0, -jnp.inf)]).astype(jnp.float32)
    grid_spec = pltpu.PrefetchScalarGridSpec(
        num_scalar_prefetch=1,
        grid=(bsz, nb),
        in_specs=[
            pl.BlockSpec((W, dq), lambda b, n, s: (cur(b, n, s), 0)),
            pl.BlockSpec((W, kv_w), lambda b, n, s: (cur(b, n, s), k_blk)),
            pl.BlockSpec((W, kv_w), lambda b, n, s: (prev(b, n, s), k_blk)),
            pl.BlockSpec((W, kv_w), lambda b, n, s: (cur(b, n, s), v_blk)),
            pl.BlockSpec((W, kv_w), lambda b, n, s: (prev(b, n, s), v_blk)),
            pl.BlockSpec((1, LANES), lambda b, n, s: (0, 0)),
            pl.BlockSpec((1, LANES), lambda b, n, s: (0, 0)),
            pl.BlockSpec((None, SW_GROUP * W, 2 * W), lambda b, n, s: (jnp.minimum(n, 1), 0, 0)),
        ],
        out_specs=pl.BlockSpec((W, dq), lambda b, n, s: (cur(b, n, s), 0)),
    )
    return pl.pallas_call(
        _swa_kernel,
        grid_spec=grid_spec,
        out_shape=jax.ShapeDtypeStruct((t, dq), jnp.bfloat16),
        compiler_params=_params(("arbitrary", "arbitrary")),
        name="swa_core",
    )(sinks.astype(jnp.float32), proj, proj, proj, proj, proj, gq2, gk2, bias)


def _pack_rounded_pairs(r):
    k = r.shape[1] // 2
    hi = lax.bitcast_convert_type(r[:, :k], jnp.uint32)
    lo = lax.bitcast_convert_type(r[:, k:], jnp.uint32)
    return hi | (lo >> 16)


def _pack_bf16_pairs(a):
    return _pack_rounded_pairs(a.astype(jnp.bfloat16).astype(jnp.float32))


def _unpack_bf16_pairs(u):
    hi = lax.bitcast_convert_type(u & jnp.uint32(0xFFFF0000), jnp.float32)
    lo = lax.bitcast_convert_type(u << 16, jnp.float32)
    return hi, lo


def _router_kernel(x_ref, g_ref, sc_ref, sh_ref, wh_ref, wl_ref, b_ref,
                   hn_ref, info_ref, infot_ref, cnt_ref, carry_ref, lower_ref, *, tm):
    @pl.when(pl.program_id(0) == 0)
    def _():
        carry_ref[...] = jnp.zeros_like(carry_ref)
        r_idx = lax.broadcasted_iota(jnp.int32, (tm, tm), 0)
        c_idx = lax.broadcasted_iota(jnp.int32, (tm, tm), 1)
        lower_ref[...] = jnp.where(c_idx < r_idx, 1.0, 0.0).astype(jnp.bfloat16)

    hn = _modulated_norm(x_ref[...], g_ref[...], sc_ref[...], sh_ref[...])
    h_hi = hn.astype(jnp.bfloat16)
    hi_f32 = h_hi.astype(jnp.float32)
    h_lo = (hn - hi_f32).astype(jnp.bfloat16)
    hn_ref[...] = _pack_rounded_pairs(hi_f32)
    wide = _bdot(h_hi, wl_ref[...])
    logits = (wide[:, :ROUTER_LANES] + (_bdot(h_lo, wh_ref[...]) + wide[:, ROUTER_LANES:])
              + b_ref[...])
    lane = lax.broadcasted_iota(jnp.int32, (tm, ROUTER_LANES), 1).astype(jnp.float32)
    big = float(4 * ROUTER_LANES)
    neg = -jnp.inf

    is_grp = (lane >= N_EXPERTS) & (lane < N_EXPERTS + N_GROUPS)
    gl = jnp.where(is_grp, logits, neg)
    gmax = jnp.max(gl, axis=-1, keepdims=True)
    gsel = jnp.min(jnp.where(gl == gmax, lane, big), axis=-1, keepdims=True) - N_EXPERTS
    p_grp = 1.0 / jnp.sum(jnp.exp(gl - gmax), axis=-1, keepdims=True)

    lo_lane = gsel * EXPERTS_PER_GROUP
    in_grp = (lane >= lo_lane) & (lane < lo_lane + EXPERTS_PER_GROUP)
    el = jnp.where(in_grp, logits, neg)
    v1 = jnp.max(el, axis=-1, keepdims=True)
    i1 = jnp.min(jnp.where(el == v1, lane, big), axis=-1, keepdims=True)
    el2 = jnp.where(lane == i1, neg, el)
    v2 = jnp.max(el2, axis=-1, keepdims=True)
    i2 = jnp.min(jnp.where(el2 == v2, lane, big), axis=-1, keepdims=True)
    e21 = jnp.exp(v2 - v1)
    gate1 = p_grp / (1.0 + e21)
    gate2 = p_grp * e21 / (1.0 + e21)

    hit1 = lane == i1
    hit2 = lane == i2
    onehot = jnp.where(hit1 | hit2, 1.0, 0.0)
    before = _bdot(lower_ref[...], onehot.astype(jnp.bfloat16)) + carry_ref[0:1, :]
    rank1 = jnp.sum(jnp.where(hit1, before, 0.0), axis=-1, keepdims=True)
    rank2 = jnp.sum(jnp.where(hit2, before, 0.0), axis=-1, keepdims=True)
    total = carry_ref[0:1, :] + jnp.sum(onehot, axis=0, keepdims=True)
    carry_ref[...] = jnp.broadcast_to(total, carry_ref.shape)
    cnt_ref[...] = jnp.broadcast_to(total, cnt_ref.shape)

    info = jnp.where(lane == 0, i1, 0.0)
    info = jnp.where(lane == 1, i2, info)
    info = jnp.where(lane == 2, rank1, info)
    info = jnp.where(lane == 3, rank2, info)
    info = jnp.where(lane == 4, gate1, info)
    info = jnp.where(lane == 5, gate2, info)
    info_ref[...] = info
    infot_ref[...] = info.T[0:8, :]


def _router(x2d, g, sc, sh, w_hi, w_lo, bias, seq):
    t, d = x2d.shape
    tm = 512
    tiles_per_batch = seq // tm
    per_batch = lambda i: (i // tiles_per_batch, 0, 0)
    const = lambda i: (0, 0)
    return pl.pallas_call(
        functools.partial(_router_kernel, tm=tm),
        grid=(t // tm,),
        in_specs=[
            pl.BlockSpec((tm, d), lambda i: (i, 0)),
            pl.BlockSpec((1, d), const),
            pl.BlockSpec((None, 1, d), per_batch),
            pl.BlockSpec((None, 1, d), per_batch),
            pl.BlockSpec((d, ROUTER_LANES), const),
            pl.BlockSpec((d, 2 * ROUTER_LANES), const),
            pl.BlockSpec((1, ROUTER_LANES), const),
        ],
        out_specs=[
            pl.BlockSpec((tm, d // 2), lambda i: (i, 0)),
            pl.BlockSpec((tm, ROUTER_LANES), lambda i: (i, 0)),
            pl.BlockSpec((8, tm), lambda i: (0, i)),
            pl.BlockSpec((8, ROUTER_LANES), const),
        ],
        out_shape=[
            jax.ShapeDtypeStruct((t, d // 2), jnp.uint32),
            jax.ShapeDtypeStruct((t, ROUTER_LANES), jnp.float32),
            jax.ShapeDtypeStruct((8, t), jnp.float32),
            jax.ShapeDtypeStruct((8, ROUTER_LANES), jnp.float32),
        ],
        scratch_shapes=[pltpu.VMEM((8, ROUTER_LANES), jnp.float32),
                        pltpu.VMEM((tm, tm), jnp.bfloat16)],
        compiler_params=_params(("arbitrary",)),
        name="moe_router",
    )(x2d, g.reshape(1, d), sc, sh, w_hi, jnp.concatenate([w_hi, w_lo], axis=1), bias)


def _sc_mesh():
    return plsc.VectorSubcoreMesh(core_axis_name="c", subcore_axis_name="s")


def _sc_dispatch(rows, d0, d1, pad_idx, n_slots):
    t, w = rows.shape
    n_pad = pad_idx.shape[1]
    zeros = jnp.zeros((SC_WINDOW, w), rows.dtype)
    sem = (pltpu.PARALLEL, pltpu.ARBITRARY)

    @pl.kernel(out_type=jax.ShapeDtypeStruct((n_slots + SC_WINDOW, w), rows.dtype), mesh=_sc_mesh())
    def dispatch(x_hbm, d0_hbm, d1_hbm, z_hbm, p_hbm, o_hbm):
        def scatter_rows(x_vmem, i0_vmem, i1_vmem):
            cols = pl.ds(pl.program_id(1) * SC_COLS, SC_COLS)
            pltpu.sync_copy(x_vmem, o_hbm.at[i0_vmem.at[0], cols])
            pltpu.sync_copy(x_vmem, o_hbm.at[i1_vmem.at[0], cols])

        pltpu.emit_pipeline(
            scatter_rows,
            grid=(t // SC_WINDOW, w // SC_COLS),
            in_specs=[pl.BlockSpec((SC_WINDOW, SC_COLS), lambda i, j: (i, j)),
                      pl.BlockSpec((1, SC_WINDOW), lambda i, j: (0, i)),
                      pl.BlockSpec((1, SC_WINDOW), lambda i, j: (0, i))],
            out_specs=[],
            core_axis_name=("c", "s"),
            dimension_semantics=sem,
        )(x_hbm, d0_hbm, d1_hbm)

        def scatter_zeros(z_vmem, p_vmem):
            cols = pl.ds(pl.program_id(1) * SC_COLS, SC_COLS)
            pltpu.sync_copy(z_vmem, o_hbm.at[p_vmem.at[0], cols])

        pltpu.emit_pipeline(
            scatter_zeros,
            grid=(n_pad // SC_WINDOW, w // SC_COLS),
            in_specs=[pl.BlockSpec((SC_WINDOW, SC_COLS), lambda i, j: (0, j)),
                      pl.BlockSpec((1, SC_WINDOW), lambda i, j: (0, i))],
            out_specs=[],
            core_axis_name=("c", "s"),
            dimension_semantics=sem,
        )(z_hbm, p_hbm)

    return dispatch(rows, d0, d1, zeros, pad_idx)


def _sc_gather(src, idx):
    n_out = idx.shape[1]
    w = src.shape[1]

    @pl.kernel(out_type=jax.ShapeDtypeStruct((n_out, w), src.dtype), mesh=_sc_mesh())
    def gather(x_hbm, i_hbm, o_hbm):
        def gather_rows(i_vmem, o_vmem):
            cols = pl.ds(pl.program_id(1) * SC_COLS, SC_COLS)
            pltpu.sync_copy(x_hbm.at[i_vmem.at[0], cols], o_vmem)

        pltpu.emit_pipeline(
            gather_rows,
            grid=(n_out // SC_WINDOW, w // SC_COLS),
            in_specs=[pl.BlockSpec((1, SC_WINDOW), lambda i, j: (0, i))],
            out_specs=[pl.BlockSpec((SC_WINDOW, SC_COLS), lambda i, j: (i, j))],
            core_axis_name=("c", "s"),
            dimension_semantics=(pltpu.PARALLEL, pltpu.ARBITRARY),
        )(i_hbm, o_hbm)

    return gather(src, idx)


def _expert_kernel(meta_ref, x_ref, w1_hbm, w3_hbm, w2_hbm, y_ref,
                   w1_buf, w3_buf, w2_buf, w1_c, w3_c, w2_c, sems, *, layer, nb):
    i = pl.program_id(0)
    e = meta_ref[i]
    slot = meta_ref[nb + i]
    nxt = meta_ref[2 * nb + i]
    used = i < meta_ref[3 * nb]
    first = used & ((i == 0) | (e != meta_ref[jnp.maximum(i - 1, 0)]))

    def weight_copies(expert, s):
        return (pltpu.make_async_copy(w1_hbm.at[layer, expert], w1_buf.at[s], sems.at[s, 0]),
                pltpu.make_async_copy(w3_hbm.at[layer, expert], w3_buf.at[s], sems.at[s, 1]),
                pltpu.make_async_copy(w2_hbm.at[layer, expert], w2_buf.at[s], sems.at[s, 2]))

    @pl.when(i == 0)
    def _():
        for cp in weight_copies(e, slot):
            cp.start()

    @pl.when(first)
    def _():
        for cp in weight_copies(e, slot):
            cp.wait()

        @pl.when(nxt >= 0)
        def _():
            for cp in weight_copies(nxt, 1 - slot):
                cp.start()

        w1_c[...] = w1_buf[slot].astype(jnp.bfloat16)
        w3_c[...] = w3_buf[slot].astype(jnp.bfloat16)
        w2_c[...] = w2_buf[slot].astype(jnp.bfloat16)

    @pl.when(used)
    def _():
        x_hi, x_lo = _unpack_bf16_pairs(x_ref[...])
        xb = jnp.concatenate([x_hi, x_lo], axis=-1).astype(jnp.bfloat16)
        h1 = _bdot(xb, w1_c[...])
        h3 = _bdot(xb, w3_c[...])
        act = (h1 * jax.nn.sigmoid(h1) * h3).astype(jnp.bfloat16)
        y_ref[...] = _pack_bf16_pairs(_bdot(act, w2_c[...]))

    @pl.when(jnp.logical_not(used))
    def _():
        y_ref[...] = jnp.zeros_like(y_ref)


def _experts(x_slots, blk_meta, w1, w3, w2, layer, nb):
    dp = x_slots.shape[1]
    d, de = w1.shape[-2:]
    rowblk = lambda i, s: (jnp.minimum(i, s[3 * nb] - 1), 0)
    grid_spec = pltpu.PrefetchScalarGridSpec(
        num_scalar_prefetch=1,
        grid=(nb,),
        in_specs=[
            pl.BlockSpec((MOE_BLOCK, dp), rowblk),
            pl.BlockSpec(memory_space=pl.ANY),
            pl.BlockSpec(memory_space=pl.ANY),
            pl.BlockSpec(memory_space=pl.ANY),
        ],
        out_specs=pl.BlockSpec((MOE_BLOCK, dp), lambda i, s: (i, 0)),
        scratch_shapes=[
            pltpu.VMEM((2, d, de), jnp.float32),
            pltpu.VMEM((2, d, de), jnp.float32),
            pltpu.VMEM((2, de, d), jnp.float32),
            pltpu.VMEM((d, de), jnp.bfloat16),
            pltpu.VMEM((d, de), jnp.bfloat16),
            pltpu.VMEM((de, d), jnp.bfloat16),
            pltpu.SemaphoreType.DMA((2, 3)),
        ],
    )
    return pl.pallas_call(
        functools.partial(_expert_kernel, layer=layer, nb=nb),
        grid_spec=grid_spec,
        out_shape=jax.ShapeDtypeStruct((nb * MOE_BLOCK, dp), jnp.uint32),
        compiler_params=_params(("arbitrary",)),
        name="moe_experts",
    )(blk_meta, x_slots, w1, w3, w2)


def _combine_kernel(x_ref, y1_ref, y2_ref, info_ref, gt_ref, o_ref):
    k = x_ref.shape[1] // 2
    info = info_ref[...]
    g1 = info[:, 4:5]
    g2 = info[:, 5:6]
    y1_hi, y1_lo = _unpack_bf16_pairs(y1_ref[...])
    y2_hi, y2_lo = _unpack_bf16_pairs(y2_ref[...])
    o_ref[:, :k] = x_ref[:, :k] + gt_ref[:, :k] * (g1 * y1_hi + g2 * y2_hi)
    o_ref[:, k:] = x_ref[:, k:] + gt_ref[:, k:] * (g1 * y1_lo + g2 * y2_lo)


def _combine(x2d, y_pairs, info, gt, seq):
    t, d = x2d.shape
    tm = min(1024, seq)
    tiles_per_batch = seq // tm
    second = t // tm
    return pl.pallas_call(
        _combine_kernel,
        grid=(t // tm,),
        in_specs=[
            pl.BlockSpec((tm, d), lambda i: (i, 0)),
            pl.BlockSpec((tm, d // 2), lambda i: (i, 0)),
            pl.BlockSpec((tm, d // 2), lambda i: (i + second, 0)),
            pl.BlockSpec((tm, ROUTER_LANES), lambda i: (i, 0)),
            pl.BlockSpec((None, 1, d), lambda i: (i // tiles_per_batch, 0, 0)),
        ],
        out_specs=pl.BlockSpec((tm, d), lambda i: (i, 0)),
        out_shape=jax.ShapeDtypeStruct((t, d), jnp.float32),
        compiler_params=_params(("arbitrary",)),
        name="moe_combine",
    )(x2d, y_pairs, y_pairs, info, gt)


def _moe_layer(x2d, g, sc, sh, gt, w_group, b_group, w_router, b_router, w1, w3, w2, layer, seq):
    t, d = x2d.shape
    w_cat = jnp.zeros((d, ROUTER_LANES), jnp.float32)
    w_cat = w_cat.at[:, :N_EXPERTS].set(w_router).at[:, N_EXPERTS:N_EXPERTS + N_GROUPS].set(w_group)
    b_cat = jnp.zeros((1, ROUTER_LANES), jnp.float32)
    b_cat = b_cat.at[0, :N_EXPERTS].set(b_router).at[0, N_EXPERTS:N_EXPERTS + N_GROUPS].set(b_group)
    w_hi, w_lo = _split_hi_lo(w_cat)
    hn, info, info_t, cnt = _router(x2d, g, sc, sh, w_hi, w_lo, b_cat, seq)

    counts = cnt[0, :N_EXPERTS].astype(jnp.int32)
    padded = (counts + MOE_BLOCK - 1) // MOE_BLOCK * MOE_BLOCK
    pad_ends = jnp.cumsum(padded)
    pad_starts = pad_ends - padded
    nb = -(-(2 * t) // MOE_BLOCK) + N_EXPERTS
    n_slots = nb * MOE_BLOCK
    it = info_t.astype(jnp.int32)
    onehot_start = lambda e: jnp.sum(
        jnp.where(e[None, :] == jnp.arange(N_EXPERTS, dtype=jnp.int32)[:, None],
                  pad_starts[:, None], 0), axis=0)
    dest1 = (onehot_start(it[0]) + it[2]).reshape(1, t)
    dest2 = (onehot_start(it[1]) + it[3]).reshape(1, t)
    lane = jnp.arange(MOE_BLOCK, dtype=jnp.int32)[None, :]
    n_padding = (padded - counts)[:, None]
    wrapped = (pad_starts + counts)[:, None] + lane % jnp.maximum(n_padding, 1)
    pad_idx = jnp.where(n_padding > 0, wrapped, n_slots + lane % SC_WINDOW).reshape(1, -1)
    experts = jnp.arange(N_EXPERTS, dtype=jnp.int32)
    blk = jnp.arange(nb, dtype=jnp.int32)
    blk_exp = jnp.minimum(
        jnp.sum((pad_ends[None, :] <= (blk * MOE_BLOCK)[:, None]).astype(jnp.int32), axis=1),
        N_EXPERTS - 1)
    n_used = pad_ends[-1] // MOE_BLOCK
    prev_exp = jnp.concatenate([jnp.full((1,), -1, jnp.int32), blk_exp[:-1]])
    is_first = (blk < n_used) & (blk_exp != prev_exp)
    blk_slot = (jnp.cumsum(is_first.astype(jnp.int32)) + 1) % 2
    later = (experts[None, :] > experts[:, None]) & (padded[None, :] > 0)
    nxt_of = jnp.min(jnp.where(later, experts[None, :], N_EXPERTS), axis=1)
    nxt_of = jnp.where(nxt_of == N_EXPERTS, -1, nxt_of)
    blk_nxt = jnp.sum(jnp.where(blk_exp[:, None] == experts[None, :], nxt_of[None, :], 0), axis=1)
    blk_meta = jnp.concatenate([blk_exp, blk_slot, blk_nxt, n_used[None]]).astype(jnp.int32)

    x_slots = _sc_dispatch(hn, dest1, dest2, pad_idx, n_slots)
    y_slots = _experts(x_slots, blk_meta, w1, w3, w2, layer, nb)
    y_pairs = _sc_gather(y_slots, jnp.concatenate([dest1, dest2], axis=1))
    return _combine(x2d, y_pairs, info, gt, seq)


def kernel(x, c, w_ada, b_ada, norm1_g, norm2_g, ml_w_in, ml_b_gate, ml_g_out, ml_w_out,
           sw_w_in, sw_g_q, sw_g_k, sw_sinks, sw_w_out, moe_w_group, moe_b_group,
           moe_w_router, moe_b_router, moe_w1, moe_w3, moe_w2):
    bsz, seq, d = x.shape
    depth = w_ada.shape[0]
    bf = jnp.bfloat16
    mod = _ada_mod(c, w_ada, b_ada)
    x2d = x.reshape(bsz * seq, d)
    for layer in range(depth):
        parts = [mod[layer, :, i * d:(i + 1) * d].reshape(bsz, 1, d) for i in range(6)]
        sh1, sc1, gt1, sh2, sc2, gt2 = parts
        j = layer // 2
        if layer % 2 == 0:
            w = ml_w_in[j]
            q_w, k_w = w[:, :ML_QK], w[:, ML_QK:2 * ML_QK]
            v_w = w[:, 2 * ML_QK:2 * ML_QK + ML_V]
            o_w = w[:, 2 * ML_QK + ML_V:2 * ML_QK + 2 * ML_V]
            g_w = w[:, 2 * ML_QK + 2 * ML_V:]
            w_main = jnp.concatenate([v_w, o_w, q_w], axis=1).astype(bf)
            wg_t = jnp.zeros((ML_GATE_ROWS, d), jnp.float32).at[:2 * ML_HEADS].set(g_w.T)
            wg_hi, wg_lo = _split_hi_lo(wg_t)
            main, k_t, g_t = _inproj(
                x2d, norm1_g[layer], sc1, sh1, w_main, seq,
                ml_extra=(k_w.T.astype(bf), wg_hi, wg_lo),
                q_cols=(2 * ML_V, 2 * ML_V + ML_QK), q_scale=ML_DK ** -0.5)
            a = _mlstm_core(main, k_t, g_t, ml_b_gate[j], ml_g_out[j], bsz, seq)
            x2d = _outproj_residual(a, ml_w_out[j].astype(bf), x2d, gt1, seq)
        else:
            w = sw_w_in[j]
            dq = SW_Q_HEADS * SW_DH
            dkv = SW_KV_HEADS * SW_DH
            dup = lambda m: jnp.concatenate(
                [m.reshape(d, SW_KV_HEADS, 1, SW_DH)] * 2, axis=2).reshape(d, 2 * dkv)
            w_main = jnp.concatenate(
                [w[:, :dq], dup(w[:, dq:dq + dkv]), dup(w[:, dq + dkv:])], axis=1).astype(bf)
            (proj,) = _inproj(x2d, norm1_g[layer], sc1, sh1, w_main, seq)
            a = _swa_core(proj, sw_sinks[j], sw_g_q[j], sw_g_k[j], bsz, seq)
            x2d = _outproj_residual(a, sw_w_out[j].astype(bf), x2d, gt1, seq)
        x2d = _moe_layer(x2d, norm2_g[layer], sc2, sh2, gt2, moe_w_group[layer],
                         moe_b_group[layer], moe_w_router[layer], moe_b_router[layer],
                         moe_w1, moe_w3, moe_w2, layer, seq)
    return x2d.reshape(bsz, seq, d)
```

```python
import functools

import jax
import jax.numpy as jnp
from jax import lax
from jax.experimental import pallas as pl
from jax.experimental.pallas import tpu as pltpu
from jax.experimental.pallas import tpu_sc as plsc

EPS = 1e-6
GATE_CAP = 15.0
LOG2E = 1.4426950408889634

ML_HEADS = 4
ML_DK = 128
ML_DV = 256
ML_QK = ML_HEADS * ML_DK
ML_V = ML_HEADS * ML_DV
ML_GATE_ROWS = 16

SW_Q_HEADS = 16
SW_KV_HEADS = 4
SW_GROUP = SW_Q_HEADS // SW_KV_HEADS
SW_DH = 64
SW_WINDOW = 128
LANES = 128

N_GROUPS = 8
EXPERTS_PER_GROUP = 8
N_EXPERTS = N_GROUPS * EXPERTS_PER_GROUP
MOE_BLOCK = 256
ROUTER_LANES = 128
SC_WINDOW = 128
SC_COLS = 256
N_PARTS = 2

VMEM_LIMIT = 56 * 1024 * 1024

_NT = (((1,), (1,)), ((), ()))


def _bdot(a, b):
    return jnp.dot(a, b, preferred_element_type=jnp.float32)


def _bdot_nt(a, b):
    return lax.dot_general(a, b, _NT, preferred_element_type=jnp.float32)


def _split_hi_lo(a):
    hi = a.astype(jnp.bfloat16)
    lo = (a - hi.astype(jnp.float32)).astype(jnp.bfloat16)
    return hi, lo


def _params(sem):
    return pltpu.CompilerParams(dimension_semantics=sem, vmem_limit_bytes=VMEM_LIMIT)


def _ada_kernel(c_ref, w_ref, b_ref, o_ref):
    c = c_ref[...]
    cond = c * jax.nn.sigmoid(c)
    c_hi, c_lo = _split_hi_lo(cond)
    w_hi, w_lo = _split_hi_lo(w_ref[...])
    acc = _bdot(c_hi, w_hi) + (_bdot(c_lo, w_hi) + _bdot(c_hi, w_lo))
    o_ref[...] = acc + b_ref[...]


def _ada_mod(c, w_ada, b_ada):
    depth, d, n = w_ada.shape
    bsz = c.shape[0]
    rows = 8
    tn = 768
    c_pad = jnp.zeros((rows, d), jnp.float32).at[:bsz].set(c)
    out = pl.pallas_call(
        _ada_kernel,
        grid=(depth, n // tn),
        in_specs=[
            pl.BlockSpec((rows, d), lambda l, j: (0, 0)),
            pl.BlockSpec((None, d, tn), lambda l, j: (l, 0, j)),
            pl.BlockSpec((None, 1, tn), lambda l, j: (l, 0, j)),
        ],
        out_specs=pl.BlockSpec((None, rows, tn), lambda l, j: (l, 0, j)),
        out_shape=jax.ShapeDtypeStruct((depth, rows, n), jnp.float32),
        compiler_params=_params(("arbitrary", "arbitrary")),
        name="ada_mod",
    )(c_pad, w_ada, b_ada.reshape(depth, 1, n))
    return out[:, :bsz]


def _modulated_norm(x, g, sc, sh):
    y = x * lax.rsqrt(jnp.mean(x * x, axis=-1, keepdims=True) + EPS)
    return y * (g * (1.0 + sc)) + sh


def _inproj_kernel(*refs, n_main, chunk, q_cols, q_scale, with_ml):
    if with_ml:
        (x_ref, g_ref, sc_ref, sh_ref, w_ref, wk_ref, wgh_ref, wgl_ref,
         o_ref, kt_ref, gt_ref) = refs
    else:
        x_ref, g_ref, sc_ref, sh_ref, w_ref, o_ref = refs
    hn = _modulated_norm(x_ref[...], g_ref[...], sc_ref[...], sh_ref[...])
    hb = hn.astype(jnp.bfloat16)
    for c0 in range(0, n_main, chunk):
        acc = _bdot(hb, w_ref[:, c0:c0 + chunk])
        if q_cols is not None and c0 >= q_cols[0]:
            acc = acc * q_scale
        o_ref[:, c0:c0 + chunk] = acc.astype(o_ref.dtype)
    if with_ml:
        kt_ref[...] = _bdot_nt(wk_ref[...], hb).astype(kt_ref.dtype)
        h_lo = (hn - hb.astype(jnp.float32)).astype(jnp.bfloat16)
        gt_ref[...] = (_bdot_nt(wgh_ref[...], hb)
                       + (_bdot_nt(wgh_ref[...], h_lo) + _bdot_nt(wgl_ref[...], hb)))


def _inproj(x2d, g, sc, sh, w_main, seq, *, ml_extra=None, q_cols=None, q_scale=1.0):
    t, d = x2d.shape
    tm = 512
    n_main = w_main.shape[1]
    tiles_per_batch = seq // tm
    row = lambda i: (i, 0)
    per_batch = lambda i: (i // tiles_per_batch, 0, 0)
    const = lambda i: (0, 0)
    in_specs = [
        pl.BlockSpec((tm, d), row),
        pl.BlockSpec((1, d), const),
        pl.BlockSpec((None, 1, d), per_batch),
        pl.BlockSpec((None, 1, d), per_batch),
        pl.BlockSpec((d, n_main), const),
    ]
    args = [x2d, g.reshape(1, d), sc, sh, w_main]
    out_specs = [pl.BlockSpec((tm, n_main), row)]
    out_shape = [jax.ShapeDtypeStruct((t, n_main), jnp.bfloat16)]
    if ml_extra is not None:
        wk_t, wg_hi, wg_lo = ml_extra
        in_specs += [pl.BlockSpec(wk_t.shape, const),
                     pl.BlockSpec(wg_hi.shape, const),
                     pl.BlockSpec(wg_lo.shape, const)]
        args += [wk_t, wg_hi, wg_lo]
        out_specs += [pl.BlockSpec((wk_t.shape[0], tm), lambda i: (0, i)),
                      pl.BlockSpec((ML_GATE_ROWS, tm), lambda i: (0, i))]
        out_shape += [jax.ShapeDtypeStruct((wk_t.shape[0], t), jnp.bfloat16),
                      jax.ShapeDtypeStruct((ML_GATE_ROWS, t), jnp.float32)]
    kern = functools.partial(_inproj_kernel, n_main=n_main, chunk=512, q_cols=q_cols,
                             q_scale=q_scale, with_ml=ml_extra is not None)
    return pl.pallas_call(
        kern,
        grid=(t // tm,),
        in_specs=in_specs,
        out_specs=out_specs,
        out_shape=out_shape,
        compiler_params=_params(("arbitrary",)),
        name="inproj_ml" if ml_extra is not None else "inproj_sw",
    )(*args)


def _mlstm_kernel(v_ref, o_ref, q_ref, kt_ref, gt_ref, bg_ref, gout_ref, out_ref,
                  c_ref, n_ref, m_ref, *, chunk):
    L = chunk
    H, dk, dv = ML_HEADS, ML_DK, ML_DV

    @pl.when(pl.program_id(1) == 0)
    def _():
        c_ref[...] = jnp.zeros_like(c_ref)
        n_ref[...] = jnp.zeros_like(n_ref)
        m_ref[...] = jnp.zeros_like(m_ref)

    z = gt_ref[...] + bg_ref[...]
    gates = GATE_CAP * jnp.tanh(z / GATE_CAP)
    log_f = jnp.minimum(gates, 0.0) - jnp.log1p(jnp.exp(-jnp.abs(gates)))
    grow = lax.broadcasted_iota(jnp.int32, (ML_GATE_ROWS, L), 0)
    is_i = grow < H
    slab = jnp.where(is_i, gates, log_f)

    r_idx = lax.broadcasted_iota(jnp.int32, (L, L), 0)
    c_idx = lax.broadcasted_iota(jnp.int32, (L, L), 1)
    upper = jnp.where(r_idx <= c_idx, 1.0, 0.0).astype(jnp.bfloat16)
    a1 = slab.astype(jnp.bfloat16)
    r1 = slab - a1.astype(jnp.float32)
    a2 = r1.astype(jnp.bfloat16)
    a3 = (r1 - a2.astype(jnp.float32)).astype(jnp.bfloat16)
    cum = _bdot(a1, upper) + (_bdot(a2, upper) + _bdot(a3, upper))
    rows = jnp.where(is_i, gates, cum)
    cols = jnp.concatenate(
        [rows, jnp.zeros((LANES - ML_GATE_ROWS, L), jnp.float32)], axis=0).T

    causal = r_idx >= c_idx
    for h in range(H):
        i_r = rows[h:h + 1, :]
        b_r = rows[H + h:H + h + 1, :]
        b_c = cols[:, H + h:H + h + 1]
        m_prev = m_ref[h][0:1, 0:1]
        n_row = n_ref[h][0:1, :]
        c_st = c_ref[h]
        q = q_ref[:, h * dk:(h + 1) * dk]
        kt = kt_ref[h * dk:(h + 1) * dk, :]
        v = v_ref[:, h * dv:(h + 1) * dv]

        u_r = i_r - b_r
        d_log = jnp.where(causal, b_c + u_r, -jnp.inf)
        inter = b_c + m_prev
        m_t = jnp.maximum(inter, jnp.max(d_log, axis=-1, keepdims=True))
        w_intra = jnp.exp(d_log - m_t)
        w_inter = jnp.exp(inter - m_t)
        s = _bdot(q, kt) * w_intra
        num = w_inter * _bdot(q, c_st.astype(jnp.bfloat16)) + _bdot(s.astype(jnp.bfloat16), v)
        qn = jnp.sum(q.astype(jnp.float32) * n_row, axis=-1, keepdims=True)
        den = w_inter * qn + jnp.sum(s, axis=-1, keepdims=True)
        hb = num * (1.0 / jnp.maximum(jnp.abs(den), jnp.exp(-m_t)))

        g_last = b_r[:, L - 1:L]
        a_r = g_last - b_r + i_r
        m_new = jnp.maximum(g_last + m_prev, jnp.max(a_r, axis=-1, keepdims=True))
        w_state = jnp.exp(a_r - m_new)
        decay = jnp.exp(g_last + m_prev - m_new)
        kw = (kt.astype(jnp.float32) * w_state).astype(jnp.bfloat16)
        c_ref[h] = decay * c_st + _bdot(kw, v)
        w16 = jnp.broadcast_to(w_state, (16, L)).astype(jnp.bfloat16)
        dn = _bdot_nt(w16, kt)
        n_ref[h] = jnp.broadcast_to(decay * n_row + dn[0:1, :], (8, dk))
        m_ref[h] = jnp.broadcast_to(m_new, (8, LANES))

        y = hb * lax.rsqrt(jnp.mean(hb * hb, axis=-1, keepdims=True) + EPS)
        y = y * gout_ref[:, h * dv:(h + 1) * dv]
        og = jax.nn.sigmoid(o_ref[:, h * dv:(h + 1) * dv].astype(jnp.float32))
        out_ref[:, h * dv:(h + 1) * dv] = (y * og).astype(out_ref.dtype)


def _mlstm_core(main, k_t, g_t, b_gate, g_out, bsz, seq, chunk=256):
    t = main.shape[0]
    nc = seq // chunk
    blk = lambda b, c: b * nc + c
    bg = jnp.zeros((ML_GATE_ROWS, 1), jnp.float32).at[:2 * ML_HEADS, 0].set(b_gate)
    return pl.pallas_call(
        functools.partial(_mlstm_kernel, chunk=chunk),
        grid=(bsz, nc),
        in_specs=[
            pl.BlockSpec((chunk, ML_V), lambda b, c: (blk(b, c), 0)),
            pl.BlockSpec((chunk, ML_V), lambda b, c: (blk(b, c), 1)),
            pl.BlockSpec((chunk, ML_QK), lambda b, c: (blk(b, c), 4)),
            pl.BlockSpec((ML_QK, chunk), lambda b, c: (0, blk(b, c))),
            pl.BlockSpec((ML_GATE_ROWS, chunk), lambda b, c: (0, blk(b, c))),
            pl.BlockSpec((ML_GATE_ROWS, 1), lambda b, c: (0, 0)),
            pl.BlockSpec((1, ML_V), lambda b, c: (0, 0)),
        ],
        out_specs=pl.BlockSpec((chunk, ML_V), lambda b, c: (blk(b, c), 0)),
        out_shape=jax.ShapeDtypeStruct((t, ML_V), jnp.bfloat16),
        scratch_shapes=[
            pltpu.VMEM((ML_HEADS, ML_DK, ML_DV), jnp.float32),
            pltpu.VMEM((ML_HEADS, 8, ML_DK), jnp.float32),
            pltpu.VMEM((ML_HEADS, 8, LANES), jnp.float32),
        ],
        compiler_params=_params(("arbitrary", "arbitrary")),
        name="mlstm_core",
    )(main, main, main, k_t, g_t, bg, g_out.reshape(1, ML_V))


def _outproj_kernel(a_ref, w_ref, x_ref, gt_ref, o_ref):
    o_ref[...] = x_ref[...] + gt_ref[...] * _bdot(a_ref[...], w_ref[...])


def _outproj_residual(a, w, x2d, gt, seq):
    t, d = x2d.shape
    tm = min(1024, seq)
    tiles_per_batch = seq // tm
    return pl.pallas_call(
        _outproj_kernel,
        grid=(t // tm,),
        in_specs=[
            pl.BlockSpec((tm, a.shape[1]), lambda i: (i, 0)),
            pl.BlockSpec(w.shape, lambda i: (0, 0)),
            pl.BlockSpec((tm, d), lambda i: (i, 0)),
            pl.BlockSpec((None, 1, d), lambda i: (i // tiles_per_batch, 0, 0)),
        ],
        out_specs=pl.BlockSpec((tm, d), lambda i: (i, 0)),
        out_shape=jax.ShapeDtypeStruct((t, d), jnp.float32),
        compiler_params=_params(("arbitrary",)),
        name="outproj_residual",
    )(a, w, x2d, gt)


def _swa_kernel(sinks_ref, q_ref, kc_ref, kp_ref, vc_ref, vp_ref, gq_ref, gk_ref, bias_ref, o_ref):
    W = SW_WINDOW
    lane = lax.broadcasted_iota(jnp.int32, (W, LANES), 1)
    low = lane < SW_DH
    row_col = lax.broadcasted_iota(jnp.int32, (SW_GROUP * W, 1), 0)
    gq = gq_ref[...]
    gk = gk_ref[...]
    for g in range(SW_KV_HEADS):
        sl = slice(g * LANES, (g + 1) * LANES)
        k2 = jnp.concatenate([kp_ref[:, sl], kc_ref[:, sl]], axis=0).astype(jnp.float32)
        kn = k2 * lax.rsqrt(jnp.mean(k2 * k2, axis=-1, keepdims=True) + EPS)
        kn = (kn * gk).astype(jnp.bfloat16)
        v2 = jnp.concatenate([vp_ref[:, sl], vc_ref[:, sl]], axis=0)
        parts = []
        for p in range(2):
            c0 = g * SW_GROUP * SW_DH + p * LANES
            qp = q_ref[:, c0:c0 + LANES].astype(jnp.float32)
            sq = qp * qp
            ss_lo = jnp.sum(jnp.where(low, sq, 0.0), axis=-1, keepdims=True)
            ss_hi = jnp.sum(jnp.where(low, 0.0, sq), axis=-1, keepdims=True)
            rs = jnp.where(low, lax.rsqrt(ss_lo / SW_DH + EPS), lax.rsqrt(ss_hi / SW_DH + EPS))
            qn = qp * rs * gq
            parts.append(jnp.where(low, qn, 0.0).astype(jnp.bfloat16))
            parts.append(jnp.where(low, 0.0, qn).astype(jnp.bfloat16))
        q4 = jnp.concatenate(parts, axis=0)
        scores = _bdot_nt(q4, kn) + bias_ref[...]
        sink = jnp.full((SW_GROUP * W, 1), sinks_ref[g * SW_GROUP + SW_GROUP - 1], jnp.float32)
        for j in range(SW_GROUP - 2, -1, -1):
            sink = jnp.where(row_col < (j + 1) * W, sinks_ref[g * SW_GROUP + j], sink)
        m = jnp.maximum(jnp.max(scores, axis=-1, keepdims=True), sink)
        pexp = jnp.exp2(scores - m)
        denom = jnp.sum(pexp, axis=-1, keepdims=True) + jnp.exp2(sink - m)
        o4 = _bdot(pexp.astype(jnp.bfloat16), v2) * (1.0 / denom)
        for p in range(2):
            oa = o4[(2 * p) * W:(2 * p + 1) * W]
            ob = o4[(2 * p + 1) * W:(2 * p + 2) * W]
            c0 = g * SW_GROUP * SW_DH + p * LANES
            o_ref[:, c0:c0 + LANES] = jnp.where(low, oa, ob).astype(o_ref.dtype)


def _swa_core(proj, sinks, g_q, g_k, bsz, seq):
    t = proj.shape[0]
    W = SW_WINDOW
    nb = seq // W
    dq = SW_Q_HEADS * SW_DH
    kv_w = SW_KV_HEADS * LANES
    k_blk = dq // kv_w
    v_blk = k_blk + 1
    cur = lambda b, n, s: b * nb + n
    prev = lambda b, n, s: b * nb + jnp.maximum(n - 1, 0)
    gq2 = jnp.concatenate([g_q, g_q]).reshape(1, LANES) * (SW_DH ** -0.5 * LOG2E)
    gk2 = jnp.concatenate([g_k, g_k]).reshape(1, LANES)
    sinks = sinks.astype(jnp.float32) * LOG2E
    qi = (jnp.arange(SW_GROUP * W) % W)[:, None]
    ki = jnp.arange(2 * W)[None, :]
    rel = qi + W - ki
    in_win = (rel >= 0) & (rel < W)
    bias = jnp.stack([jnp.where(in_win & (ki >= W), 0.0, -jnp.inf),
                      jnp.where(in_win, 0.0, -jnp.inf)]).astype(jnp.float32)
    grid_spec = pltpu.PrefetchScalarGridSpec(
        num_scalar_prefetch=1,
        grid=(bsz, nb),
        in_specs=[
            pl.BlockSpec((W, dq), lambda b, n, s: (cur(b, n, s), 0)),
            pl.BlockSpec((W, kv_w), lambda b, n, s: (cur(b, n, s), k_blk)),
            pl.BlockSpec((W, kv_w), lambda b, n, s: (prev(b, n, s), k_blk)),
            pl.BlockSpec((W, kv_w), lambda b, n, s: (cur(b, n, s), v_blk)),
            pl.BlockSpec((W, kv_w), lambda b, n, s: (prev(b, n, s), v_blk)),
            pl.BlockSpec((1, LANES), lambda b, n, s: (0, 0)),
            pl.BlockSpec((1, LANES), lambda b, n, s: (0, 0)),
            pl.BlockSpec((None, SW_GROUP * W, 2 * W), lambda b, n, s: (jnp.minimum(n, 1), 0, 0)),
        ],
        out_specs=pl.BlockSpec((W, dq), lambda b, n, s: (cur(b, n, s), 0)),
    )
    return pl.pallas_call(
        _swa_kernel,
        grid_spec=grid_spec,
        out_shape=jax.ShapeDtypeStruct((t, dq), jnp.bfloat16),
        compiler_params=_params(("arbitrary", "arbitrary")),
        name="swa_core",
    )(sinks.astype(jnp.float32), proj, proj, proj, proj, proj, gq2, gk2, bias)


def _pack_rounded_pairs(r):
    k = r.shape[1] // 2
    hi = lax.bitcast_convert_type(r[:, :k], jnp.uint32)
    lo = lax.bitcast_convert_type(r[:, k:], jnp.uint32)
    return hi | (lo >> 16)


def _pack_bf16_pairs(a):
    return _pack_rounded_pairs(a.astype(jnp.bfloat16).astype(jnp.float32))


def _unpack_bf16_pairs(u):
    hi = lax.bitcast_convert_type(u & jnp.uint32(0xFFFF0000), jnp.float32)
    lo = lax.bitcast_convert_type(u << 16, jnp.float32)
    return hi, lo


def _router_kernel(x_ref, g_ref, sc_ref, sh_ref, wh_ref, wl_ref, b_ref,
                   hn_ref, info_ref, infot_ref, cnt_ref, carry_ref, lower_ref, *, tm):
    @pl.when(pl.program_id(0) == 0)
    def _():
        carry_ref[...] = jnp.zeros_like(carry_ref)
        r_idx = lax.broadcasted_iota(jnp.int32, (tm, tm), 0)
        c_idx = lax.broadcasted_iota(jnp.int32, (tm, tm), 1)
        lower_ref[...] = jnp.where(c_idx < r_idx, 1.0, 0.0).astype(jnp.bfloat16)

    hn = _modulated_norm(x_ref[...], g_ref[...], sc_ref[...], sh_ref[...])
    h_hi = hn.astype(jnp.bfloat16)
    hi_f32 = h_hi.astype(jnp.float32)
    h_lo = (hn - hi_f32).astype(jnp.bfloat16)
    hn_ref[...] = _pack_rounded_pairs(hi_f32)
    wide = _bdot(h_hi, wl_ref[...])
    logits = (wide[:, :ROUTER_LANES] + (_bdot(h_lo, wh_ref[...]) + wide[:, ROUTER_LANES:])
              + b_ref[...])
    lane = lax.broadcasted_iota(jnp.int32, (tm, ROUTER_LANES), 1).astype(jnp.float32)
    big = float(4 * ROUTER_LANES)
    neg = -jnp.inf

    is_grp = (lane >= N_EXPERTS) & (lane < N_EXPERTS + N_GROUPS)
    gl = jnp.where(is_grp, logits, neg)
    gmax = jnp.max(gl, axis=-1, keepdims=True)
    gsel = jnp.min(jnp.where(gl == gmax, lane, big), axis=-1, keepdims=True) - N_EXPERTS
    p_grp = 1.0 / jnp.sum(jnp.exp(gl - gmax), axis=-1, keepdims=True)

    lo_lane = gsel * EXPERTS_PER_GROUP
    in_grp = (lane >= lo_lane) & (lane < lo_lane + EXPERTS_PER_GROUP)
    el = jnp.where(in_grp, logits, neg)
    v1 = jnp.max(el, axis=-1, keepdims=True)
    i1 = jnp.min(jnp.where(el == v1, lane, big), axis=-1, keepdims=True)
    el2 = jnp.where(lane == i1, neg, el)
    v2 = jnp.max(el2, axis=-1, keepdims=True)
    i2 = jnp.min(jnp.where(el2 == v2, lane, big), axis=-1, keepdims=True)
    e21 = jnp.exp(v2 - v1)
    gate1 = p_grp / (1.0 + e21)
    gate2 = p_grp * e21 / (1.0 + e21)

    hit1 = lane == i1
    hit2 = lane == i2
    onehot = jnp.where(hit1 | hit2, 1.0, 0.0)
    before = _bdot(lower_ref[...], onehot.astype(jnp.bfloat16)) + carry_ref[0:1, :]
    rank1 = jnp.sum(jnp.where(hit1, before, 0.0), axis=-1, keepdims=True)
    rank2 = jnp.sum(jnp.where(hit2, before, 0.0), axis=-1, keepdims=True)
    total = carry_ref[0:1, :] + jnp.sum(onehot, axis=0, keepdims=True)
    carry_ref[...] = jnp.broadcast_to(total, carry_ref.shape)
    cnt_ref[...] = jnp.broadcast_to(total, cnt_ref.shape)

    info = jnp.where(lane == 0, i1, 0.0)
    info = jnp.where(lane == 1, i2, info)
    info = jnp.where(lane == 2, rank1, info)
    info = jnp.where(lane == 3, rank2, info)
    info = jnp.where(lane == 4, gate1, info)
    info = jnp.where(lane == 5, gate2, info)
    info_ref[...] = info
    infot_ref[...] = info.T[0:8, :]


def _router(x2d, g, sc, sh, w_hi, w_lo, bias, seq):
    t, d = x2d.shape
    tm = 512
    tiles_per_batch = seq // tm
    per_batch = lambda i: (i // tiles_per_batch, 0, 0)
    const = lambda i: (0, 0)
    return pl.pallas_call(
        functools.partial(_router_kernel, tm=tm),
        grid=(t // tm,),
        in_specs=[
            pl.BlockSpec((tm, d), lambda i: (i, 0)),
            pl.BlockSpec((1, d), const),
            pl.BlockSpec((None, 1, d), per_batch),
            pl.BlockSpec((None, 1, d), per_batch),
            pl.BlockSpec((d, ROUTER_LANES), const),
            pl.BlockSpec((d, 2 * ROUTER_LANES), const),
            pl.BlockSpec((1, ROUTER_LANES), const),
        ],
        out_specs=[
            pl.BlockSpec((tm, d // 2), lambda i: (i, 0)),
            pl.BlockSpec((tm, ROUTER_LANES), lambda i: (i, 0)),
            pl.BlockSpec((8, tm), lambda i: (0, i)),
            pl.BlockSpec((8, ROUTER_LANES), const),
        ],
        out_shape=[
            jax.ShapeDtypeStruct((t, d // 2), jnp.uint32),
            jax.ShapeDtypeStruct((t, ROUTER_LANES), jnp.float32),
            jax.ShapeDtypeStruct((8, t), jnp.float32),
            jax.ShapeDtypeStruct((8, ROUTER_LANES), jnp.float32),
        ],
        scratch_shapes=[pltpu.VMEM((8, ROUTER_LANES), jnp.float32),
                        pltpu.VMEM((tm, tm), jnp.bfloat16)],
        compiler_params=_params(("arbitrary",)),
        name="moe_router",
    )(x2d, g.reshape(1, d), sc, sh, w_hi, jnp.concatenate([w_hi, w_lo], axis=1), bias)


def _sc_mesh():
    return plsc.VectorSubcoreMesh(core_axis_name="c", subcore_axis_name="s")


def _sc_dispatch(rows, d0, d1, pad_idx, n_slots):
    t, w = rows.shape
    n_pad = pad_idx.shape[1]
    zeros = jnp.zeros((SC_WINDOW, w), rows.dtype)
    sem = (pltpu.PARALLEL, pltpu.ARBITRARY)

    @pl.kernel(out_type=jax.ShapeDtypeStruct((n_slots + SC_WINDOW, w), rows.dtype), mesh=_sc_mesh())
    def dispatch(x_hbm, d0_hbm, d1_hbm, z_hbm, p_hbm, o_hbm):
        def scatter_rows(x_vmem, i0_vmem, i1_vmem):
            cols = pl.ds(pl.program_id(1) * SC_COLS, SC_COLS)
            pltpu.sync_copy(x_vmem, o_hbm.at[i0_vmem.at[0], cols])
            pltpu.sync_copy(x_vmem, o_hbm.at[i1_vmem.at[0], cols])

        pltpu.emit_pipeline(
            scatter_rows,
            grid=(t // SC_WINDOW, w // SC_COLS),
            in_specs=[pl.BlockSpec((SC_WINDOW, SC_COLS), lambda i, j: (i, j)),
                      pl.BlockSpec((1, SC_WINDOW), lambda i, j: (0, i)),
                      pl.BlockSpec((1, SC_WINDOW), lambda i, j: (0, i))],
            out_specs=[],
            core_axis_name=("c", "s"),
            dimension_semantics=sem,
        )(x_hbm, d0_hbm, d1_hbm)

        def scatter_zeros(z_vmem, p_vmem):
            cols = pl.ds(pl.program_id(1) * SC_COLS, SC_COLS)
            pltpu.sync_copy(z_vmem, o_hbm.at[p_vmem.at[0], cols])

        pltpu.emit_pipeline(
            scatter_zeros,
            grid=(n_pad // SC_WINDOW, w // SC_COLS),
            in_specs=[pl.BlockSpec((SC_WINDOW, SC_COLS), lambda i, j: (0, j)),
                      pl.BlockSpec((1, SC_WINDOW), lambda i, j: (0, i))],
            out_specs=[],
            core_axis_name=("c", "s"),
            dimension_semantics=sem,
        )(z_hbm, p_hbm)

    return dispatch(rows, d0, d1, zeros, pad_idx)


def _sc_gather(src, idx):
    n_out = idx.shape[1]
    w = src.shape[1]

    @pl.kernel(out_type=jax.ShapeDtypeStruct((n_out, w), src.dtype), mesh=_sc_mesh())
    def gather(x_hbm, i_hbm, o_hbm):
        def gather_rows(i_vmem, o_vmem):
            cols = pl.ds(pl.program_id(1) * SC_COLS, SC_COLS)
            pltpu.sync_copy(x_hbm.at[i_vmem.at[0], cols], o_vmem)

        pltpu.emit_pipeline(
            gather_rows,
            grid=(n_out // SC_WINDOW, w // SC_COLS),
            in_specs=[pl.BlockSpec((1, SC_WINDOW), lambda i, j: (0, i))],
            out_specs=[pl.BlockSpec((SC_WINDOW, SC_COLS), lambda i, j: (i, j))],
            core_axis_name=("c", "s"),
            dimension_semantics=(pltpu.PARALLEL, pltpu.ARBITRARY),
        )(i_hbm, o_hbm)

    return gather(src, idx)


def _expert_kernel(meta_ref, x_ref, w1_hbm, w3_hbm, w2_hbm, y_ref,
                   w1_buf, w3_buf, w2_buf, w1_c, w3_c, w2_c, sems, *, layer, nb):
    i = pl.program_id(0)
    e = meta_ref[i]
    slot = meta_ref[nb + i]
    nxt = meta_ref[2 * nb + i]
    used = i < meta_ref[3 * nb]
    first = used & ((i == 0) | (e != meta_ref[jnp.maximum(i - 1, 0)]))

    def weight_copies(expert, s):
        return (pltpu.make_async_copy(w1_hbm.at[layer, expert], w1_buf.at[s], sems.at[s, 0]),
                pltpu.make_async_copy(w3_hbm.at[layer, expert], w3_buf.at[s], sems.at[s, 1]),
                pltpu.make_async_copy(w2_hbm.at[layer, expert], w2_buf.at[s], sems.at[s, 2]))

    @pl.when(i == 0)
    def _():
        for cp in weight_copies(e, slot):
            cp.start()

    @pl.when(first)
    def _():
        for cp in weight_copies(e, slot):
            cp.wait()

        @pl.when(nxt >= 0)
        def _():
            for cp in weight_copies(nxt, 1 - slot):
                cp.start()

        w1_c[...] = w1_buf[slot].astype(jnp.bfloat16)
        w3_c[...] = w3_buf[slot].astype(jnp.bfloat16)
        w2_c[...] = w2_buf[slot].astype(jnp.bfloat16)

    @pl.when(used)
    def _():
        x_hi, x_lo = _unpack_bf16_pairs(x_ref[...])
        xb = jnp.concatenate([x_hi, x_lo], axis=-1).astype(jnp.bfloat16)
        h1 = _bdot(xb, w1_c[...])
        h3 = _bdot(xb, w3_c[...])
        act = (h1 * jax.nn.sigmoid(h1) * h3).astype(jnp.bfloat16)
        y_ref[...] = _pack_bf16_pairs(_bdot(act, w2_c[...]))

    @pl.when(jnp.logical_not(used))
    def _():
        y_ref[...] = jnp.zeros_like(y_ref)


def _experts(x_slots, blk_meta, w1, w3, w2, layer, nb):
    dp = x_slots.shape[1]
    d, de = w1.shape[-2:]
    rowblk = lambda i, s: (jnp.minimum(i, s[3 * nb] - 1), 0)
    grid_spec = pltpu.PrefetchScalarGridSpec(
        num_scalar_prefetch=1,
        grid=(nb,),
        in_specs=[
            pl.BlockSpec((MOE_BLOCK, dp), rowblk),
            pl.BlockSpec(memory_space=pl.ANY),
            pl.BlockSpec(memory_space=pl.ANY),
            pl.BlockSpec(memory_space=pl.ANY),
        ],
        out_specs=pl.BlockSpec((MOE_BLOCK, dp), lambda i, s: (i, 0)),
        scratch_shapes=[
            pltpu.VMEM((2, d, de), jnp.float32),
            pltpu.VMEM((2, d, de), jnp.float32),
            pltpu.VMEM((2, de, d), jnp.float32),
            pltpu.VMEM((d, de), jnp.bfloat16),
            pltpu.VMEM((d, de), jnp.bfloat16),
            pltpu.VMEM((de, d), jnp.bfloat16),
            pltpu.SemaphoreType.DMA((2, 3)),
        ],
    )
    return pl.pallas_call(
        functools.partial(_expert_kernel, layer=layer, nb=nb),
        grid_spec=grid_spec,
        out_shape=jax.ShapeDtypeStruct((nb * MOE_BLOCK, dp), jnp.uint32),
        compiler_params=_params(("arbitrary",)),
        name="moe_experts",
    )(blk_meta, x_slots, w1, w3, w2)


def _combine_kernel(x_ref, y1_ref, y2_ref, info_ref, gt_ref, o_ref):
    k = x_ref.shape[1] // 2
    info = info_ref[...]
    g1 = info[:, 4:5]
    g2 = info[:, 5:6]
    y1_hi, y1_lo = _unpack_bf16_pairs(y1_ref[...])
    y2_hi, y2_lo = _unpack_bf16_pairs(y2_ref[...])
    o_ref[:, :k] = x_ref[:, :k] + gt_ref[:, :k] * (g1 * y1_hi + g2 * y2_hi)
    o_ref[:, k:] = x_ref[:, k:] + gt_ref[:, k:] * (g1 * y1_lo + g2 * y2_lo)


def _combine(x2d, y_pairs, info, gt, seq):
    t, d = x2d.shape
    tm = min(1024, seq)
    tiles_per_batch = seq // tm
    second = t // tm
    return pl.pallas_call(
        _combine_kernel,
        grid=(t // tm,),
        in_specs=[
            pl.BlockSpec((tm, d), lambda i: (i, 0)),
            pl.BlockSpec((tm, d // 2), lambda i: (i, 0)),
            pl.BlockSpec((tm, d // 2), lambda i: (i + second, 0)),
            pl.BlockSpec((tm, ROUTER_LANES), lambda i: (i, 0)),
            pl.BlockSpec((None, 1, d), lambda i: (i // tiles_per_batch, 0, 0)),
        ],
        out_specs=pl.BlockSpec((tm, d), lambda i: (i, 0)),
        out_shape=jax.ShapeDtypeStruct((t, d), jnp.float32),
        compiler_params=_params(("arbitrary",)),
        name="moe_combine",
    )(x2d, y_pairs, y_pairs, info, gt)


def _slot_plan(info_t, cnt, t):
    counts = cnt[0, :N_EXPERTS].astype(jnp.int32)
    padded = (counts + MOE_BLOCK - 1) // MOE_BLOCK * MOE_BLOCK
    pad_ends = jnp.cumsum(padded)
    pad_starts = pad_ends - padded
    nb = -(-(2 * t) // MOE_BLOCK) + N_EXPERTS
    n_slots = nb * MOE_BLOCK
    it = info_t.astype(jnp.int32)
    onehot_start = lambda e: jnp.sum(
        jnp.where(e[None, :] == jnp.arange(N_EXPERTS, dtype=jnp.int32)[:, None],
                  pad_starts[:, None], 0), axis=0)
    dest1 = (onehot_start(it[0]) + it[2]).reshape(1, t)
    dest2 = (onehot_start(it[1]) + it[3]).reshape(1, t)
    lane = jnp.arange(MOE_BLOCK, dtype=jnp.int32)[None, :]
    n_padding = (padded - counts)[:, None]
    wrapped = (pad_starts + counts)[:, None] + lane % jnp.maximum(n_padding, 1)
    pad_idx = jnp.where(n_padding > 0, wrapped, n_slots + lane % SC_WINDOW).reshape(1, -1)
    experts = jnp.arange(N_EXPERTS, dtype=jnp.int32)
    blk = jnp.arange(nb, dtype=jnp.int32)
    blk_exp = jnp.minimum(
        jnp.sum((pad_ends[None, :] <= (blk * MOE_BLOCK)[:, None]).astype(jnp.int32), axis=1),
        N_EXPERTS - 1)
    n_used = pad_ends[-1] // MOE_BLOCK
    prev_exp = jnp.concatenate([jnp.full((1,), -1, jnp.int32), blk_exp[:-1]])
    is_first = (blk < n_used) & (blk_exp != prev_exp)
    blk_slot = (jnp.cumsum(is_first.astype(jnp.int32)) + 1) % 2
    later = (experts[None, :] > experts[:, None]) & (padded[None, :] > 0)
    nxt_of = jnp.min(jnp.where(later, experts[None, :], N_EXPERTS), axis=1)
    nxt_of = jnp.where(nxt_of == N_EXPERTS, -1, nxt_of)
    blk_nxt = jnp.sum(jnp.where(blk_exp[:, None] == experts[None, :], nxt_of[None, :], 0), axis=1)
    blk_meta = jnp.concatenate([blk_exp, blk_slot, blk_nxt, n_used[None]]).astype(jnp.int32)
    return dest1, dest2, pad_idx, blk_meta, nb, n_slots


def _moe_layer(xs, g, scs, shs, gts, w_group, b_group, w_router, b_router, w1, w3, w2, layer, seq):
    d = xs[0].shape[1]
    w_cat = jnp.zeros((d, ROUTER_LANES), jnp.float32)
    w_cat = w_cat.at[:, :N_EXPERTS].set(w_router).at[:, N_EXPERTS:N_EXPERTS + N_GROUPS].set(w_group)
    b_cat = jnp.zeros((1, ROUTER_LANES), jnp.float32)
    b_cat = b_cat.at[0, :N_EXPERTS].set(b_router).at[0, N_EXPERTS:N_EXPERTS + N_GROUPS].set(b_group)
    w_hi, w_lo = _split_hi_lo(w_cat)
    routed = [_router(x, g, sc, sh, w_hi, w_lo, b_cat, seq) for x, sc, sh in zip(xs, scs, shs)]
    plans = [_slot_plan(info_t, cnt, x.shape[0]) for x, (_, _, info_t, cnt) in zip(xs, routed)]
    x_slots = [_sc_dispatch(hn, d1, d2, pad_idx, n_slots)
               for (hn, _, _, _), (d1, d2, pad_idx, _, _, n_slots) in zip(routed, plans)]
    y_slots = [_experts(xsl, meta, w1, w3, w2, layer, nb)
               for xsl, (_, _, _, meta, nb, _) in zip(x_slots, plans)]
    y_pairs = [_sc_gather(ysl, jnp.concatenate([d1, d2], axis=1))
               for ysl, (d1, d2, _, _, _, _) in zip(y_slots, plans)]
    return [_combine(x, yp, info, gt, seq)
            for x, yp, (_, info, _, _), gt in zip(xs, y_pairs, routed, gts)]


def kernel(x, c, w_ada, b_ada, norm1_g, norm2_g, ml_w_in, ml_b_gate, ml_g_out, ml_w_out,
           sw_w_in, sw_g_q, sw_g_k, sw_sinks, sw_w_out, moe_w_group, moe_b_group,
           moe_w_router, moe_b_router, moe_w1, moe_w3, moe_w2):
    bsz, seq, d = x.shape
    depth = w_ada.shape[0]
    bf = jnp.bfloat16
    mod = _ada_mod(c, w_ada, b_ada)
    n_parts = N_PARTS if bsz % N_PARTS == 0 else 1
    pb = bsz // n_parts
    xs = [x[p * pb:(p + 1) * pb].reshape(pb * seq, d) for p in range(n_parts)]
    per_part = lambda m: [m[p * pb:(p + 1) * pb] for p in range(n_parts)]
    for layer in range(depth):
        sh1, sc1, gt1, sh2, sc2, gt2 = [
            per_part(mod[layer, :, i * d:(i + 1) * d].reshape(bsz, 1, d)) for i in range(6)]
        j = layer // 2
        if layer % 2 == 0:
            w = ml_w_in[j]
            q_w, k_w = w[:, :ML_QK], w[:, ML_QK:2 * ML_QK]
            v_w = w[:, 2 * ML_QK:2 * ML_QK + ML_V]
            o_w = w[:, 2 * ML_QK + ML_V:2 * ML_QK + 2 * ML_V]
            g_w = w[:, 2 * ML_QK + 2 * ML_V:]
            w_main = jnp.concatenate([v_w, o_w, q_w], axis=1).astype(bf)
            wg_t = jnp.zeros((ML_GATE_ROWS, d), jnp.float32).at[:2 * ML_HEADS].set(g_w.T)
            wg_hi, wg_lo = _split_hi_lo(wg_t)
            wk_t = k_w.T.astype(bf)
            w_out = ml_w_out[j].astype(bf)
            proj = [_inproj(xp, norm1_g[layer], sc, sh, w_main, seq, ml_extra=(wk_t, wg_hi, wg_lo),
                            q_cols=(2 * ML_V, 2 * ML_V + ML_QK), q_scale=ML_DK ** -0.5)
                    for xp, sc, sh in zip(xs, sc1, sh1)]
            mixed = [_mlstm_core(main, k_t, g_t, ml_b_gate[j], ml_g_out[j], pb, seq)
                     for main, k_t, g_t in proj]
            xs = [_outproj_residual(a, w_out, xp, gt, seq) for a, xp, gt in zip(mixed, xs, gt1)]
        else:
            w = sw_w_in[j]
            dq = SW_Q_HEADS * SW_DH
            dkv = SW_KV_HEADS * SW_DH
            dup = lambda m: jnp.concatenate(
                [m.reshape(d, SW_KV_HEADS, 1, SW_DH)] * 2, axis=2).reshape(d, 2 * dkv)
            w_main = jnp.concatenate(
                [w[:, :dq], dup(w[:, dq:dq + dkv]), dup(w[:, dq + dkv:])], axis=1).astype(bf)
            w_out = sw_w_out[j].astype(bf)
            proj = [_inproj(xp, norm1_g[layer], sc, sh, w_main, seq)[0]
                    for xp, sc, sh in zip(xs, sc1, sh1)]
            mixed = [_swa_core(pr, sw_sinks[j], sw_g_q[j], sw_g_k[j], pb, seq) for pr in proj]
            xs = [_outproj_residual(a, w_out, xp, gt, seq) for a, xp, gt in zip(mixed, xs, gt1)]
        xs = _moe_layer(xs, norm2_g[layer], sc2, sh2, gt2, moe_w_group[layer],
                        moe_b_group[layer], moe_w_router[layer], moe_b_router[layer],
                        moe_w1, moe_w3, moe_w2, layer, seq)
    return jnp.concatenate(xs, axis=0).reshape(bsz, seq, d)
```

```python
import functools

import jax
import jax.numpy as jnp
from jax import lax
from jax.experimental import pallas as pl
from jax.experimental.pallas import tpu as pltpu
from jax.experimental.pallas import tpu_sc as plsc

EPS = 1e-6
GATE_CAP = 15.0
LOG2E = 1.4426950408889634

ML_HEADS = 4
ML_DK = 128
ML_DV = 256
ML_QK = ML_HEADS * ML_DK
ML_V = ML_HEADS * ML_DV
ML_GATE_ROWS = 16

SW_Q_HEADS = 16
SW_KV_HEADS = 4
SW_GROUP = SW_Q_HEADS // SW_KV_HEADS
SW_DH = 64
SW_WINDOW = 128
LANES = 128

N_GROUPS = 8
EXPERTS_PER_GROUP = 8
N_EXPERTS = N_GROUPS * EXPERTS_PER_GROUP
MOE_BLOCK = 256
ROUTER_LANES = 128
SC_WINDOW = 128
SC_COLS = 256
N_PARTS = 1

VMEM_LIMIT = 56 * 1024 * 1024

_NT = (((1,), (1,)), ((), ()))


def _bdot(a, b):
    return jnp.dot(a, b, preferred_element_type=jnp.float32)


def _bdot_nt(a, b):
    return lax.dot_general(a, b, _NT, preferred_element_type=jnp.float32)


def _split_hi_lo(a):
    hi = a.astype(jnp.bfloat16)
    lo = (a - hi.astype(jnp.float32)).astype(jnp.bfloat16)
    return hi, lo


def _params(sem):
    return pltpu.CompilerParams(dimension_semantics=sem, vmem_limit_bytes=VMEM_LIMIT)


def _ada_kernel(c_ref, w_ref, b_ref, o_ref):
    c = c_ref[...]
    cond = c * jax.nn.sigmoid(c)
    c_hi, c_lo = _split_hi_lo(cond)
    w_hi, w_lo = _split_hi_lo(w_ref[...])
    acc = _bdot(c_hi, w_hi) + (_bdot(c_lo, w_hi) + _bdot(c_hi, w_lo))
    o_ref[...] = acc + b_ref[...]


def _ada_mod(c, w_ada, b_ada):
    depth, d, n = w_ada.shape
    bsz = c.shape[0]
    rows = 8
    tn = 768
    c_pad = jnp.zeros((rows, d), jnp.float32).at[:bsz].set(c)
    out = pl.pallas_call(
        _ada_kernel,
        grid=(depth, n // tn),
        in_specs=[
            pl.BlockSpec((rows, d), lambda l, j: (0, 0)),
            pl.BlockSpec((None, d, tn), lambda l, j: (l, 0, j)),
            pl.BlockSpec((None, 1, tn), lambda l, j: (l, 0, j)),
        ],
        out_specs=pl.BlockSpec((None, rows, tn), lambda l, j: (l, 0, j)),
        out_shape=jax.ShapeDtypeStruct((depth, rows, n), jnp.float32),
        compiler_params=_params(("arbitrary", "arbitrary")),
        name="ada_mod",
    )(c_pad, w_ada, b_ada.reshape(depth, 1, n))
    return out[:, :bsz]


def _modulated_norm(x, g, sc, sh):
    y = x * lax.rsqrt(jnp.mean(x * x, axis=-1, keepdims=True) + EPS)
    return y * (g * (1.0 + sc)) + sh


def _inproj_kernel(*refs, n_main, chunk, q_cols, q_scale, gate_cols, with_ml):
    if with_ml:
        (x_ref, g_ref, sc_ref, sh_ref, w_ref, wk_ref, wgh_ref, wgl_ref,
         o_ref, kt_ref, gt_ref) = refs
    else:
        x_ref, g_ref, sc_ref, sh_ref, w_ref, o_ref = refs
    hn = _modulated_norm(x_ref[...], g_ref[...], sc_ref[...], sh_ref[...])
    hb = hn.astype(jnp.bfloat16)
    for c0 in range(0, n_main, chunk):
        acc = _bdot(hb, w_ref[:, c0:c0 + chunk])
        if q_cols is not None and q_cols[0] <= c0 < q_cols[1]:
            acc = acc * q_scale
        if gate_cols is not None and gate_cols[0] <= c0 < gate_cols[1]:
            acc = jax.nn.sigmoid(acc)
        o_ref[:, c0:c0 + chunk] = acc.astype(o_ref.dtype)
    if with_ml:
        kt_ref[...] = _bdot_nt(wk_ref[...], hb).astype(kt_ref.dtype)
        h_lo = (hn - hb.astype(jnp.float32)).astype(jnp.bfloat16)
        gt_ref[...] = (_bdot_nt(wgh_ref[...], hb)
                       + (_bdot_nt(wgh_ref[...], h_lo) + _bdot_nt(wgl_ref[...], hb)))


def _inproj(x2d, g, sc, sh, w_main, seq, *, ml_extra=None, q_cols=None, q_scale=1.0,
            gate_cols=None):
    t, d = x2d.shape
    tm = 512
    n_main = w_main.shape[1]
    tiles_per_batch = seq // tm
    row = lambda i: (i, 0)
    per_batch = lambda i: (i // tiles_per_batch, 0, 0)
    const = lambda i: (0, 0)
    in_specs = [
        pl.BlockSpec((tm, d), row),
        pl.BlockSpec((1, d), const),
        pl.BlockSpec((None, 1, d), per_batch),
        pl.BlockSpec((None, 1, d), per_batch),
        pl.BlockSpec((d, n_main), const),
    ]
    args = [x2d, g.reshape(1, d), sc, sh, w_main]
    out_specs = [pl.BlockSpec((tm, n_main), row)]
    out_shape = [jax.ShapeDtypeStruct((t, n_main), jnp.bfloat16)]
    if ml_extra is not None:
        wk_t, wg_hi, wg_lo = ml_extra
        in_specs += [pl.BlockSpec(wk_t.shape, const),
                     pl.BlockSpec(wg_hi.shape, const),
                     pl.BlockSpec(wg_lo.shape, const)]
        args += [wk_t, wg_hi, wg_lo]
        out_specs += [pl.BlockSpec((wk_t.shape[0], tm), lambda i: (0, i)),
                      pl.BlockSpec((ML_GATE_ROWS, tm), lambda i: (0, i))]
        out_shape += [jax.ShapeDtypeStruct((wk_t.shape[0], t), jnp.bfloat16),
                      jax.ShapeDtypeStruct((ML_GATE_ROWS, t), jnp.float32)]
    kern = functools.partial(_inproj_kernel, n_main=n_main, chunk=512, q_cols=q_cols,
                             q_scale=q_scale, gate_cols=gate_cols, with_ml=ml_extra is not None)
    return pl.pallas_call(
        kern,
        grid=(t // tm,),
        in_specs=in_specs,
        out_specs=out_specs,
        out_shape=out_shape,
        compiler_params=_params(("arbitrary",)),
        name="inproj_ml" if ml_extra is not None else "inproj_sw",
    )(*args)


def _mlstm_gate_terms(graw, bias, upper):
    H = ML_HEADS
    L = graw.shape[1]
    z = graw + bias
    gates = GATE_CAP * jnp.tanh(z / GATE_CAP)
    log_f = jnp.minimum(gates, 0.0) - jnp.log1p(jnp.exp(-jnp.abs(gates)))
    row = lax.broadcasted_iota(jnp.int32, (ML_GATE_ROWS, L), 0)
    lane = lax.broadcasted_iota(jnp.int32, (ML_GATE_ROWS, L), 1)
    is_i = row < H
    slab = jnp.where(is_i, gates, log_f)
    a1 = slab.astype(jnp.bfloat16)
    r1 = slab - a1.astype(jnp.float32)
    a2 = r1.astype(jnp.bfloat16)
    a3 = (r1 - a2.astype(jnp.float32)).astype(jnp.bfloat16)
    cum = _bdot(a1, upper) + (_bdot(a2, upper) + _bdot(a3, upper))
    ib = jnp.where(is_i, gates, cum)
    b = pltpu.roll(ib, ML_GATE_ROWS - H, 0)
    u = ib - b
    cm = u
    shift = 1
    while shift < L:
        cm = jnp.maximum(cm, jnp.where(lane >= shift, pltpu.roll(cm, shift, 1), -jnp.inf))
        shift *= 2
    return b, u, cm


def _mlstm_kernel(v_ref, o_ref, q_ref, kt_ref, gt_ref, gtn_ref, bg_ref, gout_ref, out_ref,
                  c_ref, m_ref, b_ref, u_ref, cm_ref, *, chunk):
    L = chunk
    H, dk, dv = ML_HEADS, ML_DK, ML_DV
    r_idx = lax.broadcasted_iota(jnp.int32, (L, L), 0)
    c_idx = lax.broadcasted_iota(jnp.int32, (L, L), 1)
    upper = jnp.where(r_idx <= c_idx, 1.0, 0.0).astype(jnp.bfloat16)
    causal = r_idx >= c_idx
    row = lax.broadcasted_iota(jnp.int32, (ML_GATE_ROWS, L), 0)
    ones_col = jnp.where(lax.broadcasted_iota(jnp.int32, (L, LANES), 1) == 0, 1.0, 0.0
                         ).astype(jnp.bfloat16)

    @pl.when(pl.program_id(1) == 0)
    def _():
        c_ref[...] = jnp.zeros_like(c_ref)
        m_ref[...] = jnp.zeros_like(m_ref)
        b0, u0, cm0 = _mlstm_gate_terms(gt_ref[...], bg_ref[...], upper)
        b_ref[...] = b0
        u_ref[...] = u0
        cm_ref[...] = cm0

    b16 = b_ref[...]
    u16 = u_ref[...]
    cm16 = cm_ref[...]
    b_n, u_n, cm_n = _mlstm_gate_terms(gtn_ref[...], bg_ref[...], upper)
    b_ref[...] = b_n
    u_ref[...] = u_n
    cm_ref[...] = cm_n

    m_prev = m_ref[:, 0:1]
    z16 = jnp.maximum(m_prev, cm16)
    w_inter16 = jnp.exp(m_prev - z16)
    e_negm16 = jnp.exp(-(b16 + z16))
    z_last = z16[:, L - 1:L]
    w_state16 = jnp.exp(u16 - z_last)
    decay16 = jnp.exp(m_prev - z_last)
    m_ref[...] = jnp.broadcast_to(b16[:, L - 1:L] + z_last, m_ref.shape)
    stacked = jnp.where(row < H, z16,
                        jnp.where(row < 2 * H, pltpu.roll(w_inter16, H, 0),
                                  pltpu.roll(e_negm16, 2 * H, 0)))
    cols = jnp.concatenate(
        [stacked, jnp.zeros((LANES - ML_GATE_ROWS, L), jnp.float32)], axis=0).T

    for h in range(H):
        u_r = u16[h:h + 1, :]
        z_c = cols[:, h:h + 1]
        w_inter = cols[:, H + h:H + h + 1]
        e_negm = cols[:, 2 * H + h:2 * H + h + 1]
        c_ext = c_ref[h]
        q = q_ref[:, h * dk:(h + 1) * dk]
        kt = kt_ref[h * dk:(h + 1) * dk, :]
        v_ext = jnp.concatenate([v_ref[:, h * dv:(h + 1) * dv], ones_col], axis=-1)

        w_intra = jnp.exp(jnp.where(causal, u_r - z_c, -jnp.inf))
        s = (_bdot(q, kt) * w_intra).astype(jnp.bfloat16)
        nd = w_inter * _bdot(q, c_ext.astype(jnp.bfloat16)) + _bdot(s, v_ext)
        den = nd[:, dv:dv + 1]
        hb = nd[:, :dv] * (1.0 / jnp.maximum(jnp.abs(den), e_negm))

        kw = (kt.astype(jnp.float32) * w_state16[h:h + 1, :]).astype(jnp.bfloat16)
        c_ref[h] = decay16[h:h + 1, :] * c_ext + _bdot(kw, v_ext)

        y = hb * lax.rsqrt(jnp.mean(hb * hb, axis=-1, keepdims=True) + EPS)
        y = y * gout_ref[:, h * dv:(h + 1) * dv]
        og = o_ref[:, h * dv:(h + 1) * dv].astype(jnp.float32)
        out_ref[:, h * dv:(h + 1) * dv] = (y * og).astype(out_ref.dtype)


def _mlstm_core(main, k_t, g_t, b_gate, g_out, bsz, seq, chunk=256):
    t = main.shape[0]
    nc = seq // chunk
    blk = lambda b, c: b * nc + c
    bg = jnp.zeros((ML_GATE_ROWS, 1), jnp.float32).at[:2 * ML_HEADS, 0].set(b_gate)
    return pl.pallas_call(
        functools.partial(_mlstm_kernel, chunk=chunk),
        grid=(bsz, nc),
        in_specs=[
            pl.BlockSpec((chunk, ML_V), lambda b, c: (blk(b, c), 0)),
            pl.BlockSpec((chunk, ML_V), lambda b, c: (blk(b, c), 1)),
            pl.BlockSpec((chunk, ML_QK), lambda b, c: (blk(b, c), 4)),
            pl.BlockSpec((ML_QK, chunk), lambda b, c: (0, blk(b, c))),
            pl.BlockSpec((ML_GATE_ROWS, chunk), lambda b, c: (0, blk(b, c))),
            pl.BlockSpec((ML_GATE_ROWS, chunk), lambda b, c: (0, blk(b, jnp.minimum(c + 1, nc - 1)))),
            pl.BlockSpec((ML_GATE_ROWS, 1), lambda b, c: (0, 0)),
            pl.BlockSpec((1, ML_V), lambda b, c: (0, 0)),
        ],
        out_specs=pl.BlockSpec((chunk, ML_V), lambda b, c: (blk(b, c), 0)),
        out_shape=jax.ShapeDtypeStruct((t, ML_V), jnp.bfloat16),
        scratch_shapes=[
            pltpu.VMEM((ML_HEADS, ML_DK, ML_DV + LANES), jnp.float32),
            pltpu.VMEM((ML_GATE_ROWS, LANES), jnp.float32),
            pltpu.VMEM((ML_GATE_ROWS, chunk), jnp.float32),
            pltpu.VMEM((ML_GATE_ROWS, chunk), jnp.float32),
            pltpu.VMEM((ML_GATE_ROWS, chunk), jnp.float32),
        ],
        compiler_params=_params(("arbitrary", "arbitrary")),
        name="mlstm_core",
    )(main, main, main, k_t, g_t, g_t, bg, g_out.reshape(1, ML_V))


def _outproj_kernel(a_ref, w_ref, x_ref, gt_ref, o_ref):
    o_ref[...] = x_ref[...] + gt_ref[...] * _bdot(a_ref[...], w_ref[...])


def _outproj_residual(a, w, x2d, gt, seq):
    t, d = x2d.shape
    tm = min(1024, seq)
    tiles_per_batch = seq // tm
    return pl.pallas_call(
        _outproj_kernel,
        grid=(t // tm,),
        in_specs=[
            pl.BlockSpec((tm, a.shape[1]), lambda i: (i, 0)),
            pl.BlockSpec(w.shape, lambda i: (0, 0)),
            pl.BlockSpec((tm, d), lambda i: (i, 0)),
            pl.BlockSpec((None, 1, d), lambda i: (i // tiles_per_batch, 0, 0)),
        ],
        out_specs=pl.BlockSpec((tm, d), lambda i: (i, 0)),
        out_shape=jax.ShapeDtypeStruct((t, d), jnp.float32),
        compiler_params=_params(("arbitrary",)),
        name="outproj_residual",
    )(a, w, x2d, gt)


def _swa_kernel(sinks_ref, q_ref, kc_ref, kp_ref, vc_ref, vp_ref, gq_ref, gk_ref, bias_ref, o_ref):
    W = SW_WINDOW
    lane = lax.broadcasted_iota(jnp.int32, (W, LANES), 1)
    low = lane < SW_DH
    row_col = lax.broadcasted_iota(jnp.int32, (SW_GROUP * W, 1), 0)
    gq = gq_ref[...]
    gk = gk_ref[...]
    for g in range(SW_KV_HEADS):
        sl = slice(g * LANES, (g + 1) * LANES)
        k2 = jnp.concatenate([kp_ref[:, sl], kc_ref[:, sl]], axis=0).astype(jnp.float32)
        kn = k2 * lax.rsqrt(jnp.mean(k2 * k2, axis=-1, keepdims=True) + EPS)
        kn = (kn * gk).astype(jnp.bfloat16)
        v2 = jnp.concatenate([vp_ref[:, sl], vc_ref[:, sl]], axis=0)
        parts = []
        for p in range(2):
            c0 = g * SW_GROUP * SW_DH + p * LANES
            qp = q_ref[:, c0:c0 + LANES].astype(jnp.float32)
            sq = qp * qp
            ss_lo = jnp.sum(jnp.where(low, sq, 0.0), axis=-1, keepdims=True)
            ss_hi = jnp.sum(jnp.where(low, 0.0, sq), axis=-1, keepdims=True)
            rs = jnp.where(low, lax.rsqrt(ss_lo / SW_DH + EPS), lax.rsqrt(ss_hi / SW_DH + EPS))
            qn = qp * rs * gq
            parts.append(jnp.where(low, qn, 0.0).astype(jnp.bfloat16))
            parts.append(jnp.where(low, 0.0, qn).astype(jnp.bfloat16))
        q4 = jnp.concatenate(parts, axis=0)
        scores = _bdot_nt(q4, kn) + bias_ref[...]
        sink = jnp.full((SW_GROUP * W, 1), sinks_ref[g * SW_GROUP + SW_GROUP - 1], jnp.float32)
        for j in range(SW_GROUP - 2, -1, -1):
            sink = jnp.where(row_col < (j + 1) * W, sinks_ref[g * SW_GROUP + j], sink)
        m = jnp.maximum(jnp.max(scores, axis=-1, keepdims=True), sink)
        pexp = jnp.exp2(scores - m)
        denom = jnp.sum(pexp, axis=-1, keepdims=True) + jnp.exp2(sink - m)
        o4 = _bdot(pexp.astype(jnp.bfloat16), v2) * (1.0 / denom)
        for p in range(2):
            oa = o4[(2 * p) * W:(2 * p + 1) * W]
            ob = o4[(2 * p + 1) * W:(2 * p + 2) * W]
            c0 = g * SW_GROUP * SW_DH + p * LANES
            o_ref[:, c0:c0 + LANES] = jnp.where(low, oa, ob).astype(o_ref.dtype)


def _swa_core(proj, sinks, g_q, g_k, bsz, seq):
    t = proj.shape[0]
    W = SW_WINDOW
    nb = seq // W
    dq = SW_Q_HEADS * SW_DH
    kv_w = SW_KV_HEADS * LANES
    k_blk = dq // kv_w
    v_blk = k_blk + 1
    cur = lambda b, n, s: b * nb + n
    prev = lambda b, n, s: b * nb + jnp.maximum(n - 1, 0)
    gq2 = jnp.concatenate([g_q, g_q]).reshape(1, LANES) * (SW_DH ** -0.5 * LOG2E)
    gk2 = jnp.concatenate([g_k, g_k]).reshape(1, LANES)
    sinks = sinks.astype(jnp.float32) * LOG2E
    qi = (jnp.arange(SW_GROUP * W) % W)[:, None]
    ki = jnp.arange(2 * W)[None, :]
    rel = qi + W - ki
    in_win = (rel >= 0) & (rel < W)
    bias = jnp.stack([jnp.where(in_win & (ki >= W), 0.0, -jnp.inf),
                      jnp.where(in_win, 0.0, -jnp.inf)]).astype(jnp.float32)
    grid_spec = pltpu.PrefetchScalarGridSpec(
        num_scalar_prefetch=1,
        grid=(bsz, nb),
        in_specs=[
            pl.BlockSpec((W, dq), lambda b, n, s: (cur(b, n, s), 0)),
            pl.BlockSpec((W, kv_w), lambda b, n, s: (cur(b, n, s), k_blk)),
            pl.BlockSpec((W, kv_w), lambda b, n, s: (prev(b, n, s), k_blk)),
            pl.BlockSpec((W, kv_w), lambda b, n, s: (cur(b, n, s), v_blk)),
            pl.BlockSpec((W, kv_w), lambda b, n, s: (prev(b, n, s), v_blk)),
            pl.BlockSpec((1, LANES), lambda b, n, s: (0, 0)),
            pl.BlockSpec((1, LANES), lambda b, n, s: (0, 0)),
            pl.BlockSpec((None, SW_GROUP * W, 2 * W), lambda b, n, s: (jnp.minimum(n, 1), 0, 0)),
        ],
        out_specs=pl.BlockSpec((W, dq), lambda b, n, s: (cur(b, n, s), 0)),
    )
    return pl.pallas_call(
        _swa_kernel,
        grid_spec=grid_spec,
        out_shape=jax.ShapeDtypeStruct((t, dq), jnp.bfloat16),
        compiler_params=_params(("arbitrary", "arbitrary")),
        name="swa_core",
    )(sinks.astype(jnp.float32), proj, proj, proj, proj, proj, gq2, gk2, bias)


def _pack_rounded_pairs(r):
    k = r.shape[1] // 2
    hi = lax.bitcast_convert_type(r[:, :k], jnp.uint32)
    lo = lax.bitcast_convert_type(r[:, k:], jnp.uint32)
    return hi | (lo >> 16)


def _pack_bf16_pairs(a):
    return _pack_rounded_pairs(a.astype(jnp.bfloat16).astype(jnp.float32))


def _unpack_bf16_pairs(u):
    hi = lax.bitcast_convert_type(u & jnp.uint32(0xFFFF0000), jnp.float32)
    lo = lax.bitcast_convert_type(u << 16, jnp.float32)
    return hi, lo


def _router_kernel(x_ref, g_ref, sc_ref, sh_ref, wh_ref, wl_ref, b_ref,
                   hn_ref, info_ref, infot_ref, cnt_ref, carry_ref, lower_ref, *, tm):
    @pl.when(pl.program_id(0) == 0)
    def _():
        carry_ref[...] = jnp.zeros_like(carry_ref)
        r_idx = lax.broadcasted_iota(jnp.int32, (tm, tm), 0)
        c_idx = lax.broadcasted_iota(jnp.int32, (tm, tm), 1)
        lower_ref[...] = jnp.where(c_idx < r_idx, 1.0, 0.0).astype(jnp.bfloat16)

    hn = _modulated_norm(x_ref[...], g_ref[...], sc_ref[...], sh_ref[...])
    h_hi = hn.astype(jnp.bfloat16)
    hi_f32 = h_hi.astype(jnp.float32)
    h_lo = (hn - hi_f32).astype(jnp.bfloat16)
    hn_ref[...] = _pack_rounded_pairs(hi_f32)
    wide = _bdot(h_hi, wl_ref[...])
    logits = (wide[:, :ROUTER_LANES] + (_bdot(h_lo, wh_ref[...]) + wide[:, ROUTER_LANES:])
              + b_ref[...])
    lane = lax.broadcasted_iota(jnp.int32, (tm, ROUTER_LANES), 1).astype(jnp.float32)
    big = float(4 * ROUTER_LANES)
    neg = -jnp.inf

    is_grp = (lane >= N_EXPERTS) & (lane < N_EXPERTS + N_GROUPS)
    gl = jnp.where(is_grp, logits, neg)
    gmax = jnp.max(gl, axis=-1, keepdims=True)
    gsel = jnp.min(jnp.where(gl == gmax, lane, big), axis=-1, keepdims=True) - N_EXPERTS
    p_grp = 1.0 / jnp.sum(jnp.exp(gl - gmax), axis=-1, keepdims=True)

    lo_lane = gsel * EXPERTS_PER_GROUP
    in_grp = (lane >= lo_lane) & (lane < lo_lane + EXPERTS_PER_GROUP)
    el = jnp.where(in_grp, logits, neg)
    v1 = jnp.max(el, axis=-1, keepdims=True)
    i1 = jnp.min(jnp.where(el == v1, lane, big), axis=-1, keepdims=True)
    el2 = jnp.where(lane == i1, neg, el)
    v2 = jnp.max(el2, axis=-1, keepdims=True)
    i2 = jnp.min(jnp.where(el2 == v2, lane, big), axis=-1, keepdims=True)
    e21 = jnp.exp(v2 - v1)
    gate1 = p_grp / (1.0 + e21)
    gate2 = p_grp * e21 / (1.0 + e21)

    hit1 = lane == i1
    hit2 = lane == i2
    onehot = jnp.where(hit1 | hit2, 1.0, 0.0)
    before = _bdot(lower_ref[...], onehot.astype(jnp.bfloat16)) + carry_ref[0:1, :]
    rank1 = jnp.sum(jnp.where(hit1, before, 0.0), axis=-1, keepdims=True)
    rank2 = jnp.sum(jnp.where(hit2, before, 0.0), axis=-1, keepdims=True)
    total = carry_ref[0:1, :] + jnp.sum(onehot, axis=0, keepdims=True)
    carry_ref[...] = jnp.broadcast_to(total, carry_ref.shape)
    cnt_ref[...] = jnp.broadcast_to(total, cnt_ref.shape)

    info = jnp.where(lane == 0, i1, 0.0)
    info = jnp.where(lane == 1, i2, info)
    info = jnp.where(lane == 2, rank1, info)
    info = jnp.where(lane == 3, rank2, info)
    info = jnp.where(lane == 4, gate1, info)
    info = jnp.where(lane == 5, gate2, info)
    info_ref[...] = info
    infot_ref[...] = info.T[0:8, :]


def _router(x2d, g, sc, sh, w_hi, w_lo, bias, seq):
    t, d = x2d.shape
    tm = 512
    tiles_per_batch = seq // tm
    per_batch = lambda i: (i // tiles_per_batch, 0, 0)
    const = lambda i: (0, 0)
    return pl.pallas_call(
        functools.partial(_router_kernel, tm=tm),
        grid=(t // tm,),
        in_specs=[
            pl.BlockSpec((tm, d), lambda i: (i, 0)),
            pl.BlockSpec((1, d), const),
            pl.BlockSpec((None, 1, d), per_batch),
            pl.BlockSpec((None, 1, d), per_batch),
            pl.BlockSpec((d, ROUTER_LANES), const),
            pl.BlockSpec((d, 2 * ROUTER_LANES), const),
            pl.BlockSpec((1, ROUTER_LANES), const),
        ],
        out_specs=[
            pl.BlockSpec((tm, d // 2), lambda i: (i, 0)),
            pl.BlockSpec((tm, ROUTER_LANES), lambda i: (i, 0)),
            pl.BlockSpec((8, tm), lambda i: (0, i)),
            pl.BlockSpec((8, ROUTER_LANES), const),
        ],
        out_shape=[
            jax.ShapeDtypeStruct((t, d // 2), jnp.uint32),
            jax.ShapeDtypeStruct((t, ROUTER_LANES), jnp.float32),
            jax.ShapeDtypeStruct((8, t), jnp.float32),
            jax.ShapeDtypeStruct((8, ROUTER_LANES), jnp.float32),
        ],
        scratch_shapes=[pltpu.VMEM((8, ROUTER_LANES), jnp.float32),
                        pltpu.VMEM((tm, tm), jnp.bfloat16)],
        compiler_params=_params(("arbitrary",)),
        name="moe_router",
    )(x2d, g.reshape(1, d), sc, sh, w_hi, jnp.concatenate([w_hi, w_lo], axis=1), bias)


def _sc_mesh():
    return plsc.VectorSubcoreMesh(core_axis_name="c", subcore_axis_name="s")


def _sc_dispatch(rows, d0, d1, pad_idx, n_slots):
    t, w = rows.shape
    n_pad = pad_idx.shape[1]
    zeros = jnp.zeros((SC_WINDOW, w), rows.dtype)
    sem = (pltpu.PARALLEL, pltpu.ARBITRARY)

    @pl.kernel(out_type=jax.ShapeDtypeStruct((n_slots + SC_WINDOW, w), rows.dtype), mesh=_sc_mesh())
    def dispatch(x_hbm, d0_hbm, d1_hbm, z_hbm, p_hbm, o_hbm):
        def scatter_rows(x_vmem, i0_vmem, i1_vmem):
            cols = pl.ds(pl.program_id(1) * SC_COLS, SC_COLS)
            pltpu.sync_copy(x_vmem, o_hbm.at[i0_vmem.at[0], cols])
            pltpu.sync_copy(x_vmem, o_hbm.at[i1_vmem.at[0], cols])

        pltpu.emit_pipeline(
            scatter_rows,
            grid=(t // SC_WINDOW, w // SC_COLS),
            in_specs=[pl.BlockSpec((SC_WINDOW, SC_COLS), lambda i, j: (i, j)),
                      pl.BlockSpec((1, SC_WINDOW), lambda i, j: (0, i)),
                      pl.BlockSpec((1, SC_WINDOW), lambda i, j: (0, i))],
            out_specs=[],
            core_axis_name=("c", "s"),
            dimension_semantics=sem,
        )(x_hbm, d0_hbm, d1_hbm)

        def scatter_zeros(z_vmem, p_vmem):
            cols = pl.ds(pl.program_id(1) * SC_COLS, SC_COLS)
            pltpu.sync_copy(z_vmem, o_hbm.at[p_vmem.at[0], cols])

        pltpu.emit_pipeline(
            scatter_zeros,
            grid=(n_pad // SC_WINDOW, w // SC_COLS),
            in_specs=[pl.BlockSpec((SC_WINDOW, SC_COLS), lambda i, j: (0, j)),
                      pl.BlockSpec((1, SC_WINDOW), lambda i, j: (0, i))],
            out_specs=[],
            core_axis_name=("c", "s"),
            dimension_semantics=sem,
        )(z_hbm, p_hbm)

    return dispatch(rows, d0, d1, zeros, pad_idx)


def _sc_gather(src, idx):
    n_out = idx.shape[1]
    w = src.shape[1]

    @pl.kernel(out_type=jax.ShapeDtypeStruct((n_out, w), src.dtype), mesh=_sc_mesh())
    def gather(x_hbm, i_hbm, o_hbm):
        def gather_rows(i_vmem, o_vmem):
            cols = pl.ds(pl.program_id(1) * SC_COLS, SC_COLS)
            pltpu.sync_copy(x_hbm.at[i_vmem.at[0], cols], o_vmem)

        pltpu.emit_pipeline(
            gather_rows,
            grid=(n_out // SC_WINDOW, w // SC_COLS),
            in_specs=[pl.BlockSpec((1, SC_WINDOW), lambda i, j: (0, i))],
            out_specs=[pl.BlockSpec((SC_WINDOW, SC_COLS), lambda i, j: (i, j))],
            core_axis_name=("c", "s"),
            dimension_semantics=(pltpu.PARALLEL, pltpu.ARBITRARY),
        )(i_hbm, o_hbm)

    return gather(src, idx)


def _expert_kernel(meta_ref, x_ref, w1_hbm, w3_hbm, w2_hbm, y_ref,
                   w1_buf, w3_buf, w2_buf, w1_c, w3_c, w2_c, sems, *, layer, nb):
    i = pl.program_id(0)
    e = meta_ref[i]
    slot = meta_ref[nb + i]
    nxt = meta_ref[2 * nb + i]
    used = i < meta_ref[3 * nb]
    first = used & ((i == 0) | (e != meta_ref[jnp.maximum(i - 1, 0)]))

    def weight_copies(expert, s):
        return (pltpu.make_async_copy(w1_hbm.at[layer, expert], w1_buf.at[s], sems.at[s, 0]),
                pltpu.make_async_copy(w3_hbm.at[layer, expert], w3_buf.at[s], sems.at[s, 1]),
                pltpu.make_async_copy(w2_hbm.at[layer, expert], w2_buf.at[s], sems.at[s, 2]))

    @pl.when(i == 0)
    def _():
        for cp in weight_copies(e, slot):
            cp.start()

    @pl.when(first)
    def _():
        for cp in weight_copies(e, slot):
            cp.wait()

        @pl.when(nxt >= 0)
        def _():
            for cp in weight_copies(nxt, 1 - slot):
                cp.start()

        w1_c[...] = w1_buf[slot].astype(jnp.bfloat16)
        w3_c[...] = w3_buf[slot].astype(jnp.bfloat16)
        w2_c[...] = w2_buf[slot].astype(jnp.bfloat16)

    @pl.when(used)
    def _():
        x_hi, x_lo = _unpack_bf16_pairs(x_ref[...])
        xb = jnp.concatenate([x_hi, x_lo], axis=-1).astype(jnp.bfloat16)
        h1 = _bdot(xb, w1_c[...])
        h3 = _bdot(xb, w3_c[...])
        act = (h1 * jax.nn.sigmoid(h1) * h3).astype(jnp.bfloat16)
        y_ref[...] = _pack_bf16_pairs(_bdot(act, w2_c[...]))

    @pl.when(jnp.logical_not(used))
    def _():
        y_ref[...] = jnp.zeros_like(y_ref)


def _experts(x_slots, blk_meta, w1, w3, w2, layer, nb):
    dp = x_slots.shape[1]
    d, de = w1.shape[-2:]
    rowblk = lambda i, s: (jnp.minimum(i, s[3 * nb] - 1), 0)
    grid_spec = pltpu.PrefetchScalarGridSpec(
        num_scalar_prefetch=1,
        grid=(nb,),
        in_specs=[
            pl.BlockSpec((MOE_BLOCK, dp), rowblk),
            pl.BlockSpec(memory_space=pl.ANY),
            pl.BlockSpec(memory_space=pl.ANY),
            pl.BlockSpec(memory_space=pl.ANY),
        ],
        out_specs=pl.BlockSpec((MOE_BLOCK, dp), lambda i, s: (i, 0)),
        scratch_shapes=[
            pltpu.VMEM((2, d, de), jnp.float32),
            pltpu.VMEM((2, d, de), jnp.float32),
            pltpu.VMEM((2, de, d), jnp.float32),
            pltpu.VMEM((d, de), jnp.bfloat16),
            pltpu.VMEM((d, de), jnp.bfloat16),
            pltpu.VMEM((de, d), jnp.bfloat16),
            pltpu.SemaphoreType.DMA((2, 3)),
        ],
    )
    return pl.pallas_call(
        functools.partial(_expert_kernel, layer=layer, nb=nb),
        grid_spec=grid_spec,
        out_shape=jax.ShapeDtypeStruct((nb * MOE_BLOCK, dp), jnp.uint32),
        compiler_params=_params(("arbitrary",)),
        name="moe_experts",
    )(blk_meta, x_slots, w1, w3, w2)


def _combine_kernel(x_ref, y1_ref, y2_ref, info_ref, gt_ref, o_ref):
    k = x_ref.shape[1] // 2
    info = info_ref[...]
    g1 = info[:, 4:5]
    g2 = info[:, 5:6]
    y1_hi, y1_lo = _unpack_bf16_pairs(y1_ref[...])
    y2_hi, y2_lo = _unpack_bf16_pairs(y2_ref[...])
    o_ref[:, :k] = x_ref[:, :k] + gt_ref[:, :k] * (g1 * y1_hi + g2 * y2_hi)
    o_ref[:, k:] = x_ref[:, k:] + gt_ref[:, k:] * (g1 * y1_lo + g2 * y2_lo)


def _combine(x2d, y_pairs, info, gt, seq):
    t, d = x2d.shape
    tm = min(1024, seq)
    tiles_per_batch = seq // tm
    second = t // tm
    return pl.pallas_call(
        _combine_kernel,
        grid=(t // tm,),
        in_specs=[
            pl.BlockSpec((tm, d), lambda i: (i, 0)),
            pl.BlockSpec((tm, d // 2), lambda i: (i, 0)),
            pl.BlockSpec((tm, d // 2), lambda i: (i + second, 0)),
            pl.BlockSpec((tm, ROUTER_LANES), lambda i: (i, 0)),
            pl.BlockSpec((None, 1, d), lambda i: (i // tiles_per_batch, 0, 0)),
        ],
        out_specs=pl.BlockSpec((tm, d), lambda i: (i, 0)),
        out_shape=jax.ShapeDtypeStruct((t, d), jnp.float32),
        compiler_params=_params(("arbitrary",)),
        name="moe_combine",
    )(x2d, y_pairs, y_pairs, info, gt)


def _slot_plan(info_t, cnt, t):
    counts = cnt[0, :N_EXPERTS].astype(jnp.int32)
    padded = (counts + MOE_BLOCK - 1) // MOE_BLOCK * MOE_BLOCK
    pad_ends = jnp.cumsum(padded)
    pad_starts = pad_ends - padded
    nb = -(-(2 * t) // MOE_BLOCK) + N_EXPERTS
    n_slots = nb * MOE_BLOCK
    it = info_t.astype(jnp.int32)
    onehot_start = lambda e: jnp.sum(
        jnp.where(e[None, :] == jnp.arange(N_EXPERTS, dtype=jnp.int32)[:, None],
                  pad_starts[:, None], 0), axis=0)
    dest1 = (onehot_start(it[0]) + it[2]).reshape(1, t)
    dest2 = (onehot_start(it[1]) + it[3]).reshape(1, t)
    lane = jnp.arange(MOE_BLOCK, dtype=jnp.int32)[None, :]
    n_padding = (padded - counts)[:, None]
    wrapped = (pad_starts + counts)[:, None] + lane % jnp.maximum(n_padding, 1)
    pad_idx = jnp.where(n_padding > 0, wrapped, n_slots + lane % SC_WINDOW).reshape(1, -1)
    experts = jnp.arange(N_EXPERTS, dtype=jnp.int32)
    blk = jnp.arange(nb, dtype=jnp.int32)
    blk_exp = jnp.minimum(
        jnp.sum((pad_ends[None, :] <= (blk * MOE_BLOCK)[:, None]).astype(jnp.int32), axis=1),
        N_EXPERTS - 1)
    n_used = pad_ends[-1] // MOE_BLOCK
    prev_exp = jnp.concatenate([jnp.full((1,), -1, jnp.int32), blk_exp[:-1]])
    is_first = (blk < n_used) & (blk_exp != prev_exp)
    blk_slot = (jnp.cumsum(is_first.astype(jnp.int32)) + 1) % 2
    later = (experts[None, :] > experts[:, None]) & (padded[None, :] > 0)
    nxt_of = jnp.min(jnp.where(later, experts[None, :], N_EXPERTS), axis=1)
    nxt_of = jnp.where(nxt_of == N_EXPERTS, -1, nxt_of)
    blk_nxt = jnp.sum(jnp.where(blk_exp[:, None] == experts[None, :], nxt_of[None, :], 0), axis=1)
    blk_meta = jnp.concatenate([blk_exp, blk_slot, blk_nxt, n_used[None]]).astype(jnp.int32)
    return dest1, dest2, pad_idx, blk_meta, nb, n_slots


def _moe_layer(xs, g, scs, shs, gts, w_group, b_group, w_router, b_router, w1, w3, w2, layer, seq):
    d = xs[0].shape[1]
    w_cat = jnp.zeros((d, ROUTER_LANES), jnp.float32)
    w_cat = w_cat.at[:, :N_EXPERTS].set(w_router).at[:, N_EXPERTS:N_EXPERTS + N_GROUPS].set(w_group)
    b_cat = jnp.zeros((1, ROUTER_LANES), jnp.float32)
    b_cat = b_cat.at[0, :N_EXPERTS].set(b_router).at[0, N_EXPERTS:N_EXPERTS + N_GROUPS].set(b_group)
    w_hi, w_lo = _split_hi_lo(w_cat)
    routed = [_router(x, g, sc, sh, w_hi, w_lo, b_cat, seq) for x, sc, sh in zip(xs, scs, shs)]
    plans = [_slot_plan(info_t, cnt, x.shape[0]) for x, (_, _, info_t, cnt) in zip(xs, routed)]
    x_slots = [_sc_dispatch(hn, d1, d2, pad_idx, n_slots)
               for (hn, _, _, _), (d1, d2, pad_idx, _, _, n_slots) in zip(routed, plans)]
    y_slots = [_experts(xsl, meta, w1, w3, w2, layer, nb)
               for xsl, (_, _, _, meta, nb, _) in zip(x_slots, plans)]
    y_pairs = [_sc_gather(ysl, jnp.concatenate([d1, d2], axis=1))
               for ysl, (d1, d2, _, _, _, _) in zip(y_slots, plans)]
    return [_combine(x, yp, info, gt, seq)
            for x, yp, (_, info, _, _), gt in zip(xs, y_pairs, routed, gts)]


def kernel(x, c, w_ada, b_ada, norm1_g, norm2_g, ml_w_in, ml_b_gate, ml_g_out, ml_w_out,
           sw_w_in, sw_g_q, sw_g_k, sw_sinks, sw_w_out, moe_w_group, moe_b_group,
           moe_w_router, moe_b_router, moe_w1, moe_w3, moe_w2):
    bsz, seq, d = x.shape
    depth = w_ada.shape[0]
    bf = jnp.bfloat16
    mod = _ada_mod(c, w_ada, b_ada)
    n_parts = N_PARTS if bsz % N_PARTS == 0 else 1
    pb = bsz // n_parts
    xs = [x[p * pb:(p + 1) * pb].reshape(pb * seq, d) for p in range(n_parts)]
    per_part = lambda m: [m[p * pb:(p + 1) * pb] for p in range(n_parts)]
    for layer in range(depth):
        sh1, sc1, gt1, sh2, sc2, gt2 = [
            per_part(mod[layer, :, i * d:(i + 1) * d].reshape(bsz, 1, d)) for i in range(6)]
        j = layer // 2
        if layer % 2 == 0:
            w = ml_w_in[j]
            q_w, k_w = w[:, :ML_QK], w[:, ML_QK:2 * ML_QK]
            v_w = w[:, 2 * ML_QK:2 * ML_QK + ML_V]
            o_w = w[:, 2 * ML_QK + ML_V:2 * ML_QK + 2 * ML_V]
            g_w = w[:, 2 * ML_QK + 2 * ML_V:]
            w_main = jnp.concatenate([v_w, o_w, q_w], axis=1).astype(bf)
            wg_t = jnp.zeros((ML_GATE_ROWS, d), jnp.float32).at[:2 * ML_HEADS].set(g_w.T)
            wg_hi, wg_lo = _split_hi_lo(wg_t)
            wk_t = k_w.T.astype(bf)
            w_out = ml_w_out[j].astype(bf)
            proj = [_inproj(xp, norm1_g[layer], sc, sh, w_main, seq, ml_extra=(wk_t, wg_hi, wg_lo),
                            q_cols=(2 * ML_V, 2 * ML_V + ML_QK), q_scale=ML_DK ** -0.5,
                            gate_cols=(ML_V, 2 * ML_V))
                    for xp, sc, sh in zip(xs, sc1, sh1)]
            mixed = [_mlstm_core(main, k_t, g_t, ml_b_gate[j], ml_g_out[j], pb, seq)
                     for main, k_t, g_t in proj]
            xs = [_outproj_residual(a, w_out, xp, gt, seq) for a, xp, gt in zip(mixed, xs, gt1)]
        else:
            w = sw_w_in[j]
            dq = SW_Q_HEADS * SW_DH
            dkv = SW_KV_HEADS * SW_DH
            dup = lambda m: jnp.concatenate(
                [m.reshape(d, SW_KV_HEADS, 1, SW_DH)] * 2, axis=2).reshape(d, 2 * dkv)
            w_main = jnp.concatenate(
                [w[:, :dq], dup(w[:, dq:dq + dkv]), dup(w[:, dq + dkv:])], axis=1).astype(bf)
            w_out = sw_w_out[j].astype(bf)
            proj = [_inproj(xp, norm1_g[layer], sc, sh, w_main, seq)[0]
                    for xp, sc, sh in zip(xs, sc1, sh1)]
            mixed = [_swa_core(pr, sw_sinks[j], sw_g_q[j], sw_g_k[j], pb, seq) for pr in proj]
            xs = [_outproj_residual(a, w_out, xp, gt, seq) for a, xp, gt in zip(mixed, xs, gt1)]
        xs = _moe_layer(xs, norm2_g[layer], sc2, sh2, gt2, moe_w_group[layer],
                        moe_b_group[layer], moe_w_router[layer], moe_b_router[layer],
                        moe_w1, moe_w3, moe_w2, layer, seq)
    return jnp.concatenate(xs, axis=0).reshape(bsz, seq, d)
```

```python
import functools

import jax
import jax.numpy as jnp
from jax import lax
from jax.experimental import pallas as pl
from jax.experimental.pallas import tpu as pltpu
from jax.experimental.pallas import tpu_sc as plsc

EPS = 1e-6
GATE_CAP = 15.0
LOG2E = 1.4426950408889634

ML_HEADS = 4
ML_DK = 128
ML_DV = 256
ML_QK = ML_HEADS * ML_DK
ML_V = ML_HEADS * ML_DV
ML_GATE_ROWS = 16

SW_Q_HEADS = 16
SW_KV_HEADS = 4
SW_GROUP = SW_Q_HEADS // SW_KV_HEADS
SW_DH = 64
SW_WINDOW = 128
LANES = 128

N_GROUPS = 8
EXPERTS_PER_GROUP = 8
N_EXPERTS = N_GROUPS * EXPERTS_PER_GROUP
MOE_BLOCK = 256
ROUTER_LANES = 128
SC_WINDOW = 128
SC_COLS = 256

VMEM_LIMIT = 56 * 1024 * 1024

_NT = (((1,), (1,)), ((), ()))


def _bdot(a, b):
    return jnp.dot(a, b, preferred_element_type=jnp.float32)


def _bdot_nt(a, b):
    return lax.dot_general(a, b, _NT, preferred_element_type=jnp.float32)


def _split_hi_lo(a):
    hi = a.astype(jnp.bfloat16)
    lo = (a - hi.astype(jnp.float32)).astype(jnp.bfloat16)
    return hi, lo


def _params(sem):
    return pltpu.CompilerParams(dimension_semantics=sem, vmem_limit_bytes=VMEM_LIMIT)


def _ada_kernel(c_ref, w_ref, b_ref, o_ref):
    c = c_ref[...]
    cond = c * jax.nn.sigmoid(c)
    c_hi, c_lo = _split_hi_lo(cond)
    w_hi, w_lo = _split_hi_lo(w_ref[...])
    acc = _bdot(c_hi, w_hi) + (_bdot(c_lo, w_hi) + _bdot(c_hi, w_lo))
    o_ref[...] = acc + b_ref[...]


def _ada_mod(c, w_ada, b_ada):
    depth, d, n = w_ada.shape
    bsz = c.shape[0]
    rows = 8
    tn = 768
    c_pad = jnp.zeros((rows, d), jnp.float32).at[:bsz].set(c)
    out = pl.pallas_call(
        _ada_kernel,
        grid=(depth, n // tn),
        in_specs=[
            pl.BlockSpec((rows, d), lambda l, j: (0, 0)),
            pl.BlockSpec((None, d, tn), lambda l, j: (l, 0, j)),
            pl.BlockSpec((None, 1, tn), lambda l, j: (l, 0, j)),
        ],
        out_specs=pl.BlockSpec((None, rows, tn), lambda l, j: (l, 0, j)),
        out_shape=jax.ShapeDtypeStruct((depth, rows, n), jnp.float32),
        compiler_params=_params(("arbitrary", "arbitrary")),
        name="ada_mod",
    )(c_pad, w_ada, b_ada.reshape(depth, 1, n))
    return out[:, :bsz]


def _modulated_norm(x, g, sc, sh):
    y = x * lax.rsqrt(jnp.mean(x * x, axis=-1, keepdims=True) + EPS)
    return y * (g * (1.0 + sc)) + sh


def _moe_combined(x, y1_ref, y2_ref, info_ref, gt_ref):
    info = info_ref[...]
    g1 = info[:, 4:5]
    g2 = info[:, 5:6]
    y1_hi, y1_lo = _unpack_bf16_pairs(y1_ref[...])
    y2_hi, y2_lo = _unpack_bf16_pairs(y2_ref[...])
    y = jnp.concatenate([g1 * y1_hi + g2 * y2_hi, g1 * y1_lo + g2 * y2_lo], axis=-1)
    return x + gt_ref[...] * y


def _inproj_kernel(*refs, n_main, chunk, q_cols, q_scale, gate_cols, with_ml, with_combine):
    refs = list(refs)
    n_in = 5 + (4 if with_combine else 0) + (3 if with_ml else 0)
    ins, outs = refs[:n_in], refs[n_in:]
    x_ref = ins.pop(0)
    x = x_ref[...]
    if with_combine:
        y1_ref, y2_ref, info_ref, gtp_ref = ins[:4]
        ins = ins[4:]
        x = _moe_combined(x, y1_ref, y2_ref, info_ref, gtp_ref)
        outs.pop(0)[...] = x
    g_ref, sc_ref, sh_ref, w_ref = ins[:4]
    o_ref = outs[0]
    if with_ml:
        wk_ref, wgh_ref, wgl_ref = ins[4:]
        kt_ref, gt_ref = outs[1:]
    hn = _modulated_norm(x, g_ref[...], sc_ref[...], sh_ref[...])
    hb = hn.astype(jnp.bfloat16)
    for c0 in range(0, n_main, chunk):
        acc = _bdot(hb, w_ref[:, c0:c0 + chunk])
        if q_cols is not None and q_cols[0] <= c0 < q_cols[1]:
            acc = acc * q_scale
        if gate_cols is not None and gate_cols[0] <= c0 < gate_cols[1]:
            acc = jax.nn.sigmoid(acc)
        o_ref[:, c0:c0 + chunk] = acc.astype(o_ref.dtype)
    if with_ml:
        kt_ref[...] = _bdot_nt(wk_ref[...], hb).astype(kt_ref.dtype)
        h_lo = (hn - hb.astype(jnp.float32)).astype(jnp.bfloat16)
        gt_ref[...] = (_bdot_nt(wgh_ref[...], hb)
                       + (_bdot_nt(wgh_ref[...], h_lo) + _bdot_nt(wgl_ref[...], hb)))


def _inproj(x2d, g, sc, sh, w_main, seq, *, ml_extra=None, q_cols=None, q_scale=1.0,
            gate_cols=None, pending=None):
    t, d = x2d.shape
    tm = 512
    n_main = w_main.shape[1]
    tiles_per_batch = seq // tm
    row = lambda i: (i, 0)
    per_batch = lambda i: (i // tiles_per_batch, 0, 0)
    const = lambda i: (0, 0)
    in_specs = [pl.BlockSpec((tm, d), row)]
    args = [x2d]
    out_specs, out_shape = [], []
    if pending is not None:
        y_pairs, info, gt_prev = pending
        second = t // tm
        in_specs += [pl.BlockSpec((tm, d // 2), row),
                     pl.BlockSpec((tm, d // 2), lambda i: (i + second, 0)),
                     pl.BlockSpec((tm, ROUTER_LANES), row),
                     pl.BlockSpec((None, 1, d), per_batch)]
        args += [y_pairs, y_pairs, info, gt_prev]
        out_specs += [pl.BlockSpec((tm, d), row)]
        out_shape += [jax.ShapeDtypeStruct((t, d), jnp.float32)]
    in_specs += [
        pl.BlockSpec((1, d), const),
        pl.BlockSpec((None, 1, d), per_batch),
        pl.BlockSpec((None, 1, d), per_batch),
        pl.BlockSpec((d, n_main), const),
    ]
    args += [g.reshape(1, d), sc, sh, w_main]
    out_specs += [pl.BlockSpec((tm, n_main), row)]
    out_shape += [jax.ShapeDtypeStruct((t, n_main), jnp.bfloat16)]
    if ml_extra is not None:
        wk_t, wg_hi, wg_lo = ml_extra
        in_specs += [pl.BlockSpec(wk_t.shape, const),
                     pl.BlockSpec(wg_hi.shape, const),
                     pl.BlockSpec(wg_lo.shape, const)]
        args += [wk_t, wg_hi, wg_lo]
        out_specs += [pl.BlockSpec((wk_t.shape[0], tm), lambda i: (0, i)),
                      pl.BlockSpec((ML_GATE_ROWS, tm), lambda i: (0, i))]
        out_shape += [jax.ShapeDtypeStruct((wk_t.shape[0], t), jnp.bfloat16),
                      jax.ShapeDtypeStruct((ML_GATE_ROWS, t), jnp.float32)]
    kern = functools.partial(_inproj_kernel, n_main=n_main, chunk=512, q_cols=q_cols,
                             q_scale=q_scale, gate_cols=gate_cols, with_ml=ml_extra is not None,
                             with_combine=pending is not None)
    return pl.pallas_call(
        kern,
        grid=(t // tm,),
        in_specs=in_specs,
        out_specs=out_specs,
        out_shape=out_shape,
        compiler_params=_params(("arbitrary",)),
        name="inproj_ml" if ml_extra is not None else "inproj_sw",
    )(*args)


def _mlstm_gate_terms(graw, bias, upper):
    H = ML_HEADS
    L = graw.shape[1]
    z = graw + bias
    gates = GATE_CAP * jnp.tanh(z / GATE_CAP)
    log_f = jnp.minimum(gates, 0.0) - jnp.log1p(jnp.exp(-jnp.abs(gates)))
    row = lax.broadcasted_iota(jnp.int32, (ML_GATE_ROWS, L), 0)
    lane = lax.broadcasted_iota(jnp.int32, (ML_GATE_ROWS, L), 1)
    is_i = row < H
    slab = jnp.where(is_i, gates, log_f)
    a1 = slab.astype(jnp.bfloat16)
    r1 = slab - a1.astype(jnp.float32)
    a2 = r1.astype(jnp.bfloat16)
    a3 = (r1 - a2.astype(jnp.float32)).astype(jnp.bfloat16)
    cum = _bdot(a1, upper) + (_bdot(a2, upper) + _bdot(a3, upper))
    ib = jnp.where(is_i, gates, cum)
    b = pltpu.roll(ib, ML_GATE_ROWS - H, 0)
    u = ib - b
    cm = u
    shift = 1
    while shift < L:
        cm = jnp.maximum(cm, jnp.where(lane >= shift, pltpu.roll(cm, shift, 1), -jnp.inf))
        shift *= 2
    return b, u, cm


def _mlstm_kernel(v_ref, o_ref, q_ref, kt_ref, gt_ref, gtn_ref, bg_ref, gout_ref, out_ref,
                  c_ref, m_ref, b_ref, u_ref, cm_ref, *, chunk):
    L = chunk
    H, dk, dv = ML_HEADS, ML_DK, ML_DV
    r_idx = lax.broadcasted_iota(jnp.int32, (L, L), 0)
    c_idx = lax.broadcasted_iota(jnp.int32, (L, L), 1)
    upper = jnp.where(r_idx <= c_idx, 1.0, 0.0).astype(jnp.bfloat16)
    causal = r_idx >= c_idx
    row = lax.broadcasted_iota(jnp.int32, (ML_GATE_ROWS, L), 0)
    ones_col = jnp.where(lax.broadcasted_iota(jnp.int32, (L, LANES), 1) == 0, 1.0, 0.0
                         ).astype(jnp.bfloat16)

    @pl.when(pl.program_id(1) == 0)
    def _():
        c_ref[...] = jnp.zeros_like(c_ref)
        m_ref[...] = jnp.zeros_like(m_ref)
        b0, u0, cm0 = _mlstm_gate_terms(gt_ref[...], bg_ref[...], upper)
        b_ref[...] = b0
        u_ref[...] = u0
        cm_ref[...] = cm0

    b16 = b_ref[...]
    u16 = u_ref[...]
    cm16 = cm_ref[...]
    b_n, u_n, cm_n = _mlstm_gate_terms(gtn_ref[...], bg_ref[...], upper)
    b_ref[...] = b_n
    u_ref[...] = u_n
    cm_ref[...] = cm_n

    m_prev = m_ref[:, 0:1]
    z16 = jnp.maximum(m_prev, cm16)
    w_inter16 = jnp.exp(m_prev - z16)
    e_negm16 = jnp.exp(-(b16 + z16))
    z_last = z16[:, L - 1:L]
    w_state16 = jnp.exp(u16 - z_last)
    decay16 = jnp.exp(m_prev - z_last)
    m_ref[...] = jnp.broadcast_to(b16[:, L - 1:L] + z_last, m_ref.shape)
    stacked = jnp.where(row < H, z16,
                        jnp.where(row < 2 * H, pltpu.roll(w_inter16, H, 0),
                                  pltpu.roll(e_negm16, 2 * H, 0)))
    cols = jnp.concatenate(
        [stacked, jnp.zeros((LANES - ML_GATE_ROWS, L), jnp.float32)], axis=0).T

    for h in range(H):
        u_r = u16[h:h + 1, :]
        z_c = cols[:, h:h + 1]
        w_inter = cols[:, H + h:H + h + 1]
        e_negm = cols[:, 2 * H + h:2 * H + h + 1]
        c_ext = c_ref[h]
        q = q_ref[:, h * dk:(h + 1) * dk]
        kt = kt_ref[h * dk:(h + 1) * dk, :]
        v_ext = jnp.concatenate([v_ref[:, h * dv:(h + 1) * dv], ones_col], axis=-1)

        w_intra = jnp.exp(jnp.where(causal, u_r - z_c, -jnp.inf))
        s = (_bdot(q, kt) * w_intra).astype(jnp.bfloat16)
        nd = w_inter * _bdot(q, c_ext.astype(jnp.bfloat16)) + _bdot(s, v_ext)
        den = nd[:, dv:dv + 1]
        hb = nd[:, :dv] * (1.0 / jnp.maximum(jnp.abs(den), e_negm))

        kw = (kt.astype(jnp.float32) * w_state16[h:h + 1, :]).astype(jnp.bfloat16)
        c_ref[h] = decay16[h:h + 1, :] * c_ext + _bdot(kw, v_ext)

        y = hb * lax.rsqrt(jnp.mean(hb * hb, axis=-1, keepdims=True) + EPS)
        y = y * gout_ref[:, h * dv:(h + 1) * dv]
        og = o_ref[:, h * dv:(h + 1) * dv].astype(jnp.float32)
        out_ref[:, h * dv:(h + 1) * dv] = (y * og).astype(out_ref.dtype)


def _mlstm_core(main, k_t, g_t, b_gate, g_out, bsz, seq, chunk=256):
    t = main.shape[0]
    nc = seq // chunk
    blk = lambda b, c: b * nc + c
    bg = jnp.zeros((ML_GATE_ROWS, 1), jnp.float32).at[:2 * ML_HEADS, 0].set(b_gate)
    return pl.pallas_call(
        functools.partial(_mlstm_kernel, chunk=chunk),
        grid=(bsz, nc),
        in_specs=[
            pl.BlockSpec((chunk, ML_V), lambda b, c: (blk(b, c), 0)),
            pl.BlockSpec((chunk, ML_V), lambda b, c: (blk(b, c), 1)),
            pl.BlockSpec((chunk, ML_QK), lambda b, c: (blk(b, c), 4)),
            pl.BlockSpec((ML_QK, chunk), lambda b, c: (0, blk(b, c))),
            pl.BlockSpec((ML_GATE_ROWS, chunk), lambda b, c: (0, blk(b, c))),
            pl.BlockSpec((ML_GATE_ROWS, chunk), lambda b, c: (0, blk(b, jnp.minimum(c + 1, nc - 1)))),
            pl.BlockSpec((ML_GATE_ROWS, 1), lambda b, c: (0, 0)),
            pl.BlockSpec((1, ML_V), lambda b, c: (0, 0)),
        ],
        out_specs=pl.BlockSpec((chunk, ML_V), lambda b, c: (blk(b, c), 0)),
        out_shape=jax.ShapeDtypeStruct((t, ML_V), jnp.bfloat16),
        scratch_shapes=[
            pltpu.VMEM((ML_HEADS, ML_DK, ML_DV + LANES), jnp.float32),
            pltpu.VMEM((ML_GATE_ROWS, LANES), jnp.float32),
            pltpu.VMEM((ML_GATE_ROWS, chunk), jnp.float32),
            pltpu.VMEM((ML_GATE_ROWS, chunk), jnp.float32),
            pltpu.VMEM((ML_GATE_ROWS, chunk), jnp.float32),
        ],
        compiler_params=_params(("arbitrary", "arbitrary")),
        name="mlstm_core",
    )(main, main, main, k_t, g_t, g_t, bg, g_out.reshape(1, ML_V))


def _swa_kernel(sinks_ref, q_ref, kc_ref, kp_ref, vc_ref, vp_ref, gq_ref, gk_ref, bias_ref, o_ref):
    W = SW_WINDOW
    lane = lax.broadcasted_iota(jnp.int32, (W, LANES), 1)
    low = lane < SW_DH
    row_col = lax.broadcasted_iota(jnp.int32, (SW_GROUP * W, 1), 0)
    gq = gq_ref[...]
    gk = gk_ref[...]
    for g in range(SW_KV_HEADS):
        sl = slice(g * LANES, (g + 1) * LANES)
        k2 = jnp.concatenate([kp_ref[:, sl], kc_ref[:, sl]], axis=0).astype(jnp.float32)
        kn = k2 * lax.rsqrt(jnp.mean(k2 * k2, axis=-1, keepdims=True) + EPS)
        kn = (kn * gk).astype(jnp.bfloat16)
        v2 = jnp.concatenate([vp_ref[:, sl], vc_ref[:, sl]], axis=0)
        parts = []
        for p in range(2):
            c0 = g * SW_GROUP * SW_DH + p * LANES
            qp = q_ref[:, c0:c0 + LANES].astype(jnp.float32)
            sq = qp * qp
            ss_lo = jnp.sum(jnp.where(low, sq, 0.0), axis=-1, keepdims=True)
            ss_hi = jnp.sum(jnp.where(low, 0.0, sq), axis=-1, keepdims=True)
            rs = jnp.where(low, lax.rsqrt(ss_lo / SW_DH + EPS), lax.rsqrt(ss_hi / SW_DH + EPS))
            qn = qp * rs * gq
            parts.append(jnp.where(low, qn, 0.0).astype(jnp.bfloat16))
            parts.append(jnp.where(low, 0.0, qn).astype(jnp.bfloat16))
        q4 = jnp.concatenate(parts, axis=0)
        scores = _bdot_nt(q4, kn) + bias_ref[...]
        sink = jnp.full((SW_GROUP * W, 1), sinks_ref[g * SW_GROUP + SW_GROUP - 1], jnp.float32)
        for j in range(SW_GROUP - 2, -1, -1):
            sink = jnp.where(row_col < (j + 1) * W, sinks_ref[g * SW_GROUP + j], sink)
        m = jnp.maximum(jnp.max(scores, axis=-1, keepdims=True), sink)
        pexp = jnp.exp2(scores - m)
        denom = jnp.sum(pexp, axis=-1, keepdims=True) + jnp.exp2(sink - m)
        o4 = _bdot(pexp.astype(jnp.bfloat16), v2) * (1.0 / denom)
        for p in range(2):
            oa = o4[(2 * p) * W:(2 * p + 1) * W]
            ob = o4[(2 * p + 1) * W:(2 * p + 2) * W]
            c0 = g * SW_GROUP * SW_DH + p * LANES
            o_ref[:, c0:c0 + LANES] = jnp.where(low, oa, ob).astype(o_ref.dtype)


def _swa_core(proj, sinks, g_q, g_k, bsz, seq):
    t = proj.shape[0]
    W = SW_WINDOW
    nb = seq // W
    dq = SW_Q_HEADS * SW_DH
    kv_w = SW_KV_HEADS * LANES
    k_blk = dq // kv_w
    v_blk = k_blk + 1
    cur = lambda b, n, s: b * nb + n
    prev = lambda b, n, s: b * nb + jnp.maximum(n - 1, 0)
    gq2 = jnp.concatenate([g_q, g_q]).reshape(1, LANES) * (SW_DH ** -0.5 * LOG2E)
    gk2 = jnp.concatenate([g_k, g_k]).reshape(1, LANES)
    sinks = sinks.astype(jnp.float32) * LOG2E
    qi = (jnp.arange(SW_GROUP * W) % W)[:, None]
    ki = jnp.arange(2 * W)[None, :]
    rel = qi + W - ki
    in_win = (rel >= 0) & (rel < W)
    bias = jnp.stack([jnp.where(in_win & (ki >= W), 0.0, -jnp.inf),
                      jnp.where(in_win, 0.0, -jnp.inf)]).astype(jnp.float32)
    grid_spec = pltpu.PrefetchScalarGridSpec(
        num_scalar_prefetch=1,
        grid=(bsz, nb),
        in_specs=[
            pl.BlockSpec((W, dq), lambda b, n, s: (cur(b, n, s), 0)),
            pl.BlockSpec((W, kv_w), lambda b, n, s: (cur(b, n, s), k_blk)),
            pl.BlockSpec((W, kv_w), lambda b, n, s: (prev(b, n, s), k_blk)),
            pl.BlockSpec((W, kv_w), lambda b, n, s: (cur(b, n, s), v_blk)),
            pl.BlockSpec((W, kv_w), lambda b, n, s: (prev(b, n, s), v_blk)),
            pl.BlockSpec((1, LANES), lambda b, n, s: (0, 0)),
            pl.BlockSpec((1, LANES), lambda b, n, s: (0, 0)),
            pl.BlockSpec((None, SW_GROUP * W, 2 * W), lambda b, n, s: (jnp.minimum(n, 1), 0, 0)),
        ],
        out_specs=pl.BlockSpec((W, dq), lambda b, n, s: (cur(b, n, s), 0)),
    )
    return pl.pallas_call(
        _swa_kernel,
        grid_spec=grid_spec,
        out_shape=jax.ShapeDtypeStruct((t, dq), jnp.bfloat16),
        compiler_params=_params(("arbitrary", "arbitrary")),
        name="swa_core",
    )(sinks.astype(jnp.float32), proj, proj, proj, proj, proj, gq2, gk2, bias)


def _pack_rounded_pairs(r):
    k = r.shape[1] // 2
    hi = lax.bitcast_convert_type(r[:, :k], jnp.uint32)
    lo = lax.bitcast_convert_type(r[:, k:], jnp.uint32)
    return hi | (lo >> 16)


def _pack_bf16_pairs(a):
    return _pack_rounded_pairs(a.astype(jnp.bfloat16).astype(jnp.float32))


def _unpack_bf16_pairs(u):
    hi = lax.bitcast_convert_type(u & jnp.uint32(0xFFFF0000), jnp.float32)
    lo = lax.bitcast_convert_type(u << 16, jnp.float32)
    return hi, lo


def _router_kernel(a_ref, wo_ref, x_ref, gt_ref, g_ref, sc_ref, sh_ref, wh_ref, wl_ref, b_ref,
                   xo_ref, hn_ref, info_ref, infot_ref, cnt_ref, carry_ref, lower_ref, *, tm):
    @pl.when(pl.program_id(0) == 0)
    def _():
        carry_ref[...] = jnp.zeros_like(carry_ref)
        r_idx = lax.broadcasted_iota(jnp.int32, (tm, tm), 0)
        c_idx = lax.broadcasted_iota(jnp.int32, (tm, tm), 1)
        lower_ref[...] = jnp.where(c_idx < r_idx, 1.0, 0.0).astype(jnp.bfloat16)

    x_new = x_ref[...] + gt_ref[...] * _bdot(a_ref[...], wo_ref[...])
    xo_ref[...] = x_new
    hn = _modulated_norm(x_new, g_ref[...], sc_ref[...], sh_ref[...])
    h_hi = hn.astype(jnp.bfloat16)
    hi_f32 = h_hi.astype(jnp.float32)
    h_lo = (hn - hi_f32).astype(jnp.bfloat16)
    hn_ref[...] = _pack_rounded_pairs(hi_f32)
    wide = _bdot(h_hi, wl_ref[...])
    logits = (wide[:, :ROUTER_LANES] + (_bdot(h_lo, wh_ref[...]) + wide[:, ROUTER_LANES:])
              + b_ref[...])
    lane = lax.broadcasted_iota(jnp.int32, (tm, ROUTER_LANES), 1).astype(jnp.float32)
    big = float(4 * ROUTER_LANES)
    neg = -jnp.inf

    is_grp = (lane >= N_EXPERTS) & (lane < N_EXPERTS + N_GROUPS)
    gl = jnp.where(is_grp, logits, neg)
    gmax = jnp.max(gl, axis=-1, keepdims=True)
    gsel = jnp.min(jnp.where(gl == gmax, lane, big), axis=-1, keepdims=True) - N_EXPERTS
    p_grp = 1.0 / jnp.sum(jnp.exp(gl - gmax), axis=-1, keepdims=True)

    lo_lane = gsel * EXPERTS_PER_GROUP
    in_grp = (lane >= lo_lane) & (lane < lo_lane + EXPERTS_PER_GROUP)
    el = jnp.where(in_grp, logits, neg)
    v1 = jnp.max(el, axis=-1, keepdims=True)
    i1 = jnp.min(jnp.where(el == v1, lane, big), axis=-1, keepdims=True)
    el2 = jnp.where(lane == i1, neg, el)
    v2 = jnp.max(el2, axis=-1, keepdims=True)
    i2 = jnp.min(jnp.where(el2 == v2, lane, big), axis=-1, keepdims=True)
    e21 = jnp.exp(v2 - v1)
    gate1 = p_grp / (1.0 + e21)
    gate2 = p_grp * e21 / (1.0 + e21)

    hit1 = lane == i1
    hit2 = lane == i2
    onehot = jnp.where(hit1 | hit2, 1.0, 0.0)
    before = _bdot(lower_ref[...], onehot.astype(jnp.bfloat16)) + carry_ref[0:1, :]
    rank1 = jnp.sum(jnp.where(hit1, before, 0.0), axis=-1, keepdims=True)
    rank2 = jnp.sum(jnp.where(hit2, before, 0.0), axis=-1, keepdims=True)
    total = carry_ref[0:1, :] + jnp.sum(onehot, axis=0, keepdims=True)
    carry_ref[...] = jnp.broadcast_to(total, carry_ref.shape)
    cnt_ref[...] = jnp.broadcast_to(total, cnt_ref.shape)

    info = jnp.where(lane == 0, i1, 0.0)
    info = jnp.where(lane == 1, i2, info)
    info = jnp.where(lane == 2, rank1, info)
    info = jnp.where(lane == 3, rank2, info)
    info = jnp.where(lane == 4, gate1, info)
    info = jnp.where(lane == 5, gate2, info)
    info_ref[...] = info
    infot_ref[...] = info.T[0:8, :]


def _outproj_router(a, w_out, x2d, gt, g, sc, sh, w_hi, w_lo, bias, seq):
    t, d = x2d.shape
    tm = 512
    tiles_per_batch = seq // tm
    per_batch = lambda i: (i // tiles_per_batch, 0, 0)
    const = lambda i: (0, 0)
    return pl.pallas_call(
        functools.partial(_router_kernel, tm=tm),
        grid=(t // tm,),
        in_specs=[
            pl.BlockSpec((tm, a.shape[1]), lambda i: (i, 0)),
            pl.BlockSpec(w_out.shape, const),
            pl.BlockSpec((tm, d), lambda i: (i, 0)),
            pl.BlockSpec((None, 1, d), per_batch),
            pl.BlockSpec((1, d), const),
            pl.BlockSpec((None, 1, d), per_batch),
            pl.BlockSpec((None, 1, d), per_batch),
            pl.BlockSpec((d, ROUTER_LANES), const),
            pl.BlockSpec((d, 2 * ROUTER_LANES), const),
            pl.BlockSpec((1, ROUTER_LANES), const),
        ],
        out_specs=[
            pl.BlockSpec((tm, d), lambda i: (i, 0)),
            pl.BlockSpec((tm, d // 2), lambda i: (i, 0)),
            pl.BlockSpec((tm, ROUTER_LANES), lambda i: (i, 0)),
            pl.BlockSpec((8, tm), lambda i: (0, i)),
            pl.BlockSpec((8, ROUTER_LANES), const),
        ],
        out_shape=[
            jax.ShapeDtypeStruct((t, d), jnp.float32),
            jax.ShapeDtypeStruct((t, d // 2), jnp.uint32),
            jax.ShapeDtypeStruct((t, ROUTER_LANES), jnp.float32),
            jax.ShapeDtypeStruct((8, t), jnp.float32),
            jax.ShapeDtypeStruct((8, ROUTER_LANES), jnp.float32),
        ],
        scratch_shapes=[pltpu.VMEM((8, ROUTER_LANES), jnp.float32),
                        pltpu.VMEM((tm, tm), jnp.bfloat16)],
        compiler_params=_params(("arbitrary",)),
        name="outproj_router",
    )(a, w_out, x2d, gt, g.reshape(1, d), sc, sh, w_hi, jnp.concatenate([w_hi, w_lo], axis=1), bias)


def _sc_mesh():
    return plsc.VectorSubcoreMesh(core_axis_name="c", subcore_axis_name="s")


def _sc_dispatch(rows, d0, d1, pad_idx, n_slots):
    t, w = rows.shape
    n_pad = pad_idx.shape[1]
    zeros = jnp.zeros((SC_WINDOW, w), rows.dtype)
    sem = (pltpu.PARALLEL, pltpu.ARBITRARY)

    @pl.kernel(out_type=jax.ShapeDtypeStruct((n_slots + SC_WINDOW, w), rows.dtype), mesh=_sc_mesh())
    def dispatch(x_hbm, d0_hbm, d1_hbm, z_hbm, p_hbm, o_hbm):
        def scatter_rows(x_vmem, i0_vmem, i1_vmem):
            cols = pl.ds(pl.program_id(1) * SC_COLS, SC_COLS)
            pltpu.sync_copy(x_vmem, o_hbm.at[i0_vmem.at[0], cols])
            pltpu.sync_copy(x_vmem, o_hbm.at[i1_vmem.at[0], cols])

        pltpu.emit_pipeline(
            scatter_rows,
            grid=(t // SC_WINDOW, w // SC_COLS),
            in_specs=[pl.BlockSpec((SC_WINDOW, SC_COLS), lambda i, j: (i, j)),
                      pl.BlockSpec((1, SC_WINDOW), lambda i, j: (0, i)),
                      pl.BlockSpec((1, SC_WINDOW), lambda i, j: (0, i))],
            out_specs=[],
            core_axis_name=("c", "s"),
            dimension_semantics=sem,
        )(x_hbm, d0_hbm, d1_hbm)

        def scatter_zeros(z_vmem, p_vmem):
            cols = pl.ds(pl.program_id(1) * SC_COLS, SC_COLS)
            pltpu.sync_copy(z_vmem, o_hbm.at[p_vmem.at[0], cols])

        pltpu.emit_pipeline(
            scatter_zeros,
            grid=(n_pad // SC_WINDOW, w // SC_COLS),
            in_specs=[pl.BlockSpec((SC_WINDOW, SC_COLS), lambda i, j: (0, j)),
                      pl.BlockSpec((1, SC_WINDOW), lambda i, j: (0, i))],
            out_specs=[],
            core_axis_name=("c", "s"),
            dimension_semantics=sem,
        )(z_hbm, p_hbm)

    return dispatch(rows, d0, d1, zeros, pad_idx)


def _sc_gather(src, idx):
    n_out = idx.shape[1]
    w = src.shape[1]

    @pl.kernel(out_type=jax.ShapeDtypeStruct((n_out, w), src.dtype), mesh=_sc_mesh())
    def gather(x_hbm, i_hbm, o_hbm):
        def gather_rows(i_vmem, o_vmem):
            cols = pl.ds(pl.program_id(1) * SC_COLS, SC_COLS)
            pltpu.sync_copy(x_hbm.at[i_vmem.at[0], cols], o_vmem)

        pltpu.emit_pipeline(
            gather_rows,
            grid=(n_out // SC_WINDOW, w // SC_COLS),
            in_specs=[pl.BlockSpec((1, SC_WINDOW), lambda i, j: (0, i))],
            out_specs=[pl.BlockSpec((SC_WINDOW, SC_COLS), lambda i, j: (i, j))],
            core_axis_name=("c", "s"),
            dimension_semantics=(pltpu.PARALLEL, pltpu.ARBITRARY),
        )(i_hbm, o_hbm)

    return gather(src, idx)


def _expert_kernel(meta_ref, x_ref, w1_hbm, w3_hbm, w2_hbm, y_ref,
                   w1_buf, w3_buf, w2_buf, w1_c, w3_c, w2_c, sems, *, layer, nb):
    i = pl.program_id(0)
    e = meta_ref[i]
    slot = meta_ref[nb + i]
    nxt = meta_ref[2 * nb + i]
    used = i < meta_ref[3 * nb]
    first = used & ((i == 0) | (e != meta_ref[jnp.maximum(i - 1, 0)]))

    def weight_copies(expert, s):
        return (pltpu.make_async_copy(w1_hbm.at[layer, expert], w1_buf.at[s], sems.at[s, 0]),
                pltpu.make_async_copy(w3_hbm.at[layer, expert], w3_buf.at[s], sems.at[s, 1]),
                pltpu.make_async_copy(w2_hbm.at[layer, expert], w2_buf.at[s], sems.at[s, 2]))

    @pl.when(i == 0)
    def _():
        for cp in weight_copies(e, slot):
            cp.start()

    @pl.when(first)
    def _():
        for cp in weight_copies(e, slot):
            cp.wait()

        @pl.when(nxt >= 0)
        def _():
            for cp in weight_copies(nxt, 1 - slot):
                cp.start()

        w1_c[...] = w1_buf[slot].astype(jnp.bfloat16)
        w3_c[...] = w3_buf[slot].astype(jnp.bfloat16)
        w2_c[...] = w2_buf[slot].astype(jnp.bfloat16)

    @pl.when(used)
    def _():
        x_hi, x_lo = _unpack_bf16_pairs(x_ref[...])
        xb = jnp.concatenate([x_hi, x_lo], axis=-1).astype(jnp.bfloat16)
        h1 = _bdot(xb, w1_c[...])
        h3 = _bdot(xb, w3_c[...])
        act = (h1 * jax.nn.sigmoid(h1) * h3).astype(jnp.bfloat16)
        y_ref[...] = _pack_bf16_pairs(_bdot(act, w2_c[...]))

    @pl.when(jnp.logical_not(used))
    def _():
        y_ref[...] = jnp.zeros_like(y_ref)


def _experts(x_slots, blk_meta, w1, w3, w2, layer, nb):
    dp = x_slots.shape[1]
    d, de = w1.shape[-2:]
    rowblk = lambda i, s: (jnp.minimum(i, s[3 * nb] - 1), 0)
    grid_spec = pltpu.PrefetchScalarGridSpec(
        num_scalar_prefetch=1,
        grid=(nb,),
        in_specs=[
            pl.BlockSpec((MOE_BLOCK, dp), rowblk),
            pl.BlockSpec(memory_space=pl.ANY),
            pl.BlockSpec(memory_space=pl.ANY),
            pl.BlockSpec(memory_space=pl.ANY),
        ],
        out_specs=pl.BlockSpec((MOE_BLOCK, dp), lambda i, s: (i, 0)),
        scratch_shapes=[
            pltpu.VMEM((2, d, de), jnp.float32),
            pltpu.VMEM((2, d, de), jnp.float32),
            pltpu.VMEM((2, de, d), jnp.float32),
            pltpu.VMEM((d, de), jnp.bfloat16),
            pltpu.VMEM((d, de), jnp.bfloat16),
            pltpu.VMEM((de, d), jnp.bfloat16),
            pltpu.SemaphoreType.DMA((2, 3)),
        ],
    )
    return pl.pallas_call(
        functools.partial(_expert_kernel, layer=layer, nb=nb),
        grid_spec=grid_spec,
        out_shape=jax.ShapeDtypeStruct((nb * MOE_BLOCK, dp), jnp.uint32),
        compiler_params=_params(("arbitrary",)),
        name="moe_experts",
    )(blk_meta, x_slots, w1, w3, w2)


def _combine_kernel(x_ref, y1_ref, y2_ref, info_ref, gt_ref, o_ref):
    o_ref[...] = _moe_combined(x_ref[...], y1_ref, y2_ref, info_ref, gt_ref)


def _combine(x2d, y_pairs, info, gt, seq):
    t, d = x2d.shape
    tm = min(1024, seq)
    tiles_per_batch = seq // tm
    second = t // tm
    return pl.pallas_call(
        _combine_kernel,
        grid=(t // tm,),
        in_specs=[
            pl.BlockSpec((tm, d), lambda i: (i, 0)),
            pl.BlockSpec((tm, d // 2), lambda i: (i, 0)),
            pl.BlockSpec((tm, d // 2), lambda i: (i + second, 0)),
            pl.BlockSpec((tm, ROUTER_LANES), lambda i: (i, 0)),
            pl.BlockSpec((None, 1, d), lambda i: (i // tiles_per_batch, 0, 0)),
        ],
        out_specs=pl.BlockSpec((tm, d), lambda i: (i, 0)),
        out_shape=jax.ShapeDtypeStruct((t, d), jnp.float32),
        compiler_params=_params(("arbitrary",)),
        name="moe_combine",
    )(x2d, y_pairs, y_pairs, info, gt)


def _slot_plan(info_t, cnt, t):
    counts = cnt[0, :N_EXPERTS].astype(jnp.int32)
    padded = (counts + MOE_BLOCK - 1) // MOE_BLOCK * MOE_BLOCK
    pad_ends = jnp.cumsum(padded)
    pad_starts = pad_ends - padded
    nb = -(-(2 * t) // MOE_BLOCK) + N_EXPERTS
    n_slots = nb * MOE_BLOCK
    it = info_t.astype(jnp.int32)
    onehot_start = lambda e: jnp.sum(
        jnp.where(e[None, :] == jnp.arange(N_EXPERTS, dtype=jnp.int32)[:, None],
                  pad_starts[:, None], 0), axis=0)
    dest1 = (onehot_start(it[0]) + it[2]).reshape(1, t)
    dest2 = (onehot_start(it[1]) + it[3]).reshape(1, t)
    lane = jnp.arange(MOE_BLOCK, dtype=jnp.int32)[None, :]
    n_padding = (padded - counts)[:, None]
    wrapped = (pad_starts + counts)[:, None] + lane % jnp.maximum(n_padding, 1)
    pad_idx = jnp.where(n_padding > 0, wrapped, n_slots + lane % SC_WINDOW).reshape(1, -1)
    experts = jnp.arange(N_EXPERTS, dtype=jnp.int32)
    blk = jnp.arange(nb, dtype=jnp.int32)
    blk_exp = jnp.minimum(
        jnp.sum((pad_ends[None, :] <= (blk * MOE_BLOCK)[:, None]).astype(jnp.int32), axis=1),
        N_EXPERTS - 1)
    n_used = pad_ends[-1] // MOE_BLOCK
    prev_exp = jnp.concatenate([jnp.full((1,), -1, jnp.int32), blk_exp[:-1]])
    is_first = (blk < n_used) & (blk_exp != prev_exp)
    blk_slot = (jnp.cumsum(is_first.astype(jnp.int32)) + 1) % 2
    later = (experts[None, :] > experts[:, None]) & (padded[None, :] > 0)
    nxt_of = jnp.min(jnp.where(later, experts[None, :], N_EXPERTS), axis=1)
    nxt_of = jnp.where(nxt_of == N_EXPERTS, -1, nxt_of)
    blk_nxt = jnp.sum(jnp.where(blk_exp[:, None] == experts[None, :], nxt_of[None, :], 0), axis=1)
    blk_meta = jnp.concatenate([blk_exp, blk_slot, blk_nxt, n_used[None]]).astype(jnp.int32)
    return dest1, dest2, pad_idx, blk_meta, nb, n_slots


def _mixer_out_and_moe(a, w_out, x2d, gt1, g, sc, sh, w_group, b_group, w_router, b_router,
                       w1, w3, w2, layer, seq):
    t, d = x2d.shape
    w_cat = jnp.zeros((d, ROUTER_LANES), jnp.float32)
    w_cat = w_cat.at[:, :N_EXPERTS].set(w_router).at[:, N_EXPERTS:N_EXPERTS + N_GROUPS].set(w_group)
    b_cat = jnp.zeros((1, ROUTER_LANES), jnp.float32)
    b_cat = b_cat.at[0, :N_EXPERTS].set(b_router).at[0, N_EXPERTS:N_EXPERTS + N_GROUPS].set(b_group)
    w_hi, w_lo = _split_hi_lo(w_cat)
    x_new, hn, info, info_t, cnt = _outproj_router(a, w_out, x2d, gt1, g, sc, sh, w_hi, w_lo,
                                                   b_cat, seq)
    dest1, dest2, pad_idx, blk_meta, nb, n_slots = _slot_plan(info_t, cnt, t)
    x_slots = _sc_dispatch(hn, dest1, dest2, pad_idx, n_slots)
    y_slots = _experts(x_slots, blk_meta, w1, w3, w2, layer, nb)
    y_pairs = _sc_gather(y_slots, jnp.concatenate([dest1, dest2], axis=1))
    return x_new, y_pairs, info


def kernel(x, c, w_ada, b_ada, norm1_g, norm2_g, ml_w_in, ml_b_gate, ml_g_out, ml_w_out,
           sw_w_in, sw_g_q, sw_g_k, sw_sinks, sw_w_out, moe_w_group, moe_b_group,
           moe_w_router, moe_b_router, moe_w1, moe_w3, moe_w2):
    bsz, seq, d = x.shape
    depth = w_ada.shape[0]
    bf = jnp.bfloat16
    mod = _ada_mod(c, w_ada, b_ada)
    x2d = x.reshape(bsz * seq, d)
    pending = None
    for layer in range(depth):
        sh1, sc1, gt1, sh2, sc2, gt2 = [
            mod[layer, :, i * d:(i + 1) * d].reshape(bsz, 1, d) for i in range(6)]
        j = layer // 2
        if layer % 2 == 0:
            w = ml_w_in[j]
            q_w, k_w = w[:, :ML_QK], w[:, ML_QK:2 * ML_QK]
            v_w = w[:, 2 * ML_QK:2 * ML_QK + ML_V]
            o_w = w[:, 2 * ML_QK + ML_V:2 * ML_QK + 2 * ML_V]
            g_w = w[:, 2 * ML_QK + 2 * ML_V:]
            w_main = jnp.concatenate([v_w, o_w, q_w], axis=1).astype(bf)
            wg_t = jnp.zeros((ML_GATE_ROWS, d), jnp.float32).at[:2 * ML_HEADS].set(g_w.T)
            wg_hi, wg_lo = _split_hi_lo(wg_t)
            wk_t = k_w.T.astype(bf)
            w_out = ml_w_out[j].astype(bf)
            outs = _inproj(x2d, norm1_g[layer], sc1, sh1, w_main, seq,
                           ml_extra=(wk_t, wg_hi, wg_lo),
                           q_cols=(2 * ML_V, 2 * ML_V + ML_QK), q_scale=ML_DK ** -0.5,
                           gate_cols=(ML_V, 2 * ML_V), pending=pending)
            if pending is not None:
                x2d, outs = outs[0], outs[1:]
            main, k_t, g_t = outs
            a = _mlstm_core(main, k_t, g_t, ml_b_gate[j], ml_g_out[j], bsz, seq)
        else:
            w = sw_w_in[j]
            dq = SW_Q_HEADS * SW_DH
            dkv = SW_KV_HEADS * SW_DH
            dup = lambda m: jnp.concatenate(
                [m.reshape(d, SW_KV_HEADS, 1, SW_DH)] * 2, axis=2).reshape(d, 2 * dkv)
            w_main = jnp.concatenate(
                [w[:, :dq], dup(w[:, dq:dq + dkv]), dup(w[:, dq + dkv:])], axis=1).astype(bf)
            w_out = sw_w_out[j].astype(bf)
            outs = _inproj(x2d, norm1_g[layer], sc1, sh1, w_main, seq, pending=pending)
            if pending is not None:
                x2d, outs = outs[0], outs[1:]
            a = _swa_core(outs[0], sw_sinks[j], sw_g_q[j], sw_g_k[j], bsz, seq)
        x2d, y_pairs, info = _mixer_out_and_moe(
            a, w_out, x2d, gt1, norm2_g[layer], sc2, sh2, moe_w_group[layer], moe_b_group[layer],
            moe_w_router[layer], moe_b_router[layer], moe_w1, moe_w3, moe_w2, layer, seq)
        pending = (y_pairs, info, gt2)
    y_pairs, info, gt2 = pending
    return _combine(x2d, y_pairs, info, gt2, seq).reshape(bsz, seq, d)
```

```python
import functools

import jax
import jax.numpy as jnp
from jax import lax
from jax.experimental import pallas as pl
from jax.experimental.pallas import tpu as pltpu
from jax.experimental.pallas import tpu_sc as plsc

EPS = 1e-6
GATE_CAP = 15.0
LOG2E = 1.4426950408889634

ML_HEADS = 4
ML_DK = 128
ML_DV = 256
ML_QK = ML_HEADS * ML_DK
ML_V = ML_HEADS * ML_DV
ML_GATE_ROWS = 16

SW_Q_HEADS = 16
SW_KV_HEADS = 4
SW_GROUP = SW_Q_HEADS // SW_KV_HEADS
SW_DH = 64
SW_WINDOW = 128
LANES = 128

N_GROUPS = 8
EXPERTS_PER_GROUP = 8
N_EXPERTS = N_GROUPS * EXPERTS_PER_GROUP
MOE_BLOCK = 256
ROUTER_LANES = 128
SC_WINDOW = 128
SC_COLS = 256
EXPERT_BLOCKS_PER_STEP = 2

VMEM_LIMIT = 56 * 1024 * 1024

_NT = (((1,), (1,)), ((), ()))


def _bdot(a, b):
    return jnp.dot(a, b, preferred_element_type=jnp.float32)


def _bdot_nt(a, b):
    return lax.dot_general(a, b, _NT, preferred_element_type=jnp.float32)


def _split_hi_lo(a):
    hi = a.astype(jnp.bfloat16)
    lo = (a - hi.astype(jnp.float32)).astype(jnp.bfloat16)
    return hi, lo


def _params(sem):
    return pltpu.CompilerParams(dimension_semantics=sem, vmem_limit_bytes=VMEM_LIMIT)


def _ada_kernel(c_ref, w_ref, b_ref, o_ref):
    c = c_ref[...]
    cond = c * jax.nn.sigmoid(c)
    c_hi, c_lo = _split_hi_lo(cond)
    w_hi, w_lo = _split_hi_lo(w_ref[...])
    acc = _bdot(c_hi, w_hi) + (_bdot(c_lo, w_hi) + _bdot(c_hi, w_lo))
    o_ref[...] = acc + b_ref[...]


def _ada_mod(c, w_ada, b_ada):
    depth, d, n = w_ada.shape
    bsz = c.shape[0]
    rows = 8
    tn = 768
    c_pad = jnp.zeros((rows, d), jnp.float32).at[:bsz].set(c)
    out = pl.pallas_call(
        _ada_kernel,
        grid=(depth, n // tn),
        in_specs=[
            pl.BlockSpec((rows, d), lambda l, j: (0, 0)),
            pl.BlockSpec((None, d, tn), lambda l, j: (l, 0, j)),
            pl.BlockSpec((None, 1, tn), lambda l, j: (l, 0, j)),
        ],
        out_specs=pl.BlockSpec((None, rows, tn), lambda l, j: (l, 0, j)),
        out_shape=jax.ShapeDtypeStruct((depth, rows, n), jnp.float32),
        compiler_params=_params(("arbitrary", "arbitrary")),
        name="ada_mod",
    )(c_pad, w_ada, b_ada.reshape(depth, 1, n))
    return out[:, :bsz]


def _modulated_norm(x, g, sc, sh):
    y = x * lax.rsqrt(jnp.mean(x * x, axis=-1, keepdims=True) + EPS)
    return y * (g * (1.0 + sc)) + sh


def _moe_combined(x, y1_ref, y2_ref, info_ref, gt_ref):
    info = info_ref[...]
    g1 = info[:, 4:5]
    g2 = info[:, 5:6]
    y1_hi, y1_lo = _unpack_bf16_pairs(y1_ref[...])
    y2_hi, y2_lo = _unpack_bf16_pairs(y2_ref[...])
    y = jnp.concatenate([g1 * y1_hi + g2 * y2_hi, g1 * y1_lo + g2 * y2_lo], axis=-1)
    return x + gt_ref[...] * y


def _inproj_kernel(*refs, n_main, chunk, q_cols, q_scale, gate_cols, with_ml, with_combine):
    refs = list(refs)
    n_in = 5 + (4 if with_combine else 0) + (3 if with_ml else 0)
    ins, outs = refs[:n_in], refs[n_in:]
    x_ref = ins.pop(0)
    x = x_ref[...]
    if with_combine:
        y1_ref, y2_ref, info_ref, gtp_ref = ins[:4]
        ins = ins[4:]
        x = _moe_combined(x, y1_ref, y2_ref, info_ref, gtp_ref)
        outs.pop(0)[...] = x
    g_ref, sc_ref, sh_ref, w_ref = ins[:4]
    o_ref = outs[0]
    if with_ml:
        wk_ref, wgh_ref, wgl_ref = ins[4:]
        kt_ref, gt_ref = outs[1:]
    hn = _modulated_norm(x, g_ref[...], sc_ref[...], sh_ref[...])
    hb = hn.astype(jnp.bfloat16)
    for c0 in range(0, n_main, chunk):
        acc = _bdot(hb, w_ref[:, c0:c0 + chunk])
        if q_cols is not None and q_cols[0] <= c0 < q_cols[1]:
            acc = acc * q_scale
        if gate_cols is not None and gate_cols[0] <= c0 < gate_cols[1]:
            acc = jax.nn.sigmoid(acc)
        o_ref[:, c0:c0 + chunk] = acc.astype(o_ref.dtype)
    if with_ml:
        kt_ref[...] = _bdot_nt(wk_ref[...], hb).astype(kt_ref.dtype)
        h_lo = (hn - hb.astype(jnp.float32)).astype(jnp.bfloat16)
        gt_ref[...] = (_bdot_nt(wgh_ref[...], hb)
                       + (_bdot_nt(wgh_ref[...], h_lo) + _bdot_nt(wgl_ref[...], hb)))


def _inproj(x2d, g, sc, sh, w_main, seq, *, ml_extra=None, q_cols=None, q_scale=1.0,
            gate_cols=None, pending=None):
    t, d = x2d.shape
    tm = 512
    n_main = w_main.shape[1]
    tiles_per_batch = seq // tm
    row = lambda i: (i, 0)
    per_batch = lambda i: (i // tiles_per_batch, 0, 0)
    const = lambda i: (0, 0)
    in_specs = [pl.BlockSpec((tm, d), row)]
    args = [x2d]
    out_specs, out_shape = [], []
    if pending is not None:
        y_pairs, info, gt_prev = pending
        second = t // tm
        in_specs += [pl.BlockSpec((tm, d // 2), row),
                     pl.BlockSpec((tm, d // 2), lambda i: (i + second, 0)),
                     pl.BlockSpec((tm, ROUTER_LANES), row),
                     pl.BlockSpec((None, 1, d), per_batch)]
        args += [y_pairs, y_pairs, info, gt_prev]
        out_specs += [pl.BlockSpec((tm, d), row)]
        out_shape += [jax.ShapeDtypeStruct((t, d), jnp.float32)]
    in_specs += [
        pl.BlockSpec((1, d), const),
        pl.BlockSpec((None, 1, d), per_batch),
        pl.BlockSpec((None, 1, d), per_batch),
        pl.BlockSpec((d, n_main), const),
    ]
    args += [g.reshape(1, d), sc, sh, w_main]
    out_specs += [pl.BlockSpec((tm, n_main), row)]
    out_shape += [jax.ShapeDtypeStruct((t, n_main), jnp.bfloat16)]
    if ml_extra is not None:
        wk_t, wg_hi, wg_lo = ml_extra
        in_specs += [pl.BlockSpec(wk_t.shape, const),
                     pl.BlockSpec(wg_hi.shape, const),
                     pl.BlockSpec(wg_lo.shape, const)]
        args += [wk_t, wg_hi, wg_lo]
        out_specs += [pl.BlockSpec((wk_t.shape[0], tm), lambda i: (0, i)),
                      pl.BlockSpec((ML_GATE_ROWS, tm), lambda i: (0, i))]
        out_shape += [jax.ShapeDtypeStruct((wk_t.shape[0], t), jnp.bfloat16),
                      jax.ShapeDtypeStruct((ML_GATE_ROWS, t), jnp.float32)]
    kern = functools.partial(_inproj_kernel, n_main=n_main, chunk=512, q_cols=q_cols,
                             q_scale=q_scale, gate_cols=gate_cols, with_ml=ml_extra is not None,
                             with_combine=pending is not None)
    return pl.pallas_call(
        kern,
        grid=(t // tm,),
        in_specs=in_specs,
        out_specs=out_specs,
        out_shape=out_shape,
        compiler_params=_params(("arbitrary",)),
        name="inproj_ml" if ml_extra is not None else "inproj_sw",
    )(*args)


def _mlstm_gate_terms(graw, bias, upper):
    H = ML_HEADS
    L = graw.shape[1]
    z = graw + bias
    gates = GATE_CAP * jnp.tanh(z / GATE_CAP)
    log_f = jnp.minimum(gates, 0.0) - jnp.log1p(jnp.exp(-jnp.abs(gates)))
    row = lax.broadcasted_iota(jnp.int32, (ML_GATE_ROWS, L), 0)
    lane = lax.broadcasted_iota(jnp.int32, (ML_GATE_ROWS, L), 1)
    is_i = row < H
    slab = jnp.where(is_i, gates, log_f)
    a1 = slab.astype(jnp.bfloat16)
    r1 = slab - a1.astype(jnp.float32)
    a2 = r1.astype(jnp.bfloat16)
    a3 = (r1 - a2.astype(jnp.float32)).astype(jnp.bfloat16)
    cum = _bdot(a1, upper) + (_bdot(a2, upper) + _bdot(a3, upper))
    ib = jnp.where(is_i, gates, cum)
    b = pltpu.roll(ib, ML_GATE_ROWS - H, 0)
    u = ib - b
    cm = u
    shift = 1
    while shift < L:
        cm = jnp.maximum(cm, jnp.where(lane >= shift, pltpu.roll(cm, shift, 1), -jnp.inf))
        shift *= 2
    return b, u, cm


def _mlstm_kernel(v_ref, o_ref, q_ref, kt_ref, gt_ref, gtn_ref, bg_ref, gout_ref, out_ref,
                  c_ref, m_ref, b_ref, u_ref, cm_ref, *, chunk):
    L = chunk
    H, dk, dv = ML_HEADS, ML_DK, ML_DV
    r_idx = lax.broadcasted_iota(jnp.int32, (L, L), 0)
    c_idx = lax.broadcasted_iota(jnp.int32, (L, L), 1)
    upper = jnp.where(r_idx <= c_idx, 1.0, 0.0).astype(jnp.bfloat16)
    causal = r_idx >= c_idx
    row = lax.broadcasted_iota(jnp.int32, (ML_GATE_ROWS, L), 0)
    ones_col = jnp.where(lax.broadcasted_iota(jnp.int32, (L, LANES), 1) == 0, 1.0, 0.0
                         ).astype(jnp.bfloat16)

    @pl.when(pl.program_id(1) == 0)
    def _():
        c_ref[...] = jnp.zeros_like(c_ref)
        m_ref[...] = jnp.zeros_like(m_ref)
        b0, u0, cm0 = _mlstm_gate_terms(gt_ref[...], bg_ref[...], upper)
        b_ref[...] = b0
        u_ref[...] = u0
        cm_ref[...] = cm0

    b16 = b_ref[...]
    u16 = u_ref[...]
    cm16 = cm_ref[...]
    b_n, u_n, cm_n = _mlstm_gate_terms(gtn_ref[...], bg_ref[...], upper)
    b_ref[...] = b_n
    u_ref[...] = u_n
    cm_ref[...] = cm_n

    m_prev = m_ref[:, 0:1]
    z16 = jnp.maximum(m_prev, cm16)
    w_inter16 = jnp.exp(m_prev - z16)
    e_negm16 = jnp.exp(-(b16 + z16))
    z_last = z16[:, L - 1:L]
    w_state16 = jnp.exp(u16 - z_last)
    decay16 = jnp.exp(m_prev - z_last)
    m_ref[...] = jnp.broadcast_to(b16[:, L - 1:L] + z_last, m_ref.shape)
    stacked = jnp.where(row < H, z16,
                        jnp.where(row < 2 * H, pltpu.roll(w_inter16, H, 0),
                                  pltpu.roll(e_negm16, 2 * H, 0)))
    cols = jnp.concatenate(
        [stacked, jnp.zeros((LANES - ML_GATE_ROWS, L), jnp.float32)], axis=0).T

    for h in range(H):
        u_r = u16[h:h + 1, :]
        z_c = cols[:, h:h + 1]
        w_inter = cols[:, H + h:H + h + 1]
        e_negm = cols[:, 2 * H + h:2 * H + h + 1]
        c_ext = c_ref[h]
        q = q_ref[:, h * dk:(h + 1) * dk]
        kt = kt_ref[h * dk:(h + 1) * dk, :]
        v_ext = jnp.concatenate([v_ref[:, h * dv:(h + 1) * dv], ones_col], axis=-1)

        w_intra = jnp.exp(jnp.where(causal, u_r - z_c, -jnp.inf))
        s = (_bdot(q, kt) * w_intra).astype(jnp.bfloat16)
        nd = w_inter * _bdot(q, c_ext.astype(jnp.bfloat16)) + _bdot(s, v_ext)
        den = nd[:, dv:dv + 1]
        hb = nd[:, :dv] * (1.0 / jnp.maximum(jnp.abs(den), e_negm))

        kw = (kt.astype(jnp.float32) * w_state16[h:h + 1, :]).astype(jnp.bfloat16)
        c_ref[h] = decay16[h:h + 1, :] * c_ext + _bdot(kw, v_ext)

        y = hb * lax.rsqrt(jnp.mean(hb * hb, axis=-1, keepdims=True) + EPS)
        y = y * gout_ref[:, h * dv:(h + 1) * dv]
        og = o_ref[:, h * dv:(h + 1) * dv].astype(jnp.float32)
        out_ref[:, h * dv:(h + 1) * dv] = (y * og).astype(out_ref.dtype)


def _mlstm_core(main, k_t, g_t, b_gate, g_out, bsz, seq, chunk=256):
    t = main.shape[0]
    nc = seq // chunk
    blk = lambda b, c: b * nc + c
    bg = jnp.zeros((ML_GATE_ROWS, 1), jnp.float32).at[:2 * ML_HEADS, 0].set(b_gate)
    return pl.pallas_call(
        functools.partial(_mlstm_kernel, chunk=chunk),
        grid=(bsz, nc),
        in_specs=[
            pl.BlockSpec((chunk, ML_V), lambda b, c: (blk(b, c), 0)),
            pl.BlockSpec((chunk, ML_V), lambda b, c: (blk(b, c), 1)),
            pl.BlockSpec((chunk, ML_QK), lambda b, c: (blk(b, c), 4)),
            pl.BlockSpec((ML_QK, chunk), lambda b, c: (0, blk(b, c))),
            pl.BlockSpec((ML_GATE_ROWS, chunk), lambda b, c: (0, blk(b, c))),
            pl.BlockSpec((ML_GATE_ROWS, chunk), lambda b, c: (0, blk(b, jnp.minimum(c + 1, nc - 1)))),
            pl.BlockSpec((ML_GATE_ROWS, 1), lambda b, c: (0, 0)),
            pl.BlockSpec((1, ML_V), lambda b, c: (0, 0)),
        ],
        out_specs=pl.BlockSpec((chunk, ML_V), lambda b, c: (blk(b, c), 0)),
        out_shape=jax.ShapeDtypeStruct((t, ML_V), jnp.bfloat16),
        scratch_shapes=[
            pltpu.VMEM((ML_HEADS, ML_DK, ML_DV + LANES), jnp.float32),
            pltpu.VMEM((ML_GATE_ROWS, LANES), jnp.float32),
            pltpu.VMEM((ML_GATE_ROWS, chunk), jnp.float32),
            pltpu.VMEM((ML_GATE_ROWS, chunk), jnp.float32),
            pltpu.VMEM((ML_GATE_ROWS, chunk), jnp.float32),
        ],
        compiler_params=_params(("arbitrary", "arbitrary")),
        name="mlstm_core",
    )(main, main, main, k_t, g_t, g_t, bg, g_out.reshape(1, ML_V))


def _swa_kernel(sinks_ref, q_ref, kc_ref, kp_ref, vc_ref, vp_ref, gq_ref, gk_ref, bias_ref, o_ref):
    W = SW_WINDOW
    lane = lax.broadcasted_iota(jnp.int32, (W, LANES), 1)
    low = lane < SW_DH
    row_col = lax.broadcasted_iota(jnp.int32, (SW_GROUP * W, 1), 0)
    gq = gq_ref[...]
    gk = gk_ref[...]
    for g in range(SW_KV_HEADS):
        sl = slice(g * LANES, (g + 1) * LANES)
        k2 = jnp.concatenate([kp_ref[:, sl], kc_ref[:, sl]], axis=0).astype(jnp.float32)
        kn = k2 * lax.rsqrt(jnp.mean(k2 * k2, axis=-1, keepdims=True) + EPS)
        kn = (kn * gk).astype(jnp.bfloat16)
        v2 = jnp.concatenate([vp_ref[:, sl], vc_ref[:, sl]], axis=0)
        parts = []
        for p in range(2):
            c0 = g * SW_GROUP * SW_DH + p * LANES
            qp = q_ref[:, c0:c0 + LANES].astype(jnp.float32)
            sq = qp * qp
            ss_lo = jnp.sum(jnp.where(low, sq, 0.0), axis=-1, keepdims=True)
            ss_hi = jnp.sum(jnp.where(low, 0.0, sq), axis=-1, keepdims=True)
            rs = jnp.where(low, lax.rsqrt(ss_lo / SW_DH + EPS), lax.rsqrt(ss_hi / SW_DH + EPS))
            qn = qp * rs * gq
            parts.append(jnp.where(low, qn, 0.0).astype(jnp.bfloat16))
            parts.append(jnp.where(low, 0.0, qn).astype(jnp.bfloat16))
        q4 = jnp.concatenate(parts, axis=0)
        scores = _bdot_nt(q4, kn) + bias_ref[...]
        sink = jnp.full((SW_GROUP * W, 1), sinks_ref[g * SW_GROUP + SW_GROUP - 1], jnp.float32)
        for j in range(SW_GROUP - 2, -1, -1):
            sink = jnp.where(row_col < (j + 1) * W, sinks_ref[g * SW_GROUP + j], sink)
        m = jnp.maximum(jnp.max(scores, axis=-1, keepdims=True), sink)
        pexp = jnp.exp2(scores - m)
        denom = jnp.sum(pexp, axis=-1, keepdims=True) + jnp.exp2(sink - m)
        o4 = _bdot(pexp.astype(jnp.bfloat16), v2) * (1.0 / denom)
        for p in range(2):
            oa = o4[(2 * p) * W:(2 * p + 1) * W]
            ob = o4[(2 * p + 1) * W:(2 * p + 2) * W]
            c0 = g * SW_GROUP * SW_DH + p * LANES
            o_ref[:, c0:c0 + LANES] = jnp.where(low, oa, ob).astype(o_ref.dtype)


def _swa_core(proj, sinks, g_q, g_k, bsz, seq):
    t = proj.shape[0]
    W = SW_WINDOW
    nb = seq // W
    dq = SW_Q_HEADS * SW_DH
    kv_w = SW_KV_HEADS * LANES
    k_blk = dq // kv_w
    v_blk = k_blk + 1
    cur = lambda b, n, s: b * nb + n
    prev = lambda b, n, s: b * nb + jnp.maximum(n - 1, 0)
    gq2 = jnp.concatenate([g_q, g_q]).reshape(1, LANES) * (SW_DH ** -0.5 * LOG2E)
    gk2 = jnp.concatenate([g_k, g_k]).reshape(1, LANES)
    sinks = sinks.astype(jnp.float32) * LOG2E
    qi = (jnp.arange(SW_GROUP * W) % W)[:, None]
    ki = jnp.arange(2 * W)[None, :]
    rel = qi + W - ki
    in_win = (rel >= 0) & (rel < W)
    bias = jnp.stack([jnp.where(in_win & (ki >= W), 0.0, -jnp.inf),
                      jnp.where(in_win, 0.0, -jnp.inf)]).astype(jnp.float32)
    grid_spec = pltpu.PrefetchScalarGridSpec(
        num_scalar_prefetch=1,
        grid=(bsz, nb),
        in_specs=[
            pl.BlockSpec((W, dq), lambda b, n, s: (cur(b, n, s), 0)),
            pl.BlockSpec((W, kv_w), lambda b, n, s: (cur(b, n, s), k_blk)),
            pl.BlockSpec((W, kv_w), lambda b, n, s: (prev(b, n, s), k_blk)),
            pl.BlockSpec((W, kv_w), lambda b, n, s: (cur(b, n, s), v_blk)),
            pl.BlockSpec((W, kv_w), lambda b, n, s: (prev(b, n, s), v_blk)),
            pl.BlockSpec((1, LANES), lambda b, n, s: (0, 0)),
            pl.BlockSpec((1, LANES), lambda b, n, s: (0, 0)),
            pl.BlockSpec((None, SW_GROUP * W, 2 * W), lambda b, n, s: (jnp.minimum(n, 1), 0, 0)),
        ],
        out_specs=pl.BlockSpec((W, dq), lambda b, n, s: (cur(b, n, s), 0)),
    )
    return pl.pallas_call(
        _swa_kernel,
        grid_spec=grid_spec,
        out_shape=jax.ShapeDtypeStruct((t, dq), jnp.bfloat16),
        compiler_params=_params(("arbitrary", "arbitrary")),
        name="swa_core",
    )(sinks.astype(jnp.float32), proj, proj, proj, proj, proj, gq2, gk2, bias)


def _pack_rounded_pairs(r):
    k = r.shape[1] // 2
    hi = lax.bitcast_convert_type(r[:, :k], jnp.uint32)
    lo = lax.bitcast_convert_type(r[:, k:], jnp.uint32)
    return hi | (lo >> 16)


def _pack_bf16_pairs(a):
    return _pack_rounded_pairs(a.astype(jnp.bfloat16).astype(jnp.float32))


def _unpack_bf16_pairs(u):
    hi = lax.bitcast_convert_type(u & jnp.uint32(0xFFFF0000), jnp.float32)
    lo = lax.bitcast_convert_type(u << 16, jnp.float32)
    return hi, lo


def _router_kernel(a_ref, wo_ref, x_ref, gt_ref, g_ref, sc_ref, sh_ref, wh_ref, wl_ref, b_ref,
                   xo_ref, hn_ref, info_ref, infot_ref, cnt_ref, carry_ref, earlier_ref, *, tm):
    @pl.when(pl.program_id(0) == 0)
    def _():
        carry_ref[...] = jnp.zeros_like(carry_ref)
        r_idx = lax.broadcasted_iota(jnp.int32, (tm, tm), 0)
        c_idx = lax.broadcasted_iota(jnp.int32, (tm, tm), 1)
        earlier_ref[...] = jnp.where(r_idx < c_idx, 1.0, 0.0).astype(jnp.bfloat16)

    x_new = x_ref[...] + gt_ref[...] * _bdot(a_ref[...], wo_ref[...])
    xo_ref[...] = x_new
    hn = _modulated_norm(x_new, g_ref[...], sc_ref[...], sh_ref[...])
    h_hi = hn.astype(jnp.bfloat16)
    hi_f32 = h_hi.astype(jnp.float32)
    h_lo = (hn - hi_f32).astype(jnp.bfloat16)
    hn_ref[...] = _pack_rounded_pairs(hi_f32)
    wide = _bdot_nt(wl_ref[...], h_hi)
    logits = (wide[:ROUTER_LANES] + (_bdot_nt(wh_ref[...], h_lo) + wide[ROUTER_LANES:])
              + b_ref[...])
    E8 = EXPERTS_PER_GROUP
    sub = lax.broadcasted_iota(jnp.int32, (E8, tm), 0).astype(jnp.float32)
    big = float(ROUTER_LANES)
    neg = -jnp.inf

    gl = logits[N_EXPERTS:N_EXPERTS + N_GROUPS]
    gmax = jnp.max(gl, axis=0, keepdims=True)
    gsel = jnp.min(jnp.where(gl == gmax, sub, big), axis=0, keepdims=True)
    p_grp = 1.0 / jnp.sum(jnp.exp(gl - gmax), axis=0, keepdims=True)

    el = logits[0:E8]
    for grp in range(1, N_GROUPS):
        el = jnp.where(gsel == grp, logits[grp * E8:(grp + 1) * E8], el)
    v1 = jnp.max(el, axis=0, keepdims=True)
    j1 = jnp.min(jnp.where(el == v1, sub, big), axis=0, keepdims=True)
    el2 = jnp.where(sub == j1, neg, el)
    v2 = jnp.max(el2, axis=0, keepdims=True)
    j2 = jnp.min(jnp.where(el2 == v2, sub, big), axis=0, keepdims=True)
    i1 = gsel * E8 + j1
    i2 = gsel * E8 + j2
    e21 = jnp.exp(v2 - v1)
    gate1 = p_grp / (1.0 + e21)
    gate2 = p_grp * e21 / (1.0 + e21)

    erow = lax.broadcasted_iota(jnp.int32, (N_EXPERTS, tm), 0).astype(jnp.float32)
    hit1 = erow == i1
    hit2 = erow == i2
    onehot = jnp.where(hit1 | hit2, 1.0, 0.0)
    carry = carry_ref[:, 0:1]
    before = _bdot(onehot.astype(jnp.bfloat16), earlier_ref[...]) + carry
    rank1 = jnp.sum(jnp.where(hit1, before, 0.0), axis=0, keepdims=True)
    rank2 = jnp.sum(jnp.where(hit2, before, 0.0), axis=0, keepdims=True)
    total = carry + jnp.sum(onehot, axis=1, keepdims=True)
    carry_ref[...] = jnp.broadcast_to(total, carry_ref.shape)
    cnt_ref[...] = jnp.broadcast_to(total, cnt_ref.shape)

    info_t = jnp.where(sub == 0, i1, 0.0)
    info_t = jnp.where(sub == 1, i2, info_t)
    info_t = jnp.where(sub == 2, rank1, info_t)
    info_t = jnp.where(sub == 3, rank2, info_t)
    info_t = jnp.where(sub == 4, gate1, info_t)
    info_t = jnp.where(sub == 5, gate2, info_t)
    infot_ref[...] = info_t
    info_ref[...] = jnp.concatenate(
        [info_t, jnp.zeros((ROUTER_LANES - E8, tm), jnp.float32)], axis=0).T


def _outproj_router(a, w_out, x2d, gt, g, sc, sh, w_hi, w_lo, bias, seq):
    t, d = x2d.shape
    tm = 512
    tiles_per_batch = seq // tm
    per_batch = lambda i: (i // tiles_per_batch, 0, 0)
    const = lambda i: (0, 0)
    return pl.pallas_call(
        functools.partial(_router_kernel, tm=tm),
        grid=(t // tm,),
        in_specs=[
            pl.BlockSpec((tm, a.shape[1]), lambda i: (i, 0)),
            pl.BlockSpec(w_out.shape, const),
            pl.BlockSpec((tm, d), lambda i: (i, 0)),
            pl.BlockSpec((None, 1, d), per_batch),
            pl.BlockSpec((1, d), const),
            pl.BlockSpec((None, 1, d), per_batch),
            pl.BlockSpec((None, 1, d), per_batch),
            pl.BlockSpec((ROUTER_LANES, d), const),
            pl.BlockSpec((2 * ROUTER_LANES, d), const),
            pl.BlockSpec((ROUTER_LANES, 1), const),
        ],
        out_specs=[
            pl.BlockSpec((tm, d), lambda i: (i, 0)),
            pl.BlockSpec((tm, d // 2), lambda i: (i, 0)),
            pl.BlockSpec((tm, ROUTER_LANES), lambda i: (i, 0)),
            pl.BlockSpec((EXPERTS_PER_GROUP, tm), lambda i: (0, i)),
            pl.BlockSpec((N_EXPERTS, LANES), const),
        ],
        out_shape=[
            jax.ShapeDtypeStruct((t, d), jnp.float32),
            jax.ShapeDtypeStruct((t, d // 2), jnp.uint32),
            jax.ShapeDtypeStruct((t, ROUTER_LANES), jnp.float32),
            jax.ShapeDtypeStruct((EXPERTS_PER_GROUP, t), jnp.float32),
            jax.ShapeDtypeStruct((N_EXPERTS, LANES), jnp.float32),
        ],
        scratch_shapes=[pltpu.VMEM((N_EXPERTS, LANES), jnp.float32),
                        pltpu.VMEM((tm, tm), jnp.bfloat16)],
        compiler_params=_params(("arbitrary",)),
        name="outproj_router",
    )(a, w_out, x2d, gt, g.reshape(1, d), sc, sh, w_hi, jnp.concatenate([w_hi, w_lo], axis=0), bias)


def _sc_mesh():
    return plsc.VectorSubcoreMesh(core_axis_name="c", subcore_axis_name="s")


def _sc_dispatch(rows, d0, d1, pad_idx, n_slots):
    t, w = rows.shape
    n_pad = pad_idx.shape[1]
    zeros = jnp.zeros((SC_WINDOW, w), rows.dtype)
    sem = (pltpu.PARALLEL, pltpu.ARBITRARY)

    @pl.kernel(out_type=jax.ShapeDtypeStruct((n_slots + SC_WINDOW, w), rows.dtype), mesh=_sc_mesh())
    def dispatch(x_hbm, d0_hbm, d1_hbm, z_hbm, p_hbm, o_hbm):
        def scatter_rows(x_vmem, i0_vmem, i1_vmem):
            cols = pl.ds(pl.program_id(1) * SC_COLS, SC_COLS)
            pltpu.sync_copy(x_vmem, o_hbm.at[i0_vmem.at[0], cols])
            pltpu.sync_copy(x_vmem, o_hbm.at[i1_vmem.at[0], cols])

        pltpu.emit_pipeline(
            scatter_rows,
            grid=(t // SC_WINDOW, w // SC_COLS),
            in_specs=[pl.BlockSpec((SC_WINDOW, SC_COLS), lambda i, j: (i, j)),
                      pl.BlockSpec((1, SC_WINDOW), lambda i, j: (0, i)),
                      pl.BlockSpec((1, SC_WINDOW), lambda i, j: (0, i))],
            out_specs=[],
            core_axis_name=("c", "s"),
            dimension_semantics=sem,
        )(x_hbm, d0_hbm, d1_hbm)

        def scatter_zeros(z_vmem, p_vmem):
            cols = pl.ds(pl.program_id(1) * SC_COLS, SC_COLS)
            pltpu.sync_copy(z_vmem, o_hbm.at[p_vmem.at[0], cols])

        pltpu.emit_pipeline(
            scatter_zeros,
            grid=(n_pad // SC_WINDOW, w // SC_COLS),
            in_specs=[pl.BlockSpec((SC_WINDOW, SC_COLS), lambda i, j: (0, j)),
                      pl.BlockSpec((1, SC_WINDOW), lambda i, j: (0, i))],
            out_specs=[],
            core_axis_name=("c", "s"),
            dimension_semantics=sem,
        )(z_hbm, p_hbm)

    return dispatch(rows, d0, d1, zeros, pad_idx)


def _sc_gather(src, idx):
    n_out = idx.shape[1]
    w = src.shape[1]

    @pl.kernel(out_type=jax.ShapeDtypeStruct((n_out, w), src.dtype), mesh=_sc_mesh())
    def gather(x_hbm, i_hbm, o_hbm):
        def gather_rows(i_vmem, o_vmem):
            cols = pl.ds(pl.program_id(1) * SC_COLS, SC_COLS)
            pltpu.sync_copy(x_hbm.at[i_vmem.at[0], cols], o_vmem)

        pltpu.emit_pipeline(
            gather_rows,
            grid=(n_out // SC_WINDOW, w // SC_COLS),
            in_specs=[pl.BlockSpec((1, SC_WINDOW), lambda i, j: (0, i))],
            out_specs=[pl.BlockSpec((SC_WINDOW, SC_COLS), lambda i, j: (i, j))],
            core_axis_name=("c", "s"),
            dimension_semantics=(pltpu.PARALLEL, pltpu.ARBITRARY),
        )(i_hbm, o_hbm)

    return gather(src, idx)


def _expert_kernel(meta_ref, x_ref, w1_hbm, w3_hbm, w2_hbm, y_ref,
                   w1_buf, w3_buf, w2_buf, w1_c, w3_c, w2_c, sems, *, layer, nb):
    def weight_copies(expert, s):
        return (pltpu.make_async_copy(w1_hbm.at[layer, expert], w1_buf.at[s], sems.at[s, 0]),
                pltpu.make_async_copy(w3_hbm.at[layer, expert], w3_buf.at[s], sems.at[s, 1]),
                pltpu.make_async_copy(w2_hbm.at[layer, expert], w2_buf.at[s], sems.at[s, 2]))

    @pl.when(pl.program_id(0) == 0)
    def _():
        for cp in weight_copies(meta_ref[0], meta_ref[nb]):
            cp.start()

    for j in range(EXPERT_BLOCKS_PER_STEP):
        i = pl.program_id(0) * EXPERT_BLOCKS_PER_STEP + j
        rows = slice(j * MOE_BLOCK, (j + 1) * MOE_BLOCK)
        e = meta_ref[i]
        slot = meta_ref[nb + i]
        nxt = meta_ref[2 * nb + i]
        used = i < meta_ref[3 * nb]
        first = used & ((i == 0) | (e != meta_ref[jnp.maximum(i - 1, 0)]))

        @pl.when(first)
        def _():
            for cp in weight_copies(e, slot):
                cp.wait()

            @pl.when(nxt >= 0)
            def _():
                for cp in weight_copies(nxt, 1 - slot):
                    cp.start()

            w1_c[...] = w1_buf[slot].astype(jnp.bfloat16)
            w3_c[...] = w3_buf[slot].astype(jnp.bfloat16)
            w2_c[...] = w2_buf[slot].astype(jnp.bfloat16)

        @pl.when(used)
        def _():
            x_hi, x_lo = _unpack_bf16_pairs(x_ref[rows, :])
            xb = jnp.concatenate([x_hi, x_lo], axis=-1).astype(jnp.bfloat16)
            h1 = _bdot(xb, w1_c[...])
            h3 = _bdot(xb, w3_c[...])
            act = (h1 * jax.nn.sigmoid(h1) * h3).astype(jnp.bfloat16)
            y_ref[rows, :] = _pack_bf16_pairs(_bdot(act, w2_c[...]))

        @pl.when(jnp.logical_not(used))
        def _():
            y_ref[rows, :] = jnp.zeros((MOE_BLOCK, y_ref.shape[1]), y_ref.dtype)


def _experts(x_slots, blk_meta, w1, w3, w2, layer, nb):
    dp = x_slots.shape[1]
    d, de = w1.shape[-2:]
    step_rows = EXPERT_BLOCKS_PER_STEP * MOE_BLOCK
    assert nb % EXPERT_BLOCKS_PER_STEP == 0
    last_used_step = lambda s: (s[3 * nb] - 1) // EXPERT_BLOCKS_PER_STEP
    grid_spec = pltpu.PrefetchScalarGridSpec(
        num_scalar_prefetch=1,
        grid=(nb // EXPERT_BLOCKS_PER_STEP,),
        in_specs=[
            pl.BlockSpec((step_rows, dp), lambda i, s: (jnp.minimum(i, last_used_step(s)), 0)),
            pl.BlockSpec(memory_space=pl.ANY),
            pl.BlockSpec(memory_space=pl.ANY),
            pl.BlockSpec(memory_space=pl.ANY),
        ],
        out_specs=pl.BlockSpec((step_rows, dp), lambda i, s: (i, 0)),
        scratch_shapes=[
            pltpu.VMEM((2, d, de), jnp.float32),
            pltpu.VMEM((2, d, de), jnp.float32),
            pltpu.VMEM((2, de, d), jnp.float32),
            pltpu.VMEM((d, de), jnp.bfloat16),
            pltpu.VMEM((d, de), jnp.bfloat16),
            pltpu.VMEM((de, d), jnp.bfloat16),
            pltpu.SemaphoreType.DMA((2, 3)),
        ],
    )
    return pl.pallas_call(
        functools.partial(_expert_kernel, layer=layer, nb=nb),
        grid_spec=grid_spec,
        out_shape=jax.ShapeDtypeStruct((nb * MOE_BLOCK, dp), jnp.uint32),
        compiler_params=_params(("arbitrary",)),
        name="moe_experts",
    )(blk_meta, x_slots, w1, w3, w2)


def _combine_kernel(x_ref, y1_ref, y2_ref, info_ref, gt_ref, o_ref):
    o_ref[...] = _moe_combined(x_ref[...], y1_ref, y2_ref, info_ref, gt_ref)


def _combine(x2d, y_pairs, info, gt, seq):
    t, d = x2d.shape
    tm = min(1024, seq)
    tiles_per_batch = seq // tm
    second = t // tm
    return pl.pallas_call(
        _combine_kernel,
        grid=(t // tm,),
        in_specs=[
            pl.BlockSpec((tm, d), lambda i: (i, 0)),
            pl.BlockSpec((tm, d // 2), lambda i: (i, 0)),
            pl.BlockSpec((tm, d // 2), lambda i: (i + second, 0)),
            pl.BlockSpec((tm, ROUTER_LANES), lambda i: (i, 0)),
            pl.BlockSpec((None, 1, d), lambda i: (i // tiles_per_batch, 0, 0)),
        ],
        out_specs=pl.BlockSpec((tm, d), lambda i: (i, 0)),
        out_shape=jax.ShapeDtypeStruct((t, d), jnp.float32),
        compiler_params=_params(("arbitrary",)),
        name="moe_combine",
    )(x2d, y_pairs, y_pairs, info, gt)


def _slot_plan(info_t, cnt, t):
    counts = cnt[:, 0].astype(jnp.int32)
    padded = (counts + MOE_BLOCK - 1) // MOE_BLOCK * MOE_BLOCK
    pad_ends = jnp.cumsum(padded)
    pad_starts = pad_ends - padded
    nb = -(-(2 * t) // MOE_BLOCK) + N_EXPERTS
    n_slots = nb * MOE_BLOCK
    it = info_t.astype(jnp.int32)
    onehot_start = lambda e: jnp.sum(
        jnp.where(e[None, :] == jnp.arange(N_EXPERTS, dtype=jnp.int32)[:, None],
                  pad_starts[:, None], 0), axis=0)
    dest1 = (onehot_start(it[0]) + it[2]).reshape(1, t)
    dest2 = (onehot_start(it[1]) + it[3]).reshape(1, t)
    lane = jnp.arange(MOE_BLOCK, dtype=jnp.int32)[None, :]
    n_padding = (padded - counts)[:, None]
    wrapped = (pad_starts + counts)[:, None] + lane % jnp.maximum(n_padding, 1)
    pad_idx = jnp.where(n_padding > 0, wrapped, n_slots + lane % SC_WINDOW).reshape(-1)
    n_real = pad_ends[-1] // MOE_BLOCK
    n_used = -(-n_real // EXPERT_BLOCKS_PER_STEP) * EXPERT_BLOCKS_PER_STEP
    tail = jnp.arange((EXPERT_BLOCKS_PER_STEP - 1) * MOE_BLOCK, dtype=jnp.int32)
    tail_idx = jnp.where(tail < (n_used - n_real) * MOE_BLOCK, pad_ends[-1] + tail,
                         n_slots + tail % SC_WINDOW)
    pad_idx = jnp.concatenate([pad_idx, tail_idx]).reshape(1, -1)
    experts = jnp.arange(N_EXPERTS, dtype=jnp.int32)
    blk = jnp.arange(nb, dtype=jnp.int32)
    blk_exp = jnp.minimum(
        jnp.sum((pad_ends[None, :] <= (blk * MOE_BLOCK)[:, None]).astype(jnp.int32), axis=1),
        N_EXPERTS - 1)
    blk_exp = jnp.where(blk >= n_real, jnp.max(jnp.where(padded > 0, experts, 0)), blk_exp)
    prev_exp = jnp.concatenate([jnp.full((1,), -1, jnp.int32), blk_exp[:-1]])
    is_first = (blk < n_used) & (blk_exp != prev_exp)
    blk_slot = (jnp.cumsum(is_first.astype(jnp.int32)) + 1) % 2
    later = (experts[None, :] > experts[:, None]) & (padded[None, :] > 0)
    nxt_of = jnp.min(jnp.where(later, experts[None, :], N_EXPERTS), axis=1)
    nxt_of = jnp.where(nxt_of == N_EXPERTS, -1, nxt_of)
    blk_nxt = jnp.sum(jnp.where(blk_exp[:, None] == experts[None, :], nxt_of[None, :], 0), axis=1)
    blk_meta = jnp.concatenate([blk_exp, blk_slot, blk_nxt, n_used[None]]).astype(jnp.int32)
    return dest1, dest2, pad_idx, blk_meta, nb, n_slots


def _mixer_out_and_moe(a, w_out, x2d, gt1, g, sc, sh, w_group, b_group, w_router, b_router,
                       w1, w3, w2, layer, seq):
    t, d = x2d.shape
    w_cat = jnp.zeros((ROUTER_LANES, d), jnp.float32)
    w_cat = w_cat.at[:N_EXPERTS].set(w_router.T).at[N_EXPERTS:N_EXPERTS + N_GROUPS].set(w_group.T)
    b_cat = jnp.zeros((ROUTER_LANES, 1), jnp.float32)
    b_cat = b_cat.at[:N_EXPERTS, 0].set(b_router).at[N_EXPERTS:N_EXPERTS + N_GROUPS, 0].set(b_group)
    w_hi, w_lo = _split_hi_lo(w_cat)
    x_new, hn, info, info_t, cnt = _outproj_router(a, w_out, x2d, gt1, g, sc, sh, w_hi, w_lo,
                                                   b_cat, seq)
    dest1, dest2, pad_idx, blk_meta, nb, n_slots = _slot_plan(info_t, cnt, t)
    x_slots = _sc_dispatch(hn, dest1, dest2, pad_idx, n_slots)
    y_slots = _experts(x_slots, blk_meta, w1, w3, w2, layer, nb)
    y_pairs = _sc_gather(y_slots, jnp.concatenate([dest1, dest2], axis=1))
    return x_new, y_pairs, info


def kernel(x, c, w_ada, b_ada, norm1_g, norm2_g, ml_w_in, ml_b_gate, ml_g_out, ml_w_out,
           sw_w_in, sw_g_q, sw_g_k, sw_sinks, sw_w_out, moe_w_group, moe_b_group,
           moe_w_router, moe_b_router, moe_w1, moe_w3, moe_w2):
    bsz, seq, d = x.shape
    depth = w_ada.shape[0]
    bf = jnp.bfloat16
    mod = _ada_mod(c, w_ada, b_ada)
    x2d = x.reshape(bsz * seq, d)
    pending = None
    for layer in range(depth):
        sh1, sc1, gt1, sh2, sc2, gt2 = [
            mod[layer, :, i * d:(i + 1) * d].reshape(bsz, 1, d) for i in range(6)]
        j = layer // 2
        if layer % 2 == 0:
            w = ml_w_in[j]
            q_w, k_w = w[:, :ML_QK], w[:, ML_QK:2 * ML_QK]
            v_w = w[:, 2 * ML_QK:2 * ML_QK + ML_V]
            o_w = w[:, 2 * ML_QK + ML_V:2 * ML_QK + 2 * ML_V]
            g_w = w[:, 2 * ML_QK + 2 * ML_V:]
            w_main = jnp.concatenate([v_w, o_w, q_w], axis=1).astype(bf)
            wg_t = jnp.zeros((ML_GATE_ROWS, d), jnp.float32).at[:2 * ML_HEADS].set(g_w.T)
            wg_hi, wg_lo = _split_hi_lo(wg_t)
            wk_t = k_w.T.astype(bf)
            w_out = ml_w_out[j].astype(bf)
            outs = _inproj(x2d, norm1_g[layer], sc1, sh1, w_main, seq,
                           ml_extra=(wk_t, wg_hi, wg_lo),
                           q_cols=(2 * ML_V, 2 * ML_V + ML_QK), q_scale=ML_DK ** -0.5,
                           gate_cols=(ML_V, 2 * ML_V), pending=pending)
            if pending is not None:
                x2d, outs = outs[0], outs[1:]
            main, k_t, g_t = outs
            a = _mlstm_core(main, k_t, g_t, ml_b_gate[j], ml_g_out[j], bsz, seq)
        else:
            w = sw_w_in[j]
            dq = SW_Q_HEADS * SW_DH
            dkv = SW_KV_HEADS * SW_DH
            dup = lambda m: jnp.concatenate(
                [m.reshape(d, SW_KV_HEADS, 1, SW_DH)] * 2, axis=2).reshape(d, 2 * dkv)
            w_main = jnp.concatenate(
                [w[:, :dq], dup(w[:, dq:dq + dkv]), dup(w[:, dq + dkv:])], axis=1).astype(bf)
            w_out = sw_w_out[j].astype(bf)
            outs = _inproj(x2d, norm1_g[layer], sc1, sh1, w_main, seq, pending=pending)
            if pending is not None:
                x2d, outs = outs[0], outs[1:]
            a = _swa_core(outs[0], sw_sinks[j], sw_g_q[j], sw_g_k[j], bsz, seq)
        x2d, y_pairs, info = _mixer_out_and_moe(
            a, w_out, x2d, gt1, norm2_g[layer], sc2, sh2, moe_w_group[layer], moe_b_group[layer],
            moe_w_router[layer], moe_b_router[layer], moe_w1, moe_w3, moe_w2, layer, seq)
        pending = (y_pairs, info, gt2)
    y_pairs, info, gt2 = pending
    return _combine(x2d, y_pairs, info, gt2, seq).reshape(bsz, seq, d)
```

```python
import functools

import jax
import jax.numpy as jnp
from jax import lax
from jax.experimental import pallas as pl
from jax.experimental.pallas import tpu as pltpu
from jax.experimental.pallas import tpu_sc as plsc

EPS = 1e-6
GATE_CAP = 15.0
LOG2E = 1.4426950408889634

ML_HEADS = 4
ML_DK = 128
ML_DV = 256
ML_QK = ML_HEADS * ML_DK
ML_V = ML_HEADS * ML_DV
ML_GATE_ROWS = 16

SW_Q_HEADS = 16
SW_KV_HEADS = 4
SW_GROUP = SW_Q_HEADS // SW_KV_HEADS
SW_DH = 64
SW_WINDOW = 128
SW_Q_BLOCKS = 2
LANES = 128

N_GROUPS = 8
EXPERTS_PER_GROUP = 8
N_EXPERTS = N_GROUPS * EXPERTS_PER_GROUP
MOE_BLOCK = 256
ROUTER_LANES = 128
SC_WINDOW = 128
SC_COLS = 256
EXPERT_BLOCKS_PER_STEP = 4

VMEM_LIMIT = 56 * 1024 * 1024

_NT = (((1,), (1,)), ((), ()))


def _bdot(a, b):
    return jnp.dot(a, b, preferred_element_type=jnp.float32)


def _bdot_nt(a, b):
    return lax.dot_general(a, b, _NT, preferred_element_type=jnp.float32)


def _split_hi_lo(a):
    hi = a.astype(jnp.bfloat16)
    lo = (a - hi.astype(jnp.float32)).astype(jnp.bfloat16)
    return hi, lo


def _params(sem):
    return pltpu.CompilerParams(dimension_semantics=sem, vmem_limit_bytes=VMEM_LIMIT)


def _ada_kernel(c_ref, w_ref, b_ref, o_ref):
    c = c_ref[...]
    cond = c * jax.nn.sigmoid(c)
    c_hi, c_lo = _split_hi_lo(cond)
    w_hi, w_lo = _split_hi_lo(w_ref[...])
    acc = _bdot(c_hi, w_hi) + (_bdot(c_lo, w_hi) + _bdot(c_hi, w_lo))
    o_ref[...] = acc + b_ref[...]


def _ada_mod(c, w_ada, b_ada):
    depth, d, n = w_ada.shape
    bsz = c.shape[0]
    rows = 8
    tn = 768
    c_pad = jnp.zeros((rows, d), jnp.float32).at[:bsz].set(c)
    out = pl.pallas_call(
        _ada_kernel,
        grid=(depth, n // tn),
        in_specs=[
            pl.BlockSpec((rows, d), lambda l, j: (0, 0)),
            pl.BlockSpec((None, d, tn), lambda l, j: (l, 0, j)),
            pl.BlockSpec((None, 1, tn), lambda l, j: (l, 0, j)),
        ],
        out_specs=pl.BlockSpec((None, rows, tn), lambda l, j: (l, 0, j)),
        out_shape=jax.ShapeDtypeStruct((depth, rows, n), jnp.float32),
        compiler_params=_params(("arbitrary", "arbitrary")),
        name="ada_mod",
    )(c_pad, w_ada, b_ada.reshape(depth, 1, n))
    return out[:, :bsz]


def _modulated_norm(x, g, sc, sh):
    y = x * lax.rsqrt(jnp.mean(x * x, axis=-1, keepdims=True) + EPS)
    return y * (g * (1.0 + sc)) + sh


def _moe_combined(x, y1_ref, y2_ref, info_ref, gt_ref):
    info = info_ref[...]
    g1 = info[:, 4:5]
    g2 = info[:, 5:6]
    y1_hi, y1_lo = _unpack_bf16_pairs(y1_ref[...])
    y2_hi, y2_lo = _unpack_bf16_pairs(y2_ref[...])
    y = jnp.concatenate([g1 * y1_hi + g2 * y2_hi, g1 * y1_lo + g2 * y2_lo], axis=-1)
    return x + gt_ref[...] * y


def _inproj_kernel(*refs, n_main, chunk, q_cols, q_scale, gate_cols, with_ml, with_combine):
    refs = list(refs)
    n_in = 5 + (4 if with_combine else 0) + (3 if with_ml else 0)
    ins, outs = refs[:n_in], refs[n_in:]
    x_ref = ins.pop(0)
    x = x_ref[...]
    if with_combine:
        y1_ref, y2_ref, info_ref, gtp_ref = ins[:4]
        ins = ins[4:]
        x = _moe_combined(x, y1_ref, y2_ref, info_ref, gtp_ref)
        outs.pop(0)[...] = x
    g_ref, sc_ref, sh_ref, w_ref = ins[:4]
    o_ref = outs[0]
    if with_ml:
        wk_ref, wgh_ref, wgl_ref = ins[4:]
        kt_ref, gt_ref = outs[1:]
    hn = _modulated_norm(x, g_ref[...], sc_ref[...], sh_ref[...])
    hb = hn.astype(jnp.bfloat16)
    for c0 in range(0, n_main, chunk):
        acc = _bdot(hb, w_ref[:, c0:c0 + chunk])
        if q_cols is not None and q_cols[0] <= c0 < q_cols[1]:
            acc = acc * q_scale
        if gate_cols is not None and gate_cols[0] <= c0 < gate_cols[1]:
            acc = jax.nn.sigmoid(acc)
        o_ref[:, c0:c0 + chunk] = acc.astype(o_ref.dtype)
    if with_ml:
        kt_ref[...] = _bdot_nt(wk_ref[...], hb).astype(kt_ref.dtype)
        h_lo = (hn - hb.astype(jnp.float32)).astype(jnp.bfloat16)
        gt_ref[...] = (_bdot_nt(wgh_ref[...], hb)
                       + (_bdot_nt(wgh_ref[...], h_lo) + _bdot_nt(wgl_ref[...], hb)))


def _inproj(x2d, g, sc, sh, w_main, seq, *, ml_extra=None, q_cols=None, q_scale=1.0,
            gate_cols=None, pending=None):
    t, d = x2d.shape
    tm = 512
    n_main = w_main.shape[1]
    tiles_per_batch = seq // tm
    row = lambda i: (i, 0)
    per_batch = lambda i: (i // tiles_per_batch, 0, 0)
    const = lambda i: (0, 0)
    in_specs = [pl.BlockSpec((tm, d), row)]
    args = [x2d]
    out_specs, out_shape = [], []
    if pending is not None:
        y_pairs, info, gt_prev = pending
        second = t // tm
        in_specs += [pl.BlockSpec((tm, d // 2), row),
                     pl.BlockSpec((tm, d // 2), lambda i: (i + second, 0)),
                     pl.BlockSpec((tm, ROUTER_LANES), row),
                     pl.BlockSpec((None, 1, d), per_batch)]
        args += [y_pairs, y_pairs, info, gt_prev]
        out_specs += [pl.BlockSpec((tm, d), row)]
        out_shape += [jax.ShapeDtypeStruct((t, d), jnp.float32)]
    in_specs += [
        pl.BlockSpec((1, d), const),
        pl.BlockSpec((None, 1, d), per_batch),
        pl.BlockSpec((None, 1, d), per_batch),
        pl.BlockSpec((d, n_main), const),
    ]
    args += [g.reshape(1, d), sc, sh, w_main]
    out_specs += [pl.BlockSpec((tm, n_main), row)]
    out_shape += [jax.ShapeDtypeStruct((t, n_main), jnp.bfloat16)]
    if ml_extra is not None:
        wk_t, wg_hi, wg_lo = ml_extra
        in_specs += [pl.BlockSpec(wk_t.shape, const),
                     pl.BlockSpec(wg_hi.shape, const),
                     pl.BlockSpec(wg_lo.shape, const)]
        args += [wk_t, wg_hi, wg_lo]
        out_specs += [pl.BlockSpec((wk_t.shape[0], tm), lambda i: (0, i)),
                      pl.BlockSpec((ML_GATE_ROWS, tm), lambda i: (0, i))]
        out_shape += [jax.ShapeDtypeStruct((wk_t.shape[0], t), jnp.bfloat16),
                      jax.ShapeDtypeStruct((ML_GATE_ROWS, t), jnp.float32)]
    kern = functools.partial(_inproj_kernel, n_main=n_main, chunk=512, q_cols=q_cols,
                             q_scale=q_scale, gate_cols=gate_cols, with_ml=ml_extra is not None,
                             with_combine=pending is not None)
    return pl.pallas_call(
        kern,
        grid=(t // tm,),
        in_specs=in_specs,
        out_specs=out_specs,
        out_shape=out_shape,
        compiler_params=_params(("arbitrary",)),
        name="inproj_ml" if ml_extra is not None else "inproj_sw",
    )(*args)


def _mlstm_gate_terms(graw, bias, upper):
    H = ML_HEADS
    L = graw.shape[1]
    z = graw + bias
    gates = GATE_CAP * jnp.tanh(z / GATE_CAP)
    log_f = jnp.minimum(gates, 0.0) - jnp.log1p(jnp.exp(-jnp.abs(gates)))
    row = lax.broadcasted_iota(jnp.int32, (ML_GATE_ROWS, L), 0)
    lane = lax.broadcasted_iota(jnp.int32, (ML_GATE_ROWS, L), 1)
    is_i = row < H
    slab = jnp.where(is_i, gates, log_f)
    a1 = slab.astype(jnp.bfloat16)
    r1 = slab - a1.astype(jnp.float32)
    a2 = r1.astype(jnp.bfloat16)
    a3 = (r1 - a2.astype(jnp.float32)).astype(jnp.bfloat16)
    cum = _bdot(a1, upper) + (_bdot(a2, upper) + _bdot(a3, upper))
    ib = jnp.where(is_i, gates, cum)
    b = pltpu.roll(ib, ML_GATE_ROWS - H, 0)
    u = ib - b
    cm = u
    shift = 1
    while shift < L:
        cm = jnp.maximum(cm, jnp.where(lane >= shift, pltpu.roll(cm, shift, 1), -jnp.inf))
        shift *= 2
    return b, u, cm


def _mlstm_kernel(v_ref, o_ref, q_ref, kt_ref, gt_ref, gtn_ref, bg_ref, gout_ref, out_ref,
                  c_ref, m_ref, b_ref, u_ref, cm_ref, *, chunk):
    L = chunk
    H, dk, dv = ML_HEADS, ML_DK, ML_DV
    r_idx = lax.broadcasted_iota(jnp.int32, (L, L), 0)
    c_idx = lax.broadcasted_iota(jnp.int32, (L, L), 1)
    upper = jnp.where(r_idx <= c_idx, 1.0, 0.0).astype(jnp.bfloat16)
    causal = r_idx >= c_idx
    row = lax.broadcasted_iota(jnp.int32, (ML_GATE_ROWS, L), 0)
    ones_col = jnp.where(lax.broadcasted_iota(jnp.int32, (L, LANES), 1) == 0, 1.0, 0.0
                         ).astype(jnp.bfloat16)

    @pl.when(pl.program_id(1) == 0)
    def _():
        c_ref[...] = jnp.zeros_like(c_ref)
        m_ref[...] = jnp.zeros_like(m_ref)
        b0, u0, cm0 = _mlstm_gate_terms(gt_ref[...], bg_ref[...], upper)
        b_ref[...] = b0
        u_ref[...] = u0
        cm_ref[...] = cm0

    b16 = b_ref[...]
    u16 = u_ref[...]
    cm16 = cm_ref[...]
    b_n, u_n, cm_n = _mlstm_gate_terms(gtn_ref[...], bg_ref[...], upper)
    b_ref[...] = b_n
    u_ref[...] = u_n
    cm_ref[...] = cm_n

    m_prev = m_ref[:, 0:1]
    z16 = jnp.maximum(m_prev, cm16)
    w_inter16 = jnp.exp(m_prev - z16)
    e_negm16 = jnp.exp(-(b16 + z16))
    z_last = z16[:, L - 1:L]
    w_state16 = jnp.exp(u16 - z_last)
    decay16 = jnp.exp(m_prev - z_last)
    m_ref[...] = jnp.broadcast_to(b16[:, L - 1:L] + z_last, m_ref.shape)
    stacked = jnp.where(row < H, z16,
                        jnp.where(row < 2 * H, pltpu.roll(w_inter16, H, 0),
                                  pltpu.roll(e_negm16, 2 * H, 0)))
    cols = jnp.concatenate(
        [stacked, jnp.zeros((LANES - ML_GATE_ROWS, L), jnp.float32)], axis=0).T

    for h in range(H):
        u_r = u16[h:h + 1, :]
        z_c = cols[:, h:h + 1]
        w_inter = cols[:, H + h:H + h + 1]
        e_negm = cols[:, 2 * H + h:2 * H + h + 1]
        c_ext = c_ref[h]
        q = q_ref[:, h * dk:(h + 1) * dk]
        kt = kt_ref[h * dk:(h + 1) * dk, :]
        v_ext = jnp.concatenate([v_ref[:, h * dv:(h + 1) * dv], ones_col], axis=-1)

        w_intra = jnp.exp(jnp.where(causal, u_r - z_c, -jnp.inf))
        s = (_bdot(q, kt) * w_intra).astype(jnp.bfloat16)
        nd = w_inter * _bdot(q, c_ext.astype(jnp.bfloat16)) + _bdot(s, v_ext)
        den = nd[:, dv:dv + 1]
        hb = nd[:, :dv] * (1.0 / jnp.maximum(jnp.abs(den), e_negm))

        kw = (kt.astype(jnp.float32) * w_state16[h:h + 1, :]).astype(jnp.bfloat16)
        c_ref[h] = decay16[h:h + 1, :] * c_ext + _bdot(kw, v_ext)

        y = hb * lax.rsqrt(jnp.mean(hb * hb, axis=-1, keepdims=True) + EPS)
        y = y * gout_ref[:, h * dv:(h + 1) * dv]
        og = o_ref[:, h * dv:(h + 1) * dv].astype(jnp.float32)
        out_ref[:, h * dv:(h + 1) * dv] = (y * og).astype(out_ref.dtype)


def _mlstm_core(main, k_t, g_t, b_gate, g_out, bsz, seq, chunk=256):
    t = main.shape[0]
    nc = seq // chunk
    blk = lambda b, c: b * nc + c
    bg = jnp.zeros((ML_GATE_ROWS, 1), jnp.float32).at[:2 * ML_HEADS, 0].set(b_gate)
    return pl.pallas_call(
        functools.partial(_mlstm_kernel, chunk=chunk),
        grid=(bsz, nc),
        in_specs=[
            pl.BlockSpec((chunk, ML_V), lambda b, c: (blk(b, c), 0)),
            pl.BlockSpec((chunk, ML_V), lambda b, c: (blk(b, c), 1)),
            pl.BlockSpec((chunk, ML_QK), lambda b, c: (blk(b, c), 4)),
            pl.BlockSpec((ML_QK, chunk), lambda b, c: (0, blk(b, c))),
            pl.BlockSpec((ML_GATE_ROWS, chunk), lambda b, c: (0, blk(b, c))),
            pl.BlockSpec((ML_GATE_ROWS, chunk), lambda b, c: (0, blk(b, jnp.minimum(c + 1, nc - 1)))),
            pl.BlockSpec((ML_GATE_ROWS, 1), lambda b, c: (0, 0)),
            pl.BlockSpec((1, ML_V), lambda b, c: (0, 0)),
        ],
        out_specs=pl.BlockSpec((chunk, ML_V), lambda b, c: (blk(b, c), 0)),
        out_shape=jax.ShapeDtypeStruct((t, ML_V), jnp.bfloat16),
        scratch_shapes=[
            pltpu.VMEM((ML_HEADS, ML_DK, ML_DV + LANES), jnp.float32),
            pltpu.VMEM((ML_GATE_ROWS, LANES), jnp.float32),
            pltpu.VMEM((ML_GATE_ROWS, chunk), jnp.float32),
            pltpu.VMEM((ML_GATE_ROWS, chunk), jnp.float32),
            pltpu.VMEM((ML_GATE_ROWS, chunk), jnp.float32),
        ],
        compiler_params=_params(("arbitrary", "arbitrary")),
        name="mlstm_core",
    )(main, main, main, k_t, g_t, g_t, bg, g_out.reshape(1, ML_V))


def _swa_kernel(sinks_ref, q_ref, kc_ref, kp_ref, vc_ref, vp_ref, gq_ref, gk_ref, bias_ref, o_ref):
    W = SW_WINDOW
    lane = lax.broadcasted_iota(jnp.int32, (W, LANES), 1)
    low = lane < SW_DH
    row_col = lax.broadcasted_iota(jnp.int32, (SW_GROUP * W, 1), 0)
    gq = gq_ref[...]
    gk = gk_ref[...]
    first_variant = jnp.minimum(pl.program_id(1), 1)
    for g in range(SW_KV_HEADS):
        sl = slice(g * LANES, (g + 1) * LANES)
        k_all = jnp.concatenate([kp_ref[:, sl], kc_ref[:, sl]], axis=0).astype(jnp.float32)
        kn_all = k_all * lax.rsqrt(jnp.mean(k_all * k_all, axis=-1, keepdims=True) + EPS)
        kn_all = (kn_all * gk).astype(jnp.bfloat16)
        v_all = jnp.concatenate([vp_ref[:, sl], vc_ref[:, sl]], axis=0)
        sink = jnp.full((SW_GROUP * W, 1), sinks_ref[g * SW_GROUP + SW_GROUP - 1], jnp.float32)
        for j in range(SW_GROUP - 2, -1, -1):
            sink = jnp.where(row_col < (j + 1) * W, sinks_ref[g * SW_GROUP + j], sink)
        for blk in range(SW_Q_BLOCKS):
            rows = slice(blk * W, (blk + 1) * W)
            kn = kn_all[blk * W:(blk + 2) * W]
            v2 = v_all[blk * W:(blk + 2) * W]
            bias = bias_ref[first_variant] if blk == 0 else bias_ref[1]
            parts = []
            for p in range(2):
                c0 = g * SW_GROUP * SW_DH + p * LANES
                qp = q_ref[rows, c0:c0 + LANES].astype(jnp.float32)
                sq = qp * qp
                ss_lo = jnp.sum(jnp.where(low, sq, 0.0), axis=-1, keepdims=True)
                ss_hi = jnp.sum(jnp.where(low, 0.0, sq), axis=-1, keepdims=True)
                rs = jnp.where(low, lax.rsqrt(ss_lo / SW_DH + EPS), lax.rsqrt(ss_hi / SW_DH + EPS))
                qn = qp * rs * gq
                parts.append(jnp.where(low, qn, 0.0).astype(jnp.bfloat16))
                parts.append(jnp.where(low, 0.0, qn).astype(jnp.bfloat16))
            q4 = jnp.concatenate(parts, axis=0)
            scores = _bdot_nt(q4, kn) + bias
            m = jnp.maximum(jnp.max(scores, axis=-1, keepdims=True), sink)
            pexp = jnp.exp2(scores - m)
            denom = jnp.sum(pexp, axis=-1, keepdims=True) + jnp.exp2(sink - m)
            o4 = _bdot(pexp.astype(jnp.bfloat16), v2) * (1.0 / denom)
            for p in range(2):
                oa = o4[(2 * p) * W:(2 * p + 1) * W]
                ob = o4[(2 * p + 1) * W:(2 * p + 2) * W]
                c0 = g * SW_GROUP * SW_DH + p * LANES
                o_ref[rows, c0:c0 + LANES] = jnp.where(low, oa, ob).astype(o_ref.dtype)


def _swa_core(proj, sinks, g_q, g_k, bsz, seq):
    t = proj.shape[0]
    W = SW_WINDOW
    nb = seq // W
    dq = SW_Q_HEADS * SW_DH
    kv_w = SW_KV_HEADS * LANES
    k_blk = dq // kv_w
    v_blk = k_blk + 1
    QB = SW_Q_BLOCKS
    assert nb % QB == 0
    steps = nb // QB
    cur = lambda b, n, s: b * steps + n
    prev = lambda b, n, s: b * nb + jnp.maximum(QB * n - 1, 0)
    gq2 = jnp.concatenate([g_q, g_q]).reshape(1, LANES) * (SW_DH ** -0.5 * LOG2E)
    gk2 = jnp.concatenate([g_k, g_k]).reshape(1, LANES)
    sinks = sinks.astype(jnp.float32) * LOG2E
    qi = (jnp.arange(SW_GROUP * W) % W)[:, None]
    ki = jnp.arange(2 * W)[None, :]
    rel = qi + W - ki
    in_win = (rel >= 0) & (rel < W)
    bias = jnp.stack([jnp.where(in_win & (ki >= W), 0.0, -jnp.inf),
                      jnp.where(in_win, 0.0, -jnp.inf)]).astype(jnp.float32)
    grid_spec = pltpu.PrefetchScalarGridSpec(
        num_scalar_prefetch=1,
        grid=(bsz, steps),
        in_specs=[
            pl.BlockSpec((QB * W, dq), lambda b, n, s: (cur(b, n, s), 0)),
            pl.BlockSpec((QB * W, kv_w), lambda b, n, s: (cur(b, n, s), k_blk)),
            pl.BlockSpec((W, kv_w), lambda b, n, s: (prev(b, n, s), k_blk)),
            pl.BlockSpec((QB * W, kv_w), lambda b, n, s: (cur(b, n, s), v_blk)),
            pl.BlockSpec((W, kv_w), lambda b, n, s: (prev(b, n, s), v_blk)),
            pl.BlockSpec((1, LANES), lambda b, n, s: (0, 0)),
            pl.BlockSpec((1, LANES), lambda b, n, s: (0, 0)),
            pl.BlockSpec((2, SW_GROUP * W, 2 * W), lambda b, n, s: (0, 0, 0)),
        ],
        out_specs=pl.BlockSpec((QB * W, dq), lambda b, n, s: (cur(b, n, s), 0)),
    )
    return pl.pallas_call(
        _swa_kernel,
        grid_spec=grid_spec,
        out_shape=jax.ShapeDtypeStruct((t, dq), jnp.bfloat16),
        compiler_params=_params(("arbitrary", "arbitrary")),
        name="swa_core",
    )(sinks.astype(jnp.float32), proj, proj, proj, proj, proj, gq2, gk2, bias)


def _pack_rounded_pairs(r):
    k = r.shape[1] // 2
    hi = lax.bitcast_convert_type(r[:, :k], jnp.uint32)
    lo = lax.bitcast_convert_type(r[:, k:], jnp.uint32)
    return hi | (lo >> 16)


def _pack_bf16_pairs(a):
    return _pack_rounded_pairs(a.astype(jnp.bfloat16).astype(jnp.float32))


def _unpack_bf16_pairs(u):
    hi = lax.bitcast_convert_type(u & jnp.uint32(0xFFFF0000), jnp.float32)
    lo = lax.bitcast_convert_type(u << 16, jnp.float32)
    return hi, lo


def _router_kernel(a_ref, wo_ref, x_ref, gt_ref, g_ref, sc_ref, sh_ref, wh_ref, wl_ref, b_ref,
                   xo_ref, hn_ref, info_ref, infot_ref, cnt_ref, carry_ref, earlier_ref, *, tm):
    @pl.when(pl.program_id(0) == 0)
    def _():
        carry_ref[...] = jnp.zeros_like(carry_ref)
        r_idx = lax.broadcasted_iota(jnp.int32, (tm, tm), 0)
        c_idx = lax.broadcasted_iota(jnp.int32, (tm, tm), 1)
        earlier_ref[...] = jnp.where(r_idx < c_idx, 1.0, 0.0).astype(jnp.bfloat16)

    x_new = x_ref[...] + gt_ref[...] * _bdot(a_ref[...], wo_ref[...])
    xo_ref[...] = x_new
    hn = _modulated_norm(x_new, g_ref[...], sc_ref[...], sh_ref[...])
    h_hi = hn.astype(jnp.bfloat16)
    hi_f32 = h_hi.astype(jnp.float32)
    h_lo = (hn - hi_f32).astype(jnp.bfloat16)
    hn_ref[...] = _pack_rounded_pairs(hi_f32)
    wide = _bdot_nt(wl_ref[...], h_hi)
    logits = (wide[:ROUTER_LANES] + (_bdot_nt(wh_ref[...], h_lo) + wide[ROUTER_LANES:])
              + b_ref[...])
    E8 = EXPERTS_PER_GROUP
    sub = lax.broadcasted_iota(jnp.int32, (E8, tm), 0).astype(jnp.float32)
    big = float(ROUTER_LANES)
    neg = -jnp.inf

    gl = logits[N_EXPERTS:N_EXPERTS + N_GROUPS]
    gmax = jnp.max(gl, axis=0, keepdims=True)
    gsel = jnp.min(jnp.where(gl == gmax, sub, big), axis=0, keepdims=True)
    p_grp = 1.0 / jnp.sum(jnp.exp(gl - gmax), axis=0, keepdims=True)

    el = logits[0:E8]
    for grp in range(1, N_GROUPS):
        el = jnp.where(gsel == grp, logits[grp * E8:(grp + 1) * E8], el)
    v1 = jnp.max(el, axis=0, keepdims=True)
    j1 = jnp.min(jnp.where(el == v1, sub, big), axis=0, keepdims=True)
    el2 = jnp.where(sub == j1, neg, el)
    v2 = jnp.max(el2, axis=0, keepdims=True)
    j2 = jnp.min(jnp.where(el2 == v2, sub, big), axis=0, keepdims=True)
    i1 = gsel * E8 + j1
    i2 = gsel * E8 + j2
    e21 = jnp.exp(v2 - v1)
    gate1 = p_grp / (1.0 + e21)
    gate2 = p_grp * e21 / (1.0 + e21)

    erow = lax.broadcasted_iota(jnp.int32, (N_EXPERTS, tm), 0).astype(jnp.float32)
    hit1 = erow == i1
    hit2 = erow == i2
    onehot = jnp.where(hit1 | hit2, 1.0, 0.0)
    carry = carry_ref[:, 0:1]
    before = _bdot(onehot.astype(jnp.bfloat16), earlier_ref[...]) + carry
    rank1 = jnp.sum(jnp.where(hit1, before, 0.0), axis=0, keepdims=True)
    rank2 = jnp.sum(jnp.where(hit2, before, 0.0), axis=0, keepdims=True)
    total = carry + jnp.sum(onehot, axis=1, keepdims=True)
    carry_ref[...] = jnp.broadcast_to(total, carry_ref.shape)
    cnt_ref[...] = jnp.broadcast_to(total, cnt_ref.shape)

    info_t = jnp.where(sub == 0, i1, 0.0)
    info_t = jnp.where(sub == 1, i2, info_t)
    info_t = jnp.where(sub == 2, rank1, info_t)
    info_t = jnp.where(sub == 3, rank2, info_t)
    info_t = jnp.where(sub == 4, gate1, info_t)
    info_t = jnp.where(sub == 5, gate2, info_t)
    infot_ref[...] = info_t
    info_ref[...] = jnp.concatenate(
        [info_t, jnp.zeros((ROUTER_LANES - E8, tm), jnp.float32)], axis=0).T


def _outproj_router(a, w_out, x2d, gt, g, sc, sh, w_hi, w_lo, bias, seq):
    t, d = x2d.shape
    tm = 512
    tiles_per_batch = seq // tm
    per_batch = lambda i: (i // tiles_per_batch, 0, 0)
    const = lambda i: (0, 0)
    return pl.pallas_call(
        functools.partial(_router_kernel, tm=tm),
        grid=(t // tm,),
        in_specs=[
            pl.BlockSpec((tm, a.shape[1]), lambda i: (i, 0)),
            pl.BlockSpec(w_out.shape, const),
            pl.BlockSpec((tm, d), lambda i: (i, 0)),
            pl.BlockSpec((None, 1, d), per_batch),
            pl.BlockSpec((1, d), const),
            pl.BlockSpec((None, 1, d), per_batch),
            pl.BlockSpec((None, 1, d), per_batch),
            pl.BlockSpec((ROUTER_LANES, d), const),
            pl.BlockSpec((2 * ROUTER_LANES, d), const),
            pl.BlockSpec((ROUTER_LANES, 1), const),
        ],
        out_specs=[
            pl.BlockSpec((tm, d), lambda i: (i, 0)),
            pl.BlockSpec((tm, d // 2), lambda i: (i, 0)),
            pl.BlockSpec((tm, ROUTER_LANES), lambda i: (i, 0)),
            pl.BlockSpec((EXPERTS_PER_GROUP, tm), lambda i: (0, i)),
            pl.BlockSpec((N_EXPERTS, LANES), const),
        ],
        out_shape=[
            jax.ShapeDtypeStruct((t, d), jnp.float32),
            jax.ShapeDtypeStruct((t, d // 2), jnp.uint32),
            jax.ShapeDtypeStruct((t, ROUTER_LANES), jnp.float32),
            jax.ShapeDtypeStruct((EXPERTS_PER_GROUP, t), jnp.float32),
            jax.ShapeDtypeStruct((N_EXPERTS, LANES), jnp.float32),
        ],
        scratch_shapes=[pltpu.VMEM((N_EXPERTS, LANES), jnp.float32),
                        pltpu.VMEM((tm, tm), jnp.bfloat16)],
        compiler_params=_params(("arbitrary",)),
        name="outproj_router",
    )(a, w_out, x2d, gt, g.reshape(1, d), sc, sh, w_hi, jnp.concatenate([w_hi, w_lo], axis=0), bias)


def _sc_mesh():
    return plsc.VectorSubcoreMesh(core_axis_name="c", subcore_axis_name="s")


def _sc_dispatch(rows, d0, d1, pad_idx, n_slots):
    t, w = rows.shape
    n_pad = pad_idx.shape[1]
    zeros = jnp.zeros((SC_WINDOW, w), rows.dtype)
    sem = (pltpu.PARALLEL, pltpu.ARBITRARY)

    @pl.kernel(out_type=jax.ShapeDtypeStruct((n_slots + SC_WINDOW, w), rows.dtype), mesh=_sc_mesh())
    def dispatch(x_hbm, d0_hbm, d1_hbm, z_hbm, p_hbm, o_hbm):
        def scatter_rows(x_vmem, i0_vmem, i1_vmem):
            cols = pl.ds(pl.program_id(1) * SC_COLS, SC_COLS)
            pltpu.sync_copy(x_vmem, o_hbm.at[i0_vmem.at[0], cols])
            pltpu.sync_copy(x_vmem, o_hbm.at[i1_vmem.at[0], cols])

        pltpu.emit_pipeline(
            scatter_rows,
            grid=(t // SC_WINDOW, w // SC_COLS),
            in_specs=[pl.BlockSpec((SC_WINDOW, SC_COLS), lambda i, j: (i, j)),
                      pl.BlockSpec((1, SC_WINDOW), lambda i, j: (0, i)),
                      pl.BlockSpec((1, SC_WINDOW), lambda i, j: (0, i))],
            out_specs=[],
            core_axis_name=("c", "s"),
            dimension_semantics=sem,
        )(x_hbm, d0_hbm, d1_hbm)

        def scatter_zeros(z_vmem, p_vmem):
            cols = pl.ds(pl.program_id(1) * SC_COLS, SC_COLS)
            pltpu.sync_copy(z_vmem, o_hbm.at[p_vmem.at[0], cols])

        pltpu.emit_pipeline(
            scatter_zeros,
            grid=(n_pad // SC_WINDOW, w // SC_COLS),
            in_specs=[pl.BlockSpec((SC_WINDOW, SC_COLS), lambda i, j: (0, j)),
                      pl.BlockSpec((1, SC_WINDOW), lambda i, j: (0, i))],
            out_specs=[],
            core_axis_name=("c", "s"),
            dimension_semantics=sem,
        )(z_hbm, p_hbm)

    return dispatch(rows, d0, d1, zeros, pad_idx)


def _sc_gather(src, idx):
    n_out = idx.shape[1]
    w = src.shape[1]

    @pl.kernel(out_type=jax.ShapeDtypeStruct((n_out, w), src.dtype), mesh=_sc_mesh())
    def gather(x_hbm, i_hbm, o_hbm):
        def gather_rows(i_vmem, o_vmem):
            cols = pl.ds(pl.program_id(1) * SC_COLS, SC_COLS)
            pltpu.sync_copy(x_hbm.at[i_vmem.at[0], cols], o_vmem)

        pltpu.emit_pipeline(
            gather_rows,
            grid=(n_out // SC_WINDOW, w // SC_COLS),
            in_specs=[pl.BlockSpec((1, SC_WINDOW), lambda i, j: (0, i))],
            out_specs=[pl.BlockSpec((SC_WINDOW, SC_COLS), lambda i, j: (i, j))],
            core_axis_name=("c", "s"),
            dimension_semantics=(pltpu.PARALLEL, pltpu.ARBITRARY),
        )(i_hbm, o_hbm)

    return gather(src, idx)


def _expert_kernel(meta_ref, x_ref, w1_hbm, w3_hbm, w2_hbm, y_ref,
                   w1_buf, w3_buf, w2_buf, w1_c, w3_c, w2_c, sems, *, layer, nb):
    def weight_copies(expert, s):
        return (pltpu.make_async_copy(w1_hbm.at[layer, expert], w1_buf.at[s], sems.at[s, 0]),
                pltpu.make_async_copy(w3_hbm.at[layer, expert], w3_buf.at[s], sems.at[s, 1]),
                pltpu.make_async_copy(w2_hbm.at[layer, expert], w2_buf.at[s], sems.at[s, 2]))

    @pl.when(pl.program_id(0) == 0)
    def _():
        for cp in weight_copies(meta_ref[0], meta_ref[nb]):
            cp.start()

    for j in range(EXPERT_BLOCKS_PER_STEP):
        i = pl.program_id(0) * EXPERT_BLOCKS_PER_STEP + j
        rows = slice(j * MOE_BLOCK, (j + 1) * MOE_BLOCK)
        e = meta_ref[i]
        slot = meta_ref[nb + i]
        nxt = meta_ref[2 * nb + i]
        used = i < meta_ref[3 * nb]
        first = used & ((i == 0) | (e != meta_ref[jnp.maximum(i - 1, 0)]))

        @pl.when(first)
        def _():
            for cp in weight_copies(e, slot):
                cp.wait()

            @pl.when(nxt >= 0)
            def _():
                for cp in weight_copies(nxt, 1 - slot):
                    cp.start()

            w1_c[...] = w1_buf[slot].astype(jnp.bfloat16)
            w3_c[...] = w3_buf[slot].astype(jnp.bfloat16)
            w2_c[...] = w2_buf[slot].astype(jnp.bfloat16)

        @pl.when(used)
        def _():
            x_hi, x_lo = _unpack_bf16_pairs(x_ref[rows, :])
            xb = jnp.concatenate([x_hi, x_lo], axis=-1).astype(jnp.bfloat16)
            h1 = _bdot(xb, w1_c[...])
            h3 = _bdot(xb, w3_c[...])
            act = (h1 * jax.nn.sigmoid(h1) * h3).astype(jnp.bfloat16)
            y_ref[rows, :] = _pack_bf16_pairs(_bdot(act, w2_c[...]))

        @pl.when(jnp.logical_not(used))
        def _():
            y_ref[rows, :] = jnp.zeros((MOE_BLOCK, y_ref.shape[1]), y_ref.dtype)


def _experts(x_slots, blk_meta, w1, w3, w2, layer, nb):
    dp = x_slots.shape[1]
    d, de = w1.shape[-2:]
    step_rows = EXPERT_BLOCKS_PER_STEP * MOE_BLOCK
    assert nb % EXPERT_BLOCKS_PER_STEP == 0
    last_used_step = lambda s: (s[3 * nb] - 1) // EXPERT_BLOCKS_PER_STEP
    grid_spec = pltpu.PrefetchScalarGridSpec(
        num_scalar_prefetch=1,
        grid=(nb // EXPERT_BLOCKS_PER_STEP,),
        in_specs=[
            pl.BlockSpec((step_rows, dp), lambda i, s: (jnp.minimum(i, last_used_step(s)), 0)),
            pl.BlockSpec(memory_space=pl.ANY),
            pl.BlockSpec(memory_space=pl.ANY),
            pl.BlockSpec(memory_space=pl.ANY),
        ],
        out_specs=pl.BlockSpec((step_rows, dp), lambda i, s: (i, 0)),
        scratch_shapes=[
            pltpu.VMEM((2, d, de), jnp.float32),
            pltpu.VMEM((2, d, de), jnp.float32),
            pltpu.VMEM((2, de, d), jnp.float32),
            pltpu.VMEM((d, de), jnp.bfloat16),
            pltpu.VMEM((d, de), jnp.bfloat16),
            pltpu.VMEM((de, d), jnp.bfloat16),
            pltpu.SemaphoreType.DMA((2, 3)),
        ],
    )
    return pl.pallas_call(
        functools.partial(_expert_kernel, layer=layer, nb=nb),
        grid_spec=grid_spec,
        out_shape=jax.ShapeDtypeStruct((nb * MOE_BLOCK, dp), jnp.uint32),
        compiler_params=_params(("arbitrary",)),
        name="moe_experts",
    )(blk_meta, x_slots, w1, w3, w2)


def _combine_kernel(x_ref, y1_ref, y2_ref, info_ref, gt_ref, o_ref):
    o_ref[...] = _moe_combined(x_ref[...], y1_ref, y2_ref, info_ref, gt_ref)


def _combine(x2d, y_pairs, info, gt, seq):
    t, d = x2d.shape
    tm = min(1024, seq)
    tiles_per_batch = seq // tm
    second = t // tm
    return pl.pallas_call(
        _combine_kernel,
        grid=(t // tm,),
        in_specs=[
            pl.BlockSpec((tm, d), lambda i: (i, 0)),
            pl.BlockSpec((tm, d // 2), lambda i: (i, 0)),
            pl.BlockSpec((tm, d // 2), lambda i: (i + second, 0)),
            pl.BlockSpec((tm, ROUTER_LANES), lambda i: (i, 0)),
            pl.BlockSpec((None, 1, d), lambda i: (i // tiles_per_batch, 0, 0)),
        ],
        out_specs=pl.BlockSpec((tm, d), lambda i: (i, 0)),
        out_shape=jax.ShapeDtypeStruct((t, d), jnp.float32),
        compiler_params=_params(("arbitrary",)),
        name="moe_combine",
    )(x2d, y_pairs, y_pairs, info, gt)


def _slot_plan(info_t, cnt, t):
    counts = cnt[:, 0].astype(jnp.int32)
    padded = (counts + MOE_BLOCK - 1) // MOE_BLOCK * MOE_BLOCK
    pad_ends = jnp.cumsum(padded)
    pad_starts = pad_ends - padded
    nb = -(-(2 * t) // MOE_BLOCK) + N_EXPERTS
    n_slots = nb * MOE_BLOCK
    it = info_t.astype(jnp.int32)
    onehot_start = lambda e: jnp.sum(
        jnp.where(e[None, :] == jnp.arange(N_EXPERTS, dtype=jnp.int32)[:, None],
                  pad_starts[:, None], 0), axis=0)
    dest1 = (onehot_start(it[0]) + it[2]).reshape(1, t)
    dest2 = (onehot_start(it[1]) + it[3]).reshape(1, t)
    lane = jnp.arange(MOE_BLOCK, dtype=jnp.int32)[None, :]
    n_padding = (padded - counts)[:, None]
    wrapped = (pad_starts + counts)[:, None] + lane % jnp.maximum(n_padding, 1)
    pad_idx = jnp.where(n_padding > 0, wrapped, n_slots + lane % SC_WINDOW).reshape(-1)
    n_real = pad_ends[-1] // MOE_BLOCK
    n_used = -(-n_real // EXPERT_BLOCKS_PER_STEP) * EXPERT_BLOCKS_PER_STEP
    tail = jnp.arange((EXPERT_BLOCKS_PER_STEP - 1) * MOE_BLOCK, dtype=jnp.int32)
    tail_idx = jnp.where(tail < (n_used - n_real) * MOE_BLOCK, pad_ends[-1] + tail,
                         n_slots + tail % SC_WINDOW)
    pad_idx = jnp.concatenate([pad_idx, tail_idx]).reshape(1, -1)
    experts = jnp.arange(N_EXPERTS, dtype=jnp.int32)
    blk = jnp.arange(nb, dtype=jnp.int32)
    blk_exp = jnp.minimum(
        jnp.sum((pad_ends[None, :] <= (blk * MOE_BLOCK)[:, None]).astype(jnp.int32), axis=1),
        N_EXPERTS - 1)
    blk_exp = jnp.where(blk >= n_real, jnp.max(jnp.where(padded > 0, experts, 0)), blk_exp)
    prev_exp = jnp.concatenate([jnp.full((1,), -1, jnp.int32), blk_exp[:-1]])
    is_first = (blk < n_used) & (blk_exp != prev_exp)
    blk_slot = (jnp.cumsum(is_first.astype(jnp.int32)) + 1) % 2
    later = (experts[None, :] > experts[:, None]) & (padded[None, :] > 0)
    nxt_of = jnp.min(jnp.where(later, experts[None, :], N_EXPERTS), axis=1)
    nxt_of = jnp.where(nxt_of == N_EXPERTS, -1, nxt_of)
    blk_nxt = jnp.sum(jnp.where(blk_exp[:, None] == experts[None, :], nxt_of[None, :], 0), axis=1)
    blk_meta = jnp.concatenate([blk_exp, blk_slot, blk_nxt, n_used[None]]).astype(jnp.int32)
    return dest1, dest2, pad_idx, blk_meta, nb, n_slots


def _mixer_out_and_moe(a, w_out, x2d, gt1, g, sc, sh, w_group, b_group, w_router, b_router,
                       w1, w3, w2, layer, seq):
    t, d = x2d.shape
    w_cat = jnp.zeros((ROUTER_LANES, d), jnp.float32)
    w_cat = w_cat.at[:N_EXPERTS].set(w_router.T).at[N_EXPERTS:N_EXPERTS + N_GROUPS].set(w_group.T)
    b_cat = jnp.zeros((ROUTER_LANES, 1), jnp.float32)
    b_cat = b_cat.at[:N_EXPERTS, 0].set(b_router).at[N_EXPERTS:N_EXPERTS + N_GROUPS, 0].set(b_group)
    w_hi, w_lo = _split_hi_lo(w_cat)
    x_new, hn, info, info_t, cnt = _outproj_router(a, w_out, x2d, gt1, g, sc, sh, w_hi, w_lo,
                                                   b_cat, seq)
    dest1, dest2, pad_idx, blk_meta, nb, n_slots = _slot_plan(info_t, cnt, t)
    x_slots = _sc_dispatch(hn, dest1, dest2, pad_idx, n_slots)
    y_slots = _experts(x_slots, blk_meta, w1, w3, w2, layer, nb)
    y_pairs = _sc_gather(y_slots, jnp.concatenate([dest1, dest2], axis=1))
    return x_new, y_pairs, info


def kernel(x, c, w_ada, b_ada, norm1_g, norm2_g, ml_w_in, ml_b_gate, ml_g_out, ml_w_out,
           sw_w_in, sw_g_q, sw_g_k, sw_sinks, sw_w_out, moe_w_group, moe_b_group,
           moe_w_router, moe_b_router, moe_w1, moe_w3, moe_w2):
    bsz, seq, d = x.shape
    depth = w_ada.shape[0]
    bf = jnp.bfloat16
    mod = _ada_mod(c, w_ada, b_ada)
    x2d = x.reshape(bsz * seq, d)
    pending = None
    for layer in range(depth):
        sh1, sc1, gt1, sh2, sc2, gt2 = [
            mod[layer, :, i * d:(i + 1) * d].reshape(bsz, 1, d) for i in range(6)]
        j = layer // 2
        if layer % 2 == 0:
            w = ml_w_in[j]
            q_w, k_w = w[:, :ML_QK], w[:, ML_QK:2 * ML_QK]
            v_w = w[:, 2 * ML_QK:2 * ML_QK + ML_V]
            o_w = w[:, 2 * ML_QK + ML_V:2 * ML_QK + 2 * ML_V]
            g_w = w[:, 2 * ML_QK + 2 * ML_V:]
            w_main = jnp.concatenate([v_w, o_w, q_w], axis=1).astype(bf)
            wg_t = jnp.zeros((ML_GATE_ROWS, d), jnp.float32).at[:2 * ML_HEADS].set(g_w.T)
            wg_hi, wg_lo = _split_hi_lo(wg_t)
            wk_t = k_w.T.astype(bf)
            w_out = ml_w_out[j].astype(bf)
            outs = _inproj(x2d, norm1_g[layer], sc1, sh1, w_main, seq,
                           ml_extra=(wk_t, wg_hi, wg_lo),
                           q_cols=(2 * ML_V, 2 * ML_V + ML_QK), q_scale=ML_DK ** -0.5,
                           gate_cols=(ML_V, 2 * ML_V), pending=pending)
            if pending is not None:
                x2d, outs = outs[0], outs[1:]
            main, k_t, g_t = outs
            a = _mlstm_core(main, k_t, g_t, ml_b_gate[j], ml_g_out[j], bsz, seq)
        else:
            w = sw_w_in[j]
            dq = SW_Q_HEADS * SW_DH
            dkv = SW_KV_HEADS * SW_DH
            dup = lambda m: jnp.concatenate(
                [m.reshape(d, SW_KV_HEADS, 1, SW_DH)] * 2, axis=2).reshape(d, 2 * dkv)
            w_main = jnp.concatenate(
                [w[:, :dq], dup(w[:, dq:dq + dkv]), dup(w[:, dq + dkv:])], axis=1).astype(bf)
            w_out = sw_w_out[j].astype(bf)
            outs = _inproj(x2d, norm1_g[layer], sc1, sh1, w_main, seq, pending=pending)
            if pending is not None:
                x2d, outs = outs[0], outs[1:]
            a = _swa_core(outs[0], sw_sinks[j], sw_g_q[j], sw_g_k[j], bsz, seq)
        x2d, y_pairs, info = _mixer_out_and_moe(
            a, w_out, x2d, gt1, norm2_g[layer], sc2, sh2, moe_w_group[layer], moe_b_group[layer],
            moe_w_router[layer], moe_b_router[layer], moe_w1, moe_w3, moe_w2, layer, seq)
        pending = (y_pairs, info, gt2)
    y_pairs, info, gt2 = pending
    return _combine(x2d, y_pairs, info, gt2, seq).reshape(bsz, seq, d)
```

```python
import functools

import jax
import jax.numpy as jnp
from jax import lax
from jax.experimental import pallas as pl
from jax.experimental.pallas import tpu as pltpu
from jax.experimental.pallas import tpu_sc as plsc

EPS = 1e-6
GATE_CAP = 15.0
LOG2E = 1.4426950408889634

ML_HEADS = 4
ML_DK = 128
ML_DV = 256
ML_QK = ML_HEADS * ML_DK
ML_V = ML_HEADS * ML_DV
ML_GATE_ROWS = 16

SW_Q_HEADS = 16
SW_KV_HEADS = 4
SW_GROUP = SW_Q_HEADS // SW_KV_HEADS
SW_DH = 64
SW_WINDOW = 128
SW_Q_BLOCKS = 1
LANES = 128

N_GROUPS = 8
EXPERTS_PER_GROUP = 8
N_EXPERTS = N_GROUPS * EXPERTS_PER_GROUP
MOE_BLOCK = 256
ROUTER_LANES = 128
SC_WINDOW = 128
SC_COLS = 256
EXPERT_BLOCKS_PER_STEP = 4

VMEM_LIMIT = 56 * 1024 * 1024

_NT = (((1,), (1,)), ((), ()))


def _bdot(a, b):
    return jnp.dot(a, b, preferred_element_type=jnp.float32)


def _bdot_nt(a, b):
    return lax.dot_general(a, b, _NT, preferred_element_type=jnp.float32)


def _split_hi_lo(a):
    hi = a.astype(jnp.bfloat16)
    lo = (a - hi.astype(jnp.float32)).astype(jnp.bfloat16)
    return hi, lo


def _params(sem):
    return pltpu.CompilerParams(dimension_semantics=sem, vmem_limit_bytes=VMEM_LIMIT)


def _ada_kernel(c_ref, w_ref, b_ref, o_ref):
    c = c_ref[...]
    cond = c * jax.nn.sigmoid(c)
    c_hi, c_lo = _split_hi_lo(cond)
    w_hi, w_lo = _split_hi_lo(w_ref[...])
    acc = _bdot(c_hi, w_hi) + (_bdot(c_lo, w_hi) + _bdot(c_hi, w_lo))
    o_ref[...] = acc + b_ref[...]


def _ada_mod(c, w_ada, b_ada):
    depth, d, n = w_ada.shape
    bsz = c.shape[0]
    rows = 8
    tn = 768
    c_pad = jnp.zeros((rows, d), jnp.float32).at[:bsz].set(c)
    out = pl.pallas_call(
        _ada_kernel,
        grid=(depth, n // tn),
        in_specs=[
            pl.BlockSpec((rows, d), lambda l, j: (0, 0)),
            pl.BlockSpec((None, d, tn), lambda l, j: (l, 0, j)),
            pl.BlockSpec((None, 1, tn), lambda l, j: (l, 0, j)),
        ],
        out_specs=pl.BlockSpec((None, rows, tn), lambda l, j: (l, 0, j)),
        out_shape=jax.ShapeDtypeStruct((depth, rows, n), jnp.float32),
        compiler_params=_params(("arbitrary", "arbitrary")),
        name="ada_mod",
    )(c_pad, w_ada, b_ada.reshape(depth, 1, n))
    return out[:, :bsz]


def _modulated_norm(x, g, sc, sh):
    y = x * lax.rsqrt(jnp.mean(x * x, axis=-1, keepdims=True) + EPS)
    return y * (g * (1.0 + sc)) + sh


def _moe_combined(x, y1_ref, y2_ref, info_ref, gt_ref):
    info = info_ref[...]
    g1 = info[:, 4:5]
    g2 = info[:, 5:6]
    y1_hi, y1_lo = _unpack_bf16_pairs(y1_ref[...])
    y2_hi, y2_lo = _unpack_bf16_pairs(y2_ref[...])
    y = jnp.concatenate([g1 * y1_hi + g2 * y2_hi, g1 * y1_lo + g2 * y2_lo], axis=-1)
    return x + gt_ref[...] * y


def _inproj_kernel(*refs, n_main, chunk, q_cols, q_scale, gate_cols, with_ml, with_combine):
    refs = list(refs)
    n_in = 5 + (4 if with_combine else 0) + (3 if with_ml else 0)
    ins, outs = refs[:n_in], refs[n_in:]
    x_ref = ins.pop(0)
    x = x_ref[...]
    if with_combine:
        y1_ref, y2_ref, info_ref, gtp_ref = ins[:4]
        ins = ins[4:]
        x = _moe_combined(x, y1_ref, y2_ref, info_ref, gtp_ref)
        outs.pop(0)[...] = x
    g_ref, sc_ref, sh_ref, w_ref = ins[:4]
    o_ref = outs[0]
    if with_ml:
        wk_ref, wgh_ref, wgl_ref = ins[4:]
        kt_ref, gt_ref = outs[1:]
    hn = _modulated_norm(x, g_ref[...], sc_ref[...], sh_ref[...])
    hb = hn.astype(jnp.bfloat16)
    for c0 in range(0, n_main, chunk):
        acc = _bdot(hb, w_ref[:, c0:c0 + chunk])
        if q_cols is not None and q_cols[0] <= c0 < q_cols[1]:
            acc = acc * q_scale
        if gate_cols is not None and gate_cols[0] <= c0 < gate_cols[1]:
            acc = jax.nn.sigmoid(acc)
        o_ref[:, c0:c0 + chunk] = acc.astype(o_ref.dtype)
    if with_ml:
        kt_ref[...] = _bdot_nt(wk_ref[...], hb).astype(kt_ref.dtype)
        h_lo = (hn - hb.astype(jnp.float32)).astype(jnp.bfloat16)
        gt_ref[...] = (_bdot_nt(wgh_ref[...], hb)
                       + (_bdot_nt(wgh_ref[...], h_lo) + _bdot_nt(wgl_ref[...], hb)))


def _inproj(x2d, g, sc, sh, w_main, seq, *, ml_extra=None, q_cols=None, q_scale=1.0,
            gate_cols=None, pending=None):
    t, d = x2d.shape
    tm = 512
    n_main = w_main.shape[1]
    tiles_per_batch = seq // tm
    row = lambda i: (i, 0)
    per_batch = lambda i: (i // tiles_per_batch, 0, 0)
    const = lambda i: (0, 0)
    in_specs = [pl.BlockSpec((tm, d), row)]
    args = [x2d]
    out_specs, out_shape = [], []
    if pending is not None:
        y_pairs, info, gt_prev = pending
        second = t // tm
        in_specs += [pl.BlockSpec((tm, d // 2), row),
                     pl.BlockSpec((tm, d // 2), lambda i: (i + second, 0)),
                     pl.BlockSpec((tm, ROUTER_LANES), row),
                     pl.BlockSpec((None, 1, d), per_batch)]
        args += [y_pairs, y_pairs, info, gt_prev]
        out_specs += [pl.BlockSpec((tm, d), row)]
        out_shape += [jax.ShapeDtypeStruct((t, d), jnp.float32)]
    in_specs += [
        pl.BlockSpec((1, d), const),
        pl.BlockSpec((None, 1, d), per_batch),
        pl.BlockSpec((None, 1, d), per_batch),
        pl.BlockSpec((d, n_main), const),
    ]
    args += [g.reshape(1, d), sc, sh, w_main]
    out_specs += [pl.BlockSpec((tm, n_main), row)]
    out_shape += [jax.ShapeDtypeStruct((t, n_main), jnp.bfloat16)]
    if ml_extra is not None:
        wk_t, wg_hi, wg_lo = ml_extra
        in_specs += [pl.BlockSpec(wk_t.shape, const),
                     pl.BlockSpec(wg_hi.shape, const),
                     pl.BlockSpec(wg_lo.shape, const)]
        args += [wk_t, wg_hi, wg_lo]
        out_specs += [pl.BlockSpec((wk_t.shape[0], tm), lambda i: (0, i)),
                      pl.BlockSpec((ML_GATE_ROWS, tm), lambda i: (0, i))]
        out_shape += [jax.ShapeDtypeStruct((wk_t.shape[0], t), jnp.bfloat16),
                      jax.ShapeDtypeStruct((ML_GATE_ROWS, t), jnp.float32)]
    kern = functools.partial(_inproj_kernel, n_main=n_main, chunk=512, q_cols=q_cols,
                             q_scale=q_scale, gate_cols=gate_cols, with_ml=ml_extra is not None,
                             with_combine=pending is not None)
    return pl.pallas_call(
        kern,
        grid=(t // tm,),
        in_specs=in_specs,
        out_specs=out_specs,
        out_shape=out_shape,
        compiler_params=_params(("arbitrary",)),
        name="inproj_ml" if ml_extra is not None else "inproj_sw",
    )(*args)


def _mlstm_gate_terms(graw, bias, upper):
    H = ML_HEADS
    L = graw.shape[1]
    z = graw + bias
    gates = GATE_CAP * jnp.tanh(z / GATE_CAP)
    log_f = jnp.minimum(gates, 0.0) - jnp.log1p(jnp.exp(-jnp.abs(gates)))
    row = lax.broadcasted_iota(jnp.int32, (ML_GATE_ROWS, L), 0)
    lane = lax.broadcasted_iota(jnp.int32, (ML_GATE_ROWS, L), 1)
    is_i = row < H
    slab = jnp.where(is_i, gates, log_f)
    a1 = slab.astype(jnp.bfloat16)
    r1 = slab - a1.astype(jnp.float32)
    a2 = r1.astype(jnp.bfloat16)
    a3 = (r1 - a2.astype(jnp.float32)).astype(jnp.bfloat16)
    cum = _bdot(a1, upper) + (_bdot(a2, upper) + _bdot(a3, upper))
    ib = jnp.where(is_i, gates, cum)
    b = pltpu.roll(ib, ML_GATE_ROWS - H, 0)
    u = ib - b
    cm = u
    shift = 1
    while shift < L:
        cm = jnp.maximum(cm, jnp.where(lane >= shift, pltpu.roll(cm, shift, 1), -jnp.inf))
        shift *= 2
    return b, u, cm


def _mlstm_kernel(v_ref, o_ref, q_ref, kt_ref, gt_ref, gtn_ref, bg_ref, gout_ref, out_ref,
                  c_ref, m_ref, b_ref, u_ref, cm_ref, *, chunk):
    L = chunk
    H, dk, dv = ML_HEADS, ML_DK, ML_DV
    r_idx = lax.broadcasted_iota(jnp.int32, (L, L), 0)
    c_idx = lax.broadcasted_iota(jnp.int32, (L, L), 1)
    upper = jnp.where(r_idx <= c_idx, 1.0, 0.0).astype(jnp.bfloat16)
    causal = r_idx >= c_idx
    row = lax.broadcasted_iota(jnp.int32, (ML_GATE_ROWS, L), 0)
    ones_col = jnp.where(lax.broadcasted_iota(jnp.int32, (L, LANES), 1) == 0, 1.0, 0.0
                         ).astype(jnp.bfloat16)

    @pl.when(pl.program_id(1) == 0)
    def _():
        c_ref[...] = jnp.zeros_like(c_ref)
        m_ref[...] = jnp.zeros_like(m_ref)
        b0, u0, cm0 = _mlstm_gate_terms(gt_ref[...], bg_ref[...], upper)
        b_ref[...] = b0
        u_ref[...] = u0
        cm_ref[...] = cm0

    b16 = b_ref[...]
    u16 = u_ref[...]
    cm16 = cm_ref[...]
    b_n, u_n, cm_n = _mlstm_gate_terms(gtn_ref[...], bg_ref[...], upper)
    b_ref[...] = b_n
    u_ref[...] = u_n
    cm_ref[...] = cm_n

    m_prev = m_ref[:, 0:1]
    z16 = jnp.maximum(m_prev, cm16)
    w_inter16 = jnp.exp(m_prev - z16)
    e_negm16 = jnp.exp(-(b16 + z16))
    z_last = z16[:, L - 1:L]
    w_state16 = jnp.exp(u16 - z_last)
    decay16 = jnp.exp(m_prev - z_last)
    m_ref[...] = jnp.broadcast_to(b16[:, L - 1:L] + z_last, m_ref.shape)
    stacked = jnp.where(row < H, z16,
                        jnp.where(row < 2 * H, pltpu.roll(w_inter16, H, 0),
                                  pltpu.roll(e_negm16, 2 * H, 0)))
    cols = jnp.concatenate(
        [stacked, jnp.zeros((LANES - ML_GATE_ROWS, L), jnp.float32)], axis=0).T

    for h in range(H):
        u_r = u16[h:h + 1, :]
        z_c = cols[:, h:h + 1]
        w_inter = cols[:, H + h:H + h + 1]
        e_negm = cols[:, 2 * H + h:2 * H + h + 1]
        c_ext = c_ref[h]
        q = q_ref[:, h * dk:(h + 1) * dk]
        kt = kt_ref[h * dk:(h + 1) * dk, :]
        v_ext = jnp.concatenate([v_ref[:, h * dv:(h + 1) * dv], ones_col], axis=-1)

        w_intra = jnp.exp(jnp.where(causal, u_r - z_c, -jnp.inf))
        s = (_bdot(q, kt) * w_intra).astype(jnp.bfloat16)
        nd = w_inter * _bdot(q, c_ext.astype(jnp.bfloat16)) + _bdot(s, v_ext)
        den = nd[:, dv:dv + 1]
        hb = nd[:, :dv] * (1.0 / jnp.maximum(jnp.abs(den), e_negm))

        kw = (kt.astype(jnp.float32) * w_state16[h:h + 1, :]).astype(jnp.bfloat16)
        c_ref[h] = decay16[h:h + 1, :] * c_ext + _bdot(kw, v_ext)

        y = hb * lax.rsqrt(jnp.mean(hb * hb, axis=-1, keepdims=True) + EPS)
        y = y * gout_ref[:, h * dv:(h + 1) * dv]
        og = o_ref[:, h * dv:(h + 1) * dv].astype(jnp.float32)
        out_ref[:, h * dv:(h + 1) * dv] = (y * og).astype(out_ref.dtype)


def _mlstm_core(main, k_t, g_t, b_gate, g_out, bsz, seq, chunk=256):
    t = main.shape[0]
    nc = seq // chunk
    blk = lambda b, c: b * nc + c
    bg = jnp.zeros((ML_GATE_ROWS, 1), jnp.float32).at[:2 * ML_HEADS, 0].set(b_gate)
    return pl.pallas_call(
        functools.partial(_mlstm_kernel, chunk=chunk),
        grid=(bsz, nc),
        in_specs=[
            pl.BlockSpec((chunk, ML_V), lambda b, c: (blk(b, c), 0)),
            pl.BlockSpec((chunk, ML_V), lambda b, c: (blk(b, c), 1)),
            pl.BlockSpec((chunk, ML_QK), lambda b, c: (blk(b, c), 4)),
            pl.BlockSpec((ML_QK, chunk), lambda b, c: (0, blk(b, c))),
            pl.BlockSpec((ML_GATE_ROWS, chunk), lambda b, c: (0, blk(b, c))),
            pl.BlockSpec((ML_GATE_ROWS, chunk), lambda b, c: (0, blk(b, jnp.minimum(c + 1, nc - 1)))),
            pl.BlockSpec((ML_GATE_ROWS, 1), lambda b, c: (0, 0)),
            pl.BlockSpec((1, ML_V), lambda b, c: (0, 0)),
        ],
        out_specs=pl.BlockSpec((chunk, ML_V), lambda b, c: (blk(b, c), 0)),
        out_shape=jax.ShapeDtypeStruct((t, ML_V), jnp.bfloat16),
        scratch_shapes=[
            pltpu.VMEM((ML_HEADS, ML_DK, ML_DV + LANES), jnp.float32),
            pltpu.VMEM((ML_GATE_ROWS, LANES), jnp.float32),
            pltpu.VMEM((ML_GATE_ROWS, chunk), jnp.float32),
            pltpu.VMEM((ML_GATE_ROWS, chunk), jnp.float32),
            pltpu.VMEM((ML_GATE_ROWS, chunk), jnp.float32),
        ],
        compiler_params=_params(("arbitrary", "arbitrary")),
        name="mlstm_core",
    )(main, main, main, k_t, g_t, g_t, bg, g_out.reshape(1, ML_V))


def _swa_kernel(sinks_ref, q_ref, kc_ref, kp_ref, vc_ref, vp_ref, gq_ref, gk_ref, bias_ref, o_ref):
    W = SW_WINDOW
    lane = lax.broadcasted_iota(jnp.int32, (W, LANES), 1)
    low = lane < SW_DH
    row_col = lax.broadcasted_iota(jnp.int32, (SW_GROUP * W, 1), 0)
    gq = gq_ref[...]
    gk = gk_ref[...]
    first_variant = jnp.minimum(pl.program_id(1), 1)
    for g in range(SW_KV_HEADS):
        sl = slice(g * LANES, (g + 1) * LANES)
        k_all = jnp.concatenate([kp_ref[:, sl], kc_ref[:, sl]], axis=0).astype(jnp.float32)
        kn_all = k_all * lax.rsqrt(jnp.mean(k_all * k_all, axis=-1, keepdims=True) + EPS)
        kn_all = (kn_all * gk).astype(jnp.bfloat16)
        v_all = jnp.concatenate([vp_ref[:, sl], vc_ref[:, sl]], axis=0)
        sink = jnp.full((SW_GROUP * W, 1), sinks_ref[g * SW_GROUP + SW_GROUP - 1], jnp.float32)
        for j in range(SW_GROUP - 2, -1, -1):
            sink = jnp.where(row_col < (j + 1) * W, sinks_ref[g * SW_GROUP + j], sink)
        for blk in range(SW_Q_BLOCKS):
            rows = slice(blk * W, (blk + 1) * W)
            kn = kn_all[blk * W:(blk + 2) * W]
            v2 = v_all[blk * W:(blk + 2) * W]
            bias = bias_ref[first_variant] if blk == 0 else bias_ref[1]
            parts = []
            for p in range(2):
                c0 = g * SW_GROUP * SW_DH + p * LANES
                qp = q_ref[rows, c0:c0 + LANES].astype(jnp.float32)
                sq = qp * qp
                ss_lo = jnp.sum(jnp.where(low, sq, 0.0), axis=-1, keepdims=True)
                ss_hi = jnp.sum(jnp.where(low, 0.0, sq), axis=-1, keepdims=True)
                rs = jnp.where(low, lax.rsqrt(ss_lo / SW_DH + EPS), lax.rsqrt(ss_hi / SW_DH + EPS))
                qn = qp * rs * gq
                parts.append(jnp.where(low, qn, 0.0).astype(jnp.bfloat16))
                parts.append(jnp.where(low, 0.0, qn).astype(jnp.bfloat16))
            q4 = jnp.concatenate(parts, axis=0)
            scores = _bdot_nt(q4, kn) + bias
            m = jnp.maximum(jnp.max(scores, axis=-1, keepdims=True), sink)
            pexp = jnp.exp2(scores - m)
            denom = jnp.sum(pexp, axis=-1, keepdims=True) + jnp.exp2(sink - m)
            o4 = _bdot(pexp.astype(jnp.bfloat16), v2) * (1.0 / denom)
            for p in range(2):
                oa = o4[(2 * p) * W:(2 * p + 1) * W]
                ob = o4[(2 * p + 1) * W:(2 * p + 2) * W]
                c0 = g * SW_GROUP * SW_DH + p * LANES
                o_ref[rows, c0:c0 + LANES] = jnp.where(low, oa, ob).astype(o_ref.dtype)


def _swa_core(proj, sinks, g_q, g_k, bsz, seq):
    t = proj.shape[0]
    W = SW_WINDOW
    nb = seq // W
    dq = SW_Q_HEADS * SW_DH
    kv_w = SW_KV_HEADS * LANES
    k_blk = dq // kv_w
    v_blk = k_blk + 1
    QB = SW_Q_BLOCKS
    assert nb % QB == 0
    steps = nb // QB
    cur = lambda b, n, s: b * steps + n
    prev = lambda b, n, s: b * nb + jnp.maximum(QB * n - 1, 0)
    gq2 = jnp.concatenate([g_q, g_q]).reshape(1, LANES) * (SW_DH ** -0.5 * LOG2E)
    gk2 = jnp.concatenate([g_k, g_k]).reshape(1, LANES)
    sinks = sinks.astype(jnp.float32) * LOG2E
    qi = (jnp.arange(SW_GROUP * W) % W)[:, None]
    ki = jnp.arange(2 * W)[None, :]
    rel = qi + W - ki
    in_win = (rel >= 0) & (rel < W)
    bias = jnp.stack([jnp.where(in_win & (ki >= W), 0.0, -jnp.inf),
                      jnp.where(in_win, 0.0, -jnp.inf)]).astype(jnp.float32)
    grid_spec = pltpu.PrefetchScalarGridSpec(
        num_scalar_prefetch=1,
        grid=(bsz, steps),
        in_specs=[
            pl.BlockSpec((QB * W, dq), lambda b, n, s: (cur(b, n, s), 0)),
            pl.BlockSpec((QB * W, kv_w), lambda b, n, s: (cur(b, n, s), k_blk)),
            pl.BlockSpec((W, kv_w), lambda b, n, s: (prev(b, n, s), k_blk)),
            pl.BlockSpec((QB * W, kv_w), lambda b, n, s: (cur(b, n, s), v_blk)),
            pl.BlockSpec((W, kv_w), lambda b, n, s: (prev(b, n, s), v_blk)),
            pl.BlockSpec((1, LANES), lambda b, n, s: (0, 0)),
            pl.BlockSpec((1, LANES), lambda b, n, s: (0, 0)),
            pl.BlockSpec((2, SW_GROUP * W, 2 * W), lambda b, n, s: (0, 0, 0)),
        ],
        out_specs=pl.BlockSpec((QB * W, dq), lambda b, n, s: (cur(b, n, s), 0)),
    )
    return pl.pallas_call(
        _swa_kernel,
        grid_spec=grid_spec,
        out_shape=jax.ShapeDtypeStruct((t, dq), jnp.bfloat16),
        compiler_params=_params(("arbitrary", "arbitrary")),
        name="swa_core",
    )(sinks.astype(jnp.float32), proj, proj, proj, proj, proj, gq2, gk2, bias)


def _pack_rounded_pairs(r):
    k = r.shape[1] // 2
    hi = lax.bitcast_convert_type(r[:, :k], jnp.uint32)
    lo = lax.bitcast_convert_type(r[:, k:], jnp.uint32)
    return hi | (lo >> 16)


def _pack_bf16_pairs(a):
    return _pack_rounded_pairs(a.astype(jnp.bfloat16).astype(jnp.float32))


def _unpack_bf16_pairs(u):
    hi = lax.bitcast_convert_type(u & jnp.uint32(0xFFFF0000), jnp.float32)
    lo = lax.bitcast_convert_type(u << 16, jnp.float32)
    return hi, lo


def _router_kernel(a_ref, wo_ref, x_ref, gt_ref, g_ref, sc_ref, sh_ref, wh_ref, wl_ref, b_ref,
                   xo_ref, hn_ref, info_ref, infot_ref, cnt_ref, carry_ref, earlier_ref, *, tm):
    @pl.when(pl.program_id(0) == 0)
    def _():
        carry_ref[...] = jnp.zeros_like(carry_ref)
        r_idx = lax.broadcasted_iota(jnp.int32, (tm, tm), 0)
        c_idx = lax.broadcasted_iota(jnp.int32, (tm, tm), 1)
        earlier_ref[...] = jnp.where(r_idx < c_idx, 1.0, 0.0).astype(jnp.bfloat16)

    x_new = x_ref[...] + gt_ref[...] * _bdot(a_ref[...], wo_ref[...])
    xo_ref[...] = x_new
    hn = _modulated_norm(x_new, g_ref[...], sc_ref[...], sh_ref[...])
    h_hi = hn.astype(jnp.bfloat16)
    hi_f32 = h_hi.astype(jnp.float32)
    h_lo = (hn - hi_f32).astype(jnp.bfloat16)
    hn_ref[...] = _pack_rounded_pairs(hi_f32)
    wide = _bdot_nt(wl_ref[...], h_hi)
    logits = (wide[:ROUTER_LANES] + (_bdot_nt(wh_ref[...], h_lo) + wide[ROUTER_LANES:])
              + b_ref[...])
    E8 = EXPERTS_PER_GROUP
    sub = lax.broadcasted_iota(jnp.int32, (E8, tm), 0).astype(jnp.float32)
    big = float(ROUTER_LANES)
    neg = -jnp.inf

    gl = logits[N_EXPERTS:N_EXPERTS + N_GROUPS]
    gmax = jnp.max(gl, axis=0, keepdims=True)
    gsel = jnp.min(jnp.where(gl == gmax, sub, big), axis=0, keepdims=True)
    p_grp = 1.0 / jnp.sum(jnp.exp(gl - gmax), axis=0, keepdims=True)

    el = logits[0:E8]
    for grp in range(1, N_GROUPS):
        el = jnp.where(gsel == grp, logits[grp * E8:(grp + 1) * E8], el)
    v1 = jnp.max(el, axis=0, keepdims=True)
    j1 = jnp.min(jnp.where(el == v1, sub, big), axis=0, keepdims=True)
    el2 = jnp.where(sub == j1, neg, el)
    v2 = jnp.max(el2, axis=0, keepdims=True)
    j2 = jnp.min(jnp.where(el2 == v2, sub, big), axis=0, keepdims=True)
    i1 = gsel * E8 + j1
    i2 = gsel * E8 + j2
    e21 = jnp.exp(v2 - v1)
    gate1 = p_grp / (1.0 + e21)
    gate2 = p_grp * e21 / (1.0 + e21)

    erow = lax.broadcasted_iota(jnp.int32, (N_EXPERTS, tm), 0).astype(jnp.float32)
    hit1 = erow == i1
    hit2 = erow == i2
    onehot = jnp.where(hit1 | hit2, 1.0, 0.0)
    carry = carry_ref[:, 0:1]
    before = _bdot(onehot.astype(jnp.bfloat16), earlier_ref[...]) + carry
    rank1 = jnp.sum(jnp.where(hit1, before, 0.0), axis=0, keepdims=True)
    rank2 = jnp.sum(jnp.where(hit2, before, 0.0), axis=0, keepdims=True)
    total = carry + jnp.sum(onehot, axis=1, keepdims=True)
    carry_ref[...] = jnp.broadcast_to(total, carry_ref.shape)
    cnt_ref[...] = jnp.broadcast_to(total, cnt_ref.shape)

    info_t = jnp.where(sub == 0, i1, 0.0)
    info_t = jnp.where(sub == 1, i2, info_t)
    info_t = jnp.where(sub == 2, rank1, info_t)
    info_t = jnp.where(sub == 3, rank2, info_t)
    info_t = jnp.where(sub == 4, gate1, info_t)
    info_t = jnp.where(sub == 5, gate2, info_t)
    infot_ref[...] = info_t
    info_ref[...] = jnp.concatenate(
        [info_t, jnp.zeros((ROUTER_LANES - E8, tm), jnp.float32)], axis=0).T


def _outproj_router(a, w_out, x2d, gt, g, sc, sh, w_hi, w_lo, bias, seq):
    t, d = x2d.shape
    tm = 512
    tiles_per_batch = seq // tm
    per_batch = lambda i: (i // tiles_per_batch, 0, 0)
    const = lambda i: (0, 0)
    return pl.pallas_call(
        functools.partial(_router_kernel, tm=tm),
        grid=(t // tm,),
        in_specs=[
            pl.BlockSpec((tm, a.shape[1]), lambda i: (i, 0)),
            pl.BlockSpec(w_out.shape, const),
            pl.BlockSpec((tm, d), lambda i: (i, 0)),
            pl.BlockSpec((None, 1, d), per_batch),
            pl.BlockSpec((1, d), const),
            pl.BlockSpec((None, 1, d), per_batch),
            pl.BlockSpec((None, 1, d), per_batch),
            pl.BlockSpec((ROUTER_LANES, d), const),
            pl.BlockSpec((2 * ROUTER_LANES, d), const),
            pl.BlockSpec((ROUTER_LANES, 1), const),
        ],
        out_specs=[
            pl.BlockSpec((tm, d), lambda i: (i, 0)),
            pl.BlockSpec((tm, d // 2), lambda i: (i, 0)),
            pl.BlockSpec((tm, ROUTER_LANES), lambda i: (i, 0)),
            pl.BlockSpec((EXPERTS_PER_GROUP, tm), lambda i: (0, i)),
            pl.BlockSpec((N_EXPERTS, LANES), const),
        ],
        out_shape=[
            jax.ShapeDtypeStruct((t, d), jnp.float32),
            jax.ShapeDtypeStruct((t, d // 2), jnp.uint32),
            jax.ShapeDtypeStruct((t, ROUTER_LANES), jnp.float32),
            jax.ShapeDtypeStruct((EXPERTS_PER_GROUP, t), jnp.float32),
            jax.ShapeDtypeStruct((N_EXPERTS, LANES), jnp.float32),
        ],
        scratch_shapes=[pltpu.VMEM((N_EXPERTS, LANES), jnp.float32),
                        pltpu.VMEM((tm, tm), jnp.bfloat16)],
        compiler_params=_params(("arbitrary",)),
        name="outproj_router",
    )(a, w_out, x2d, gt, g.reshape(1, d), sc, sh, w_hi, jnp.concatenate([w_hi, w_lo], axis=0), bias)


def _sc_mesh():
    return plsc.VectorSubcoreMesh(core_axis_name="c", subcore_axis_name="s")


def _sc_dispatch(rows, d0, d1, pad_idx, n_slots):
    t, w = rows.shape
    n_pad = pad_idx.shape[1]
    zeros = jnp.zeros((SC_WINDOW, w), rows.dtype)
    sem = (pltpu.PARALLEL, pltpu.ARBITRARY)

    @pl.kernel(out_type=jax.ShapeDtypeStruct((n_slots + SC_WINDOW, w), rows.dtype), mesh=_sc_mesh(),
               scratch_types=[pltpu.SemaphoreType.DMA, pltpu.SemaphoreType.DMA])
    def dispatch(x_hbm, d0_hbm, d1_hbm, z_hbm, p_hbm, o_hbm, sem0, sem1):
        def scatter_rows(x_vmem, i0_vmem, i1_vmem):
            cols = pl.ds(pl.program_id(1) * SC_COLS, SC_COLS)
            first = pltpu.async_copy(x_vmem, o_hbm.at[i0_vmem.at[0], cols], sem0)
            second = pltpu.async_copy(x_vmem, o_hbm.at[i1_vmem.at[0], cols], sem1)
            first.wait()
            second.wait()

        pltpu.emit_pipeline(
            scatter_rows,
            grid=(t // SC_WINDOW, w // SC_COLS),
            in_specs=[pl.BlockSpec((SC_WINDOW, SC_COLS), lambda i, j: (i, j)),
                      pl.BlockSpec((1, SC_WINDOW), lambda i, j: (0, i)),
                      pl.BlockSpec((1, SC_WINDOW), lambda i, j: (0, i))],
            out_specs=[],
            core_axis_name=("c", "s"),
            dimension_semantics=sem,
        )(x_hbm, d0_hbm, d1_hbm)

        def scatter_zeros(z_vmem, p_vmem):
            cols = pl.ds(pl.program_id(1) * SC_COLS, SC_COLS)
            pltpu.sync_copy(z_vmem, o_hbm.at[p_vmem.at[0], cols])

        pltpu.emit_pipeline(
            scatter_zeros,
            grid=(n_pad // SC_WINDOW, w // SC_COLS),
            in_specs=[pl.BlockSpec((SC_WINDOW, SC_COLS), lambda i, j: (0, j)),
                      pl.BlockSpec((1, SC_WINDOW), lambda i, j: (0, i))],
            out_specs=[],
            core_axis_name=("c", "s"),
            dimension_semantics=sem,
        )(z_hbm, p_hbm)

    return dispatch(rows, d0, d1, zeros, pad_idx)


def _sc_gather(src, idx):
    n_out = idx.shape[1]
    w = src.shape[1]

    @pl.kernel(out_type=jax.ShapeDtypeStruct((n_out, w), src.dtype), mesh=_sc_mesh())
    def gather(x_hbm, i_hbm, o_hbm):
        def gather_rows(i_vmem, o_vmem):
            cols = pl.ds(pl.program_id(1) * SC_COLS, SC_COLS)
            pltpu.sync_copy(x_hbm.at[i_vmem.at[0], cols], o_vmem)

        pltpu.emit_pipeline(
            gather_rows,
            grid=(n_out // SC_WINDOW, w // SC_COLS),
            in_specs=[pl.BlockSpec((1, SC_WINDOW), lambda i, j: (0, i))],
            out_specs=[pl.BlockSpec((SC_WINDOW, SC_COLS), lambda i, j: (i, j))],
            core_axis_name=("c", "s"),
            dimension_semantics=(pltpu.PARALLEL, pltpu.ARBITRARY),
        )(i_hbm, o_hbm)

    return gather(src, idx)


def _expert_kernel(meta_ref, x_ref, w1_hbm, w3_hbm, w2_hbm, y_ref,
                   w1_buf, w3_buf, w2_buf, w1_c, w3_c, w2_c, sems, *, layer, nb):
    def weight_copies(expert, s):
        return (pltpu.make_async_copy(w1_hbm.at[layer, expert], w1_buf.at[s], sems.at[s, 0]),
                pltpu.make_async_copy(w3_hbm.at[layer, expert], w3_buf.at[s], sems.at[s, 1]),
                pltpu.make_async_copy(w2_hbm.at[layer, expert], w2_buf.at[s], sems.at[s, 2]))

    @pl.when(pl.program_id(0) == 0)
    def _():
        for cp in weight_copies(meta_ref[0], meta_ref[nb]):
            cp.start()

    for j in range(EXPERT_BLOCKS_PER_STEP):
        i = pl.program_id(0) * EXPERT_BLOCKS_PER_STEP + j
        rows = slice(j * MOE_BLOCK, (j + 1) * MOE_BLOCK)
        e = meta_ref[i]
        slot = meta_ref[nb + i]
        nxt = meta_ref[2 * nb + i]
        used = i < meta_ref[3 * nb]
        first = used & ((i == 0) | (e != meta_ref[jnp.maximum(i - 1, 0)]))

        @pl.when(first)
        def _():
            for cp in weight_copies(e, slot):
                cp.wait()

            @pl.when(nxt >= 0)
            def _():
                for cp in weight_copies(nxt, 1 - slot):
                    cp.start()

            w1_c[...] = w1_buf[slot].astype(jnp.bfloat16)
            w3_c[...] = w3_buf[slot].astype(jnp.bfloat16)
            w2_c[...] = w2_buf[slot].astype(jnp.bfloat16)

        @pl.when(used)
        def _():
            x_hi, x_lo = _unpack_bf16_pairs(x_ref[rows, :])
            xb = jnp.concatenate([x_hi, x_lo], axis=-1).astype(jnp.bfloat16)
            h1 = _bdot(xb, w1_c[...])
            h3 = _bdot(xb, w3_c[...])
            act = (h1 * jax.nn.sigmoid(h1) * h3).astype(jnp.bfloat16)
            y_ref[rows, :] = _pack_bf16_pairs(_bdot(act, w2_c[...]))

        @pl.when(jnp.logical_not(used))
        def _():
            y_ref[rows, :] = jnp.zeros((MOE_BLOCK, y_ref.shape[1]), y_ref.dtype)


def _experts(x_slots, blk_meta, w1, w3, w2, layer, nb):
    dp = x_slots.shape[1]
    d, de = w1.shape[-2:]
    step_rows = EXPERT_BLOCKS_PER_STEP * MOE_BLOCK
    assert nb % EXPERT_BLOCKS_PER_STEP == 0
    last_used_step = lambda s: (s[3 * nb] - 1) // EXPERT_BLOCKS_PER_STEP
    grid_spec = pltpu.PrefetchScalarGridSpec(
        num_scalar_prefetch=1,
        grid=(nb // EXPERT_BLOCKS_PER_STEP,),
        in_specs=[
            pl.BlockSpec((step_rows, dp), lambda i, s: (jnp.minimum(i, last_used_step(s)), 0)),
            pl.BlockSpec(memory_space=pl.ANY),
            pl.BlockSpec(memory_space=pl.ANY),
            pl.BlockSpec(memory_space=pl.ANY),
        ],
        out_specs=pl.BlockSpec((step_rows, dp), lambda i, s: (i, 0)),
        scratch_shapes=[
            pltpu.VMEM((2, d, de), jnp.float32),
            pltpu.VMEM((2, d, de), jnp.float32),
            pltpu.VMEM((2, de, d), jnp.float32),
            pltpu.VMEM((d, de), jnp.bfloat16),
            pltpu.VMEM((d, de), jnp.bfloat16),
            pltpu.VMEM((de, d), jnp.bfloat16),
            pltpu.SemaphoreType.DMA((2, 3)),
        ],
    )
    return pl.pallas_call(
        functools.partial(_expert_kernel, layer=layer, nb=nb),
        grid_spec=grid_spec,
        out_shape=jax.ShapeDtypeStruct((nb * MOE_BLOCK, dp), jnp.uint32),
        compiler_params=_params(("arbitrary",)),
        name="moe_experts",
    )(blk_meta, x_slots, w1, w3, w2)


def _combine_kernel(x_ref, y1_ref, y2_ref, info_ref, gt_ref, o_ref):
    o_ref[...] = _moe_combined(x_ref[...], y1_ref, y2_ref, info_ref, gt_ref)


def _combine(x2d, y_pairs, info, gt, seq):
    t, d = x2d.shape
    tm = min(1024, seq)
    tiles_per_batch = seq // tm
    second = t // tm
    return pl.pallas_call(
        _combine_kernel,
        grid=(t // tm,),
        in_specs=[
            pl.BlockSpec((tm, d), lambda i: (i, 0)),
            pl.BlockSpec((tm, d // 2), lambda i: (i, 0)),
            pl.BlockSpec((tm, d // 2), lambda i: (i + second, 0)),
            pl.BlockSpec((tm, ROUTER_LANES), lambda i: (i, 0)),
            pl.BlockSpec((None, 1, d), lambda i: (i // tiles_per_batch, 0, 0)),
        ],
        out_specs=pl.BlockSpec((tm, d), lambda i: (i, 0)),
        out_shape=jax.ShapeDtypeStruct((t, d), jnp.float32),
        compiler_params=_params(("arbitrary",)),
        name="moe_combine",
    )(x2d, y_pairs, y_pairs, info, gt)


def _slot_plan(info_t, cnt, t):
    counts = cnt[:, 0].astype(jnp.int32)
    padded = (counts + MOE_BLOCK - 1) // MOE_BLOCK * MOE_BLOCK
    pad_ends = jnp.cumsum(padded)
    pad_starts = pad_ends - padded
    nb = -(-(2 * t) // MOE_BLOCK) + N_EXPERTS
    n_slots = nb * MOE_BLOCK
    it = info_t.astype(jnp.int32)
    onehot_start = lambda e: jnp.sum(
        jnp.where(e[None, :] == jnp.arange(N_EXPERTS, dtype=jnp.int32)[:, None],
                  pad_starts[:, None], 0), axis=0)
    dest1 = (onehot_start(it[0]) + it[2]).reshape(1, t)
    dest2 = (onehot_start(it[1]) + it[3]).reshape(1, t)
    lane = jnp.arange(MOE_BLOCK, dtype=jnp.int32)[None, :]
    n_padding = (padded - counts)[:, None]
    wrapped = (pad_starts + counts)[:, None] + lane % jnp.maximum(n_padding, 1)
    pad_idx = jnp.where(n_padding > 0, wrapped, n_slots + lane % SC_WINDOW).reshape(-1)
    n_real = pad_ends[-1] // MOE_BLOCK
    n_used = -(-n_real // EXPERT_BLOCKS_PER_STEP) * EXPERT_BLOCKS_PER_STEP
    tail = jnp.arange((EXPERT_BLOCKS_PER_STEP - 1) * MOE_BLOCK, dtype=jnp.int32)
    tail_idx = jnp.where(tail < (n_used - n_real) * MOE_BLOCK, pad_ends[-1] + tail,
                         n_slots + tail % SC_WINDOW)
    pad_idx = jnp.concatenate([pad_idx, tail_idx]).reshape(1, -1)
    experts = jnp.arange(N_EXPERTS, dtype=jnp.int32)
    blk = jnp.arange(nb, dtype=jnp.int32)
    blk_exp = jnp.minimum(
        jnp.sum((pad_ends[None, :] <= (blk * MOE_BLOCK)[:, None]).astype(jnp.int32), axis=1),
        N_EXPERTS - 1)
    blk_exp = jnp.where(blk >= n_real, jnp.max(jnp.where(padded > 0, experts, 0)), blk_exp)
    prev_exp = jnp.concatenate([jnp.full((1,), -1, jnp.int32), blk_exp[:-1]])
    is_first = (blk < n_used) & (blk_exp != prev_exp)
    blk_slot = (jnp.cumsum(is_first.astype(jnp.int32)) + 1) % 2
    later = (experts[None, :] > experts[:, None]) & (padded[None, :] > 0)
    nxt_of = jnp.min(jnp.where(later, experts[None, :], N_EXPERTS), axis=1)
    nxt_of = jnp.where(nxt_of == N_EXPERTS, -1, nxt_of)
    blk_nxt = jnp.sum(jnp.where(blk_exp[:, None] == experts[None, :], nxt_of[None, :], 0), axis=1)
    blk_meta = jnp.concatenate([blk_exp, blk_slot, blk_nxt, n_used[None]]).astype(jnp.int32)
    return dest1, dest2, pad_idx, blk_meta, nb, n_slots


def _mixer_out_and_moe(a, w_out, x2d, gt1, g, sc, sh, w_group, b_group, w_router, b_router,
                       w1, w3, w2, layer, seq):
    t, d = x2d.shape
    w_cat = jnp.zeros((ROUTER_LANES, d), jnp.float32)
    w_cat = w_cat.at[:N_EXPERTS].set(w_router.T).at[N_EXPERTS:N_EXPERTS + N_GROUPS].set(w_group.T)
    b_cat = jnp.zeros((ROUTER_LANES, 1), jnp.float32)
    b_cat = b_cat.at[:N_EXPERTS, 0].set(b_router).at[N_EXPERTS:N_EXPERTS + N_GROUPS, 0].set(b_group)
    w_hi, w_lo = _split_hi_lo(w_cat)
    x_new, hn, info, info_t, cnt = _outproj_router(a, w_out, x2d, gt1, g, sc, sh, w_hi, w_lo,
                                                   b_cat, seq)
    dest1, dest2, pad_idx, blk_meta, nb, n_slots = _slot_plan(info_t, cnt, t)
    x_slots = _sc_dispatch(hn, dest1, dest2, pad_idx, n_slots)
    y_slots = _experts(x_slots, blk_meta, w1, w3, w2, layer, nb)
    y_pairs = _sc_gather(y_slots, jnp.concatenate([dest1, dest2], axis=1))
    return x_new, y_pairs, info


def kernel(x, c, w_ada, b_ada, norm1_g, norm2_g, ml_w_in, ml_b_gate, ml_g_out, ml_w_out,
           sw_w_in, sw_g_q, sw_g_k, sw_sinks, sw_w_out, moe_w_group, moe_b_group,
           moe_w_router, moe_b_router, moe_w1, moe_w3, moe_w2):
    bsz, seq, d = x.shape
    depth = w_ada.shape[0]
    bf = jnp.bfloat16
    mod = _ada_mod(c, w_ada, b_ada)
    x2d = x.reshape(bsz * seq, d)
    pending = None
    for layer in range(depth):
        sh1, sc1, gt1, sh2, sc2, gt2 = [
            mod[layer, :, i * d:(i + 1) * d].reshape(bsz, 1, d) for i in range(6)]
        j = layer // 2
        if layer % 2 == 0:
            w = ml_w_in[j]
            q_w, k_w = w[:, :ML_QK], w[:, ML_QK:2 * ML_QK]
            v_w = w[:, 2 * ML_QK:2 * ML_QK + ML_V]
            o_w = w[:, 2 * ML_QK + ML_V:2 * ML_QK + 2 * ML_V]
            g_w = w[:, 2 * ML_QK + 2 * ML_V:]
            w_main = jnp.concatenate([v_w, o_w, q_w], axis=1).astype(bf)
            wg_t = jnp.zeros((ML_GATE_ROWS, d), jnp.float32).at[:2 * ML_HEADS].set(g_w.T)
            wg_hi, wg_lo = _split_hi_lo(wg_t)
            wk_t = k_w.T.astype(bf)
            w_out = ml_w_out[j].astype(bf)
            outs = _inproj(x2d, norm1_g[layer], sc1, sh1, w_main, seq,
                           ml_extra=(wk_t, wg_hi, wg_lo),
                           q_cols=(2 * ML_V, 2 * ML_V + ML_QK), q_scale=ML_DK ** -0.5,
                           gate_cols=(ML_V, 2 * ML_V), pending=pending)
            if pending is not None:
                x2d, outs = outs[0], outs[1:]
            main, k_t, g_t = outs
            a = _mlstm_core(main, k_t, g_t, ml_b_gate[j], ml_g_out[j], bsz, seq)
        else:
            w = sw_w_in[j]
            dq = SW_Q_HEADS * SW_DH
            dkv = SW_KV_HEADS * SW_DH
            dup = lambda m: jnp.concatenate(
                [m.reshape(d, SW_KV_HEADS, 1, SW_DH)] * 2, axis=2).reshape(d, 2 * dkv)
            w_main = jnp.concatenate(
                [w[:, :dq], dup(w[:, dq:dq + dkv]), dup(w[:, dq + dkv:])], axis=1).astype(bf)
            w_out = sw_w_out[j].astype(bf)
            outs = _inproj(x2d, norm1_g[layer], sc1, sh1, w_main, seq, pending=pending)
            if pending is not None:
                x2d, outs = outs[0], outs[1:]
            a = _swa_core(outs[0], sw_sinks[j], sw_g_q[j], sw_g_k[j], bsz, seq)
        x2d, y_pairs, info = _mixer_out_and_moe(
            a, w_out, x2d, gt1, norm2_g[layer], sc2, sh2, moe_w_group[layer], moe_b_group[layer],
            moe_w_router[layer], moe_b_router[layer], moe_w1, moe_w3, moe_w2, layer, seq)
        pending = (y_pairs, info, gt2)
    y_pairs, info, gt2 = pending
    return _combine(x2d, y_pairs, info, gt2, seq).reshape(bsz, seq, d)
```

```python
import functools

import jax
import jax.numpy as jnp
from jax import lax
from jax.experimental import pallas as pl
from jax.experimental.pallas import tpu as pltpu
from jax.experimental.pallas import tpu_sc as plsc

EPS = 1e-6
GATE_CAP = 15.0
LOG2E = 1.4426950408889634

ML_HEADS = 4
ML_DK = 128
ML_DV = 256
ML_QK = ML_HEADS * ML_DK
ML_V = ML_HEADS * ML_DV
ML_GATE_ROWS = 16

SW_Q_HEADS = 16
SW_KV_HEADS = 4
SW_GROUP = SW_Q_HEADS // SW_KV_HEADS
SW_DH = 64
SW_WINDOW = 128
SW_Q_BLOCKS = 1
LANES = 128

N_GROUPS = 8
EXPERTS_PER_GROUP = 8
N_EXPERTS = N_GROUPS * EXPERTS_PER_GROUP
MOE_BLOCK = 256
ROUTER_LANES = 128
SC_WINDOW = 128
SC_COLS = 256
EXPERT_BLOCKS_PER_STEP = 4

VMEM_LIMIT = 56 * 1024 * 1024

_NT = (((1,), (1,)), ((), ()))


def _bdot(a, b):
    return jnp.dot(a, b, preferred_element_type=jnp.float32)


def _bdot_nt(a, b):
    return lax.dot_general(a, b, _NT, preferred_element_type=jnp.float32)


def _split_hi_lo(a):
    hi = a.astype(jnp.bfloat16)
    lo = (a - hi.astype(jnp.float32)).astype(jnp.bfloat16)
    return hi, lo


def _params(sem):
    return pltpu.CompilerParams(dimension_semantics=sem, vmem_limit_bytes=VMEM_LIMIT)


def _ada_kernel(c_ref, w_ref, b_ref, o_ref):
    c = c_ref[...]
    cond = c * jax.nn.sigmoid(c)
    c_hi, c_lo = _split_hi_lo(cond)
    w_hi, w_lo = _split_hi_lo(w_ref[...])
    acc = _bdot(c_hi, w_hi) + (_bdot(c_lo, w_hi) + _bdot(c_hi, w_lo))
    o_ref[...] = acc + b_ref[...]


def _ada_mod(c, w_ada, b_ada):
    depth, d, n = w_ada.shape
    bsz = c.shape[0]
    rows = 8
    tn = 768
    c_pad = jnp.zeros((rows, d), jnp.float32).at[:bsz].set(c)
    out = pl.pallas_call(
        _ada_kernel,
        grid=(depth, n // tn),
        in_specs=[
            pl.BlockSpec((rows, d), lambda l, j: (0, 0)),
            pl.BlockSpec((None, d, tn), lambda l, j: (l, 0, j)),
            pl.BlockSpec((None, 1, tn), lambda l, j: (l, 0, j)),
        ],
        out_specs=pl.BlockSpec((None, rows, tn), lambda l, j: (l, 0, j)),
        out_shape=jax.ShapeDtypeStruct((depth, rows, n), jnp.float32),
        compiler_params=_params(("arbitrary", "arbitrary")),
        name="ada_mod",
    )(c_pad, w_ada, b_ada.reshape(depth, 1, n))
    return out[:, :bsz]


def _modulated_norm(x, g, sc, sh):
    y = x * lax.rsqrt(jnp.mean(x * x, axis=-1, keepdims=True) + EPS)
    return y * (g * (1.0 + sc)) + sh


def _moe_combined(x, y1_ref, y2_ref, info_ref, gt_ref):
    info = info_ref[...]
    g1 = info[:, 4:5]
    g2 = info[:, 5:6]
    y1_hi, y1_lo = _unpack_bf16_pairs(y1_ref[...])
    y2_hi, y2_lo = _unpack_bf16_pairs(y2_ref[...])
    y = jnp.concatenate([g1 * y1_hi + g2 * y2_hi, g1 * y1_lo + g2 * y2_lo], axis=-1)
    return x + gt_ref[...] * y


def _inproj_kernel(*refs, n_main, chunk, q_cols, q_scale, gate_cols, with_ml, with_combine):
    refs = list(refs)
    n_in = 5 + (4 if with_combine else 0) + (2 if with_ml else 0)
    n_scratch = 4 if with_ml else 2
    ins, outs, scratch = refs[:n_in], refs[n_in:-n_scratch], refs[-n_scratch:]
    x_ref = ins.pop(0)
    if with_combine:
        y1_ref, y2_ref, info_ref, gtp_ref = ins[:4]
        ins = ins[4:]
        xo_ref = outs.pop(0)
    g_ref, sc_ref, sh_ref, w_ref = ins[:4]
    o_ref = outs[0]
    if with_ml:
        wk_ref, wgh_ref = ins[4:]
        kt_ref, gt_ref = outs[1:]

    def normalise(hb_dst, lo_dst):
        x = x_ref[...]
        if with_combine:
            x = _moe_combined(x, y1_ref, y2_ref, info_ref, gtp_ref)
            xo_ref[...] = x
        hn = _modulated_norm(x, g_ref[...], sc_ref[...], sh_ref[...])
        hb = hn.astype(jnp.bfloat16)
        hb_dst[...] = hb
        if with_ml:
            lo_dst[...] = (hn - hb.astype(jnp.float32)).astype(jnp.bfloat16)

    def project(hb_src, lo_src):
        hb = hb_src[...]
        for c0 in range(0, n_main, chunk):
            c1 = min(c0 + chunk, n_main)
            acc = _bdot(hb, w_ref[:, c0:c1])
            if q_cols is not None and q_cols[0] <= c0 < q_cols[1]:
                acc = acc * q_scale
            if gate_cols is not None and gate_cols[0] <= c0 < gate_cols[1]:
                acc = jax.nn.sigmoid(acc)
            o_ref[:, c0:c1] = acc.astype(o_ref.dtype)
        if with_ml:
            nk = kt_ref.shape[0]
            stacked = _bdot_nt(wk_ref[...], hb)
            kt_ref[...] = stacked[:nk].astype(kt_ref.dtype)
            gt_ref[...] = (stacked[nk:nk + ML_GATE_ROWS]
                           + (_bdot_nt(wgh_ref[...], lo_src[...]) + stacked[nk + ML_GATE_ROWS:]))

    hb_a, hb_b = scratch[:2]
    lo_a, lo_b = scratch[2:] if with_ml else (None, None)
    s = pl.program_id(0)

    @pl.when(s == 0)
    def _():
        hb_b[...] = jnp.zeros_like(hb_b)
        if with_ml:
            lo_b[...] = jnp.zeros_like(lo_b)

    @pl.when(s % 2 == 0)
    def _():
        normalise(hb_a, lo_a)
        project(hb_b, lo_b)

    @pl.when(s % 2 == 1)
    def _():
        normalise(hb_b, lo_b)
        project(hb_a, lo_a)


def _inproj(x2d, g, sc, sh, w_main, seq, *, ml_extra=None, q_cols=None, q_scale=1.0,
            gate_cols=None, pending=None):
    t, d = x2d.shape
    tm = 512
    n_main = w_main.shape[1]
    tiles_per_batch = seq // tm
    n_tiles = t // tm
    norm_tile = lambda s: jnp.minimum(s, n_tiles - 1)
    proj_tile = lambda s: jnp.maximum(s - 1, 0)
    row = lambda s: (norm_tile(s), 0)
    per_batch = lambda s: (norm_tile(s) // tiles_per_batch, 0, 0)
    const = lambda s: (0, 0)
    in_specs = [pl.BlockSpec((tm, d), row)]
    args = [x2d]
    out_specs, out_shape = [], []
    if pending is not None:
        y_pairs, info, gt_prev = pending
        in_specs += [pl.BlockSpec((tm, d // 2), row),
                     pl.BlockSpec((tm, d // 2), lambda s: (norm_tile(s) + n_tiles, 0)),
                     pl.BlockSpec((tm, ROUTER_LANES), row),
                     pl.BlockSpec((None, 1, d), per_batch)]
        args += [y_pairs, y_pairs, info, gt_prev]
        out_specs += [pl.BlockSpec((tm, d), row)]
        out_shape += [jax.ShapeDtypeStruct((t, d), jnp.float32)]
    in_specs += [
        pl.BlockSpec((1, d), const),
        pl.BlockSpec((None, 1, d), per_batch),
        pl.BlockSpec((None, 1, d), per_batch),
        pl.BlockSpec((d, n_main), const),
    ]
    args += [g.reshape(1, d), sc, sh, w_main]
    out_specs += [pl.BlockSpec((tm, n_main), lambda s: (proj_tile(s), 0))]
    out_shape += [jax.ShapeDtypeStruct((t, n_main), jnp.bfloat16)]
    scratch = [pltpu.VMEM((tm, d), jnp.bfloat16), pltpu.VMEM((tm, d), jnp.bfloat16)]
    if ml_extra is not None:
        wk_t, wg_hi, wg_lo = ml_extra
        nk = wk_t.shape[0]
        stacked = jnp.concatenate([wk_t, wg_hi, wg_lo], axis=0)
        in_specs += [pl.BlockSpec(stacked.shape, const),
                     pl.BlockSpec(wg_hi.shape, const)]
        args += [stacked, wg_hi]
        out_specs += [pl.BlockSpec((nk, tm), lambda s: (0, proj_tile(s))),
                      pl.BlockSpec((ML_GATE_ROWS, tm), lambda s: (0, proj_tile(s)))]
        out_shape += [jax.ShapeDtypeStruct((nk, t), jnp.bfloat16),
                      jax.ShapeDtypeStruct((ML_GATE_ROWS, t), jnp.float32)]
        scratch += [pltpu.VMEM((tm, d), jnp.bfloat16), pltpu.VMEM((tm, d), jnp.bfloat16)]
    kern = functools.partial(_inproj_kernel, n_main=n_main, chunk=1024, q_cols=q_cols,
                             q_scale=q_scale, gate_cols=gate_cols, with_ml=ml_extra is not None,
                             with_combine=pending is not None)
    return pl.pallas_call(
        kern,
        grid=(n_tiles + 1,),
        in_specs=in_specs,
        out_specs=out_specs,
        out_shape=out_shape,
        scratch_shapes=scratch,
        compiler_params=_params(("arbitrary",)),
        name="inproj_ml" if ml_extra is not None else "inproj_sw",
    )(*args)


def _mlstm_gate_terms(graw, bias, upper):
    H = ML_HEADS
    L = graw.shape[1]
    z = graw + bias
    gates = GATE_CAP * jnp.tanh(z / GATE_CAP)
    log_f = jnp.minimum(gates, 0.0) - jnp.log1p(jnp.exp(-jnp.abs(gates)))
    row = lax.broadcasted_iota(jnp.int32, (ML_GATE_ROWS, L), 0)
    lane = lax.broadcasted_iota(jnp.int32, (ML_GATE_ROWS, L), 1)
    is_i = row < H
    slab = jnp.where(is_i, gates, log_f)
    a1 = slab.astype(jnp.bfloat16)
    r1 = slab - a1.astype(jnp.float32)
    a2 = r1.astype(jnp.bfloat16)
    a3 = (r1 - a2.astype(jnp.float32)).astype(jnp.bfloat16)
    cum = _bdot(a1, upper) + (_bdot(a2, upper) + _bdot(a3, upper))
    ib = jnp.where(is_i, gates, cum)
    b = pltpu.roll(ib, ML_GATE_ROWS - H, 0)
    u = ib - b
    cm = u
    shift = 1
    while shift < L:
        cm = jnp.maximum(cm, jnp.where(lane >= shift, pltpu.roll(cm, shift, 1), -jnp.inf))
        shift *= 2
    return b, u, cm


def _mlstm_kernel(v_ref, o_ref, q_ref, kt_ref, gt_ref, gtn_ref, bg_ref, gout_ref, out_ref,
                  c_ref, m_ref, b_ref, u_ref, cm_ref, *, chunk):
    L = chunk
    H, dk, dv = ML_HEADS, ML_DK, ML_DV
    r_idx = lax.broadcasted_iota(jnp.int32, (L, L), 0)
    c_idx = lax.broadcasted_iota(jnp.int32, (L, L), 1)
    upper = jnp.where(r_idx <= c_idx, 1.0, 0.0).astype(jnp.bfloat16)
    causal = r_idx >= c_idx
    row = lax.broadcasted_iota(jnp.int32, (ML_GATE_ROWS, L), 0)
    ones_col = jnp.where(lax.broadcasted_iota(jnp.int32, (L, LANES), 1) == 0, 1.0, 0.0
                         ).astype(jnp.bfloat16)

    @pl.when(pl.program_id(1) == 0)
    def _():
        c_ref[...] = jnp.zeros_like(c_ref)
        m_ref[...] = jnp.zeros_like(m_ref)
        b0, u0, cm0 = _mlstm_gate_terms(gt_ref[...], bg_ref[...], upper)
        b_ref[...] = b0
        u_ref[...] = u0
        cm_ref[...] = cm0

    b16 = b_ref[...]
    u16 = u_ref[...]
    cm16 = cm_ref[...]
    b_n, u_n, cm_n = _mlstm_gate_terms(gtn_ref[...], bg_ref[...], upper)
    b_ref[...] = b_n
    u_ref[...] = u_n
    cm_ref[...] = cm_n

    m_prev = m_ref[:, 0:1]
    z16 = jnp.maximum(m_prev, cm16)
    w_inter16 = jnp.exp(m_prev - z16)
    e_negm16 = jnp.exp(-(b16 + z16))
    z_last = z16[:, L - 1:L]
    w_state16 = jnp.exp(u16 - z_last)
    decay16 = jnp.exp(m_prev - z_last)
    m_ref[...] = jnp.broadcast_to(b16[:, L - 1:L] + z_last, m_ref.shape)
    stacked = jnp.where(row < H, z16,
                        jnp.where(row < 2 * H, pltpu.roll(w_inter16, H, 0),
                                  pltpu.roll(e_negm16, 2 * H, 0)))
    cols = jnp.concatenate(
        [stacked, jnp.zeros((LANES - ML_GATE_ROWS, L), jnp.float32)], axis=0).T

    for h in range(H):
        u_r = u16[h:h + 1, :]
        z_c = cols[:, h:h + 1]
        w_inter = cols[:, H + h:H + h + 1]
        e_negm = cols[:, 2 * H + h:2 * H + h + 1]
        c_ext = c_ref[h]
        q = q_ref[:, h * dk:(h + 1) * dk]
        kt = kt_ref[h * dk:(h + 1) * dk, :]
        v_ext = jnp.concatenate([v_ref[:, h * dv:(h + 1) * dv], ones_col], axis=-1)

        w_intra = jnp.exp(jnp.where(causal, u_r - z_c, -jnp.inf))
        s = (_bdot(q, kt) * w_intra).astype(jnp.bfloat16)
        nd = w_inter * _bdot(q, c_ext.astype(jnp.bfloat16)) + _bdot(s, v_ext)
        den = nd[:, dv:dv + 1]
        hb = nd[:, :dv] * (1.0 / jnp.maximum(jnp.abs(den), e_negm))

        kw = (kt.astype(jnp.float32) * w_state16[h:h + 1, :]).astype(jnp.bfloat16)
        c_ref[h] = decay16[h:h + 1, :] * c_ext + _bdot(kw, v_ext)

        y = hb * lax.rsqrt(jnp.mean(hb * hb, axis=-1, keepdims=True) + EPS)
        y = y * gout_ref[:, h * dv:(h + 1) * dv]
        og = o_ref[:, h * dv:(h + 1) * dv].astype(jnp.float32)
        out_ref[:, h * dv:(h + 1) * dv] = (y * og).astype(out_ref.dtype)


def _mlstm_core(main, k_t, g_t, b_gate, g_out, bsz, seq, chunk=256):
    t = main.shape[0]
    nc = seq // chunk
    blk = lambda b, c: b * nc + c
    bg = jnp.zeros((ML_GATE_ROWS, 1), jnp.float32).at[:2 * ML_HEADS, 0].set(b_gate)
    return pl.pallas_call(
        functools.partial(_mlstm_kernel, chunk=chunk),
        grid=(bsz, nc),
        in_specs=[
            pl.BlockSpec((chunk, ML_V), lambda b, c: (blk(b, c), 0)),
            pl.BlockSpec((chunk, ML_V), lambda b, c: (blk(b, c), 1)),
            pl.BlockSpec((chunk, ML_QK), lambda b, c: (blk(b, c), 4)),
            pl.BlockSpec((ML_QK, chunk), lambda b, c: (0, blk(b, c))),
            pl.BlockSpec((ML_GATE_ROWS, chunk), lambda b, c: (0, blk(b, c))),
            pl.BlockSpec((ML_GATE_ROWS, chunk), lambda b, c: (0, blk(b, jnp.minimum(c + 1, nc - 1)))),
            pl.BlockSpec((ML_GATE_ROWS, 1), lambda b, c: (0, 0)),
            pl.BlockSpec((1, ML_V), lambda b, c: (0, 0)),
        ],
        out_specs=pl.BlockSpec((chunk, ML_V), lambda b, c: (blk(b, c), 0)),
        out_shape=jax.ShapeDtypeStruct((t, ML_V), jnp.bfloat16),
        scratch_shapes=[
            pltpu.VMEM((ML_HEADS, ML_DK, ML_DV + LANES), jnp.float32),
            pltpu.VMEM((ML_GATE_ROWS, LANES), jnp.float32),
            pltpu.VMEM((ML_GATE_ROWS, chunk), jnp.float32),
            pltpu.VMEM((ML_GATE_ROWS, chunk), jnp.float32),
            pltpu.VMEM((ML_GATE_ROWS, chunk), jnp.float32),
        ],
        compiler_params=_params(("arbitrary", "arbitrary")),
        name="mlstm_core",
    )(main, main, main, k_t, g_t, g_t, bg, g_out.reshape(1, ML_V))


def _swa_kernel(sinks_ref, q_ref, kc_ref, kp_ref, vc_ref, vp_ref, gq_ref, gk_ref, bias_ref, o_ref):
    W = SW_WINDOW
    lane = lax.broadcasted_iota(jnp.int32, (W, LANES), 1)
    low = lane < SW_DH
    row_col = lax.broadcasted_iota(jnp.int32, (SW_GROUP * W, 1), 0)
    gq = gq_ref[...]
    gk = gk_ref[...]
    first_variant = jnp.minimum(pl.program_id(1), 1)
    for g in range(SW_KV_HEADS):
        sl = slice(g * LANES, (g + 1) * LANES)
        k_all = jnp.concatenate([kp_ref[:, sl], kc_ref[:, sl]], axis=0).astype(jnp.float32)
        kn_all = k_all * lax.rsqrt(jnp.mean(k_all * k_all, axis=-1, keepdims=True) + EPS)
        kn_all = (kn_all * gk).astype(jnp.bfloat16)
        v_all = jnp.concatenate([vp_ref[:, sl], vc_ref[:, sl]], axis=0)
        sink = jnp.full((SW_GROUP * W, 1), sinks_ref[g * SW_GROUP + SW_GROUP - 1], jnp.float32)
        for j in range(SW_GROUP - 2, -1, -1):
            sink = jnp.where(row_col < (j + 1) * W, sinks_ref[g * SW_GROUP + j], sink)
        for blk in range(SW_Q_BLOCKS):
            rows = slice(blk * W, (blk + 1) * W)
            kn = kn_all[blk * W:(blk + 2) * W]
            v2 = v_all[blk * W:(blk + 2) * W]
            bias = bias_ref[first_variant] if blk == 0 else bias_ref[1]
            parts = []
            for p in range(2):
                c0 = g * SW_GROUP * SW_DH + p * LANES
                qp = q_ref[rows, c0:c0 + LANES].astype(jnp.float32)
                sq = qp * qp
                ss_lo = jnp.sum(jnp.where(low, sq, 0.0), axis=-1, keepdims=True)
                ss_hi = jnp.sum(jnp.where(low, 0.0, sq), axis=-1, keepdims=True)
                rs = jnp.where(low, lax.rsqrt(ss_lo / SW_DH + EPS), lax.rsqrt(ss_hi / SW_DH + EPS))
                qn = qp * rs * gq
                parts.append(jnp.where(low, qn, 0.0).astype(jnp.bfloat16))
                parts.append(jnp.where(low, 0.0, qn).astype(jnp.bfloat16))
            q4 = jnp.concatenate(parts, axis=0)
            scores = _bdot_nt(q4, kn) + bias
            m = jnp.maximum(jnp.max(scores, axis=-1, keepdims=True), sink)
            pexp = jnp.exp2(scores - m)
            denom = jnp.sum(pexp, axis=-1, keepdims=True) + jnp.exp2(sink - m)
            o4 = _bdot(pexp.astype(jnp.bfloat16), v2) * (1.0 / denom)
            for p in range(2):
                oa = o4[(2 * p) * W:(2 * p + 1) * W]
                ob = o4[(2 * p + 1) * W:(2 * p + 2) * W]
                c0 = g * SW_GROUP * SW_DH + p * LANES
                o_ref[rows, c0:c0 + LANES] = jnp.where(low, oa, ob).astype(o_ref.dtype)


def _swa_core(proj, sinks, g_q, g_k, bsz, seq):
    t = proj.shape[0]
    W = SW_WINDOW
    nb = seq // W
    dq = SW_Q_HEADS * SW_DH
    kv_w = SW_KV_HEADS * LANES
    k_blk = dq // kv_w
    v_blk = k_blk + 1
    QB = SW_Q_BLOCKS
    assert nb % QB == 0
    steps = nb // QB
    cur = lambda b, n, s: b * steps + n
    prev = lambda b, n, s: b * nb + jnp.maximum(QB * n - 1, 0)
    gq2 = jnp.concatenate([g_q, g_q]).reshape(1, LANES) * (SW_DH ** -0.5 * LOG2E)
    gk2 = jnp.concatenate([g_k, g_k]).reshape(1, LANES)
    sinks = sinks.astype(jnp.float32) * LOG2E
    qi = (jnp.arange(SW_GROUP * W) % W)[:, None]
    ki = jnp.arange(2 * W)[None, :]
    rel = qi + W - ki
    in_win = (rel >= 0) & (rel < W)
    bias = jnp.stack([jnp.where(in_win & (ki >= W), 0.0, -jnp.inf),
                      jnp.where(in_win, 0.0, -jnp.inf)]).astype(jnp.float32)
    grid_spec = pltpu.PrefetchScalarGridSpec(
        num_scalar_prefetch=1,
        grid=(bsz, steps),
        in_specs=[
            pl.BlockSpec((QB * W, dq), lambda b, n, s: (cur(b, n, s), 0)),
            pl.BlockSpec((QB * W, kv_w), lambda b, n, s: (cur(b, n, s), k_blk)),
            pl.BlockSpec((W, kv_w), lambda b, n, s: (prev(b, n, s), k_blk)),
            pl.BlockSpec((QB * W, kv_w), lambda b, n, s: (cur(b, n, s), v_blk)),
            pl.BlockSpec((W, kv_w), lambda b, n, s: (prev(b, n, s), v_blk)),
            pl.BlockSpec((1, LANES), lambda b, n, s: (0, 0)),
            pl.BlockSpec((1, LANES), lambda b, n, s: (0, 0)),
            pl.BlockSpec((2, SW_GROUP * W, 2 * W), lambda b, n, s: (0, 0, 0)),
        ],
        out_specs=pl.BlockSpec((QB * W, dq), lambda b, n, s: (cur(b, n, s), 0)),
    )
    return pl.pallas_call(
        _swa_kernel,
        grid_spec=grid_spec,
        out_shape=jax.ShapeDtypeStruct((t, dq), jnp.bfloat16),
        compiler_params=_params(("arbitrary", "arbitrary")),
        name="swa_core",
    )(sinks.astype(jnp.float32), proj, proj, proj, proj, proj, gq2, gk2, bias)


def _pack_rounded_pairs(r):
    k = r.shape[1] // 2
    hi = lax.bitcast_convert_type(r[:, :k], jnp.uint32)
    lo = lax.bitcast_convert_type(r[:, k:], jnp.uint32)
    return hi | (lo >> 16)


def _pack_bf16_pairs(a):
    return _pack_rounded_pairs(a.astype(jnp.bfloat16).astype(jnp.float32))


def _unpack_bf16_pairs(u):
    hi = lax.bitcast_convert_type(u & jnp.uint32(0xFFFF0000), jnp.float32)
    lo = lax.bitcast_convert_type(u << 16, jnp.float32)
    return hi, lo


def _router_kernel(a_ref, wo_ref, x_ref, gt_ref, g_ref, sc_ref, sh_ref, wh_ref, wl_ref, b_ref,
                   xo_ref, hn_ref, info_ref, infot_ref, cnt_ref, carry_ref, earlier_ref, *, tm):
    @pl.when(pl.program_id(0) == 0)
    def _():
        carry_ref[...] = jnp.zeros_like(carry_ref)
        r_idx = lax.broadcasted_iota(jnp.int32, (tm, tm), 0)
        c_idx = lax.broadcasted_iota(jnp.int32, (tm, tm), 1)
        earlier_ref[...] = jnp.where(r_idx < c_idx, 1.0, 0.0).astype(jnp.bfloat16)

    x_new = x_ref[...] + gt_ref[...] * _bdot(a_ref[...], wo_ref[...])
    xo_ref[...] = x_new
    hn = _modulated_norm(x_new, g_ref[...], sc_ref[...], sh_ref[...])
    h_hi = hn.astype(jnp.bfloat16)
    hi_f32 = h_hi.astype(jnp.float32)
    h_lo = (hn - hi_f32).astype(jnp.bfloat16)
    hn_ref[...] = _pack_rounded_pairs(hi_f32)
    wide = _bdot_nt(wl_ref[...], h_hi)
    logits = (wide[:ROUTER_LANES] + (_bdot_nt(wh_ref[...], h_lo) + wide[ROUTER_LANES:])
              + b_ref[...])
    E8 = EXPERTS_PER_GROUP
    sub = lax.broadcasted_iota(jnp.int32, (E8, tm), 0).astype(jnp.float32)
    big = float(ROUTER_LANES)
    neg = -jnp.inf

    gl = logits[N_EXPERTS:N_EXPERTS + N_GROUPS]
    gmax = jnp.max(gl, axis=0, keepdims=True)
    gsel = jnp.min(jnp.where(gl == gmax, sub, big), axis=0, keepdims=True)
    p_grp = 1.0 / jnp.sum(jnp.exp(gl - gmax), axis=0, keepdims=True)

    el = logits[0:E8]
    for grp in range(1, N_GROUPS):
        el = jnp.where(gsel == grp, logits[grp * E8:(grp + 1) * E8], el)
    v1 = jnp.max(el, axis=0, keepdims=True)
    j1 = jnp.min(jnp.where(el == v1, sub, big), axis=0, keepdims=True)
    el2 = jnp.where(sub == j1, neg, el)
    v2 = jnp.max(el2, axis=0, keepdims=True)
    j2 = jnp.min(jnp.where(el2 == v2, sub, big), axis=0, keepdims=True)
    i1 = gsel * E8 + j1
    i2 = gsel * E8 + j2
    e21 = jnp.exp(v2 - v1)
    gate1 = p_grp / (1.0 + e21)
    gate2 = p_grp * e21 / (1.0 + e21)

    erow = lax.broadcasted_iota(jnp.int32, (N_EXPERTS, tm), 0).astype(jnp.float32)
    hit1 = erow == i1
    hit2 = erow == i2
    onehot = jnp.where(hit1 | hit2, 1.0, 0.0)
    carry = carry_ref[:, 0:1]
    before = _bdot(onehot.astype(jnp.bfloat16), earlier_ref[...]) + carry
    rank1 = jnp.sum(jnp.where(hit1, before, 0.0), axis=0, keepdims=True)
    rank2 = jnp.sum(jnp.where(hit2, before, 0.0), axis=0, keepdims=True)
    total = carry + jnp.sum(onehot, axis=1, keepdims=True)
    carry_ref[...] = jnp.broadcast_to(total, carry_ref.shape)
    cnt_ref[...] = jnp.broadcast_to(total, cnt_ref.shape)

    info_t = jnp.where(sub == 0, i1, 0.0)
    info_t = jnp.where(sub == 1, i2, info_t)
    info_t = jnp.where(sub == 2, rank1, info_t)
    info_t = jnp.where(sub == 3, rank2, info_t)
    info_t = jnp.where(sub == 4, gate1, info_t)
    info_t = jnp.where(sub == 5, gate2, info_t)
    infot_ref[...] = info_t
    info_ref[...] = jnp.concatenate(
        [info_t, jnp.zeros((ROUTER_LANES - E8, tm), jnp.float32)], axis=0).T


def _outproj_router(a, w_out, x2d, gt, g, sc, sh, w_hi, w_lo, bias, seq):
    t, d = x2d.shape
    tm = 512
    tiles_per_batch = seq // tm
    per_batch = lambda i: (i // tiles_per_batch, 0, 0)
    const = lambda i: (0, 0)
    return pl.pallas_call(
        functools.partial(_router_kernel, tm=tm),
        grid=(t // tm,),
        in_specs=[
            pl.BlockSpec((tm, a.shape[1]), lambda i: (i, 0)),
            pl.BlockSpec(w_out.shape, const),
            pl.BlockSpec((tm, d), lambda i: (i, 0)),
            pl.BlockSpec((None, 1, d), per_batch),
            pl.BlockSpec((1, d), const),
            pl.BlockSpec((None, 1, d), per_batch),
            pl.BlockSpec((None, 1, d), per_batch),
            pl.BlockSpec((ROUTER_LANES, d), const),
            pl.BlockSpec((2 * ROUTER_LANES, d), const),
            pl.BlockSpec((ROUTER_LANES, 1), const),
        ],
        out_specs=[
            pl.BlockSpec((tm, d), lambda i: (i, 0)),
            pl.BlockSpec((tm, d // 2), lambda i: (i, 0)),
            pl.BlockSpec((tm, ROUTER_LANES), lambda i: (i, 0)),
            pl.BlockSpec((EXPERTS_PER_GROUP, tm), lambda i: (0, i)),
            pl.BlockSpec((N_EXPERTS, LANES), const),
        ],
        out_shape=[
            jax.ShapeDtypeStruct((t, d), jnp.float32),
            jax.ShapeDtypeStruct((t, d // 2), jnp.uint32),
            jax.ShapeDtypeStruct((t, ROUTER_LANES), jnp.float32),
            jax.ShapeDtypeStruct((EXPERTS_PER_GROUP, t), jnp.float32),
            jax.ShapeDtypeStruct((N_EXPERTS, LANES), jnp.float32),
        ],
        scratch_shapes=[pltpu.VMEM((N_EXPERTS, LANES), jnp.float32),
                        pltpu.VMEM((tm, tm), jnp.bfloat16)],
        compiler_params=_params(("arbitrary",)),
        name="outproj_router",
    )(a, w_out, x2d, gt, g.reshape(1, d), sc, sh, w_hi, jnp.concatenate([w_hi, w_lo], axis=0), bias)


def _sc_mesh():
    return plsc.VectorSubcoreMesh(core_axis_name="c", subcore_axis_name="s")


def _sc_dispatch(rows, d0, d1, pad_idx, n_slots):
    t, w = rows.shape
    n_pad = pad_idx.shape[1]
    zeros = jnp.zeros((SC_WINDOW, w), rows.dtype)
    sem = (pltpu.PARALLEL, pltpu.ARBITRARY)

    @pl.kernel(out_type=jax.ShapeDtypeStruct((n_slots + SC_WINDOW, w), rows.dtype), mesh=_sc_mesh(),
               scratch_types=[pltpu.SemaphoreType.DMA, pltpu.SemaphoreType.DMA])
    def dispatch(x_hbm, d0_hbm, d1_hbm, z_hbm, p_hbm, o_hbm, sem0, sem1):
        def scatter_rows(x_vmem, i0_vmem, i1_vmem):
            cols = pl.ds(pl.program_id(1) * SC_COLS, SC_COLS)
            first = pltpu.async_copy(x_vmem, o_hbm.at[i0_vmem.at[0], cols], sem0)
            second = pltpu.async_copy(x_vmem, o_hbm.at[i1_vmem.at[0], cols], sem1)
            first.wait()
            second.wait()

        pltpu.emit_pipeline(
            scatter_rows,
            grid=(t // SC_WINDOW, w // SC_COLS),
            in_specs=[pl.BlockSpec((SC_WINDOW, SC_COLS), lambda i, j: (i, j)),
                      pl.BlockSpec((1, SC_WINDOW), lambda i, j: (0, i)),
                      pl.BlockSpec((1, SC_WINDOW), lambda i, j: (0, i))],
            out_specs=[],
            core_axis_name=("c", "s"),
            dimension_semantics=sem,
        )(x_hbm, d0_hbm, d1_hbm)

        def scatter_zeros(z_vmem, p_vmem):
            cols = pl.ds(pl.program_id(1) * SC_COLS, SC_COLS)
            pltpu.sync_copy(z_vmem, o_hbm.at[p_vmem.at[0], cols])

        pltpu.emit_pipeline(
            scatter_zeros,
            grid=(n_pad // SC_WINDOW, w // SC_COLS),
            in_specs=[pl.BlockSpec((SC_WINDOW, SC_COLS), lambda i, j: (0, j)),
                      pl.BlockSpec((1, SC_WINDOW), lambda i, j: (0, i))],
            out_specs=[],
            core_axis_name=("c", "s"),
            dimension_semantics=sem,
        )(z_hbm, p_hbm)

    return dispatch(rows, d0, d1, zeros, pad_idx)


def _sc_gather(src, idx):
    n_out = idx.shape[1]
    w = src.shape[1]

    @pl.kernel(out_type=jax.ShapeDtypeStruct((n_out, w), src.dtype), mesh=_sc_mesh())
    def gather(x_hbm, i_hbm, o_hbm):
        def gather_rows(i_vmem, o_vmem):
            cols = pl.ds(pl.program_id(1) * SC_COLS, SC_COLS)
            pltpu.sync_copy(x_hbm.at[i_vmem.at[0], cols], o_vmem)

        pltpu.emit_pipeline(
            gather_rows,
            grid=(n_out // SC_WINDOW, w // SC_COLS),
            in_specs=[pl.BlockSpec((1, SC_WINDOW), lambda i, j: (0, i))],
            out_specs=[pl.BlockSpec((SC_WINDOW, SC_COLS), lambda i, j: (i, j))],
            core_axis_name=("c", "s"),
            dimension_semantics=(pltpu.PARALLEL, pltpu.ARBITRARY),
        )(i_hbm, o_hbm)

    return gather(src, idx)


def _expert_kernel(meta_ref, x_ref, w1_hbm, w3_hbm, w2_hbm, y_ref,
                   w1_buf, w3_buf, w2_buf, w1_c, w3_c, w2_c, sems, *, layer, nb):
    def weight_copies(expert, s):
        return (pltpu.make_async_copy(w1_hbm.at[layer, expert], w1_buf.at[s], sems.at[s, 0]),
                pltpu.make_async_copy(w3_hbm.at[layer, expert], w3_buf.at[s], sems.at[s, 1]),
                pltpu.make_async_copy(w2_hbm.at[layer, expert], w2_buf.at[s], sems.at[s, 2]))

    @pl.when(pl.program_id(0) == 0)
    def _():
        for cp in weight_copies(meta_ref[0], meta_ref[nb]):
            cp.start()

    for j in range(EXPERT_BLOCKS_PER_STEP):
        i = pl.program_id(0) * EXPERT_BLOCKS_PER_STEP + j
        rows = slice(j * MOE_BLOCK, (j + 1) * MOE_BLOCK)
        e = meta_ref[i]
        slot = meta_ref[nb + i]
        nxt = meta_ref[2 * nb + i]
        used = i < meta_ref[3 * nb]
        first = used & ((i == 0) | (e != meta_ref[jnp.maximum(i - 1, 0)]))

        @pl.when(first)
        def _():
            for cp in weight_copies(e, slot):
                cp.wait()

            @pl.when(nxt >= 0)
            def _():
                for cp in weight_copies(nxt, 1 - slot):
                    cp.start()

            w1_c[...] = w1_buf[slot].astype(jnp.bfloat16)
            w3_c[...] = w3_buf[slot].astype(jnp.bfloat16)
            w2_c[...] = w2_buf[slot].astype(jnp.bfloat16)

        @pl.when(used)
        def _():
            x_hi, x_lo = _unpack_bf16_pairs(x_ref[rows, :])
            xb = jnp.concatenate([x_hi, x_lo], axis=-1).astype(jnp.bfloat16)
            h1 = _bdot(xb, w1_c[...])
            h3 = _bdot(xb, w3_c[...])
            act = (h1 * jax.nn.sigmoid(h1) * h3).astype(jnp.bfloat16)
            y_ref[rows, :] = _pack_bf16_pairs(_bdot(act, w2_c[...]))

        @pl.when(jnp.logical_not(used))
        def _():
            y_ref[rows, :] = jnp.zeros((MOE_BLOCK, y_ref.shape[1]), y_ref.dtype)


def _experts(x_slots, blk_meta, w1, w3, w2, layer, nb):
    dp = x_slots.shape[1]
    d, de = w1.shape[-2:]
    step_rows = EXPERT_BLOCKS_PER_STEP * MOE_BLOCK
    assert nb % EXPERT_BLOCKS_PER_STEP == 0
    last_used_step = lambda s: (s[3 * nb] - 1) // EXPERT_BLOCKS_PER_STEP
    grid_spec = pltpu.PrefetchScalarGridSpec(
        num_scalar_prefetch=1,
        grid=(nb // EXPERT_BLOCKS_PER_STEP,),
        in_specs=[
            pl.BlockSpec((step_rows, dp), lambda i, s: (jnp.minimum(i, last_used_step(s)), 0)),
            pl.BlockSpec(memory_space=pl.ANY),
            pl.BlockSpec(memory_space=pl.ANY),
            pl.BlockSpec(memory_space=pl.ANY),
        ],
        out_specs=pl.BlockSpec((step_rows, dp), lambda i, s: (i, 0)),
        scratch_shapes=[
            pltpu.VMEM((2, d, de), jnp.float32),
            pltpu.VMEM((2, d, de), jnp.float32),
            pltpu.VMEM((2, de, d), jnp.float32),
            pltpu.VMEM((d, de), jnp.bfloat16),
            pltpu.VMEM((d, de), jnp.bfloat16),
            pltpu.VMEM((de, d), jnp.bfloat16),
            pltpu.SemaphoreType.DMA((2, 3)),
        ],
    )
    return pl.pallas_call(
        functools.partial(_expert_kernel, layer=layer, nb=nb),
        grid_spec=grid_spec,
        out_shape=jax.ShapeDtypeStruct((nb * MOE_BLOCK, dp), jnp.uint32),
        compiler_params=_params(("arbitrary",)),
        name="moe_experts",
    )(blk_meta, x_slots, w1, w3, w2)


def _combine_kernel(x_ref, y1_ref, y2_ref, info_ref, gt_ref, o_ref):
    o_ref[...] = _moe_combined(x_ref[...], y1_ref, y2_ref, info_ref, gt_ref)


def _combine(x2d, y_pairs, info, gt, seq):
    t, d = x2d.shape
    tm = min(1024, seq)
    tiles_per_batch = seq // tm
    second = t // tm
    return pl.pallas_call(
        _combine_kernel,
        grid=(t // tm,),
        in_specs=[
            pl.BlockSpec((tm, d), lambda i: (i, 0)),
            pl.BlockSpec((tm, d // 2), lambda i: (i, 0)),
            pl.BlockSpec((tm, d // 2), lambda i: (i + second, 0)),
            pl.BlockSpec((tm, ROUTER_LANES), lambda i: (i, 0)),
            pl.BlockSpec((None, 1, d), lambda i: (i // tiles_per_batch, 0, 0)),
        ],
        out_specs=pl.BlockSpec((tm, d), lambda i: (i, 0)),
        out_shape=jax.ShapeDtypeStruct((t, d), jnp.float32),
        compiler_params=_params(("arbitrary",)),
        name="moe_combine",
    )(x2d, y_pairs, y_pairs, info, gt)


def _slot_plan(info_t, cnt, t):
    counts = cnt[:, 0].astype(jnp.int32)
    padded = (counts + MOE_BLOCK - 1) // MOE_BLOCK * MOE_BLOCK
    pad_ends = jnp.cumsum(padded)
    pad_starts = pad_ends - padded
    nb = -(-(2 * t) // MOE_BLOCK) + N_EXPERTS
    n_slots = nb * MOE_BLOCK
    it = info_t.astype(jnp.int32)
    onehot_start = lambda e: jnp.sum(
        jnp.where(e[None, :] == jnp.arange(N_EXPERTS, dtype=jnp.int32)[:, None],
                  pad_starts[:, None], 0), axis=0)
    dest1 = (onehot_start(it[0]) + it[2]).reshape(1, t)
    dest2 = (onehot_start(it[1]) + it[3]).reshape(1, t)
    lane = jnp.arange(MOE_BLOCK, dtype=jnp.int32)[None, :]
    n_padding = (padded - counts)[:, None]
    wrapped = (pad_starts + counts)[:, None] + lane % jnp.maximum(n_padding, 1)
    pad_idx = jnp.where(n_padding > 0, wrapped, n_slots + lane % SC_WINDOW).reshape(-1)
    n_real = pad_ends[-1] // MOE_BLOCK
    n_used = -(-n_real // EXPERT_BLOCKS_PER_STEP) * EXPERT_BLOCKS_PER_STEP
    tail = jnp.arange((EXPERT_BLOCKS_PER_STEP - 1) * MOE_BLOCK, dtype=jnp.int32)
    tail_idx = jnp.where(tail < (n_used - n_real) * MOE_BLOCK, pad_ends[-1] + tail,
                         n_slots + tail % SC_WINDOW)
    pad_idx = jnp.concatenate([pad_idx, tail_idx]).reshape(1, -1)
    experts = jnp.arange(N_EXPERTS, dtype=jnp.int32)
    blk = jnp.arange(nb, dtype=jnp.int32)
    blk_exp = jnp.minimum(
        jnp.sum((pad_ends[None, :] <= (blk * MOE_BLOCK)[:, None]).astype(jnp.int32), axis=1),
        N_EXPERTS - 1)
    blk_exp = jnp.where(blk >= n_real, jnp.max(jnp.where(padded > 0, experts, 0)), blk_exp)
    prev_exp = jnp.concatenate([jnp.full((1,), -1, jnp.int32), blk_exp[:-1]])
    is_first = (blk < n_used) & (blk_exp != prev_exp)
    blk_slot = (jnp.cumsum(is_first.astype(jnp.int32)) + 1) % 2
    later = (experts[None, :] > experts[:, None]) & (padded[None, :] > 0)
    nxt_of = jnp.min(jnp.where(later, experts[None, :], N_EXPERTS), axis=1)
    nxt_of = jnp.where(nxt_of == N_EXPERTS, -1, nxt_of)
    blk_nxt = jnp.sum(jnp.where(blk_exp[:, None] == experts[None, :], nxt_of[None, :], 0), axis=1)
    blk_meta = jnp.concatenate([blk_exp, blk_slot, blk_nxt, n_used[None]]).astype(jnp.int32)
    return dest1, dest2, pad_idx, blk_meta, nb, n_slots


def _mixer_out_and_moe(a, w_out, x2d, gt1, g, sc, sh, w_group, b_group, w_router, b_router,
                       w1, w3, w2, layer, seq):
    t, d = x2d.shape
    w_cat = jnp.zeros((ROUTER_LANES, d), jnp.float32)
    w_cat = w_cat.at[:N_EXPERTS].set(w_router.T).at[N_EXPERTS:N_EXPERTS + N_GROUPS].set(w_group.T)
    b_cat = jnp.zeros((ROUTER_LANES, 1), jnp.float32)
    b_cat = b_cat.at[:N_EXPERTS, 0].set(b_router).at[N_EXPERTS:N_EXPERTS + N_GROUPS, 0].set(b_group)
    w_hi, w_lo = _split_hi_lo(w_cat)
    x_new, hn, info, info_t, cnt = _outproj_router(a, w_out, x2d, gt1, g, sc, sh, w_hi, w_lo,
                                                   b_cat, seq)
    dest1, dest2, pad_idx, blk_meta, nb, n_slots = _slot_plan(info_t, cnt, t)
    x_slots = _sc_dispatch(hn, dest1, dest2, pad_idx, n_slots)
    y_slots = _experts(x_slots, blk_meta, w1, w3, w2, layer, nb)
    y_pairs = _sc_gather(y_slots, jnp.concatenate([dest1, dest2], axis=1))
    return x_new, y_pairs, info


def kernel(x, c, w_ada, b_ada, norm1_g, norm2_g, ml_w_in, ml_b_gate, ml_g_out, ml_w_out,
           sw_w_in, sw_g_q, sw_g_k, sw_sinks, sw_w_out, moe_w_group, moe_b_group,
           moe_w_router, moe_b_router, moe_w1, moe_w3, moe_w2):
    bsz, seq, d = x.shape
    depth = w_ada.shape[0]
    bf = jnp.bfloat16
    mod = _ada_mod(c, w_ada, b_ada)
    x2d = x.reshape(bsz * seq, d)
    pending = None
    for layer in range(depth):
        sh1, sc1, gt1, sh2, sc2, gt2 = [
            mod[layer, :, i * d:(i + 1) * d].reshape(bsz, 1, d) for i in range(6)]
        j = layer // 2
        if layer % 2 == 0:
            w = ml_w_in[j]
            q_w, k_w = w[:, :ML_QK], w[:, ML_QK:2 * ML_QK]
            v_w = w[:, 2 * ML_QK:2 * ML_QK + ML_V]
            o_w = w[:, 2 * ML_QK + ML_V:2 * ML_QK + 2 * ML_V]
            g_w = w[:, 2 * ML_QK + 2 * ML_V:]
            w_main = jnp.concatenate([v_w, o_w, q_w], axis=1).astype(bf)
            wg_t = jnp.zeros((ML_GATE_ROWS, d), jnp.float32).at[:2 * ML_HEADS].set(g_w.T)
            wg_hi, wg_lo = _split_hi_lo(wg_t)
            wk_t = k_w.T.astype(bf)
            w_out = ml_w_out[j].astype(bf)
            outs = _inproj(x2d, norm1_g[layer], sc1, sh1, w_main, seq,
                           ml_extra=(wk_t, wg_hi, wg_lo),
                           q_cols=(2 * ML_V, 2 * ML_V + ML_QK), q_scale=ML_DK ** -0.5,
                           gate_cols=(ML_V, 2 * ML_V), pending=pending)
            if pending is not None:
                x2d, outs = outs[0], outs[1:]
            main, k_t, g_t = outs
            a = _mlstm_core(main, k_t, g_t, ml_b_gate[j], ml_g_out[j], bsz, seq)
        else:
            w = sw_w_in[j]
            dq = SW_Q_HEADS * SW_DH
            dkv = SW_KV_HEADS * SW_DH
            dup = lambda m: jnp.concatenate(
                [m.reshape(d, SW_KV_HEADS, 1, SW_DH)] * 2, axis=2).reshape(d, 2 * dkv)
            w_main = jnp.concatenate(
                [w[:, :dq], dup(w[:, dq:dq + dkv]), dup(w[:, dq + dkv:])], axis=1).astype(bf)
            w_out = sw_w_out[j].astype(bf)
            outs = _inproj(x2d, norm1_g[layer], sc1, sh1, w_main, seq, pending=pending)
            if pending is not None:
                x2d, outs = outs[0], outs[1:]
            a = _swa_core(outs[0], sw_sinks[j], sw_g_q[j], sw_g_k[j], bsz, seq)
        x2d, y_pairs, info = _mixer_out_and_moe(
            a, w_out, x2d, gt1, norm2_g[layer], sc2, sh2, moe_w_group[layer], moe_b_group[layer],
            moe_w_router[layer], moe_b_router[layer], moe_w1, moe_w3, moe_w2, layer, seq)
        pending = (y_pairs, info, gt2)
    y_pairs, info, gt2 = pending
    return _combine(x2d, y_pairs, info, gt2, seq).reshape(bsz, seq, d)
```

```python
import functools

import jax
import jax.numpy as jnp
from jax import lax
from jax.experimental import pallas as pl
from jax.experimental.pallas import tpu as pltpu
from jax.experimental.pallas import tpu_sc as plsc

EPS = 1e-6
GATE_CAP = 15.0
LOG2E = 1.4426950408889634

ML_HEADS = 4
ML_DK = 128
ML_DV = 256
ML_QK = ML_HEADS * ML_DK
ML_V = ML_HEADS * ML_DV
ML_GATE_ROWS = 16

SW_Q_HEADS = 16
SW_KV_HEADS = 4
SW_GROUP = SW_Q_HEADS // SW_KV_HEADS
SW_DH = 64
SW_WINDOW = 128
LANES = 128

N_GROUPS = 8
EXPERTS_PER_GROUP = 8
N_EXPERTS = N_GROUPS * EXPERTS_PER_GROUP
MOE_BLOCK = 256
ROUTER_LANES = 128
SC_WINDOW = 128
SC_COLS = 256
EXPERT_BLOCKS_PER_STEP = 4

VMEM_LIMIT = 56 * 1024 * 1024

_NT = (((1,), (1,)), ((), ()))


def _bdot(a, b):
    return jnp.dot(a, b, preferred_element_type=jnp.float32)


def _bdot_nt(a, b):
    return lax.dot_general(a, b, _NT, preferred_element_type=jnp.float32)


def _split_hi_lo(a):
    hi = a.astype(jnp.bfloat16)
    lo = (a - hi.astype(jnp.float32)).astype(jnp.bfloat16)
    return hi, lo


def _params(sem):
    return pltpu.CompilerParams(dimension_semantics=sem, vmem_limit_bytes=VMEM_LIMIT)


def _ada_kernel(c_ref, w_ref, b_ref, o_ref):
    c = c_ref[...]
    cond = c * jax.nn.sigmoid(c)
    c_hi, c_lo = _split_hi_lo(cond)
    w_hi, w_lo = _split_hi_lo(w_ref[...])
    acc = _bdot(c_hi, w_hi) + (_bdot(c_lo, w_hi) + _bdot(c_hi, w_lo))
    o_ref[...] = acc + b_ref[...]


def _ada_mod(c, w_ada, b_ada):
    depth, d, n = w_ada.shape
    bsz = c.shape[0]
    rows = 8
    tn = 768
    c_pad = jnp.zeros((rows, d), jnp.float32).at[:bsz].set(c)
    out = pl.pallas_call(
        _ada_kernel,
        grid=(depth, n // tn),
        in_specs=[
            pl.BlockSpec((rows, d), lambda l, j: (0, 0)),
            pl.BlockSpec((None, d, tn), lambda l, j: (l, 0, j)),
            pl.BlockSpec((None, 1, tn), lambda l, j: (l, 0, j)),
        ],
        out_specs=pl.BlockSpec((None, rows, tn), lambda l, j: (l, 0, j)),
        out_shape=jax.ShapeDtypeStruct((depth, rows, n), jnp.float32),
        compiler_params=_params(("arbitrary", "arbitrary")),
        name="ada_mod",
    )(c_pad, w_ada, b_ada.reshape(depth, 1, n))
    return out[:, :bsz]


def _modulated_norm(x, g, sc, sh):
    y = x * lax.rsqrt(jnp.mean(x * x, axis=-1, keepdims=True) + EPS)
    return y * (g * (1.0 + sc)) + sh


def _moe_combined(x, y1_ref, y2_ref, info_ref, gt_ref):
    info = info_ref[...]
    g1 = info[:, 4:5]
    g2 = info[:, 5:6]
    y1_hi, y1_lo = _unpack_bf16_pairs(y1_ref[...])
    y2_hi, y2_lo = _unpack_bf16_pairs(y2_ref[...])
    y = jnp.concatenate([g1 * y1_hi + g2 * y2_hi, g1 * y1_lo + g2 * y2_lo], axis=-1)
    return x + gt_ref[...] * y


def _qk_head_norm(acc, c0, qk_norm, gq, gk):
    q_hi, k_hi = qk_norm
    low = lax.broadcasted_iota(jnp.int32, (acc.shape[0], LANES), 1) < SW_DH
    slabs = []
    for j in range(acc.shape[1] // LANES):
        slab = acc[:, j * LANES:(j + 1) * LANES]
        if c0 + j * LANES < q_hi:
            sq = slab * slab
            ss_lo = jnp.sum(jnp.where(low, sq, 0.0), axis=-1, keepdims=True)
            ss_hi = jnp.sum(jnp.where(low, 0.0, sq), axis=-1, keepdims=True)
            rs = jnp.where(low, lax.rsqrt(ss_lo / SW_DH + EPS), lax.rsqrt(ss_hi / SW_DH + EPS))
            slab = slab * rs * gq
        elif c0 + j * LANES < k_hi:
            slab = slab * lax.rsqrt(jnp.mean(slab * slab, axis=-1, keepdims=True) + EPS) * gk
        slabs.append(slab)
    return jnp.concatenate(slabs, axis=-1)


def _inproj_kernel(*refs, n_main, chunk, q_cols, q_scale, gate_cols, qk_norm, with_ml,
                   with_combine):
    refs = list(refs)
    n_in = (5 + (4 if with_combine else 0) + (2 if with_ml else 0)
            + (2 if qk_norm is not None else 0))
    n_scratch = 4 if with_ml else 2
    ins, outs, scratch = refs[:n_in], refs[n_in:-n_scratch], refs[-n_scratch:]
    x_ref = ins.pop(0)
    if with_combine:
        y1_ref, y2_ref, info_ref, gtp_ref = ins[:4]
        ins = ins[4:]
        xo_ref = outs.pop(0)
    g_ref, sc_ref, sh_ref, w_ref = ins[:4]
    o_ref = outs[0]
    if with_ml:
        wk_ref, wgh_ref = ins[4:]
        kt_ref, gt_ref = outs[1:]
    if qk_norm is not None:
        gq_ref, gk_ref = ins[4:]

    def normalise(hb_dst, lo_dst):
        x = x_ref[...]
        if with_combine:
            x = _moe_combined(x, y1_ref, y2_ref, info_ref, gtp_ref)
            xo_ref[...] = x
        hn = _modulated_norm(x, g_ref[...], sc_ref[...], sh_ref[...])
        hb = hn.astype(jnp.bfloat16)
        hb_dst[...] = hb
        if with_ml:
            lo_dst[...] = (hn - hb.astype(jnp.float32)).astype(jnp.bfloat16)

    def project(hb_src, lo_src):
        hb = hb_src[...]
        for c0 in range(0, n_main, chunk):
            c1 = min(c0 + chunk, n_main)
            acc = _bdot(hb, w_ref[:, c0:c1])
            if q_cols is not None and q_cols[0] <= c0 < q_cols[1]:
                acc = acc * q_scale
            if gate_cols is not None and gate_cols[0] <= c0 < gate_cols[1]:
                acc = jax.nn.sigmoid(acc)
            if qk_norm is not None and c0 < qk_norm[1]:
                acc = _qk_head_norm(acc, c0, qk_norm, gq_ref[...], gk_ref[...])
            o_ref[:, c0:c1] = acc.astype(o_ref.dtype)
        if with_ml:
            nk = kt_ref.shape[0]
            stacked = _bdot_nt(wk_ref[...], hb)
            kt_ref[...] = stacked[:nk].astype(kt_ref.dtype)
            gt_ref[...] = (stacked[nk:nk + ML_GATE_ROWS]
                           + (_bdot_nt(wgh_ref[...], lo_src[...]) + stacked[nk + ML_GATE_ROWS:]))

    hb_a, hb_b = scratch[:2]
    lo_a, lo_b = scratch[2:] if with_ml else (None, None)
    s = pl.program_id(0)

    @pl.when(s == 0)
    def _():
        hb_b[...] = jnp.zeros_like(hb_b)
        if with_ml:
            lo_b[...] = jnp.zeros_like(lo_b)

    @pl.when(s % 2 == 0)
    def _():
        normalise(hb_a, lo_a)
        project(hb_b, lo_b)

    @pl.when(s % 2 == 1)
    def _():
        normalise(hb_b, lo_b)
        project(hb_a, lo_a)


def _inproj(x2d, g, sc, sh, w_main, seq, *, ml_extra=None, q_cols=None, q_scale=1.0,
            gate_cols=None, qk_norm=None, pending=None):
    t, d = x2d.shape
    tm = 512
    n_main = w_main.shape[1]
    tiles_per_batch = seq // tm
    n_tiles = t // tm
    norm_tile = lambda s: jnp.minimum(s, n_tiles - 1)
    proj_tile = lambda s: jnp.maximum(s - 1, 0)
    row = lambda s: (norm_tile(s), 0)
    per_batch = lambda s: (norm_tile(s) // tiles_per_batch, 0, 0)
    const = lambda s: (0, 0)
    in_specs = [pl.BlockSpec((tm, d), row)]
    args = [x2d]
    out_specs, out_shape = [], []
    if pending is not None:
        y_pairs, info, gt_prev = pending
        in_specs += [pl.BlockSpec((tm, d // 2), row),
                     pl.BlockSpec((tm, d // 2), lambda s: (norm_tile(s) + n_tiles, 0)),
                     pl.BlockSpec((tm, ROUTER_LANES), row),
                     pl.BlockSpec((None, 1, d), per_batch)]
        args += [y_pairs, y_pairs, info, gt_prev]
        out_specs += [pl.BlockSpec((tm, d), row)]
        out_shape += [jax.ShapeDtypeStruct((t, d), jnp.float32)]
    in_specs += [
        pl.BlockSpec((1, d), const),
        pl.BlockSpec((None, 1, d), per_batch),
        pl.BlockSpec((None, 1, d), per_batch),
        pl.BlockSpec((d, n_main), const),
    ]
    args += [g.reshape(1, d), sc, sh, w_main]
    out_specs += [pl.BlockSpec((tm, n_main), lambda s: (proj_tile(s), 0))]
    out_shape += [jax.ShapeDtypeStruct((t, n_main), jnp.bfloat16)]
    scratch = [pltpu.VMEM((tm, d), jnp.bfloat16), pltpu.VMEM((tm, d), jnp.bfloat16)]
    if ml_extra is not None:
        wk_t, wg_hi, wg_lo = ml_extra
        nk = wk_t.shape[0]
        stacked = jnp.concatenate([wk_t, wg_hi, wg_lo], axis=0)
        in_specs += [pl.BlockSpec(stacked.shape, const),
                     pl.BlockSpec(wg_hi.shape, const)]
        args += [stacked, wg_hi]
        out_specs += [pl.BlockSpec((nk, tm), lambda s: (0, proj_tile(s))),
                      pl.BlockSpec((ML_GATE_ROWS, tm), lambda s: (0, proj_tile(s)))]
        out_shape += [jax.ShapeDtypeStruct((nk, t), jnp.bfloat16),
                      jax.ShapeDtypeStruct((ML_GATE_ROWS, t), jnp.float32)]
        scratch += [pltpu.VMEM((tm, d), jnp.bfloat16), pltpu.VMEM((tm, d), jnp.bfloat16)]
    if qk_norm is not None:
        in_specs += [pl.BlockSpec((1, LANES), const), pl.BlockSpec((1, LANES), const)]
        args += [qk_norm[2], qk_norm[3]]
    kern = functools.partial(_inproj_kernel, n_main=n_main, chunk=1024, q_cols=q_cols,
                             q_scale=q_scale, gate_cols=gate_cols,
                             qk_norm=None if qk_norm is None else qk_norm[:2],
                             with_ml=ml_extra is not None, with_combine=pending is not None)
    return pl.pallas_call(
        kern,
        grid=(n_tiles + 1,),
        in_specs=in_specs,
        out_specs=out_specs,
        out_shape=out_shape,
        scratch_shapes=scratch,
        compiler_params=_params(("arbitrary",)),
        name="inproj_ml" if ml_extra is not None else "inproj_sw",
    )(*args)


def _mlstm_gate_terms(graw, bias, upper):
    H = ML_HEADS
    L = graw.shape[1]
    z = graw + bias
    gates = GATE_CAP * jnp.tanh(z / GATE_CAP)
    log_f = jnp.minimum(gates, 0.0) - jnp.log1p(jnp.exp(-jnp.abs(gates)))
    row = lax.broadcasted_iota(jnp.int32, (ML_GATE_ROWS, L), 0)
    lane = lax.broadcasted_iota(jnp.int32, (ML_GATE_ROWS, L), 1)
    is_i = row < H
    slab = jnp.where(is_i, gates, log_f)
    a1 = slab.astype(jnp.bfloat16)
    r1 = slab - a1.astype(jnp.float32)
    a2 = r1.astype(jnp.bfloat16)
    a3 = (r1 - a2.astype(jnp.float32)).astype(jnp.bfloat16)
    cum = _bdot(a1, upper) + (_bdot(a2, upper) + _bdot(a3, upper))
    ib = jnp.where(is_i, gates, cum)
    b = pltpu.roll(ib, ML_GATE_ROWS - H, 0)
    u = ib - b
    cm = u
    shift = 1
    while shift < L:
        cm = jnp.maximum(cm, jnp.where(lane >= shift, pltpu.roll(cm, shift, 1), -jnp.inf))
        shift *= 2
    return b, u, cm


def _mlstm_kernel(v_ref, o_ref, q_ref, kt_ref, gt_ref, gtn_ref, bg_ref, gout_ref, out_ref,
                  c_ref, m_ref, b_ref, u_ref, cm_ref, *, chunk):
    L = chunk
    H, dk, dv = ML_HEADS, ML_DK, ML_DV
    r_idx = lax.broadcasted_iota(jnp.int32, (L, L), 0)
    c_idx = lax.broadcasted_iota(jnp.int32, (L, L), 1)
    upper = jnp.where(r_idx <= c_idx, 1.0, 0.0).astype(jnp.bfloat16)
    causal = r_idx >= c_idx
    row = lax.broadcasted_iota(jnp.int32, (ML_GATE_ROWS, L), 0)
    ones_col = jnp.where(lax.broadcasted_iota(jnp.int32, (L, LANES), 1) == 0, 1.0, 0.0
                         ).astype(jnp.bfloat16)

    @pl.when(pl.program_id(1) == 0)
    def _():
        c_ref[...] = jnp.zeros_like(c_ref)
        m_ref[...] = jnp.zeros_like(m_ref)
        b0, u0, cm0 = _mlstm_gate_terms(gt_ref[...], bg_ref[...], upper)
        b_ref[...] = b0
        u_ref[...] = u0
        cm_ref[...] = cm0

    b16 = b_ref[...]
    u16 = u_ref[...]
    cm16 = cm_ref[...]
    b_n, u_n, cm_n = _mlstm_gate_terms(gtn_ref[...], bg_ref[...], upper)
    b_ref[...] = b_n
    u_ref[...] = u_n
    cm_ref[...] = cm_n

    m_prev = m_ref[:, 0:1]
    z16 = jnp.maximum(m_prev, cm16)
    w_inter16 = jnp.exp(m_prev - z16)
    e_negm16 = jnp.exp(-(b16 + z16))
    z_last = z16[:, L - 1:L]
    w_state16 = jnp.exp(u16 - z_last)
    decay16 = jnp.exp(m_prev - z_last)
    m_ref[...] = jnp.broadcast_to(b16[:, L - 1:L] + z_last, m_ref.shape)
    stacked = jnp.where(row < H, z16,
                        jnp.where(row < 2 * H, pltpu.roll(w_inter16, H, 0),
                                  pltpu.roll(e_negm16, 2 * H, 0)))
    cols = jnp.concatenate(
        [stacked, jnp.zeros((LANES - ML_GATE_ROWS, L), jnp.float32)], axis=0).T

    for h in range(H):
        u_r = u16[h:h + 1, :]
        z_c = cols[:, h:h + 1]
        w_inter = cols[:, H + h:H + h + 1]
        e_negm = cols[:, 2 * H + h:2 * H + h + 1]
        c_ext = c_ref[h]
        q = q_ref[:, h * dk:(h + 1) * dk]
        kt = kt_ref[h * dk:(h + 1) * dk, :]
        v_ext = jnp.concatenate([v_ref[:, h * dv:(h + 1) * dv], ones_col], axis=-1)

        w_intra = jnp.exp(jnp.where(causal, u_r - z_c, -jnp.inf))
        s = (_bdot(q, kt) * w_intra).astype(jnp.bfloat16)
        nd = w_inter * _bdot(q, c_ext.astype(jnp.bfloat16)) + _bdot(s, v_ext)
        den = nd[:, dv:dv + 1]
        hb = nd[:, :dv] * (1.0 / jnp.maximum(jnp.abs(den), e_negm))

        kw = (kt.astype(jnp.float32) * w_state16[h:h + 1, :]).astype(jnp.bfloat16)
        c_ref[h] = decay16[h:h + 1, :] * c_ext + _bdot(kw, v_ext)

        y = hb * lax.rsqrt(jnp.mean(hb * hb, axis=-1, keepdims=True) + EPS)
        y = y * gout_ref[:, h * dv:(h + 1) * dv]
        og = o_ref[:, h * dv:(h + 1) * dv].astype(jnp.float32)
        out_ref[:, h * dv:(h + 1) * dv] = (y * og).astype(out_ref.dtype)


def _mlstm_core(main, k_t, g_t, b_gate, g_out, bsz, seq, chunk=256):
    t = main.shape[0]
    nc = seq // chunk
    blk = lambda b, c: b * nc + c
    bg = jnp.zeros((ML_GATE_ROWS, 1), jnp.float32).at[:2 * ML_HEADS, 0].set(b_gate)
    return pl.pallas_call(
        functools.partial(_mlstm_kernel, chunk=chunk),
        grid=(bsz, nc),
        in_specs=[
            pl.BlockSpec((chunk, ML_V), lambda b, c: (blk(b, c), 0)),
            pl.BlockSpec((chunk, ML_V), lambda b, c: (blk(b, c), 1)),
            pl.BlockSpec((chunk, ML_QK), lambda b, c: (blk(b, c), 4)),
            pl.BlockSpec((ML_QK, chunk), lambda b, c: (0, blk(b, c))),
            pl.BlockSpec((ML_GATE_ROWS, chunk), lambda b, c: (0, blk(b, c))),
            pl.BlockSpec((ML_GATE_ROWS, chunk), lambda b, c: (0, blk(b, jnp.minimum(c + 1, nc - 1)))),
            pl.BlockSpec((ML_GATE_ROWS, 1), lambda b, c: (0, 0)),
            pl.BlockSpec((1, ML_V), lambda b, c: (0, 0)),
        ],
        out_specs=pl.BlockSpec((chunk, ML_V), lambda b, c: (blk(b, c), 0)),
        out_shape=jax.ShapeDtypeStruct((t, ML_V), jnp.bfloat16),
        scratch_shapes=[
            pltpu.VMEM((ML_HEADS, ML_DK, ML_DV + LANES), jnp.float32),
            pltpu.VMEM((ML_GATE_ROWS, LANES), jnp.float32),
            pltpu.VMEM((ML_GATE_ROWS, chunk), jnp.float32),
            pltpu.VMEM((ML_GATE_ROWS, chunk), jnp.float32),
            pltpu.VMEM((ML_GATE_ROWS, chunk), jnp.float32),
        ],
        compiler_params=_params(("arbitrary", "arbitrary")),
        name="mlstm_core",
    )(main, main, main, k_t, g_t, g_t, bg, g_out.reshape(1, ML_V))


def _swa_kernel(sinks_ref, q_ref, kc_ref, kp_ref, vc_ref, vp_ref, bias_ref, o_ref):
    W = SW_WINDOW
    lane = lax.broadcasted_iota(jnp.int32, (W, LANES), 1)
    low = lane < SW_DH
    row_col = lax.broadcasted_iota(jnp.int32, (SW_GROUP * W, 1), 0)
    bias = bias_ref[jnp.minimum(pl.program_id(1), 1)]
    for g in range(SW_KV_HEADS):
        sl = slice(g * LANES, (g + 1) * LANES)
        kn = jnp.concatenate([kp_ref[:, sl], kc_ref[:, sl]], axis=0)
        v2 = jnp.concatenate([vp_ref[:, sl], vc_ref[:, sl]], axis=0)
        sink = jnp.full((SW_GROUP * W, 1), sinks_ref[g * SW_GROUP + SW_GROUP - 1], jnp.float32)
        for j in range(SW_GROUP - 2, -1, -1):
            sink = jnp.where(row_col < (j + 1) * W, sinks_ref[g * SW_GROUP + j], sink)
        parts = []
        for p in range(2):
            c0 = g * SW_GROUP * SW_DH + p * LANES
            qp = q_ref[:, c0:c0 + LANES].astype(jnp.float32)
            parts.append(jnp.where(low, qp, 0.0).astype(jnp.bfloat16))
            parts.append(jnp.where(low, 0.0, qp).astype(jnp.bfloat16))
        q4 = jnp.concatenate(parts, axis=0)
        scores = _bdot_nt(q4, kn) + bias
        m = jnp.maximum(jnp.max(scores, axis=-1, keepdims=True), sink)
        pexp = jnp.exp2(scores - m)
        denom = jnp.sum(pexp, axis=-1, keepdims=True) + jnp.exp2(sink - m)
        o4 = _bdot(pexp.astype(jnp.bfloat16), v2) * (1.0 / denom)
        for p in range(2):
            oa = o4[(2 * p) * W:(2 * p + 1) * W]
            ob = o4[(2 * p + 1) * W:(2 * p + 2) * W]
            c0 = g * SW_GROUP * SW_DH + p * LANES
            o_ref[:, c0:c0 + LANES] = jnp.where(low, oa, ob).astype(o_ref.dtype)


def _swa_core(proj, sinks, bsz, seq):
    t = proj.shape[0]
    W = SW_WINDOW
    nb = seq // W
    dq = SW_Q_HEADS * SW_DH
    kv_w = SW_KV_HEADS * LANES
    k_blk = dq // kv_w
    v_blk = k_blk + 1
    cur = lambda b, n, s: b * nb + n
    prev = lambda b, n, s: b * nb + jnp.maximum(n - 1, 0)
    sinks = sinks.astype(jnp.float32) * LOG2E
    qi = (jnp.arange(SW_GROUP * W) % W)[:, None]
    ki = jnp.arange(2 * W)[None, :]
    rel = qi + W - ki
    in_win = (rel >= 0) & (rel < W)
    bias = jnp.stack([jnp.where(in_win & (ki >= W), 0.0, -jnp.inf),
                      jnp.where(in_win, 0.0, -jnp.inf)]).astype(jnp.float32)
    grid_spec = pltpu.PrefetchScalarGridSpec(
        num_scalar_prefetch=1,
        grid=(bsz, nb),
        in_specs=[
            pl.BlockSpec((W, dq), lambda b, n, s: (cur(b, n, s), 0)),
            pl.BlockSpec((W, kv_w), lambda b, n, s: (cur(b, n, s), k_blk)),
            pl.BlockSpec((W, kv_w), lambda b, n, s: (prev(b, n, s), k_blk)),
            pl.BlockSpec((W, kv_w), lambda b, n, s: (cur(b, n, s), v_blk)),
            pl.BlockSpec((W, kv_w), lambda b, n, s: (prev(b, n, s), v_blk)),
            pl.BlockSpec((2, SW_GROUP * W, 2 * W), lambda b, n, s: (0, 0, 0)),
        ],
        out_specs=pl.BlockSpec((W, dq), lambda b, n, s: (cur(b, n, s), 0)),
    )
    return pl.pallas_call(
        _swa_kernel,
        grid_spec=grid_spec,
        out_shape=jax.ShapeDtypeStruct((t, dq), jnp.bfloat16),
        compiler_params=_params(("arbitrary", "arbitrary")),
        name="swa_core",
    )(sinks, proj, proj, proj, proj, proj, bias)


def _pack_rounded_pairs(r):
    k = r.shape[1] // 2
    hi = lax.bitcast_convert_type(r[:, :k], jnp.uint32)
    lo = lax.bitcast_convert_type(r[:, k:], jnp.uint32)
    return hi | (lo >> 16)


def _pack_bf16_pairs(a):
    return _pack_rounded_pairs(a.astype(jnp.bfloat16).astype(jnp.float32))


def _unpack_bf16_pairs(u):
    hi = lax.bitcast_convert_type(u & jnp.uint32(0xFFFF0000), jnp.float32)
    lo = lax.bitcast_convert_type(u << 16, jnp.float32)
    return hi, lo


def _route_tile(x_new, g_ref, sc_ref, sh_ref, wh_ref, wl_ref, b_ref,
                hn_ref, info_ref, infot_ref, cnt_ref, carry_ref, earlier_ref, tm):
    hn = _modulated_norm(x_new, g_ref[...], sc_ref[...], sh_ref[...])
    h_hi = hn.astype(jnp.bfloat16)
    hi_f32 = h_hi.astype(jnp.float32)
    h_lo = (hn - hi_f32).astype(jnp.bfloat16)
    hn_ref[...] = _pack_rounded_pairs(hi_f32)
    wide = _bdot_nt(wl_ref[...], h_hi)
    logits = (wide[:ROUTER_LANES] + (_bdot_nt(wh_ref[...], h_lo) + wide[ROUTER_LANES:])
              + b_ref[...])
    E8 = EXPERTS_PER_GROUP
    sub = lax.broadcasted_iota(jnp.int32, (E8, tm), 0).astype(jnp.float32)
    big = float(ROUTER_LANES)
    neg = -jnp.inf

    gl = logits[N_EXPERTS:N_EXPERTS + N_GROUPS]
    gmax = jnp.max(gl, axis=0, keepdims=True)
    gsel = jnp.min(jnp.where(gl == gmax, sub, big), axis=0, keepdims=True)
    p_grp = 1.0 / jnp.sum(jnp.exp(gl - gmax), axis=0, keepdims=True)

    el = logits[0:E8]
    for grp in range(1, N_GROUPS):
        el = jnp.where(gsel == grp, logits[grp * E8:(grp + 1) * E8], el)
    v1 = jnp.max(el, axis=0, keepdims=True)
    j1 = jnp.min(jnp.where(el == v1, sub, big), axis=0, keepdims=True)
    el2 = jnp.where(sub == j1, neg, el)
    v2 = jnp.max(el2, axis=0, keepdims=True)
    j2 = jnp.min(jnp.where(el2 == v2, sub, big), axis=0, keepdims=True)
    i1 = gsel * E8 + j1
    i2 = gsel * E8 + j2
    e21 = jnp.exp(v2 - v1)
    gate1 = p_grp / (1.0 + e21)
    gate2 = p_grp * e21 / (1.0 + e21)

    erow = lax.broadcasted_iota(jnp.int32, (N_EXPERTS, tm), 0).astype(jnp.float32)
    hit1 = erow == i1
    hit2 = erow == i2
    onehot = jnp.where(hit1 | hit2, 1.0, 0.0)
    carry = carry_ref[:, 0:1]
    before = _bdot(onehot.astype(jnp.bfloat16), earlier_ref[...]) + carry
    rank1 = jnp.sum(jnp.where(hit1, before, 0.0), axis=0, keepdims=True)
    rank2 = jnp.sum(jnp.where(hit2, before, 0.0), axis=0, keepdims=True)
    total = carry + jnp.sum(onehot, axis=1, keepdims=True)
    carry_ref[...] = jnp.broadcast_to(total, carry_ref.shape)
    cnt_ref[...] = jnp.broadcast_to(total, cnt_ref.shape)

    info_t = jnp.where(sub == 0, i1, 0.0)
    info_t = jnp.where(sub == 1, i2, info_t)
    info_t = jnp.where(sub == 2, rank1, info_t)
    info_t = jnp.where(sub == 3, rank2, info_t)
    info_t = jnp.where(sub == 4, gate1, info_t)
    info_t = jnp.where(sub == 5, gate2, info_t)
    infot_ref[...] = info_t
    info_ref[...] = jnp.concatenate(
        [info_t, jnp.zeros((ROUTER_LANES - E8, tm), jnp.float32)], axis=0).T


def _router_kernel(a_ref, wo_ref, x_ref, gt_ref, g_ref, sc_ref, sh_ref, wh_ref, wl_ref, b_ref,
                   xo_ref, hn_ref, info_ref, infot_ref, cnt_ref, carry_ref, earlier_ref, *, tm):
    @pl.when(pl.program_id(0) == 0)
    def _():
        carry_ref[...] = jnp.zeros_like(carry_ref)
        r_idx = lax.broadcasted_iota(jnp.int32, (tm, tm), 0)
        c_idx = lax.broadcasted_iota(jnp.int32, (tm, tm), 1)
        earlier_ref[...] = jnp.where(r_idx < c_idx, 1.0, 0.0).astype(jnp.bfloat16)

    x_new = x_ref[...] + gt_ref[...] * _bdot(a_ref[...], wo_ref[...])
    xo_ref[...] = x_new
    _route_tile(x_new, g_ref, sc_ref, sh_ref, wh_ref, wl_ref, b_ref,
                hn_ref, info_ref, infot_ref, cnt_ref, carry_ref, earlier_ref, tm)


def _outproj_router(a, w_out, x2d, gt, g, sc, sh, w_hi, w_lo, bias, seq):
    t, d = x2d.shape
    tm = 512
    tiles_per_batch = seq // tm
    per_batch = lambda i: (i // tiles_per_batch, 0, 0)
    const = lambda i: (0, 0)
    return pl.pallas_call(
        functools.partial(_router_kernel, tm=tm),
        grid=(t // tm,),
        in_specs=[
            pl.BlockSpec((tm, a.shape[1]), lambda i: (i, 0)),
            pl.BlockSpec(w_out.shape, const),
            pl.BlockSpec((tm, d), lambda i: (i, 0)),
            pl.BlockSpec((None, 1, d), per_batch),
            pl.BlockSpec((1, d), const),
            pl.BlockSpec((None, 1, d), per_batch),
            pl.BlockSpec((None, 1, d), per_batch),
            pl.BlockSpec((ROUTER_LANES, d), const),
            pl.BlockSpec((2 * ROUTER_LANES, d), const),
            pl.BlockSpec((ROUTER_LANES, 1), const),
        ],
        out_specs=[
            pl.BlockSpec((tm, d), lambda i: (i, 0)),
            pl.BlockSpec((tm, d // 2), lambda i: (i, 0)),
            pl.BlockSpec((tm, ROUTER_LANES), lambda i: (i, 0)),
            pl.BlockSpec((EXPERTS_PER_GROUP, tm), lambda i: (0, i)),
            pl.BlockSpec((N_EXPERTS, LANES), const),
        ],
        out_shape=[
            jax.ShapeDtypeStruct((t, d), jnp.float32),
            jax.ShapeDtypeStruct((t, d // 2), jnp.uint32),
            jax.ShapeDtypeStruct((t, ROUTER_LANES), jnp.float32),
            jax.ShapeDtypeStruct((EXPERTS_PER_GROUP, t), jnp.float32),
            jax.ShapeDtypeStruct((N_EXPERTS, LANES), jnp.float32),
        ],
        scratch_shapes=[pltpu.VMEM((N_EXPERTS, LANES), jnp.float32),
                        pltpu.VMEM((tm, tm), jnp.bfloat16)],
        compiler_params=_params(("arbitrary",)),
        name="outproj_router",
    )(a, w_out, x2d, gt, g.reshape(1, d), sc, sh, w_hi, jnp.concatenate([w_hi, w_lo], axis=0), bias)


def _sc_mesh():
    return plsc.VectorSubcoreMesh(core_axis_name="c", subcore_axis_name="s")


def _sc_dispatch(rows, d0, d1, pad_idx, n_slots):
    t, w = rows.shape
    n_pad = pad_idx.shape[1]
    zeros = jnp.zeros((SC_WINDOW, w), rows.dtype)
    sem = (pltpu.PARALLEL, pltpu.ARBITRARY)

    @pl.kernel(out_type=jax.ShapeDtypeStruct((n_slots + SC_WINDOW, w), rows.dtype), mesh=_sc_mesh(),
               scratch_types=[pltpu.SemaphoreType.DMA, pltpu.SemaphoreType.DMA])
    def dispatch(x_hbm, d0_hbm, d1_hbm, z_hbm, p_hbm, o_hbm, sem0, sem1):
        def scatter_rows(x_vmem, i0_vmem, i1_vmem):
            cols = pl.ds(pl.program_id(1) * SC_COLS, SC_COLS)
            first = pltpu.async_copy(x_vmem, o_hbm.at[i0_vmem.at[0], cols], sem0)
            second = pltpu.async_copy(x_vmem, o_hbm.at[i1_vmem.at[0], cols], sem1)
            first.wait()
            second.wait()

        pltpu.emit_pipeline(
            scatter_rows,
            grid=(t // SC_WINDOW, w // SC_COLS),
            in_specs=[pl.BlockSpec((SC_WINDOW, SC_COLS), lambda i, j: (i, j)),
                      pl.BlockSpec((1, SC_WINDOW), lambda i, j: (0, i)),
                      pl.BlockSpec((1, SC_WINDOW), lambda i, j: (0, i))],
            out_specs=[],
            core_axis_name=("c", "s"),
            dimension_semantics=sem,
        )(x_hbm, d0_hbm, d1_hbm)

        def scatter_zeros(z_vmem, p_vmem):
            cols = pl.ds(pl.program_id(1) * SC_COLS, SC_COLS)
            pltpu.sync_copy(z_vmem, o_hbm.at[p_vmem.at[0], cols])

        pltpu.emit_pipeline(
            scatter_zeros,
            grid=(n_pad // SC_WINDOW, w // SC_COLS),
            in_specs=[pl.BlockSpec((SC_WINDOW, SC_COLS), lambda i, j: (0, j)),
                      pl.BlockSpec((1, SC_WINDOW), lambda i, j: (0, i))],
            out_specs=[],
            core_axis_name=("c", "s"),
            dimension_semantics=sem,
        )(z_hbm, p_hbm)

    return dispatch(rows, d0, d1, zeros, pad_idx)


def _sc_gather(src, idx):
    n_out = idx.shape[1]
    w = src.shape[1]

    @pl.kernel(out_type=jax.ShapeDtypeStruct((n_out, w), src.dtype), mesh=_sc_mesh())
    def gather(x_hbm, i_hbm, o_hbm):
        def gather_rows(i_vmem, o_vmem):
            cols = pl.ds(pl.program_id(1) * SC_COLS, SC_COLS)
            pltpu.sync_copy(x_hbm.at[i_vmem.at[0], cols], o_vmem)

        pltpu.emit_pipeline(
            gather_rows,
            grid=(n_out // SC_WINDOW, w // SC_COLS),
            in_specs=[pl.BlockSpec((1, SC_WINDOW), lambda i, j: (0, i))],
            out_specs=[pl.BlockSpec((SC_WINDOW, SC_COLS), lambda i, j: (i, j))],
            core_axis_name=("c", "s"),
            dimension_semantics=(pltpu.PARALLEL, pltpu.ARBITRARY),
        )(i_hbm, o_hbm)

    return gather(src, idx)


def _expert_kernel(meta_ref, x_ref, w1_hbm, w3_hbm, w2_hbm, y_ref,
                   w1_buf, w3_buf, w2_buf, w1_c, w3_c, w2_c, sems, *, layer, nb):
    def weight_copies(expert, s):
        return (pltpu.make_async_copy(w1_hbm.at[layer, expert], w1_buf.at[s], sems.at[s, 0]),
                pltpu.make_async_copy(w3_hbm.at[layer, expert], w3_buf.at[s], sems.at[s, 1]),
                pltpu.make_async_copy(w2_hbm.at[layer, expert], w2_buf.at[s], sems.at[s, 2]))

    @pl.when(pl.program_id(0) == 0)
    def _():
        for cp in weight_copies(meta_ref[0], meta_ref[nb]):
            cp.start()

    for j in range(EXPERT_BLOCKS_PER_STEP):
        i = pl.program_id(0) * EXPERT_BLOCKS_PER_STEP + j
        rows = slice(j * MOE_BLOCK, (j + 1) * MOE_BLOCK)
        e = meta_ref[i]
        slot = meta_ref[nb + i]
        nxt = meta_ref[2 * nb + i]
        used = i < meta_ref[3 * nb]
        first = used & ((i == 0) | (e != meta_ref[jnp.maximum(i - 1, 0)]))

        @pl.when(first)
        def _():
            for cp in weight_copies(e, slot):
                cp.wait()

            @pl.when(nxt >= 0)
            def _():
                for cp in weight_copies(nxt, 1 - slot):
                    cp.start()

            w1_c[...] = w1_buf[slot].astype(jnp.bfloat16)
            w3_c[...] = w3_buf[slot].astype(jnp.bfloat16)
            w2_c[...] = w2_buf[slot].astype(jnp.bfloat16)

        @pl.when(used)
        def _():
            x_hi, x_lo = _unpack_bf16_pairs(x_ref[rows, :])
            xb = jnp.concatenate([x_hi, x_lo], axis=-1).astype(jnp.bfloat16)
            h1 = _bdot(xb, w1_c[...])
            h3 = _bdot(xb, w3_c[...])
            act = (h1 * jax.nn.sigmoid(h1) * h3).astype(jnp.bfloat16)
            y_ref[rows, :] = _pack_bf16_pairs(_bdot(act, w2_c[...]))

        @pl.when(jnp.logical_not(used))
        def _():
            y_ref[rows, :] = jnp.zeros((MOE_BLOCK, y_ref.shape[1]), y_ref.dtype)


def _experts(x_slots, blk_meta, w1, w3, w2, layer, nb):
    dp = x_slots.shape[1]
    d, de = w1.shape[-2:]
    step_rows = EXPERT_BLOCKS_PER_STEP * MOE_BLOCK
    assert nb % EXPERT_BLOCKS_PER_STEP == 0
    last_used_step = lambda s: (s[3 * nb] - 1) // EXPERT_BLOCKS_PER_STEP
    grid_spec = pltpu.PrefetchScalarGridSpec(
        num_scalar_prefetch=1,
        grid=(nb // EXPERT_BLOCKS_PER_STEP,),
        in_specs=[
            pl.BlockSpec((step_rows, dp), lambda i, s: (jnp.minimum(i, last_used_step(s)), 0)),
            pl.BlockSpec(memory_space=pl.ANY),
            pl.BlockSpec(memory_space=pl.ANY),
            pl.BlockSpec(memory_space=pl.ANY),
        ],
        out_specs=pl.BlockSpec((step_rows, dp), lambda i, s: (i, 0)),
        scratch_shapes=[
            pltpu.VMEM((2, d, de), jnp.float32),
            pltpu.VMEM((2, d, de), jnp.float32),
            pltpu.VMEM((2, de, d), jnp.float32),
            pltpu.VMEM((d, de), jnp.bfloat16),
            pltpu.VMEM((d, de), jnp.bfloat16),
            pltpu.VMEM((de, d), jnp.bfloat16),
            pltpu.SemaphoreType.DMA((2, 3)),
        ],
    )
    return pl.pallas_call(
        functools.partial(_expert_kernel, layer=layer, nb=nb),
        grid_spec=grid_spec,
        out_shape=jax.ShapeDtypeStruct((nb * MOE_BLOCK, dp), jnp.uint32),
        compiler_params=_params(("arbitrary",)),
        name="moe_experts",
    )(blk_meta, x_slots, w1, w3, w2)


def _combine_kernel(x_ref, y1_ref, y2_ref, info_ref, gt_ref, o_ref):
    o_ref[...] = _moe_combined(x_ref[...], y1_ref, y2_ref, info_ref, gt_ref)


def _combine(x2d, y_pairs, info, gt, seq):
    t, d = x2d.shape
    tm = min(1024, seq)
    tiles_per_batch = seq // tm
    second = t // tm
    return pl.pallas_call(
        _combine_kernel,
        grid=(t // tm,),
        in_specs=[
            pl.BlockSpec((tm, d), lambda i: (i, 0)),
            pl.BlockSpec((tm, d // 2), lambda i: (i, 0)),
            pl.BlockSpec((tm, d // 2), lambda i: (i + second, 0)),
            pl.BlockSpec((tm, ROUTER_LANES), lambda i: (i, 0)),
            pl.BlockSpec((None, 1, d), lambda i: (i // tiles_per_batch, 0, 0)),
        ],
        out_specs=pl.BlockSpec((tm, d), lambda i: (i, 0)),
        out_shape=jax.ShapeDtypeStruct((t, d), jnp.float32),
        compiler_params=_params(("arbitrary",)),
        name="moe_combine",
    )(x2d, y_pairs, y_pairs, info, gt)


def _slot_plan(info_t, cnt, t):
    counts = cnt[:, 0].astype(jnp.int32)
    padded = (counts + MOE_BLOCK - 1) // MOE_BLOCK * MOE_BLOCK
    pad_ends = jnp.cumsum(padded)
    pad_starts = pad_ends - padded
    nb = -(-(2 * t) // MOE_BLOCK) + N_EXPERTS
    n_slots = nb * MOE_BLOCK
    it = info_t.astype(jnp.int32)
    onehot_start = lambda e: jnp.sum(
        jnp.where(e[None, :] == jnp.arange(N_EXPERTS, dtype=jnp.int32)[:, None],
                  pad_starts[:, None], 0), axis=0)
    dest1 = (onehot_start(it[0]) + it[2]).reshape(1, t)
    dest2 = (onehot_start(it[1]) + it[3]).reshape(1, t)
    lane = jnp.arange(MOE_BLOCK, dtype=jnp.int32)[None, :]
    n_padding = (padded - counts)[:, None]
    wrapped = (pad_starts + counts)[:, None] + lane % jnp.maximum(n_padding, 1)
    pad_idx = jnp.where(n_padding > 0, wrapped, n_slots + lane % SC_WINDOW).reshape(-1)
    n_real = pad_ends[-1] // MOE_BLOCK
    n_used = -(-n_real // EXPERT_BLOCKS_PER_STEP) * EXPERT_BLOCKS_PER_STEP
    tail = jnp.arange((EXPERT_BLOCKS_PER_STEP - 1) * MOE_BLOCK, dtype=jnp.int32)
    tail_idx = jnp.where(tail < (n_used - n_real) * MOE_BLOCK, pad_ends[-1] + tail,
                         n_slots + tail % SC_WINDOW)
    pad_idx = jnp.concatenate([pad_idx, tail_idx]).reshape(1, -1)
    experts = jnp.arange(N_EXPERTS, dtype=jnp.int32)
    blk = jnp.arange(nb, dtype=jnp.int32)
    blk_exp = jnp.minimum(
        jnp.sum((pad_ends[None, :] <= (blk * MOE_BLOCK)[:, None]).astype(jnp.int32), axis=1),
        N_EXPERTS - 1)
    blk_exp = jnp.where(blk >= n_real, jnp.max(jnp.where(padded > 0, experts, 0)), blk_exp)
    prev_exp = jnp.concatenate([jnp.full((1,), -1, jnp.int32), blk_exp[:-1]])
    is_first = (blk < n_used) & (blk_exp != prev_exp)
    blk_slot = (jnp.cumsum(is_first.astype(jnp.int32)) + 1) % 2
    later = (experts[None, :] > experts[:, None]) & (padded[None, :] > 0)
    nxt_of = jnp.min(jnp.where(later, experts[None, :], N_EXPERTS), axis=1)
    nxt_of = jnp.where(nxt_of == N_EXPERTS, -1, nxt_of)
    blk_nxt = jnp.sum(jnp.where(blk_exp[:, None] == experts[None, :], nxt_of[None, :], 0), axis=1)
    blk_meta = jnp.concatenate([blk_exp, blk_slot, blk_nxt, n_used[None]]).astype(jnp.int32)
    return dest1, dest2, pad_idx, blk_meta, nb, n_slots


def _mixer_out_and_moe(a, w_out, x2d, gt1, g, sc, sh, w_group, b_group, w_router, b_router,
                       w1, w3, w2, layer, seq):
    t, d = x2d.shape
    w_cat = jnp.zeros((ROUTER_LANES, d), jnp.float32)
    w_cat = w_cat.at[:N_EXPERTS].set(w_router.T).at[N_EXPERTS:N_EXPERTS + N_GROUPS].set(w_group.T)
    b_cat = jnp.zeros((ROUTER_LANES, 1), jnp.float32)
    b_cat = b_cat.at[:N_EXPERTS, 0].set(b_router).at[N_EXPERTS:N_EXPERTS + N_GROUPS, 0].set(b_group)
    w_hi, w_lo = _split_hi_lo(w_cat)
    x_new, hn, info, info_t, cnt = _outproj_router(a, w_out, x2d, gt1, g, sc, sh, w_hi, w_lo,
                                                   b_cat, seq)
    dest1, dest2, pad_idx, blk_meta, nb, n_slots = _slot_plan(info_t, cnt, t)
    x_slots = _sc_dispatch(hn, dest1, dest2, pad_idx, n_slots)
    y_slots = _experts(x_slots, blk_meta, w1, w3, w2, layer, nb)
    y_pairs = _sc_gather(y_slots, jnp.concatenate([dest1, dest2], axis=1))
    return x_new, y_pairs, info


def kernel(x, c, w_ada, b_ada, norm1_g, norm2_g, ml_w_in, ml_b_gate, ml_g_out, ml_w_out,
           sw_w_in, sw_g_q, sw_g_k, sw_sinks, sw_w_out, moe_w_group, moe_b_group,
           moe_w_router, moe_b_router, moe_w1, moe_w3, moe_w2):
    bsz, seq, d = x.shape
    depth = w_ada.shape[0]
    bf = jnp.bfloat16
    mod = _ada_mod(c, w_ada, b_ada)
    x2d = x.reshape(bsz * seq, d)
    pending = None
    for layer in range(depth):
        sh1, sc1, gt1, sh2, sc2, gt2 = [
            mod[layer, :, i * d:(i + 1) * d].reshape(bsz, 1, d) for i in range(6)]
        j = layer // 2
        if layer % 2 == 0:
            w = ml_w_in[j]
            q_w, k_w = w[:, :ML_QK], w[:, ML_QK:2 * ML_QK]
            v_w = w[:, 2 * ML_QK:2 * ML_QK + ML_V]
            o_w = w[:, 2 * ML_QK + ML_V:2 * ML_QK + 2 * ML_V]
            g_w = w[:, 2 * ML_QK + 2 * ML_V:]
            w_main = jnp.concatenate([v_w, o_w, q_w], axis=1).astype(bf)
            wg_t = jnp.zeros((ML_GATE_ROWS, d), jnp.float32).at[:2 * ML_HEADS].set(g_w.T)
            wg_hi, wg_lo = _split_hi_lo(wg_t)
            wk_t = k_w.T.astype(bf)
            w_out = ml_w_out[j].astype(bf)
            outs = _inproj(x2d, norm1_g[layer], sc1, sh1, w_main, seq,
                           ml_extra=(wk_t, wg_hi, wg_lo),
                           q_cols=(2 * ML_V, 2 * ML_V + ML_QK), q_scale=ML_DK ** -0.5,
                           gate_cols=(ML_V, 2 * ML_V), pending=pending)
            if pending is not None:
                x2d, outs = outs[0], outs[1:]
            main, k_t, g_t = outs
            a = _mlstm_core(main, k_t, g_t, ml_b_gate[j], ml_g_out[j], bsz, seq)
        else:
            w = sw_w_in[j]
            dq = SW_Q_HEADS * SW_DH
            dkv = SW_KV_HEADS * SW_DH
            dup = lambda m: jnp.concatenate(
                [m.reshape(d, SW_KV_HEADS, 1, SW_DH)] * 2, axis=2).reshape(d, 2 * dkv)
            w_main = jnp.concatenate(
                [w[:, :dq], dup(w[:, dq:dq + dkv]), dup(w[:, dq + dkv:])], axis=1).astype(bf)
            w_out = sw_w_out[j].astype(bf)
            gq = jnp.concatenate([sw_g_q[j], sw_g_q[j]]).reshape(1, LANES) * (SW_DH ** -0.5 * LOG2E)
            gk = jnp.concatenate([sw_g_k[j], sw_g_k[j]]).reshape(1, LANES)
            outs = _inproj(x2d, norm1_g[layer], sc1, sh1, w_main, seq, pending=pending,
                           qk_norm=(dq, dq + 2 * dkv, gq, gk))
            if pending is not None:
                x2d, outs = outs[0], outs[1:]
            a = _swa_core(outs[0], sw_sinks[j], bsz, seq)
        x2d, y_pairs, info = _mixer_out_and_moe(
            a, w_out, x2d, gt1, norm2_g[layer], sc2, sh2, moe_w_group[layer], moe_b_group[layer],
            moe_w_router[layer], moe_b_router[layer], moe_w1, moe_w3, moe_w2, layer, seq)
        pending = (y_pairs, info, gt2)
    y_pairs, info, gt2 = pending
    return _combine(x2d, y_pairs, info, gt2, seq).reshape(bsz, seq, d)
```

```python
import functools

import jax
import jax.numpy as jnp
from jax import lax
from jax.experimental import pallas as pl
from jax.experimental.pallas import tpu as pltpu
from jax.experimental.pallas import tpu_sc as plsc

EPS = 1e-6
GATE_CAP = 15.0
LOG2E = 1.4426950408889634

ML_HEADS = 4
ML_DK = 128
ML_DV = 256
ML_QK = ML_HEADS * ML_DK
ML_V = ML_HEADS * ML_DV
ML_GATE_ROWS = 16

SW_Q_HEADS = 16
SW_KV_HEADS = 4
SW_GROUP = SW_Q_HEADS // SW_KV_HEADS
SW_DH = 64
SW_WINDOW = 128
LANES = 128

N_GROUPS = 8
EXPERTS_PER_GROUP = 8
N_EXPERTS = N_GROUPS * EXPERTS_PER_GROUP
MOE_BLOCK = 256
ROUTER_LANES = 128
SC_WINDOW = 128
SC_ROWS = 64
EXPERT_BLOCKS_PER_STEP = 4

SUBLANES = 8
TOKEN_TILE = 512
COMBINE_TILE = 1024
PROJ_COL_CHUNK = 1024
ML_CHUNK = 256
ADA_COL_TILE = 768
V7X_VMEM_BYTES = 64 * 1024 * 1024
VMEM_LIMIT = V7X_VMEM_BYTES - 8 * 1024 * 1024

_NT = (((1,), (1,)), ((), ()))


def _bdot(a, b):
    return jnp.dot(a, b, preferred_element_type=jnp.float32)


def _bdot_nt(a, b):
    return lax.dot_general(a, b, _NT, preferred_element_type=jnp.float32)


def _split_hi_lo(a):
    hi = a.astype(jnp.bfloat16)
    lo = (a - hi.astype(jnp.float32)).astype(jnp.bfloat16)
    return hi, lo


def _params(sem):
    return pltpu.CompilerParams(dimension_semantics=sem, vmem_limit_bytes=VMEM_LIMIT)


def _ada_kernel(c_ref, w_ref, b_ref, o_ref):
    c = c_ref[...]
    cond = c * jax.nn.sigmoid(c)
    c_hi, c_lo = _split_hi_lo(cond)
    w_hi, w_lo = _split_hi_lo(w_ref[...])
    acc = _bdot(c_hi, w_hi) + (_bdot(c_lo, w_hi) + _bdot(c_hi, w_lo))
    o_ref[...] = acc + b_ref[...]


def _ada_mod(c, w_ada, b_ada):
    depth, d, n = w_ada.shape
    bsz = c.shape[0]
    rows = SUBLANES
    tn = ADA_COL_TILE
    c_pad = jnp.zeros((rows, d), jnp.float32).at[:bsz].set(c)
    out = pl.pallas_call(
        _ada_kernel,
        grid=(depth, n // tn),
        in_specs=[
            pl.BlockSpec((rows, d), lambda l, j: (0, 0)),
            pl.BlockSpec((None, d, tn), lambda l, j: (l, 0, j)),
            pl.BlockSpec((None, 1, tn), lambda l, j: (l, 0, j)),
        ],
        out_specs=pl.BlockSpec((None, rows, tn), lambda l, j: (l, 0, j)),
        out_shape=jax.ShapeDtypeStruct((depth, rows, n), jnp.float32),
        compiler_params=_params(("arbitrary", "arbitrary")),
        name="ada_mod",
    )(c_pad, w_ada, b_ada.reshape(depth, 1, n))
    return out[:, :bsz]


def _modulated_norm(x, g, sc, sh):
    y = x * lax.rsqrt(jnp.mean(x * x, axis=-1, keepdims=True) + EPS)
    return y * (g * (1.0 + sc)) + sh


def _moe_combined(x, y1_ref, y2_ref, info_ref, gt_ref):
    info = info_ref[...]
    g1 = info[:, 4:5]
    g2 = info[:, 5:6]
    y1_hi, y1_lo = _unpack_bf16_pairs(y1_ref[...])
    y2_hi, y2_lo = _unpack_bf16_pairs(y2_ref[...])
    y = jnp.concatenate([g1 * y1_hi + g2 * y2_hi, g1 * y1_lo + g2 * y2_lo], axis=-1)
    return x + gt_ref[...] * y


def _qk_head_norm(acc, c0, qk_norm, gq, gk):
    q_hi, k_hi = qk_norm
    low = lax.broadcasted_iota(jnp.int32, (acc.shape[0], LANES), 1) < SW_DH
    slabs = []
    for j in range(acc.shape[1] // LANES):
        slab = acc[:, j * LANES:(j + 1) * LANES]
        if c0 + j * LANES < q_hi:
            sq = slab * slab
            ss_lo = jnp.sum(jnp.where(low, sq, 0.0), axis=-1, keepdims=True)
            ss_hi = jnp.sum(jnp.where(low, 0.0, sq), axis=-1, keepdims=True)
            rs = jnp.where(low, lax.rsqrt(ss_lo / SW_DH + EPS), lax.rsqrt(ss_hi / SW_DH + EPS))
            slab = slab * rs * gq
        elif c0 + j * LANES < k_hi:
            slab = slab * lax.rsqrt(jnp.mean(slab * slab, axis=-1, keepdims=True) + EPS) * gk
        slabs.append(slab)
    return jnp.concatenate(slabs, axis=-1)


def _inproj_kernel(*refs, n_main, chunk, q_cols, q_scale, gate_cols, qk_norm, with_ml,
                   with_combine):
    refs = list(refs)
    n_in = (5 + (4 if with_combine else 0) + (2 if with_ml else 0)
            + (2 if qk_norm is not None else 0))
    n_scratch = 4 if with_ml else 2
    ins, outs, scratch = refs[:n_in], refs[n_in:-n_scratch], refs[-n_scratch:]
    x_ref = ins.pop(0)
    if with_combine:
        y1_ref, y2_ref, info_ref, gtp_ref = ins[:4]
        ins = ins[4:]
        xo_ref = outs.pop(0)
    g_ref, sc_ref, sh_ref, w_ref = ins[:4]
    o_ref = outs[0]
    if with_ml:
        wk_ref, wgh_ref = ins[4:]
        kt_ref, gt_ref = outs[1:]
    if qk_norm is not None:
        gq_ref, gk_ref = ins[4:]

    def normalise(hb_dst, lo_dst):
        x = x_ref[...]
        if with_combine:
            x = _moe_combined(x, y1_ref, y2_ref, info_ref, gtp_ref)
            xo_ref[...] = x
        hn = _modulated_norm(x, g_ref[...], sc_ref[...], sh_ref[...])
        hb = hn.astype(jnp.bfloat16)
        hb_dst[...] = hb
        if with_ml:
            lo_dst[...] = (hn - hb.astype(jnp.float32)).astype(jnp.bfloat16)

    def project(hb_src, lo_src):
        hb = hb_src[...]
        for c0 in range(0, n_main, chunk):
            c1 = min(c0 + chunk, n_main)
            acc = _bdot(hb, w_ref[:, c0:c1])
            if q_cols is not None and q_cols[0] <= c0 < q_cols[1]:
                acc = acc * q_scale
            if gate_cols is not None and gate_cols[0] <= c0 < gate_cols[1]:
                acc = jax.nn.sigmoid(acc)
            if qk_norm is not None and c0 < qk_norm[1]:
                acc = _qk_head_norm(acc, c0, qk_norm, gq_ref[...], gk_ref[...])
            o_ref[:, c0:c1] = acc.astype(o_ref.dtype)
        if with_ml:
            nk = kt_ref.shape[0]
            stacked = _bdot_nt(wk_ref[...], hb)
            kt_ref[...] = stacked[:nk].astype(kt_ref.dtype)
            gt_ref[...] = (stacked[nk:nk + ML_GATE_ROWS]
                           + (_bdot_nt(wgh_ref[...], lo_src[...]) + stacked[nk + ML_GATE_ROWS:]))

    hb_a, hb_b = scratch[:2]
    lo_a, lo_b = scratch[2:] if with_ml else (None, None)
    s = pl.program_id(0)

    @pl.when(s == 0)
    def _():
        hb_b[...] = jnp.zeros_like(hb_b)
        if with_ml:
            lo_b[...] = jnp.zeros_like(lo_b)

    @pl.when(s % 2 == 0)
    def _():
        normalise(hb_a, lo_a)
        project(hb_b, lo_b)

    @pl.when(s % 2 == 1)
    def _():
        normalise(hb_b, lo_b)
        project(hb_a, lo_a)


def _inproj(x2d, g, sc, sh, w_main, seq, *, ml_extra=None, q_cols=None, q_scale=1.0,
            gate_cols=None, qk_norm=None, pending=None):
    t, d = x2d.shape
    tm = TOKEN_TILE
    n_main = w_main.shape[1]
    tiles_per_batch = seq // tm
    n_tiles = t // tm
    norm_tile = lambda s: jnp.minimum(s, n_tiles - 1)
    proj_tile = lambda s: jnp.maximum(s - 1, 0)
    row = lambda s: (norm_tile(s), 0)
    per_batch = lambda s: (norm_tile(s) // tiles_per_batch, 0, 0)
    const = lambda s: (0, 0)
    in_specs = [pl.BlockSpec((tm, d), row)]
    args = [x2d]
    out_specs, out_shape = [], []
    if pending is not None:
        y_pairs, info, gt_prev = pending
        in_specs += [pl.BlockSpec((tm, d // 2), row),
                     pl.BlockSpec((tm, d // 2), lambda s: (norm_tile(s) + n_tiles, 0)),
                     pl.BlockSpec((tm, ROUTER_LANES), row),
                     pl.BlockSpec((None, 1, d), per_batch)]
        args += [y_pairs, y_pairs, info, gt_prev]
        out_specs += [pl.BlockSpec((tm, d), row)]
        out_shape += [jax.ShapeDtypeStruct((t, d), jnp.float32)]
    in_specs += [
        pl.BlockSpec((1, d), const),
        pl.BlockSpec((None, 1, d), per_batch),
        pl.BlockSpec((None, 1, d), per_batch),
        pl.BlockSpec((d, n_main), const),
    ]
    args += [g.reshape(1, d), sc, sh, w_main]
    out_specs += [pl.BlockSpec((tm, n_main), lambda s: (proj_tile(s), 0))]
    out_shape += [jax.ShapeDtypeStruct((t, n_main), jnp.bfloat16)]
    scratch = [pltpu.VMEM((tm, d), jnp.bfloat16), pltpu.VMEM((tm, d), jnp.bfloat16)]
    if ml_extra is not None:
        wk_t, wg_hi, wg_lo = ml_extra
        nk = wk_t.shape[0]
        stacked = jnp.concatenate([wk_t, wg_hi, wg_lo], axis=0)
        in_specs += [pl.BlockSpec(stacked.shape, const),
                     pl.BlockSpec(wg_hi.shape, const)]
        args += [stacked, wg_hi]
        out_specs += [pl.BlockSpec((nk, tm), lambda s: (0, proj_tile(s))),
                      pl.BlockSpec((ML_GATE_ROWS, tm), lambda s: (0, proj_tile(s)))]
        out_shape += [jax.ShapeDtypeStruct((nk, t), jnp.bfloat16),
                      jax.ShapeDtypeStruct((ML_GATE_ROWS, t), jnp.float32)]
        scratch += [pltpu.VMEM((tm, d), jnp.bfloat16), pltpu.VMEM((tm, d), jnp.bfloat16)]
    if qk_norm is not None:
        in_specs += [pl.BlockSpec((1, LANES), const), pl.BlockSpec((1, LANES), const)]
        args += [qk_norm[2], qk_norm[3]]
    kern = functools.partial(_inproj_kernel, n_main=n_main, chunk=PROJ_COL_CHUNK, q_cols=q_cols,
                             q_scale=q_scale, gate_cols=gate_cols,
                             qk_norm=None if qk_norm is None else qk_norm[:2],
                             with_ml=ml_extra is not None, with_combine=pending is not None)
    return pl.pallas_call(
        kern,
        grid=(n_tiles + 1,),
        in_specs=in_specs,
        out_specs=out_specs,
        out_shape=out_shape,
        scratch_shapes=scratch,
        compiler_params=_params(("arbitrary",)),
        name="inproj_ml" if ml_extra is not None else "inproj_sw",
    )(*args)


def _mlstm_gate_terms(graw, bias, upper):
    H = ML_HEADS
    L = graw.shape[1]
    z = graw + bias
    gates = GATE_CAP * jnp.tanh(z / GATE_CAP)
    log_f = jnp.minimum(gates, 0.0) - jnp.log1p(jnp.exp(-jnp.abs(gates)))
    row = lax.broadcasted_iota(jnp.int32, (ML_GATE_ROWS, L), 0)
    lane = lax.broadcasted_iota(jnp.int32, (ML_GATE_ROWS, L), 1)
    is_i = row < H
    slab = jnp.where(is_i, gates, log_f)
    a1 = slab.astype(jnp.bfloat16)
    r1 = slab - a1.astype(jnp.float32)
    a2 = r1.astype(jnp.bfloat16)
    a3 = (r1 - a2.astype(jnp.float32)).astype(jnp.bfloat16)
    cum = _bdot(a1, upper) + (_bdot(a2, upper) + _bdot(a3, upper))
    ib = jnp.where(is_i, gates, cum)
    b = pltpu.roll(ib, ML_GATE_ROWS - H, 0)
    u = ib - b
    cm = u
    shift = 1
    while shift < L:
        cm = jnp.maximum(cm, jnp.where(lane >= shift, pltpu.roll(cm, shift, 1), -jnp.inf))
        shift *= 2
    return b, u, cm


def _mlstm_kernel(v_ref, o_ref, q_ref, kt_ref, gt_ref, gtn_ref, bg_ref, gout_ref, out_ref,
                  c_ref, m_ref, b_ref, u_ref, cm_ref, *, chunk):
    L = chunk
    H, dk, dv = ML_HEADS, ML_DK, ML_DV
    r_idx = lax.broadcasted_iota(jnp.int32, (L, L), 0)
    c_idx = lax.broadcasted_iota(jnp.int32, (L, L), 1)
    upper = jnp.where(r_idx <= c_idx, 1.0, 0.0).astype(jnp.bfloat16)
    causal = r_idx >= c_idx
    row = lax.broadcasted_iota(jnp.int32, (ML_GATE_ROWS, L), 0)
    ones_col = jnp.where(lax.broadcasted_iota(jnp.int32, (L, LANES), 1) == 0, 1.0, 0.0
                         ).astype(jnp.bfloat16)

    @pl.when(pl.program_id(1) == 0)
    def _():
        c_ref[...] = jnp.zeros_like(c_ref)
        m_ref[...] = jnp.zeros_like(m_ref)
        b0, u0, cm0 = _mlstm_gate_terms(gt_ref[...], bg_ref[...], upper)
        b_ref[...] = b0
        u_ref[...] = u0
        cm_ref[...] = cm0

    b16 = b_ref[...]
    u16 = u_ref[...]
    cm16 = cm_ref[...]
    b_n, u_n, cm_n = _mlstm_gate_terms(gtn_ref[...], bg_ref[...], upper)
    b_ref[...] = b_n
    u_ref[...] = u_n
    cm_ref[...] = cm_n

    m_prev = m_ref[:, 0:1]
    z16 = jnp.maximum(m_prev, cm16)
    w_inter16 = jnp.exp(m_prev - z16)
    e_negm16 = jnp.exp(-(b16 + z16))
    z_last = z16[:, L - 1:L]
    w_state16 = jnp.exp(u16 - z_last)
    decay16 = jnp.exp(m_prev - z_last)
    m_ref[...] = jnp.broadcast_to(b16[:, L - 1:L] + z_last, m_ref.shape)
    stacked = jnp.where(row < H, z16,
                        jnp.where(row < 2 * H, pltpu.roll(w_inter16, H, 0),
                                  pltpu.roll(e_negm16, 2 * H, 0)))
    cols = jnp.concatenate(
        [stacked, jnp.zeros((LANES - ML_GATE_ROWS, L), jnp.float32)], axis=0).T

    for h in range(H):
        u_r = u16[h:h + 1, :]
        z_c = cols[:, h:h + 1]
        w_inter = cols[:, H + h:H + h + 1]
        e_negm = cols[:, 2 * H + h:2 * H + h + 1]
        c_ext = c_ref[h]
        q = q_ref[:, h * dk:(h + 1) * dk]
        kt = kt_ref[h * dk:(h + 1) * dk, :]
        v_ext = jnp.concatenate([v_ref[:, h * dv:(h + 1) * dv], ones_col], axis=-1)

        w_intra = jnp.exp(jnp.where(causal, u_r - z_c, -jnp.inf))
        s = (_bdot(q, kt) * w_intra).astype(jnp.bfloat16)
        nd = w_inter * _bdot(q, c_ext.astype(jnp.bfloat16)) + _bdot(s, v_ext)
        den = nd[:, dv:dv + 1]
        hb = nd[:, :dv] * (1.0 / jnp.maximum(jnp.abs(den), e_negm))

        kw = (kt.astype(jnp.float32) * w_state16[h:h + 1, :]).astype(jnp.bfloat16)
        c_ref[h] = decay16[h:h + 1, :] * c_ext + _bdot(kw, v_ext)

        y = hb * lax.rsqrt(jnp.mean(hb * hb, axis=-1, keepdims=True) + EPS)
        y = y * gout_ref[:, h * dv:(h + 1) * dv]
        og = o_ref[:, h * dv:(h + 1) * dv].astype(jnp.float32)
        out_ref[:, h * dv:(h + 1) * dv] = (y * og).astype(out_ref.dtype)


def _mlstm_core(main, k_t, g_t, b_gate, g_out, bsz, seq):
    chunk = ML_CHUNK
    t = main.shape[0]
    nc = seq // chunk
    blk = lambda b, c: b * nc + c
    bg = jnp.zeros((ML_GATE_ROWS, 1), jnp.float32).at[:2 * ML_HEADS, 0].set(b_gate)
    return pl.pallas_call(
        functools.partial(_mlstm_kernel, chunk=chunk),
        grid=(bsz, nc),
        in_specs=[
            pl.BlockSpec((chunk, ML_V), lambda b, c: (blk(b, c), 0)),
            pl.BlockSpec((chunk, ML_V), lambda b, c: (blk(b, c), 1)),
            pl.BlockSpec((chunk, ML_QK), lambda b, c: (blk(b, c), 4)),
            pl.BlockSpec((ML_QK, chunk), lambda b, c: (0, blk(b, c))),
            pl.BlockSpec((ML_GATE_ROWS, chunk), lambda b, c: (0, blk(b, c))),
            pl.BlockSpec((ML_GATE_ROWS, chunk), lambda b, c: (0, blk(b, jnp.minimum(c + 1, nc - 1)))),
            pl.BlockSpec((ML_GATE_ROWS, 1), lambda b, c: (0, 0)),
            pl.BlockSpec((1, ML_V), lambda b, c: (0, 0)),
        ],
        out_specs=pl.BlockSpec((chunk, ML_V), lambda b, c: (blk(b, c), 0)),
        out_shape=jax.ShapeDtypeStruct((t, ML_V), jnp.bfloat16),
        scratch_shapes=[
            pltpu.VMEM((ML_HEADS, ML_DK, ML_DV + LANES), jnp.float32),
            pltpu.VMEM((ML_GATE_ROWS, LANES), jnp.float32),
            pltpu.VMEM((ML_GATE_ROWS, chunk), jnp.float32),
            pltpu.VMEM((ML_GATE_ROWS, chunk), jnp.float32),
            pltpu.VMEM((ML_GATE_ROWS, chunk), jnp.float32),
        ],
        compiler_params=_params(("arbitrary", "arbitrary")),
        name="mlstm_core",
    )(main, main, main, k_t, g_t, g_t, bg, g_out.reshape(1, ML_V))


def _swa_kernel(sinks_ref, q_ref, kc_ref, kp_ref, vc_ref, vp_ref, bias_ref, o_ref):
    W = SW_WINDOW
    lane = lax.broadcasted_iota(jnp.int32, (W, LANES), 1)
    low = lane < SW_DH
    row_col = lax.broadcasted_iota(jnp.int32, (SW_GROUP * W, 1), 0)
    bias = bias_ref[jnp.minimum(pl.program_id(1), 1)]
    for g in range(SW_KV_HEADS):
        sl = slice(g * LANES, (g + 1) * LANES)
        kn = jnp.concatenate([kp_ref[:, sl], kc_ref[:, sl]], axis=0)
        v2 = jnp.concatenate([vp_ref[:, sl], vc_ref[:, sl]], axis=0)
        sink = jnp.full((SW_GROUP * W, 1), sinks_ref[g * SW_GROUP + SW_GROUP - 1], jnp.float32)
        for j in range(SW_GROUP - 2, -1, -1):
            sink = jnp.where(row_col < (j + 1) * W, sinks_ref[g * SW_GROUP + j], sink)
        parts = []
        for p in range(2):
            c0 = g * SW_GROUP * SW_DH + p * LANES
            qp = q_ref[:, c0:c0 + LANES].astype(jnp.float32)
            parts.append(jnp.where(low, qp, 0.0).astype(jnp.bfloat16))
            parts.append(jnp.where(low, 0.0, qp).astype(jnp.bfloat16))
        q4 = jnp.concatenate(parts, axis=0)
        scores = _bdot_nt(q4, kn) + bias
        m = jnp.maximum(jnp.max(scores, axis=-1, keepdims=True), sink)
        pexp = jnp.exp2(scores - m)
        denom = jnp.sum(pexp, axis=-1, keepdims=True) + jnp.exp2(sink - m)
        o4 = _bdot(pexp.astype(jnp.bfloat16), v2) * (1.0 / denom)
        for p in range(2):
            oa = o4[(2 * p) * W:(2 * p + 1) * W]
            ob = o4[(2 * p + 1) * W:(2 * p + 2) * W]
            c0 = g * SW_GROUP * SW_DH + p * LANES
            o_ref[:, c0:c0 + LANES] = jnp.where(low, oa, ob).astype(o_ref.dtype)


def _swa_core(proj, sinks, bsz, seq):
    t = proj.shape[0]
    W = SW_WINDOW
    nb = seq // W
    dq = SW_Q_HEADS * SW_DH
    kv_w = SW_KV_HEADS * LANES
    k_blk = dq // kv_w
    v_blk = k_blk + 1
    cur = lambda b, n, s: b * nb + n
    prev = lambda b, n, s: b * nb + jnp.maximum(n - 1, 0)
    sinks = sinks.astype(jnp.float32) * LOG2E
    qi = (jnp.arange(SW_GROUP * W) % W)[:, None]
    ki = jnp.arange(2 * W)[None, :]
    rel = qi + W - ki
    in_win = (rel >= 0) & (rel < W)
    bias = jnp.stack([jnp.where(in_win & (ki >= W), 0.0, -jnp.inf),
                      jnp.where(in_win, 0.0, -jnp.inf)]).astype(jnp.float32)
    grid_spec = pltpu.PrefetchScalarGridSpec(
        num_scalar_prefetch=1,
        grid=(bsz, nb),
        in_specs=[
            pl.BlockSpec((W, dq), lambda b, n, s: (cur(b, n, s), 0)),
            pl.BlockSpec((W, kv_w), lambda b, n, s: (cur(b, n, s), k_blk)),
            pl.BlockSpec((W, kv_w), lambda b, n, s: (prev(b, n, s), k_blk)),
            pl.BlockSpec((W, kv_w), lambda b, n, s: (cur(b, n, s), v_blk)),
            pl.BlockSpec((W, kv_w), lambda b, n, s: (prev(b, n, s), v_blk)),
            pl.BlockSpec((2, SW_GROUP * W, 2 * W), lambda b, n, s: (0, 0, 0)),
        ],
        out_specs=pl.BlockSpec((W, dq), lambda b, n, s: (cur(b, n, s), 0)),
    )
    return pl.pallas_call(
        _swa_kernel,
        grid_spec=grid_spec,
        out_shape=jax.ShapeDtypeStruct((t, dq), jnp.bfloat16),
        compiler_params=_params(("arbitrary", "arbitrary")),
        name="swa_core",
    )(sinks, proj, proj, proj, proj, proj, bias)


def _pack_rounded_pairs(r):
    k = r.shape[1] // 2
    hi = lax.bitcast_convert_type(r[:, :k], jnp.uint32)
    lo = lax.bitcast_convert_type(r[:, k:], jnp.uint32)
    return hi | (lo >> 16)


def _pack_bf16_pairs(a):
    return _pack_rounded_pairs(a.astype(jnp.bfloat16).astype(jnp.float32))


def _unpack_bf16_pairs(u):
    hi = lax.bitcast_convert_type(u & jnp.uint32(0xFFFF0000), jnp.float32)
    lo = lax.bitcast_convert_type(u << 16, jnp.float32)
    return hi, lo


def _route_tile(x_new, g_ref, sc_ref, sh_ref, wh_ref, wl_ref, b_ref,
                hn_ref, info_ref, infot_ref, cnt_ref, carry_ref, earlier_ref, tm):
    hn = _modulated_norm(x_new, g_ref[...], sc_ref[...], sh_ref[...])
    h_hi = hn.astype(jnp.bfloat16)
    hi_f32 = h_hi.astype(jnp.float32)
    h_lo = (hn - hi_f32).astype(jnp.bfloat16)
    hn_ref[...] = _pack_rounded_pairs(hi_f32)
    wide = _bdot_nt(wl_ref[...], h_hi)
    logits = (wide[:ROUTER_LANES] + (_bdot_nt(wh_ref[...], h_lo) + wide[ROUTER_LANES:])
              + b_ref[...])
    E8 = EXPERTS_PER_GROUP
    sub = lax.broadcasted_iota(jnp.int32, (E8, tm), 0).astype(jnp.float32)
    big = float(ROUTER_LANES)
    neg = -jnp.inf

    gl = logits[N_EXPERTS:N_EXPERTS + N_GROUPS]
    gmax = jnp.max(gl, axis=0, keepdims=True)
    gsel = jnp.min(jnp.where(gl == gmax, sub, big), axis=0, keepdims=True)
    p_grp = 1.0 / jnp.sum(jnp.exp(gl - gmax), axis=0, keepdims=True)

    el = logits[0:E8]
    for grp in range(1, N_GROUPS):
        el = jnp.where(gsel == grp, logits[grp * E8:(grp + 1) * E8], el)
    v1 = jnp.max(el, axis=0, keepdims=True)
    j1 = jnp.min(jnp.where(el == v1, sub, big), axis=0, keepdims=True)
    el2 = jnp.where(sub == j1, neg, el)
    v2 = jnp.max(el2, axis=0, keepdims=True)
    j2 = jnp.min(jnp.where(el2 == v2, sub, big), axis=0, keepdims=True)
    i1 = gsel * E8 + j1
    i2 = gsel * E8 + j2
    e21 = jnp.exp(v2 - v1)
    gate1 = p_grp / (1.0 + e21)
    gate2 = p_grp * e21 / (1.0 + e21)

    erow = lax.broadcasted_iota(jnp.int32, (N_EXPERTS, tm), 0).astype(jnp.float32)
    hit1 = erow == i1
    hit2 = erow == i2
    onehot = jnp.where(hit1 | hit2, 1.0, 0.0)
    carry = carry_ref[:, 0:1]
    before = _bdot(onehot.astype(jnp.bfloat16), earlier_ref[...]) + carry
    rank1 = jnp.sum(jnp.where(hit1, before, 0.0), axis=0, keepdims=True)
    rank2 = jnp.sum(jnp.where(hit2, before, 0.0), axis=0, keepdims=True)
    total = carry + jnp.sum(onehot, axis=1, keepdims=True)
    carry_ref[...] = jnp.broadcast_to(total, carry_ref.shape)
    cnt_ref[...] = jnp.broadcast_to(total, cnt_ref.shape)

    info_t = jnp.where(sub == 0, i1, 0.0)
    info_t = jnp.where(sub == 1, i2, info_t)
    info_t = jnp.where(sub == 2, rank1, info_t)
    info_t = jnp.where(sub == 3, rank2, info_t)
    info_t = jnp.where(sub == 4, gate1, info_t)
    info_t = jnp.where(sub == 5, gate2, info_t)
    infot_ref[...] = info_t
    info_ref[...] = jnp.concatenate(
        [info_t, jnp.zeros((ROUTER_LANES - E8, tm), jnp.float32)], axis=0).T


def _router_kernel(a_ref, wo_ref, x_ref, gt_ref, g_ref, sc_ref, sh_ref, wh_ref, wl_ref, b_ref,
                   xo_ref, hn_ref, info_ref, infot_ref, cnt_ref, carry_ref, earlier_ref, *, tm):
    @pl.when(pl.program_id(0) == 0)
    def _():
        carry_ref[...] = jnp.zeros_like(carry_ref)
        r_idx = lax.broadcasted_iota(jnp.int32, (tm, tm), 0)
        c_idx = lax.broadcasted_iota(jnp.int32, (tm, tm), 1)
        earlier_ref[...] = jnp.where(r_idx < c_idx, 1.0, 0.0).astype(jnp.bfloat16)

    x_new = x_ref[...] + gt_ref[...] * _bdot(a_ref[...], wo_ref[...])
    xo_ref[...] = x_new
    _route_tile(x_new, g_ref, sc_ref, sh_ref, wh_ref, wl_ref, b_ref,
                hn_ref, info_ref, infot_ref, cnt_ref, carry_ref, earlier_ref, tm)


def _outproj_router(a, w_out, x2d, gt, g, sc, sh, w_hi, w_lo, bias, seq):
    t, d = x2d.shape
    tm = TOKEN_TILE
    tiles_per_batch = seq // tm
    per_batch = lambda i: (i // tiles_per_batch, 0, 0)
    const = lambda i: (0, 0)
    return pl.pallas_call(
        functools.partial(_router_kernel, tm=tm),
        grid=(t // tm,),
        in_specs=[
            pl.BlockSpec((tm, a.shape[1]), lambda i: (i, 0)),
            pl.BlockSpec(w_out.shape, const),
            pl.BlockSpec((tm, d), lambda i: (i, 0)),
            pl.BlockSpec((None, 1, d), per_batch),
            pl.BlockSpec((1, d), const),
            pl.BlockSpec((None, 1, d), per_batch),
            pl.BlockSpec((None, 1, d), per_batch),
            pl.BlockSpec((ROUTER_LANES, d), const),
            pl.BlockSpec((2 * ROUTER_LANES, d), const),
            pl.BlockSpec((ROUTER_LANES, 1), const),
        ],
        out_specs=[
            pl.BlockSpec((tm, d), lambda i: (i, 0)),
            pl.BlockSpec((tm, d // 2), lambda i: (i, 0)),
            pl.BlockSpec((tm, ROUTER_LANES), lambda i: (i, 0)),
            pl.BlockSpec((EXPERTS_PER_GROUP, tm), lambda i: (0, i)),
            pl.BlockSpec((N_EXPERTS, LANES), const),
        ],
        out_shape=[
            jax.ShapeDtypeStruct((t, d), jnp.float32),
            jax.ShapeDtypeStruct((t, d // 2), jnp.uint32),
            jax.ShapeDtypeStruct((t, ROUTER_LANES), jnp.float32),
            jax.ShapeDtypeStruct((EXPERTS_PER_GROUP, t), jnp.float32),
            jax.ShapeDtypeStruct((N_EXPERTS, LANES), jnp.float32),
        ],
        scratch_shapes=[pltpu.VMEM((N_EXPERTS, LANES), jnp.float32),
                        pltpu.VMEM((tm, tm), jnp.bfloat16)],
        compiler_params=_params(("arbitrary",)),
        name="outproj_router",
    )(a, w_out, x2d, gt, g.reshape(1, d), sc, sh, w_hi, jnp.concatenate([w_hi, w_lo], axis=0), bias)


def _sc_mesh():
    return plsc.VectorSubcoreMesh(core_axis_name="c", subcore_axis_name="s")


def _sc_dispatch(rows, d0, d1, pad_idx, n_slots):
    t, w = rows.shape
    n_pad = pad_idx.shape[1]
    zeros = jnp.zeros((SC_ROWS, w), rows.dtype)
    sem = (pltpu.PARALLEL, pltpu.ARBITRARY)
    parts = SC_WINDOW // SC_ROWS

    @pl.kernel(out_type=jax.ShapeDtypeStruct((n_slots + SC_WINDOW, w), rows.dtype), mesh=_sc_mesh(),
               scratch_types=[pltpu.SemaphoreType.DMA, pltpu.SemaphoreType.DMA])
    def dispatch(x_hbm, d0_hbm, d1_hbm, z_hbm, p_hbm, o_hbm, sem0, sem1):
        def scatter_rows(x_vmem, i0_vmem, i1_vmem):
            part = pl.ds(pl.program_id(1) * SC_ROWS, SC_ROWS)
            first = pltpu.async_copy(x_vmem, o_hbm.at[i0_vmem.at[0, part]], sem0)
            second = pltpu.async_copy(x_vmem, o_hbm.at[i1_vmem.at[0, part]], sem1)
            first.wait()
            second.wait()

        pltpu.emit_pipeline(
            scatter_rows,
            grid=(t // SC_WINDOW, parts),
            in_specs=[pl.BlockSpec((SC_ROWS, w), lambda i, j: (parts * i + j, 0)),
                      pl.BlockSpec((1, SC_WINDOW), lambda i, j: (0, i)),
                      pl.BlockSpec((1, SC_WINDOW), lambda i, j: (0, i))],
            out_specs=[],
            core_axis_name=("c", "s"),
            dimension_semantics=sem,
        )(x_hbm, d0_hbm, d1_hbm)

        def scatter_zeros(z_vmem, p_vmem):
            part = pl.ds(pl.program_id(1) * SC_ROWS, SC_ROWS)
            pltpu.sync_copy(z_vmem, o_hbm.at[p_vmem.at[0, part]])

        pltpu.emit_pipeline(
            scatter_zeros,
            grid=(n_pad // SC_WINDOW, parts),
            in_specs=[pl.BlockSpec((SC_ROWS, w), lambda i, j: (0, 0)),
                      pl.BlockSpec((1, SC_WINDOW), lambda i, j: (0, i))],
            out_specs=[],
            core_axis_name=("c", "s"),
            dimension_semantics=sem,
        )(z_hbm, p_hbm)

    return dispatch(rows, d0, d1, zeros, pad_idx)


def _sc_gather(src, idx):
    n_out = idx.shape[1]
    w = src.shape[1]
    parts = SC_WINDOW // SC_ROWS

    @pl.kernel(out_type=jax.ShapeDtypeStruct((n_out, w), src.dtype), mesh=_sc_mesh())
    def gather(x_hbm, i_hbm, o_hbm):
        def gather_rows(i_vmem, o_vmem):
            part = pl.ds(pl.program_id(1) * SC_ROWS, SC_ROWS)
            pltpu.sync_copy(x_hbm.at[i_vmem.at[0, part]], o_vmem)

        pltpu.emit_pipeline(
            gather_rows,
            grid=(n_out // SC_WINDOW, parts),
            in_specs=[pl.BlockSpec((1, SC_WINDOW), lambda i, j: (0, i))],
            out_specs=[pl.BlockSpec((SC_ROWS, w), lambda i, j: (parts * i + j, 0))],
            core_axis_name=("c", "s"),
            dimension_semantics=(pltpu.PARALLEL, pltpu.ARBITRARY),
        )(i_hbm, o_hbm)

    return gather(src, idx)


def _expert_kernel(meta_ref, x_ref, w1_hbm, w3_hbm, w2_hbm, y_ref,
                   w1_buf, w3_buf, w2_buf, w1_c, w3_c, w2_c, sems, *, layer, nb):
    def weight_copies(expert, s):
        return (pltpu.make_async_copy(w1_hbm.at[layer, expert], w1_buf.at[s], sems.at[s, 0]),
                pltpu.make_async_copy(w3_hbm.at[layer, expert], w3_buf.at[s], sems.at[s, 1]),
                pltpu.make_async_copy(w2_hbm.at[layer, expert], w2_buf.at[s], sems.at[s, 2]))

    @pl.when(pl.program_id(0) == 0)
    def _():
        for cp in weight_copies(meta_ref[0], meta_ref[nb]):
            cp.start()

    for j in range(EXPERT_BLOCKS_PER_STEP):
        i = pl.program_id(0) * EXPERT_BLOCKS_PER_STEP + j
        rows = slice(j * MOE_BLOCK, (j + 1) * MOE_BLOCK)
        e = meta_ref[i]
        slot = meta_ref[nb + i]
        nxt = meta_ref[2 * nb + i]
        used = i < meta_ref[3 * nb]
        first = used & ((i == 0) | (e != meta_ref[jnp.maximum(i - 1, 0)]))

        @pl.when(first)
        def _():
            for cp in weight_copies(e, slot):
                cp.wait()

            @pl.when(nxt >= 0)
            def _():
                for cp in weight_copies(nxt, 1 - slot):
                    cp.start()

            w1_c[...] = w1_buf[slot].astype(jnp.bfloat16)
            w3_c[...] = w3_buf[slot].astype(jnp.bfloat16)
            w2_c[...] = w2_buf[slot].astype(jnp.bfloat16)

        @pl.when(used)
        def _():
            x_hi, x_lo = _unpack_bf16_pairs(x_ref[rows, :])
            xb = jnp.concatenate([x_hi, x_lo], axis=-1).astype(jnp.bfloat16)
            h1 = _bdot(xb, w1_c[...])
            h3 = _bdot(xb, w3_c[...])
            act = (h1 * jax.nn.sigmoid(h1) * h3).astype(jnp.bfloat16)
            y_ref[rows, :] = _pack_bf16_pairs(_bdot(act, w2_c[...]))

        @pl.when(jnp.logical_not(used))
        def _():
            y_ref[rows, :] = jnp.zeros((MOE_BLOCK, y_ref.shape[1]), y_ref.dtype)


def _experts(x_slots, blk_meta, w1, w3, w2, layer, nb):
    dp = x_slots.shape[1]
    d, de = w1.shape[-2:]
    step_rows = EXPERT_BLOCKS_PER_STEP * MOE_BLOCK
    assert nb % EXPERT_BLOCKS_PER_STEP == 0
    last_used_step = lambda s: (s[3 * nb] - 1) // EXPERT_BLOCKS_PER_STEP
    grid_spec = pltpu.PrefetchScalarGridSpec(
        num_scalar_prefetch=1,
        grid=(nb // EXPERT_BLOCKS_PER_STEP,),
        in_specs=[
            pl.BlockSpec((step_rows, dp), lambda i, s: (jnp.minimum(i, last_used_step(s)), 0)),
            pl.BlockSpec(memory_space=pl.ANY),
            pl.BlockSpec(memory_space=pl.ANY),
            pl.BlockSpec(memory_space=pl.ANY),
        ],
        out_specs=pl.BlockSpec((step_rows, dp), lambda i, s: (i, 0)),
        scratch_shapes=[
            pltpu.VMEM((2, d, de), jnp.float32),
            pltpu.VMEM((2, d, de), jnp.float32),
            pltpu.VMEM((2, de, d), jnp.float32),
            pltpu.VMEM((d, de), jnp.bfloat16),
            pltpu.VMEM((d, de), jnp.bfloat16),
            pltpu.VMEM((de, d), jnp.bfloat16),
            pltpu.SemaphoreType.DMA((2, 3)),
        ],
    )
    return pl.pallas_call(
        functools.partial(_expert_kernel, layer=layer, nb=nb),
        grid_spec=grid_spec,
        out_shape=jax.ShapeDtypeStruct((nb * MOE_BLOCK, dp), jnp.uint32),
        compiler_params=_params(("arbitrary",)),
        name="moe_experts",
    )(blk_meta, x_slots, w1, w3, w2)


def _combine_kernel(x_ref, y1_ref, y2_ref, info_ref, gt_ref, o_ref):
    o_ref[...] = _moe_combined(x_ref[...], y1_ref, y2_ref, info_ref, gt_ref)


def _combine(x2d, y_pairs, info, gt, seq):
    t, d = x2d.shape
    tm = min(COMBINE_TILE, seq)
    tiles_per_batch = seq // tm
    second = t // tm
    return pl.pallas_call(
        _combine_kernel,
        grid=(t // tm,),
        in_specs=[
            pl.BlockSpec((tm, d), lambda i: (i, 0)),
            pl.BlockSpec((tm, d // 2), lambda i: (i, 0)),
            pl.BlockSpec((tm, d // 2), lambda i: (i + second, 0)),
            pl.BlockSpec((tm, ROUTER_LANES), lambda i: (i, 0)),
            pl.BlockSpec((None, 1, d), lambda i: (i // tiles_per_batch, 0, 0)),
        ],
        out_specs=pl.BlockSpec((tm, d), lambda i: (i, 0)),
        out_shape=jax.ShapeDtypeStruct((t, d), jnp.float32),
        compiler_params=_params(("arbitrary",)),
        name="moe_combine",
    )(x2d, y_pairs, y_pairs, info, gt)


def _slot_plan(info_t, cnt, t):
    counts = cnt[:, 0].astype(jnp.int32)
    padded = (counts + MOE_BLOCK - 1) // MOE_BLOCK * MOE_BLOCK
    pad_ends = jnp.cumsum(padded)
    pad_starts = pad_ends - padded
    nb = -(-(2 * t) // MOE_BLOCK) + N_EXPERTS
    n_slots = nb * MOE_BLOCK
    it = info_t.astype(jnp.int32)
    onehot_start = lambda e: jnp.sum(
        jnp.where(e[None, :] == jnp.arange(N_EXPERTS, dtype=jnp.int32)[:, None],
                  pad_starts[:, None], 0), axis=0)
    dest1 = (onehot_start(it[0]) + it[2]).reshape(1, t)
    dest2 = (onehot_start(it[1]) + it[3]).reshape(1, t)
    lane = jnp.arange(MOE_BLOCK, dtype=jnp.int32)[None, :]
    n_padding = (padded - counts)[:, None]
    wrapped = (pad_starts + counts)[:, None] + lane % jnp.maximum(n_padding, 1)
    pad_idx = jnp.where(n_padding > 0, wrapped, n_slots + lane % SC_WINDOW).reshape(-1)
    n_real = pad_ends[-1] // MOE_BLOCK
    n_used = -(-n_real // EXPERT_BLOCKS_PER_STEP) * EXPERT_BLOCKS_PER_STEP
    tail = jnp.arange((EXPERT_BLOCKS_PER_STEP - 1) * MOE_BLOCK, dtype=jnp.int32)
    tail_idx = jnp.where(tail < (n_used - n_real) * MOE_BLOCK, pad_ends[-1] + tail,
                         n_slots + tail % SC_WINDOW)
    pad_idx = jnp.concatenate([pad_idx, tail_idx]).reshape(1, -1)
    experts = jnp.arange(N_EXPERTS, dtype=jnp.int32)
    blk = jnp.arange(nb, dtype=jnp.int32)
    blk_exp = jnp.minimum(
        jnp.sum((pad_ends[None, :] <= (blk * MOE_BLOCK)[:, None]).astype(jnp.int32), axis=1),
        N_EXPERTS - 1)
    blk_exp = jnp.where(blk >= n_real, jnp.max(jnp.where(padded > 0, experts, 0)), blk_exp)
    prev_exp = jnp.concatenate([jnp.full((1,), -1, jnp.int32), blk_exp[:-1]])
    is_first = (blk < n_used) & (blk_exp != prev_exp)
    blk_slot = (jnp.cumsum(is_first.astype(jnp.int32)) + 1) % 2
    later = (experts[None, :] > experts[:, None]) & (padded[None, :] > 0)
    nxt_of = jnp.min(jnp.where(later, experts[None, :], N_EXPERTS), axis=1)
    nxt_of = jnp.where(nxt_of == N_EXPERTS, -1, nxt_of)
    blk_nxt = jnp.sum(jnp.where(blk_exp[:, None] == experts[None, :], nxt_of[None, :], 0), axis=1)
    blk_meta = jnp.concatenate([blk_exp, blk_slot, blk_nxt, n_used[None]]).astype(jnp.int32)
    return dest1, dest2, pad_idx, blk_meta, nb, n_slots


def _mixer_out_and_moe(a, w_out, x2d, gt1, g, sc, sh, w_group, b_group, w_router, b_router,
                       w1, w3, w2, layer, seq):
    t, d = x2d.shape
    w_cat = jnp.zeros((ROUTER_LANES, d), jnp.float32)
    w_cat = w_cat.at[:N_EXPERTS].set(w_router.T).at[N_EXPERTS:N_EXPERTS + N_GROUPS].set(w_group.T)
    b_cat = jnp.zeros((ROUTER_LANES, 1), jnp.float32)
    b_cat = b_cat.at[:N_EXPERTS, 0].set(b_router).at[N_EXPERTS:N_EXPERTS + N_GROUPS, 0].set(b_group)
    w_hi, w_lo = _split_hi_lo(w_cat)
    x_new, hn, info, info_t, cnt = _outproj_router(a, w_out, x2d, gt1, g, sc, sh, w_hi, w_lo,
                                                   b_cat, seq)
    dest1, dest2, pad_idx, blk_meta, nb, n_slots = _slot_plan(info_t, cnt, t)
    x_slots = _sc_dispatch(hn, dest1, dest2, pad_idx, n_slots)
    y_slots = _experts(x_slots, blk_meta, w1, w3, w2, layer, nb)
    y_pairs = _sc_gather(y_slots, jnp.concatenate([dest1, dest2], axis=1))
    return x_new, y_pairs, info


def kernel(x, c, w_ada, b_ada, norm1_g, norm2_g, ml_w_in, ml_b_gate, ml_g_out, ml_w_out,
           sw_w_in, sw_g_q, sw_g_k, sw_sinks, sw_w_out, moe_w_group, moe_b_group,
           moe_w_router, moe_b_router, moe_w1, moe_w3, moe_w2):
    bsz, seq, d = x.shape
    depth = w_ada.shape[0]
    bf = jnp.bfloat16
    mod = _ada_mod(c, w_ada, b_ada)
    x2d = x.reshape(bsz * seq, d)
    pending = None
    for layer in range(depth):
        sh1, sc1, gt1, sh2, sc2, gt2 = [
            mod[layer, :, i * d:(i + 1) * d].reshape(bsz, 1, d) for i in range(6)]
        j = layer // 2
        if layer % 2 == 0:
            w = ml_w_in[j]
            q_w, k_w = w[:, :ML_QK], w[:, ML_QK:2 * ML_QK]
            v_w = w[:, 2 * ML_QK:2 * ML_QK + ML_V]
            o_w = w[:, 2 * ML_QK + ML_V:2 * ML_QK + 2 * ML_V]
            g_w = w[:, 2 * ML_QK + 2 * ML_V:]
            w_main = jnp.concatenate([v_w, o_w, q_w], axis=1).astype(bf)
            wg_t = jnp.zeros((ML_GATE_ROWS, d), jnp.float32).at[:2 * ML_HEADS].set(g_w.T)
            wg_hi, wg_lo = _split_hi_lo(wg_t)
            wk_t = k_w.T.astype(bf)
            w_out = ml_w_out[j].astype(bf)
            outs = _inproj(x2d, norm1_g[layer], sc1, sh1, w_main, seq,
                           ml_extra=(wk_t, wg_hi, wg_lo),
                           q_cols=(2 * ML_V, 2 * ML_V + ML_QK), q_scale=ML_DK ** -0.5,
                           gate_cols=(ML_V, 2 * ML_V), pending=pending)
            if pending is not None:
                x2d, outs = outs[0], outs[1:]
            main, k_t, g_t = outs
            a = _mlstm_core(main, k_t, g_t, ml_b_gate[j], ml_g_out[j], bsz, seq)
        else:
            w = sw_w_in[j]
            dq = SW_Q_HEADS * SW_DH
            dkv = SW_KV_HEADS * SW_DH
            dup = lambda m: jnp.concatenate(
                [m.reshape(d, SW_KV_HEADS, 1, SW_DH)] * 2, axis=2).reshape(d, 2 * dkv)
            w_main = jnp.concatenate(
                [w[:, :dq], dup(w[:, dq:dq + dkv]), dup(w[:, dq + dkv:])], axis=1).astype(bf)
            w_out = sw_w_out[j].astype(bf)
            gq = jnp.concatenate([sw_g_q[j], sw_g_q[j]]).reshape(1, LANES) * (SW_DH ** -0.5 * LOG2E)
            gk = jnp.concatenate([sw_g_k[j], sw_g_k[j]]).reshape(1, LANES)
            outs = _inproj(x2d, norm1_g[layer], sc1, sh1, w_main, seq, pending=pending,
                           qk_norm=(dq, dq + 2 * dkv, gq, gk))
            if pending is not None:
                x2d, outs = outs[0], outs[1:]
            a = _swa_core(outs[0], sw_sinks[j], bsz, seq)
        x2d, y_pairs, info = _mixer_out_and_moe(
            a, w_out, x2d, gt1, norm2_g[layer], sc2, sh2, moe_w_group[layer], moe_b_group[layer],
            moe_w_router[layer], moe_b_router[layer], moe_w1, moe_w3, moe_w2, layer, seq)
        pending = (y_pairs, info, gt2)
    y_pairs, info, gt2 = pending
    return _combine(x2d, y_pairs, info, gt2, seq).reshape(bsz, seq, d)
```

```python
import functools

import jax
import jax.numpy as jnp
from jax import lax
from jax.experimental import pallas as pl
from jax.experimental.pallas import tpu as pltpu
from jax.experimental.pallas import tpu_sc as plsc

EPS = 1e-6
GATE_CAP = 15.0
LOG2E = 1.4426950408889634

ML_HEADS = 4
ML_DK = 128
ML_DV = 256
ML_QK = ML_HEADS * ML_DK
ML_V = ML_HEADS * ML_DV
ML_GATE_ROWS = 16

SW_Q_HEADS = 16
SW_KV_HEADS = 4
SW_GROUP = SW_Q_HEADS // SW_KV_HEADS
SW_DH = 64
SW_WINDOW = 128
LANES = 128

N_GROUPS = 8
EXPERTS_PER_GROUP = 8
N_EXPERTS = N_GROUPS * EXPERTS_PER_GROUP
MOE_BLOCK = 256
ROUTER_LANES = 128
SC_WINDOW = 128
SC_ROWS = 64
EXPERT_BLOCKS_PER_STEP = 4

SUBLANES = 8
TOKEN_TILE = 512
COMBINE_TILE = 1024
PROJ_COL_CHUNK = 1024
ML_CHUNK = 256
ADA_COL_TILE = 768
V7X_VMEM_BYTES = 64 * 1024 * 1024
VMEM_LIMIT = V7X_VMEM_BYTES - 8 * 1024 * 1024

_NT = (((1,), (1,)), ((), ()))


def _bdot(a, b):
    return jnp.dot(a, b, preferred_element_type=jnp.float32)


def _bdot_nt(a, b):
    return lax.dot_general(a, b, _NT, preferred_element_type=jnp.float32)


def _split_hi_lo(a):
    hi = a.astype(jnp.bfloat16)
    lo = (a - hi.astype(jnp.float32)).astype(jnp.bfloat16)
    return hi, lo


def _params(sem):
    return pltpu.CompilerParams(dimension_semantics=sem, vmem_limit_bytes=VMEM_LIMIT)


def _ada_kernel(c_ref, w_ref, b_ref, o_ref):
    c = c_ref[...]
    cond = c * jax.nn.sigmoid(c)
    c_hi, c_lo = _split_hi_lo(cond)
    w_hi, w_lo = _split_hi_lo(w_ref[...])
    acc = _bdot(c_hi, w_hi) + (_bdot(c_lo, w_hi) + _bdot(c_hi, w_lo))
    o_ref[...] = acc + b_ref[...]


def _ada_mod(c, w_ada, b_ada):
    depth, d, n = w_ada.shape
    bsz = c.shape[0]
    rows = SUBLANES
    tn = ADA_COL_TILE
    c_pad = jnp.zeros((rows, d), jnp.float32).at[:bsz].set(c)
    out = pl.pallas_call(
        _ada_kernel,
        grid=(depth, n // tn),
        in_specs=[
            pl.BlockSpec((rows, d), lambda l, j: (0, 0)),
            pl.BlockSpec((None, d, tn), lambda l, j: (l, 0, j)),
            pl.BlockSpec((None, 1, tn), lambda l, j: (l, 0, j)),
        ],
        out_specs=pl.BlockSpec((None, rows, tn), lambda l, j: (l, 0, j)),
        out_shape=jax.ShapeDtypeStruct((depth, rows, n), jnp.float32),
        compiler_params=_params(("arbitrary", "arbitrary")),
        name="ada_mod",
    )(c_pad, w_ada, b_ada.reshape(depth, 1, n))
    return out[:, :bsz]


def _modulated_norm(x, g, sc, sh):
    y = x * lax.rsqrt(jnp.mean(x * x, axis=-1, keepdims=True) + EPS)
    return y * (g * (1.0 + sc)) + sh


def _moe_combined(x, y1_ref, y2_ref, info_ref, gt_ref):
    info = info_ref[...]
    g1 = info[:, 4:5]
    g2 = info[:, 5:6]
    y1_hi, y1_lo = _unpack_bf16_pairs(y1_ref[...])
    y2_hi, y2_lo = _unpack_bf16_pairs(y2_ref[...])
    y = jnp.concatenate([g1 * y1_hi + g2 * y2_hi, g1 * y1_lo + g2 * y2_lo], axis=-1)
    return x + gt_ref[...] * y


def _qk_head_norm(acc, c0, qk_norm, gq, gk):
    q_hi, k_hi = qk_norm
    low = lax.broadcasted_iota(jnp.int32, (acc.shape[0], LANES), 1) < SW_DH
    slabs = []
    for j in range(acc.shape[1] // LANES):
        slab = acc[:, j * LANES:(j + 1) * LANES]
        if c0 + j * LANES < q_hi:
            sq = slab * slab
            ss_lo = jnp.sum(jnp.where(low, sq, 0.0), axis=-1, keepdims=True)
            ss_hi = jnp.sum(jnp.where(low, 0.0, sq), axis=-1, keepdims=True)
            rs = jnp.where(low, lax.rsqrt(ss_lo / SW_DH + EPS), lax.rsqrt(ss_hi / SW_DH + EPS))
            slab = slab * rs * gq
        elif c0 + j * LANES < k_hi:
            slab = slab * lax.rsqrt(jnp.mean(slab * slab, axis=-1, keepdims=True) + EPS) * gk
        slabs.append(slab)
    return jnp.concatenate(slabs, axis=-1)


def _inproj_kernel(*refs, n_main, chunk, q_cols, q_scale, gate_cols, qk_norm, with_ml,
                   with_combine):
    refs = list(refs)
    n_in = (5 + (4 if with_combine else 0) + (2 if with_ml else 0)
            + (2 if qk_norm is not None else 0))
    n_scratch = 4 if with_ml else 2
    ins, outs, scratch = refs[:n_in], refs[n_in:-n_scratch], refs[-n_scratch:]
    x_ref = ins.pop(0)
    if with_combine:
        y1_ref, y2_ref, info_ref, gtp_ref = ins[:4]
        ins = ins[4:]
        xo_ref = outs.pop(0)
    g_ref, sc_ref, sh_ref, w_ref = ins[:4]
    o_ref = outs[0]
    if with_ml:
        wk_ref, wgh_ref = ins[4:]
        kt_ref, gt_ref = outs[1:]
    if qk_norm is not None:
        gq_ref, gk_ref = ins[4:]

    def normalise(hb_dst, lo_dst):
        x = x_ref[...]
        if with_combine:
            x = _moe_combined(x, y1_ref, y2_ref, info_ref, gtp_ref)
            xo_ref[...] = x
        hn = _modulated_norm(x, g_ref[...], sc_ref[...], sh_ref[...])
        hb = hn.astype(jnp.bfloat16)
        hb_dst[...] = hb
        if with_ml:
            lo_dst[...] = (hn - hb.astype(jnp.float32)).astype(jnp.bfloat16)

    def project(hb_src, lo_src):
        hb = hb_src[...]
        for c0 in range(0, n_main, chunk):
            c1 = min(c0 + chunk, n_main)
            acc = _bdot(hb, w_ref[:, c0:c1])
            if q_cols is not None and q_cols[0] <= c0 < q_cols[1]:
                acc = acc * q_scale
            if gate_cols is not None and gate_cols[0] <= c0 < gate_cols[1]:
                acc = jax.nn.sigmoid(acc)
            if qk_norm is not None and c0 < qk_norm[1]:
                acc = _qk_head_norm(acc, c0, qk_norm, gq_ref[...], gk_ref[...])
            o_ref[:, c0:c1] = acc.astype(o_ref.dtype)
        if with_ml:
            nk = kt_ref.shape[0]
            stacked = _bdot_nt(wk_ref[...], hb)
            kt_ref[...] = stacked[:nk].astype(kt_ref.dtype)
            gt_ref[...] = (stacked[nk:nk + ML_GATE_ROWS]
                           + (_bdot_nt(wgh_ref[...], lo_src[...]) + stacked[nk + ML_GATE_ROWS:]))

    hb_a, hb_b = scratch[:2]
    lo_a, lo_b = scratch[2:] if with_ml else (None, None)
    s = pl.program_id(0)

    @pl.when(s == 0)
    def _():
        hb_b[...] = jnp.zeros_like(hb_b)
        if with_ml:
            lo_b[...] = jnp.zeros_like(lo_b)

    @pl.when(s % 2 == 0)
    def _():
        normalise(hb_a, lo_a)
        project(hb_b, lo_b)

    @pl.when(s % 2 == 1)
    def _():
        normalise(hb_b, lo_b)
        project(hb_a, lo_a)


def _inproj(x2d, g, sc, sh, w_main, seq, *, ml_extra=None, q_cols=None, q_scale=1.0,
            gate_cols=None, qk_norm=None, pending=None):
    t, d = x2d.shape
    tm = TOKEN_TILE
    n_main = w_main.shape[1]
    tiles_per_batch = seq // tm
    n_tiles = t // tm
    norm_tile = lambda s: jnp.minimum(s, n_tiles - 1)
    proj_tile = lambda s: jnp.maximum(s - 1, 0)
    row = lambda s: (norm_tile(s), 0)
    per_batch = lambda s: (norm_tile(s) // tiles_per_batch, 0, 0)
    const = lambda s: (0, 0)
    in_specs = [pl.BlockSpec((tm, d), row)]
    args = [x2d]
    out_specs, out_shape = [], []
    if pending is not None:
        y_pairs, info, gt_prev = pending
        in_specs += [pl.BlockSpec((tm, d // 2), row),
                     pl.BlockSpec((tm, d // 2), lambda s: (norm_tile(s) + n_tiles, 0)),
                     pl.BlockSpec((tm, ROUTER_LANES), row),
                     pl.BlockSpec((None, 1, d), per_batch)]
        args += [y_pairs, y_pairs, info, gt_prev]
        out_specs += [pl.BlockSpec((tm, d), row)]
        out_shape += [jax.ShapeDtypeStruct((t, d), jnp.float32)]
    in_specs += [
        pl.BlockSpec((1, d), const),
        pl.BlockSpec((None, 1, d), per_batch),
        pl.BlockSpec((None, 1, d), per_batch),
        pl.BlockSpec((d, n_main), const),
    ]
    args += [g.reshape(1, d), sc, sh, w_main]
    out_specs += [pl.BlockSpec((tm, n_main), lambda s: (proj_tile(s), 0))]
    out_shape += [jax.ShapeDtypeStruct((t, n_main), jnp.bfloat16)]
    scratch = [pltpu.VMEM((tm, d), jnp.bfloat16), pltpu.VMEM((tm, d), jnp.bfloat16)]
    if ml_extra is not None:
        wk_t, wg_hi, wg_lo = ml_extra
        nk = wk_t.shape[0]
        stacked = jnp.concatenate([wk_t, wg_hi, wg_lo], axis=0)
        in_specs += [pl.BlockSpec(stacked.shape, const),
                     pl.BlockSpec(wg_hi.shape, const)]
        args += [stacked, wg_hi]
        out_specs += [pl.BlockSpec((nk, tm), lambda s: (0, proj_tile(s))),
                      pl.BlockSpec((ML_GATE_ROWS, tm), lambda s: (0, proj_tile(s)))]
        out_shape += [jax.ShapeDtypeStruct((nk, t), jnp.bfloat16),
                      jax.ShapeDtypeStruct((ML_GATE_ROWS, t), jnp.float32)]
        scratch += [pltpu.VMEM((tm, d), jnp.bfloat16), pltpu.VMEM((tm, d), jnp.bfloat16)]
    if qk_norm is not None:
        in_specs += [pl.BlockSpec((1, LANES), const), pl.BlockSpec((1, LANES), const)]
        args += [qk_norm[2], qk_norm[3]]
    kern = functools.partial(_inproj_kernel, n_main=n_main, chunk=PROJ_COL_CHUNK, q_cols=q_cols,
                             q_scale=q_scale, gate_cols=gate_cols,
                             qk_norm=None if qk_norm is None else qk_norm[:2],
                             with_ml=ml_extra is not None, with_combine=pending is not None)
    return pl.pallas_call(
        kern,
        grid=(n_tiles + 1,),
        in_specs=in_specs,
        out_specs=out_specs,
        out_shape=out_shape,
        scratch_shapes=scratch,
        compiler_params=_params(("arbitrary",)),
        name="inproj_ml" if ml_extra is not None else "inproj_sw",
    )(*args)


def _mlstm_gate_terms(graw, bias, upper):
    H = ML_HEADS
    L = graw.shape[1]
    z = graw + bias
    gates = GATE_CAP * jnp.tanh(z / GATE_CAP)
    log_f = jnp.minimum(gates, 0.0) - jnp.log1p(jnp.exp(-jnp.abs(gates)))
    row = lax.broadcasted_iota(jnp.int32, (ML_GATE_ROWS, L), 0)
    lane = lax.broadcasted_iota(jnp.int32, (ML_GATE_ROWS, L), 1)
    is_i = row < H
    slab = jnp.where(is_i, gates, log_f)
    a1 = slab.astype(jnp.bfloat16)
    r1 = slab - a1.astype(jnp.float32)
    a2 = r1.astype(jnp.bfloat16)
    a3 = (r1 - a2.astype(jnp.float32)).astype(jnp.bfloat16)
    cum = _bdot(a1, upper) + (_bdot(a2, upper) + _bdot(a3, upper))
    ib = jnp.where(is_i, gates, cum)
    b = pltpu.roll(ib, ML_GATE_ROWS - H, 0)
    u = ib - b
    cm = u
    shift = 1
    while shift < L:
        cm = jnp.maximum(cm, jnp.where(lane >= shift, pltpu.roll(cm, shift, 1), -jnp.inf))
        shift *= 2
    return b, u, cm


def _mlstm_kernel(v_ref, o_ref, q_ref, kt_ref, gt_ref, gtn_ref, bg_ref, gout_ref, out_ref,
                  c_ref, m_ref, b_ref, u_ref, cm_ref, *, chunk):
    L = chunk
    H, dk, dv = ML_HEADS, ML_DK, ML_DV
    r_idx = lax.broadcasted_iota(jnp.int32, (L, L), 0)
    c_idx = lax.broadcasted_iota(jnp.int32, (L, L), 1)
    upper = jnp.where(r_idx <= c_idx, 1.0, 0.0).astype(jnp.bfloat16)
    causal = r_idx >= c_idx
    row = lax.broadcasted_iota(jnp.int32, (ML_GATE_ROWS, L), 0)
    ones_col = jnp.where(lax.broadcasted_iota(jnp.int32, (L, LANES), 1) == 0, 1.0, 0.0
                         ).astype(jnp.bfloat16)

    @pl.when(pl.program_id(1) == 0)
    def _():
        c_ref[...] = jnp.zeros_like(c_ref)
        m_ref[...] = jnp.zeros_like(m_ref)
        b0, u0, cm0 = _mlstm_gate_terms(gt_ref[...], bg_ref[...], upper)
        b_ref[...] = b0
        u_ref[...] = u0
        cm_ref[...] = cm0

    b16 = b_ref[...]
    u16 = u_ref[...]
    cm16 = cm_ref[...]
    b_n, u_n, cm_n = _mlstm_gate_terms(gtn_ref[...], bg_ref[...], upper)
    b_ref[...] = b_n
    u_ref[...] = u_n
    cm_ref[...] = cm_n

    m_prev = m_ref[:, 0:1]
    z16 = jnp.maximum(m_prev, cm16)
    w_inter16 = jnp.exp(m_prev - z16)
    e_negm16 = jnp.exp(-(b16 + z16))
    z_last = z16[:, L - 1:L]
    w_state16 = jnp.exp(u16 - z_last)
    decay16 = jnp.exp(m_prev - z_last)
    m_ref[...] = jnp.broadcast_to(b16[:, L - 1:L] + z_last, m_ref.shape)
    stacked = jnp.where(row < H, z16,
                        jnp.where(row < 2 * H, pltpu.roll(w_inter16, H, 0),
                                  pltpu.roll(e_negm16, 2 * H, 0)))
    cols = jnp.concatenate(
        [stacked, jnp.zeros((LANES - ML_GATE_ROWS, L), jnp.float32)], axis=0).T

    for h in range(H):
        u_r = u16[h:h + 1, :]
        z_c = cols[:, h:h + 1]
        w_inter = cols[:, H + h:H + h + 1]
        e_negm = cols[:, 2 * H + h:2 * H + h + 1]
        c_ext = c_ref[h]
        q = q_ref[:, h * dk:(h + 1) * dk]
        kt = kt_ref[h * dk:(h + 1) * dk, :]
        v_ext = jnp.concatenate([v_ref[:, h * dv:(h + 1) * dv], ones_col], axis=-1)

        c_bf = c_ext.astype(jnp.bfloat16)
        for half in range(2):
            rs = slice(half * (L // 2), (half + 1) * (L // 2))
            nk = (half + 1) * (L // 2)
            w_intra = jnp.exp(jnp.where(causal[rs, :nk], u_r[:, :nk] - z_c[rs], -jnp.inf))
            s = (_bdot(q[rs], kt[:, :nk]) * w_intra).astype(jnp.bfloat16)
            nd = w_inter[rs] * _bdot(q[rs], c_bf) + _bdot(s, v_ext[:nk])
            den = nd[:, dv:dv + 1]
            hb = nd[:, :dv] * (1.0 / jnp.maximum(jnp.abs(den), e_negm[rs]))
            y = hb * lax.rsqrt(jnp.mean(hb * hb, axis=-1, keepdims=True) + EPS)
            y = y * gout_ref[:, h * dv:(h + 1) * dv]
            og = o_ref[rs, h * dv:(h + 1) * dv].astype(jnp.float32)
            out_ref[rs, h * dv:(h + 1) * dv] = (y * og).astype(out_ref.dtype)

        kw = (kt.astype(jnp.float32) * w_state16[h:h + 1, :]).astype(jnp.bfloat16)
        c_ref[h] = decay16[h:h + 1, :] * c_ext + _bdot(kw, v_ext)


def _mlstm_core(main, k_t, g_t, b_gate, g_out, bsz, seq):
    chunk = ML_CHUNK
    t = main.shape[0]
    nc = seq // chunk
    blk = lambda b, c: b * nc + c
    bg = jnp.zeros((ML_GATE_ROWS, 1), jnp.float32).at[:2 * ML_HEADS, 0].set(b_gate)
    return pl.pallas_call(
        functools.partial(_mlstm_kernel, chunk=chunk),
        grid=(bsz, nc),
        in_specs=[
            pl.BlockSpec((chunk, ML_V), lambda b, c: (blk(b, c), 0)),
            pl.BlockSpec((chunk, ML_V), lambda b, c: (blk(b, c), 1)),
            pl.BlockSpec((chunk, ML_QK), lambda b, c: (blk(b, c), 4)),
            pl.BlockSpec((ML_QK, chunk), lambda b, c: (0, blk(b, c))),
            pl.BlockSpec((ML_GATE_ROWS, chunk), lambda b, c: (0, blk(b, c))),
            pl.BlockSpec((ML_GATE_ROWS, chunk), lambda b, c: (0, blk(b, jnp.minimum(c + 1, nc - 1)))),
            pl.BlockSpec((ML_GATE_ROWS, 1), lambda b, c: (0, 0)),
            pl.BlockSpec((1, ML_V), lambda b, c: (0, 0)),
        ],
        out_specs=pl.BlockSpec((chunk, ML_V), lambda b, c: (blk(b, c), 0)),
        out_shape=jax.ShapeDtypeStruct((t, ML_V), jnp.bfloat16),
        scratch_shapes=[
            pltpu.VMEM((ML_HEADS, ML_DK, ML_DV + LANES), jnp.float32),
            pltpu.VMEM((ML_GATE_ROWS, LANES), jnp.float32),
            pltpu.VMEM((ML_GATE_ROWS, chunk), jnp.float32),
            pltpu.VMEM((ML_GATE_ROWS, chunk), jnp.float32),
            pltpu.VMEM((ML_GATE_ROWS, chunk), jnp.float32),
        ],
        compiler_params=_params(("arbitrary", "arbitrary")),
        name="mlstm_core",
    )(main, main, main, k_t, g_t, g_t, bg, g_out.reshape(1, ML_V))


def _swa_kernel(sinks_ref, q_ref, kc_ref, kp_ref, vc_ref, vp_ref, bias_ref, o_ref):
    W = SW_WINDOW
    lane = lax.broadcasted_iota(jnp.int32, (W, LANES), 1)
    low = lane < SW_DH
    row_col = lax.broadcasted_iota(jnp.int32, (SW_GROUP * W, 1), 0)
    bias = bias_ref[jnp.minimum(pl.program_id(1), 1)]
    for g in range(SW_KV_HEADS):
        sl = slice(g * LANES, (g + 1) * LANES)
        kn = jnp.concatenate([kp_ref[:, sl], kc_ref[:, sl]], axis=0)
        v2 = jnp.concatenate([vp_ref[:, sl], vc_ref[:, sl]], axis=0)
        sink = jnp.full((SW_GROUP * W, 1), sinks_ref[g * SW_GROUP + SW_GROUP - 1], jnp.float32)
        for j in range(SW_GROUP - 2, -1, -1):
            sink = jnp.where(row_col < (j + 1) * W, sinks_ref[g * SW_GROUP + j], sink)
        parts = []
        for p in range(2):
            c0 = g * SW_GROUP * SW_DH + p * LANES
            qp = q_ref[:, c0:c0 + LANES].astype(jnp.float32)
            parts.append(jnp.where(low, qp, 0.0).astype(jnp.bfloat16))
            parts.append(jnp.where(low, 0.0, qp).astype(jnp.bfloat16))
        q4 = jnp.concatenate(parts, axis=0)
        scores = _bdot_nt(q4, kn) + bias
        m = jnp.maximum(jnp.max(scores, axis=-1, keepdims=True), sink)
        pexp = jnp.exp2(scores - m)
        denom = jnp.sum(pexp, axis=-1, keepdims=True) + jnp.exp2(sink - m)
        o4 = _bdot(pexp.astype(jnp.bfloat16), v2) * (1.0 / denom)
        for p in range(2):
            oa = o4[(2 * p) * W:(2 * p + 1) * W]
            ob = o4[(2 * p + 1) * W:(2 * p + 2) * W]
            c0 = g * SW_GROUP * SW_DH + p * LANES
            o_ref[:, c0:c0 + LANES] = jnp.where(low, oa, ob).astype(o_ref.dtype)


def _swa_core(proj, sinks, bsz, seq):
    t = proj.shape[0]
    W = SW_WINDOW
    nb = seq // W
    dq = SW_Q_HEADS * SW_DH
    kv_w = SW_KV_HEADS * LANES
    k_blk = dq // kv_w
    v_blk = k_blk + 1
    cur = lambda b, n, s: b * nb + n
    prev = lambda b, n, s: b * nb + jnp.maximum(n - 1, 0)
    sinks = sinks.astype(jnp.float32) * LOG2E
    qi = (jnp.arange(SW_GROUP * W) % W)[:, None]
    ki = jnp.arange(2 * W)[None, :]
    rel = qi + W - ki
    in_win = (rel >= 0) & (rel < W)
    bias = jnp.stack([jnp.where(in_win & (ki >= W), 0.0, -jnp.inf),
                      jnp.where(in_win, 0.0, -jnp.inf)]).astype(jnp.float32)
    grid_spec = pltpu.PrefetchScalarGridSpec(
        num_scalar_prefetch=1,
        grid=(bsz, nb),
        in_specs=[
            pl.BlockSpec((W, dq), lambda b, n, s: (cur(b, n, s), 0)),
            pl.BlockSpec((W, kv_w), lambda b, n, s: (cur(b, n, s), k_blk)),
            pl.BlockSpec((W, kv_w), lambda b, n, s: (prev(b, n, s), k_blk)),
            pl.BlockSpec((W, kv_w), lambda b, n, s: (cur(b, n, s), v_blk)),
            pl.BlockSpec((W, kv_w), lambda b, n, s: (prev(b, n, s), v_blk)),
            pl.BlockSpec((2, SW_GROUP * W, 2 * W), lambda b, n, s: (0, 0, 0)),
        ],
        out_specs=pl.BlockSpec((W, dq), lambda b, n, s: (cur(b, n, s), 0)),
    )
    return pl.pallas_call(
        _swa_kernel,
        grid_spec=grid_spec,
        out_shape=jax.ShapeDtypeStruct((t, dq), jnp.bfloat16),
        compiler_params=_params(("arbitrary", "arbitrary")),
        name="swa_core",
    )(sinks, proj, proj, proj, proj, proj, bias)


def _pack_rounded_pairs(r):
    k = r.shape[1] // 2
    hi = lax.bitcast_convert_type(r[:, :k], jnp.uint32)
    lo = lax.bitcast_convert_type(r[:, k:], jnp.uint32)
    return hi | (lo >> 16)


def _pack_bf16_pairs(a):
    return _pack_rounded_pairs(a.astype(jnp.bfloat16).astype(jnp.float32))


def _unpack_bf16_pairs(u):
    hi = lax.bitcast_convert_type(u & jnp.uint32(0xFFFF0000), jnp.float32)
    lo = lax.bitcast_convert_type(u << 16, jnp.float32)
    return hi, lo


def _route_tile(x_new, g_ref, sc_ref, sh_ref, wh_ref, wl_ref, b_ref,
                hn_ref, info_ref, infot_ref, cnt_ref, carry_ref, earlier_ref, tm):
    hn = _modulated_norm(x_new, g_ref[...], sc_ref[...], sh_ref[...])
    h_hi = hn.astype(jnp.bfloat16)
    hi_f32 = h_hi.astype(jnp.float32)
    h_lo = (hn - hi_f32).astype(jnp.bfloat16)
    hn_ref[...] = _pack_rounded_pairs(hi_f32)
    wide = _bdot_nt(wl_ref[...], h_hi)
    logits = (wide[:ROUTER_LANES] + (_bdot_nt(wh_ref[...], h_lo) + wide[ROUTER_LANES:])
              + b_ref[...])
    E8 = EXPERTS_PER_GROUP
    sub = lax.broadcasted_iota(jnp.int32, (E8, tm), 0).astype(jnp.float32)
    big = float(ROUTER_LANES)
    neg = -jnp.inf

    gl = logits[N_EXPERTS:N_EXPERTS + N_GROUPS]
    gmax = jnp.max(gl, axis=0, keepdims=True)
    gsel = jnp.min(jnp.where(gl == gmax, sub, big), axis=0, keepdims=True)
    p_grp = 1.0 / jnp.sum(jnp.exp(gl - gmax), axis=0, keepdims=True)

    el = logits[0:E8]
    for grp in range(1, N_GROUPS):
        el = jnp.where(gsel == grp, logits[grp * E8:(grp + 1) * E8], el)
    v1 = jnp.max(el, axis=0, keepdims=True)
    j1 = jnp.min(jnp.where(el == v1, sub, big), axis=0, keepdims=True)
    el2 = jnp.where(sub == j1, neg, el)
    v2 = jnp.max(el2, axis=0, keepdims=True)
    j2 = jnp.min(jnp.where(el2 == v2, sub, big), axis=0, keepdims=True)
    i1 = gsel * E8 + j1
    i2 = gsel * E8 + j2
    e21 = jnp.exp(v2 - v1)
    gate1 = p_grp / (1.0 + e21)
    gate2 = p_grp * e21 / (1.0 + e21)

    erow = lax.broadcasted_iota(jnp.int32, (N_EXPERTS, tm), 0).astype(jnp.float32)
    hit1 = erow == i1
    hit2 = erow == i2
    onehot = jnp.where(hit1 | hit2, 1.0, 0.0)
    carry = carry_ref[:, 0:1]
    before = _bdot(onehot.astype(jnp.bfloat16), earlier_ref[...]) + carry
    rank1 = jnp.sum(jnp.where(hit1, before, 0.0), axis=0, keepdims=True)
    rank2 = jnp.sum(jnp.where(hit2, before, 0.0), axis=0, keepdims=True)
    total = carry + jnp.sum(onehot, axis=1, keepdims=True)
    carry_ref[...] = jnp.broadcast_to(total, carry_ref.shape)
    cnt_ref[...] = jnp.broadcast_to(total, cnt_ref.shape)

    info_t = jnp.where(sub == 0, i1, 0.0)
    info_t = jnp.where(sub == 1, i2, info_t)
    info_t = jnp.where(sub == 2, rank1, info_t)
    info_t = jnp.where(sub == 3, rank2, info_t)
    info_t = jnp.where(sub == 4, gate1, info_t)
    info_t = jnp.where(sub == 5, gate2, info_t)
    infot_ref[...] = info_t
    info_ref[...] = jnp.concatenate(
        [info_t, jnp.zeros((ROUTER_LANES - E8, tm), jnp.float32)], axis=0).T


def _router_kernel(a_ref, wo_ref, x_ref, gt_ref, g_ref, sc_ref, sh_ref, wh_ref, wl_ref, b_ref,
                   xo_ref, hn_ref, info_ref, infot_ref, cnt_ref, carry_ref, earlier_ref, *, tm):
    @pl.when(pl.program_id(0) == 0)
    def _():
        carry_ref[...] = jnp.zeros_like(carry_ref)
        r_idx = lax.broadcasted_iota(jnp.int32, (tm, tm), 0)
        c_idx = lax.broadcasted_iota(jnp.int32, (tm, tm), 1)
        earlier_ref[...] = jnp.where(r_idx < c_idx, 1.0, 0.0).astype(jnp.bfloat16)

    x_new = x_ref[...] + gt_ref[...] * _bdot(a_ref[...], wo_ref[...])
    xo_ref[...] = x_new
    _route_tile(x_new, g_ref, sc_ref, sh_ref, wh_ref, wl_ref, b_ref,
                hn_ref, info_ref, infot_ref, cnt_ref, carry_ref, earlier_ref, tm)


def _outproj_router(a, w_out, x2d, gt, g, sc, sh, w_hi, w_lo, bias, seq):
    t, d = x2d.shape
    tm = TOKEN_TILE
    tiles_per_batch = seq // tm
    per_batch = lambda i: (i // tiles_per_batch, 0, 0)
    const = lambda i: (0, 0)
    return pl.pallas_call(
        functools.partial(_router_kernel, tm=tm),
        grid=(t // tm,),
        in_specs=[
            pl.BlockSpec((tm, a.shape[1]), lambda i: (i, 0)),
            pl.BlockSpec(w_out.shape, const),
            pl.BlockSpec((tm, d), lambda i: (i, 0)),
            pl.BlockSpec((None, 1, d), per_batch),
            pl.BlockSpec((1, d), const),
            pl.BlockSpec((None, 1, d), per_batch),
            pl.BlockSpec((None, 1, d), per_batch),
            pl.BlockSpec((ROUTER_LANES, d), const),
            pl.BlockSpec((2 * ROUTER_LANES, d), const),
            pl.BlockSpec((ROUTER_LANES, 1), const),
        ],
        out_specs=[
            pl.BlockSpec((tm, d), lambda i: (i, 0)),
            pl.BlockSpec((tm, d // 2), lambda i: (i, 0)),
            pl.BlockSpec((tm, ROUTER_LANES), lambda i: (i, 0)),
            pl.BlockSpec((EXPERTS_PER_GROUP, tm), lambda i: (0, i)),
            pl.BlockSpec((N_EXPERTS, LANES), const),
        ],
        out_shape=[
            jax.ShapeDtypeStruct((t, d), jnp.float32),
            jax.ShapeDtypeStruct((t, d // 2), jnp.uint32),
            jax.ShapeDtypeStruct((t, ROUTER_LANES), jnp.float32),
            jax.ShapeDtypeStruct((EXPERTS_PER_GROUP, t), jnp.float32),
            jax.ShapeDtypeStruct((N_EXPERTS, LANES), jnp.float32),
        ],
        scratch_shapes=[pltpu.VMEM((N_EXPERTS, LANES), jnp.float32),
                        pltpu.VMEM((tm, tm), jnp.bfloat16)],
        compiler_params=_params(("arbitrary",)),
        name="outproj_router",
    )(a, w_out, x2d, gt, g.reshape(1, d), sc, sh, w_hi, jnp.concatenate([w_hi, w_lo], axis=0), bias)


def _sc_mesh():
    return plsc.VectorSubcoreMesh(core_axis_name="c", subcore_axis_name="s")


def _sc_dispatch(rows, d0, d1, pad_idx, n_slots):
    t, w = rows.shape
    n_pad = pad_idx.shape[1]
    zeros = jnp.zeros((SC_ROWS, w), rows.dtype)
    sem = (pltpu.PARALLEL, pltpu.ARBITRARY)
    parts = SC_WINDOW // SC_ROWS

    @pl.kernel(out_type=jax.ShapeDtypeStruct((n_slots + SC_WINDOW, w), rows.dtype), mesh=_sc_mesh(),
               scratch_types=[pltpu.SemaphoreType.DMA, pltpu.SemaphoreType.DMA])
    def dispatch(x_hbm, d0_hbm, d1_hbm, z_hbm, p_hbm, o_hbm, sem0, sem1):
        def scatter_rows(x_vmem, i0_vmem, i1_vmem):
            part = pl.ds(pl.program_id(1) * SC_ROWS, SC_ROWS)
            first = pltpu.async_copy(x_vmem, o_hbm.at[i0_vmem.at[0, part]], sem0)
            second = pltpu.async_copy(x_vmem, o_hbm.at[i1_vmem.at[0, part]], sem1)
            first.wait()
            second.wait()

        pltpu.emit_pipeline(
            scatter_rows,
            grid=(t // SC_WINDOW, parts),
            in_specs=[pl.BlockSpec((SC_ROWS, w), lambda i, j: (parts * i + j, 0)),
                      pl.BlockSpec((1, SC_WINDOW), lambda i, j: (0, i)),
                      pl.BlockSpec((1, SC_WINDOW), lambda i, j: (0, i))],
            out_specs=[],
            core_axis_name=("c", "s"),
            dimension_semantics=sem,
        )(x_hbm, d0_hbm, d1_hbm)

        def scatter_zeros(z_vmem, p_vmem):
            part = pl.ds(pl.program_id(1) * SC_ROWS, SC_ROWS)
            pltpu.sync_copy(z_vmem, o_hbm.at[p_vmem.at[0, part]])

        pltpu.emit_pipeline(
            scatter_zeros,
            grid=(n_pad // SC_WINDOW, parts),
            in_specs=[pl.BlockSpec((SC_ROWS, w), lambda i, j: (0, 0)),
                      pl.BlockSpec((1, SC_WINDOW), lambda i, j: (0, i))],
            out_specs=[],
            core_axis_name=("c", "s"),
            dimension_semantics=sem,
        )(z_hbm, p_hbm)

    return dispatch(rows, d0, d1, zeros, pad_idx)


def _sc_gather(src, idx):
    n_out = idx.shape[1]
    w = src.shape[1]
    parts = SC_WINDOW // SC_ROWS

    @pl.kernel(out_type=jax.ShapeDtypeStruct((n_out, w), src.dtype), mesh=_sc_mesh())
    def gather(x_hbm, i_hbm, o_hbm):
        def gather_rows(i_vmem, o_vmem):
            part = pl.ds(pl.program_id(1) * SC_ROWS, SC_ROWS)
            pltpu.sync_copy(x_hbm.at[i_vmem.at[0, part]], o_vmem)

        pltpu.emit_pipeline(
            gather_rows,
            grid=(n_out // SC_WINDOW, parts),
            in_specs=[pl.BlockSpec((1, SC_WINDOW), lambda i, j: (0, i))],
            out_specs=[pl.BlockSpec((SC_ROWS, w), lambda i, j: (parts * i + j, 0))],
            core_axis_name=("c", "s"),
            dimension_semantics=(pltpu.PARALLEL, pltpu.ARBITRARY),
        )(i_hbm, o_hbm)

    return gather(src, idx)


def _expert_kernel(meta_ref, x_ref, w1_hbm, w3_hbm, w2_hbm, y_ref,
                   w1_buf, w3_buf, w2_buf, w1_c, w3_c, w2_c, sems, *, layer, nb):
    def weight_copies(expert, s):
        return (pltpu.make_async_copy(w1_hbm.at[layer, expert], w1_buf.at[s], sems.at[s, 0]),
                pltpu.make_async_copy(w3_hbm.at[layer, expert], w3_buf.at[s], sems.at[s, 1]),
                pltpu.make_async_copy(w2_hbm.at[layer, expert], w2_buf.at[s], sems.at[s, 2]))

    @pl.when(pl.program_id(0) == 0)
    def _():
        for cp in weight_copies(meta_ref[0], meta_ref[nb]):
            cp.start()

    for j in range(EXPERT_BLOCKS_PER_STEP):
        i = pl.program_id(0) * EXPERT_BLOCKS_PER_STEP + j
        rows = slice(j * MOE_BLOCK, (j + 1) * MOE_BLOCK)
        e = meta_ref[i]
        slot = meta_ref[nb + i]
        nxt = meta_ref[2 * nb + i]
        used = i < meta_ref[3 * nb]
        first = used & ((i == 0) | (e != meta_ref[jnp.maximum(i - 1, 0)]))

        @pl.when(first)
        def _():
            for cp in weight_copies(e, slot):
                cp.wait()

            @pl.when(nxt >= 0)
            def _():
                for cp in weight_copies(nxt, 1 - slot):
                    cp.start()

            w1_c[...] = w1_buf[slot].astype(jnp.bfloat16)
            w3_c[...] = w3_buf[slot].astype(jnp.bfloat16)
            w2_c[...] = w2_buf[slot].astype(jnp.bfloat16)

        @pl.when(used)
        def _():
            x_hi, x_lo = _unpack_bf16_pairs(x_ref[rows, :])
            xb = jnp.concatenate([x_hi, x_lo], axis=-1).astype(jnp.bfloat16)
            full = 256
            tail = jnp.concatenate([w1_c[:, full:], w3_c[:, full:]], axis=-1)
            half_rows = MOE_BLOCK // 2
            h_tail = jnp.concatenate([_bdot(xb[:half_rows], tail), _bdot(xb[half_rows:], tail)], axis=0)
            h1 = jnp.concatenate([_bdot(xb, w1_c[:, :full]), h_tail[:, :LANES]], axis=-1)
            h3 = jnp.concatenate([_bdot(xb, w3_c[:, :full]), h_tail[:, LANES:]], axis=-1)
            act = (h1 * jax.nn.sigmoid(h1) * h3).astype(jnp.bfloat16)
            y_ref[rows, :] = _pack_bf16_pairs(_bdot(act, w2_c[...]))

        @pl.when(jnp.logical_not(used))
        def _():
            y_ref[rows, :] = jnp.zeros((MOE_BLOCK, y_ref.shape[1]), y_ref.dtype)


def _experts(x_slots, blk_meta, w1, w3, w2, layer, nb):
    dp = x_slots.shape[1]
    d, de = w1.shape[-2:]
    step_rows = EXPERT_BLOCKS_PER_STEP * MOE_BLOCK
    assert nb % EXPERT_BLOCKS_PER_STEP == 0
    last_used_step = lambda s: (s[3 * nb] - 1) // EXPERT_BLOCKS_PER_STEP
    grid_spec = pltpu.PrefetchScalarGridSpec(
        num_scalar_prefetch=1,
        grid=(nb // EXPERT_BLOCKS_PER_STEP,),
        in_specs=[
            pl.BlockSpec((step_rows, dp), lambda i, s: (jnp.minimum(i, last_used_step(s)), 0)),
            pl.BlockSpec(memory_space=pl.ANY),
            pl.BlockSpec(memory_space=pl.ANY),
            pl.BlockSpec(memory_space=pl.ANY),
        ],
        out_specs=pl.BlockSpec((step_rows, dp), lambda i, s: (i, 0)),
        scratch_shapes=[
            pltpu.VMEM((2, d, de), jnp.float32),
            pltpu.VMEM((2, d, de), jnp.float32),
            pltpu.VMEM((2, de, d), jnp.float32),
            pltpu.VMEM((d, de), jnp.bfloat16),
            pltpu.VMEM((d, de), jnp.bfloat16),
            pltpu.VMEM((de, d), jnp.bfloat16),
            pltpu.SemaphoreType.DMA((2, 3)),
        ],
    )
    return pl.pallas_call(
        functools.partial(_expert_kernel, layer=layer, nb=nb),
        grid_spec=grid_spec,
        out_shape=jax.ShapeDtypeStruct((nb * MOE_BLOCK, dp), jnp.uint32),
        compiler_params=_params(("arbitrary",)),
        name="moe_experts",
    )(blk_meta, x_slots, w1, w3, w2)


def _combine_kernel(x_ref, y1_ref, y2_ref, info_ref, gt_ref, o_ref):
    o_ref[...] = _moe_combined(x_ref[...], y1_ref, y2_ref, info_ref, gt_ref)


def _combine(x2d, y_pairs, info, gt, seq):
    t, d = x2d.shape
    tm = min(COMBINE_TILE, seq)
    tiles_per_batch = seq // tm
    second = t // tm
    return pl.pallas_call(
        _combine_kernel,
        grid=(t // tm,),
        in_specs=[
            pl.BlockSpec((tm, d), lambda i: (i, 0)),
            pl.BlockSpec((tm, d // 2), lambda i: (i, 0)),
            pl.BlockSpec((tm, d // 2), lambda i: (i + second, 0)),
            pl.BlockSpec((tm, ROUTER_LANES), lambda i: (i, 0)),
            pl.BlockSpec((None, 1, d), lambda i: (i // tiles_per_batch, 0, 0)),
        ],
        out_specs=pl.BlockSpec((tm, d), lambda i: (i, 0)),
        out_shape=jax.ShapeDtypeStruct((t, d), jnp.float32),
        compiler_params=_params(("arbitrary",)),
        name="moe_combine",
    )(x2d, y_pairs, y_pairs, info, gt)


def _slot_plan(info_t, cnt, t):
    counts = cnt[:, 0].astype(jnp.int32)
    padded = (counts + MOE_BLOCK - 1) // MOE_BLOCK * MOE_BLOCK
    pad_ends = jnp.cumsum(padded)
    pad_starts = pad_ends - padded
    nb = -(-(2 * t) // MOE_BLOCK) + N_EXPERTS
    n_slots = nb * MOE_BLOCK
    it = info_t.astype(jnp.int32)
    onehot_start = lambda e: jnp.sum(
        jnp.where(e[None, :] == jnp.arange(N_EXPERTS, dtype=jnp.int32)[:, None],
                  pad_starts[:, None], 0), axis=0)
    dest1 = (onehot_start(it[0]) + it[2]).reshape(1, t)
    dest2 = (onehot_start(it[1]) + it[3]).reshape(1, t)
    lane = jnp.arange(MOE_BLOCK, dtype=jnp.int32)[None, :]
    n_padding = (padded - counts)[:, None]
    wrapped = (pad_starts + counts)[:, None] + lane % jnp.maximum(n_padding, 1)
    pad_idx = jnp.where(n_padding > 0, wrapped, n_slots + lane % SC_WINDOW).reshape(-1)
    n_real = pad_ends[-1] // MOE_BLOCK
    n_used = -(-n_real // EXPERT_BLOCKS_PER_STEP) * EXPERT_BLOCKS_PER_STEP
    tail = jnp.arange((EXPERT_BLOCKS_PER_STEP - 1) * MOE_BLOCK, dtype=jnp.int32)
    tail_idx = jnp.where(tail < (n_used - n_real) * MOE_BLOCK, pad_ends[-1] + tail,
                         n_slots + tail % SC_WINDOW)
    pad_idx = jnp.concatenate([pad_idx, tail_idx]).reshape(1, -1)
    experts = jnp.arange(N_EXPERTS, dtype=jnp.int32)
    blk = jnp.arange(nb, dtype=jnp.int32)
    blk_exp = jnp.minimum(
        jnp.sum((pad_ends[None, :] <= (blk * MOE_BLOCK)[:, None]).astype(jnp.int32), axis=1),
        N_EXPERTS - 1)
    blk_exp = jnp.where(blk >= n_real, jnp.max(jnp.where(padded > 0, experts, 0)), blk_exp)
    prev_exp = jnp.concatenate([jnp.full((1,), -1, jnp.int32), blk_exp[:-1]])
    is_first = (blk < n_used) & (blk_exp != prev_exp)
    blk_slot = (jnp.cumsum(is_first.astype(jnp.int32)) + 1) % 2
    later = (experts[None, :] > experts[:, None]) & (padded[None, :] > 0)
    nxt_of = jnp.min(jnp.where(later, experts[None, :], N_EXPERTS), axis=1)
    nxt_of = jnp.where(nxt_of == N_EXPERTS, -1, nxt_of)
    blk_nxt = jnp.sum(jnp.where(blk_exp[:, None] == experts[None, :], nxt_of[None, :], 0), axis=1)
    blk_meta = jnp.concatenate([blk_exp, blk_slot, blk_nxt, n_used[None]]).astype(jnp.int32)
    return dest1, dest2, pad_idx, blk_meta, nb, n_slots


def _mixer_out_and_moe(a, w_out, x2d, gt1, g, sc, sh, w_group, b_group, w_router, b_router,
                       w1, w3, w2, layer, seq):
    t, d = x2d.shape
    w_cat = jnp.zeros((ROUTER_LANES, d), jnp.float32)
    w_cat = w_cat.at[:N_EXPERTS].set(w_router.T).at[N_EXPERTS:N_EXPERTS + N_GROUPS].set(w_group.T)
    b_cat = jnp.zeros((ROUTER_LANES, 1), jnp.float32)
    b_cat = b_cat.at[:N_EXPERTS, 0].set(b_router).at[N_EXPERTS:N_EXPERTS + N_GROUPS, 0].set(b_group)
    w_hi, w_lo = _split_hi_lo(w_cat)
    x_new, hn, info, info_t, cnt = _outproj_router(a, w_out, x2d, gt1, g, sc, sh, w_hi, w_lo,
                                                   b_cat, seq)
    dest1, dest2, pad_idx, blk_meta, nb, n_slots = _slot_plan(info_t, cnt, t)
    x_slots = _sc_dispatch(hn, dest1, dest2, pad_idx, n_slots)
    y_slots = _experts(x_slots, blk_meta, w1, w3, w2, layer, nb)
    y_pairs = _sc_gather(y_slots, jnp.concatenate([dest1, dest2], axis=1))
    return x_new, y_pairs, info


def kernel(x, c, w_ada, b_ada, norm1_g, norm2_g, ml_w_in, ml_b_gate, ml_g_out, ml_w_out,
           sw_w_in, sw_g_q, sw_g_k, sw_sinks, sw_w_out, moe_w_group, moe_b_group,
           moe_w_router, moe_b_router, moe_w1, moe_w3, moe_w2):
    bsz, seq, d = x.shape
    depth = w_ada.shape[0]
    bf = jnp.bfloat16
    mod = _ada_mod(c, w_ada, b_ada)
    x2d = x.reshape(bsz * seq, d)
    pending = None
    for layer in range(depth):
        sh1, sc1, gt1, sh2, sc2, gt2 = [
            mod[layer, :, i * d:(i + 1) * d].reshape(bsz, 1, d) for i in range(6)]
        j = layer // 2
        if layer % 2 == 0:
            w = ml_w_in[j]
            q_w, k_w = w[:, :ML_QK], w[:, ML_QK:2 * ML_QK]
            v_w = w[:, 2 * ML_QK:2 * ML_QK + ML_V]
            o_w = w[:, 2 * ML_QK + ML_V:2 * ML_QK + 2 * ML_V]
            g_w = w[:, 2 * ML_QK + 2 * ML_V:]
            w_main = jnp.concatenate([v_w, o_w, q_w], axis=1).astype(bf)
            wg_t = jnp.zeros((ML_GATE_ROWS, d), jnp.float32).at[:2 * ML_HEADS].set(g_w.T)
            wg_hi, wg_lo = _split_hi_lo(wg_t)
            wk_t = k_w.T.astype(bf)
            w_out = ml_w_out[j].astype(bf)
            outs = _inproj(x2d, norm1_g[layer], sc1, sh1, w_main, seq,
                           ml_extra=(wk_t, wg_hi, wg_lo),
                           q_cols=(2 * ML_V, 2 * ML_V + ML_QK), q_scale=ML_DK ** -0.5,
                           gate_cols=(ML_V, 2 * ML_V), pending=pending)
            if pending is not None:
                x2d, outs = outs[0], outs[1:]
            main, k_t, g_t = outs
            a = _mlstm_core(main, k_t, g_t, ml_b_gate[j], ml_g_out[j], bsz, seq)
        else:
            w = sw_w_in[j]
            dq = SW_Q_HEADS * SW_DH
            dkv = SW_KV_HEADS * SW_DH
            dup = lambda m: jnp.concatenate(
                [m.reshape(d, SW_KV_HEADS, 1, SW_DH)] * 2, axis=2).reshape(d, 2 * dkv)
            w_main = jnp.concatenate(
                [w[:, :dq], dup(w[:, dq:dq + dkv]), dup(w[:, dq + dkv:])], axis=1).astype(bf)
            w_out = sw_w_out[j].astype(bf)
            gq = jnp.concatenate([sw_g_q[j], sw_g_q[j]]).reshape(1, LANES) * (SW_DH ** -0.5 * LOG2E)
            gk = jnp.concatenate([sw_g_k[j], sw_g_k[j]]).reshape(1, LANES)
            outs = _inproj(x2d, norm1_g[layer], sc1, sh1, w_main, seq, pending=pending,
                           qk_norm=(dq, dq + 2 * dkv, gq, gk))
            if pending is not None:
                x2d, outs = outs[0], outs[1:]
            a = _swa_core(outs[0], sw_sinks[j], bsz, seq)
        x2d, y_pairs, info = _mixer_out_and_moe(
            a, w_out, x2d, gt1, norm2_g[layer], sc2, sh2, moe_w_group[layer], moe_b_group[layer],
            moe_w_router[layer], moe_b_router[layer], moe_w1, moe_w3, moe_w2, layer, seq)
        pending = (y_pairs, info, gt2)
    y_pairs, info, gt2 = pending
    return _combine(x2d, y_pairs, info, gt2, seq).reshape(bsz, seq, d)
```

```python
import functools

import jax
import jax.numpy as jnp
from jax import lax
from jax.experimental import pallas as pl
from jax.experimental.pallas import tpu as pltpu
from jax.experimental.pallas import tpu_sc as plsc

EPS = 1e-6
GATE_CAP = 15.0
LOG2E = 1.4426950408889634

ML_HEADS = 4
ML_DK = 128
ML_DV = 256
ML_QK = ML_HEADS * ML_DK
ML_V = ML_HEADS * ML_DV
ML_GATE_ROWS = 16

SW_Q_HEADS = 16
SW_KV_HEADS = 4
SW_GROUP = SW_Q_HEADS // SW_KV_HEADS
SW_DH = 64
SW_WINDOW = 128
LANES = 128

N_GROUPS = 8
EXPERTS_PER_GROUP = 8
N_EXPERTS = N_GROUPS * EXPERTS_PER_GROUP
MOE_BLOCK = 256
ROUTER_LANES = 128
SC_WINDOW = 128
SC_ROWS = 64
EXPERT_BLOCKS_PER_STEP = 4

SUBLANES = 8
TOKEN_TILE = 512
COMBINE_TILE = 1024
PROJ_COL_CHUNK = 1024
ML_CHUNK = 256
ADA_COL_TILE = 768
V7X_VMEM_BYTES = 64 * 1024 * 1024
VMEM_LIMIT = V7X_VMEM_BYTES - 8 * 1024 * 1024

_NT = (((1,), (1,)), ((), ()))


def _bdot(a, b):
    return jnp.dot(a, b, preferred_element_type=jnp.float32)


def _bdot_nt(a, b):
    return lax.dot_general(a, b, _NT, preferred_element_type=jnp.float32)


def _split_hi_lo(a):
    hi = a.astype(jnp.bfloat16)
    lo = (a - hi.astype(jnp.float32)).astype(jnp.bfloat16)
    return hi, lo


def _params(sem):
    return pltpu.CompilerParams(dimension_semantics=sem, vmem_limit_bytes=VMEM_LIMIT)


def _ada_kernel(c_ref, w_ref, b_ref, o_ref):
    c = c_ref[...]
    cond = c * jax.nn.sigmoid(c)
    c_hi, c_lo = _split_hi_lo(cond)
    w_hi, w_lo = _split_hi_lo(w_ref[...])
    acc = _bdot(c_hi, w_hi) + (_bdot(c_lo, w_hi) + _bdot(c_hi, w_lo))
    o_ref[...] = acc + b_ref[...]


def _ada_mod(c, w_ada, b_ada):
    depth, d, n = w_ada.shape
    bsz = c.shape[0]
    rows = SUBLANES
    tn = ADA_COL_TILE
    c_pad = jnp.zeros((rows, d), jnp.float32).at[:bsz].set(c)
    out = pl.pallas_call(
        _ada_kernel,
        grid=(depth, n // tn),
        in_specs=[
            pl.BlockSpec((rows, d), lambda l, j: (0, 0)),
            pl.BlockSpec((None, d, tn), lambda l, j: (l, 0, j)),
            pl.BlockSpec((None, 1, tn), lambda l, j: (l, 0, j)),
        ],
        out_specs=pl.BlockSpec((None, rows, tn), lambda l, j: (l, 0, j)),
        out_shape=jax.ShapeDtypeStruct((depth, rows, n), jnp.float32),
        compiler_params=_params(("arbitrary", "arbitrary")),
        name="ada_mod",
    )(c_pad, w_ada, b_ada.reshape(depth, 1, n))
    return out[:, :bsz]


def _modulated_norm(x, g, sc, sh):
    y = x * lax.rsqrt(jnp.mean(x * x, axis=-1, keepdims=True) + EPS)
    return y * (g * (1.0 + sc)) + sh


def _moe_combined(x, y1_ref, y2_ref, info_ref, gt_ref):
    info = info_ref[...]
    g1 = info[:, 4:5]
    g2 = info[:, 5:6]
    y1_hi, y1_lo = _unpack_bf16_pairs(y1_ref[...])
    y2_hi, y2_lo = _unpack_bf16_pairs(y2_ref[...])
    y = jnp.concatenate([g1 * y1_hi + g2 * y2_hi, g1 * y1_lo + g2 * y2_lo], axis=-1)
    return x + gt_ref[...] * y


def _qk_head_norm(acc, c0, qk_norm, gq, gk):
    q_hi, k_hi = qk_norm
    low = lax.broadcasted_iota(jnp.int32, (acc.shape[0], LANES), 1) < SW_DH
    slabs = []
    for j in range(acc.shape[1] // LANES):
        slab = acc[:, j * LANES:(j + 1) * LANES]
        if c0 + j * LANES < q_hi:
            sq = slab * slab
            ss_lo = jnp.sum(jnp.where(low, sq, 0.0), axis=-1, keepdims=True)
            ss_hi = jnp.sum(jnp.where(low, 0.0, sq), axis=-1, keepdims=True)
            rs = jnp.where(low, lax.rsqrt(ss_lo / SW_DH + EPS), lax.rsqrt(ss_hi / SW_DH + EPS))
            slab = slab * rs * gq
        elif c0 + j * LANES < k_hi:
            slab = slab * lax.rsqrt(jnp.mean(slab * slab, axis=-1, keepdims=True) + EPS) * gk
        slabs.append(slab)
    return jnp.concatenate(slabs, axis=-1)


def _inproj_kernel(*refs, n_main, chunk, q_cols, q_scale, gate_cols, qk_norm, with_ml,
                   with_combine):
    refs = list(refs)
    n_in = (5 + (4 if with_combine else 0) + (2 if with_ml else 0)
            + (2 if qk_norm is not None else 0))
    n_scratch = 4 if with_ml else 2
    ins, outs, scratch = refs[:n_in], refs[n_in:-n_scratch], refs[-n_scratch:]
    x_ref = ins.pop(0)
    if with_combine:
        y1_ref, y2_ref, info_ref, gtp_ref = ins[:4]
        ins = ins[4:]
        xo_ref = outs.pop(0)
    g_ref, sc_ref, sh_ref, w_ref = ins[:4]
    o_ref = outs[0]
    if with_ml:
        wk_ref, wgh_ref = ins[4:]
        kt_ref, gt_ref = outs[1:]
    if qk_norm is not None:
        gq_ref, gk_ref = ins[4:]

    def normalise(hb_dst, lo_dst):
        x = x_ref[...]
        if with_combine:
            x = _moe_combined(x, y1_ref, y2_ref, info_ref, gtp_ref)
            xo_ref[...] = x
        hn = _modulated_norm(x, g_ref[...], sc_ref[...], sh_ref[...])
        hb = hn.astype(jnp.bfloat16)
        hb_dst[...] = hb
        if with_ml:
            lo_dst[...] = (hn - hb.astype(jnp.float32)).astype(jnp.bfloat16)

    def project(hb_src, lo_src):
        hb = hb_src[...]
        for c0 in range(0, n_main, chunk):
            c1 = min(c0 + chunk, n_main)
            acc = _bdot(hb, w_ref[:, c0:c1])
            if q_cols is not None and q_cols[0] <= c0 < q_cols[1]:
                acc = acc * q_scale
            if gate_cols is not None and gate_cols[0] <= c0 < gate_cols[1]:
                acc = jax.nn.sigmoid(acc)
            if qk_norm is not None and c0 < qk_norm[1]:
                acc = _qk_head_norm(acc, c0, qk_norm, gq_ref[...], gk_ref[...])
            o_ref[:, c0:c1] = acc.astype(o_ref.dtype)
        if with_ml:
            nk = kt_ref.shape[0]
            stacked = _bdot_nt(wk_ref[...], hb)
            kt_ref[...] = stacked[:nk].astype(kt_ref.dtype)
            gt_ref[...] = (stacked[nk:nk + ML_GATE_ROWS]
                           + (_bdot_nt(wgh_ref[...], lo_src[...]) + stacked[nk + ML_GATE_ROWS:]))

    hb_a, hb_b = scratch[:2]
    lo_a, lo_b = scratch[2:] if with_ml else (None, None)
    s = pl.program_id(0)

    @pl.when(s == 0)
    def _():
        hb_b[...] = jnp.zeros_like(hb_b)
        if with_ml:
            lo_b[...] = jnp.zeros_like(lo_b)

    @pl.when(s % 2 == 0)
    def _():
        normalise(hb_a, lo_a)
        project(hb_b, lo_b)

    @pl.when(s % 2 == 1)
    def _():
        normalise(hb_b, lo_b)
        project(hb_a, lo_a)


def _inproj(x2d, g, sc, sh, w_main, seq, *, ml_extra=None, q_cols=None, q_scale=1.0,
            gate_cols=None, qk_norm=None, pending=None):
    t, d = x2d.shape
    tm = TOKEN_TILE
    n_main = w_main.shape[1]
    tiles_per_batch = seq // tm
    n_tiles = t // tm
    norm_tile = lambda s: jnp.minimum(s, n_tiles - 1)
    proj_tile = lambda s: jnp.maximum(s - 1, 0)
    row = lambda s: (norm_tile(s), 0)
    per_batch = lambda s: (norm_tile(s) // tiles_per_batch, 0, 0)
    const = lambda s: (0, 0)
    in_specs = [pl.BlockSpec((tm, d), row)]
    args = [x2d]
    out_specs, out_shape = [], []
    if pending is not None:
        y_pairs, info, gt_prev = pending
        in_specs += [pl.BlockSpec((tm, d // 2), row),
                     pl.BlockSpec((tm, d // 2), lambda s: (norm_tile(s) + n_tiles, 0)),
                     pl.BlockSpec((tm, ROUTER_LANES), row),
                     pl.BlockSpec((None, 1, d), per_batch)]
        args += [y_pairs, y_pairs, info, gt_prev]
        out_specs += [pl.BlockSpec((tm, d), row)]
        out_shape += [jax.ShapeDtypeStruct((t, d), jnp.float32)]
    in_specs += [
        pl.BlockSpec((1, d), const),
        pl.BlockSpec((None, 1, d), per_batch),
        pl.BlockSpec((None, 1, d), per_batch),
        pl.BlockSpec((d, n_main), const),
    ]
    args += [g.reshape(1, d), sc, sh, w_main]
    out_specs += [pl.BlockSpec((tm, n_main), lambda s: (proj_tile(s), 0))]
    out_shape += [jax.ShapeDtypeStruct((t, n_main), jnp.bfloat16)]
    scratch = [pltpu.VMEM((tm, d), jnp.bfloat16), pltpu.VMEM((tm, d), jnp.bfloat16)]
    if ml_extra is not None:
        wk_t, wg_hi, wg_lo = ml_extra
        nk = wk_t.shape[0]
        stacked = jnp.concatenate([wk_t, wg_hi, wg_lo], axis=0)
        in_specs += [pl.BlockSpec(stacked.shape, const),
                     pl.BlockSpec(wg_hi.shape, const)]
        args += [stacked, wg_hi]
        out_specs += [pl.BlockSpec((nk, tm), lambda s: (0, proj_tile(s))),
                      pl.BlockSpec((ML_GATE_ROWS, tm), lambda s: (0, proj_tile(s)))]
        out_shape += [jax.ShapeDtypeStruct((nk, t), jnp.bfloat16),
                      jax.ShapeDtypeStruct((ML_GATE_ROWS, t), jnp.float32)]
        scratch += [pltpu.VMEM((tm, d), jnp.bfloat16), pltpu.VMEM((tm, d), jnp.bfloat16)]
    if qk_norm is not None:
        in_specs += [pl.BlockSpec((1, LANES), const), pl.BlockSpec((1, LANES), const)]
        args += [qk_norm[2], qk_norm[3]]
    kern = functools.partial(_inproj_kernel, n_main=n_main, chunk=PROJ_COL_CHUNK, q_cols=q_cols,
                             q_scale=q_scale, gate_cols=gate_cols,
                             qk_norm=None if qk_norm is None else qk_norm[:2],
                             with_ml=ml_extra is not None, with_combine=pending is not None)
    return pl.pallas_call(
        kern,
        grid=(n_tiles + 1,),
        in_specs=in_specs,
        out_specs=out_specs,
        out_shape=out_shape,
        scratch_shapes=scratch,
        compiler_params=_params(("arbitrary",)),
        name="inproj_ml" if ml_extra is not None else "inproj_sw",
    )(*args)


def _mlstm_gate_terms(graw, bias, upper):
    H = ML_HEADS
    L = graw.shape[1]
    z = graw + bias
    gates = GATE_CAP * jnp.tanh(z / GATE_CAP)
    log_f = jnp.minimum(gates, 0.0) - jnp.log1p(jnp.exp(-jnp.abs(gates)))
    row = lax.broadcasted_iota(jnp.int32, (ML_GATE_ROWS, L), 0)
    lane = lax.broadcasted_iota(jnp.int32, (ML_GATE_ROWS, L), 1)
    is_i = row < H
    slab = jnp.where(is_i, gates, log_f)
    a1 = slab.astype(jnp.bfloat16)
    r1 = slab - a1.astype(jnp.float32)
    a2 = r1.astype(jnp.bfloat16)
    a3 = (r1 - a2.astype(jnp.float32)).astype(jnp.bfloat16)
    cum = _bdot(a1, upper) + (_bdot(a2, upper) + _bdot(a3, upper))
    ib = jnp.where(is_i, gates, cum)
    b = pltpu.roll(ib, ML_GATE_ROWS - H, 0)
    u = ib - b
    cm = u
    shift = 1
    while shift < L:
        cm = jnp.maximum(cm, jnp.where(lane >= shift, pltpu.roll(cm, shift, 1), -jnp.inf))
        shift *= 2
    return b, u, cm


def _mlstm_kernel(v_ref, o_ref, q_ref, kt_ref, gt_ref, gtn_ref, bg_ref, gout_ref, out_ref,
                  c_ref, m_ref, b_ref, u_ref, cm_ref, *, chunk):
    L = chunk
    H, dk, dv = ML_HEADS, ML_DK, ML_DV
    r_idx = lax.broadcasted_iota(jnp.int32, (L, L), 0)
    c_idx = lax.broadcasted_iota(jnp.int32, (L, L), 1)
    upper = jnp.where(r_idx <= c_idx, 1.0, 0.0).astype(jnp.bfloat16)
    causal = r_idx >= c_idx
    row = lax.broadcasted_iota(jnp.int32, (ML_GATE_ROWS, L), 0)
    ones_col = jnp.where(lax.broadcasted_iota(jnp.int32, (L, LANES), 1) == 0, 1.0, 0.0
                         ).astype(jnp.bfloat16)

    @pl.when(pl.program_id(1) == 0)
    def _():
        c_ref[...] = jnp.zeros_like(c_ref)
        m_ref[...] = jnp.zeros_like(m_ref)
        b0, u0, cm0 = _mlstm_gate_terms(gt_ref[...], bg_ref[...], upper)
        b_ref[...] = b0
        u_ref[...] = u0
        cm_ref[...] = cm0

    b16 = b_ref[...]
    u16 = u_ref[...]
    cm16 = cm_ref[...]
    b_n, u_n, cm_n = _mlstm_gate_terms(gtn_ref[...], bg_ref[...], upper)
    b_ref[...] = b_n
    u_ref[...] = u_n
    cm_ref[...] = cm_n

    m_prev = m_ref[:, 0:1]
    z16 = jnp.maximum(m_prev, cm16)
    w_inter16 = jnp.exp(m_prev - z16)
    e_negm16 = jnp.exp(-(b16 + z16))
    z_last = z16[:, L - 1:L]
    w_state16 = jnp.exp(u16 - z_last)
    decay16 = jnp.exp(m_prev - z_last)
    m_ref[...] = jnp.broadcast_to(b16[:, L - 1:L] + z_last, m_ref.shape)
    stacked = jnp.where(row < H, z16,
                        jnp.where(row < 2 * H, pltpu.roll(w_inter16, H, 0),
                                  pltpu.roll(e_negm16, 2 * H, 0)))
    cols = jnp.concatenate(
        [stacked, jnp.zeros((LANES - ML_GATE_ROWS, L), jnp.float32)], axis=0).T

    for h in range(H):
        u_r = u16[h:h + 1, :]
        z_c = cols[:, h:h + 1]
        w_inter = cols[:, H + h:H + h + 1]
        e_negm = cols[:, 2 * H + h:2 * H + h + 1]
        c_ext = c_ref[h]
        q = q_ref[:, h * dk:(h + 1) * dk]
        kt = kt_ref[h * dk:(h + 1) * dk, :]
        v_ext = jnp.concatenate([v_ref[:, h * dv:(h + 1) * dv], ones_col], axis=-1)

        w_intra = jnp.exp(jnp.where(causal, u_r - z_c, -jnp.inf))
        s = (_bdot(q, kt) * w_intra).astype(jnp.bfloat16)
        nd = w_inter * _bdot(q, c_ext.astype(jnp.bfloat16)) + _bdot(s, v_ext)
        den = nd[:, dv:dv + 1]
        hb = nd[:, :dv] * (1.0 / jnp.maximum(jnp.abs(den), e_negm))

        kw = (kt.astype(jnp.float32) * w_state16[h:h + 1, :]).astype(jnp.bfloat16)
        c_ref[h] = decay16[h:h + 1, :] * c_ext + _bdot(kw, v_ext)

        y = hb * lax.rsqrt(jnp.mean(hb * hb, axis=-1, keepdims=True) + EPS)
        y = y * gout_ref[:, h * dv:(h + 1) * dv]
        og = o_ref[:, h * dv:(h + 1) * dv].astype(jnp.float32)
        out_ref[:, h * dv:(h + 1) * dv] = (y * og).astype(out_ref.dtype)


def _mlstm_core(main, k_t, g_t, b_gate, g_out, bsz, seq):
    chunk = ML_CHUNK
    t = main.shape[0]
    nc = seq // chunk
    blk = lambda b, c: b * nc + c
    bg = jnp.zeros((ML_GATE_ROWS, 1), jnp.float32).at[:2 * ML_HEADS, 0].set(b_gate)
    return pl.pallas_call(
        functools.partial(_mlstm_kernel, chunk=chunk),
        grid=(bsz, nc),
        in_specs=[
            pl.BlockSpec((chunk, ML_V), lambda b, c: (blk(b, c), 0)),
            pl.BlockSpec((chunk, ML_V), lambda b, c: (blk(b, c), 1)),
            pl.BlockSpec((chunk, ML_QK), lambda b, c: (blk(b, c), 4)),
            pl.BlockSpec((ML_QK, chunk), lambda b, c: (0, blk(b, c))),
            pl.BlockSpec((ML_GATE_ROWS, chunk), lambda b, c: (0, blk(b, c))),
            pl.BlockSpec((ML_GATE_ROWS, chunk), lambda b, c: (0, blk(b, jnp.minimum(c + 1, nc - 1)))),
            pl.BlockSpec((ML_GATE_ROWS, 1), lambda b, c: (0, 0)),
            pl.BlockSpec((1, ML_V), lambda b, c: (0, 0)),
        ],
        out_specs=pl.BlockSpec((chunk, ML_V), lambda b, c: (blk(b, c), 0)),
        out_shape=jax.ShapeDtypeStruct((t, ML_V), jnp.bfloat16),
        scratch_shapes=[
            pltpu.VMEM((ML_HEADS, ML_DK, ML_DV + LANES), jnp.float32),
            pltpu.VMEM((ML_GATE_ROWS, LANES), jnp.float32),
            pltpu.VMEM((ML_GATE_ROWS, chunk), jnp.float32),
            pltpu.VMEM((ML_GATE_ROWS, chunk), jnp.float32),
            pltpu.VMEM((ML_GATE_ROWS, chunk), jnp.float32),
        ],
        compiler_params=_params(("arbitrary", "arbitrary")),
        name="mlstm_core",
    )(main, main, main, k_t, g_t, g_t, bg, g_out.reshape(1, ML_V))


def _swa_kernel(sinks_ref, q_ref, kc_ref, kp_ref, vc_ref, vp_ref, bias_ref, o_ref):
    W = SW_WINDOW
    lane = lax.broadcasted_iota(jnp.int32, (W, LANES), 1)
    low = lane < SW_DH
    row_col = lax.broadcasted_iota(jnp.int32, (SW_GROUP * W, 1), 0)
    bias = bias_ref[jnp.minimum(pl.program_id(1), 1)]
    for g in range(SW_KV_HEADS):
        sl = slice(g * LANES, (g + 1) * LANES)
        kn = jnp.concatenate([kp_ref[:, sl], kc_ref[:, sl]], axis=0)
        v2 = jnp.concatenate([vp_ref[:, sl], vc_ref[:, sl]], axis=0)
        sink = jnp.full((SW_GROUP * W, 1), sinks_ref[g * SW_GROUP + SW_GROUP - 1], jnp.float32)
        for j in range(SW_GROUP - 2, -1, -1):
            sink = jnp.where(row_col < (j + 1) * W, sinks_ref[g * SW_GROUP + j], sink)
        parts = []
        for p in range(2):
            c0 = g * SW_GROUP * SW_DH + p * LANES
            qp = q_ref[:, c0:c0 + LANES].astype(jnp.float32)
            parts.append(jnp.where(low, qp, 0.0).astype(jnp.bfloat16))
            parts.append(jnp.where(low, 0.0, qp).astype(jnp.bfloat16))
        q4 = jnp.concatenate(parts, axis=0)
        scores = _bdot_nt(q4, kn) + bias
        m = jnp.maximum(jnp.max(scores, axis=-1, keepdims=True), sink)
        pexp = jnp.exp2(scores - m)
        denom = jnp.sum(pexp, axis=-1, keepdims=True) + jnp.exp2(sink - m)
        o4 = _bdot(pexp.astype(jnp.bfloat16), v2) * (1.0 / denom)
        for p in range(2):
            oa = o4[(2 * p) * W:(2 * p + 1) * W]
            ob = o4[(2 * p + 1) * W:(2 * p + 2) * W]
            c0 = g * SW_GROUP * SW_DH + p * LANES
            o_ref[:, c0:c0 + LANES] = jnp.where(low, oa, ob).astype(o_ref.dtype)


def _swa_core(proj, sinks, bsz, seq):
    t = proj.shape[0]
    W = SW_WINDOW
    nb = seq // W
    dq = SW_Q_HEADS * SW_DH
    kv_w = SW_KV_HEADS * LANES
    k_blk = dq // kv_w
    v_blk = k_blk + 1
    cur = lambda b, n, s: b * nb + n
    prev = lambda b, n, s: b * nb + jnp.maximum(n - 1, 0)
    sinks = sinks.astype(jnp.float32) * LOG2E
    qi = (jnp.arange(SW_GROUP * W) % W)[:, None]
    ki = jnp.arange(2 * W)[None, :]
    rel = qi + W - ki
    in_win = (rel >= 0) & (rel < W)
    bias = jnp.stack([jnp.where(in_win & (ki >= W), 0.0, -jnp.inf),
                      jnp.where(in_win, 0.0, -jnp.inf)]).astype(jnp.float32)
    grid_spec = pltpu.PrefetchScalarGridSpec(
        num_scalar_prefetch=1,
        grid=(bsz, nb),
        in_specs=[
            pl.BlockSpec((W, dq), lambda b, n, s: (cur(b, n, s), 0)),
            pl.BlockSpec((W, kv_w), lambda b, n, s: (cur(b, n, s), k_blk)),
            pl.BlockSpec((W, kv_w), lambda b, n, s: (prev(b, n, s), k_blk)),
            pl.BlockSpec((W, kv_w), lambda b, n, s: (cur(b, n, s), v_blk)),
            pl.BlockSpec((W, kv_w), lambda b, n, s: (prev(b, n, s), v_blk)),
            pl.BlockSpec((2, SW_GROUP * W, 2 * W), lambda b, n, s: (0, 0, 0)),
        ],
        out_specs=pl.BlockSpec((W, dq), lambda b, n, s: (cur(b, n, s), 0)),
    )
    return pl.pallas_call(
        _swa_kernel,
        grid_spec=grid_spec,
        out_shape=jax.ShapeDtypeStruct((t, dq), jnp.bfloat16),
        compiler_params=_params(("arbitrary", "arbitrary")),
        name="swa_core",
    )(sinks, proj, proj, proj, proj, proj, bias)


def _pack_rounded_pairs(r):
    k = r.shape[1] // 2
    hi = lax.bitcast_convert_type(r[:, :k], jnp.uint32)
    lo = lax.bitcast_convert_type(r[:, k:], jnp.uint32)
    return hi | (lo >> 16)


def _pack_bf16_pairs(a):
    return _pack_rounded_pairs(a.astype(jnp.bfloat16).astype(jnp.float32))


def _unpack_bf16_pairs(u):
    hi = lax.bitcast_convert_type(u & jnp.uint32(0xFFFF0000), jnp.float32)
    lo = lax.bitcast_convert_type(u << 16, jnp.float32)
    return hi, lo


def _route_tile(x_new, g_ref, sc_ref, sh_ref, wh_ref, wl_ref, b_ref,
                hn_ref, info_ref, infot_ref, cnt_ref, carry_ref, earlier_ref, tm):
    hn = _modulated_norm(x_new, g_ref[...], sc_ref[...], sh_ref[...])
    h_hi = hn.astype(jnp.bfloat16)
    hi_f32 = h_hi.astype(jnp.float32)
    h_lo = (hn - hi_f32).astype(jnp.bfloat16)
    hn_ref[...] = _pack_rounded_pairs(hi_f32)
    wide = _bdot_nt(wl_ref[...], h_hi)
    logits = (wide[:ROUTER_LANES] + (_bdot_nt(wh_ref[...], h_lo) + wide[ROUTER_LANES:])
              + b_ref[...])
    E8 = EXPERTS_PER_GROUP
    sub = lax.broadcasted_iota(jnp.int32, (E8, tm), 0).astype(jnp.float32)
    big = float(ROUTER_LANES)
    neg = -jnp.inf

    gl = logits[N_EXPERTS:N_EXPERTS + N_GROUPS]
    gmax = jnp.max(gl, axis=0, keepdims=True)
    gsel = jnp.min(jnp.where(gl == gmax, sub, big), axis=0, keepdims=True)
    p_grp = 1.0 / jnp.sum(jnp.exp(gl - gmax), axis=0, keepdims=True)

    el = logits[0:E8]
    for grp in range(1, N_GROUPS):
        el = jnp.where(gsel == grp, logits[grp * E8:(grp + 1) * E8], el)
    v1 = jnp.max(el, axis=0, keepdims=True)
    j1 = jnp.min(jnp.where(el == v1, sub, big), axis=0, keepdims=True)
    el2 = jnp.where(sub == j1, neg, el)
    v2 = jnp.max(el2, axis=0, keepdims=True)
    j2 = jnp.min(jnp.where(el2 == v2, sub, big), axis=0, keepdims=True)
    i1 = gsel * E8 + j1
    i2 = gsel * E8 + j2
    e21 = jnp.exp(v2 - v1)
    gate1 = p_grp / (1.0 + e21)
    gate2 = p_grp * e21 / (1.0 + e21)

    erow = lax.broadcasted_iota(jnp.int32, (N_EXPERTS, tm), 0).astype(jnp.float32)
    hit1 = erow == i1
    hit2 = erow == i2
    onehot = jnp.where(hit1 | hit2, 1.0, 0.0)
    carry = carry_ref[:, 0:1]
    before = _bdot(onehot.astype(jnp.bfloat16), earlier_ref[...]) + carry
    rank1 = jnp.sum(jnp.where(hit1, before, 0.0), axis=0, keepdims=True)
    rank2 = jnp.sum(jnp.where(hit2, before, 0.0), axis=0, keepdims=True)
    total = carry + jnp.sum(onehot, axis=1, keepdims=True)
    carry_ref[...] = jnp.broadcast_to(total, carry_ref.shape)
    cnt_ref[...] = jnp.broadcast_to(total, cnt_ref.shape)

    info_t = jnp.where(sub == 0, i1, 0.0)
    info_t = jnp.where(sub == 1, i2, info_t)
    info_t = jnp.where(sub == 2, rank1, info_t)
    info_t = jnp.where(sub == 3, rank2, info_t)
    info_t = jnp.where(sub == 4, gate1, info_t)
    info_t = jnp.where(sub == 5, gate2, info_t)
    infot_ref[...] = info_t
    info_ref[...] = jnp.concatenate(
        [info_t, jnp.zeros((ROUTER_LANES - E8, tm), jnp.float32)], axis=0).T


def _router_kernel(a_ref, wo_ref, x_ref, gt_ref, g_ref, sc_ref, sh_ref, wh_ref, wl_ref, b_ref,
                   xo_ref, hn_ref, info_ref, infot_ref, cnt_ref, carry_ref, earlier_ref, *, tm):
    @pl.when(pl.program_id(0) == 0)
    def _():
        carry_ref[...] = jnp.zeros_like(carry_ref)
        r_idx = lax.broadcasted_iota(jnp.int32, (tm, tm), 0)
        c_idx = lax.broadcasted_iota(jnp.int32, (tm, tm), 1)
        earlier_ref[...] = jnp.where(r_idx < c_idx, 1.0, 0.0).astype(jnp.bfloat16)

    x_new = x_ref[...] + gt_ref[...] * _bdot(a_ref[...], wo_ref[...])
    xo_ref[...] = x_new
    _route_tile(x_new, g_ref, sc_ref, sh_ref, wh_ref, wl_ref, b_ref,
                hn_ref, info_ref, infot_ref, cnt_ref, carry_ref, earlier_ref, tm)


def _outproj_router(a, w_out, x2d, gt, g, sc, sh, w_hi, w_lo, bias, seq):
    t, d = x2d.shape
    tm = TOKEN_TILE
    tiles_per_batch = seq // tm
    per_batch = lambda i: (i // tiles_per_batch, 0, 0)
    const = lambda i: (0, 0)
    return pl.pallas_call(
        functools.partial(_router_kernel, tm=tm),
        grid=(t // tm,),
        in_specs=[
            pl.BlockSpec((tm, a.shape[1]), lambda i: (i, 0)),
            pl.BlockSpec(w_out.shape, const),
            pl.BlockSpec((tm, d), lambda i: (i, 0)),
            pl.BlockSpec((None, 1, d), per_batch),
            pl.BlockSpec((1, d), const),
            pl.BlockSpec((None, 1, d), per_batch),
            pl.BlockSpec((None, 1, d), per_batch),
            pl.BlockSpec((ROUTER_LANES, d), const),
            pl.BlockSpec((2 * ROUTER_LANES, d), const),
            pl.BlockSpec((ROUTER_LANES, 1), const),
        ],
        out_specs=[
            pl.BlockSpec((tm, d), lambda i: (i, 0)),
            pl.BlockSpec((tm, d // 2), lambda i: (i, 0)),
            pl.BlockSpec((tm, ROUTER_LANES), lambda i: (i, 0)),
            pl.BlockSpec((EXPERTS_PER_GROUP, tm), lambda i: (0, i)),
            pl.BlockSpec((N_EXPERTS, LANES), const),
        ],
        out_shape=[
            jax.ShapeDtypeStruct((t, d), jnp.float32),
            jax.ShapeDtypeStruct((t, d // 2), jnp.uint32),
            jax.ShapeDtypeStruct((t, ROUTER_LANES), jnp.float32),
            jax.ShapeDtypeStruct((EXPERTS_PER_GROUP, t), jnp.float32),
            jax.ShapeDtypeStruct((N_EXPERTS, LANES), jnp.float32),
        ],
        scratch_shapes=[pltpu.VMEM((N_EXPERTS, LANES), jnp.float32),
                        pltpu.VMEM((tm, tm), jnp.bfloat16)],
        compiler_params=_params(("arbitrary",)),
        name="outproj_router",
    )(a, w_out, x2d, gt, g.reshape(1, d), sc, sh, w_hi, jnp.concatenate([w_hi, w_lo], axis=0), bias)


def _sc_mesh():
    return plsc.VectorSubcoreMesh(core_axis_name="c", subcore_axis_name="s")


def _sc_dispatch(rows, d0, d1, pad_idx, n_slots):
    t, w = rows.shape
    n_pad = pad_idx.shape[1]
    zeros = jnp.zeros((SC_ROWS, w), rows.dtype)
    sem = (pltpu.PARALLEL, pltpu.ARBITRARY)
    parts = SC_WINDOW // SC_ROWS

    @pl.kernel(out_type=jax.ShapeDtypeStruct((n_slots + SC_WINDOW, w), rows.dtype), mesh=_sc_mesh(),
               scratch_types=[pltpu.SemaphoreType.DMA, pltpu.SemaphoreType.DMA])
    def dispatch(x_hbm, d0_hbm, d1_hbm, z_hbm, p_hbm, o_hbm, sem0, sem1):
        def scatter_rows(x_vmem, i0_vmem, i1_vmem):
            part = pl.ds(pl.program_id(1) * SC_ROWS, SC_ROWS)
            first = pltpu.async_copy(x_vmem, o_hbm.at[i0_vmem.at[0, part]], sem0)
            second = pltpu.async_copy(x_vmem, o_hbm.at[i1_vmem.at[0, part]], sem1)
            first.wait()
            second.wait()

        pltpu.emit_pipeline(
            scatter_rows,
            grid=(t // SC_WINDOW, parts),
            in_specs=[pl.BlockSpec((SC_ROWS, w), lambda i, j: (parts * i + j, 0)),
                      pl.BlockSpec((1, SC_WINDOW), lambda i, j: (0, i)),
                      pl.BlockSpec((1, SC_WINDOW), lambda i, j: (0, i))],
            out_specs=[],
            core_axis_name=("c", "s"),
            dimension_semantics=sem,
        )(x_hbm, d0_hbm, d1_hbm)

        def scatter_zeros(z_vmem, p_vmem):
            part = pl.ds(pl.program_id(1) * SC_ROWS, SC_ROWS)
            pltpu.sync_copy(z_vmem, o_hbm.at[p_vmem.at[0, part]])

        pltpu.emit_pipeline(
            scatter_zeros,
            grid=(n_pad // SC_WINDOW, parts),
            in_specs=[pl.BlockSpec((SC_ROWS, w), lambda i, j: (0, 0)),
                      pl.BlockSpec((1, SC_WINDOW), lambda i, j: (0, i))],
            out_specs=[],
            core_axis_name=("c", "s"),
            dimension_semantics=sem,
        )(z_hbm, p_hbm)

    return dispatch(rows, d0, d1, zeros, pad_idx)


def _sc_gather(src, idx):
    n_out = idx.shape[1]
    w = src.shape[1]
    parts = SC_WINDOW // SC_ROWS

    @pl.kernel(out_type=jax.ShapeDtypeStruct((n_out, w), src.dtype), mesh=_sc_mesh())
    def gather(x_hbm, i_hbm, o_hbm):
        def gather_rows(i_vmem, o_vmem):
            part = pl.ds(pl.program_id(1) * SC_ROWS, SC_ROWS)
            pltpu.sync_copy(x_hbm.at[i_vmem.at[0, part]], o_vmem)

        pltpu.emit_pipeline(
            gather_rows,
            grid=(n_out // SC_WINDOW, parts),
            in_specs=[pl.BlockSpec((1, SC_WINDOW), lambda i, j: (0, i))],
            out_specs=[pl.BlockSpec((SC_ROWS, w), lambda i, j: (parts * i + j, 0))],
            core_axis_name=("c", "s"),
            dimension_semantics=(pltpu.PARALLEL, pltpu.ARBITRARY),
        )(i_hbm, o_hbm)

    return gather(src, idx)


def _expert_kernel(meta_ref, x_ref, w1_hbm, w3_hbm, w2_hbm, y_ref,
                   w1_buf, w3_buf, w2_buf, w1_c, w3_c, w2_c, sems, *, layer, nb):
    def weight_copies(expert, s):
        return (pltpu.make_async_copy(w1_hbm.at[layer, expert], w1_buf.at[s], sems.at[s, 0]),
                pltpu.make_async_copy(w3_hbm.at[layer, expert], w3_buf.at[s], sems.at[s, 1]),
                pltpu.make_async_copy(w2_hbm.at[layer, expert], w2_buf.at[s], sems.at[s, 2]))

    @pl.when(pl.program_id(0) == 0)
    def _():
        for cp in weight_copies(meta_ref[0], meta_ref[nb]):
            cp.start()

    for j in range(EXPERT_BLOCKS_PER_STEP):
        i = pl.program_id(0) * EXPERT_BLOCKS_PER_STEP + j
        rows = slice(j * MOE_BLOCK, (j + 1) * MOE_BLOCK)
        e = meta_ref[i]
        slot = meta_ref[nb + i]
        nxt = meta_ref[2 * nb + i]
        used = i < meta_ref[3 * nb]
        first = used & ((i == 0) | (e != meta_ref[jnp.maximum(i - 1, 0)]))

        @pl.when(first)
        def _():
            for cp in weight_copies(e, slot):
                cp.wait()

            @pl.when(nxt >= 0)
            def _():
                for cp in weight_copies(nxt, 1 - slot):
                    cp.start()

            w1_c[...] = w1_buf[slot].astype(jnp.bfloat16)
            w3_c[...] = w3_buf[slot].astype(jnp.bfloat16)
            w2_c[...] = w2_buf[slot].astype(jnp.bfloat16)

        @pl.when(used)
        def _():
            x_hi, x_lo = _unpack_bf16_pairs(x_ref[rows, :])
            xb = jnp.concatenate([x_hi, x_lo], axis=-1).astype(jnp.bfloat16)
            full = 256
            tail = jnp.concatenate([w1_c[:, full:], w3_c[:, full:]], axis=-1)
            half_rows = MOE_BLOCK // 2
            h_tail = jnp.concatenate([_bdot(xb[:half_rows], tail), _bdot(xb[half_rows:], tail)], axis=0)
            h1 = jnp.concatenate([_bdot(xb, w1_c[:, :full]), h_tail[:, :LANES]], axis=-1)
            h3 = jnp.concatenate([_bdot(xb, w3_c[:, :full]), h_tail[:, LANES:]], axis=-1)
            act = (h1 * jax.nn.sigmoid(h1) * h3).astype(jnp.bfloat16)
            y_ref[rows, :] = _pack_bf16_pairs(_bdot(act, w2_c[...]))

        @pl.when(jnp.logical_not(used))
        def _():
            y_ref[rows, :] = jnp.zeros((MOE_BLOCK, y_ref.shape[1]), y_ref.dtype)


def _experts(x_slots, blk_meta, w1, w3, w2, layer, nb):
    dp = x_slots.shape[1]
    d, de = w1.shape[-2:]
    step_rows = EXPERT_BLOCKS_PER_STEP * MOE_BLOCK
    assert nb % EXPERT_BLOCKS_PER_STEP == 0
    last_used_step = lambda s: (s[3 * nb] - 1) // EXPERT_BLOCKS_PER_STEP
    grid_spec = pltpu.PrefetchScalarGridSpec(
        num_scalar_prefetch=1,
        grid=(nb // EXPERT_BLOCKS_PER_STEP,),
        in_specs=[
            pl.BlockSpec((step_rows, dp), lambda i, s: (jnp.minimum(i, last_used_step(s)), 0)),
            pl.BlockSpec(memory_space=pl.ANY),
            pl.BlockSpec(memory_space=pl.ANY),
            pl.BlockSpec(memory_space=pl.ANY),
        ],
        out_specs=pl.BlockSpec((step_rows, dp), lambda i, s: (i, 0)),
        scratch_shapes=[
            pltpu.VMEM((2, d, de), jnp.float32),
            pltpu.VMEM((2, d, de), jnp.float32),
            pltpu.VMEM((2, de, d), jnp.float32),
            pltpu.VMEM((d, de), jnp.bfloat16),
            pltpu.VMEM((d, de), jnp.bfloat16),
            pltpu.VMEM((de, d), jnp.bfloat16),
            pltpu.SemaphoreType.DMA((2, 3)),
        ],
    )
    return pl.pallas_call(
        functools.partial(_expert_kernel, layer=layer, nb=nb),
        grid_spec=grid_spec,
        out_shape=jax.ShapeDtypeStruct((nb * MOE_BLOCK, dp), jnp.uint32),
        compiler_params=_params(("arbitrary",)),
        name="moe_experts",
    )(blk_meta, x_slots, w1, w3, w2)


def _combine_kernel(x_ref, y1_ref, y2_ref, info_ref, gt_ref, o_ref):
    o_ref[...] = _moe_combined(x_ref[...], y1_ref, y2_ref, info_ref, gt_ref)


def _combine(x2d, y_pairs, info, gt, seq):
    t, d = x2d.shape
    tm = min(COMBINE_TILE, seq)
    tiles_per_batch = seq // tm
    second = t // tm
    return pl.pallas_call(
        _combine_kernel,
        grid=(t // tm,),
        in_specs=[
            pl.BlockSpec((tm, d), lambda i: (i, 0)),
            pl.BlockSpec((tm, d // 2), lambda i: (i, 0)),
            pl.BlockSpec((tm, d // 2), lambda i: (i + second, 0)),
            pl.BlockSpec((tm, ROUTER_LANES), lambda i: (i, 0)),
            pl.BlockSpec((None, 1, d), lambda i: (i // tiles_per_batch, 0, 0)),
        ],
        out_specs=pl.BlockSpec((tm, d), lambda i: (i, 0)),
        out_shape=jax.ShapeDtypeStruct((t, d), jnp.float32),
        compiler_params=_params(("arbitrary",)),
        name="moe_combine",
    )(x2d, y_pairs, y_pairs, info, gt)


def _slot_plan(info_t, cnt, t):
    counts = cnt[:, 0].astype(jnp.int32)
    padded = (counts + MOE_BLOCK - 1) // MOE_BLOCK * MOE_BLOCK
    pad_ends = jnp.cumsum(padded)
    pad_starts = pad_ends - padded
    nb = -(-(2 * t) // MOE_BLOCK) + N_EXPERTS
    n_slots = nb * MOE_BLOCK
    it = info_t.astype(jnp.int32)
    onehot_start = lambda e: jnp.sum(
        jnp.where(e[None, :] == jnp.arange(N_EXPERTS, dtype=jnp.int32)[:, None],
                  pad_starts[:, None], 0), axis=0)
    dest1 = (onehot_start(it[0]) + it[2]).reshape(1, t)
    dest2 = (onehot_start(it[1]) + it[3]).reshape(1, t)
    lane = jnp.arange(MOE_BLOCK, dtype=jnp.int32)[None, :]
    n_padding = (padded - counts)[:, None]
    wrapped = (pad_starts + counts)[:, None] + lane % jnp.maximum(n_padding, 1)
    pad_idx = jnp.where(n_padding > 0, wrapped, n_slots + lane % SC_WINDOW).reshape(-1)
    n_real = pad_ends[-1] // MOE_BLOCK
    n_used = -(-n_real // EXPERT_BLOCKS_PER_STEP) * EXPERT_BLOCKS_PER_STEP
    tail = jnp.arange((EXPERT_BLOCKS_PER_STEP - 1) * MOE_BLOCK, dtype=jnp.int32)
    tail_idx = jnp.where(tail < (n_used - n_real) * MOE_BLOCK, pad_ends[-1] + tail,
                         n_slots + tail % SC_WINDOW)
    pad_idx = jnp.concatenate([pad_idx, tail_idx]).reshape(1, -1)
    experts = jnp.arange(N_EXPERTS, dtype=jnp.int32)
    blk = jnp.arange(nb, dtype=jnp.int32)
    blk_exp = jnp.minimum(
        jnp.sum((pad_ends[None, :] <= (blk * MOE_BLOCK)[:, None]).astype(jnp.int32), axis=1),
        N_EXPERTS - 1)
    blk_exp = jnp.where(blk >= n_real, jnp.max(jnp.where(padded > 0, experts, 0)), blk_exp)
    prev_exp = jnp.concatenate([jnp.full((1,), -1, jnp.int32), blk_exp[:-1]])
    is_first = (blk < n_used) & (blk_exp != prev_exp)
    blk_slot = (jnp.cumsum(is_first.astype(jnp.int32)) + 1) % 2
    later = (experts[None, :] > experts[:, None]) & (padded[None, :] > 0)
    nxt_of = jnp.min(jnp.where(later, experts[None, :], N_EXPERTS), axis=1)
    nxt_of = jnp.where(nxt_of == N_EXPERTS, -1, nxt_of)
    blk_nxt = jnp.sum(jnp.where(blk_exp[:, None] == experts[None, :], nxt_of[None, :], 0), axis=1)
    blk_meta = jnp.concatenate([blk_exp, blk_slot, blk_nxt, n_used[None]]).astype(jnp.int32)
    return dest1, dest2, pad_idx, blk_meta, nb, n_slots


def _mixer_out_and_moe(a, w_out, x2d, gt1, g, sc, sh, w_group, b_group, w_router, b_router,
                       w1, w3, w2, layer, seq):
    t, d = x2d.shape
    w_cat = jnp.zeros((ROUTER_LANES, d), jnp.float32)
    w_cat = w_cat.at[:N_EXPERTS].set(w_router.T).at[N_EXPERTS:N_EXPERTS + N_GROUPS].set(w_group.T)
    b_cat = jnp.zeros((ROUTER_LANES, 1), jnp.float32)
    b_cat = b_cat.at[:N_EXPERTS, 0].set(b_router).at[N_EXPERTS:N_EXPERTS + N_GROUPS, 0].set(b_group)
    w_hi, w_lo = _split_hi_lo(w_cat)
    x_new, hn, info, info_t, cnt = _outproj_router(a, w_out, x2d, gt1, g, sc, sh, w_hi, w_lo,
                                                   b_cat, seq)
    dest1, dest2, pad_idx, blk_meta, nb, n_slots = _slot_plan(info_t, cnt, t)
    x_slots = _sc_dispatch(hn, dest1, dest2, pad_idx, n_slots)
    y_slots = _experts(x_slots, blk_meta, w1, w3, w2, layer, nb)
    y_pairs = _sc_gather(y_slots, jnp.concatenate([dest1, dest2], axis=1))
    return x_new, y_pairs, info


def kernel(x, c, w_ada, b_ada, norm1_g, norm2_g, ml_w_in, ml_b_gate, ml_g_out, ml_w_out,
           sw_w_in, sw_g_q, sw_g_k, sw_sinks, sw_w_out, moe_w_group, moe_b_group,
           moe_w_router, moe_b_router, moe_w1, moe_w3, moe_w2):
    bsz, seq, d = x.shape
    depth = w_ada.shape[0]
    bf = jnp.bfloat16
    mod = _ada_mod(c, w_ada, b_ada)
    x2d = x.reshape(bsz * seq, d)
    pending = None
    for layer in range(depth):
        sh1, sc1, gt1, sh2, sc2, gt2 = [
            mod[layer, :, i * d:(i + 1) * d].reshape(bsz, 1, d) for i in range(6)]
        j = layer // 2
        if layer % 2 == 0:
            w = ml_w_in[j]
            q_w, k_w = w[:, :ML_QK], w[:, ML_QK:2 * ML_QK]
            v_w = w[:, 2 * ML_QK:2 * ML_QK + ML_V]
            o_w = w[:, 2 * ML_QK + ML_V:2 * ML_QK + 2 * ML_V]
            g_w = w[:, 2 * ML_QK + 2 * ML_V:]
            w_main = jnp.concatenate([v_w, o_w, q_w], axis=1).astype(bf)
            wg_t = jnp.zeros((ML_GATE_ROWS, d), jnp.float32).at[:2 * ML_HEADS].set(g_w.T)
            wg_hi, wg_lo = _split_hi_lo(wg_t)
            wk_t = k_w.T.astype(bf)
            w_out = ml_w_out[j].astype(bf)
            outs = _inproj(x2d, norm1_g[layer], sc1, sh1, w_main, seq,
                           ml_extra=(wk_t, wg_hi, wg_lo),
                           q_cols=(2 * ML_V, 2 * ML_V + ML_QK), q_scale=ML_DK ** -0.5,
                           gate_cols=(ML_V, 2 * ML_V), pending=pending)
            if pending is not None:
                x2d, outs = outs[0], outs[1:]
            main, k_t, g_t = outs
            a = _mlstm_core(main, k_t, g_t, ml_b_gate[j], ml_g_out[j], bsz, seq)
        else:
            w = sw_w_in[j]
            dq = SW_Q_HEADS * SW_DH
            dkv = SW_KV_HEADS * SW_DH
            dup = lambda m: jnp.concatenate(
                [m.reshape(d, SW_KV_HEADS, 1, SW_DH)] * 2, axis=2).reshape(d, 2 * dkv)
            w_main = jnp.concatenate(
                [w[:, :dq], dup(w[:, dq:dq + dkv]), dup(w[:, dq + dkv:])], axis=1).astype(bf)
            w_out = sw_w_out[j].astype(bf)
            gq = jnp.concatenate([sw_g_q[j], sw_g_q[j]]).reshape(1, LANES) * (SW_DH ** -0.5 * LOG2E)
            gk = jnp.concatenate([sw_g_k[j], sw_g_k[j]]).reshape(1, LANES)
            outs = _inproj(x2d, norm1_g[layer], sc1, sh1, w_main, seq, pending=pending,
                           qk_norm=(dq, dq + 2 * dkv, gq, gk))
            if pending is not None:
                x2d, outs = outs[0], outs[1:]
            a = _swa_core(outs[0], sw_sinks[j], bsz, seq)
        x2d, y_pairs, info = _mixer_out_and_moe(
            a, w_out, x2d, gt1, norm2_g[layer], sc2, sh2, moe_w_group[layer], moe_b_group[layer],
            moe_w_router[layer], moe_b_router[layer], moe_w1, moe_w3, moe_w2, layer, seq)
        pending = (y_pairs, info, gt2)
    y_pairs, info, gt2 = pending
    return _combine(x2d, y_pairs, info, gt2, seq).reshape(bsz, seq, d)
```

```python
import functools

import jax
import jax.numpy as jnp
from jax import lax
from jax.experimental import pallas as pl
from jax.experimental.pallas import tpu as pltpu
from jax.experimental.pallas import tpu_sc as plsc

EPS = 1e-6
GATE_CAP = 15.0
LOG2E = 1.4426950408889634

ML_HEADS = 4
ML_DK = 128
ML_DV = 256
ML_QK = ML_HEADS * ML_DK
ML_V = ML_HEADS * ML_DV
ML_GATE_ROWS = 16

SW_Q_HEADS = 16
SW_KV_HEADS = 4
SW_GROUP = SW_Q_HEADS // SW_KV_HEADS
SW_DH = 64
SW_WINDOW = 128
LANES = 128

N_GROUPS = 8
EXPERTS_PER_GROUP = 8
N_EXPERTS = N_GROUPS * EXPERTS_PER_GROUP
MOE_BLOCK = 256
ROUTER_LANES = 128
SC_WINDOW = 128
SC_ROWS = 64
EXPERT_BLOCKS_PER_STEP = 4

SUBLANES = 8
TOKEN_TILE = 512
ROUTER_TILE = 1024
COMBINE_TILE = 1024
PROJ_COL_CHUNK = 1024
ML_CHUNK = 256
ADA_COL_TILE = 768
V7X_VMEM_BYTES = 64 * 1024 * 1024
VMEM_LIMIT = V7X_VMEM_BYTES - 8 * 1024 * 1024

_NT = (((1,), (1,)), ((), ()))


def _bdot(a, b):
    return jnp.dot(a, b, preferred_element_type=jnp.float32)


def _bdot_nt(a, b):
    return lax.dot_general(a, b, _NT, preferred_element_type=jnp.float32)


def _split_hi_lo(a):
    hi = a.astype(jnp.bfloat16)
    lo = (a - hi.astype(jnp.float32)).astype(jnp.bfloat16)
    return hi, lo


def _params(sem):
    return pltpu.CompilerParams(dimension_semantics=sem, vmem_limit_bytes=VMEM_LIMIT)


def _ada_kernel(c_ref, w_ref, b_ref, o_ref):
    c = c_ref[...]
    cond = c * jax.nn.sigmoid(c)
    c_hi, c_lo = _split_hi_lo(cond)
    w_hi, w_lo = _split_hi_lo(w_ref[...])
    acc = _bdot(c_hi, w_hi) + (_bdot(c_lo, w_hi) + _bdot(c_hi, w_lo))
    o_ref[...] = acc + b_ref[...]


def _ada_mod(c, w_ada, b_ada):
    depth, d, n = w_ada.shape
    bsz = c.shape[0]
    rows = SUBLANES
    tn = ADA_COL_TILE
    c_pad = jnp.zeros((rows, d), jnp.float32).at[:bsz].set(c)
    out = pl.pallas_call(
        _ada_kernel,
        grid=(depth, n // tn),
        in_specs=[
            pl.BlockSpec((rows, d), lambda l, j: (0, 0)),
            pl.BlockSpec((None, d, tn), lambda l, j: (l, 0, j)),
            pl.BlockSpec((None, 1, tn), lambda l, j: (l, 0, j)),
        ],
        out_specs=pl.BlockSpec((None, rows, tn), lambda l, j: (l, 0, j)),
        out_shape=jax.ShapeDtypeStruct((depth, rows, n), jnp.float32),
        compiler_params=_params(("arbitrary", "arbitrary")),
        name="ada_mod",
    )(c_pad, w_ada, b_ada.reshape(depth, 1, n))
    return out[:, :bsz]


def _modulated_norm(x, g, sc, sh):
    y = x * lax.rsqrt(jnp.mean(x * x, axis=-1, keepdims=True) + EPS)
    return y * (g * (1.0 + sc)) + sh


def _moe_combined(x, y1_ref, y2_ref, info_ref, gt_ref):
    info = info_ref[...]
    g1 = info[:, 4:5]
    g2 = info[:, 5:6]
    y1_hi, y1_lo = _unpack_bf16_pairs(y1_ref[...])
    y2_hi, y2_lo = _unpack_bf16_pairs(y2_ref[...])
    y = jnp.concatenate([g1 * y1_hi + g2 * y2_hi, g1 * y1_lo + g2 * y2_lo], axis=-1)
    return x + gt_ref[...] * y


def _qk_head_norm(acc, c0, qk_norm, gq, gk):
    q_hi, k_hi = qk_norm
    low = lax.broadcasted_iota(jnp.int32, (acc.shape[0], LANES), 1) < SW_DH
    slabs = []
    for j in range(acc.shape[1] // LANES):
        slab = acc[:, j * LANES:(j + 1) * LANES]
        if c0 + j * LANES < q_hi:
            sq = slab * slab
            ss_lo = jnp.sum(jnp.where(low, sq, 0.0), axis=-1, keepdims=True)
            ss_hi = jnp.sum(jnp.where(low, 0.0, sq), axis=-1, keepdims=True)
            rs = jnp.where(low, lax.rsqrt(ss_lo / SW_DH + EPS), lax.rsqrt(ss_hi / SW_DH + EPS))
            slab = slab * rs * gq
        elif c0 + j * LANES < k_hi:
            slab = slab * lax.rsqrt(jnp.mean(slab * slab, axis=-1, keepdims=True) + EPS) * gk
        slabs.append(slab)
    return jnp.concatenate(slabs, axis=-1)


def _inproj_kernel(*refs, n_main, chunk, q_cols, q_scale, gate_cols, qk_norm, with_ml,
                   with_combine):
    refs = list(refs)
    n_in = (5 + (4 if with_combine else 0) + (2 if with_ml else 0)
            + (2 if qk_norm is not None else 0))
    n_scratch = 4 if with_ml else 2
    ins, outs, scratch = refs[:n_in], refs[n_in:-n_scratch], refs[-n_scratch:]
    x_ref = ins.pop(0)
    if with_combine:
        y1_ref, y2_ref, info_ref, gtp_ref = ins[:4]
        ins = ins[4:]
        xo_ref = outs.pop(0)
    g_ref, sc_ref, sh_ref, w_ref = ins[:4]
    o_ref = outs[0]
    if with_ml:
        wk_ref, wgh_ref = ins[4:]
        kt_ref, gt_ref = outs[1:]
    if qk_norm is not None:
        gq_ref, gk_ref = ins[4:]

    def normalise(hb_dst, lo_dst):
        x = x_ref[...]
        if with_combine:
            x = _moe_combined(x, y1_ref, y2_ref, info_ref, gtp_ref)
            xo_ref[...] = x
        hn = _modulated_norm(x, g_ref[...], sc_ref[...], sh_ref[...])
        hb = hn.astype(jnp.bfloat16)
        hb_dst[...] = hb
        if with_ml:
            lo_dst[...] = (hn - hb.astype(jnp.float32)).astype(jnp.bfloat16)

    def project(hb_src, lo_src):
        hb = hb_src[...]
        for c0 in range(0, n_main, chunk):
            c1 = min(c0 + chunk, n_main)
            acc = _bdot(hb, w_ref[:, c0:c1])
            if q_cols is not None and q_cols[0] <= c0 < q_cols[1]:
                acc = acc * q_scale
            if gate_cols is not None and gate_cols[0] <= c0 < gate_cols[1]:
                acc = jax.nn.sigmoid(acc)
            if qk_norm is not None and c0 < qk_norm[1]:
                acc = _qk_head_norm(acc, c0, qk_norm, gq_ref[...], gk_ref[...])
            o_ref[:, c0:c1] = acc.astype(o_ref.dtype)
        if with_ml:
            nk = kt_ref.shape[0]
            stacked = _bdot_nt(wk_ref[...], hb)
            kt_ref[...] = stacked[:nk].astype(kt_ref.dtype)
            gt_ref[...] = (stacked[nk:nk + ML_GATE_ROWS]
                           + (_bdot_nt(wgh_ref[...], lo_src[...]) + stacked[nk + ML_GATE_ROWS:]))

    hb_a, hb_b = scratch[:2]
    lo_a, lo_b = scratch[2:] if with_ml else (None, None)
    s = pl.program_id(0)

    @pl.when(s == 0)
    def _():
        hb_b[...] = jnp.zeros_like(hb_b)
        if with_ml:
            lo_b[...] = jnp.zeros_like(lo_b)

    @pl.when(s % 2 == 0)
    def _():
        normalise(hb_a, lo_a)
        project(hb_b, lo_b)

    @pl.when(s % 2 == 1)
    def _():
        normalise(hb_b, lo_b)
        project(hb_a, lo_a)


def _inproj(x2d, g, sc, sh, w_main, seq, *, ml_extra=None, q_cols=None, q_scale=1.0,
            gate_cols=None, qk_norm=None, pending=None):
    t, d = x2d.shape
    tm = TOKEN_TILE
    n_main = w_main.shape[1]
    tiles_per_batch = seq // tm
    n_tiles = t // tm
    norm_tile = lambda s: jnp.minimum(s, n_tiles - 1)
    proj_tile = lambda s: jnp.maximum(s - 1, 0)
    row = lambda s: (norm_tile(s), 0)
    per_batch = lambda s: (norm_tile(s) // tiles_per_batch, 0, 0)
    const = lambda s: (0, 0)
    in_specs = [pl.BlockSpec((tm, d), row)]
    args = [x2d]
    out_specs, out_shape = [], []
    if pending is not None:
        y_pairs, info, gt_prev = pending
        in_specs += [pl.BlockSpec((tm, d // 2), row),
                     pl.BlockSpec((tm, d // 2), lambda s: (norm_tile(s) + n_tiles, 0)),
                     pl.BlockSpec((tm, ROUTER_LANES), row),
                     pl.BlockSpec((None, 1, d), per_batch)]
        args += [y_pairs, y_pairs, info, gt_prev]
        out_specs += [pl.BlockSpec((tm, d), row)]
        out_shape += [jax.ShapeDtypeStruct((t, d), jnp.float32)]
    in_specs += [
        pl.BlockSpec((1, d), const),
        pl.BlockSpec((None, 1, d), per_batch),
        pl.BlockSpec((None, 1, d), per_batch),
        pl.BlockSpec((d, n_main), const),
    ]
    args += [g.reshape(1, d), sc, sh, w_main]
    out_specs += [pl.BlockSpec((tm, n_main), lambda s: (proj_tile(s), 0))]
    out_shape += [jax.ShapeDtypeStruct((t, n_main), jnp.bfloat16)]
    scratch = [pltpu.VMEM((tm, d), jnp.bfloat16), pltpu.VMEM((tm, d), jnp.bfloat16)]
    if ml_extra is not None:
        wk_t, wg_hi, wg_lo = ml_extra
        nk = wk_t.shape[0]
        stacked = jnp.concatenate([wk_t, wg_hi, wg_lo], axis=0)
        in_specs += [pl.BlockSpec(stacked.shape, const),
                     pl.BlockSpec(wg_hi.shape, const)]
        args += [stacked, wg_hi]
        out_specs += [pl.BlockSpec((nk, tm), lambda s: (0, proj_tile(s))),
                      pl.BlockSpec((ML_GATE_ROWS, tm), lambda s: (0, proj_tile(s)))]
        out_shape += [jax.ShapeDtypeStruct((nk, t), jnp.bfloat16),
                      jax.ShapeDtypeStruct((ML_GATE_ROWS, t), jnp.float32)]
        scratch += [pltpu.VMEM((tm, d), jnp.bfloat16), pltpu.VMEM((tm, d), jnp.bfloat16)]
    if qk_norm is not None:
        in_specs += [pl.BlockSpec((1, LANES), const), pl.BlockSpec((1, LANES), const)]
        args += [qk_norm[2], qk_norm[3]]
    kern = functools.partial(_inproj_kernel, n_main=n_main, chunk=PROJ_COL_CHUNK, q_cols=q_cols,
                             q_scale=q_scale, gate_cols=gate_cols,
                             qk_norm=None if qk_norm is None else qk_norm[:2],
                             with_ml=ml_extra is not None, with_combine=pending is not None)
    return pl.pallas_call(
        kern,
        grid=(n_tiles + 1,),
        in_specs=in_specs,
        out_specs=out_specs,
        out_shape=out_shape,
        scratch_shapes=scratch,
        compiler_params=_params(("arbitrary",)),
        name="inproj_ml" if ml_extra is not None else "inproj_sw",
    )(*args)


def _mlstm_gate_terms(graw, bias, upper):
    H = ML_HEADS
    L = graw.shape[1]
    z = graw + bias
    gates = GATE_CAP * jnp.tanh(z / GATE_CAP)
    log_f = jnp.minimum(gates, 0.0) - jnp.log1p(jnp.exp(-jnp.abs(gates)))
    row = lax.broadcasted_iota(jnp.int32, (ML_GATE_ROWS, L), 0)
    lane = lax.broadcasted_iota(jnp.int32, (ML_GATE_ROWS, L), 1)
    is_i = row < H
    slab = jnp.where(is_i, gates, log_f)
    a1 = slab.astype(jnp.bfloat16)
    r1 = slab - a1.astype(jnp.float32)
    a2 = r1.astype(jnp.bfloat16)
    a3 = (r1 - a2.astype(jnp.float32)).astype(jnp.bfloat16)
    cum = _bdot(a1, upper) + (_bdot(a2, upper) + _bdot(a3, upper))
    ib = jnp.where(is_i, gates, cum)
    b = pltpu.roll(ib, ML_GATE_ROWS - H, 0)
    u = ib - b
    cm = u
    shift = 1
    while shift < L:
        cm = jnp.maximum(cm, jnp.where(lane >= shift, pltpu.roll(cm, shift, 1), -jnp.inf))
        shift *= 2
    return b, u, cm


def _mlstm_kernel(v_ref, o_ref, q_ref, kt_ref, gt_ref, gtn_ref, bg_ref, gout_ref, out_ref,
                  c_ref, m_ref, b_ref, u_ref, cm_ref, *, chunk):
    L = chunk
    H, dk, dv = ML_HEADS, ML_DK, ML_DV
    r_idx = lax.broadcasted_iota(jnp.int32, (L, L), 0)
    c_idx = lax.broadcasted_iota(jnp.int32, (L, L), 1)
    upper = jnp.where(r_idx <= c_idx, 1.0, 0.0).astype(jnp.bfloat16)
    causal = r_idx >= c_idx
    row = lax.broadcasted_iota(jnp.int32, (ML_GATE_ROWS, L), 0)
    ones_col = jnp.where(lax.broadcasted_iota(jnp.int32, (L, LANES), 1) == 0, 1.0, 0.0
                         ).astype(jnp.bfloat16)

    @pl.when(pl.program_id(1) == 0)
    def _():
        c_ref[...] = jnp.zeros_like(c_ref)
        m_ref[...] = jnp.zeros_like(m_ref)
        b0, u0, cm0 = _mlstm_gate_terms(gt_ref[...], bg_ref[...], upper)
        b_ref[...] = b0
        u_ref[...] = u0
        cm_ref[...] = cm0

    b16 = b_ref[...]
    u16 = u_ref[...]
    cm16 = cm_ref[...]
    b_n, u_n, cm_n = _mlstm_gate_terms(gtn_ref[...], bg_ref[...], upper)
    b_ref[...] = b_n
    u_ref[...] = u_n
    cm_ref[...] = cm_n

    m_prev = m_ref[:, 0:1]
    z16 = jnp.maximum(m_prev, cm16)
    w_inter16 = jnp.exp(m_prev - z16)
    e_negm16 = jnp.exp(-(b16 + z16))
    z_last = z16[:, L - 1:L]
    w_state16 = jnp.exp(u16 - z_last)
    decay16 = jnp.exp(m_prev - z_last)
    m_ref[...] = jnp.broadcast_to(b16[:, L - 1:L] + z_last, m_ref.shape)
    stacked = jnp.where(row < H, z16,
                        jnp.where(row < 2 * H, pltpu.roll(w_inter16, H, 0),
                                  pltpu.roll(e_negm16, 2 * H, 0)))
    cols = jnp.concatenate(
        [stacked, jnp.zeros((LANES - ML_GATE_ROWS, L), jnp.float32)], axis=0).T

    for h in range(H):
        u_r = u16[h:h + 1, :]
        z_c = cols[:, h:h + 1]
        w_inter = cols[:, H + h:H + h + 1]
        e_negm = cols[:, 2 * H + h:2 * H + h + 1]
        c_ext = c_ref[h]
        q = q_ref[:, h * dk:(h + 1) * dk]
        kt = kt_ref[h * dk:(h + 1) * dk, :]
        v_ext = jnp.concatenate([v_ref[:, h * dv:(h + 1) * dv], ones_col], axis=-1)

        w_intra = jnp.exp(jnp.where(causal, u_r - z_c, -jnp.inf))
        s = (_bdot(q, kt) * w_intra).astype(jnp.bfloat16)
        nd = w_inter * _bdot(q, c_ext.astype(jnp.bfloat16)) + _bdot(s, v_ext)
        den = nd[:, dv:dv + 1]
        hb = nd[:, :dv] * (1.0 / jnp.maximum(jnp.abs(den), e_negm))

        kw = (kt.astype(jnp.float32) * w_state16[h:h + 1, :]).astype(jnp.bfloat16)
        c_ref[h] = decay16[h:h + 1, :] * c_ext + _bdot(kw, v_ext)

        y = hb * lax.rsqrt(jnp.mean(hb * hb, axis=-1, keepdims=True) + EPS)
        y = y * gout_ref[:, h * dv:(h + 1) * dv]
        og = o_ref[:, h * dv:(h + 1) * dv].astype(jnp.float32)
        out_ref[:, h * dv:(h + 1) * dv] = (y * og).astype(out_ref.dtype)


def _mlstm_core(main, k_t, g_t, b_gate, g_out, bsz, seq):
    chunk = ML_CHUNK
    t = main.shape[0]
    nc = seq // chunk
    blk = lambda b, c: b * nc + c
    bg = jnp.zeros((ML_GATE_ROWS, 1), jnp.float32).at[:2 * ML_HEADS, 0].set(b_gate)
    return pl.pallas_call(
        functools.partial(_mlstm_kernel, chunk=chunk),
        grid=(bsz, nc),
        in_specs=[
            pl.BlockSpec((chunk, ML_V), lambda b, c: (blk(b, c), 0)),
            pl.BlockSpec((chunk, ML_V), lambda b, c: (blk(b, c), 1)),
            pl.BlockSpec((chunk, ML_QK), lambda b, c: (blk(b, c), 4)),
            pl.BlockSpec((ML_QK, chunk), lambda b, c: (0, blk(b, c))),
            pl.BlockSpec((ML_GATE_ROWS, chunk), lambda b, c: (0, blk(b, c))),
            pl.BlockSpec((ML_GATE_ROWS, chunk), lambda b, c: (0, blk(b, jnp.minimum(c + 1, nc - 1)))),
            pl.BlockSpec((ML_GATE_ROWS, 1), lambda b, c: (0, 0)),
            pl.BlockSpec((1, ML_V), lambda b, c: (0, 0)),
        ],
        out_specs=pl.BlockSpec((chunk, ML_V), lambda b, c: (blk(b, c), 0)),
        out_shape=jax.ShapeDtypeStruct((t, ML_V), jnp.bfloat16),
        scratch_shapes=[
            pltpu.VMEM((ML_HEADS, ML_DK, ML_DV + LANES), jnp.float32),
            pltpu.VMEM((ML_GATE_ROWS, LANES), jnp.float32),
            pltpu.VMEM((ML_GATE_ROWS, chunk), jnp.float32),
            pltpu.VMEM((ML_GATE_ROWS, chunk), jnp.float32),
            pltpu.VMEM((ML_GATE_ROWS, chunk), jnp.float32),
        ],
        compiler_params=_params(("arbitrary", "arbitrary")),
        name="mlstm_core",
    )(main, main, main, k_t, g_t, g_t, bg, g_out.reshape(1, ML_V))


def _swa_kernel(sinks_ref, q_ref, kc_ref, kp_ref, vc_ref, vp_ref, bias_ref, o_ref):
    W = SW_WINDOW
    lane = lax.broadcasted_iota(jnp.int32, (W, LANES), 1)
    low = lane < SW_DH
    row_col = lax.broadcasted_iota(jnp.int32, (SW_GROUP * W, 1), 0)
    bias = bias_ref[jnp.minimum(pl.program_id(1), 1)]
    for g in range(SW_KV_HEADS):
        sl = slice(g * LANES, (g + 1) * LANES)
        kn = jnp.concatenate([kp_ref[:, sl], kc_ref[:, sl]], axis=0)
        v2 = jnp.concatenate([vp_ref[:, sl], vc_ref[:, sl]], axis=0)
        sink = jnp.full((SW_GROUP * W, 1), sinks_ref[g * SW_GROUP + SW_GROUP - 1], jnp.float32)
        for j in range(SW_GROUP - 2, -1, -1):
            sink = jnp.where(row_col < (j + 1) * W, sinks_ref[g * SW_GROUP + j], sink)
        parts = []
        for p in range(2):
            c0 = g * SW_GROUP * SW_DH + p * LANES
            qp = q_ref[:, c0:c0 + LANES].astype(jnp.float32)
            parts.append(jnp.where(low, qp, 0.0).astype(jnp.bfloat16))
            parts.append(jnp.where(low, 0.0, qp).astype(jnp.bfloat16))
        q4 = jnp.concatenate(parts, axis=0)
        scores = _bdot_nt(q4, kn) + bias
        m = jnp.maximum(jnp.max(scores, axis=-1, keepdims=True), sink)
        pexp = jnp.exp2(scores - m)
        denom = jnp.sum(pexp, axis=-1, keepdims=True) + jnp.exp2(sink - m)
        o4 = _bdot(pexp.astype(jnp.bfloat16), v2) * (1.0 / denom)
        for p in range(2):
            oa = o4[(2 * p) * W:(2 * p + 1) * W]
            ob = o4[(2 * p + 1) * W:(2 * p + 2) * W]
            c0 = g * SW_GROUP * SW_DH + p * LANES
            o_ref[:, c0:c0 + LANES] = jnp.where(low, oa, ob).astype(o_ref.dtype)


def _swa_core(proj, sinks, bsz, seq):
    t = proj.shape[0]
    W = SW_WINDOW
    nb = seq // W
    dq = SW_Q_HEADS * SW_DH
    kv_w = SW_KV_HEADS * LANES
    k_blk = dq // kv_w
    v_blk = k_blk + 1
    cur = lambda b, n, s: b * nb + n
    prev = lambda b, n, s: b * nb + jnp.maximum(n - 1, 0)
    sinks = sinks.astype(jnp.float32) * LOG2E
    qi = (jnp.arange(SW_GROUP * W) % W)[:, None]
    ki = jnp.arange(2 * W)[None, :]
    rel = qi + W - ki
    in_win = (rel >= 0) & (rel < W)
    bias = jnp.stack([jnp.where(in_win & (ki >= W), 0.0, -jnp.inf),
                      jnp.where(in_win, 0.0, -jnp.inf)]).astype(jnp.float32)
    grid_spec = pltpu.PrefetchScalarGridSpec(
        num_scalar_prefetch=1,
        grid=(bsz, nb),
        in_specs=[
            pl.BlockSpec((W, dq), lambda b, n, s: (cur(b, n, s), 0)),
            pl.BlockSpec((W, kv_w), lambda b, n, s: (cur(b, n, s), k_blk)),
            pl.BlockSpec((W, kv_w), lambda b, n, s: (prev(b, n, s), k_blk)),
            pl.BlockSpec((W, kv_w), lambda b, n, s: (cur(b, n, s), v_blk)),
            pl.BlockSpec((W, kv_w), lambda b, n, s: (prev(b, n, s), v_blk)),
            pl.BlockSpec((2, SW_GROUP * W, 2 * W), lambda b, n, s: (0, 0, 0)),
        ],
        out_specs=pl.BlockSpec((W, dq), lambda b, n, s: (cur(b, n, s), 0)),
    )
    return pl.pallas_call(
        _swa_kernel,
        grid_spec=grid_spec,
        out_shape=jax.ShapeDtypeStruct((t, dq), jnp.bfloat16),
        compiler_params=_params(("arbitrary", "arbitrary")),
        name="swa_core",
    )(sinks, proj, proj, proj, proj, proj, bias)


def _pack_rounded_pairs(r):
    k = r.shape[1] // 2
    hi = lax.bitcast_convert_type(r[:, :k], jnp.uint32)
    lo = lax.bitcast_convert_type(r[:, k:], jnp.uint32)
    return hi | (lo >> 16)


def _pack_bf16_pairs(a):
    return _pack_rounded_pairs(a.astype(jnp.bfloat16).astype(jnp.float32))


def _unpack_bf16_pairs(u):
    hi = lax.bitcast_convert_type(u & jnp.uint32(0xFFFF0000), jnp.float32)
    lo = lax.bitcast_convert_type(u << 16, jnp.float32)
    return hi, lo


def _route_tile(x_new, g_ref, sc_ref, sh_ref, wh_ref, wl_ref, b_ref,
                hn_ref, info_ref, infot_ref, cnt_ref, carry_ref, earlier_ref, tm):
    hn = _modulated_norm(x_new, g_ref[...], sc_ref[...], sh_ref[...])
    h_hi = hn.astype(jnp.bfloat16)
    hi_f32 = h_hi.astype(jnp.float32)
    h_lo = (hn - hi_f32).astype(jnp.bfloat16)
    hn_ref[...] = _pack_rounded_pairs(hi_f32)
    wide = _bdot_nt(wl_ref[...], h_hi)
    logits = (wide[:ROUTER_LANES] + (_bdot_nt(wh_ref[...], h_lo) + wide[ROUTER_LANES:])
              + b_ref[...])
    E8 = EXPERTS_PER_GROUP
    sub = lax.broadcasted_iota(jnp.int32, (E8, tm), 0).astype(jnp.float32)
    big = float(ROUTER_LANES)
    neg = -jnp.inf

    gl = logits[N_EXPERTS:N_EXPERTS + N_GROUPS]
    gmax = jnp.max(gl, axis=0, keepdims=True)
    gsel = jnp.min(jnp.where(gl == gmax, sub, big), axis=0, keepdims=True)
    p_grp = 1.0 / jnp.sum(jnp.exp(gl - gmax), axis=0, keepdims=True)

    el = logits[0:E8]
    for grp in range(1, N_GROUPS):
        el = jnp.where(gsel == grp, logits[grp * E8:(grp + 1) * E8], el)
    v1 = jnp.max(el, axis=0, keepdims=True)
    j1 = jnp.min(jnp.where(el == v1, sub, big), axis=0, keepdims=True)
    el2 = jnp.where(sub == j1, neg, el)
    v2 = jnp.max(el2, axis=0, keepdims=True)
    j2 = jnp.min(jnp.where(el2 == v2, sub, big), axis=0, keepdims=True)
    i1 = gsel * E8 + j1
    i2 = gsel * E8 + j2
    e21 = jnp.exp(v2 - v1)
    gate1 = p_grp / (1.0 + e21)
    gate2 = p_grp * e21 / (1.0 + e21)

    erow = lax.broadcasted_iota(jnp.int32, (N_EXPERTS, tm), 0).astype(jnp.float32)
    hit1 = erow == i1
    hit2 = erow == i2
    onehot = jnp.where(hit1 | hit2, 1.0, 0.0)
    carry = carry_ref[:, 0:1]
    before = _bdot(onehot.astype(jnp.bfloat16), earlier_ref[...]) + carry
    rank1 = jnp.sum(jnp.where(hit1, before, 0.0), axis=0, keepdims=True)
    rank2 = jnp.sum(jnp.where(hit2, before, 0.0), axis=0, keepdims=True)
    total = carry + jnp.sum(onehot, axis=1, keepdims=True)
    carry_ref[...] = jnp.broadcast_to(total, carry_ref.shape)
    cnt_ref[...] = jnp.broadcast_to(total, cnt_ref.shape)

    info_t = jnp.where(sub == 0, i1, 0.0)
    info_t = jnp.where(sub == 1, i2, info_t)
    info_t = jnp.where(sub == 2, rank1, info_t)
    info_t = jnp.where(sub == 3, rank2, info_t)
    info_t = jnp.where(sub == 4, gate1, info_t)
    info_t = jnp.where(sub == 5, gate2, info_t)
    infot_ref[...] = info_t
    info_ref[...] = jnp.concatenate(
        [info_t, jnp.zeros((ROUTER_LANES - E8, tm), jnp.float32)], axis=0).T


def _router_kernel(a_ref, wo_ref, x_ref, gt_ref, g_ref, sc_ref, sh_ref, wh_ref, wl_ref, b_ref,
                   xo_ref, hn_ref, info_ref, infot_ref, cnt_ref, carry_ref, earlier_ref, *, tm):
    @pl.when(pl.program_id(0) == 0)
    def _():
        carry_ref[...] = jnp.zeros_like(carry_ref)
        r_idx = lax.broadcasted_iota(jnp.int32, (tm, tm), 0)
        c_idx = lax.broadcasted_iota(jnp.int32, (tm, tm), 1)
        earlier_ref[...] = jnp.where(r_idx < c_idx, 1.0, 0.0).astype(jnp.bfloat16)

    x_new = x_ref[...] + gt_ref[...] * _bdot(a_ref[...], wo_ref[...])
    xo_ref[...] = x_new
    _route_tile(x_new, g_ref, sc_ref, sh_ref, wh_ref, wl_ref, b_ref,
                hn_ref, info_ref, infot_ref, cnt_ref, carry_ref, earlier_ref, tm)


def _outproj_router(a, w_out, x2d, gt, g, sc, sh, w_hi, w_lo, bias, seq):
    t, d = x2d.shape
    tm = min(ROUTER_TILE, seq)
    tiles_per_batch = seq // tm
    per_batch = lambda i: (i // tiles_per_batch, 0, 0)
    const = lambda i: (0, 0)
    return pl.pallas_call(
        functools.partial(_router_kernel, tm=tm),
        grid=(t // tm,),
        in_specs=[
            pl.BlockSpec((tm, a.shape[1]), lambda i: (i, 0)),
            pl.BlockSpec(w_out.shape, const),
            pl.BlockSpec((tm, d), lambda i: (i, 0)),
            pl.BlockSpec((None, 1, d), per_batch),
            pl.BlockSpec((1, d), const),
            pl.BlockSpec((None, 1, d), per_batch),
            pl.BlockSpec((None, 1, d), per_batch),
            pl.BlockSpec((ROUTER_LANES, d), const),
            pl.BlockSpec((2 * ROUTER_LANES, d), const),
            pl.BlockSpec((ROUTER_LANES, 1), const),
        ],
        out_specs=[
            pl.BlockSpec((tm, d), lambda i: (i, 0)),
            pl.BlockSpec((tm, d // 2), lambda i: (i, 0)),
            pl.BlockSpec((tm, ROUTER_LANES), lambda i: (i, 0)),
            pl.BlockSpec((EXPERTS_PER_GROUP, tm), lambda i: (0, i)),
            pl.BlockSpec((N_EXPERTS, LANES), const),
        ],
        out_shape=[
            jax.ShapeDtypeStruct((t, d), jnp.float32),
            jax.ShapeDtypeStruct((t, d // 2), jnp.uint32),
            jax.ShapeDtypeStruct((t, ROUTER_LANES), jnp.float32),
            jax.ShapeDtypeStruct((EXPERTS_PER_GROUP, t), jnp.float32),
            jax.ShapeDtypeStruct((N_EXPERTS, LANES), jnp.float32),
        ],
        scratch_shapes=[pltpu.VMEM((N_EXPERTS, LANES), jnp.float32),
                        pltpu.VMEM((tm, tm), jnp.bfloat16)],
        compiler_params=_params(("arbitrary",)),
        name="outproj_router",
    )(a, w_out, x2d, gt, g.reshape(1, d), sc, sh, w_hi, jnp.concatenate([w_hi, w_lo], axis=0), bias)


def _sc_mesh():
    return plsc.VectorSubcoreMesh(core_axis_name="c", subcore_axis_name="s")


def _sc_dispatch(rows, d0, d1, pad_idx, n_slots):
    t, w = rows.shape
    n_pad = pad_idx.shape[1]
    zeros = jnp.zeros((SC_ROWS, w), rows.dtype)
    sem = (pltpu.PARALLEL, pltpu.ARBITRARY)
    parts = SC_WINDOW // SC_ROWS

    @pl.kernel(out_type=jax.ShapeDtypeStruct((n_slots + SC_WINDOW, w), rows.dtype), mesh=_sc_mesh(),
               scratch_types=[pltpu.SemaphoreType.DMA, pltpu.SemaphoreType.DMA])
    def dispatch(x_hbm, d0_hbm, d1_hbm, z_hbm, p_hbm, o_hbm, sem0, sem1):
        def scatter_rows(x_vmem, i0_vmem, i1_vmem):
            part = pl.ds(pl.program_id(1) * SC_ROWS, SC_ROWS)
            first = pltpu.async_copy(x_vmem, o_hbm.at[i0_vmem.at[0, part]], sem0)
            second = pltpu.async_copy(x_vmem, o_hbm.at[i1_vmem.at[0, part]], sem1)
            first.wait()
            second.wait()

        pltpu.emit_pipeline(
            scatter_rows,
            grid=(t // SC_WINDOW, parts),
            in_specs=[pl.BlockSpec((SC_ROWS, w), lambda i, j: (parts * i + j, 0)),
                      pl.BlockSpec((1, SC_WINDOW), lambda i, j: (0, i)),
                      pl.BlockSpec((1, SC_WINDOW), lambda i, j: (0, i))],
            out_specs=[],
            core_axis_name=("c", "s"),
            dimension_semantics=sem,
        )(x_hbm, d0_hbm, d1_hbm)

        def scatter_zeros(z_vmem, p_vmem):
            part = pl.ds(pl.program_id(1) * SC_ROWS, SC_ROWS)
            pltpu.sync_copy(z_vmem, o_hbm.at[p_vmem.at[0, part]])

        pltpu.emit_pipeline(
            scatter_zeros,
            grid=(n_pad // SC_WINDOW, parts),
            in_specs=[pl.BlockSpec((SC_ROWS, w), lambda i, j: (0, 0)),
                      pl.BlockSpec((1, SC_WINDOW), lambda i, j: (0, i))],
            out_specs=[],
            core_axis_name=("c", "s"),
            dimension_semantics=sem,
        )(z_hbm, p_hbm)

    return dispatch(rows, d0, d1, zeros, pad_idx)


def _sc_gather(src, idx):
    n_out = idx.shape[1]
    w = src.shape[1]
    parts = SC_WINDOW // SC_ROWS

    @pl.kernel(out_type=jax.ShapeDtypeStruct((n_out, w), src.dtype), mesh=_sc_mesh())
    def gather(x_hbm, i_hbm, o_hbm):
        def gather_rows(i_vmem, o_vmem):
            part = pl.ds(pl.program_id(1) * SC_ROWS, SC_ROWS)
            pltpu.sync_copy(x_hbm.at[i_vmem.at[0, part]], o_vmem)

        pltpu.emit_pipeline(
            gather_rows,
            grid=(n_out // SC_WINDOW, parts),
            in_specs=[pl.BlockSpec((1, SC_WINDOW), lambda i, j: (0, i))],
            out_specs=[pl.BlockSpec((SC_ROWS, w), lambda i, j: (parts * i + j, 0))],
            core_axis_name=("c", "s"),
            dimension_semantics=(pltpu.PARALLEL, pltpu.ARBITRARY),
        )(i_hbm, o_hbm)

    return gather(src, idx)


def _expert_kernel(meta_ref, x_ref, w1_hbm, w3_hbm, w2_hbm, y_ref,
                   w1_buf, w3_buf, w2_buf, w1_c, w3_c, w2_c, sems, *, layer, nb):
    def weight_copies(expert, s):
        return (pltpu.make_async_copy(w1_hbm.at[layer, expert], w1_buf.at[s], sems.at[s, 0]),
                pltpu.make_async_copy(w3_hbm.at[layer, expert], w3_buf.at[s], sems.at[s, 1]),
                pltpu.make_async_copy(w2_hbm.at[layer, expert], w2_buf.at[s], sems.at[s, 2]))

    @pl.when(pl.program_id(0) == 0)
    def _():
        for cp in weight_copies(meta_ref[0], meta_ref[nb]):
            cp.start()

    for j in range(EXPERT_BLOCKS_PER_STEP):
        i = pl.program_id(0) * EXPERT_BLOCKS_PER_STEP + j
        rows = slice(j * MOE_BLOCK, (j + 1) * MOE_BLOCK)
        e = meta_ref[i]
        slot = meta_ref[nb + i]
        nxt = meta_ref[2 * nb + i]
        used = i < meta_ref[3 * nb]
        first = used & ((i == 0) | (e != meta_ref[jnp.maximum(i - 1, 0)]))

        @pl.when(first)
        def _():
            for cp in weight_copies(e, slot):
                cp.wait()

            @pl.when(nxt >= 0)
            def _():
                for cp in weight_copies(nxt, 1 - slot):
                    cp.start()

            w1_c[...] = w1_buf[slot].astype(jnp.bfloat16)
            w3_c[...] = w3_buf[slot].astype(jnp.bfloat16)
            w2_c[...] = w2_buf[slot].astype(jnp.bfloat16)

        @pl.when(used)
        def _():
            x_hi, x_lo = _unpack_bf16_pairs(x_ref[rows, :])
            xb = jnp.concatenate([x_hi, x_lo], axis=-1).astype(jnp.bfloat16)
            full = 256
            tail = jnp.concatenate([w1_c[:, full:], w3_c[:, full:]], axis=-1)
            half_rows = MOE_BLOCK // 2
            h_tail = jnp.concatenate([_bdot(xb[:half_rows], tail), _bdot(xb[half_rows:], tail)], axis=0)
            h1 = jnp.concatenate([_bdot(xb, w1_c[:, :full]), h_tail[:, :LANES]], axis=-1)
            h3 = jnp.concatenate([_bdot(xb, w3_c[:, :full]), h_tail[:, LANES:]], axis=-1)
            act = (h1 * jax.nn.sigmoid(h1) * h3).astype(jnp.bfloat16)
            y_ref[rows, :] = _pack_bf16_pairs(_bdot(act, w2_c[...]))

        @pl.when(jnp.logical_not(used))
        def _():
            y_ref[rows, :] = jnp.zeros((MOE_BLOCK, y_ref.shape[1]), y_ref.dtype)


def _experts(x_slots, blk_meta, w1, w3, w2, layer, nb):
    dp = x_slots.shape[1]
    d, de = w1.shape[-2:]
    step_rows = EXPERT_BLOCKS_PER_STEP * MOE_BLOCK
    assert nb % EXPERT_BLOCKS_PER_STEP == 0
    last_used_step = lambda s: (s[3 * nb] - 1) // EXPERT_BLOCKS_PER_STEP
    grid_spec = pltpu.PrefetchScalarGridSpec(
        num_scalar_prefetch=1,
        grid=(nb // EXPERT_BLOCKS_PER_STEP,),
        in_specs=[
            pl.BlockSpec((step_rows, dp), lambda i, s: (jnp.minimum(i, last_used_step(s)), 0)),
            pl.BlockSpec(memory_space=pl.ANY),
            pl.BlockSpec(memory_space=pl.ANY),
            pl.BlockSpec(memory_space=pl.ANY),
        ],
        out_specs=pl.BlockSpec((step_rows, dp), lambda i, s: (i, 0)),
        scratch_shapes=[
            pltpu.VMEM((2, d, de), jnp.float32),
            pltpu.VMEM((2, d, de), jnp.float32),
            pltpu.VMEM((2, de, d), jnp.float32),
            pltpu.VMEM((d, de), jnp.bfloat16),
            pltpu.VMEM((d, de), jnp.bfloat16),
            pltpu.VMEM((de, d), jnp.bfloat16),
            pltpu.SemaphoreType.DMA((2, 3)),
        ],
    )
    return pl.pallas_call(
        functools.partial(_expert_kernel, layer=layer, nb=nb),
        grid_spec=grid_spec,
        out_shape=jax.ShapeDtypeStruct((nb * MOE_BLOCK, dp), jnp.uint32),
        compiler_params=_params(("arbitrary",)),
        name="moe_experts",
    )(blk_meta, x_slots, w1, w3, w2)


def _combine_kernel(x_ref, y1_ref, y2_ref, info_ref, gt_ref, o_ref):
    o_ref[...] = _moe_combined(x_ref[...], y1_ref, y2_ref, info_ref, gt_ref)


def _combine(x2d, y_pairs, info, gt, seq):
    t, d = x2d.shape
    tm = min(COMBINE_TILE, seq)
    tiles_per_batch = seq // tm
    second = t // tm
    return pl.pallas_call(
        _combine_kernel,
        grid=(t // tm,),
        in_specs=[
            pl.BlockSpec((tm, d), lambda i: (i, 0)),
            pl.BlockSpec((tm, d // 2), lambda i: (i, 0)),
            pl.BlockSpec((tm, d // 2), lambda i: (i + second, 0)),
            pl.BlockSpec((tm, ROUTER_LANES), lambda i: (i, 0)),
            pl.BlockSpec((None, 1, d), lambda i: (i // tiles_per_batch, 0, 0)),
        ],
        out_specs=pl.BlockSpec((tm, d), lambda i: (i, 0)),
        out_shape=jax.ShapeDtypeStruct((t, d), jnp.float32),
        compiler_params=_params(("arbitrary",)),
        name="moe_combine",
    )(x2d, y_pairs, y_pairs, info, gt)


def _slot_plan(info_t, cnt, t):
    counts = cnt[:, 0].astype(jnp.int32)
    padded = (counts + MOE_BLOCK - 1) // MOE_BLOCK * MOE_BLOCK
    pad_ends = jnp.cumsum(padded)
    pad_starts = pad_ends - padded
    nb = -(-(2 * t) // MOE_BLOCK) + N_EXPERTS
    n_slots = nb * MOE_BLOCK
    it = info_t.astype(jnp.int32)
    onehot_start = lambda e: jnp.sum(
        jnp.where(e[None, :] == jnp.arange(N_EXPERTS, dtype=jnp.int32)[:, None],
                  pad_starts[:, None], 0), axis=0)
    dest1 = (onehot_start(it[0]) + it[2]).reshape(1, t)
    dest2 = (onehot_start(it[1]) + it[3]).reshape(1, t)
    lane = jnp.arange(MOE_BLOCK, dtype=jnp.int32)[None, :]
    n_padding = (padded - counts)[:, None]
    wrapped = (pad_starts + counts)[:, None] + lane % jnp.maximum(n_padding, 1)
    pad_idx = jnp.where(n_padding > 0, wrapped, n_slots + lane % SC_WINDOW).reshape(-1)
    n_real = pad_ends[-1] // MOE_BLOCK
    n_used = -(-n_real // EXPERT_BLOCKS_PER_STEP) * EXPERT_BLOCKS_PER_STEP
    tail = jnp.arange((EXPERT_BLOCKS_PER_STEP - 1) * MOE_BLOCK, dtype=jnp.int32)
    tail_idx = jnp.where(tail < (n_used - n_real) * MOE_BLOCK, pad_ends[-1] + tail,
                         n_slots + tail % SC_WINDOW)
    pad_idx = jnp.concatenate([pad_idx, tail_idx]).reshape(1, -1)
    experts = jnp.arange(N_EXPERTS, dtype=jnp.int32)
    blk = jnp.arange(nb, dtype=jnp.int32)
    blk_exp = jnp.minimum(
        jnp.sum((pad_ends[None, :] <= (blk * MOE_BLOCK)[:, None]).astype(jnp.int32), axis=1),
        N_EXPERTS - 1)
    blk_exp = jnp.where(blk >= n_real, jnp.max(jnp.where(padded > 0, experts, 0)), blk_exp)
    prev_exp = jnp.concatenate([jnp.full((1,), -1, jnp.int32), blk_exp[:-1]])
    is_first = (blk < n_used) & (blk_exp != prev_exp)
    blk_slot = (jnp.cumsum(is_first.astype(jnp.int32)) + 1) % 2
    later = (experts[None, :] > experts[:, None]) & (padded[None, :] > 0)
    nxt_of = jnp.min(jnp.where(later, experts[None, :], N_EXPERTS), axis=1)
    nxt_of = jnp.where(nxt_of == N_EXPERTS, -1, nxt_of)
    blk_nxt = jnp.sum(jnp.where(blk_exp[:, None] == experts[None, :], nxt_of[None, :], 0), axis=1)
    blk_meta = jnp.concatenate([blk_exp, blk_slot, blk_nxt, n_used[None]]).astype(jnp.int32)
    return dest1, dest2, pad_idx, blk_meta, nb, n_slots


def _mixer_out_and_moe(a, w_out, x2d, gt1, g, sc, sh, w_group, b_group, w_router, b_router,
                       w1, w3, w2, layer, seq):
    t, d = x2d.shape
    w_cat = jnp.zeros((ROUTER_LANES, d), jnp.float32)
    w_cat = w_cat.at[:N_EXPERTS].set(w_router.T).at[N_EXPERTS:N_EXPERTS + N_GROUPS].set(w_group.T)
    b_cat = jnp.zeros((ROUTER_LANES, 1), jnp.float32)
    b_cat = b_cat.at[:N_EXPERTS, 0].set(b_router).at[N_EXPERTS:N_EXPERTS + N_GROUPS, 0].set(b_group)
    w_hi, w_lo = _split_hi_lo(w_cat)
    x_new, hn, info, info_t, cnt = _outproj_router(a, w_out, x2d, gt1, g, sc, sh, w_hi, w_lo,
                                                   b_cat, seq)
    dest1, dest2, pad_idx, blk_meta, nb, n_slots = _slot_plan(info_t, cnt, t)
    x_slots = _sc_dispatch(hn, dest1, dest2, pad_idx, n_slots)
    y_slots = _experts(x_slots, blk_meta, w1, w3, w2, layer, nb)
    y_pairs = _sc_gather(y_slots, jnp.concatenate([dest1, dest2], axis=1))
    return x_new, y_pairs, info


def kernel(x, c, w_ada, b_ada, norm1_g, norm2_g, ml_w_in, ml_b_gate, ml_g_out, ml_w_out,
           sw_w_in, sw_g_q, sw_g_k, sw_sinks, sw_w_out, moe_w_group, moe_b_group,
           moe_w_router, moe_b_router, moe_w1, moe_w3, moe_w2):
    bsz, seq, d = x.shape
    depth = w_ada.shape[0]
    bf = jnp.bfloat16
    mod = _ada_mod(c, w_ada, b_ada)
    x2d = x.reshape(bsz * seq, d)
    pending = None
    for layer in range(depth):
        sh1, sc1, gt1, sh2, sc2, gt2 = [
            mod[layer, :, i * d:(i + 1) * d].reshape(bsz, 1, d) for i in range(6)]
        j = layer // 2
        if layer % 2 == 0:
            w = ml_w_in[j]
            q_w, k_w = w[:, :ML_QK], w[:, ML_QK:2 * ML_QK]
            v_w = w[:, 2 * ML_QK:2 * ML_QK + ML_V]
            o_w = w[:, 2 * ML_QK + ML_V:2 * ML_QK + 2 * ML_V]
            g_w = w[:, 2 * ML_QK + 2 * ML_V:]
            w_main = jnp.concatenate([v_w, o_w, q_w], axis=1).astype(bf)
            wg_t = jnp.zeros((ML_GATE_ROWS, d), jnp.float32).at[:2 * ML_HEADS].set(g_w.T)
            wg_hi, wg_lo = _split_hi_lo(wg_t)
            wk_t = k_w.T.astype(bf)
            w_out = ml_w_out[j].astype(bf)
            outs = _inproj(x2d, norm1_g[layer], sc1, sh1, w_main, seq,
                           ml_extra=(wk_t, wg_hi, wg_lo),
                           q_cols=(2 * ML_V, 2 * ML_V + ML_QK), q_scale=ML_DK ** -0.5,
                           gate_cols=(ML_V, 2 * ML_V), pending=pending)
            if pending is not None:
                x2d, outs = outs[0], outs[1:]
            main, k_t, g_t = outs
            a = _mlstm_core(main, k_t, g_t, ml_b_gate[j], ml_g_out[j], bsz, seq)
        else:
            w = sw_w_in[j]
            dq = SW_Q_HEADS * SW_DH
            dkv = SW_KV_HEADS * SW_DH
            dup = lambda m: jnp.concatenate(
                [m.reshape(d, SW_KV_HEADS, 1, SW_DH)] * 2, axis=2).reshape(d, 2 * dkv)
            w_main = jnp.concatenate(
                [w[:, :dq], dup(w[:, dq:dq + dkv]), dup(w[:, dq + dkv:])], axis=1).astype(bf)
            w_out = sw_w_out[j].astype(bf)
            gq = jnp.concatenate([sw_g_q[j], sw_g_q[j]]).reshape(1, LANES) * (SW_DH ** -0.5 * LOG2E)
            gk = jnp.concatenate([sw_g_k[j], sw_g_k[j]]).reshape(1, LANES)
            outs = _inproj(x2d, norm1_g[layer], sc1, sh1, w_main, seq, pending=pending,
                           qk_norm=(dq, dq + 2 * dkv, gq, gk))
            if pending is not None:
                x2d, outs = outs[0], outs[1:]
            a = _swa_core(outs[0], sw_sinks[j], bsz, seq)
        x2d, y_pairs, info = _mixer_out_and_moe(
            a, w_out, x2d, gt1, norm2_g[layer], sc2, sh2, moe_w_group[layer], moe_b_group[layer],
            moe_w_router[layer], moe_b_router[layer], moe_w1, moe_w3, moe_w2, layer, seq)
        pending = (y_pairs, info, gt2)
    y_pairs, info, gt2 = pending
    return _combine(x2d, y_pairs, info, gt2, seq).reshape(bsz, seq, d)
```

```python
import functools

import jax
import jax.numpy as jnp
from jax import lax
from jax.experimental import pallas as pl
from jax.experimental.pallas import tpu as pltpu
from jax.experimental.pallas import tpu_sc as plsc

EPS = 1e-6
GATE_CAP = 15.0
LOG2E = 1.4426950408889634

ML_HEADS = 4
ML_DK = 128
ML_DV = 256
ML_QK = ML_HEADS * ML_DK
ML_V = ML_HEADS * ML_DV
ML_GATE_ROWS = 16

SW_Q_HEADS = 16
SW_KV_HEADS = 4
SW_GROUP = SW_Q_HEADS // SW_KV_HEADS
SW_DH = 64
SW_WINDOW = 128
LANES = 128

N_GROUPS = 8
EXPERTS_PER_GROUP = 8
N_EXPERTS = N_GROUPS * EXPERTS_PER_GROUP
MOE_BLOCK = 256
ROUTER_LANES = 128
SC_WINDOW = 128
SC_ROWS = 64
EXPERT_BLOCKS_PER_STEP = 8

SUBLANES = 8
TOKEN_TILE = 512
ROUTER_TILE = 1024
COMBINE_TILE = 1024
PROJ_COL_CHUNK = 1024
ML_CHUNK = 256
ADA_COL_TILE = 768
V7X_VMEM_BYTES = 64 * 1024 * 1024
VMEM_LIMIT = V7X_VMEM_BYTES - 8 * 1024 * 1024

_NT = (((1,), (1,)), ((), ()))


def _bdot(a, b):
    return jnp.dot(a, b, preferred_element_type=jnp.float32)


def _bdot_nt(a, b):
    return lax.dot_general(a, b, _NT, preferred_element_type=jnp.float32)


def _split_hi_lo(a):
    hi = a.astype(jnp.bfloat16)
    lo = (a - hi.astype(jnp.float32)).astype(jnp.bfloat16)
    return hi, lo


def _params(sem):
    return pltpu.CompilerParams(dimension_semantics=sem, vmem_limit_bytes=VMEM_LIMIT)


def _ada_kernel(c_ref, w_ref, b_ref, o_ref):
    c = c_ref[...]
    cond = c * jax.nn.sigmoid(c)
    c_hi, c_lo = _split_hi_lo(cond)
    w_hi, w_lo = _split_hi_lo(w_ref[...])
    acc = _bdot(c_hi, w_hi) + (_bdot(c_lo, w_hi) + _bdot(c_hi, w_lo))
    o_ref[...] = acc + b_ref[...]


def _ada_mod(c, w_ada, b_ada):
    depth, d, n = w_ada.shape
    bsz = c.shape[0]
    rows = SUBLANES
    tn = ADA_COL_TILE
    c_pad = jnp.zeros((rows, d), jnp.float32).at[:bsz].set(c)
    out = pl.pallas_call(
        _ada_kernel,
        grid=(depth, n // tn),
        in_specs=[
            pl.BlockSpec((rows, d), lambda l, j: (0, 0)),
            pl.BlockSpec((None, d, tn), lambda l, j: (l, 0, j)),
            pl.BlockSpec((None, 1, tn), lambda l, j: (l, 0, j)),
        ],
        out_specs=pl.BlockSpec((None, rows, tn), lambda l, j: (l, 0, j)),
        out_shape=jax.ShapeDtypeStruct((depth, rows, n), jnp.float32),
        compiler_params=_params(("arbitrary", "arbitrary")),
        name="ada_mod",
    )(c_pad, w_ada, b_ada.reshape(depth, 1, n))
    return out[:, :bsz]


def _modulated_norm(x, g, sc, sh):
    y = x * lax.rsqrt(jnp.mean(x * x, axis=-1, keepdims=True) + EPS)
    return y * (g * (1.0 + sc)) + sh


def _moe_combined(x, y1_ref, y2_ref, info_ref, gt_ref):
    info = info_ref[...]
    g1 = info[:, 4:5]
    g2 = info[:, 5:6]
    y1_hi, y1_lo = _unpack_bf16_pairs(y1_ref[...])
    y2_hi, y2_lo = _unpack_bf16_pairs(y2_ref[...])
    y = jnp.concatenate([g1 * y1_hi + g2 * y2_hi, g1 * y1_lo + g2 * y2_lo], axis=-1)
    return x + gt_ref[...] * y


def _qk_head_norm(acc, c0, qk_norm, gq, gk):
    q_hi, k_hi = qk_norm
    low = lax.broadcasted_iota(jnp.int32, (acc.shape[0], LANES), 1) < SW_DH
    slabs = []
    for j in range(acc.shape[1] // LANES):
        slab = acc[:, j * LANES:(j + 1) * LANES]
        if c0 + j * LANES < q_hi:
            sq = slab * slab
            ss_lo = jnp.sum(jnp.where(low, sq, 0.0), axis=-1, keepdims=True)
            ss_hi = jnp.sum(jnp.where(low, 0.0, sq), axis=-1, keepdims=True)
            rs = jnp.where(low, lax.rsqrt(ss_lo / SW_DH + EPS), lax.rsqrt(ss_hi / SW_DH + EPS))
            slab = slab * rs * gq
        elif c0 + j * LANES < k_hi:
            slab = slab * lax.rsqrt(jnp.mean(slab * slab, axis=-1, keepdims=True) + EPS) * gk
        slabs.append(slab)
    return jnp.concatenate(slabs, axis=-1)


def _inproj_kernel(*refs, n_main, chunk, q_cols, q_scale, gate_cols, qk_norm, with_ml,
                   with_combine):
    refs = list(refs)
    n_in = (5 + (4 if with_combine else 0) + (2 if with_ml else 0)
            + (2 if qk_norm is not None else 0))
    n_scratch = 4 if with_ml else 2
    ins, outs, scratch = refs[:n_in], refs[n_in:-n_scratch], refs[-n_scratch:]
    x_ref = ins.pop(0)
    if with_combine:
        y1_ref, y2_ref, info_ref, gtp_ref = ins[:4]
        ins = ins[4:]
        xo_ref = outs.pop(0)
    g_ref, sc_ref, sh_ref, w_ref = ins[:4]
    o_ref = outs[0]
    if with_ml:
        wk_ref, wgh_ref = ins[4:]
        kt_ref, gt_ref = outs[1:]
    if qk_norm is not None:
        gq_ref, gk_ref = ins[4:]

    def normalise(hb_dst, lo_dst):
        x = x_ref[...]
        if with_combine:
            x = _moe_combined(x, y1_ref, y2_ref, info_ref, gtp_ref)
            xo_ref[...] = x
        hn = _modulated_norm(x, g_ref[...], sc_ref[...], sh_ref[...])
        hb = hn.astype(jnp.bfloat16)
        hb_dst[...] = hb
        if with_ml:
            lo_dst[...] = (hn - hb.astype(jnp.float32)).astype(jnp.bfloat16)

    def project(hb_src, lo_src):
        hb = hb_src[...]
        for c0 in range(0, n_main, chunk):
            c1 = min(c0 + chunk, n_main)
            acc = _bdot(hb, w_ref[:, c0:c1])
            if q_cols is not None and q_cols[0] <= c0 < q_cols[1]:
                acc = acc * q_scale
            if gate_cols is not None and gate_cols[0] <= c0 < gate_cols[1]:
                acc = jax.nn.sigmoid(acc)
            if qk_norm is not None and c0 < qk_norm[1]:
                acc = _qk_head_norm(acc, c0, qk_norm, gq_ref[...], gk_ref[...])
            o_ref[:, c0:c1] = acc.astype(o_ref.dtype)
        if with_ml:
            nk = kt_ref.shape[0]
            stacked = _bdot_nt(wk_ref[...], hb)
            kt_ref[...] = stacked[:nk].astype(kt_ref.dtype)
            gt_ref[...] = (stacked[nk:nk + ML_GATE_ROWS]
                           + (_bdot_nt(wgh_ref[...], lo_src[...]) + stacked[nk + ML_GATE_ROWS:]))

    hb_a, hb_b = scratch[:2]
    lo_a, lo_b = scratch[2:] if with_ml else (None, None)
    s = pl.program_id(0)

    @pl.when(s == 0)
    def _():
        hb_b[...] = jnp.zeros_like(hb_b)
        if with_ml:
            lo_b[...] = jnp.zeros_like(lo_b)

    @pl.when(s % 2 == 0)
    def _():
        normalise(hb_a, lo_a)
        project(hb_b, lo_b)

    @pl.when(s % 2 == 1)
    def _():
        normalise(hb_b, lo_b)
        project(hb_a, lo_a)


def _inproj(x2d, g, sc, sh, w_main, seq, *, ml_extra=None, q_cols=None, q_scale=1.0,
            gate_cols=None, qk_norm=None, pending=None):
    t, d = x2d.shape
    tm = TOKEN_TILE if ml_extra is not None else min(2 * TOKEN_TILE, seq)
    n_main = w_main.shape[1]
    tiles_per_batch = seq // tm
    n_tiles = t // tm
    resident = pl.Buffered(1)
    norm_tile = lambda s: jnp.minimum(s, n_tiles - 1)
    proj_tile = lambda s: jnp.maximum(s - 1, 0)
    row = lambda s: (norm_tile(s), 0)
    per_batch = lambda s: (norm_tile(s) // tiles_per_batch, 0, 0)
    const = lambda s: (0, 0)
    in_specs = [pl.BlockSpec((tm, d), row)]
    args = [x2d]
    out_specs, out_shape = [], []
    if pending is not None:
        y_pairs, info, gt_prev = pending
        in_specs += [pl.BlockSpec((tm, d // 2), row),
                     pl.BlockSpec((tm, d // 2), lambda s: (norm_tile(s) + n_tiles, 0)),
                     pl.BlockSpec((tm, ROUTER_LANES), row),
                     pl.BlockSpec((None, 1, d), per_batch)]
        args += [y_pairs, y_pairs, info, gt_prev]
        out_specs += [pl.BlockSpec((tm, d), row)]
        out_shape += [jax.ShapeDtypeStruct((t, d), jnp.float32)]
    in_specs += [
        pl.BlockSpec((1, d), const),
        pl.BlockSpec((None, 1, d), per_batch),
        pl.BlockSpec((None, 1, d), per_batch),
        pl.BlockSpec((d, n_main), const, pipeline_mode=resident),
    ]
    args += [g.reshape(1, d), sc, sh, w_main]
    out_specs += [pl.BlockSpec((tm, n_main), lambda s: (proj_tile(s), 0))]
    out_shape += [jax.ShapeDtypeStruct((t, n_main), jnp.bfloat16)]
    scratch = [pltpu.VMEM((tm, d), jnp.bfloat16), pltpu.VMEM((tm, d), jnp.bfloat16)]
    if ml_extra is not None:
        wk_t, wg_hi, wg_lo = ml_extra
        nk = wk_t.shape[0]
        stacked = jnp.concatenate([wk_t, wg_hi, wg_lo], axis=0)
        in_specs += [pl.BlockSpec(stacked.shape, const, pipeline_mode=resident),
                     pl.BlockSpec(wg_hi.shape, const, pipeline_mode=resident)]
        args += [stacked, wg_hi]
        out_specs += [pl.BlockSpec((nk, tm), lambda s: (0, proj_tile(s))),
                      pl.BlockSpec((ML_GATE_ROWS, tm), lambda s: (0, proj_tile(s)))]
        out_shape += [jax.ShapeDtypeStruct((nk, t), jnp.bfloat16),
                      jax.ShapeDtypeStruct((ML_GATE_ROWS, t), jnp.float32)]
        scratch += [pltpu.VMEM((tm, d), jnp.bfloat16), pltpu.VMEM((tm, d), jnp.bfloat16)]
    if qk_norm is not None:
        in_specs += [pl.BlockSpec((1, LANES), const), pl.BlockSpec((1, LANES), const)]
        args += [qk_norm[2], qk_norm[3]]
    kern = functools.partial(_inproj_kernel, n_main=n_main, chunk=PROJ_COL_CHUNK, q_cols=q_cols,
                             q_scale=q_scale, gate_cols=gate_cols,
                             qk_norm=None if qk_norm is None else qk_norm[:2],
                             with_ml=ml_extra is not None, with_combine=pending is not None)
    return pl.pallas_call(
        kern,
        grid=(n_tiles + 1,),
        in_specs=in_specs,
        out_specs=out_specs,
        out_shape=out_shape,
        scratch_shapes=scratch,
        compiler_params=_params(("arbitrary",)),
        name="inproj_ml" if ml_extra is not None else "inproj_sw",
    )(*args)


def _mlstm_gate_terms(graw, bias, upper):
    H = ML_HEADS
    L = graw.shape[1]
    z = graw + bias
    gates = GATE_CAP * jnp.tanh(z / GATE_CAP)
    log_f = jnp.minimum(gates, 0.0) - jnp.log1p(jnp.exp(-jnp.abs(gates)))
    row = lax.broadcasted_iota(jnp.int32, (ML_GATE_ROWS, L), 0)
    lane = lax.broadcasted_iota(jnp.int32, (ML_GATE_ROWS, L), 1)
    is_i = row < H
    slab = jnp.where(is_i, gates, log_f)
    a1 = slab.astype(jnp.bfloat16)
    r1 = slab - a1.astype(jnp.float32)
    a2 = r1.astype(jnp.bfloat16)
    a3 = (r1 - a2.astype(jnp.float32)).astype(jnp.bfloat16)
    cum = _bdot(a1, upper) + (_bdot(a2, upper) + _bdot(a3, upper))
    ib = jnp.where(is_i, gates, cum)
    b = pltpu.roll(ib, ML_GATE_ROWS - H, 0)
    u = ib - b
    cm = u
    shift = 1
    while shift < L:
        cm = jnp.maximum(cm, jnp.where(lane >= shift, pltpu.roll(cm, shift, 1), -jnp.inf))
        shift *= 2
    return b, u, cm


def _mlstm_kernel(v_ref, o_ref, q_ref, kt_ref, gt_ref, gtn_ref, bg_ref, gout_ref, out_ref,
                  c_ref, m_ref, b_ref, u_ref, cm_ref, *, chunk):
    L = chunk
    H, dk, dv = ML_HEADS, ML_DK, ML_DV
    r_idx = lax.broadcasted_iota(jnp.int32, (L, L), 0)
    c_idx = lax.broadcasted_iota(jnp.int32, (L, L), 1)
    upper = jnp.where(r_idx <= c_idx, 1.0, 0.0).astype(jnp.bfloat16)
    causal = r_idx >= c_idx
    row = lax.broadcasted_iota(jnp.int32, (ML_GATE_ROWS, L), 0)
    ones_col = jnp.where(lax.broadcasted_iota(jnp.int32, (L, LANES), 1) == 0, 1.0, 0.0
                         ).astype(jnp.bfloat16)

    @pl.when(pl.program_id(1) == 0)
    def _():
        c_ref[...] = jnp.zeros_like(c_ref)
        m_ref[...] = jnp.zeros_like(m_ref)
        b0, u0, cm0 = _mlstm_gate_terms(gt_ref[...], bg_ref[...], upper)
        b_ref[...] = b0
        u_ref[...] = u0
        cm_ref[...] = cm0

    b16 = b_ref[...]
    u16 = u_ref[...]
    cm16 = cm_ref[...]
    b_n, u_n, cm_n = _mlstm_gate_terms(gtn_ref[...], bg_ref[...], upper)
    b_ref[...] = b_n
    u_ref[...] = u_n
    cm_ref[...] = cm_n

    m_prev = m_ref[:, 0:1]
    z16 = jnp.maximum(m_prev, cm16)
    w_inter16 = jnp.exp(m_prev - z16)
    e_negm16 = jnp.exp(-(b16 + z16))
    z_last = z16[:, L - 1:L]
    w_state16 = jnp.exp(u16 - z_last)
    decay16 = jnp.exp(m_prev - z_last)
    m_ref[...] = jnp.broadcast_to(b16[:, L - 1:L] + z_last, m_ref.shape)
    stacked = jnp.where(row < H, z16,
                        jnp.where(row < 2 * H, pltpu.roll(w_inter16, H, 0),
                                  pltpu.roll(e_negm16, 2 * H, 0)))
    cols = jnp.concatenate(
        [stacked, jnp.zeros((LANES - ML_GATE_ROWS, L), jnp.float32)], axis=0).T

    for h in range(H):
        u_r = u16[h:h + 1, :]
        z_c = cols[:, h:h + 1]
        w_inter = cols[:, H + h:H + h + 1]
        e_negm = cols[:, 2 * H + h:2 * H + h + 1]
        c_ext = c_ref[h]
        q = q_ref[:, h * dk:(h + 1) * dk]
        kt = kt_ref[h * dk:(h + 1) * dk, :]
        v_ext = jnp.concatenate([v_ref[:, h * dv:(h + 1) * dv], ones_col], axis=-1)

        w_intra = jnp.exp(jnp.where(causal, u_r - z_c, -jnp.inf))
        s = (_bdot(q, kt) * w_intra).astype(jnp.bfloat16)
        nd = w_inter * _bdot(q, c_ext.astype(jnp.bfloat16)) + _bdot(s, v_ext)
        den = nd[:, dv:dv + 1]
        hb = nd[:, :dv] * (1.0 / jnp.maximum(jnp.abs(den), e_negm))

        kw = (kt.astype(jnp.float32) * w_state16[h:h + 1, :]).astype(jnp.bfloat16)
        c_ref[h] = decay16[h:h + 1, :] * c_ext + _bdot(kw, v_ext)

        y = hb * lax.rsqrt(jnp.mean(hb * hb, axis=-1, keepdims=True) + EPS)
        y = y * gout_ref[:, h * dv:(h + 1) * dv]
        og = o_ref[:, h * dv:(h + 1) * dv].astype(jnp.float32)
        out_ref[:, h * dv:(h + 1) * dv] = (y * og).astype(out_ref.dtype)


def _mlstm_core(main, k_t, g_t, b_gate, g_out, bsz, seq):
    chunk = ML_CHUNK
    t = main.shape[0]
    nc = seq // chunk
    blk = lambda b, c: b * nc + c
    bg = jnp.zeros((ML_GATE_ROWS, 1), jnp.float32).at[:2 * ML_HEADS, 0].set(b_gate)
    return pl.pallas_call(
        functools.partial(_mlstm_kernel, chunk=chunk),
        grid=(bsz, nc),
        in_specs=[
            pl.BlockSpec((chunk, ML_V), lambda b, c: (blk(b, c), 0)),
            pl.BlockSpec((chunk, ML_V), lambda b, c: (blk(b, c), 1)),
            pl.BlockSpec((chunk, ML_QK), lambda b, c: (blk(b, c), 4)),
            pl.BlockSpec((ML_QK, chunk), lambda b, c: (0, blk(b, c))),
            pl.BlockSpec((ML_GATE_ROWS, chunk), lambda b, c: (0, blk(b, c))),
            pl.BlockSpec((ML_GATE_ROWS, chunk), lambda b, c: (0, blk(b, jnp.minimum(c + 1, nc - 1)))),
            pl.BlockSpec((ML_GATE_ROWS, 1), lambda b, c: (0, 0)),
            pl.BlockSpec((1, ML_V), lambda b, c: (0, 0)),
        ],
        out_specs=pl.BlockSpec((chunk, ML_V), lambda b, c: (blk(b, c), 0)),
        out_shape=jax.ShapeDtypeStruct((t, ML_V), jnp.bfloat16),
        scratch_shapes=[
            pltpu.VMEM((ML_HEADS, ML_DK, ML_DV + LANES), jnp.float32),
            pltpu.VMEM((ML_GATE_ROWS, LANES), jnp.float32),
            pltpu.VMEM((ML_GATE_ROWS, chunk), jnp.float32),
            pltpu.VMEM((ML_GATE_ROWS, chunk), jnp.float32),
            pltpu.VMEM((ML_GATE_ROWS, chunk), jnp.float32),
        ],
        compiler_params=_params(("arbitrary", "arbitrary")),
        name="mlstm_core",
    )(main, main, main, k_t, g_t, g_t, bg, g_out.reshape(1, ML_V))


def _swa_kernel(sinks_ref, q_ref, kc_ref, kp_ref, vc_ref, vp_ref, bias_ref, o_ref):
    W = SW_WINDOW
    lane = lax.broadcasted_iota(jnp.int32, (W, LANES), 1)
    low = lane < SW_DH
    row_col = lax.broadcasted_iota(jnp.int32, (SW_GROUP * W, 1), 0)
    bias = bias_ref[jnp.minimum(pl.program_id(1), 1)]
    for g in range(SW_KV_HEADS):
        sl = slice(g * LANES, (g + 1) * LANES)
        kn = jnp.concatenate([kp_ref[:, sl], kc_ref[:, sl]], axis=0)
        v2 = jnp.concatenate([vp_ref[:, sl], vc_ref[:, sl]], axis=0)
        sink = jnp.full((SW_GROUP * W, 1), sinks_ref[g * SW_GROUP + SW_GROUP - 1], jnp.float32)
        for j in range(SW_GROUP - 2, -1, -1):
            sink = jnp.where(row_col < (j + 1) * W, sinks_ref[g * SW_GROUP + j], sink)
        parts = []
        for p in range(2):
            c0 = g * SW_GROUP * SW_DH + p * LANES
            qp = q_ref[:, c0:c0 + LANES].astype(jnp.float32)
            parts.append(jnp.where(low, qp, 0.0).astype(jnp.bfloat16))
            parts.append(jnp.where(low, 0.0, qp).astype(jnp.bfloat16))
        q4 = jnp.concatenate(parts, axis=0)
        scores = _bdot_nt(q4, kn) + bias
        m = jnp.maximum(jnp.max(scores, axis=-1, keepdims=True), sink)
        pexp = jnp.exp2(scores - m)
        denom = jnp.sum(pexp, axis=-1, keepdims=True) + jnp.exp2(sink - m)
        o4 = _bdot(pexp.astype(jnp.bfloat16), v2) * (1.0 / denom)
        for p in range(2):
            oa = o4[(2 * p) * W:(2 * p + 1) * W]
            ob = o4[(2 * p + 1) * W:(2 * p + 2) * W]
            c0 = g * SW_GROUP * SW_DH + p * LANES
            o_ref[:, c0:c0 + LANES] = jnp.where(low, oa, ob).astype(o_ref.dtype)


def _swa_core(proj, sinks, bsz, seq):
    t = proj.shape[0]
    W = SW_WINDOW
    nb = seq // W
    dq = SW_Q_HEADS * SW_DH
    kv_w = SW_KV_HEADS * LANES
    k_blk = dq // kv_w
    v_blk = k_blk + 1
    cur = lambda b, n, s: b * nb + n
    prev = lambda b, n, s: b * nb + jnp.maximum(n - 1, 0)
    sinks = sinks.astype(jnp.float32) * LOG2E
    qi = (jnp.arange(SW_GROUP * W) % W)[:, None]
    ki = jnp.arange(2 * W)[None, :]
    rel = qi + W - ki
    in_win = (rel >= 0) & (rel < W)
    bias = jnp.stack([jnp.where(in_win & (ki >= W), 0.0, -jnp.inf),
                      jnp.where(in_win, 0.0, -jnp.inf)]).astype(jnp.float32)
    grid_spec = pltpu.PrefetchScalarGridSpec(
        num_scalar_prefetch=1,
        grid=(bsz, nb),
        in_specs=[
            pl.BlockSpec((W, dq), lambda b, n, s: (cur(b, n, s), 0)),
            pl.BlockSpec((W, kv_w), lambda b, n, s: (cur(b, n, s), k_blk)),
            pl.BlockSpec((W, kv_w), lambda b, n, s: (prev(b, n, s), k_blk)),
            pl.BlockSpec((W, kv_w), lambda b, n, s: (cur(b, n, s), v_blk)),
            pl.BlockSpec((W, kv_w), lambda b, n, s: (prev(b, n, s), v_blk)),
            pl.BlockSpec((2, SW_GROUP * W, 2 * W), lambda b, n, s: (0, 0, 0)),
        ],
        out_specs=pl.BlockSpec((W, dq), lambda b, n, s: (cur(b, n, s), 0)),
    )
    return pl.pallas_call(
        _swa_kernel,
        grid_spec=grid_spec,
        out_shape=jax.ShapeDtypeStruct((t, dq), jnp.bfloat16),
        compiler_params=_params(("arbitrary", "arbitrary")),
        name="swa_core",
    )(sinks, proj, proj, proj, proj, proj, bias)


def _pack_rounded_pairs(r):
    k = r.shape[1] // 2
    hi = lax.bitcast_convert_type(r[:, :k], jnp.uint32)
    lo = lax.bitcast_convert_type(r[:, k:], jnp.uint32)
    return hi | (lo >> 16)


def _pack_bf16_pairs(a):
    return _pack_rounded_pairs(a.astype(jnp.bfloat16).astype(jnp.float32))


def _unpack_bf16_pairs(u):
    hi = lax.bitcast_convert_type(u & jnp.uint32(0xFFFF0000), jnp.float32)
    lo = lax.bitcast_convert_type(u << 16, jnp.float32)
    return hi, lo


def _route_tile(x_new, g_ref, sc_ref, sh_ref, wh_ref, wl_ref, b_ref,
                hn_ref, info_ref, infot_ref, cnt_ref, carry_ref, earlier_ref, tm):
    hn = _modulated_norm(x_new, g_ref[...], sc_ref[...], sh_ref[...])
    h_hi = hn.astype(jnp.bfloat16)
    hi_f32 = h_hi.astype(jnp.float32)
    h_lo = (hn - hi_f32).astype(jnp.bfloat16)
    hn_ref[...] = _pack_rounded_pairs(hi_f32)
    wide = _bdot_nt(wl_ref[...], h_hi)
    logits = (wide[:ROUTER_LANES] + (_bdot_nt(wh_ref[...], h_lo) + wide[ROUTER_LANES:])
              + b_ref[...])
    E8 = EXPERTS_PER_GROUP
    sub = lax.broadcasted_iota(jnp.int32, (E8, tm), 0).astype(jnp.float32)
    big = float(ROUTER_LANES)
    neg = -jnp.inf

    gl = logits[N_EXPERTS:N_EXPERTS + N_GROUPS]
    gmax = jnp.max(gl, axis=0, keepdims=True)
    gsel = jnp.min(jnp.where(gl == gmax, sub, big), axis=0, keepdims=True)
    p_grp = 1.0 / jnp.sum(jnp.exp(gl - gmax), axis=0, keepdims=True)

    el = logits[0:E8]
    for grp in range(1, N_GROUPS):
        el = jnp.where(gsel == grp, logits[grp * E8:(grp + 1) * E8], el)
    v1 = jnp.max(el, axis=0, keepdims=True)
    j1 = jnp.min(jnp.where(el == v1, sub, big), axis=0, keepdims=True)
    el2 = jnp.where(sub == j1, neg, el)
    v2 = jnp.max(el2, axis=0, keepdims=True)
    j2 = jnp.min(jnp.where(el2 == v2, sub, big), axis=0, keepdims=True)
    i1 = gsel * E8 + j1
    i2 = gsel * E8 + j2
    e21 = jnp.exp(v2 - v1)
    gate1 = p_grp / (1.0 + e21)
    gate2 = p_grp * e21 / (1.0 + e21)

    erow = lax.broadcasted_iota(jnp.int32, (N_EXPERTS, tm), 0).astype(jnp.float32)
    hit1 = erow == i1
    hit2 = erow == i2
    onehot = jnp.where(hit1 | hit2, 1.0, 0.0)
    carry = carry_ref[:, 0:1]
    before = _bdot(onehot.astype(jnp.bfloat16), earlier_ref[...]) + carry
    rank1 = jnp.sum(jnp.where(hit1, before, 0.0), axis=0, keepdims=True)
    rank2 = jnp.sum(jnp.where(hit2, before, 0.0), axis=0, keepdims=True)
    total = carry + jnp.sum(onehot, axis=1, keepdims=True)
    carry_ref[...] = jnp.broadcast_to(total, carry_ref.shape)
    cnt_ref[...] = jnp.broadcast_to(total, cnt_ref.shape)

    info_t = jnp.where(sub == 0, i1, 0.0)
    info_t = jnp.where(sub == 1, i2, info_t)
    info_t = jnp.where(sub == 2, rank1, info_t)
    info_t = jnp.where(sub == 3, rank2, info_t)
    info_t = jnp.where(sub == 4, gate1, info_t)
    info_t = jnp.where(sub == 5, gate2, info_t)
    infot_ref[...] = info_t
    info_ref[...] = jnp.concatenate(
        [info_t, jnp.zeros((ROUTER_LANES - E8, tm), jnp.float32)], axis=0).T


def _router_kernel(a_ref, wo_ref, x_ref, gt_ref, g_ref, sc_ref, sh_ref, wh_ref, wl_ref, b_ref,
                   xo_ref, hn_ref, info_ref, infot_ref, cnt_ref, carry_ref, earlier_ref, *, tm):
    @pl.when(pl.program_id(0) == 0)
    def _():
        carry_ref[...] = jnp.zeros_like(carry_ref)
        r_idx = lax.broadcasted_iota(jnp.int32, (tm, tm), 0)
        c_idx = lax.broadcasted_iota(jnp.int32, (tm, tm), 1)
        earlier_ref[...] = jnp.where(r_idx < c_idx, 1.0, 0.0).astype(jnp.bfloat16)

    x_new = x_ref[...] + gt_ref[...] * _bdot(a_ref[...], wo_ref[...])
    xo_ref[...] = x_new
    _route_tile(x_new, g_ref, sc_ref, sh_ref, wh_ref, wl_ref, b_ref,
                hn_ref, info_ref, infot_ref, cnt_ref, carry_ref, earlier_ref, tm)


def _outproj_router(a, w_out, x2d, gt, g, sc, sh, w_hi, w_lo, bias, seq):
    t, d = x2d.shape
    tm = min(ROUTER_TILE, seq)
    tiles_per_batch = seq // tm
    per_batch = lambda i: (i // tiles_per_batch, 0, 0)
    const = lambda i: (0, 0)
    return pl.pallas_call(
        functools.partial(_router_kernel, tm=tm),
        grid=(t // tm,),
        in_specs=[
            pl.BlockSpec((tm, a.shape[1]), lambda i: (i, 0)),
            pl.BlockSpec(w_out.shape, const),
            pl.BlockSpec((tm, d), lambda i: (i, 0)),
            pl.BlockSpec((None, 1, d), per_batch),
            pl.BlockSpec((1, d), const),
            pl.BlockSpec((None, 1, d), per_batch),
            pl.BlockSpec((None, 1, d), per_batch),
            pl.BlockSpec((ROUTER_LANES, d), const),
            pl.BlockSpec((2 * ROUTER_LANES, d), const),
            pl.BlockSpec((ROUTER_LANES, 1), const),
        ],
        out_specs=[
            pl.BlockSpec((tm, d), lambda i: (i, 0)),
            pl.BlockSpec((tm, d // 2), lambda i: (i, 0)),
            pl.BlockSpec((tm, ROUTER_LANES), lambda i: (i, 0)),
            pl.BlockSpec((EXPERTS_PER_GROUP, tm), lambda i: (0, i)),
            pl.BlockSpec((N_EXPERTS, LANES), const),
        ],
        out_shape=[
            jax.ShapeDtypeStruct((t, d), jnp.float32),
            jax.ShapeDtypeStruct((t, d // 2), jnp.uint32),
            jax.ShapeDtypeStruct((t, ROUTER_LANES), jnp.float32),
            jax.ShapeDtypeStruct((EXPERTS_PER_GROUP, t), jnp.float32),
            jax.ShapeDtypeStruct((N_EXPERTS, LANES), jnp.float32),
        ],
        scratch_shapes=[pltpu.VMEM((N_EXPERTS, LANES), jnp.float32),
                        pltpu.VMEM((tm, tm), jnp.bfloat16)],
        compiler_params=_params(("arbitrary",)),
        name="outproj_router",
    )(a, w_out, x2d, gt, g.reshape(1, d), sc, sh, w_hi, jnp.concatenate([w_hi, w_lo], axis=0), bias)


def _sc_mesh():
    return plsc.VectorSubcoreMesh(core_axis_name="c", subcore_axis_name="s")


def _sc_dispatch(rows, d0, d1, pad_idx, n_slots):
    t, w = rows.shape
    n_pad = pad_idx.shape[1]
    zeros = jnp.zeros((SC_ROWS, w), rows.dtype)
    sem = (pltpu.PARALLEL, pltpu.ARBITRARY)
    parts = SC_WINDOW // SC_ROWS

    @pl.kernel(out_type=jax.ShapeDtypeStruct((n_slots + SC_WINDOW, w), rows.dtype), mesh=_sc_mesh(),
               scratch_types=[pltpu.SemaphoreType.DMA, pltpu.SemaphoreType.DMA])
    def dispatch(x_hbm, d0_hbm, d1_hbm, z_hbm, p_hbm, o_hbm, sem0, sem1):
        def scatter_rows(x_vmem, i0_vmem, i1_vmem):
            part = pl.ds(pl.program_id(1) * SC_ROWS, SC_ROWS)
            first = pltpu.async_copy(x_vmem, o_hbm.at[i0_vmem.at[0, part]], sem0)
            second = pltpu.async_copy(x_vmem, o_hbm.at[i1_vmem.at[0, part]], sem1)
            first.wait()
            second.wait()

        pltpu.emit_pipeline(
            scatter_rows,
            grid=(t // SC_WINDOW, parts),
            in_specs=[pl.BlockSpec((SC_ROWS, w), lambda i, j: (parts * i + j, 0)),
                      pl.BlockSpec((1, SC_WINDOW), lambda i, j: (0, i)),
                      pl.BlockSpec((1, SC_WINDOW), lambda i, j: (0, i))],
            out_specs=[],
            core_axis_name=("c", "s"),
            dimension_semantics=sem,
        )(x_hbm, d0_hbm, d1_hbm)

        def scatter_zeros(z_vmem, p_vmem):
            part = pl.ds(pl.program_id(1) * SC_ROWS, SC_ROWS)
            pltpu.sync_copy(z_vmem, o_hbm.at[p_vmem.at[0, part]])

        pltpu.emit_pipeline(
            scatter_zeros,
            grid=(n_pad // SC_WINDOW, parts),
            in_specs=[pl.BlockSpec((SC_ROWS, w), lambda i, j: (0, 0)),
                      pl.BlockSpec((1, SC_WINDOW), lambda i, j: (0, i))],
            out_specs=[],
            core_axis_name=("c", "s"),
            dimension_semantics=sem,
        )(z_hbm, p_hbm)

    return dispatch(rows, d0, d1, zeros, pad_idx)


def _sc_gather(src, idx):
    n_out = idx.shape[1]
    w = src.shape[1]
    parts = SC_WINDOW // SC_ROWS

    @pl.kernel(out_type=jax.ShapeDtypeStruct((n_out, w), src.dtype), mesh=_sc_mesh())
    def gather(x_hbm, i_hbm, o_hbm):
        def gather_rows(i_vmem, o_vmem):
            part = pl.ds(pl.program_id(1) * SC_ROWS, SC_ROWS)
            pltpu.sync_copy(x_hbm.at[i_vmem.at[0, part]], o_vmem)

        pltpu.emit_pipeline(
            gather_rows,
            grid=(n_out // SC_WINDOW, parts),
            in_specs=[pl.BlockSpec((1, SC_WINDOW), lambda i, j: (0, i))],
            out_specs=[pl.BlockSpec((SC_ROWS, w), lambda i, j: (parts * i + j, 0))],
            core_axis_name=("c", "s"),
            dimension_semantics=(pltpu.PARALLEL, pltpu.ARBITRARY),
        )(i_hbm, o_hbm)

    return gather(src, idx)


def _expert_kernel(meta_ref, x_ref, w1_hbm, w3_hbm, w2_hbm, y_ref,
                   w1_buf, w3_buf, w2_buf, w1_c, w3_c, w2_c, sems, *, layer, nb):
    def weight_copies(expert, s):
        return (pltpu.make_async_copy(w1_hbm.at[layer, expert], w1_buf.at[s], sems.at[s, 0]),
                pltpu.make_async_copy(w3_hbm.at[layer, expert], w3_buf.at[s], sems.at[s, 1]),
                pltpu.make_async_copy(w2_hbm.at[layer, expert], w2_buf.at[s], sems.at[s, 2]))

    @pl.when(pl.program_id(0) == 0)
    def _():
        for cp in weight_copies(meta_ref[0], meta_ref[nb]):
            cp.start()

    for j in range(EXPERT_BLOCKS_PER_STEP):
        i = pl.program_id(0) * EXPERT_BLOCKS_PER_STEP + j
        rows = slice(j * MOE_BLOCK, (j + 1) * MOE_BLOCK)
        e = meta_ref[i]
        slot = meta_ref[nb + i]
        nxt = meta_ref[2 * nb + i]
        used = i < meta_ref[3 * nb]
        first = used & ((i == 0) | (e != meta_ref[jnp.maximum(i - 1, 0)]))

        @pl.when(first)
        def _():
            for cp in weight_copies(e, slot):
                cp.wait()

            @pl.when(nxt >= 0)
            def _():
                for cp in weight_copies(nxt, 1 - slot):
                    cp.start()

            w1_c[...] = w1_buf[slot].astype(jnp.bfloat16)
            w3_c[...] = w3_buf[slot].astype(jnp.bfloat16)
            w2_c[...] = w2_buf[slot].astype(jnp.bfloat16)

        @pl.when(used)
        def _():
            x_hi, x_lo = _unpack_bf16_pairs(x_ref[rows, :])
            xb = jnp.concatenate([x_hi, x_lo], axis=-1).astype(jnp.bfloat16)
            full = 256
            tail = jnp.concatenate([w1_c[:, full:], w3_c[:, full:]], axis=-1)
            half_rows = MOE_BLOCK // 2
            h_tail = jnp.concatenate([_bdot(xb[:half_rows], tail), _bdot(xb[half_rows:], tail)], axis=0)
            h1 = jnp.concatenate([_bdot(xb, w1_c[:, :full]), h_tail[:, :LANES]], axis=-1)
            h3 = jnp.concatenate([_bdot(xb, w3_c[:, :full]), h_tail[:, LANES:]], axis=-1)
            act = (h1 * jax.nn.sigmoid(h1) * h3).astype(jnp.bfloat16)
            y_ref[rows, :] = _pack_bf16_pairs(_bdot(act, w2_c[...]))

        @pl.when(jnp.logical_not(used))
        def _():
            y_ref[rows, :] = jnp.zeros((MOE_BLOCK, y_ref.shape[1]), y_ref.dtype)


def _experts(x_slots, blk_meta, w1, w3, w2, layer, nb):
    dp = x_slots.shape[1]
    d, de = w1.shape[-2:]
    step_rows = EXPERT_BLOCKS_PER_STEP * MOE_BLOCK
    assert nb % EXPERT_BLOCKS_PER_STEP == 0
    last_used_step = lambda s: (s[3 * nb] - 1) // EXPERT_BLOCKS_PER_STEP
    grid_spec = pltpu.PrefetchScalarGridSpec(
        num_scalar_prefetch=1,
        grid=(nb // EXPERT_BLOCKS_PER_STEP,),
        in_specs=[
            pl.BlockSpec((step_rows, dp), lambda i, s: (jnp.minimum(i, last_used_step(s)), 0)),
            pl.BlockSpec(memory_space=pl.ANY),
            pl.BlockSpec(memory_space=pl.ANY),
            pl.BlockSpec(memory_space=pl.ANY),
        ],
        out_specs=pl.BlockSpec((step_rows, dp), lambda i, s: (i, 0)),
        scratch_shapes=[
            pltpu.VMEM((2, d, de), jnp.float32),
            pltpu.VMEM((2, d, de), jnp.float32),
            pltpu.VMEM((2, de, d), jnp.float32),
            pltpu.VMEM((d, de), jnp.bfloat16),
            pltpu.VMEM((d, de), jnp.bfloat16),
            pltpu.VMEM((de, d), jnp.bfloat16),
            pltpu.SemaphoreType.DMA((2, 3)),
        ],
    )
    return pl.pallas_call(
        functools.partial(_expert_kernel, layer=layer, nb=nb),
        grid_spec=grid_spec,
        out_shape=jax.ShapeDtypeStruct((nb * MOE_BLOCK, dp), jnp.uint32),
        compiler_params=_params(("arbitrary",)),
        name="moe_experts",
    )(blk_meta, x_slots, w1, w3, w2)


def _combine_kernel(x_ref, y1_ref, y2_ref, info_ref, gt_ref, o_ref):
    o_ref[...] = _moe_combined(x_ref[...], y1_ref, y2_ref, info_ref, gt_ref)


def _combine(x2d, y_pairs, info, gt, seq):
    t, d = x2d.shape
    tm = min(COMBINE_TILE, seq)
    tiles_per_batch = seq // tm
    second = t // tm
    return pl.pallas_call(
        _combine_kernel,
        grid=(t // tm,),
        in_specs=[
            pl.BlockSpec((tm, d), lambda i: (i, 0)),
            pl.BlockSpec((tm, d // 2), lambda i: (i, 0)),
            pl.BlockSpec((tm, d // 2), lambda i: (i + second, 0)),
            pl.BlockSpec((tm, ROUTER_LANES), lambda i: (i, 0)),
            pl.BlockSpec((None, 1, d), lambda i: (i // tiles_per_batch, 0, 0)),
        ],
        out_specs=pl.BlockSpec((tm, d), lambda i: (i, 0)),
        out_shape=jax.ShapeDtypeStruct((t, d), jnp.float32),
        compiler_params=_params(("arbitrary",)),
        name="moe_combine",
    )(x2d, y_pairs, y_pairs, info, gt)


def _slot_plan(info_t, cnt, t):
    counts = cnt[:, 0].astype(jnp.int32)
    padded = (counts + MOE_BLOCK - 1) // MOE_BLOCK * MOE_BLOCK
    pad_ends = jnp.cumsum(padded)
    pad_starts = pad_ends - padded
    nb = -(-(2 * t) // MOE_BLOCK) + N_EXPERTS
    n_slots = nb * MOE_BLOCK
    it = info_t.astype(jnp.int32)
    onehot_start = lambda e: jnp.sum(
        jnp.where(e[None, :] == jnp.arange(N_EXPERTS, dtype=jnp.int32)[:, None],
                  pad_starts[:, None], 0), axis=0)
    dest1 = (onehot_start(it[0]) + it[2]).reshape(1, t)
    dest2 = (onehot_start(it[1]) + it[3]).reshape(1, t)
    lane = jnp.arange(MOE_BLOCK, dtype=jnp.int32)[None, :]
    n_padding = (padded - counts)[:, None]
    wrapped = (pad_starts + counts)[:, None] + lane % jnp.maximum(n_padding, 1)
    pad_idx = jnp.where(n_padding > 0, wrapped, n_slots + lane % SC_WINDOW).reshape(-1)
    n_real = pad_ends[-1] // MOE_BLOCK
    n_used = -(-n_real // EXPERT_BLOCKS_PER_STEP) * EXPERT_BLOCKS_PER_STEP
    tail = jnp.arange((EXPERT_BLOCKS_PER_STEP - 1) * MOE_BLOCK, dtype=jnp.int32)
    tail_idx = jnp.where(tail < (n_used - n_real) * MOE_BLOCK, pad_ends[-1] + tail,
                         n_slots + tail % SC_WINDOW)
    pad_idx = jnp.concatenate([pad_idx, tail_idx]).reshape(1, -1)
    experts = jnp.arange(N_EXPERTS, dtype=jnp.int32)
    blk = jnp.arange(nb, dtype=jnp.int32)
    blk_exp = jnp.minimum(
        jnp.sum((pad_ends[None, :] <= (blk * MOE_BLOCK)[:, None]).astype(jnp.int32), axis=1),
        N_EXPERTS - 1)
    blk_exp = jnp.where(blk >= n_real, jnp.max(jnp.where(padded > 0, experts, 0)), blk_exp)
    prev_exp = jnp.concatenate([jnp.full((1,), -1, jnp.int32), blk_exp[:-1]])
    is_first = (blk < n_used) & (blk_exp != prev_exp)
    blk_slot = (jnp.cumsum(is_first.astype(jnp.int32)) + 1) % 2
    later = (experts[None, :] > experts[:, None]) & (padded[None, :] > 0)
    nxt_of = jnp.min(jnp.where(later, experts[None, :], N_EXPERTS), axis=1)
    nxt_of = jnp.where(nxt_of == N_EXPERTS, -1, nxt_of)
    blk_nxt = jnp.sum(jnp.where(blk_exp[:, None] == experts[None, :], nxt_of[None, :], 0), axis=1)
    blk_meta = jnp.concatenate([blk_exp, blk_slot, blk_nxt, n_used[None]]).astype(jnp.int32)
    return dest1, dest2, pad_idx, blk_meta, nb, n_slots


def _mixer_out_and_moe(a, w_out, x2d, gt1, g, sc, sh, w_group, b_group, w_router, b_router,
                       w1, w3, w2, layer, seq):
    t, d = x2d.shape
    w_cat = jnp.zeros((ROUTER_LANES, d), jnp.float32)
    w_cat = w_cat.at[:N_EXPERTS].set(w_router.T).at[N_EXPERTS:N_EXPERTS + N_GROUPS].set(w_group.T)
    b_cat = jnp.zeros((ROUTER_LANES, 1), jnp.float32)
    b_cat = b_cat.at[:N_EXPERTS, 0].set(b_router).at[N_EXPERTS:N_EXPERTS + N_GROUPS, 0].set(b_group)
    w_hi, w_lo = _split_hi_lo(w_cat)
    x_new, hn, info, info_t, cnt = _outproj_router(a, w_out, x2d, gt1, g, sc, sh, w_hi, w_lo,
                                                   b_cat, seq)
    dest1, dest2, pad_idx, blk_meta, nb, n_slots = _slot_plan(info_t, cnt, t)
    x_slots = _sc_dispatch(hn, dest1, dest2, pad_idx, n_slots)
    y_slots = _experts(x_slots, blk_meta, w1, w3, w2, layer, nb)
    y_pairs = _sc_gather(y_slots, jnp.concatenate([dest1, dest2], axis=1))
    return x_new, y_pairs, info


def kernel(x, c, w_ada, b_ada, norm1_g, norm2_g, ml_w_in, ml_b_gate, ml_g_out, ml_w_out,
           sw_w_in, sw_g_q, sw_g_k, sw_sinks, sw_w_out, moe_w_group, moe_b_group,
           moe_w_router, moe_b_router, moe_w1, moe_w3, moe_w2):
    bsz, seq, d = x.shape
    depth = w_ada.shape[0]
    bf = jnp.bfloat16
    mod = _ada_mod(c, w_ada, b_ada)
    x2d = x.reshape(bsz * seq, d)
    pending = None
    for layer in range(depth):
        sh1, sc1, gt1, sh2, sc2, gt2 = [
            mod[layer, :, i * d:(i + 1) * d].reshape(bsz, 1, d) for i in range(6)]
        j = layer // 2
        if layer % 2 == 0:
            w = ml_w_in[j]
            q_w, k_w = w[:, :ML_QK], w[:, ML_QK:2 * ML_QK]
            v_w = w[:, 2 * ML_QK:2 * ML_QK + ML_V]
            o_w = w[:, 2 * ML_QK + ML_V:2 * ML_QK + 2 * ML_V]
            g_w = w[:, 2 * ML_QK + 2 * ML_V:]
            w_main = jnp.concatenate([v_w, o_w, q_w], axis=1).astype(bf)
            wg_t = jnp.zeros((ML_GATE_ROWS, d), jnp.float32).at[:2 * ML_HEADS].set(g_w.T)
            wg_hi, wg_lo = _split_hi_lo(wg_t)
            wk_t = k_w.T.astype(bf)
            w_out = ml_w_out[j].astype(bf)
            outs = _inproj(x2d, norm1_g[layer], sc1, sh1, w_main, seq,
                           ml_extra=(wk_t, wg_hi, wg_lo),
                           q_cols=(2 * ML_V, 2 * ML_V + ML_QK), q_scale=ML_DK ** -0.5,
                           gate_cols=(ML_V, 2 * ML_V), pending=pending)
            if pending is not None:
                x2d, outs = outs[0], outs[1:]
            main, k_t, g_t = outs
            a = _mlstm_core(main, k_t, g_t, ml_b_gate[j], ml_g_out[j], bsz, seq)
        else:
            w = sw_w_in[j]
            dq = SW_Q_HEADS * SW_DH
            dkv = SW_KV_HEADS * SW_DH
            dup = lambda m: jnp.concatenate(
                [m.reshape(d, SW_KV_HEADS, 1, SW_DH)] * 2, axis=2).reshape(d, 2 * dkv)
            w_main = jnp.concatenate(
                [w[:, :dq], dup(w[:, dq:dq + dkv]), dup(w[:, dq + dkv:])], axis=1).astype(bf)
            w_out = sw_w_out[j].astype(bf)
            gq = jnp.concatenate([sw_g_q[j], sw_g_q[j]]).reshape(1, LANES) * (SW_DH ** -0.5 * LOG2E)
            gk = jnp.concatenate([sw_g_k[j], sw_g_k[j]]).reshape(1, LANES)
            outs = _inproj(x2d, norm1_g[layer], sc1, sh1, w_main, seq, pending=pending,
                           qk_norm=(dq, dq + 2 * dkv, gq, gk))
            if pending is not None:
                x2d, outs = outs[0], outs[1:]
            a = _swa_core(outs[0], sw_sinks[j], bsz, seq)
        x2d, y_pairs, info = _mixer_out_and_moe(
            a, w_out, x2d, gt1, norm2_g[layer], sc2, sh2, moe_w_group[layer], moe_b_group[layer],
            moe_w_router[layer], moe_b_router[layer], moe_w1, moe_w3, moe_w2, layer, seq)
        pending = (y_pairs, info, gt2)
    y_pairs, info, gt2 = pending
    return _combine(x2d, y_pairs, info, gt2, seq).reshape(bsz, seq, d)
```

```python
import functools

import jax
import jax.numpy as jnp
from jax import lax
from jax.experimental import pallas as pl
from jax.experimental.pallas import tpu as pltpu
from jax.experimental.pallas import tpu_sc as plsc

EPS = 1e-6
GATE_CAP = 15.0
LOG2E = 1.4426950408889634

ML_HEADS = 4
ML_DK = 128
ML_DV = 256
ML_QK = ML_HEADS * ML_DK
ML_V = ML_HEADS * ML_DV
ML_GATE_ROWS = 16

SW_Q_HEADS = 16
SW_KV_HEADS = 4
SW_GROUP = SW_Q_HEADS // SW_KV_HEADS
SW_DH = 64
SW_WINDOW = 128
LANES = 128

N_GROUPS = 8
EXPERTS_PER_GROUP = 8
N_EXPERTS = N_GROUPS * EXPERTS_PER_GROUP
MOE_BLOCK = 256
ROUTER_LANES = 128
SC_WINDOW = 128
SC_ROWS = 64
EXPERT_BLOCKS_PER_STEP = 4
EXPERT_STAGES = 3

SUBLANES = 8
TOKEN_TILE = 512
ROUTER_TILE = 1024
COMBINE_TILE = 1024
PROJ_COL_CHUNK = 1024
ML_CHUNK = 256
ADA_COL_TILE = 768
V7X_VMEM_BYTES = 64 * 1024 * 1024
VMEM_LIMIT = V7X_VMEM_BYTES - 8 * 1024 * 1024

_NT = (((1,), (1,)), ((), ()))


def _bdot(a, b):
    return jnp.dot(a, b, preferred_element_type=jnp.float32)


def _bdot_nt(a, b):
    return lax.dot_general(a, b, _NT, preferred_element_type=jnp.float32)


def _split_hi_lo(a):
    hi = a.astype(jnp.bfloat16)
    lo = (a - hi.astype(jnp.float32)).astype(jnp.bfloat16)
    return hi, lo


def _params(sem):
    return pltpu.CompilerParams(dimension_semantics=sem, vmem_limit_bytes=VMEM_LIMIT)


def _ada_kernel(c_ref, w_ref, b_ref, o_ref):
    c = c_ref[...]
    cond = c * jax.nn.sigmoid(c)
    c_hi, c_lo = _split_hi_lo(cond)
    w_hi, w_lo = _split_hi_lo(w_ref[...])
    acc = _bdot(c_hi, w_hi) + (_bdot(c_lo, w_hi) + _bdot(c_hi, w_lo))
    o_ref[...] = acc + b_ref[...]


def _ada_mod(c, w_ada, b_ada):
    depth, d, n = w_ada.shape
    bsz = c.shape[0]
    rows = SUBLANES
    tn = ADA_COL_TILE
    c_pad = jnp.zeros((rows, d), jnp.float32).at[:bsz].set(c)
    out = pl.pallas_call(
        _ada_kernel,
        grid=(depth, n // tn),
        in_specs=[
            pl.BlockSpec((rows, d), lambda l, j: (0, 0)),
            pl.BlockSpec((None, d, tn), lambda l, j: (l, 0, j)),
            pl.BlockSpec((None, 1, tn), lambda l, j: (l, 0, j)),
        ],
        out_specs=pl.BlockSpec((None, rows, tn), lambda l, j: (l, 0, j)),
        out_shape=jax.ShapeDtypeStruct((depth, rows, n), jnp.float32),
        compiler_params=_params(("arbitrary", "arbitrary")),
        name="ada_mod",
    )(c_pad, w_ada, b_ada.reshape(depth, 1, n))
    return out[:, :bsz]


def _modulated_norm(x, g, sc, sh):
    y = x * lax.rsqrt(jnp.mean(x * x, axis=-1, keepdims=True) + EPS)
    return y * (g * (1.0 + sc)) + sh


def _moe_combined(x, y1_ref, y2_ref, info_ref, gt_ref):
    info = info_ref[...]
    g1 = info[:, 4:5]
    g2 = info[:, 5:6]
    y1_hi, y1_lo = _unpack_bf16_pairs(y1_ref[...])
    y2_hi, y2_lo = _unpack_bf16_pairs(y2_ref[...])
    y = jnp.concatenate([g1 * y1_hi + g2 * y2_hi, g1 * y1_lo + g2 * y2_lo], axis=-1)
    return x + gt_ref[...] * y


def _qk_head_norm(acc, c0, qk_norm, gq, gk):
    q_hi, k_hi = qk_norm
    low = lax.broadcasted_iota(jnp.int32, (acc.shape[0], LANES), 1) < SW_DH
    slabs = []
    for j in range(acc.shape[1] // LANES):
        slab = acc[:, j * LANES:(j + 1) * LANES]
        if c0 + j * LANES < q_hi:
            sq = slab * slab
            ss_lo = jnp.sum(jnp.where(low, sq, 0.0), axis=-1, keepdims=True)
            ss_hi = jnp.sum(jnp.where(low, 0.0, sq), axis=-1, keepdims=True)
            rs = jnp.where(low, lax.rsqrt(ss_lo / SW_DH + EPS), lax.rsqrt(ss_hi / SW_DH + EPS))
            slab = slab * rs * gq
        elif c0 + j * LANES < k_hi:
            slab = slab * lax.rsqrt(jnp.mean(slab * slab, axis=-1, keepdims=True) + EPS) * gk
        slabs.append(slab)
    return jnp.concatenate(slabs, axis=-1)


def _inproj_kernel(*refs, n_main, chunk, q_cols, q_scale, gate_cols, qk_norm, with_ml,
                   with_combine):
    refs = list(refs)
    n_in = (5 + (4 if with_combine else 0) + (2 if with_ml else 0)
            + (2 if qk_norm is not None else 0))
    n_scratch = 4 if with_ml else 2
    ins, outs, scratch = refs[:n_in], refs[n_in:-n_scratch], refs[-n_scratch:]
    x_ref = ins.pop(0)
    if with_combine:
        y1_ref, y2_ref, info_ref, gtp_ref = ins[:4]
        ins = ins[4:]
        xo_ref = outs.pop(0)
    g_ref, sc_ref, sh_ref, w_ref = ins[:4]
    o_ref = outs[0]
    if with_ml:
        wk_ref, wgh_ref = ins[4:]
        kt_ref, gt_ref = outs[1:]
    if qk_norm is not None:
        gq_ref, gk_ref = ins[4:]

    def normalise(hb_dst, lo_dst):
        x = x_ref[...]
        if with_combine:
            x = _moe_combined(x, y1_ref, y2_ref, info_ref, gtp_ref)
            xo_ref[...] = x
        hn = _modulated_norm(x, g_ref[...], sc_ref[...], sh_ref[...])
        hb = hn.astype(jnp.bfloat16)
        hb_dst[...] = hb
        if with_ml:
            lo_dst[...] = (hn - hb.astype(jnp.float32)).astype(jnp.bfloat16)

    def project(hb_src, lo_src):
        hb = hb_src[...]
        for c0 in range(0, n_main, chunk):
            c1 = min(c0 + chunk, n_main)
            acc = _bdot(hb, w_ref[:, c0:c1])
            if q_cols is not None and q_cols[0] <= c0 < q_cols[1]:
                acc = acc * q_scale
            if gate_cols is not None and gate_cols[0] <= c0 < gate_cols[1]:
                acc = jax.nn.sigmoid(acc)
            if qk_norm is not None and c0 < qk_norm[1]:
                acc = _qk_head_norm(acc, c0, qk_norm, gq_ref[...], gk_ref[...])
            o_ref[:, c0:c1] = acc.astype(o_ref.dtype)
        if with_ml:
            nk = kt_ref.shape[0]
            stacked = _bdot_nt(wk_ref[...], hb)
            kt_ref[...] = stacked[:nk].astype(kt_ref.dtype)
            gt_ref[...] = (stacked[nk:nk + ML_GATE_ROWS]
                           + (_bdot_nt(wgh_ref[...], lo_src[...]) + stacked[nk + ML_GATE_ROWS:]))

    hb_a, hb_b = scratch[:2]
    lo_a, lo_b = scratch[2:] if with_ml else (None, None)
    s = pl.program_id(0)

    @pl.when(s == 0)
    def _():
        hb_b[...] = jnp.zeros_like(hb_b)
        if with_ml:
            lo_b[...] = jnp.zeros_like(lo_b)

    @pl.when(s % 2 == 0)
    def _():
        normalise(hb_a, lo_a)
        project(hb_b, lo_b)

    @pl.when(s % 2 == 1)
    def _():
        normalise(hb_b, lo_b)
        project(hb_a, lo_a)


def _inproj(x2d, g, sc, sh, w_main, seq, *, ml_extra=None, q_cols=None, q_scale=1.0,
            gate_cols=None, qk_norm=None, pending=None):
    t, d = x2d.shape
    tm = TOKEN_TILE if ml_extra is not None else min(2 * TOKEN_TILE, seq)
    n_main = w_main.shape[1]
    tiles_per_batch = seq // tm
    n_tiles = t // tm
    resident = pl.Buffered(1)
    norm_tile = lambda s: jnp.minimum(s, n_tiles - 1)
    proj_tile = lambda s: jnp.maximum(s - 1, 0)
    row = lambda s: (norm_tile(s), 0)
    per_batch = lambda s: (norm_tile(s) // tiles_per_batch, 0, 0)
    const = lambda s: (0, 0)
    in_specs = [pl.BlockSpec((tm, d), row)]
    args = [x2d]
    out_specs, out_shape = [], []
    if pending is not None:
        y_pairs, info, gt_prev = pending
        in_specs += [pl.BlockSpec((tm, d // 2), row),
                     pl.BlockSpec((tm, d // 2), lambda s: (norm_tile(s) + n_tiles, 0)),
                     pl.BlockSpec((tm, ROUTER_LANES), row),
                     pl.BlockSpec((None, 1, d), per_batch)]
        args += [y_pairs, y_pairs, info, gt_prev]
        out_specs += [pl.BlockSpec((tm, d), row)]
        out_shape += [jax.ShapeDtypeStruct((t, d), jnp.float32)]
    in_specs += [
        pl.BlockSpec((1, d), const),
        pl.BlockSpec((None, 1, d), per_batch),
        pl.BlockSpec((None, 1, d), per_batch),
        pl.BlockSpec((d, n_main), const, pipeline_mode=resident),
    ]
    args += [g.reshape(1, d), sc, sh, w_main]
    out_specs += [pl.BlockSpec((tm, n_main), lambda s: (proj_tile(s), 0))]
    out_shape += [jax.ShapeDtypeStruct((t, n_main), jnp.bfloat16)]
    scratch = [pltpu.VMEM((tm, d), jnp.bfloat16), pltpu.VMEM((tm, d), jnp.bfloat16)]
    if ml_extra is not None:
        wk_t, wg_hi, wg_lo = ml_extra
        nk = wk_t.shape[0]
        stacked = jnp.concatenate([wk_t, wg_hi, wg_lo], axis=0)
        in_specs += [pl.BlockSpec(stacked.shape, const, pipeline_mode=resident),
                     pl.BlockSpec(wg_hi.shape, const, pipeline_mode=resident)]
        args += [stacked, wg_hi]
        out_specs += [pl.BlockSpec((nk, tm), lambda s: (0, proj_tile(s))),
                      pl.BlockSpec((ML_GATE_ROWS, tm), lambda s: (0, proj_tile(s)))]
        out_shape += [jax.ShapeDtypeStruct((nk, t), jnp.bfloat16),
                      jax.ShapeDtypeStruct((ML_GATE_ROWS, t), jnp.float32)]
        scratch += [pltpu.VMEM((tm, d), jnp.bfloat16), pltpu.VMEM((tm, d), jnp.bfloat16)]
    if qk_norm is not None:
        in_specs += [pl.BlockSpec((1, LANES), const), pl.BlockSpec((1, LANES), const)]
        args += [qk_norm[2], qk_norm[3]]
    kern = functools.partial(_inproj_kernel, n_main=n_main, chunk=PROJ_COL_CHUNK, q_cols=q_cols,
                             q_scale=q_scale, gate_cols=gate_cols,
                             qk_norm=None if qk_norm is None else qk_norm[:2],
                             with_ml=ml_extra is not None, with_combine=pending is not None)
    return pl.pallas_call(
        kern,
        grid=(n_tiles + 1,),
        in_specs=in_specs,
        out_specs=out_specs,
        out_shape=out_shape,
        scratch_shapes=scratch,
        compiler_params=_params(("arbitrary",)),
        name="inproj_ml" if ml_extra is not None else "inproj_sw",
    )(*args)


def _mlstm_gate_terms(graw, bias, upper):
    H = ML_HEADS
    L = graw.shape[1]
    z = graw + bias
    gates = GATE_CAP * jnp.tanh(z / GATE_CAP)
    log_f = jnp.minimum(gates, 0.0) - jnp.log1p(jnp.exp(-jnp.abs(gates)))
    row = lax.broadcasted_iota(jnp.int32, (ML_GATE_ROWS, L), 0)
    lane = lax.broadcasted_iota(jnp.int32, (ML_GATE_ROWS, L), 1)
    is_i = row < H
    slab = jnp.where(is_i, gates, log_f)
    a1 = slab.astype(jnp.bfloat16)
    r1 = slab - a1.astype(jnp.float32)
    a2 = r1.astype(jnp.bfloat16)
    a3 = (r1 - a2.astype(jnp.float32)).astype(jnp.bfloat16)
    cum = _bdot(a1, upper) + (_bdot(a2, upper) + _bdot(a3, upper))
    ib = jnp.where(is_i, gates, cum)
    b = pltpu.roll(ib, ML_GATE_ROWS - H, 0)
    u = ib - b
    cm = u
    shift = 1
    while shift < L:
        cm = jnp.maximum(cm, jnp.where(lane >= shift, pltpu.roll(cm, shift, 1), -jnp.inf))
        shift *= 2
    return b, u, cm


def _mlstm_kernel(v_ref, o_ref, q_ref, kt_ref, gt_ref, gtn_ref, bg_ref, gout_ref, out_ref,
                  c_ref, m_ref, b_ref, u_ref, cm_ref, *, chunk):
    L = chunk
    H, dk, dv = ML_HEADS, ML_DK, ML_DV
    r_idx = lax.broadcasted_iota(jnp.int32, (L, L), 0)
    c_idx = lax.broadcasted_iota(jnp.int32, (L, L), 1)
    upper = jnp.where(r_idx <= c_idx, 1.0, 0.0).astype(jnp.bfloat16)
    causal = r_idx >= c_idx
    row = lax.broadcasted_iota(jnp.int32, (ML_GATE_ROWS, L), 0)
    ones_col = jnp.where(lax.broadcasted_iota(jnp.int32, (L, LANES), 1) == 0, 1.0, 0.0
                         ).astype(jnp.bfloat16)

    @pl.when(pl.program_id(1) == 0)
    def _():
        c_ref[...] = jnp.zeros_like(c_ref)
        m_ref[...] = jnp.zeros_like(m_ref)
        b0, u0, cm0 = _mlstm_gate_terms(gt_ref[...], bg_ref[...], upper)
        b_ref[...] = b0
        u_ref[...] = u0
        cm_ref[...] = cm0

    b16 = b_ref[...]
    u16 = u_ref[...]
    cm16 = cm_ref[...]
    b_n, u_n, cm_n = _mlstm_gate_terms(gtn_ref[...], bg_ref[...], upper)
    b_ref[...] = b_n
    u_ref[...] = u_n
    cm_ref[...] = cm_n

    m_prev = m_ref[:, 0:1]
    z16 = jnp.maximum(m_prev, cm16)
    w_inter16 = jnp.exp(m_prev - z16)
    e_negm16 = jnp.exp(-(b16 + z16))
    z_last = z16[:, L - 1:L]
    w_state16 = jnp.exp(u16 - z_last)
    decay16 = jnp.exp(m_prev - z_last)
    m_ref[...] = jnp.broadcast_to(b16[:, L - 1:L] + z_last, m_ref.shape)
    stacked = jnp.where(row < H, z16,
                        jnp.where(row < 2 * H, pltpu.roll(w_inter16, H, 0),
                                  pltpu.roll(e_negm16, 2 * H, 0)))
    cols = jnp.concatenate(
        [stacked, jnp.zeros((LANES - ML_GATE_ROWS, L), jnp.float32)], axis=0).T

    for h in range(H):
        u_r = u16[h:h + 1, :]
        z_c = cols[:, h:h + 1]
        w_inter = cols[:, H + h:H + h + 1]
        e_negm = cols[:, 2 * H + h:2 * H + h + 1]
        c_ext = c_ref[h]
        q = q_ref[:, h * dk:(h + 1) * dk]
        kt = kt_ref[h * dk:(h + 1) * dk, :]
        v_ext = jnp.concatenate([v_ref[:, h * dv:(h + 1) * dv], ones_col], axis=-1)

        w_intra = jnp.exp(jnp.where(causal, u_r - z_c, -jnp.inf))
        s = (_bdot(q, kt) * w_intra).astype(jnp.bfloat16)
        nd = w_inter * _bdot(q, c_ext.astype(jnp.bfloat16)) + _bdot(s, v_ext)
        den = nd[:, dv:dv + 1]
        hb = nd[:, :dv] * (1.0 / jnp.maximum(jnp.abs(den), e_negm))

        kw = (kt.astype(jnp.float32) * w_state16[h:h + 1, :]).astype(jnp.bfloat16)
        c_ref[h] = decay16[h:h + 1, :] * c_ext + _bdot(kw, v_ext)

        y = hb * lax.rsqrt(jnp.mean(hb * hb, axis=-1, keepdims=True) + EPS)
        y = y * gout_ref[:, h * dv:(h + 1) * dv]
        og = o_ref[:, h * dv:(h + 1) * dv].astype(jnp.float32)
        out_ref[:, h * dv:(h + 1) * dv] = (y * og).astype(out_ref.dtype)


def _mlstm_core(main, k_t, g_t, b_gate, g_out, bsz, seq):
    chunk = ML_CHUNK
    t = main.shape[0]
    nc = seq // chunk
    blk = lambda b, c: b * nc + c
    bg = jnp.zeros((ML_GATE_ROWS, 1), jnp.float32).at[:2 * ML_HEADS, 0].set(b_gate)
    return pl.pallas_call(
        functools.partial(_mlstm_kernel, chunk=chunk),
        grid=(bsz, nc),
        in_specs=[
            pl.BlockSpec((chunk, ML_V), lambda b, c: (blk(b, c), 0)),
            pl.BlockSpec((chunk, ML_V), lambda b, c: (blk(b, c), 1)),
            pl.BlockSpec((chunk, ML_QK), lambda b, c: (blk(b, c), 4)),
            pl.BlockSpec((ML_QK, chunk), lambda b, c: (0, blk(b, c))),
            pl.BlockSpec((ML_GATE_ROWS, chunk), lambda b, c: (0, blk(b, c))),
            pl.BlockSpec((ML_GATE_ROWS, chunk), lambda b, c: (0, blk(b, jnp.minimum(c + 1, nc - 1)))),
            pl.BlockSpec((ML_GATE_ROWS, 1), lambda b, c: (0, 0)),
            pl.BlockSpec((1, ML_V), lambda b, c: (0, 0)),
        ],
        out_specs=pl.BlockSpec((chunk, ML_V), lambda b, c: (blk(b, c), 0)),
        out_shape=jax.ShapeDtypeStruct((t, ML_V), jnp.bfloat16),
        scratch_shapes=[
            pltpu.VMEM((ML_HEADS, ML_DK, ML_DV + LANES), jnp.float32),
            pltpu.VMEM((ML_GATE_ROWS, LANES), jnp.float32),
            pltpu.VMEM((ML_GATE_ROWS, chunk), jnp.float32),
            pltpu.VMEM((ML_GATE_ROWS, chunk), jnp.float32),
            pltpu.VMEM((ML_GATE_ROWS, chunk), jnp.float32),
        ],
        compiler_params=_params(("arbitrary", "arbitrary")),
        name="mlstm_core",
    )(main, main, main, k_t, g_t, g_t, bg, g_out.reshape(1, ML_V))


def _swa_kernel(sinks_ref, q_ref, kc_ref, kp_ref, vc_ref, vp_ref, bias_ref, o_ref):
    W = SW_WINDOW
    lane = lax.broadcasted_iota(jnp.int32, (W, LANES), 1)
    low = lane < SW_DH
    row_col = lax.broadcasted_iota(jnp.int32, (SW_GROUP * W, 1), 0)
    bias = bias_ref[jnp.minimum(pl.program_id(1), 1)]
    for g in range(SW_KV_HEADS):
        sl = slice(g * LANES, (g + 1) * LANES)
        kn = jnp.concatenate([kp_ref[:, sl], kc_ref[:, sl]], axis=0)
        v2 = jnp.concatenate([vp_ref[:, sl], vc_ref[:, sl]], axis=0)
        sink = jnp.full((SW_GROUP * W, 1), sinks_ref[g * SW_GROUP + SW_GROUP - 1], jnp.float32)
        for j in range(SW_GROUP - 2, -1, -1):
            sink = jnp.where(row_col < (j + 1) * W, sinks_ref[g * SW_GROUP + j], sink)
        parts = []
        for p in range(2):
            c0 = g * SW_GROUP * SW_DH + p * LANES
            qp = q_ref[:, c0:c0 + LANES].astype(jnp.float32)
            parts.append(jnp.where(low, qp, 0.0).astype(jnp.bfloat16))
            parts.append(jnp.where(low, 0.0, qp).astype(jnp.bfloat16))
        q4 = jnp.concatenate(parts, axis=0)
        scores = _bdot_nt(q4, kn) + bias
        m = jnp.maximum(jnp.max(scores, axis=-1, keepdims=True), sink)
        pexp = jnp.exp2(scores - m)
        denom = jnp.sum(pexp, axis=-1, keepdims=True) + jnp.exp2(sink - m)
        o4 = _bdot(pexp.astype(jnp.bfloat16), v2) * (1.0 / denom)
        for p in range(2):
            oa = o4[(2 * p) * W:(2 * p + 1) * W]
            ob = o4[(2 * p + 1) * W:(2 * p + 2) * W]
            c0 = g * SW_GROUP * SW_DH + p * LANES
            o_ref[:, c0:c0 + LANES] = jnp.where(low, oa, ob).astype(o_ref.dtype)


def _swa_core(proj, sinks, bsz, seq):
    t = proj.shape[0]
    W = SW_WINDOW
    nb = seq // W
    dq = SW_Q_HEADS * SW_DH
    kv_w = SW_KV_HEADS * LANES
    k_blk = dq // kv_w
    v_blk = k_blk + 1
    cur = lambda b, n, s: b * nb + n
    prev = lambda b, n, s: b * nb + jnp.maximum(n - 1, 0)
    sinks = sinks.astype(jnp.float32) * LOG2E
    qi = (jnp.arange(SW_GROUP * W) % W)[:, None]
    ki = jnp.arange(2 * W)[None, :]
    rel = qi + W - ki
    in_win = (rel >= 0) & (rel < W)
    bias = jnp.stack([jnp.where(in_win & (ki >= W), 0.0, -jnp.inf),
                      jnp.where(in_win, 0.0, -jnp.inf)]).astype(jnp.float32)
    grid_spec = pltpu.PrefetchScalarGridSpec(
        num_scalar_prefetch=1,
        grid=(bsz, nb),
        in_specs=[
            pl.BlockSpec((W, dq), lambda b, n, s: (cur(b, n, s), 0)),
            pl.BlockSpec((W, kv_w), lambda b, n, s: (cur(b, n, s), k_blk)),
            pl.BlockSpec((W, kv_w), lambda b, n, s: (prev(b, n, s), k_blk)),
            pl.BlockSpec((W, kv_w), lambda b, n, s: (cur(b, n, s), v_blk)),
            pl.BlockSpec((W, kv_w), lambda b, n, s: (prev(b, n, s), v_blk)),
            pl.BlockSpec((2, SW_GROUP * W, 2 * W), lambda b, n, s: (0, 0, 0)),
        ],
        out_specs=pl.BlockSpec((W, dq), lambda b, n, s: (cur(b, n, s), 0)),
    )
    return pl.pallas_call(
        _swa_kernel,
        grid_spec=grid_spec,
        out_shape=jax.ShapeDtypeStruct((t, dq), jnp.bfloat16),
        compiler_params=_params(("arbitrary", "arbitrary")),
        name="swa_core",
    )(sinks, proj, proj, proj, proj, proj, bias)


def _pack_rounded_pairs(r):
    k = r.shape[1] // 2
    hi = lax.bitcast_convert_type(r[:, :k], jnp.uint32)
    lo = lax.bitcast_convert_type(r[:, k:], jnp.uint32)
    return hi | (lo >> 16)


def _pack_bf16_pairs(a):
    return _pack_rounded_pairs(a.astype(jnp.bfloat16).astype(jnp.float32))


def _unpack_bf16_pairs(u):
    hi = lax.bitcast_convert_type(u & jnp.uint32(0xFFFF0000), jnp.float32)
    lo = lax.bitcast_convert_type(u << 16, jnp.float32)
    return hi, lo


def _route_tile(x_new, g_ref, sc_ref, sh_ref, wh_ref, wl_ref, b_ref,
                hn_ref, info_ref, infot_ref, cnt_ref, carry_ref, earlier_ref, tm):
    hn = _modulated_norm(x_new, g_ref[...], sc_ref[...], sh_ref[...])
    h_hi = hn.astype(jnp.bfloat16)
    hi_f32 = h_hi.astype(jnp.float32)
    h_lo = (hn - hi_f32).astype(jnp.bfloat16)
    hn_ref[...] = _pack_rounded_pairs(hi_f32)
    wide = _bdot_nt(wl_ref[...], h_hi)
    logits = (wide[:ROUTER_LANES] + (_bdot_nt(wh_ref[...], h_lo) + wide[ROUTER_LANES:])
              + b_ref[...])
    E8 = EXPERTS_PER_GROUP
    sub = lax.broadcasted_iota(jnp.int32, (E8, tm), 0).astype(jnp.float32)
    big = float(ROUTER_LANES)
    neg = -jnp.inf

    gl = logits[N_EXPERTS:N_EXPERTS + N_GROUPS]
    gmax = jnp.max(gl, axis=0, keepdims=True)
    gsel = jnp.min(jnp.where(gl == gmax, sub, big), axis=0, keepdims=True)
    p_grp = 1.0 / jnp.sum(jnp.exp(gl - gmax), axis=0, keepdims=True)

    el = logits[0:E8]
    for grp in range(1, N_GROUPS):
        el = jnp.where(gsel == grp, logits[grp * E8:(grp + 1) * E8], el)
    v1 = jnp.max(el, axis=0, keepdims=True)
    j1 = jnp.min(jnp.where(el == v1, sub, big), axis=0, keepdims=True)
    el2 = jnp.where(sub == j1, neg, el)
    v2 = jnp.max(el2, axis=0, keepdims=True)
    j2 = jnp.min(jnp.where(el2 == v2, sub, big), axis=0, keepdims=True)
    i1 = gsel * E8 + j1
    i2 = gsel * E8 + j2
    e21 = jnp.exp(v2 - v1)
    gate1 = p_grp / (1.0 + e21)
    gate2 = p_grp * e21 / (1.0 + e21)

    erow = lax.broadcasted_iota(jnp.int32, (N_EXPERTS, tm), 0).astype(jnp.float32)
    hit1 = erow == i1
    hit2 = erow == i2
    onehot = jnp.where(hit1 | hit2, 1.0, 0.0)
    carry = carry_ref[:, 0:1]
    before = _bdot(onehot.astype(jnp.bfloat16), earlier_ref[...]) + carry
    rank1 = jnp.sum(jnp.where(hit1, before, 0.0), axis=0, keepdims=True)
    rank2 = jnp.sum(jnp.where(hit2, before, 0.0), axis=0, keepdims=True)
    total = carry + jnp.sum(onehot, axis=1, keepdims=True)
    carry_ref[...] = jnp.broadcast_to(total, carry_ref.shape)
    cnt_ref[...] = jnp.broadcast_to(total, cnt_ref.shape)

    info_t = jnp.where(sub == 0, i1, 0.0)
    info_t = jnp.where(sub == 1, i2, info_t)
    info_t = jnp.where(sub == 2, rank1, info_t)
    info_t = jnp.where(sub == 3, rank2, info_t)
    info_t = jnp.where(sub == 4, gate1, info_t)
    info_t = jnp.where(sub == 5, gate2, info_t)
    infot_ref[...] = info_t
    info_ref[...] = jnp.concatenate(
        [info_t, jnp.zeros((ROUTER_LANES - E8, tm), jnp.float32)], axis=0).T


def _router_kernel(a_ref, wo_ref, x_ref, gt_ref, g_ref, sc_ref, sh_ref, wh_ref, wl_ref, b_ref,
                   xo_ref, hn_ref, info_ref, infot_ref, cnt_ref, carry_ref, earlier_ref, *, tm):
    @pl.when(pl.program_id(0) == 0)
    def _():
        carry_ref[...] = jnp.zeros_like(carry_ref)
        r_idx = lax.broadcasted_iota(jnp.int32, (tm, tm), 0)
        c_idx = lax.broadcasted_iota(jnp.int32, (tm, tm), 1)
        earlier_ref[...] = jnp.where(r_idx < c_idx, 1.0, 0.0).astype(jnp.bfloat16)

    x_new = x_ref[...] + gt_ref[...] * _bdot(a_ref[...], wo_ref[...])
    xo_ref[...] = x_new
    _route_tile(x_new, g_ref, sc_ref, sh_ref, wh_ref, wl_ref, b_ref,
                hn_ref, info_ref, infot_ref, cnt_ref, carry_ref, earlier_ref, tm)


def _outproj_router(a, w_out, x2d, gt, g, sc, sh, w_hi, w_lo, bias, seq):
    t, d = x2d.shape
    tm = min(ROUTER_TILE, seq)
    tiles_per_batch = seq // tm
    per_batch = lambda i: (i // tiles_per_batch, 0, 0)
    const = lambda i: (0, 0)
    return pl.pallas_call(
        functools.partial(_router_kernel, tm=tm),
        grid=(t // tm,),
        in_specs=[
            pl.BlockSpec((tm, a.shape[1]), lambda i: (i, 0)),
            pl.BlockSpec(w_out.shape, const),
            pl.BlockSpec((tm, d), lambda i: (i, 0)),
            pl.BlockSpec((None, 1, d), per_batch),
            pl.BlockSpec((1, d), const),
            pl.BlockSpec((None, 1, d), per_batch),
            pl.BlockSpec((None, 1, d), per_batch),
            pl.BlockSpec((ROUTER_LANES, d), const),
            pl.BlockSpec((2 * ROUTER_LANES, d), const),
            pl.BlockSpec((ROUTER_LANES, 1), const),
        ],
        out_specs=[
            pl.BlockSpec((tm, d), lambda i: (i, 0)),
            pl.BlockSpec((tm, d // 2), lambda i: (i, 0)),
            pl.BlockSpec((tm, ROUTER_LANES), lambda i: (i, 0)),
            pl.BlockSpec((EXPERTS_PER_GROUP, tm), lambda i: (0, i)),
            pl.BlockSpec((N_EXPERTS, LANES), const),
        ],
        out_shape=[
            jax.ShapeDtypeStruct((t, d), jnp.float32),
            jax.ShapeDtypeStruct((t, d // 2), jnp.uint32),
            jax.ShapeDtypeStruct((t, ROUTER_LANES), jnp.float32),
            jax.ShapeDtypeStruct((EXPERTS_PER_GROUP, t), jnp.float32),
            jax.ShapeDtypeStruct((N_EXPERTS, LANES), jnp.float32),
        ],
        scratch_shapes=[pltpu.VMEM((N_EXPERTS, LANES), jnp.float32),
                        pltpu.VMEM((tm, tm), jnp.bfloat16)],
        compiler_params=_params(("arbitrary",)),
        name="outproj_router",
    )(a, w_out, x2d, gt, g.reshape(1, d), sc, sh, w_hi, jnp.concatenate([w_hi, w_lo], axis=0), bias)


def _sc_mesh():
    return plsc.VectorSubcoreMesh(core_axis_name="c", subcore_axis_name="s")


def _sc_dispatch(rows, d0, d1, pad_idx, n_slots):
    t, w = rows.shape
    n_pad = pad_idx.shape[1]
    zeros = jnp.zeros((SC_ROWS, w), rows.dtype)
    sem = (pltpu.PARALLEL, pltpu.ARBITRARY)
    parts = SC_WINDOW // SC_ROWS

    @pl.kernel(out_type=jax.ShapeDtypeStruct((n_slots + SC_WINDOW, w), rows.dtype), mesh=_sc_mesh(),
               scratch_types=[pltpu.SemaphoreType.DMA, pltpu.SemaphoreType.DMA])
    def dispatch(x_hbm, d0_hbm, d1_hbm, z_hbm, p_hbm, o_hbm, sem0, sem1):
        def scatter_rows(x_vmem, i0_vmem, i1_vmem):
            part = pl.ds(pl.program_id(1) * SC_ROWS, SC_ROWS)
            first = pltpu.async_copy(x_vmem, o_hbm.at[i0_vmem.at[0, part]], sem0)
            second = pltpu.async_copy(x_vmem, o_hbm.at[i1_vmem.at[0, part]], sem1)
            first.wait()
            second.wait()

        pltpu.emit_pipeline(
            scatter_rows,
            grid=(t // SC_WINDOW, parts),
            in_specs=[pl.BlockSpec((SC_ROWS, w), lambda i, j: (parts * i + j, 0)),
                      pl.BlockSpec((1, SC_WINDOW), lambda i, j: (0, i)),
                      pl.BlockSpec((1, SC_WINDOW), lambda i, j: (0, i))],
            out_specs=[],
            core_axis_name=("c", "s"),
            dimension_semantics=sem,
        )(x_hbm, d0_hbm, d1_hbm)

        def scatter_zeros(z_vmem, p_vmem):
            part = pl.ds(pl.program_id(1) * SC_ROWS, SC_ROWS)
            pltpu.sync_copy(z_vmem, o_hbm.at[p_vmem.at[0, part]])

        pltpu.emit_pipeline(
            scatter_zeros,
            grid=(n_pad // SC_WINDOW, parts),
            in_specs=[pl.BlockSpec((SC_ROWS, w), lambda i, j: (0, 0)),
                      pl.BlockSpec((1, SC_WINDOW), lambda i, j: (0, i))],
            out_specs=[],
            core_axis_name=("c", "s"),
            dimension_semantics=sem,
        )(z_hbm, p_hbm)

    return dispatch(rows, d0, d1, zeros, pad_idx)


def _sc_gather(src, idx):
    n_out = idx.shape[1]
    w = src.shape[1]
    parts = SC_WINDOW // SC_ROWS

    @pl.kernel(out_type=jax.ShapeDtypeStruct((n_out, w), src.dtype), mesh=_sc_mesh())
    def gather(x_hbm, i_hbm, o_hbm):
        def gather_rows(i_vmem, o_vmem):
            part = pl.ds(pl.program_id(1) * SC_ROWS, SC_ROWS)
            pltpu.sync_copy(x_hbm.at[i_vmem.at[0, part]], o_vmem)

        pltpu.emit_pipeline(
            gather_rows,
            grid=(n_out // SC_WINDOW, parts),
            in_specs=[pl.BlockSpec((1, SC_WINDOW), lambda i, j: (0, i))],
            out_specs=[pl.BlockSpec((SC_ROWS, w), lambda i, j: (parts * i + j, 0))],
            core_axis_name=("c", "s"),
            dimension_semantics=(pltpu.PARALLEL, pltpu.ARBITRARY),
        )(i_hbm, o_hbm)

    return gather(src, idx)


def _expert_kernel(meta_ref, x_ref, w1_hbm, w3_hbm, w2_hbm, y_ref,
                   w1_buf, w3_buf, w2_buf, w1_c, w3_c, w2_c, sems, *, layer, nb):
    def weight_copies(expert, s):
        return (pltpu.make_async_copy(w1_hbm.at[layer, expert], w1_buf.at[s], sems.at[s, 0]),
                pltpu.make_async_copy(w3_hbm.at[layer, expert], w3_buf.at[s], sems.at[s, 1]),
                pltpu.make_async_copy(w2_hbm.at[layer, expert], w2_buf.at[s], sems.at[s, 2]))

    @pl.when(pl.program_id(0) == 0)
    def _():
        for cp in weight_copies(meta_ref[0], 0):
            cp.start()
        second = meta_ref[3 * nb + 1]

        @pl.when(second >= 0)
        def _():
            for cp in weight_copies(second, 1):
                cp.start()

    for j in range(EXPERT_BLOCKS_PER_STEP):
        i = pl.program_id(0) * EXPERT_BLOCKS_PER_STEP + j
        rows = slice(j * MOE_BLOCK, (j + 1) * MOE_BLOCK)
        e = meta_ref[i]
        slot = meta_ref[nb + i]
        nxt = meta_ref[2 * nb + i]
        used = i < meta_ref[3 * nb]
        first = used & ((i == 0) | (e != meta_ref[jnp.maximum(i - 1, 0)]))

        @pl.when(first)
        def _():
            for cp in weight_copies(e, slot):
                cp.wait()

            @pl.when(nxt >= 0)
            def _():
                for cp in weight_copies(nxt, (slot + EXPERT_STAGES - 1) % EXPERT_STAGES):
                    cp.start()

            w1_c[...] = w1_buf[slot].astype(jnp.bfloat16)
            w3_c[...] = w3_buf[slot].astype(jnp.bfloat16)
            w2_c[...] = w2_buf[slot].astype(jnp.bfloat16)

        @pl.when(used)
        def _():
            x_hi, x_lo = _unpack_bf16_pairs(x_ref[rows, :])
            xb = jnp.concatenate([x_hi, x_lo], axis=-1).astype(jnp.bfloat16)
            full = 256
            tail = jnp.concatenate([w1_c[:, full:], w3_c[:, full:]], axis=-1)
            half_rows = MOE_BLOCK // 2
            h_tail = jnp.concatenate([_bdot(xb[:half_rows], tail), _bdot(xb[half_rows:], tail)], axis=0)
            h1 = jnp.concatenate([_bdot(xb, w1_c[:, :full]), h_tail[:, :LANES]], axis=-1)
            h3 = jnp.concatenate([_bdot(xb, w3_c[:, :full]), h_tail[:, LANES:]], axis=-1)
            act = (h1 * jax.nn.sigmoid(h1) * h3).astype(jnp.bfloat16)
            y_ref[rows, :] = _pack_bf16_pairs(_bdot(act, w2_c[...]))

        @pl.when(jnp.logical_not(used))
        def _():
            y_ref[rows, :] = jnp.zeros((MOE_BLOCK, y_ref.shape[1]), y_ref.dtype)


def _experts(x_slots, blk_meta, w1, w3, w2, layer, nb):
    dp = x_slots.shape[1]
    d, de = w1.shape[-2:]
    step_rows = EXPERT_BLOCKS_PER_STEP * MOE_BLOCK
    assert nb % EXPERT_BLOCKS_PER_STEP == 0
    last_used_step = lambda s: (s[3 * nb] - 1) // EXPERT_BLOCKS_PER_STEP
    grid_spec = pltpu.PrefetchScalarGridSpec(
        num_scalar_prefetch=1,
        grid=(nb // EXPERT_BLOCKS_PER_STEP,),
        in_specs=[
            pl.BlockSpec((step_rows, dp), lambda i, s: (jnp.minimum(i, last_used_step(s)), 0)),
            pl.BlockSpec(memory_space=pl.ANY),
            pl.BlockSpec(memory_space=pl.ANY),
            pl.BlockSpec(memory_space=pl.ANY),
        ],
        out_specs=pl.BlockSpec((step_rows, dp), lambda i, s: (i, 0)),
        scratch_shapes=[
            pltpu.VMEM((EXPERT_STAGES, d, de), jnp.float32),
            pltpu.VMEM((EXPERT_STAGES, d, de), jnp.float32),
            pltpu.VMEM((EXPERT_STAGES, de, d), jnp.float32),
            pltpu.VMEM((d, de), jnp.bfloat16),
            pltpu.VMEM((d, de), jnp.bfloat16),
            pltpu.VMEM((de, d), jnp.bfloat16),
            pltpu.SemaphoreType.DMA((EXPERT_STAGES, 3)),
        ],
    )
    return pl.pallas_call(
        functools.partial(_expert_kernel, layer=layer, nb=nb),
        grid_spec=grid_spec,
        out_shape=jax.ShapeDtypeStruct((nb * MOE_BLOCK, dp), jnp.uint32),
        compiler_params=_params(("arbitrary",)),
        name="moe_experts",
    )(blk_meta, x_slots, w1, w3, w2)


def _combine_kernel(x_ref, y1_ref, y2_ref, info_ref, gt_ref, o_ref):
    o_ref[...] = _moe_combined(x_ref[...], y1_ref, y2_ref, info_ref, gt_ref)


def _combine(x2d, y_pairs, info, gt, seq):
    t, d = x2d.shape
    tm = min(COMBINE_TILE, seq)
    tiles_per_batch = seq // tm
    second = t // tm
    return pl.pallas_call(
        _combine_kernel,
        grid=(t // tm,),
        in_specs=[
            pl.BlockSpec((tm, d), lambda i: (i, 0)),
            pl.BlockSpec((tm, d // 2), lambda i: (i, 0)),
            pl.BlockSpec((tm, d // 2), lambda i: (i + second, 0)),
            pl.BlockSpec((tm, ROUTER_LANES), lambda i: (i, 0)),
            pl.BlockSpec((None, 1, d), lambda i: (i // tiles_per_batch, 0, 0)),
        ],
        out_specs=pl.BlockSpec((tm, d), lambda i: (i, 0)),
        out_shape=jax.ShapeDtypeStruct((t, d), jnp.float32),
        compiler_params=_params(("arbitrary",)),
        name="moe_combine",
    )(x2d, y_pairs, y_pairs, info, gt)


def _slot_plan(info_t, cnt, t):
    counts = cnt[:, 0].astype(jnp.int32)
    padded = (counts + MOE_BLOCK - 1) // MOE_BLOCK * MOE_BLOCK
    pad_ends = jnp.cumsum(padded)
    pad_starts = pad_ends - padded
    nb = -(-(2 * t) // MOE_BLOCK) + N_EXPERTS
    n_slots = nb * MOE_BLOCK
    it = info_t.astype(jnp.int32)
    onehot_start = lambda e: jnp.sum(
        jnp.where(e[None, :] == jnp.arange(N_EXPERTS, dtype=jnp.int32)[:, None],
                  pad_starts[:, None], 0), axis=0)
    dest1 = (onehot_start(it[0]) + it[2]).reshape(1, t)
    dest2 = (onehot_start(it[1]) + it[3]).reshape(1, t)
    lane = jnp.arange(MOE_BLOCK, dtype=jnp.int32)[None, :]
    n_padding = (padded - counts)[:, None]
    wrapped = (pad_starts + counts)[:, None] + lane % jnp.maximum(n_padding, 1)
    pad_idx = jnp.where(n_padding > 0, wrapped, n_slots + lane % SC_WINDOW).reshape(-1)
    n_real = pad_ends[-1] // MOE_BLOCK
    n_used = -(-n_real // EXPERT_BLOCKS_PER_STEP) * EXPERT_BLOCKS_PER_STEP
    tail = jnp.arange((EXPERT_BLOCKS_PER_STEP - 1) * MOE_BLOCK, dtype=jnp.int32)
    tail_idx = jnp.where(tail < (n_used - n_real) * MOE_BLOCK, pad_ends[-1] + tail,
                         n_slots + tail % SC_WINDOW)
    pad_idx = jnp.concatenate([pad_idx, tail_idx]).reshape(1, -1)
    experts = jnp.arange(N_EXPERTS, dtype=jnp.int32)
    blk = jnp.arange(nb, dtype=jnp.int32)
    blk_exp = jnp.minimum(
        jnp.sum((pad_ends[None, :] <= (blk * MOE_BLOCK)[:, None]).astype(jnp.int32), axis=1),
        N_EXPERTS - 1)
    blk_exp = jnp.where(blk >= n_real, jnp.max(jnp.where(padded > 0, experts, 0)), blk_exp)
    prev_exp = jnp.concatenate([jnp.full((1,), -1, jnp.int32), blk_exp[:-1]])
    is_first = (blk < n_used) & (blk_exp != prev_exp)
    blk_slot = (jnp.cumsum(is_first.astype(jnp.int32)) + EXPERT_STAGES - 1) % EXPERT_STAGES
    later = (experts[None, :] > experts[:, None]) & (padded[None, :] > 0)
    nxt_of = jnp.min(jnp.where(later, experts[None, :], N_EXPERTS), axis=1)
    nxt_of = jnp.where(nxt_of == N_EXPERTS, -1, nxt_of)
    lookup = lambda table, e: jnp.sum(
        jnp.where(e[:, None] == experts[None, :], table[None, :], 0), axis=1)
    nxt2_of = jnp.where(nxt_of >= 0, lookup(nxt_of, jnp.maximum(nxt_of, 0)), -1)
    blk_nxt = lookup(nxt2_of, blk_exp)
    second = lookup(nxt_of, blk_exp[:1])
    blk_meta = jnp.concatenate([blk_exp, blk_slot, blk_nxt, n_used[None], second]).astype(jnp.int32)
    return dest1, dest2, pad_idx, blk_meta, nb, n_slots


def _mixer_out_and_moe(a, w_out, x2d, gt1, g, sc, sh, w_group, b_group, w_router, b_router,
                       w1, w3, w2, layer, seq):
    t, d = x2d.shape
    w_cat = jnp.zeros((ROUTER_LANES, d), jnp.float32)
    w_cat = w_cat.at[:N_EXPERTS].set(w_router.T).at[N_EXPERTS:N_EXPERTS + N_GROUPS].set(w_group.T)
    b_cat = jnp.zeros((ROUTER_LANES, 1), jnp.float32)
    b_cat = b_cat.at[:N_EXPERTS, 0].set(b_router).at[N_EXPERTS:N_EXPERTS + N_GROUPS, 0].set(b_group)
    w_hi, w_lo = _split_hi_lo(w_cat)
    x_new, hn, info, info_t, cnt = _outproj_router(a, w_out, x2d, gt1, g, sc, sh, w_hi, w_lo,
                                                   b_cat, seq)
    dest1, dest2, pad_idx, blk_meta, nb, n_slots = _slot_plan(info_t, cnt, t)
    x_slots = _sc_dispatch(hn, dest1, dest2, pad_idx, n_slots)
    y_slots = _experts(x_slots, blk_meta, w1, w3, w2, layer, nb)
    y_pairs = _sc_gather(y_slots, jnp.concatenate([dest1, dest2], axis=1))
    return x_new, y_pairs, info


def kernel(x, c, w_ada, b_ada, norm1_g, norm2_g, ml_w_in, ml_b_gate, ml_g_out, ml_w_out,
           sw_w_in, sw_g_q, sw_g_k, sw_sinks, sw_w_out, moe_w_group, moe_b_group,
           moe_w_router, moe_b_router, moe_w1, moe_w3, moe_w2):
    bsz, seq, d = x.shape
    depth = w_ada.shape[0]
    bf = jnp.bfloat16
    mod = _ada_mod(c, w_ada, b_ada)
    x2d = x.reshape(bsz * seq, d)
    pending = None
    for layer in range(depth):
        sh1, sc1, gt1, sh2, sc2, gt2 = [
            mod[layer, :, i * d:(i + 1) * d].reshape(bsz, 1, d) for i in range(6)]
        j = layer // 2
        if layer % 2 == 0:
            w = ml_w_in[j]
            q_w, k_w = w[:, :ML_QK], w[:, ML_QK:2 * ML_QK]
            v_w = w[:, 2 * ML_QK:2 * ML_QK + ML_V]
            o_w = w[:, 2 * ML_QK + ML_V:2 * ML_QK + 2 * ML_V]
            g_w = w[:, 2 * ML_QK + 2 * ML_V:]
            w_main = jnp.concatenate([v_w, o_w, q_w], axis=1).astype(bf)
            wg_t = jnp.zeros((ML_GATE_ROWS, d), jnp.float32).at[:2 * ML_HEADS].set(g_w.T)
            wg_hi, wg_lo = _split_hi_lo(wg_t)
            wk_t = k_w.T.astype(bf)
            w_out = ml_w_out[j].astype(bf)
            outs = _inproj(x2d, norm1_g[layer], sc1, sh1, w_main, seq,
                           ml_extra=(wk_t, wg_hi, wg_lo),
                           q_cols=(2 * ML_V, 2 * ML_V + ML_QK), q_scale=ML_DK ** -0.5,
                           gate_cols=(ML_V, 2 * ML_V), pending=pending)
            if pending is not None:
                x2d, outs = outs[0], outs[1:]
            main, k_t, g_t = outs
            a = _mlstm_core(main, k_t, g_t, ml_b_gate[j], ml_g_out[j], bsz, seq)
        else:
            w = sw_w_in[j]
            dq = SW_Q_HEADS * SW_DH
            dkv = SW_KV_HEADS * SW_DH
            dup = lambda m: jnp.concatenate(
                [m.reshape(d, SW_KV_HEADS, 1, SW_DH)] * 2, axis=2).reshape(d, 2 * dkv)
            w_main = jnp.concatenate(
                [w[:, :dq], dup(w[:, dq:dq + dkv]), dup(w[:, dq + dkv:])], axis=1).astype(bf)
            w_out = sw_w_out[j].astype(bf)
            gq = jnp.concatenate([sw_g_q[j], sw_g_q[j]]).reshape(1, LANES) * (SW_DH ** -0.5 * LOG2E)
            gk = jnp.concatenate([sw_g_k[j], sw_g_k[j]]).reshape(1, LANES)
            outs = _inproj(x2d, norm1_g[layer], sc1, sh1, w_main, seq, pending=pending,
                           qk_norm=(dq, dq + 2 * dkv, gq, gk))
            if pending is not None:
                x2d, outs = outs[0], outs[1:]
            a = _swa_core(outs[0], sw_sinks[j], bsz, seq)
        x2d, y_pairs, info = _mixer_out_and_moe(
            a, w_out, x2d, gt1, norm2_g[layer], sc2, sh2, moe_w_group[layer], moe_b_group[layer],
            moe_w_router[layer], moe_b_router[layer], moe_w1, moe_w3, moe_w2, layer, seq)
        pending = (y_pairs, info, gt2)
    y_pairs, info, gt2 = pending
    return _combine(x2d, y_pairs, info, gt2, seq).reshape(bsz, seq, d)
```

```python
import functools

import jax
import jax.numpy as jnp
from jax import lax
from jax.experimental import pallas as pl
from jax.experimental.pallas import tpu as pltpu
from jax.experimental.pallas import tpu_sc as plsc

EPS = 1e-6
GATE_CAP = 15.0
LOG2E = 1.4426950408889634

ML_HEADS = 4
ML_DK = 128
ML_DV = 256
ML_QK = ML_HEADS * ML_DK
ML_V = ML_HEADS * ML_DV
ML_GATE_ROWS = 16

SW_Q_HEADS = 16
SW_KV_HEADS = 4
SW_GROUP = SW_Q_HEADS // SW_KV_HEADS
SW_DH = 64
SW_WINDOW = 128
LANES = 128

N_GROUPS = 8
EXPERTS_PER_GROUP = 8
N_EXPERTS = N_GROUPS * EXPERTS_PER_GROUP
MOE_BLOCK = 256
ROUTER_LANES = 128
SC_WINDOW = 128
SC_ROWS = 64
EXPERT_BLOCKS_PER_STEP = 4
EXPERT_STAGES = 3

SUBLANES = 8
TOKEN_TILE = 512
ROUTER_TILE = 1024
COMBINE_TILE = 1024
PROJ_COL_CHUNK = 1024
ML_CHUNK = 256
ADA_COL_TILE = 768
V7X_VMEM_BYTES = 64 * 1024 * 1024
VMEM_LIMIT = V7X_VMEM_BYTES - 8 * 1024 * 1024

_NT = (((1,), (1,)), ((), ()))


def _bdot(a, b):
    return jnp.dot(a, b, preferred_element_type=jnp.float32)


def _bdot_nt(a, b):
    return lax.dot_general(a, b, _NT, preferred_element_type=jnp.float32)


def _split_hi_lo(a):
    hi = a.astype(jnp.bfloat16)
    lo = (a - hi.astype(jnp.float32)).astype(jnp.bfloat16)
    return hi, lo


def _params(sem):
    return pltpu.CompilerParams(dimension_semantics=sem, vmem_limit_bytes=VMEM_LIMIT)


def _ada_kernel(c_ref, w_ref, b_ref, o_ref):
    c = c_ref[...]
    cond = c * jax.nn.sigmoid(c)
    c_hi, c_lo = _split_hi_lo(cond)
    w_hi, w_lo = _split_hi_lo(w_ref[...])
    acc = _bdot(c_hi, w_hi) + (_bdot(c_lo, w_hi) + _bdot(c_hi, w_lo))
    o_ref[...] = acc + b_ref[...]


def _ada_mod(c, w_ada, b_ada):
    depth, d, n = w_ada.shape
    bsz = c.shape[0]
    rows = SUBLANES
    tn = ADA_COL_TILE
    c_pad = jnp.zeros((rows, d), jnp.float32).at[:bsz].set(c)
    out = pl.pallas_call(
        _ada_kernel,
        grid=(depth, n // tn),
        in_specs=[
            pl.BlockSpec((rows, d), lambda l, j: (0, 0)),
            pl.BlockSpec((None, d, tn), lambda l, j: (l, 0, j)),
            pl.BlockSpec((None, 1, tn), lambda l, j: (l, 0, j)),
        ],
        out_specs=pl.BlockSpec((None, rows, tn), lambda l, j: (l, 0, j)),
        out_shape=jax.ShapeDtypeStruct((depth, rows, n), jnp.float32),
        compiler_params=_params(("arbitrary", "arbitrary")),
        name="ada_mod",
    )(c_pad, w_ada, b_ada.reshape(depth, 1, n))
    return out[:, :bsz]


def _modulated_norm(x, g, sc, sh):
    y = x * lax.rsqrt(jnp.mean(x * x, axis=-1, keepdims=True) + EPS)
    return y * (g * (1.0 + sc)) + sh


def _moe_combined(x, y1_ref, y2_ref, info_ref, gt_ref):
    info = info_ref[...]
    g1 = info[:, 4:5]
    g2 = info[:, 5:6]
    y1_hi, y1_lo = _unpack_bf16_pairs(y1_ref[...])
    y2_hi, y2_lo = _unpack_bf16_pairs(y2_ref[...])
    y = jnp.concatenate([g1 * y1_hi + g2 * y2_hi, g1 * y1_lo + g2 * y2_lo], axis=-1)
    return x + gt_ref[...] * y


def _qk_head_norm(acc, c0, qk_norm, gq, gk):
    q_hi, k_hi = qk_norm
    low = lax.broadcasted_iota(jnp.int32, (acc.shape[0], LANES), 1) < SW_DH
    slabs = []
    for j in range(acc.shape[1] // LANES):
        slab = acc[:, j * LANES:(j + 1) * LANES]
        if c0 + j * LANES < q_hi:
            sq = slab * slab
            ss_lo = jnp.sum(jnp.where(low, sq, 0.0), axis=-1, keepdims=True)
            ss_hi = jnp.sum(jnp.where(low, 0.0, sq), axis=-1, keepdims=True)
            rs = jnp.where(low, lax.rsqrt(ss_lo / SW_DH + EPS), lax.rsqrt(ss_hi / SW_DH + EPS))
            slab = slab * rs * gq
        elif c0 + j * LANES < k_hi:
            slab = slab * lax.rsqrt(jnp.mean(slab * slab, axis=-1, keepdims=True) + EPS) * gk
        slabs.append(slab)
    return jnp.concatenate(slabs, axis=-1)


def _inproj_kernel(*refs, n_main, chunk, q_cols, q_scale, gate_cols, qk_norm, with_ml,
                   with_combine):
    refs = list(refs)
    n_in = (5 + (4 if with_combine else 0) + (2 if with_ml else 0)
            + (2 if qk_norm is not None else 0))
    n_scratch = 4 if with_ml else 2
    ins, outs, scratch = refs[:n_in], refs[n_in:-n_scratch], refs[-n_scratch:]
    x_ref = ins.pop(0)
    if with_combine:
        y1_ref, y2_ref, info_ref, gtp_ref = ins[:4]
        ins = ins[4:]
        xo_ref = outs.pop(0)
    g_ref, sc_ref, sh_ref, w_ref = ins[:4]
    o_ref = outs[0]
    if with_ml:
        wk_ref, wgh_ref = ins[4:]
        kt_ref, gt_ref = outs[1:]
    if qk_norm is not None:
        gq_ref, gk_ref = ins[4:]

    def normalise(hb_dst, lo_dst):
        x = x_ref[...]
        if with_combine:
            x = _moe_combined(x, y1_ref, y2_ref, info_ref, gtp_ref)
            xo_ref[...] = x
        hn = _modulated_norm(x, g_ref[...], sc_ref[...], sh_ref[...])
        hb = hn.astype(jnp.bfloat16)
        hb_dst[...] = hb
        if with_ml:
            lo_dst[...] = (hn - hb.astype(jnp.float32)).astype(jnp.bfloat16)

    def project(hb_src, lo_src):
        hb = hb_src[...]
        for c0 in range(0, n_main, chunk):
            c1 = min(c0 + chunk, n_main)
            acc = _bdot(hb, w_ref[:, c0:c1])
            if q_cols is not None and q_cols[0] <= c0 < q_cols[1]:
                acc = acc * q_scale
            if gate_cols is not None and gate_cols[0] <= c0 < gate_cols[1]:
                acc = jax.nn.sigmoid(acc)
            if qk_norm is not None and c0 < qk_norm[1]:
                acc = _qk_head_norm(acc, c0, qk_norm, gq_ref[...], gk_ref[...])
            o_ref[:, c0:c1] = acc.astype(o_ref.dtype)
        if with_ml:
            nk = kt_ref.shape[0]
            stacked = _bdot_nt(wk_ref[...], hb)
            kt_ref[...] = stacked[:nk].astype(kt_ref.dtype)
            gt_ref[...] = (stacked[nk:nk + ML_GATE_ROWS]
                           + (_bdot_nt(wgh_ref[...], lo_src[...]) + stacked[nk + ML_GATE_ROWS:]))

    hb_a, hb_b = scratch[:2]
    lo_a, lo_b = scratch[2:] if with_ml else (None, None)
    s = pl.program_id(0)

    @pl.when(s == 0)
    def _():
        hb_b[...] = jnp.zeros_like(hb_b)
        if with_ml:
            lo_b[...] = jnp.zeros_like(lo_b)

    @pl.when(s % 2 == 0)
    def _():
        normalise(hb_a, lo_a)
        project(hb_b, lo_b)

    @pl.when(s % 2 == 1)
    def _():
        normalise(hb_b, lo_b)
        project(hb_a, lo_a)


def _inproj(x2d, g, sc, sh, w_main, seq, *, ml_extra=None, q_cols=None, q_scale=1.0,
            gate_cols=None, qk_norm=None, pending=None):
    t, d = x2d.shape
    tm = TOKEN_TILE if ml_extra is not None else min(2 * TOKEN_TILE, seq)
    n_main = w_main.shape[1]
    tiles_per_batch = seq // tm
    n_tiles = t // tm
    resident = pl.Buffered(1)
    norm_tile = lambda s: jnp.minimum(s, n_tiles - 1)
    proj_tile = lambda s: jnp.maximum(s - 1, 0)
    row = lambda s: (norm_tile(s), 0)
    per_batch = lambda s: (norm_tile(s) // tiles_per_batch, 0, 0)
    const = lambda s: (0, 0)
    in_specs = [pl.BlockSpec((tm, d), row)]
    args = [x2d]
    out_specs, out_shape = [], []
    if pending is not None:
        y_pairs, info, gt_prev = pending
        in_specs += [pl.BlockSpec((tm, d // 2), row),
                     pl.BlockSpec((tm, d // 2), lambda s: (norm_tile(s) + n_tiles, 0)),
                     pl.BlockSpec((tm, ROUTER_LANES), row),
                     pl.BlockSpec((None, 1, d), per_batch)]
        args += [y_pairs, y_pairs, info, gt_prev]
        out_specs += [pl.BlockSpec((tm, d), row)]
        out_shape += [jax.ShapeDtypeStruct((t, d), jnp.float32)]
    in_specs += [
        pl.BlockSpec((1, d), const),
        pl.BlockSpec((None, 1, d), per_batch),
        pl.BlockSpec((None, 1, d), per_batch),
        pl.BlockSpec((d, n_main), const, pipeline_mode=resident),
    ]
    args += [g.reshape(1, d), sc, sh, w_main]
    out_specs += [pl.BlockSpec((tm, n_main), lambda s: (proj_tile(s), 0))]
    out_shape += [jax.ShapeDtypeStruct((t, n_main), jnp.bfloat16)]
    scratch = [pltpu.VMEM((tm, d), jnp.bfloat16), pltpu.VMEM((tm, d), jnp.bfloat16)]
    if ml_extra is not None:
        wk_t, wg_hi, wg_lo = ml_extra
        nk = wk_t.shape[0]
        stacked = jnp.concatenate([wk_t, wg_hi, wg_lo], axis=0)
        in_specs += [pl.BlockSpec(stacked.shape, const, pipeline_mode=resident),
                     pl.BlockSpec(wg_hi.shape, const, pipeline_mode=resident)]
        args += [stacked, wg_hi]
        out_specs += [pl.BlockSpec((nk, tm), lambda s: (0, proj_tile(s))),
                      pl.BlockSpec((ML_GATE_ROWS, tm), lambda s: (0, proj_tile(s)))]
        out_shape += [jax.ShapeDtypeStruct((nk, t), jnp.bfloat16),
                      jax.ShapeDtypeStruct((ML_GATE_ROWS, t), jnp.float32)]
        scratch += [pltpu.VMEM((tm, d), jnp.bfloat16), pltpu.VMEM((tm, d), jnp.bfloat16)]
    if qk_norm is not None:
        in_specs += [pl.BlockSpec((1, LANES), const), pl.BlockSpec((1, LANES), const)]
        args += [qk_norm[2], qk_norm[3]]
    kern = functools.partial(_inproj_kernel, n_main=n_main, chunk=PROJ_COL_CHUNK, q_cols=q_cols,
                             q_scale=q_scale, gate_cols=gate_cols,
                             qk_norm=None if qk_norm is None else qk_norm[:2],
                             with_ml=ml_extra is not None, with_combine=pending is not None)
    return pl.pallas_call(
        kern,
        grid=(n_tiles + 1,),
        in_specs=in_specs,
        out_specs=out_specs,
        out_shape=out_shape,
        scratch_shapes=scratch,
        compiler_params=_params(("arbitrary",)),
        name="inproj_ml" if ml_extra is not None else "inproj_sw",
    )(*args)


def _mlstm_gate_terms(graw, bias, upper):
    H = ML_HEADS
    L = graw.shape[1]
    z = graw + bias
    gates = GATE_CAP * jnp.tanh(z / GATE_CAP)
    log_f = jnp.minimum(gates, 0.0) - jnp.log1p(jnp.exp(-jnp.abs(gates)))
    row = lax.broadcasted_iota(jnp.int32, (ML_GATE_ROWS, L), 0)
    lane = lax.broadcasted_iota(jnp.int32, (ML_GATE_ROWS, L), 1)
    is_i = row < H
    slab = jnp.where(is_i, gates, log_f)
    a1 = slab.astype(jnp.bfloat16)
    r1 = slab - a1.astype(jnp.float32)
    a2 = r1.astype(jnp.bfloat16)
    a3 = (r1 - a2.astype(jnp.float32)).astype(jnp.bfloat16)
    cum = _bdot(a1, upper) + (_bdot(a2, upper) + _bdot(a3, upper))
    ib = jnp.where(is_i, gates, cum)
    b = pltpu.roll(ib, ML_GATE_ROWS - H, 0)
    u = ib - b
    cm = u
    shift = 1
    while shift < L:
        cm = jnp.maximum(cm, jnp.where(lane >= shift, pltpu.roll(cm, shift, 1), -jnp.inf))
        shift *= 2
    return b, u, cm


def _mlstm_kernel(v_ref, o_ref, q_ref, kt_ref, gt_ref, gtn_ref, bg_ref, gout_ref, out_ref,
                  c_ref, m_ref, b_ref, u_ref, cm_ref, *, chunk):
    L = chunk
    H, dk, dv = ML_HEADS, ML_DK, ML_DV
    r_idx = lax.broadcasted_iota(jnp.int32, (L, L), 0)
    c_idx = lax.broadcasted_iota(jnp.int32, (L, L), 1)
    upper = jnp.where(r_idx <= c_idx, 1.0, 0.0).astype(jnp.bfloat16)
    causal = r_idx >= c_idx
    row = lax.broadcasted_iota(jnp.int32, (ML_GATE_ROWS, L), 0)
    ones_col = jnp.where(lax.broadcasted_iota(jnp.int32, (L, LANES), 1) == 0, 1.0, 0.0
                         ).astype(jnp.bfloat16)

    @pl.when(pl.program_id(1) == 0)
    def _():
        c_ref[...] = jnp.zeros_like(c_ref)
        m_ref[...] = jnp.zeros_like(m_ref)
        b0, u0, cm0 = _mlstm_gate_terms(gt_ref[...], bg_ref[...], upper)
        b_ref[...] = b0
        u_ref[...] = u0
        cm_ref[...] = cm0

    b16 = b_ref[...]
    u16 = u_ref[...]
    cm16 = cm_ref[...]
    b_n, u_n, cm_n = _mlstm_gate_terms(gtn_ref[...], bg_ref[...], upper)
    b_ref[...] = b_n
    u_ref[...] = u_n
    cm_ref[...] = cm_n

    m_prev = m_ref[:, 0:1]
    z16 = jnp.maximum(m_prev, cm16)
    w_inter16 = jnp.exp(m_prev - z16)
    e_negm16 = jnp.exp(-(b16 + z16))
    z_last = z16[:, L - 1:L]
    w_state16 = jnp.exp(u16 - z_last)
    decay16 = jnp.exp(m_prev - z_last)
    m_ref[...] = jnp.broadcast_to(b16[:, L - 1:L] + z_last, m_ref.shape)
    stacked = jnp.where(row < H, z16,
                        jnp.where(row < 2 * H, pltpu.roll(w_inter16, H, 0),
                                  pltpu.roll(e_negm16, 2 * H, 0)))
    cols = jnp.concatenate(
        [stacked, jnp.zeros((LANES - ML_GATE_ROWS, L), jnp.float32)], axis=0).T

    for h in range(H):
        u_r = u16[h:h + 1, :]
        z_c = cols[:, h:h + 1]
        w_inter = cols[:, H + h:H + h + 1]
        e_negm = cols[:, 2 * H + h:2 * H + h + 1]
        c_ext = c_ref[h]
        q = q_ref[:, h * dk:(h + 1) * dk]
        kt = kt_ref[h * dk:(h + 1) * dk, :]
        v_ext = jnp.concatenate([v_ref[:, h * dv:(h + 1) * dv], ones_col], axis=-1)

        w_intra = jnp.exp(jnp.where(causal, u_r - z_c, -jnp.inf))
        s = (_bdot(q, kt) * w_intra).astype(jnp.bfloat16)
        nd = w_inter * _bdot(q, c_ext.astype(jnp.bfloat16)) + _bdot(s, v_ext)
        den = nd[:, dv:dv + 1]
        hb = nd[:, :dv] * (1.0 / jnp.maximum(jnp.abs(den), e_negm))

        kw = (kt.astype(jnp.float32) * w_state16[h:h + 1, :]).astype(jnp.bfloat16)
        c_ref[h] = decay16[h:h + 1, :] * c_ext + _bdot(kw, v_ext)

        y = hb * lax.rsqrt(jnp.mean(hb * hb, axis=-1, keepdims=True) + EPS)
        y = y * gout_ref[:, h * dv:(h + 1) * dv]
        og = o_ref[:, h * dv:(h + 1) * dv].astype(jnp.float32)
        out_ref[:, h * dv:(h + 1) * dv] = (y * og).astype(out_ref.dtype)


def _mlstm_core(main, k_t, g_t, b_gate, g_out, bsz, seq):
    chunk = ML_CHUNK
    t = main.shape[0]
    nc = seq // chunk
    blk = lambda b, c: b * nc + c
    bg = jnp.zeros((ML_GATE_ROWS, 1), jnp.float32).at[:2 * ML_HEADS, 0].set(b_gate)
    return pl.pallas_call(
        functools.partial(_mlstm_kernel, chunk=chunk),
        grid=(bsz, nc),
        in_specs=[
            pl.BlockSpec((chunk, ML_V), lambda b, c: (blk(b, c), 0)),
            pl.BlockSpec((chunk, ML_V), lambda b, c: (blk(b, c), 1)),
            pl.BlockSpec((chunk, ML_QK), lambda b, c: (blk(b, c), 4)),
            pl.BlockSpec((ML_QK, chunk), lambda b, c: (0, blk(b, c))),
            pl.BlockSpec((ML_GATE_ROWS, chunk), lambda b, c: (0, blk(b, c))),
            pl.BlockSpec((ML_GATE_ROWS, chunk), lambda b, c: (0, blk(b, jnp.minimum(c + 1, nc - 1)))),
            pl.BlockSpec((ML_GATE_ROWS, 1), lambda b, c: (0, 0)),
            pl.BlockSpec((1, ML_V), lambda b, c: (0, 0)),
        ],
        out_specs=pl.BlockSpec((chunk, ML_V), lambda b, c: (blk(b, c), 0)),
        out_shape=jax.ShapeDtypeStruct((t, ML_V), jnp.bfloat16),
        scratch_shapes=[
            pltpu.VMEM((ML_HEADS, ML_DK, ML_DV + LANES), jnp.float32),
            pltpu.VMEM((ML_GATE_ROWS, LANES), jnp.float32),
            pltpu.VMEM((ML_GATE_ROWS, chunk), jnp.float32),
            pltpu.VMEM((ML_GATE_ROWS, chunk), jnp.float32),
            pltpu.VMEM((ML_GATE_ROWS, chunk), jnp.float32),
        ],
        compiler_params=_params(("arbitrary", "arbitrary")),
        name="mlstm_core",
    )(main, main, main, k_t, g_t, g_t, bg, g_out.reshape(1, ML_V))


def _swa_kernel(q_ref, kc_ref, kp_ref, vc_ref, vp_ref, bias_ref, o_ref):
    W = SW_WINDOW
    lane = lax.broadcasted_iota(jnp.int32, (W, LANES), 1)
    low = lane < SW_DH
    not_first_row = lax.broadcasted_iota(jnp.int32, (W, LANES), 0) > 0
    variant = jnp.minimum(pl.program_id(1), 1)
    for g in range(SW_KV_HEADS):
        sl = slice(g * LANES, (g + 1) * LANES)
        k_prev = jnp.where(not_first_row, kp_ref[:, sl].astype(jnp.float32), 0.0)
        v_prev = jnp.where(not_first_row, vp_ref[:, sl].astype(jnp.float32), 0.0)
        kn = jnp.concatenate([k_prev.astype(jnp.bfloat16), kc_ref[:, sl]], axis=0)
        v2 = jnp.concatenate([v_prev.astype(jnp.bfloat16), vc_ref[:, sl]], axis=0)
        parts = []
        for p in range(2):
            c0 = g * SW_GROUP * SW_DH + p * LANES
            qp = q_ref[:, c0:c0 + LANES].astype(jnp.float32)
            parts.append(jnp.where(low, qp, 0.0).astype(jnp.bfloat16))
            parts.append(jnp.where(low, 0.0, qp).astype(jnp.bfloat16))
        q4 = jnp.concatenate(parts, axis=0)
        scores = _bdot_nt(q4, kn) + bias_ref[variant, g]
        m = jnp.max(scores, axis=-1, keepdims=True)
        pexp = jnp.exp2(scores - m)
        denom = jnp.sum(pexp, axis=-1, keepdims=True)
        o4 = _bdot(pexp.astype(jnp.bfloat16), v2) * (1.0 / denom)
        for p in range(2):
            oa = o4[(2 * p) * W:(2 * p + 1) * W]
            ob = o4[(2 * p + 1) * W:(2 * p + 2) * W]
            c0 = g * SW_GROUP * SW_DH + p * LANES
            o_ref[:, c0:c0 + LANES] = jnp.where(low, oa, ob).astype(o_ref.dtype)


def _swa_core(proj, sinks, bsz, seq):
    t = proj.shape[0]
    W = SW_WINDOW
    nb = seq // W
    dq = SW_Q_HEADS * SW_DH
    kv_w = SW_KV_HEADS * LANES
    k_blk = dq // kv_w
    v_blk = k_blk + 1
    cur = lambda b, n: b * nb + n
    prev = lambda b, n: b * nb + jnp.maximum(n - 1, 0)
    qi = (jnp.arange(SW_GROUP * W) % W)[:, None]
    ki = jnp.arange(2 * W)[None, :]
    rel = qi + W - ki
    in_win = (rel >= 0) & (rel < W)
    window = jnp.stack([jnp.where(in_win & (ki >= W), 0.0, -jnp.inf),
                        jnp.where(in_win, 0.0, -jnp.inf)]).astype(jnp.float32)
    sink_rows = jnp.repeat(sinks.astype(jnp.float32).reshape(SW_KV_HEADS, SW_GROUP) * LOG2E, W, axis=1)
    bias = jnp.where(ki[None, None] == 0, sink_rows[None, :, :, None], window[:, None])
    return pl.pallas_call(
        _swa_kernel,
        grid=(bsz, nb),
        in_specs=[
            pl.BlockSpec((W, dq), lambda b, n: (cur(b, n), 0)),
            pl.BlockSpec((W, kv_w), lambda b, n: (cur(b, n), k_blk)),
            pl.BlockSpec((W, kv_w), lambda b, n: (prev(b, n), k_blk)),
            pl.BlockSpec((W, kv_w), lambda b, n: (cur(b, n), v_blk)),
            pl.BlockSpec((W, kv_w), lambda b, n: (prev(b, n), v_blk)),
            pl.BlockSpec(bias.shape, lambda b, n: (0, 0, 0, 0)),
        ],
        out_specs=pl.BlockSpec((W, dq), lambda b, n: (cur(b, n), 0)),
        out_shape=jax.ShapeDtypeStruct((t, dq), jnp.bfloat16),
        compiler_params=_params(("arbitrary", "arbitrary")),
        name="swa_core",
    )(proj, proj, proj, proj, proj, bias)


def _pack_rounded_pairs(r):
    k = r.shape[1] // 2
    hi = lax.bitcast_convert_type(r[:, :k], jnp.uint32)
    lo = lax.bitcast_convert_type(r[:, k:], jnp.uint32)
    return hi | (lo >> 16)


def _pack_bf16_pairs(a):
    return _pack_rounded_pairs(a.astype(jnp.bfloat16).astype(jnp.float32))


def _unpack_bf16_pairs(u):
    hi = lax.bitcast_convert_type(u & jnp.uint32(0xFFFF0000), jnp.float32)
    lo = lax.bitcast_convert_type(u << 16, jnp.float32)
    return hi, lo


def _route_tile(x_new, g_ref, sc_ref, sh_ref, wh_ref, wl_ref, b_ref,
                hn_ref, info_ref, infot_ref, cnt_ref, carry_ref, earlier_ref, tm):
    hn = _modulated_norm(x_new, g_ref[...], sc_ref[...], sh_ref[...])
    h_hi = hn.astype(jnp.bfloat16)
    hi_f32 = h_hi.astype(jnp.float32)
    h_lo = (hn - hi_f32).astype(jnp.bfloat16)
    hn_ref[...] = _pack_rounded_pairs(hi_f32)
    wide = _bdot_nt(wl_ref[...], h_hi)
    logits = (wide[:ROUTER_LANES] + (_bdot_nt(wh_ref[...], h_lo) + wide[ROUTER_LANES:])
              + b_ref[...])
    E8 = EXPERTS_PER_GROUP
    sub = lax.broadcasted_iota(jnp.int32, (E8, tm), 0).astype(jnp.float32)
    big = float(ROUTER_LANES)
    neg = -jnp.inf

    gl = logits[N_EXPERTS:N_EXPERTS + N_GROUPS]
    gmax = jnp.max(gl, axis=0, keepdims=True)
    gsel = jnp.min(jnp.where(gl == gmax, sub, big), axis=0, keepdims=True)
    p_grp = 1.0 / jnp.sum(jnp.exp(gl - gmax), axis=0, keepdims=True)

    el = logits[0:E8]
    for grp in range(1, N_GROUPS):
        el = jnp.where(gsel == grp, logits[grp * E8:(grp + 1) * E8], el)
    v1 = jnp.max(el, axis=0, keepdims=True)
    j1 = jnp.min(jnp.where(el == v1, sub, big), axis=0, keepdims=True)
    el2 = jnp.where(sub == j1, neg, el)
    v2 = jnp.max(el2, axis=0, keepdims=True)
    j2 = jnp.min(jnp.where(el2 == v2, sub, big), axis=0, keepdims=True)
    i1 = gsel * E8 + j1
    i2 = gsel * E8 + j2
    e21 = jnp.exp(v2 - v1)
    gate1 = p_grp / (1.0 + e21)
    gate2 = p_grp * e21 / (1.0 + e21)

    erow = lax.broadcasted_iota(jnp.int32, (N_EXPERTS, tm), 0).astype(jnp.float32)
    hit1 = erow == i1
    hit2 = erow == i2
    onehot = jnp.where(hit1 | hit2, 1.0, 0.0)
    carry = carry_ref[:, 0:1]
    before = _bdot(onehot.astype(jnp.bfloat16), earlier_ref[...]) + carry
    rank1 = jnp.sum(jnp.where(hit1, before, 0.0), axis=0, keepdims=True)
    rank2 = jnp.sum(jnp.where(hit2, before, 0.0), axis=0, keepdims=True)
    total = carry + jnp.sum(onehot, axis=1, keepdims=True)
    carry_ref[...] = jnp.broadcast_to(total, carry_ref.shape)
    cnt_ref[...] = jnp.broadcast_to(total, cnt_ref.shape)

    info_t = jnp.where(sub == 0, i1, 0.0)
    info_t = jnp.where(sub == 1, i2, info_t)
    info_t = jnp.where(sub == 2, rank1, info_t)
    info_t = jnp.where(sub == 3, rank2, info_t)
    info_t = jnp.where(sub == 4, gate1, info_t)
    info_t = jnp.where(sub == 5, gate2, info_t)
    infot_ref[...] = info_t
    info_ref[...] = jnp.concatenate(
        [info_t, jnp.zeros((ROUTER_LANES - E8, tm), jnp.float32)], axis=0).T


def _router_kernel(a_ref, wo_ref, x_ref, gt_ref, g_ref, sc_ref, sh_ref, wh_ref, wl_ref, b_ref,
                   xo_ref, hn_ref, info_ref, infot_ref, cnt_ref, carry_ref, earlier_ref, *, tm):
    @pl.when(pl.program_id(0) == 0)
    def _():
        carry_ref[...] = jnp.zeros_like(carry_ref)
        r_idx = lax.broadcasted_iota(jnp.int32, (tm, tm), 0)
        c_idx = lax.broadcasted_iota(jnp.int32, (tm, tm), 1)
        earlier_ref[...] = jnp.where(r_idx < c_idx, 1.0, 0.0).astype(jnp.bfloat16)

    x_new = x_ref[...] + gt_ref[...] * _bdot(a_ref[...], wo_ref[...])
    xo_ref[...] = x_new
    _route_tile(x_new, g_ref, sc_ref, sh_ref, wh_ref, wl_ref, b_ref,
                hn_ref, info_ref, infot_ref, cnt_ref, carry_ref, earlier_ref, tm)


def _outproj_router(a, w_out, x2d, gt, g, sc, sh, w_hi, w_lo, bias, seq):
    t, d = x2d.shape
    tm = min(ROUTER_TILE, seq)
    tiles_per_batch = seq // tm
    per_batch = lambda i: (i // tiles_per_batch, 0, 0)
    const = lambda i: (0, 0)
    return pl.pallas_call(
        functools.partial(_router_kernel, tm=tm),
        grid=(t // tm,),
        in_specs=[
            pl.BlockSpec((tm, a.shape[1]), lambda i: (i, 0)),
            pl.BlockSpec(w_out.shape, const),
            pl.BlockSpec((tm, d), lambda i: (i, 0)),
            pl.BlockSpec((None, 1, d), per_batch),
            pl.BlockSpec((1, d), const),
            pl.BlockSpec((None, 1, d), per_batch),
            pl.BlockSpec((None, 1, d), per_batch),
            pl.BlockSpec((ROUTER_LANES, d), const),
            pl.BlockSpec((2 * ROUTER_LANES, d), const),
            pl.BlockSpec((ROUTER_LANES, 1), const),
        ],
        out_specs=[
            pl.BlockSpec((tm, d), lambda i: (i, 0)),
            pl.BlockSpec((tm, d // 2), lambda i: (i, 0)),
            pl.BlockSpec((tm, ROUTER_LANES), lambda i: (i, 0)),
            pl.BlockSpec((EXPERTS_PER_GROUP, tm), lambda i: (0, i)),
            pl.BlockSpec((N_EXPERTS, LANES), const),
        ],
        out_shape=[
            jax.ShapeDtypeStruct((t, d), jnp.float32),
            jax.ShapeDtypeStruct((t, d // 2), jnp.uint32),
            jax.ShapeDtypeStruct((t, ROUTER_LANES), jnp.float32),
            jax.ShapeDtypeStruct((EXPERTS_PER_GROUP, t), jnp.float32),
            jax.ShapeDtypeStruct((N_EXPERTS, LANES), jnp.float32),
        ],
        scratch_shapes=[pltpu.VMEM((N_EXPERTS, LANES), jnp.float32),
                        pltpu.VMEM((tm, tm), jnp.bfloat16)],
        compiler_params=_params(("arbitrary",)),
        name="outproj_router",
    )(a, w_out, x2d, gt, g.reshape(1, d), sc, sh, w_hi, jnp.concatenate([w_hi, w_lo], axis=0), bias)


def _sc_mesh():
    return plsc.VectorSubcoreMesh(core_axis_name="c", subcore_axis_name="s")


def _sc_dispatch(rows, d0, d1, pad_idx, n_slots):
    t, w = rows.shape
    n_pad = pad_idx.shape[1]
    zeros = jnp.zeros((SC_ROWS, w), rows.dtype)
    sem = (pltpu.PARALLEL, pltpu.ARBITRARY)
    parts = SC_WINDOW // SC_ROWS

    @pl.kernel(out_type=jax.ShapeDtypeStruct((n_slots + SC_WINDOW, w), rows.dtype), mesh=_sc_mesh(),
               scratch_types=[pltpu.SemaphoreType.DMA, pltpu.SemaphoreType.DMA])
    def dispatch(x_hbm, d0_hbm, d1_hbm, z_hbm, p_hbm, o_hbm, sem0, sem1):
        def scatter_rows(x_vmem, i0_vmem, i1_vmem):
            part = pl.ds(pl.program_id(1) * SC_ROWS, SC_ROWS)
            first = pltpu.async_copy(x_vmem, o_hbm.at[i0_vmem.at[0, part]], sem0)
            second = pltpu.async_copy(x_vmem, o_hbm.at[i1_vmem.at[0, part]], sem1)
            first.wait()
            second.wait()

        pltpu.emit_pipeline(
            scatter_rows,
            grid=(t // SC_WINDOW, parts),
            in_specs=[pl.BlockSpec((SC_ROWS, w), lambda i, j: (parts * i + j, 0)),
                      pl.BlockSpec((1, SC_WINDOW), lambda i, j: (0, i)),
                      pl.BlockSpec((1, SC_WINDOW), lambda i, j: (0, i))],
            out_specs=[],
            core_axis_name=("c", "s"),
            dimension_semantics=sem,
        )(x_hbm, d0_hbm, d1_hbm)

        def scatter_zeros(z_vmem, p_vmem):
            part = pl.ds(pl.program_id(1) * SC_ROWS, SC_ROWS)
            pltpu.sync_copy(z_vmem, o_hbm.at[p_vmem.at[0, part]])

        pltpu.emit_pipeline(
            scatter_zeros,
            grid=(n_pad // SC_WINDOW, parts),
            in_specs=[pl.BlockSpec((SC_ROWS, w), lambda i, j: (0, 0)),
                      pl.BlockSpec((1, SC_WINDOW), lambda i, j: (0, i))],
            out_specs=[],
            core_axis_name=("c", "s"),
            dimension_semantics=sem,
        )(z_hbm, p_hbm)

    return dispatch(rows, d0, d1, zeros, pad_idx)


def _sc_gather(src, idx):
    n_out = idx.shape[1]
    w = src.shape[1]
    parts = SC_WINDOW // SC_ROWS

    @pl.kernel(out_type=jax.ShapeDtypeStruct((n_out, w), src.dtype), mesh=_sc_mesh())
    def gather(x_hbm, i_hbm, o_hbm):
        def gather_rows(i_vmem, o_vmem):
            part = pl.ds(pl.program_id(1) * SC_ROWS, SC_ROWS)
            pltpu.sync_copy(x_hbm.at[i_vmem.at[0, part]], o_vmem)

        pltpu.emit_pipeline(
            gather_rows,
            grid=(n_out // SC_WINDOW, parts),
            in_specs=[pl.BlockSpec((1, SC_WINDOW), lambda i, j: (0, i))],
            out_specs=[pl.BlockSpec((SC_ROWS, w), lambda i, j: (parts * i + j, 0))],
            core_axis_name=("c", "s"),
            dimension_semantics=(pltpu.PARALLEL, pltpu.ARBITRARY),
        )(i_hbm, o_hbm)

    return gather(src, idx)


def _expert_kernel(meta_ref, x_ref, w1_hbm, w3_hbm, w2_hbm, y_ref,
                   w1_buf, w3_buf, w2_buf, w1_c, w3_c, w2_c, sems, *, layer, nb):
    def weight_copies(expert, s):
        return (pltpu.make_async_copy(w1_hbm.at[layer, expert], w1_buf.at[s], sems.at[s, 0]),
                pltpu.make_async_copy(w3_hbm.at[layer, expert], w3_buf.at[s], sems.at[s, 1]),
                pltpu.make_async_copy(w2_hbm.at[layer, expert], w2_buf.at[s], sems.at[s, 2]))

    @pl.when(pl.program_id(0) == 0)
    def _():
        for cp in weight_copies(meta_ref[0], 0):
            cp.start()
        second = meta_ref[3 * nb + 1]

        @pl.when(second >= 0)
        def _():
            for cp in weight_copies(second, 1):
                cp.start()

    for j in range(EXPERT_BLOCKS_PER_STEP):
        i = pl.program_id(0) * EXPERT_BLOCKS_PER_STEP + j
        rows = slice(j * MOE_BLOCK, (j + 1) * MOE_BLOCK)
        e = meta_ref[i]
        slot = meta_ref[nb + i]
        nxt = meta_ref[2 * nb + i]
        used = i < meta_ref[3 * nb]
        first = used & ((i == 0) | (e != meta_ref[jnp.maximum(i - 1, 0)]))

        @pl.when(first)
        def _():
            for cp in weight_copies(e, slot):
                cp.wait()

            @pl.when(nxt >= 0)
            def _():
                for cp in weight_copies(nxt, (slot + EXPERT_STAGES - 1) % EXPERT_STAGES):
                    cp.start()

            w1_c[...] = w1_buf[slot].astype(jnp.bfloat16)
            w3_c[...] = w3_buf[slot].astype(jnp.bfloat16)
            w2_c[...] = w2_buf[slot].astype(jnp.bfloat16)

        @pl.when(used)
        def _():
            x_hi, x_lo = _unpack_bf16_pairs(x_ref[rows, :])
            xb = jnp.concatenate([x_hi, x_lo], axis=-1).astype(jnp.bfloat16)
            full = 256
            tail = jnp.concatenate([w1_c[:, full:], w3_c[:, full:]], axis=-1)
            half_rows = MOE_BLOCK // 2
            h_tail = jnp.concatenate([_bdot(xb[:half_rows], tail), _bdot(xb[half_rows:], tail)], axis=0)
            h1 = jnp.concatenate([_bdot(xb, w1_c[:, :full]), h_tail[:, :LANES]], axis=-1)
            h3 = jnp.concatenate([_bdot(xb, w3_c[:, :full]), h_tail[:, LANES:]], axis=-1)
            act = (h1 * jax.nn.sigmoid(h1) * h3).astype(jnp.bfloat16)
            y_ref[rows, :] = _pack_bf16_pairs(_bdot(act, w2_c[...]))

        @pl.when(jnp.logical_not(used))
        def _():
            y_ref[rows, :] = jnp.zeros((MOE_BLOCK, y_ref.shape[1]), y_ref.dtype)


def _experts(x_slots, blk_meta, w1, w3, w2, layer, nb):
    dp = x_slots.shape[1]
    d, de = w1.shape[-2:]
    step_rows = EXPERT_BLOCKS_PER_STEP * MOE_BLOCK
    assert nb % EXPERT_BLOCKS_PER_STEP == 0
    last_used_step = lambda s: (s[3 * nb] - 1) // EXPERT_BLOCKS_PER_STEP
    grid_spec = pltpu.PrefetchScalarGridSpec(
        num_scalar_prefetch=1,
        grid=(nb // EXPERT_BLOCKS_PER_STEP,),
        in_specs=[
            pl.BlockSpec((step_rows, dp), lambda i, s: (jnp.minimum(i, last_used_step(s)), 0)),
            pl.BlockSpec(memory_space=pl.ANY),
            pl.BlockSpec(memory_space=pl.ANY),
            pl.BlockSpec(memory_space=pl.ANY),
        ],
        out_specs=pl.BlockSpec((step_rows, dp), lambda i, s: (i, 0)),
        scratch_shapes=[
            pltpu.VMEM((EXPERT_STAGES, d, de), jnp.float32),
            pltpu.VMEM((EXPERT_STAGES, d, de), jnp.float32),
            pltpu.VMEM((EXPERT_STAGES, de, d), jnp.float32),
            pltpu.VMEM((d, de), jnp.bfloat16),
            pltpu.VMEM((d, de), jnp.bfloat16),
            pltpu.VMEM((de, d), jnp.bfloat16),
            pltpu.SemaphoreType.DMA((EXPERT_STAGES, 3)),
        ],
    )
    return pl.pallas_call(
        functools.partial(_expert_kernel, layer=layer, nb=nb),
        grid_spec=grid_spec,
        out_shape=jax.ShapeDtypeStruct((nb * MOE_BLOCK, dp), jnp.uint32),
        compiler_params=_params(("arbitrary",)),
        name="moe_experts",
    )(blk_meta, x_slots, w1, w3, w2)


def _combine_kernel(x_ref, y1_ref, y2_ref, info_ref, gt_ref, o_ref):
    o_ref[...] = _moe_combined(x_ref[...], y1_ref, y2_ref, info_ref, gt_ref)


def _combine(x2d, y_pairs, info, gt, seq):
    t, d = x2d.shape
    tm = min(COMBINE_TILE, seq)
    tiles_per_batch = seq // tm
    second = t // tm
    return pl.pallas_call(
        _combine_kernel,
        grid=(t // tm,),
        in_specs=[
            pl.BlockSpec((tm, d), lambda i: (i, 0)),
            pl.BlockSpec((tm, d // 2), lambda i: (i, 0)),
            pl.BlockSpec((tm, d // 2), lambda i: (i + second, 0)),
            pl.BlockSpec((tm, ROUTER_LANES), lambda i: (i, 0)),
            pl.BlockSpec((None, 1, d), lambda i: (i // tiles_per_batch, 0, 0)),
        ],
        out_specs=pl.BlockSpec((tm, d), lambda i: (i, 0)),
        out_shape=jax.ShapeDtypeStruct((t, d), jnp.float32),
        compiler_params=_params(("arbitrary",)),
        name="moe_combine",
    )(x2d, y_pairs, y_pairs, info, gt)


def _slot_plan(info_t, cnt, t):
    counts = cnt[:, 0].astype(jnp.int32)
    padded = (counts + MOE_BLOCK - 1) // MOE_BLOCK * MOE_BLOCK
    pad_ends = jnp.cumsum(padded)
    pad_starts = pad_ends - padded
    nb = -(-(2 * t) // MOE_BLOCK) + N_EXPERTS
    n_slots = nb * MOE_BLOCK
    it = info_t.astype(jnp.int32)
    onehot_start = lambda e: jnp.sum(
        jnp.where(e[None, :] == jnp.arange(N_EXPERTS, dtype=jnp.int32)[:, None],
                  pad_starts[:, None], 0), axis=0)
    dest1 = (onehot_start(it[0]) + it[2]).reshape(1, t)
    dest2 = (onehot_start(it[1]) + it[3]).reshape(1, t)
    lane = jnp.arange(MOE_BLOCK, dtype=jnp.int32)[None, :]
    n_padding = (padded - counts)[:, None]
    wrapped = (pad_starts + counts)[:, None] + lane % jnp.maximum(n_padding, 1)
    pad_idx = jnp.where(n_padding > 0, wrapped, n_slots + lane % SC_WINDOW).reshape(-1)
    n_real = pad_ends[-1] // MOE_BLOCK
    n_used = -(-n_real // EXPERT_BLOCKS_PER_STEP) * EXPERT_BLOCKS_PER_STEP
    tail = jnp.arange((EXPERT_BLOCKS_PER_STEP - 1) * MOE_BLOCK, dtype=jnp.int32)
    tail_idx = jnp.where(tail < (n_used - n_real) * MOE_BLOCK, pad_ends[-1] + tail,
                         n_slots + tail % SC_WINDOW)
    pad_idx = jnp.concatenate([pad_idx, tail_idx]).reshape(1, -1)
    experts = jnp.arange(N_EXPERTS, dtype=jnp.int32)
    blk = jnp.arange(nb, dtype=jnp.int32)
    blk_exp = jnp.minimum(
        jnp.sum((pad_ends[None, :] <= (blk * MOE_BLOCK)[:, None]).astype(jnp.int32), axis=1),
        N_EXPERTS - 1)
    blk_exp = jnp.where(blk >= n_real, jnp.max(jnp.where(padded > 0, experts, 0)), blk_exp)
    prev_exp = jnp.concatenate([jnp.full((1,), -1, jnp.int32), blk_exp[:-1]])
    is_first = (blk < n_used) & (blk_exp != prev_exp)
    blk_slot = (jnp.cumsum(is_first.astype(jnp.int32)) + EXPERT_STAGES - 1) % EXPERT_STAGES
    later = (experts[None, :] > experts[:, None]) & (padded[None, :] > 0)
    nxt_of = jnp.min(jnp.where(later, experts[None, :], N_EXPERTS), axis=1)
    nxt_of = jnp.where(nxt_of == N_EXPERTS, -1, nxt_of)
    lookup = lambda table, e: jnp.sum(
        jnp.where(e[:, None] == experts[None, :], table[None, :], 0), axis=1)
    nxt2_of = jnp.where(nxt_of >= 0, lookup(nxt_of, jnp.maximum(nxt_of, 0)), -1)
    blk_nxt = lookup(nxt2_of, blk_exp)
    second = lookup(nxt_of, blk_exp[:1])
    blk_meta = jnp.concatenate([blk_exp, blk_slot, blk_nxt, n_used[None], second]).astype(jnp.int32)
    return dest1, dest2, pad_idx, blk_meta, nb, n_slots


def _mixer_out_and_moe(a, w_out, x2d, gt1, g, sc, sh, w_group, b_group, w_router, b_router,
                       w1, w3, w2, layer, seq):
    t, d = x2d.shape
    w_cat = jnp.zeros((ROUTER_LANES, d), jnp.float32)
    w_cat = w_cat.at[:N_EXPERTS].set(w_router.T).at[N_EXPERTS:N_EXPERTS + N_GROUPS].set(w_group.T)
    b_cat = jnp.zeros((ROUTER_LANES, 1), jnp.float32)
    b_cat = b_cat.at[:N_EXPERTS, 0].set(b_router).at[N_EXPERTS:N_EXPERTS + N_GROUPS, 0].set(b_group)
    w_hi, w_lo = _split_hi_lo(w_cat)
    x_new, hn, info, info_t, cnt = _outproj_router(a, w_out, x2d, gt1, g, sc, sh, w_hi, w_lo,
                                                   b_cat, seq)
    dest1, dest2, pad_idx, blk_meta, nb, n_slots = _slot_plan(info_t, cnt, t)
    x_slots = _sc_dispatch(hn, dest1, dest2, pad_idx, n_slots)
    y_slots = _experts(x_slots, blk_meta, w1, w3, w2, layer, nb)
    y_pairs = _sc_gather(y_slots, jnp.concatenate([dest1, dest2], axis=1))
    return x_new, y_pairs, info


def kernel(x, c, w_ada, b_ada, norm1_g, norm2_g, ml_w_in, ml_b_gate, ml_g_out, ml_w_out,
           sw_w_in, sw_g_q, sw_g_k, sw_sinks, sw_w_out, moe_w_group, moe_b_group,
           moe_w_router, moe_b_router, moe_w1, moe_w3, moe_w2):
    bsz, seq, d = x.shape
    depth = w_ada.shape[0]
    bf = jnp.bfloat16
    mod = _ada_mod(c, w_ada, b_ada)
    x2d = x.reshape(bsz * seq, d)
    pending = None
    for layer in range(depth):
        sh1, sc1, gt1, sh2, sc2, gt2 = [
            mod[layer, :, i * d:(i + 1) * d].reshape(bsz, 1, d) for i in range(6)]
        j = layer // 2
        if layer % 2 == 0:
            w = ml_w_in[j]
            q_w, k_w = w[:, :ML_QK], w[:, ML_QK:2 * ML_QK]
            v_w = w[:, 2 * ML_QK:2 * ML_QK + ML_V]
            o_w = w[:, 2 * ML_QK + ML_V:2 * ML_QK + 2 * ML_V]
            g_w = w[:, 2 * ML_QK + 2 * ML_V:]
            w_main = jnp.concatenate([v_w, o_w, q_w], axis=1).astype(bf)
            wg_t = jnp.zeros((ML_GATE_ROWS, d), jnp.float32).at[:2 * ML_HEADS].set(g_w.T)
            wg_hi, wg_lo = _split_hi_lo(wg_t)
            wk_t = k_w.T.astype(bf)
            w_out = ml_w_out[j].astype(bf)
            outs = _inproj(x2d, norm1_g[layer], sc1, sh1, w_main, seq,
                           ml_extra=(wk_t, wg_hi, wg_lo),
                           q_cols=(2 * ML_V, 2 * ML_V + ML_QK), q_scale=ML_DK ** -0.5,
                           gate_cols=(ML_V, 2 * ML_V), pending=pending)
            if pending is not None:
                x2d, outs = outs[0], outs[1:]
            main, k_t, g_t = outs
            a = _mlstm_core(main, k_t, g_t, ml_b_gate[j], ml_g_out[j], bsz, seq)
        else:
            w = sw_w_in[j]
            dq = SW_Q_HEADS * SW_DH
            dkv = SW_KV_HEADS * SW_DH
            dup = lambda m: jnp.concatenate(
                [m.reshape(d, SW_KV_HEADS, 1, SW_DH)] * 2, axis=2).reshape(d, 2 * dkv)
            w_main = jnp.concatenate(
                [w[:, :dq], dup(w[:, dq:dq + dkv]), dup(w[:, dq + dkv:])], axis=1).astype(bf)
            w_out = sw_w_out[j].astype(bf)
            gq = jnp.concatenate([sw_g_q[j], sw_g_q[j]]).reshape(1, LANES) * (SW_DH ** -0.5 * LOG2E)
            gk = jnp.concatenate([sw_g_k[j], sw_g_k[j]]).reshape(1, LANES)
            outs = _inproj(x2d, norm1_g[layer], sc1, sh1, w_main, seq, pending=pending,
                           qk_norm=(dq, dq + 2 * dkv, gq, gk))
            if pending is not None:
                x2d, outs = outs[0], outs[1:]
            a = _swa_core(outs[0], sw_sinks[j], bsz, seq)
        x2d, y_pairs, info = _mixer_out_and_moe(
            a, w_out, x2d, gt1, norm2_g[layer], sc2, sh2, moe_w_group[layer], moe_b_group[layer],
            moe_w_router[layer], moe_b_router[layer], moe_w1, moe_w3, moe_w2, layer, seq)
        pending = (y_pairs, info, gt2)
    y_pairs, info, gt2 = pending
    return _combine(x2d, y_pairs, info, gt2, seq).reshape(bsz, seq, d)
```

```python
import functools

import jax
import jax.numpy as jnp
from jax import lax
from jax.experimental import pallas as pl
from jax.experimental.pallas import tpu as pltpu
from jax.experimental.pallas import tpu_sc as plsc

EPS = 1e-6
GATE_CAP = 15.0
LOG2E = 1.4426950408889634

ML_HEADS = 4
ML_DK = 128
ML_DV = 256
ML_QK = ML_HEADS * ML_DK
ML_V = ML_HEADS * ML_DV
ML_GATE_ROWS = 16

SW_Q_HEADS = 16
SW_KV_HEADS = 4
SW_GROUP = SW_Q_HEADS // SW_KV_HEADS
SW_DH = 64
SW_WINDOW = 128
LANES = 128

N_GROUPS = 8
EXPERTS_PER_GROUP = 8
N_EXPERTS = N_GROUPS * EXPERTS_PER_GROUP
MOE_BLOCK = 256
ROUTER_LANES = 128
SC_WINDOW = 128
SC_ROWS = 64
EXPERT_BLOCKS_PER_STEP = 4
EXPERT_STAGES = 3

SUBLANES = 8
TOKEN_TILE = 512
ROUTER_TILE = 1024
COMBINE_TILE = 1024
PROJ_COL_CHUNK = 1024
ML_CHUNK = 256
ADA_COL_TILE = 768
V7X_VMEM_BYTES = 64 * 1024 * 1024
VMEM_LIMIT = V7X_VMEM_BYTES - 8 * 1024 * 1024

_NT = (((1,), (1,)), ((), ()))


def _bdot(a, b):
    return jnp.dot(a, b, preferred_element_type=jnp.float32)


def _bdot_nt(a, b):
    return lax.dot_general(a, b, _NT, preferred_element_type=jnp.float32)


def _split_hi_lo(a):
    hi = a.astype(jnp.bfloat16)
    lo = (a - hi.astype(jnp.float32)).astype(jnp.bfloat16)
    return hi, lo


def _params(sem):
    return pltpu.CompilerParams(dimension_semantics=sem, vmem_limit_bytes=VMEM_LIMIT)


def _ada_kernel(c_ref, w_ref, b_ref, o_ref):
    c = c_ref[...]
    cond = c * jax.nn.sigmoid(c)
    c_hi, c_lo = _split_hi_lo(cond)
    w_hi, w_lo = _split_hi_lo(w_ref[...])
    acc = _bdot(c_hi, w_hi) + (_bdot(c_lo, w_hi) + _bdot(c_hi, w_lo))
    o_ref[...] = acc + b_ref[...]


def _ada_mod(c, w_ada, b_ada, layer):
    depth, d, n = w_ada.shape
    bsz = c.shape[0]
    rows = SUBLANES
    tn = ADA_COL_TILE
    c_pad = jnp.zeros((rows, d), jnp.float32).at[:bsz].set(c)
    out = pl.pallas_call(
        _ada_kernel,
        grid=(n // tn,),
        in_specs=[
            pl.BlockSpec((rows, d), lambda j: (0, 0)),
            pl.BlockSpec((None, d, tn), lambda j: (layer, 0, j)),
            pl.BlockSpec((None, 1, tn), lambda j: (layer, 0, j)),
        ],
        out_specs=pl.BlockSpec((rows, tn), lambda j: (0, j)),
        out_shape=jax.ShapeDtypeStruct((rows, n), jnp.float32),
        compiler_params=_params(("arbitrary",)),
        name="ada_mod",
    )(c_pad, w_ada, b_ada.reshape(depth, 1, n))
    return out[:bsz]


def _modulated_norm(x, g, sc, sh):
    y = x * lax.rsqrt(jnp.mean(x * x, axis=-1, keepdims=True) + EPS)
    return y * (g * (1.0 + sc)) + sh


def _moe_combined(x, y1_ref, y2_ref, info_ref, gt_ref):
    info = info_ref[...]
    g1 = info[:, 4:5]
    g2 = info[:, 5:6]
    y1_hi, y1_lo = _unpack_bf16_pairs(y1_ref[...])
    y2_hi, y2_lo = _unpack_bf16_pairs(y2_ref[...])
    y = jnp.concatenate([g1 * y1_hi + g2 * y2_hi, g1 * y1_lo + g2 * y2_lo], axis=-1)
    return x + gt_ref[...] * y


def _qk_head_norm(acc, c0, qk_norm, gq, gk):
    q_hi, k_hi = qk_norm
    low = lax.broadcasted_iota(jnp.int32, (acc.shape[0], LANES), 1) < SW_DH
    slabs = []
    for j in range(acc.shape[1] // LANES):
        slab = acc[:, j * LANES:(j + 1) * LANES]
        if c0 + j * LANES < q_hi:
            sq = slab * slab
            ss_lo = jnp.sum(jnp.where(low, sq, 0.0), axis=-1, keepdims=True)
            ss_hi = jnp.sum(jnp.where(low, 0.0, sq), axis=-1, keepdims=True)
            rs = jnp.where(low, lax.rsqrt(ss_lo / SW_DH + EPS), lax.rsqrt(ss_hi / SW_DH + EPS))
            slab = slab * rs * gq
        elif c0 + j * LANES < k_hi:
            slab = slab * lax.rsqrt(jnp.mean(slab * slab, axis=-1, keepdims=True) + EPS) * gk
        slabs.append(slab)
    return jnp.concatenate(slabs, axis=-1)


def _inproj_kernel(*refs, n_main, chunk, q_cols, q_scale, gate_cols, qk_norm, with_ml,
                   with_combine):
    refs = list(refs)
    n_in = (5 + (4 if with_combine else 0) + (2 if with_ml else 0)
            + (2 if qk_norm is not None else 0))
    n_scratch = 4 if with_ml else 2
    ins, outs, scratch = refs[:n_in], refs[n_in:-n_scratch], refs[-n_scratch:]
    x_ref = ins.pop(0)
    if with_combine:
        y1_ref, y2_ref, info_ref, gtp_ref = ins[:4]
        ins = ins[4:]
        xo_ref = outs.pop(0)
    g_ref, sc_ref, sh_ref, w_ref = ins[:4]
    o_ref = outs[0]
    if with_ml:
        wk_ref, wgh_ref = ins[4:]
        kt_ref, gt_ref = outs[1:]
    if qk_norm is not None:
        gq_ref, gk_ref = ins[4:]

    def normalise(hb_dst, lo_dst):
        x = x_ref[...]
        if with_combine:
            x = _moe_combined(x, y1_ref, y2_ref, info_ref, gtp_ref)
            xo_ref[...] = x
        hn = _modulated_norm(x, g_ref[...], sc_ref[...], sh_ref[...])
        hb = hn.astype(jnp.bfloat16)
        hb_dst[...] = hb
        if with_ml:
            lo_dst[...] = (hn - hb.astype(jnp.float32)).astype(jnp.bfloat16)

    def project(hb_src, lo_src):
        hb = hb_src[...]
        for c0 in range(0, n_main, chunk):
            c1 = min(c0 + chunk, n_main)
            acc = _bdot(hb, w_ref[:, c0:c1])
            if q_cols is not None and q_cols[0] <= c0 < q_cols[1]:
                acc = acc * q_scale
            if gate_cols is not None and gate_cols[0] <= c0 < gate_cols[1]:
                acc = jax.nn.sigmoid(acc)
            if qk_norm is not None and c0 < qk_norm[1]:
                acc = _qk_head_norm(acc, c0, qk_norm, gq_ref[...], gk_ref[...])
            o_ref[:, c0:c1] = acc.astype(o_ref.dtype)
        if with_ml:
            nk = kt_ref.shape[0]
            stacked = _bdot_nt(wk_ref[...], hb)
            kt_ref[...] = stacked[:nk].astype(kt_ref.dtype)
            gt_ref[...] = (stacked[nk:nk + ML_GATE_ROWS]
                           + (_bdot_nt(wgh_ref[...], lo_src[...]) + stacked[nk + ML_GATE_ROWS:]))

    hb_a, hb_b = scratch[:2]
    lo_a, lo_b = scratch[2:] if with_ml else (None, None)
    s = pl.program_id(0)

    @pl.when(s == 0)
    def _():
        hb_b[...] = jnp.zeros_like(hb_b)
        if with_ml:
            lo_b[...] = jnp.zeros_like(lo_b)

    @pl.when(s % 2 == 0)
    def _():
        normalise(hb_a, lo_a)
        project(hb_b, lo_b)

    @pl.when(s % 2 == 1)
    def _():
        normalise(hb_b, lo_b)
        project(hb_a, lo_a)


def _inproj(x2d, g, sc, sh, w_main, seq, *, ml_extra=None, q_cols=None, q_scale=1.0,
            gate_cols=None, qk_norm=None, pending=None):
    t, d = x2d.shape
    tm = TOKEN_TILE if ml_extra is not None else min(2 * TOKEN_TILE, seq)
    n_main = w_main.shape[1]
    tiles_per_batch = seq // tm
    n_tiles = t // tm
    resident = pl.Buffered(1)
    norm_tile = lambda s: jnp.minimum(s, n_tiles - 1)
    proj_tile = lambda s: jnp.maximum(s - 1, 0)
    row = lambda s: (norm_tile(s), 0)
    per_batch = lambda s: (norm_tile(s) // tiles_per_batch, 0, 0)
    const = lambda s: (0, 0)
    in_specs = [pl.BlockSpec((tm, d), row)]
    args = [x2d]
    out_specs, out_shape = [], []
    if pending is not None:
        y_pairs, info, gt_prev = pending
        in_specs += [pl.BlockSpec((tm, d // 2), row),
                     pl.BlockSpec((tm, d // 2), lambda s: (norm_tile(s) + n_tiles, 0)),
                     pl.BlockSpec((tm, ROUTER_LANES), row),
                     pl.BlockSpec((None, 1, d), per_batch)]
        args += [y_pairs, y_pairs, info, gt_prev]
        out_specs += [pl.BlockSpec((tm, d), row)]
        out_shape += [jax.ShapeDtypeStruct((t, d), jnp.float32)]
    in_specs += [
        pl.BlockSpec((1, d), const),
        pl.BlockSpec((None, 1, d), per_batch),
        pl.BlockSpec((None, 1, d), per_batch),
        pl.BlockSpec((d, n_main), const, pipeline_mode=resident),
    ]
    args += [g.reshape(1, d), sc, sh, w_main]
    out_specs += [pl.BlockSpec((tm, n_main), lambda s: (proj_tile(s), 0))]
    out_shape += [jax.ShapeDtypeStruct((t, n_main), jnp.bfloat16)]
    scratch = [pltpu.VMEM((tm, d), jnp.bfloat16), pltpu.VMEM((tm, d), jnp.bfloat16)]
    if ml_extra is not None:
        wk_t, wg_hi, wg_lo = ml_extra
        nk = wk_t.shape[0]
        stacked = jnp.concatenate([wk_t, wg_hi, wg_lo], axis=0)
        in_specs += [pl.BlockSpec(stacked.shape, const, pipeline_mode=resident),
                     pl.BlockSpec(wg_hi.shape, const, pipeline_mode=resident)]
        args += [stacked, wg_hi]
        out_specs += [pl.BlockSpec((nk, tm), lambda s: (0, proj_tile(s))),
                      pl.BlockSpec((ML_GATE_ROWS, tm), lambda s: (0, proj_tile(s)))]
        out_shape += [jax.ShapeDtypeStruct((nk, t), jnp.bfloat16),
                      jax.ShapeDtypeStruct((ML_GATE_ROWS, t), jnp.float32)]
        scratch += [pltpu.VMEM((tm, d), jnp.bfloat16), pltpu.VMEM((tm, d), jnp.bfloat16)]
    if qk_norm is not None:
        in_specs += [pl.BlockSpec((1, LANES), const), pl.BlockSpec((1, LANES), const)]
        args += [qk_norm[2], qk_norm[3]]
    kern = functools.partial(_inproj_kernel, n_main=n_main, chunk=PROJ_COL_CHUNK, q_cols=q_cols,
                             q_scale=q_scale, gate_cols=gate_cols,
                             qk_norm=None if qk_norm is None else qk_norm[:2],
                             with_ml=ml_extra is not None, with_combine=pending is not None)
    return pl.pallas_call(
        kern,
        grid=(n_tiles + 1,),
        in_specs=in_specs,
        out_specs=out_specs,
        out_shape=out_shape,
        scratch_shapes=scratch,
        compiler_params=_params(("arbitrary",)),
        name="inproj_ml" if ml_extra is not None else "inproj_sw",
    )(*args)


def _mlstm_gate_terms(graw, bias, upper):
    H = ML_HEADS
    L = graw.shape[1]
    z = graw + bias
    gates = GATE_CAP * jnp.tanh(z / GATE_CAP)
    log_f = jnp.minimum(gates, 0.0) - jnp.log1p(jnp.exp(-jnp.abs(gates)))
    row = lax.broadcasted_iota(jnp.int32, (ML_GATE_ROWS, L), 0)
    lane = lax.broadcasted_iota(jnp.int32, (ML_GATE_ROWS, L), 1)
    is_i = row < H
    slab = jnp.where(is_i, gates, log_f)
    a1 = slab.astype(jnp.bfloat16)
    r1 = slab - a1.astype(jnp.float32)
    a2 = r1.astype(jnp.bfloat16)
    a3 = (r1 - a2.astype(jnp.float32)).astype(jnp.bfloat16)
    cum = _bdot(a1, upper) + (_bdot(a2, upper) + _bdot(a3, upper))
    ib = jnp.where(is_i, gates, cum)
    b = pltpu.roll(ib, ML_GATE_ROWS - H, 0)
    u = ib - b
    cm = u
    shift = 1
    while shift < L:
        cm = jnp.maximum(cm, jnp.where(lane >= shift, pltpu.roll(cm, shift, 1), -jnp.inf))
        shift *= 2
    return b, u, cm


def _mlstm_kernel(v_ref, o_ref, q_ref, kt_ref, gt_ref, gtn_ref, bg_ref, gout_ref, out_ref,
                  c_ref, m_ref, b_ref, u_ref, cm_ref, *, chunk):
    L = chunk
    H, dk, dv = ML_HEADS, ML_DK, ML_DV
    r_idx = lax.broadcasted_iota(jnp.int32, (L, L), 0)
    c_idx = lax.broadcasted_iota(jnp.int32, (L, L), 1)
    upper = jnp.where(r_idx <= c_idx, 1.0, 0.0).astype(jnp.bfloat16)
    causal = r_idx >= c_idx
    row = lax.broadcasted_iota(jnp.int32, (ML_GATE_ROWS, L), 0)
    ones_col = jnp.where(lax.broadcasted_iota(jnp.int32, (L, LANES), 1) == 0, 1.0, 0.0
                         ).astype(jnp.bfloat16)

    @pl.when(pl.program_id(1) == 0)
    def _():
        c_ref[...] = jnp.zeros_like(c_ref)
        m_ref[...] = jnp.zeros_like(m_ref)
        b0, u0, cm0 = _mlstm_gate_terms(gt_ref[...], bg_ref[...], upper)
        b_ref[...] = b0
        u_ref[...] = u0
        cm_ref[...] = cm0

    b16 = b_ref[...]
    u16 = u_ref[...]
    cm16 = cm_ref[...]
    b_n, u_n, cm_n = _mlstm_gate_terms(gtn_ref[...], bg_ref[...], upper)
    b_ref[...] = b_n
    u_ref[...] = u_n
    cm_ref[...] = cm_n

    m_prev = m_ref[:, 0:1]
    z16 = jnp.maximum(m_prev, cm16)
    w_inter16 = jnp.exp(m_prev - z16)
    e_negm16 = jnp.exp(-(b16 + z16))
    z_last = z16[:, L - 1:L]
    w_state16 = jnp.exp(u16 - z_last)
    decay16 = jnp.exp(m_prev - z_last)
    m_ref[...] = jnp.broadcast_to(b16[:, L - 1:L] + z_last, m_ref.shape)
    stacked = jnp.where(row < H, z16,
                        jnp.where(row < 2 * H, pltpu.roll(w_inter16, H, 0),
                                  pltpu.roll(e_negm16, 2 * H, 0)))
    cols = jnp.concatenate(
        [stacked, jnp.zeros((LANES - ML_GATE_ROWS, L), jnp.float32)], axis=0).T

    for h in range(H):
        u_r = u16[h:h + 1, :]
        z_c = cols[:, h:h + 1]
        w_inter = cols[:, H + h:H + h + 1]
        e_negm = cols[:, 2 * H + h:2 * H + h + 1]
        c_ext = c_ref[h]
        q = q_ref[:, h * dk:(h + 1) * dk]
        kt = kt_ref[h * dk:(h + 1) * dk, :]
        v_ext = jnp.concatenate([v_ref[:, h * dv:(h + 1) * dv], ones_col], axis=-1)

        w_intra = jnp.exp(jnp.where(causal, u_r - z_c, -jnp.inf))
        s = (_bdot(q, kt) * w_intra).astype(jnp.bfloat16)
        nd = w_inter * _bdot(q, c_ext.astype(jnp.bfloat16)) + _bdot(s, v_ext)
        den = nd[:, dv:dv + 1]
        hb = nd[:, :dv] * (1.0 / jnp.maximum(jnp.abs(den), e_negm))

        kw = (kt.astype(jnp.float32) * w_state16[h:h + 1, :]).astype(jnp.bfloat16)
        c_ref[h] = decay16[h:h + 1, :] * c_ext + _bdot(kw, v_ext)

        y = hb * lax.rsqrt(jnp.mean(hb * hb, axis=-1, keepdims=True) + EPS)
        y = y * gout_ref[:, h * dv:(h + 1) * dv]
        og = o_ref[:, h * dv:(h + 1) * dv].astype(jnp.float32)
        out_ref[:, h * dv:(h + 1) * dv] = (y * og).astype(out_ref.dtype)


def _mlstm_core(main, k_t, g_t, b_gate, g_out, bsz, seq):
    chunk = ML_CHUNK
    t = main.shape[0]
    nc = seq // chunk
    blk = lambda b, c: b * nc + c
    bg = jnp.zeros((ML_GATE_ROWS, 1), jnp.float32).at[:2 * ML_HEADS, 0].set(b_gate)
    return pl.pallas_call(
        functools.partial(_mlstm_kernel, chunk=chunk),
        grid=(bsz, nc),
        in_specs=[
            pl.BlockSpec((chunk, ML_V), lambda b, c: (blk(b, c), 0)),
            pl.BlockSpec((chunk, ML_V), lambda b, c: (blk(b, c), 1)),
            pl.BlockSpec((chunk, ML_QK), lambda b, c: (blk(b, c), 4)),
            pl.BlockSpec((ML_QK, chunk), lambda b, c: (0, blk(b, c))),
            pl.BlockSpec((ML_GATE_ROWS, chunk), lambda b, c: (0, blk(b, c))),
            pl.BlockSpec((ML_GATE_ROWS, chunk), lambda b, c: (0, blk(b, jnp.minimum(c + 1, nc - 1)))),
            pl.BlockSpec((ML_GATE_ROWS, 1), lambda b, c: (0, 0)),
            pl.BlockSpec((1, ML_V), lambda b, c: (0, 0)),
        ],
        out_specs=pl.BlockSpec((chunk, ML_V), lambda b, c: (blk(b, c), 0)),
        out_shape=jax.ShapeDtypeStruct((t, ML_V), jnp.bfloat16),
        scratch_shapes=[
            pltpu.VMEM((ML_HEADS, ML_DK, ML_DV + LANES), jnp.float32),
            pltpu.VMEM((ML_GATE_ROWS, LANES), jnp.float32),
            pltpu.VMEM((ML_GATE_ROWS, chunk), jnp.float32),
            pltpu.VMEM((ML_GATE_ROWS, chunk), jnp.float32),
            pltpu.VMEM((ML_GATE_ROWS, chunk), jnp.float32),
        ],
        compiler_params=_params(("arbitrary", "arbitrary")),
        name="mlstm_core",
    )(main, main, main, k_t, g_t, g_t, bg, g_out.reshape(1, ML_V))


def _swa_kernel(q_ref, kc_ref, kp_ref, vc_ref, vp_ref, bias_ref, o_ref):
    W = SW_WINDOW
    lane = lax.broadcasted_iota(jnp.int32, (W, LANES), 1)
    low = lane < SW_DH
    not_first_row = lax.broadcasted_iota(jnp.int32, (W, LANES), 0) > 0
    variant = jnp.minimum(pl.program_id(1), 1)
    for g in range(SW_KV_HEADS):
        sl = slice(g * LANES, (g + 1) * LANES)
        k_prev = jnp.where(not_first_row, kp_ref[:, sl].astype(jnp.float32), 0.0)
        v_prev = jnp.where(not_first_row, vp_ref[:, sl].astype(jnp.float32), 0.0)
        kn = jnp.concatenate([k_prev.astype(jnp.bfloat16), kc_ref[:, sl]], axis=0)
        v2 = jnp.concatenate([v_prev.astype(jnp.bfloat16), vc_ref[:, sl]], axis=0)
        parts = []
        for p in range(2):
            c0 = g * SW_GROUP * SW_DH + p * LANES
            qp = q_ref[:, c0:c0 + LANES].astype(jnp.float32)
            parts.append(jnp.where(low, qp, 0.0).astype(jnp.bfloat16))
            parts.append(jnp.where(low, 0.0, qp).astype(jnp.bfloat16))
        q4 = jnp.concatenate(parts, axis=0)
        scores = _bdot_nt(q4, kn) + bias_ref[variant, g]
        m = jnp.max(scores, axis=-1, keepdims=True)
        pexp = jnp.exp2(scores - m)
        denom = jnp.sum(pexp, axis=-1, keepdims=True)
        o4 = _bdot(pexp.astype(jnp.bfloat16), v2) * (1.0 / denom)
        for p in range(2):
            oa = o4[(2 * p) * W:(2 * p + 1) * W]
            ob = o4[(2 * p + 1) * W:(2 * p + 2) * W]
            c0 = g * SW_GROUP * SW_DH + p * LANES
            o_ref[:, c0:c0 + LANES] = jnp.where(low, oa, ob).astype(o_ref.dtype)


def _swa_core(proj, sinks, bsz, seq):
    t = proj.shape[0]
    W = SW_WINDOW
    nb = seq // W
    dq = SW_Q_HEADS * SW_DH
    kv_w = SW_KV_HEADS * LANES
    k_blk = dq // kv_w
    v_blk = k_blk + 1
    cur = lambda b, n: b * nb + n
    prev = lambda b, n: b * nb + jnp.maximum(n - 1, 0)
    qi = (jnp.arange(SW_GROUP * W) % W)[:, None]
    ki = jnp.arange(2 * W)[None, :]
    rel = qi + W - ki
    in_win = (rel >= 0) & (rel < W)
    window = jnp.stack([jnp.where(in_win & (ki >= W), 0.0, -jnp.inf),
                        jnp.where(in_win, 0.0, -jnp.inf)]).astype(jnp.float32)
    sink_rows = jnp.repeat(sinks.astype(jnp.float32).reshape(SW_KV_HEADS, SW_GROUP) * LOG2E, W, axis=1)
    bias = jnp.where(ki[None, None] == 0, sink_rows[None, :, :, None], window[:, None])
    return pl.pallas_call(
        _swa_kernel,
        grid=(bsz, nb),
        in_specs=[
            pl.BlockSpec((W, dq), lambda b, n: (cur(b, n), 0)),
            pl.BlockSpec((W, kv_w), lambda b, n: (cur(b, n), k_blk)),
            pl.BlockSpec((W, kv_w), lambda b, n: (prev(b, n), k_blk)),
            pl.BlockSpec((W, kv_w), lambda b, n: (cur(b, n), v_blk)),
            pl.BlockSpec((W, kv_w), lambda b, n: (prev(b, n), v_blk)),
            pl.BlockSpec(bias.shape, lambda b, n: (0, 0, 0, 0)),
        ],
        out_specs=pl.BlockSpec((W, dq), lambda b, n: (cur(b, n), 0)),
        out_shape=jax.ShapeDtypeStruct((t, dq), jnp.bfloat16),
        compiler_params=_params(("arbitrary", "arbitrary")),
        name="swa_core",
    )(proj, proj, proj, proj, proj, bias)


def _pack_rounded_pairs(r):
    k = r.shape[1] // 2
    hi = lax.bitcast_convert_type(r[:, :k], jnp.uint32)
    lo = lax.bitcast_convert_type(r[:, k:], jnp.uint32)
    return hi | (lo >> 16)


def _pack_bf16_pairs(a):
    return _pack_rounded_pairs(a.astype(jnp.bfloat16).astype(jnp.float32))


def _unpack_bf16_pairs(u):
    hi = lax.bitcast_convert_type(u & jnp.uint32(0xFFFF0000), jnp.float32)
    lo = lax.bitcast_convert_type(u << 16, jnp.float32)
    return hi, lo


def _route_tile(x_new, g_ref, sc_ref, sh_ref, wh_ref, wl_ref, b_ref,
                hn_ref, info_ref, infot_ref, cnt_ref, carry_ref, earlier_ref, tm):
    hn = _modulated_norm(x_new, g_ref[...], sc_ref[...], sh_ref[...])
    h_hi = hn.astype(jnp.bfloat16)
    hi_f32 = h_hi.astype(jnp.float32)
    h_lo = (hn - hi_f32).astype(jnp.bfloat16)
    hn_ref[...] = _pack_rounded_pairs(hi_f32)
    wide = _bdot_nt(wl_ref[...], h_hi)
    logits = (wide[:ROUTER_LANES] + (_bdot_nt(wh_ref[...], h_lo) + wide[ROUTER_LANES:])
              + b_ref[...])
    E8 = EXPERTS_PER_GROUP
    sub = lax.broadcasted_iota(jnp.int32, (E8, tm), 0).astype(jnp.float32)
    big = float(ROUTER_LANES)
    neg = -jnp.inf

    gl = logits[N_EXPERTS:N_EXPERTS + N_GROUPS]
    gmax = jnp.max(gl, axis=0, keepdims=True)
    gsel = jnp.min(jnp.where(gl == gmax, sub, big), axis=0, keepdims=True)
    p_grp = 1.0 / jnp.sum(jnp.exp(gl - gmax), axis=0, keepdims=True)

    el = logits[0:E8]
    for grp in range(1, N_GROUPS):
        el = jnp.where(gsel == grp, logits[grp * E8:(grp + 1) * E8], el)
    v1 = jnp.max(el, axis=0, keepdims=True)
    j1 = jnp.min(jnp.where(el == v1, sub, big), axis=0, keepdims=True)
    el2 = jnp.where(sub == j1, neg, el)
    v2 = jnp.max(el2, axis=0, keepdims=True)
    j2 = jnp.min(jnp.where(el2 == v2, sub, big), axis=0, keepdims=True)
    i1 = gsel * E8 + j1
    i2 = gsel * E8 + j2
    e21 = jnp.exp(v2 - v1)
    gate1 = p_grp / (1.0 + e21)
    gate2 = p_grp * e21 / (1.0 + e21)

    erow = lax.broadcasted_iota(jnp.int32, (N_EXPERTS, tm), 0).astype(jnp.float32)
    hit1 = erow == i1
    hit2 = erow == i2
    onehot = jnp.where(hit1 | hit2, 1.0, 0.0)
    carry = carry_ref[:, 0:1]
    before = _bdot(onehot.astype(jnp.bfloat16), earlier_ref[...]) + carry
    rank1 = jnp.sum(jnp.where(hit1, before, 0.0), axis=0, keepdims=True)
    rank2 = jnp.sum(jnp.where(hit2, before, 0.0), axis=0, keepdims=True)
    total = carry + jnp.sum(onehot, axis=1, keepdims=True)
    carry_ref[...] = jnp.broadcast_to(total, carry_ref.shape)
    cnt_ref[...] = jnp.broadcast_to(total, cnt_ref.shape)

    info_t = jnp.where(sub == 0, i1, 0.0)
    info_t = jnp.where(sub == 1, i2, info_t)
    info_t = jnp.where(sub == 2, rank1, info_t)
    info_t = jnp.where(sub == 3, rank2, info_t)
    info_t = jnp.where(sub == 4, gate1, info_t)
    info_t = jnp.where(sub == 5, gate2, info_t)
    infot_ref[...] = info_t
    info_ref[...] = jnp.concatenate(
        [info_t, jnp.zeros((ROUTER_LANES - E8, tm), jnp.float32)], axis=0).T


def _router_kernel(a_ref, wo_ref, x_ref, gt_ref, g_ref, sc_ref, sh_ref, wh_ref, wl_ref, b_ref,
                   xo_ref, hn_ref, info_ref, infot_ref, cnt_ref, carry_ref, earlier_ref, *, tm):
    @pl.when(pl.program_id(0) == 0)
    def _():
        carry_ref[...] = jnp.zeros_like(carry_ref)
        r_idx = lax.broadcasted_iota(jnp.int32, (tm, tm), 0)
        c_idx = lax.broadcasted_iota(jnp.int32, (tm, tm), 1)
        earlier_ref[...] = jnp.where(r_idx < c_idx, 1.0, 0.0).astype(jnp.bfloat16)

    x_new = x_ref[...] + gt_ref[...] * _bdot(a_ref[...], wo_ref[...])
    xo_ref[...] = x_new
    _route_tile(x_new, g_ref, sc_ref, sh_ref, wh_ref, wl_ref, b_ref,
                hn_ref, info_ref, infot_ref, cnt_ref, carry_ref, earlier_ref, tm)


def _outproj_router(a, w_out, x2d, gt, g, sc, sh, w_hi, w_lo, bias, seq):
    t, d = x2d.shape
    tm = min(ROUTER_TILE, seq)
    tiles_per_batch = seq // tm
    per_batch = lambda i: (i // tiles_per_batch, 0, 0)
    const = lambda i: (0, 0)
    return pl.pallas_call(
        functools.partial(_router_kernel, tm=tm),
        grid=(t // tm,),
        in_specs=[
            pl.BlockSpec((tm, a.shape[1]), lambda i: (i, 0)),
            pl.BlockSpec(w_out.shape, const),
            pl.BlockSpec((tm, d), lambda i: (i, 0)),
            pl.BlockSpec((None, 1, d), per_batch),
            pl.BlockSpec((1, d), const),
            pl.BlockSpec((None, 1, d), per_batch),
            pl.BlockSpec((None, 1, d), per_batch),
            pl.BlockSpec((ROUTER_LANES, d), const),
            pl.BlockSpec((2 * ROUTER_LANES, d), const),
            pl.BlockSpec((ROUTER_LANES, 1), const),
        ],
        out_specs=[
            pl.BlockSpec((tm, d), lambda i: (i, 0)),
            pl.BlockSpec((tm, d // 2), lambda i: (i, 0)),
            pl.BlockSpec((tm, ROUTER_LANES), lambda i: (i, 0)),
            pl.BlockSpec((EXPERTS_PER_GROUP, tm), lambda i: (0, i)),
            pl.BlockSpec((N_EXPERTS, LANES), const),
        ],
        out_shape=[
            jax.ShapeDtypeStruct((t, d), jnp.float32),
            jax.ShapeDtypeStruct((t, d // 2), jnp.uint32),
            jax.ShapeDtypeStruct((t, ROUTER_LANES), jnp.float32),
            jax.ShapeDtypeStruct((EXPERTS_PER_GROUP, t), jnp.float32),
            jax.ShapeDtypeStruct((N_EXPERTS, LANES), jnp.float32),
        ],
        scratch_shapes=[pltpu.VMEM((N_EXPERTS, LANES), jnp.float32),
                        pltpu.VMEM((tm, tm), jnp.bfloat16)],
        compiler_params=_params(("arbitrary",)),
        name="outproj_router",
    )(a, w_out, x2d, gt, g.reshape(1, d), sc, sh, w_hi, jnp.concatenate([w_hi, w_lo], axis=0), bias)


def _sc_mesh():
    return plsc.VectorSubcoreMesh(core_axis_name="c", subcore_axis_name="s")


def _sc_dispatch(rows, d0, d1, pad_idx, n_slots):
    t, w = rows.shape
    n_pad = pad_idx.shape[1]
    zeros = jnp.zeros((SC_ROWS, w), rows.dtype)
    sem = (pltpu.PARALLEL, pltpu.ARBITRARY)
    parts = SC_WINDOW // SC_ROWS

    @pl.kernel(out_type=jax.ShapeDtypeStruct((n_slots + SC_WINDOW, w), rows.dtype), mesh=_sc_mesh(),
               scratch_types=[pltpu.SemaphoreType.DMA, pltpu.SemaphoreType.DMA])
    def dispatch(x_hbm, d0_hbm, d1_hbm, z_hbm, p_hbm, o_hbm, sem0, sem1):
        def scatter_rows(x_vmem, i0_vmem, i1_vmem):
            part = pl.ds(pl.program_id(1) * SC_ROWS, SC_ROWS)
            first = pltpu.async_copy(x_vmem, o_hbm.at[i0_vmem.at[0, part]], sem0)
            second = pltpu.async_copy(x_vmem, o_hbm.at[i1_vmem.at[0, part]], sem1)
            first.wait()
            second.wait()

        pltpu.emit_pipeline(
            scatter_rows,
            grid=(t // SC_WINDOW, parts),
            in_specs=[pl.BlockSpec((SC_ROWS, w), lambda i, j: (parts * i + j, 0)),
                      pl.BlockSpec((1, SC_WINDOW), lambda i, j: (0, i)),
                      pl.BlockSpec((1, SC_WINDOW), lambda i, j: (0, i))],
            out_specs=[],
            core_axis_name=("c", "s"),
            dimension_semantics=sem,
        )(x_hbm, d0_hbm, d1_hbm)

        def scatter_zeros(z_vmem, p_vmem):
            part = pl.ds(pl.program_id(1) * SC_ROWS, SC_ROWS)
            pltpu.sync_copy(z_vmem, o_hbm.at[p_vmem.at[0, part]])

        pltpu.emit_pipeline(
            scatter_zeros,
            grid=(n_pad // SC_WINDOW, parts),
            in_specs=[pl.BlockSpec((SC_ROWS, w), lambda i, j: (0, 0)),
                      pl.BlockSpec((1, SC_WINDOW), lambda i, j: (0, i))],
            out_specs=[],
            core_axis_name=("c", "s"),
            dimension_semantics=sem,
        )(z_hbm, p_hbm)

    return dispatch(rows, d0, d1, zeros, pad_idx)


def _sc_gather(src, idx):
    n_out = idx.shape[1]
    w = src.shape[1]
    parts = SC_WINDOW // SC_ROWS

    @pl.kernel(out_type=jax.ShapeDtypeStruct((n_out, w), src.dtype), mesh=_sc_mesh())
    def gather(x_hbm, i_hbm, o_hbm):
        def gather_rows(i_vmem, o_vmem):
            part = pl.ds(pl.program_id(1) * SC_ROWS, SC_ROWS)
            pltpu.sync_copy(x_hbm.at[i_vmem.at[0, part]], o_vmem)

        pltpu.emit_pipeline(
            gather_rows,
            grid=(n_out // SC_WINDOW, parts),
            in_specs=[pl.BlockSpec((1, SC_WINDOW), lambda i, j: (0, i))],
            out_specs=[pl.BlockSpec((SC_ROWS, w), lambda i, j: (parts * i + j, 0))],
            core_axis_name=("c", "s"),
            dimension_semantics=(pltpu.PARALLEL, pltpu.ARBITRARY),
        )(i_hbm, o_hbm)

    return gather(src, idx)


def _expert_kernel(meta_ref, x_ref, w1_hbm, w3_hbm, w2_hbm, y_ref,
                   w1_buf, w3_buf, w2_buf, w1_c, w3_c, w2_c, sems, *, layer, nb):
    def weight_copies(expert, s):
        return (pltpu.make_async_copy(w1_hbm.at[layer, expert], w1_buf.at[s], sems.at[s, 0]),
                pltpu.make_async_copy(w3_hbm.at[layer, expert], w3_buf.at[s], sems.at[s, 1]),
                pltpu.make_async_copy(w2_hbm.at[layer, expert], w2_buf.at[s], sems.at[s, 2]))

    @pl.when(pl.program_id(0) == 0)
    def _():
        for cp in weight_copies(meta_ref[0], 0):
            cp.start()
        second = meta_ref[3 * nb + 1]

        @pl.when(second >= 0)
        def _():
            for cp in weight_copies(second, 1):
                cp.start()

    for j in range(EXPERT_BLOCKS_PER_STEP):
        i = pl.program_id(0) * EXPERT_BLOCKS_PER_STEP + j
        rows = slice(j * MOE_BLOCK, (j + 1) * MOE_BLOCK)
        e = meta_ref[i]
        slot = meta_ref[nb + i]
        nxt = meta_ref[2 * nb + i]
        used = i < meta_ref[3 * nb]
        first = used & ((i == 0) | (e != meta_ref[jnp.maximum(i - 1, 0)]))

        @pl.when(first)
        def _():
            for cp in weight_copies(e, slot):
                cp.wait()

            @pl.when(nxt >= 0)
            def _():
                for cp in weight_copies(nxt, (slot + EXPERT_STAGES - 1) % EXPERT_STAGES):
                    cp.start()

            w1_c[...] = w1_buf[slot].astype(jnp.bfloat16)
            w3_c[...] = w3_buf[slot].astype(jnp.bfloat16)
            w2_c[...] = w2_buf[slot].astype(jnp.bfloat16)

        @pl.when(used)
        def _():
            x_hi, x_lo = _unpack_bf16_pairs(x_ref[rows, :])
            xb = jnp.concatenate([x_hi, x_lo], axis=-1).astype(jnp.bfloat16)
            full = 256
            tail = jnp.concatenate([w1_c[:, full:], w3_c[:, full:]], axis=-1)
            half_rows = MOE_BLOCK // 2
            h_tail = jnp.concatenate([_bdot(xb[:half_rows], tail), _bdot(xb[half_rows:], tail)], axis=0)
            h1 = jnp.concatenate([_bdot(xb, w1_c[:, :full]), h_tail[:, :LANES]], axis=-1)
            h3 = jnp.concatenate([_bdot(xb, w3_c[:, :full]), h_tail[:, LANES:]], axis=-1)
            act = (h1 * jax.nn.sigmoid(h1) * h3).astype(jnp.bfloat16)
            y_ref[rows, :] = _pack_bf16_pairs(_bdot(act, w2_c[...]))

        @pl.when(jnp.logical_not(used))
        def _():
            y_ref[rows, :] = jnp.zeros((MOE_BLOCK, y_ref.shape[1]), y_ref.dtype)


def _experts(x_slots, blk_meta, w1, w3, w2, layer, nb):
    dp = x_slots.shape[1]
    d, de = w1.shape[-2:]
    step_rows = EXPERT_BLOCKS_PER_STEP * MOE_BLOCK
    assert nb % EXPERT_BLOCKS_PER_STEP == 0
    last_used_step = lambda s: (s[3 * nb] - 1) // EXPERT_BLOCKS_PER_STEP
    grid_spec = pltpu.PrefetchScalarGridSpec(
        num_scalar_prefetch=1,
        grid=(nb // EXPERT_BLOCKS_PER_STEP,),
        in_specs=[
            pl.BlockSpec((step_rows, dp), lambda i, s: (jnp.minimum(i, last_used_step(s)), 0)),
            pl.BlockSpec(memory_space=pl.ANY),
            pl.BlockSpec(memory_space=pl.ANY),
            pl.BlockSpec(memory_space=pl.ANY),
        ],
        out_specs=pl.BlockSpec((step_rows, dp), lambda i, s: (i, 0)),
        scratch_shapes=[
            pltpu.VMEM((EXPERT_STAGES, d, de), jnp.float32),
            pltpu.VMEM((EXPERT_STAGES, d, de), jnp.float32),
            pltpu.VMEM((EXPERT_STAGES, de, d), jnp.float32),
            pltpu.VMEM((d, de), jnp.bfloat16),
            pltpu.VMEM((d, de), jnp.bfloat16),
            pltpu.VMEM((de, d), jnp.bfloat16),
            pltpu.SemaphoreType.DMA((EXPERT_STAGES, 3)),
        ],
    )
    return pl.pallas_call(
        functools.partial(_expert_kernel, layer=layer, nb=nb),
        grid_spec=grid_spec,
        out_shape=jax.ShapeDtypeStruct((nb * MOE_BLOCK, dp), jnp.uint32),
        compiler_params=_params(("arbitrary",)),
        name="moe_experts",
    )(blk_meta, x_slots, w1, w3, w2)


def _combine_kernel(x_ref, y1_ref, y2_ref, info_ref, gt_ref, o_ref):
    o_ref[...] = _moe_combined(x_ref[...], y1_ref, y2_ref, info_ref, gt_ref)


def _combine(x2d, y_pairs, info, gt, seq):
    t, d = x2d.shape
    tm = min(COMBINE_TILE, seq)
    tiles_per_batch = seq // tm
    second = t // tm
    return pl.pallas_call(
        _combine_kernel,
        grid=(t // tm,),
        in_specs=[
            pl.BlockSpec((tm, d), lambda i: (i, 0)),
            pl.BlockSpec((tm, d // 2), lambda i: (i, 0)),
            pl.BlockSpec((tm, d // 2), lambda i: (i + second, 0)),
            pl.BlockSpec((tm, ROUTER_LANES), lambda i: (i, 0)),
            pl.BlockSpec((None, 1, d), lambda i: (i // tiles_per_batch, 0, 0)),
        ],
        out_specs=pl.BlockSpec((tm, d), lambda i: (i, 0)),
        out_shape=jax.ShapeDtypeStruct((t, d), jnp.float32),
        compiler_params=_params(("arbitrary",)),
        name="moe_combine",
    )(x2d, y_pairs, y_pairs, info, gt)


def _slot_plan(info_t, cnt, t):
    counts = cnt[:, 0].astype(jnp.int32)
    padded = (counts + MOE_BLOCK - 1) // MOE_BLOCK * MOE_BLOCK
    pad_ends = jnp.cumsum(padded)
    pad_starts = pad_ends - padded
    nb = -(-(2 * t) // MOE_BLOCK) + N_EXPERTS
    n_slots = nb * MOE_BLOCK
    it = info_t.astype(jnp.int32)
    onehot_start = lambda e: jnp.sum(
        jnp.where(e[None, :] == jnp.arange(N_EXPERTS, dtype=jnp.int32)[:, None],
                  pad_starts[:, None], 0), axis=0)
    dest1 = (onehot_start(it[0]) + it[2]).reshape(1, t)
    dest2 = (onehot_start(it[1]) + it[3]).reshape(1, t)
    lane = jnp.arange(MOE_BLOCK, dtype=jnp.int32)[None, :]
    n_padding = (padded - counts)[:, None]
    wrapped = (pad_starts + counts)[:, None] + lane % jnp.maximum(n_padding, 1)
    pad_idx = jnp.where(n_padding > 0, wrapped, n_slots + lane % SC_WINDOW).reshape(-1)
    n_real = pad_ends[-1] // MOE_BLOCK
    n_used = -(-n_real // EXPERT_BLOCKS_PER_STEP) * EXPERT_BLOCKS_PER_STEP
    tail = jnp.arange((EXPERT_BLOCKS_PER_STEP - 1) * MOE_BLOCK, dtype=jnp.int32)
    tail_idx = jnp.where(tail < (n_used - n_real) * MOE_BLOCK, pad_ends[-1] + tail,
                         n_slots + tail % SC_WINDOW)
    pad_idx = jnp.concatenate([pad_idx, tail_idx]).reshape(1, -1)
    experts = jnp.arange(N_EXPERTS, dtype=jnp.int32)
    blk = jnp.arange(nb, dtype=jnp.int32)
    blk_exp = jnp.minimum(
        jnp.sum((pad_ends[None, :] <= (blk * MOE_BLOCK)[:, None]).astype(jnp.int32), axis=1),
        N_EXPERTS - 1)
    blk_exp = jnp.where(blk >= n_real, jnp.max(jnp.where(padded > 0, experts, 0)), blk_exp)
    prev_exp = jnp.concatenate([jnp.full((1,), -1, jnp.int32), blk_exp[:-1]])
    is_first = (blk < n_used) & (blk_exp != prev_exp)
    blk_slot = (jnp.cumsum(is_first.astype(jnp.int32)) + EXPERT_STAGES - 1) % EXPERT_STAGES
    later = (experts[None, :] > experts[:, None]) & (padded[None, :] > 0)
    nxt_of = jnp.min(jnp.where(later, experts[None, :], N_EXPERTS), axis=1)
    nxt_of = jnp.where(nxt_of == N_EXPERTS, -1, nxt_of)
    lookup = lambda table, e: jnp.sum(
        jnp.where(e[:, None] == experts[None, :], table[None, :], 0), axis=1)
    nxt2_of = jnp.where(nxt_of >= 0, lookup(nxt_of, jnp.maximum(nxt_of, 0)), -1)
    blk_nxt = lookup(nxt2_of, blk_exp)
    second = lookup(nxt_of, blk_exp[:1])
    blk_meta = jnp.concatenate([blk_exp, blk_slot, blk_nxt, n_used[None], second]).astype(jnp.int32)
    return dest1, dest2, pad_idx, blk_meta, nb, n_slots


def _mixer_out_and_moe(a, w_out, x2d, gt1, g, sc, sh, w_group, b_group, w_router, b_router,
                       w1, w3, w2, layer, seq):
    t, d = x2d.shape
    w_cat = jnp.zeros((ROUTER_LANES, d), jnp.float32)
    w_cat = w_cat.at[:N_EXPERTS].set(w_router.T).at[N_EXPERTS:N_EXPERTS + N_GROUPS].set(w_group.T)
    b_cat = jnp.zeros((ROUTER_LANES, 1), jnp.float32)
    b_cat = b_cat.at[:N_EXPERTS, 0].set(b_router).at[N_EXPERTS:N_EXPERTS + N_GROUPS, 0].set(b_group)
    w_hi, w_lo = _split_hi_lo(w_cat)
    x_new, hn, info, info_t, cnt = _outproj_router(a, w_out, x2d, gt1, g, sc, sh, w_hi, w_lo,
                                                   b_cat, seq)
    dest1, dest2, pad_idx, blk_meta, nb, n_slots = _slot_plan(info_t, cnt, t)
    x_slots = _sc_dispatch(hn, dest1, dest2, pad_idx, n_slots)
    y_slots = _experts(x_slots, blk_meta, w1, w3, w2, layer, nb)
    y_pairs = _sc_gather(y_slots, jnp.concatenate([dest1, dest2], axis=1))
    return x_new, y_pairs, info


def kernel(x, c, w_ada, b_ada, norm1_g, norm2_g, ml_w_in, ml_b_gate, ml_g_out, ml_w_out,
           sw_w_in, sw_g_q, sw_g_k, sw_sinks, sw_w_out, moe_w_group, moe_b_group,
           moe_w_router, moe_b_router, moe_w1, moe_w3, moe_w2):
    bsz, seq, d = x.shape
    depth = w_ada.shape[0]
    bf = jnp.bfloat16
    x2d = x.reshape(bsz * seq, d)
    pending = None
    for layer in range(depth):
        mod = _ada_mod(c, w_ada, b_ada, layer)
        sh1, sc1, gt1, sh2, sc2, gt2 = [
            mod[:, i * d:(i + 1) * d].reshape(bsz, 1, d) for i in range(6)]
        j = layer // 2
        if layer % 2 == 0:
            w = ml_w_in[j]
            q_w, k_w = w[:, :ML_QK], w[:, ML_QK:2 * ML_QK]
            v_w = w[:, 2 * ML_QK:2 * ML_QK + ML_V]
            o_w = w[:, 2 * ML_QK + ML_V:2 * ML_QK + 2 * ML_V]
            g_w = w[:, 2 * ML_QK + 2 * ML_V:]
            w_main = jnp.concatenate([v_w, o_w, q_w], axis=1).astype(bf)
            wg_t = jnp.zeros((ML_GATE_ROWS, d), jnp.float32).at[:2 * ML_HEADS].set(g_w.T)
            wg_hi, wg_lo = _split_hi_lo(wg_t)
            wk_t = k_w.T.astype(bf)
            w_out = ml_w_out[j].astype(bf)
            outs = _inproj(x2d, norm1_g[layer], sc1, sh1, w_main, seq,
                           ml_extra=(wk_t, wg_hi, wg_lo),
                           q_cols=(2 * ML_V, 2 * ML_V + ML_QK), q_scale=ML_DK ** -0.5,
                           gate_cols=(ML_V, 2 * ML_V), pending=pending)
            if pending is not None:
                x2d, outs = outs[0], outs[1:]
            main, k_t, g_t = outs
            a = _mlstm_core(main, k_t, g_t, ml_b_gate[j], ml_g_out[j], bsz, seq)
        else:
            w = sw_w_in[j]
            dq = SW_Q_HEADS * SW_DH
            dkv = SW_KV_HEADS * SW_DH
            dup = lambda m: jnp.concatenate(
                [m.reshape(d, SW_KV_HEADS, 1, SW_DH)] * 2, axis=2).reshape(d, 2 * dkv)
            w_main = jnp.concatenate(
                [w[:, :dq], dup(w[:, dq:dq + dkv]), dup(w[:, dq + dkv:])], axis=1).astype(bf)
            w_out = sw_w_out[j].astype(bf)
            gq = jnp.concatenate([sw_g_q[j], sw_g_q[j]]).reshape(1, LANES) * (SW_DH ** -0.5 * LOG2E)
            gk = jnp.concatenate([sw_g_k[j], sw_g_k[j]]).reshape(1, LANES)
            outs = _inproj(x2d, norm1_g[layer], sc1, sh1, w_main, seq, pending=pending,
                           qk_norm=(dq, dq + 2 * dkv, gq, gk))
            if pending is not None:
                x2d, outs = outs[0], outs[1:]
            a = _swa_core(outs[0], sw_sinks[j], bsz, seq)
        x2d, y_pairs, info = _mixer_out_and_moe(
            a, w_out, x2d, gt1, norm2_g[layer], sc2, sh2, moe_w_group[layer], moe_b_group[layer],
            moe_w_router[layer], moe_b_router[layer], moe_w1, moe_w3, moe_w2, layer, seq)
        pending = (y_pairs, info, gt2)
    y_pairs, info, gt2 = pending
    return _combine(x2d, y_pairs, info, gt2, seq).reshape(bsz, seq, d)
```

```python
import functools

import jax
import jax.numpy as jnp
from jax import lax
from jax.experimental import pallas as pl
from jax.experimental.pallas import tpu as pltpu
from jax.experimental.pallas import tpu_sc as plsc

EPS = 1e-6
GATE_CAP = 15.0
LOG2E = 1.4426950408889634

ML_HEADS = 4
ML_DK = 128
ML_DV = 256
ML_QK = ML_HEADS * ML_DK
ML_V = ML_HEADS * ML_DV
ML_GATE_ROWS = 16

SW_Q_HEADS = 16
SW_KV_HEADS = 4
SW_GROUP = SW_Q_HEADS // SW_KV_HEADS
SW_DH = 64
SW_WINDOW = 128
LANES = 128

N_GROUPS = 8
EXPERTS_PER_GROUP = 8
N_EXPERTS = N_GROUPS * EXPERTS_PER_GROUP
MOE_BLOCK = 256
ROUTER_LANES = 128
SC_WINDOW = 128
SC_ROWS = 64
EXPERT_BLOCKS_PER_STEP = 4
EXPERT_STAGES = 3

SUBLANES = 8
TOKEN_TILE = 512
ROUTER_TILE = 1024
COMBINE_TILE = 1024
PROJ_COL_CHUNK = 1024
ML_CHUNK = 256
ADA_COL_TILE = 768
V7X_VMEM_BYTES = 64 * 1024 * 1024
VMEM_LIMIT = V7X_VMEM_BYTES - 8 * 1024 * 1024

_NT = (((1,), (1,)), ((), ()))


def _bdot(a, b):
    return jnp.dot(a, b, preferred_element_type=jnp.float32)


def _bdot_nt(a, b):
    return lax.dot_general(a, b, _NT, preferred_element_type=jnp.float32)


def _split_hi_lo(a):
    hi = a.astype(jnp.bfloat16)
    lo = (a - hi.astype(jnp.float32)).astype(jnp.bfloat16)
    return hi, lo


def _params(sem):
    return pltpu.CompilerParams(dimension_semantics=sem, vmem_limit_bytes=VMEM_LIMIT)


def _ada_kernel(c_ref, w_ref, b_ref, o_ref):
    c = c_ref[...]
    cond = c * jax.nn.sigmoid(c)
    c_hi, c_lo = _split_hi_lo(cond)
    w_hi, w_lo = _split_hi_lo(w_ref[...])
    acc = _bdot(c_hi, w_hi) + (_bdot(c_lo, w_hi) + _bdot(c_hi, w_lo))
    o_ref[...] = acc + b_ref[...]


def _ada_mod(c, w_ada, b_ada):
    depth, d, n = w_ada.shape
    bsz = c.shape[0]
    rows = SUBLANES
    tn = ADA_COL_TILE
    c_pad = jnp.zeros((rows, d), jnp.float32).at[:bsz].set(c)
    out = pl.pallas_call(
        _ada_kernel,
        grid=(depth, n // tn),
        in_specs=[
            pl.BlockSpec((rows, d), lambda l, j: (0, 0)),
            pl.BlockSpec((None, d, tn), lambda l, j: (l, 0, j)),
            pl.BlockSpec((None, 1, tn), lambda l, j: (l, 0, j)),
        ],
        out_specs=pl.BlockSpec((None, rows, tn), lambda l, j: (l, 0, j)),
        out_shape=jax.ShapeDtypeStruct((depth, rows, n), jnp.float32),
        compiler_params=_params(("arbitrary", "arbitrary")),
        name="ada_mod",
    )(c_pad, w_ada, b_ada.reshape(depth, 1, n))
    return out[:, :bsz]


def _modulated_norm(x, g, sc, sh):
    y = x * lax.rsqrt(jnp.mean(x * x, axis=-1, keepdims=True) + EPS)
    return y * (g * (1.0 + sc)) + sh


def _moe_combined(x, y1_ref, y2_ref, info_ref, gt_ref):
    info = info_ref[...]
    g1 = info[:, 4:5]
    g2 = info[:, 5:6]
    y1_hi, y1_lo = _unpack_bf16_pairs(y1_ref[...])
    y2_hi, y2_lo = _unpack_bf16_pairs(y2_ref[...])
    y = jnp.concatenate([g1 * y1_hi + g2 * y2_hi, g1 * y1_lo + g2 * y2_lo], axis=-1)
    return x + gt_ref[...] * y


def _qk_head_norm(acc, c0, qk_norm, gq, gk):
    q_hi, k_hi = qk_norm
    low = lax.broadcasted_iota(jnp.int32, (acc.shape[0], LANES), 1) < SW_DH
    slabs = []
    for j in range(acc.shape[1] // LANES):
        slab = acc[:, j * LANES:(j + 1) * LANES]
        if c0 + j * LANES < q_hi:
            sq = slab * slab
            ss_lo = jnp.sum(jnp.where(low, sq, 0.0), axis=-1, keepdims=True)
            ss_hi = jnp.sum(jnp.where(low, 0.0, sq), axis=-1, keepdims=True)
            rs = jnp.where(low, lax.rsqrt(ss_lo / SW_DH + EPS), lax.rsqrt(ss_hi / SW_DH + EPS))
            slab = slab * rs * gq
        elif c0 + j * LANES < k_hi:
            slab = slab * lax.rsqrt(jnp.mean(slab * slab, axis=-1, keepdims=True) + EPS) * gk
        slabs.append(slab)
    return jnp.concatenate(slabs, axis=-1)


def _inproj_kernel(*refs, n_main, chunk, q_cols, q_scale, gate_cols, qk_norm, with_ml,
                   with_combine):
    refs = list(refs)
    n_in = (5 + (4 if with_combine else 0) + (2 if with_ml else 0)
            + (2 if qk_norm is not None else 0))
    n_scratch = 4 if with_ml else 2
    ins, outs, scratch = refs[:n_in], refs[n_in:-n_scratch], refs[-n_scratch:]
    x_ref = ins.pop(0)
    if with_combine:
        y1_ref, y2_ref, info_ref, gtp_ref = ins[:4]
        ins = ins[4:]
        xo_ref = outs.pop(0)
    g_ref, sc_ref, sh_ref, w_ref = ins[:4]
    o_ref = outs[0]
    if with_ml:
        wk_ref, wgh_ref = ins[4:]
        kt_ref, gt_ref = outs[1:]
    if qk_norm is not None:
        gq_ref, gk_ref = ins[4:]

    def normalise(hb_dst, lo_dst):
        x = x_ref[...]
        if with_combine:
            x = _moe_combined(x, y1_ref, y2_ref, info_ref, gtp_ref)
            xo_ref[...] = x
        hn = _modulated_norm(x, g_ref[...], sc_ref[...], sh_ref[...])
        hb = hn.astype(jnp.bfloat16)
        hb_dst[...] = hb
        if with_ml:
            lo_dst[...] = (hn - hb.astype(jnp.float32)).astype(jnp.bfloat16)

    def project(hb_src, lo_src):
        hb = hb_src[...]
        for c0 in range(0, n_main, chunk):
            c1 = min(c0 + chunk, n_main)
            acc = _bdot(hb, w_ref[:, c0:c1])
            if q_cols is not None and q_cols[0] <= c0 < q_cols[1]:
                acc = acc * q_scale
            if gate_cols is not None and gate_cols[0] <= c0 < gate_cols[1]:
                acc = jax.nn.sigmoid(acc)
            if qk_norm is not None and c0 < qk_norm[1]:
                acc = _qk_head_norm(acc, c0, qk_norm, gq_ref[...], gk_ref[...])
            o_ref[:, c0:c1] = acc.astype(o_ref.dtype)
        if with_ml:
            nk = kt_ref.shape[0]
            stacked = _bdot_nt(wk_ref[...], hb)
            kt_ref[...] = stacked[:nk].astype(kt_ref.dtype)
            gt_ref[...] = (stacked[nk:nk + ML_GATE_ROWS]
                           + (_bdot_nt(wgh_ref[...], lo_src[...]) + stacked[nk + ML_GATE_ROWS:]))

    hb_a, hb_b = scratch[:2]
    lo_a, lo_b = scratch[2:] if with_ml else (None, None)
    s = pl.program_id(0)

    @pl.when(s == 0)
    def _():
        hb_b[...] = jnp.zeros_like(hb_b)
        if with_ml:
            lo_b[...] = jnp.zeros_like(lo_b)

    @pl.when(s % 2 == 0)
    def _():
        normalise(hb_a, lo_a)
        project(hb_b, lo_b)

    @pl.when(s % 2 == 1)
    def _():
        normalise(hb_b, lo_b)
        project(hb_a, lo_a)


def _inproj(x2d, g, sc, sh, w_main, seq, *, ml_extra=None, q_cols=None, q_scale=1.0,
            gate_cols=None, qk_norm=None, pending=None):
    t, d = x2d.shape
    tm = TOKEN_TILE if ml_extra is not None else min(2 * TOKEN_TILE, seq)
    n_main = w_main.shape[1]
    tiles_per_batch = seq // tm
    n_tiles = t // tm
    resident = pl.Buffered(1)
    norm_tile = lambda s: jnp.minimum(s, n_tiles - 1)
    proj_tile = lambda s: jnp.maximum(s - 1, 0)
    row = lambda s: (norm_tile(s), 0)
    per_batch = lambda s: (norm_tile(s) // tiles_per_batch, 0, 0)
    const = lambda s: (0, 0)
    in_specs = [pl.BlockSpec((tm, d), row)]
    args = [x2d]
    out_specs, out_shape = [], []
    if pending is not None:
        y_pairs, info, gt_prev = pending
        in_specs += [pl.BlockSpec((tm, d // 2), row),
                     pl.BlockSpec((tm, d // 2), lambda s: (norm_tile(s) + n_tiles, 0)),
                     pl.BlockSpec((tm, ROUTER_LANES), row),
                     pl.BlockSpec((None, 1, d), per_batch)]
        args += [y_pairs, y_pairs, info, gt_prev]
        out_specs += [pl.BlockSpec((tm, d), row)]
        out_shape += [jax.ShapeDtypeStruct((t, d), jnp.float32)]
    in_specs += [
        pl.BlockSpec((1, d), const),
        pl.BlockSpec((None, 1, d), per_batch),
        pl.BlockSpec((None, 1, d), per_batch),
        pl.BlockSpec((d, n_main), const, pipeline_mode=resident),
    ]
    args += [g.reshape(1, d), sc, sh, w_main]
    out_specs += [pl.BlockSpec((tm, n_main), lambda s: (proj_tile(s), 0))]
    out_shape += [jax.ShapeDtypeStruct((t, n_main), jnp.bfloat16)]
    scratch = [pltpu.VMEM((tm, d), jnp.bfloat16), pltpu.VMEM((tm, d), jnp.bfloat16)]
    if ml_extra is not None:
        wk_t, wg_hi, wg_lo = ml_extra
        nk = wk_t.shape[0]
        stacked = jnp.concatenate([wk_t, wg_hi, wg_lo], axis=0)
        in_specs += [pl.BlockSpec(stacked.shape, const, pipeline_mode=resident),
                     pl.BlockSpec(wg_hi.shape, const, pipeline_mode=resident)]
        args += [stacked, wg_hi]
        out_specs += [pl.BlockSpec((nk, tm), lambda s: (0, proj_tile(s))),
                      pl.BlockSpec((ML_GATE_ROWS, tm), lambda s: (0, proj_tile(s)))]
        out_shape += [jax.ShapeDtypeStruct((nk, t), jnp.bfloat16),
                      jax.ShapeDtypeStruct((ML_GATE_ROWS, t), jnp.float32)]
        scratch += [pltpu.VMEM((tm, d), jnp.bfloat16), pltpu.VMEM((tm, d), jnp.bfloat16)]
    if qk_norm is not None:
        in_specs += [pl.BlockSpec((1, LANES), const), pl.BlockSpec((1, LANES), const)]
        args += [qk_norm[2], qk_norm[3]]
    kern = functools.partial(_inproj_kernel, n_main=n_main, chunk=PROJ_COL_CHUNK, q_cols=q_cols,
                             q_scale=q_scale, gate_cols=gate_cols,
                             qk_norm=None if qk_norm is None else qk_norm[:2],
                             with_ml=ml_extra is not None, with_combine=pending is not None)
    return pl.pallas_call(
        kern,
        grid=(n_tiles + 1,),
        in_specs=in_specs,
        out_specs=out_specs,
        out_shape=out_shape,
        scratch_shapes=scratch,
        compiler_params=_params(("arbitrary",)),
        name="inproj_ml" if ml_extra is not None else "inproj_sw",
    )(*args)


def _mlstm_gate_terms(graw, bias, upper):
    H = ML_HEADS
    L = graw.shape[1]
    z = graw + bias
    gates = GATE_CAP * jnp.tanh(z / GATE_CAP)
    log_f = jnp.minimum(gates, 0.0) - jnp.log1p(jnp.exp(-jnp.abs(gates)))
    row = lax.broadcasted_iota(jnp.int32, (ML_GATE_ROWS, L), 0)
    lane = lax.broadcasted_iota(jnp.int32, (ML_GATE_ROWS, L), 1)
    is_i = row < H
    slab = jnp.where(is_i, gates, log_f)
    a1 = slab.astype(jnp.bfloat16)
    r1 = slab - a1.astype(jnp.float32)
    a2 = r1.astype(jnp.bfloat16)
    a3 = (r1 - a2.astype(jnp.float32)).astype(jnp.bfloat16)
    cum = _bdot(a1, upper) + (_bdot(a2, upper) + _bdot(a3, upper))
    ib = jnp.where(is_i, gates, cum) * LOG2E
    b = pltpu.roll(ib, ML_GATE_ROWS - H, 0)
    u = ib - b
    cm = u
    shift = 1
    while shift < L:
        cm = jnp.maximum(cm, jnp.where(lane >= shift, pltpu.roll(cm, shift, 1), -jnp.inf))
        shift *= 2
    return b, u, cm


def _mlstm_kernel(v_ref, o_ref, q_ref, kt_ref, gt_ref, gtn_ref, bg_ref, gout_ref, out_ref,
                  c_ref, m_ref, b_ref, u_ref, cm_ref, *, chunk):
    L = chunk
    H, dk, dv = ML_HEADS, ML_DK, ML_DV
    r_idx = lax.broadcasted_iota(jnp.int32, (L, L), 0)
    c_idx = lax.broadcasted_iota(jnp.int32, (L, L), 1)
    upper = jnp.where(r_idx <= c_idx, 1.0, 0.0).astype(jnp.bfloat16)
    causal = r_idx >= c_idx
    row = lax.broadcasted_iota(jnp.int32, (ML_GATE_ROWS, L), 0)
    ones_col = jnp.where(lax.broadcasted_iota(jnp.int32, (L, LANES), 1) == 0, 1.0, 0.0
                         ).astype(jnp.bfloat16)

    @pl.when(pl.program_id(1) == 0)
    def _():
        c_ref[...] = jnp.zeros_like(c_ref)
        m_ref[...] = jnp.zeros_like(m_ref)
        b0, u0, cm0 = _mlstm_gate_terms(gt_ref[...], bg_ref[...], upper)
        b_ref[...] = b0
        u_ref[...] = u0
        cm_ref[...] = cm0

    b16 = b_ref[...]
    u16 = u_ref[...]
    cm16 = cm_ref[...]
    b_n, u_n, cm_n = _mlstm_gate_terms(gtn_ref[...], bg_ref[...], upper)
    b_ref[...] = b_n
    u_ref[...] = u_n
    cm_ref[...] = cm_n

    m_prev = m_ref[:, 0:1]
    z16 = jnp.maximum(m_prev, cm16)
    w_inter16 = jnp.exp2(m_prev - z16)
    e_negm16 = jnp.exp2(-(b16 + z16))
    z_last = z16[:, L - 1:L]
    w_state16 = jnp.exp2(u16 - z_last)
    decay16 = jnp.exp2(m_prev - z_last)
    m_ref[...] = jnp.broadcast_to(b16[:, L - 1:L] + z_last, m_ref.shape)
    stacked = jnp.where(row < H, z16,
                        jnp.where(row < 2 * H, pltpu.roll(w_inter16, H, 0),
                                  pltpu.roll(e_negm16, 2 * H, 0)))
    cols = jnp.concatenate(
        [stacked, jnp.zeros((LANES - ML_GATE_ROWS, L), jnp.float32)], axis=0).T

    for h in range(H):
        u_r = u16[h:h + 1, :]
        z_c = cols[:, h:h + 1]
        w_inter = cols[:, H + h:H + h + 1]
        e_negm = cols[:, 2 * H + h:2 * H + h + 1]
        c_ext = c_ref[h]
        q = q_ref[:, h * dk:(h + 1) * dk]
        kt = kt_ref[h * dk:(h + 1) * dk, :]
        v_ext = jnp.concatenate([v_ref[:, h * dv:(h + 1) * dv], ones_col], axis=-1)

        w_intra = jnp.exp2(jnp.where(causal, u_r - z_c, -jnp.inf))
        s = (_bdot(q, kt) * w_intra).astype(jnp.bfloat16)
        nd = w_inter * _bdot(q, c_ext.astype(jnp.bfloat16)) + _bdot(s, v_ext)
        den = nd[:, dv:dv + 1]
        hb = nd[:, :dv] * (1.0 / jnp.maximum(jnp.abs(den), e_negm))

        kw = (kt.astype(jnp.float32) * w_state16[h:h + 1, :]).astype(jnp.bfloat16)
        c_ref[h] = decay16[h:h + 1, :] * c_ext + _bdot(kw, v_ext)

        y = hb * lax.rsqrt(jnp.mean(hb * hb, axis=-1, keepdims=True) + EPS)
        y = y * gout_ref[:, h * dv:(h + 1) * dv]
        og = o_ref[:, h * dv:(h + 1) * dv].astype(jnp.float32)
        out_ref[:, h * dv:(h + 1) * dv] = (y * og).astype(out_ref.dtype)


def _mlstm_core(main, k_t, g_t, b_gate, g_out, bsz, seq):
    chunk = ML_CHUNK
    t = main.shape[0]
    nc = seq // chunk
    blk = lambda b, c: b * nc + c
    bg = jnp.zeros((ML_GATE_ROWS, 1), jnp.float32).at[:2 * ML_HEADS, 0].set(b_gate)
    return pl.pallas_call(
        functools.partial(_mlstm_kernel, chunk=chunk),
        grid=(bsz, nc),
        in_specs=[
            pl.BlockSpec((chunk, ML_V), lambda b, c: (blk(b, c), 0)),
            pl.BlockSpec((chunk, ML_V), lambda b, c: (blk(b, c), 1)),
            pl.BlockSpec((chunk, ML_QK), lambda b, c: (blk(b, c), 4)),
            pl.BlockSpec((ML_QK, chunk), lambda b, c: (0, blk(b, c))),
            pl.BlockSpec((ML_GATE_ROWS, chunk), lambda b, c: (0, blk(b, c))),
            pl.BlockSpec((ML_GATE_ROWS, chunk), lambda b, c: (0, blk(b, jnp.minimum(c + 1, nc - 1)))),
            pl.BlockSpec((ML_GATE_ROWS, 1), lambda b, c: (0, 0)),
            pl.BlockSpec((1, ML_V), lambda b, c: (0, 0)),
        ],
        out_specs=pl.BlockSpec((chunk, ML_V), lambda b, c: (blk(b, c), 0)),
        out_shape=jax.ShapeDtypeStruct((t, ML_V), jnp.bfloat16),
        scratch_shapes=[
            pltpu.VMEM((ML_HEADS, ML_DK, ML_DV + LANES), jnp.float32),
            pltpu.VMEM((ML_GATE_ROWS, LANES), jnp.float32),
            pltpu.VMEM((ML_GATE_ROWS, chunk), jnp.float32),
            pltpu.VMEM((ML_GATE_ROWS, chunk), jnp.float32),
            pltpu.VMEM((ML_GATE_ROWS, chunk), jnp.float32),
        ],
        compiler_params=_params(("arbitrary", "arbitrary")),
        name="mlstm_core",
    )(main, main, main, k_t, g_t, g_t, bg, g_out.reshape(1, ML_V))


def _swa_kernel(q_ref, kc_ref, kp_ref, vc_ref, vp_ref, bias_ref, o_ref):
    W = SW_WINDOW
    lane = lax.broadcasted_iota(jnp.int32, (W, LANES), 1)
    low = lane < SW_DH
    not_first_row = lax.broadcasted_iota(jnp.int32, (W, LANES), 0) > 0
    variant = jnp.minimum(pl.program_id(1), 1)
    for g in range(SW_KV_HEADS):
        sl = slice(g * LANES, (g + 1) * LANES)
        k_prev = jnp.where(not_first_row, kp_ref[:, sl].astype(jnp.float32), 0.0)
        v_prev = jnp.where(not_first_row, vp_ref[:, sl].astype(jnp.float32), 0.0)
        kn = jnp.concatenate([k_prev.astype(jnp.bfloat16), kc_ref[:, sl]], axis=0)
        v2 = jnp.concatenate([v_prev.astype(jnp.bfloat16), vc_ref[:, sl]], axis=0)
        parts = []
        for p in range(2):
            c0 = g * SW_GROUP * SW_DH + p * LANES
            qp = q_ref[:, c0:c0 + LANES].astype(jnp.float32)
            parts.append(jnp.where(low, qp, 0.0).astype(jnp.bfloat16))
            parts.append(jnp.where(low, 0.0, qp).astype(jnp.bfloat16))
        q4 = jnp.concatenate(parts, axis=0)
        scores = _bdot_nt(q4, kn) + bias_ref[variant, g]
        m = jnp.max(scores, axis=-1, keepdims=True)
        pexp = jnp.exp2(scores - m)
        denom = jnp.sum(pexp, axis=-1, keepdims=True)
        o4 = _bdot(pexp.astype(jnp.bfloat16), v2) * (1.0 / denom)
        for p in range(2):
            oa = o4[(2 * p) * W:(2 * p + 1) * W]
            ob = o4[(2 * p + 1) * W:(2 * p + 2) * W]
            c0 = g * SW_GROUP * SW_DH + p * LANES
            o_ref[:, c0:c0 + LANES] = jnp.where(low, oa, ob).astype(o_ref.dtype)


def _swa_core(proj, sinks, bsz, seq):
    t = proj.shape[0]
    W = SW_WINDOW
    nb = seq // W
    dq = SW_Q_HEADS * SW_DH
    kv_w = SW_KV_HEADS * LANES
    k_blk = dq // kv_w
    v_blk = k_blk + 1
    cur = lambda b, n: b * nb + n
    prev = lambda b, n: b * nb + jnp.maximum(n - 1, 0)
    qi = (jnp.arange(SW_GROUP * W) % W)[:, None]
    ki = jnp.arange(2 * W)[None, :]
    rel = qi + W - ki
    in_win = (rel >= 0) & (rel < W)
    window = jnp.stack([jnp.where(in_win & (ki >= W), 0.0, -jnp.inf),
                        jnp.where(in_win, 0.0, -jnp.inf)]).astype(jnp.float32)
    sink_rows = jnp.repeat(sinks.astype(jnp.float32).reshape(SW_KV_HEADS, SW_GROUP) * LOG2E, W, axis=1)
    bias = jnp.where(ki[None, None] == 0, sink_rows[None, :, :, None], window[:, None])
    return pl.pallas_call(
        _swa_kernel,
        grid=(bsz, nb),
        in_specs=[
            pl.BlockSpec((W, dq), lambda b, n: (cur(b, n), 0)),
            pl.BlockSpec((W, kv_w), lambda b, n: (cur(b, n), k_blk)),
            pl.BlockSpec((W, kv_w), lambda b, n: (prev(b, n), k_blk)),
            pl.BlockSpec((W, kv_w), lambda b, n: (cur(b, n), v_blk)),
            pl.BlockSpec((W, kv_w), lambda b, n: (prev(b, n), v_blk)),
            pl.BlockSpec(bias.shape, lambda b, n: (0, 0, 0, 0)),
        ],
        out_specs=pl.BlockSpec((W, dq), lambda b, n: (cur(b, n), 0)),
        out_shape=jax.ShapeDtypeStruct((t, dq), jnp.bfloat16),
        compiler_params=_params(("arbitrary", "arbitrary")),
        name="swa_core",
    )(proj, proj, proj, proj, proj, bias)


def _pack_rounded_pairs(r):
    k = r.shape[1] // 2
    hi = lax.bitcast_convert_type(r[:, :k], jnp.uint32)
    lo = lax.bitcast_convert_type(r[:, k:], jnp.uint32)
    return hi | (lo >> 16)


def _pack_bf16_pairs(a):
    return _pack_rounded_pairs(a.astype(jnp.bfloat16).astype(jnp.float32))


def _unpack_bf16_pairs(u):
    hi = lax.bitcast_convert_type(u & jnp.uint32(0xFFFF0000), jnp.float32)
    lo = lax.bitcast_convert_type(u << 16, jnp.float32)
    return hi, lo


def _route_tile(x_new, g_ref, sc_ref, sh_ref, wh_ref, wl_ref, b_ref,
                hn_ref, info_ref, infot_ref, cnt_ref, carry_ref, earlier_ref, tm):
    hn = _modulated_norm(x_new, g_ref[...], sc_ref[...], sh_ref[...])
    h_hi = hn.astype(jnp.bfloat16)
    hi_f32 = h_hi.astype(jnp.float32)
    h_lo = (hn - hi_f32).astype(jnp.bfloat16)
    hn_ref[...] = _pack_rounded_pairs(hi_f32)
    wide = _bdot_nt(wl_ref[...], h_hi)
    logits = (wide[:ROUTER_LANES] + (_bdot_nt(wh_ref[...], h_lo) + wide[ROUTER_LANES:])
              + b_ref[...])
    E8 = EXPERTS_PER_GROUP
    sub = lax.broadcasted_iota(jnp.int32, (E8, tm), 0).astype(jnp.float32)
    big = float(ROUTER_LANES)
    neg = -jnp.inf

    gl = logits[N_EXPERTS:N_EXPERTS + N_GROUPS]
    gmax = jnp.max(gl, axis=0, keepdims=True)
    gsel = jnp.min(jnp.where(gl == gmax, sub, big), axis=0, keepdims=True)
    p_grp = 1.0 / jnp.sum(jnp.exp(gl - gmax), axis=0, keepdims=True)

    el = logits[0:E8]
    for grp in range(1, N_GROUPS):
        el = jnp.where(gsel == grp, logits[grp * E8:(grp + 1) * E8], el)
    v1 = jnp.max(el, axis=0, keepdims=True)
    j1 = jnp.min(jnp.where(el == v1, sub, big), axis=0, keepdims=True)
    el2 = jnp.where(sub == j1, neg, el)
    v2 = jnp.max(el2, axis=0, keepdims=True)
    j2 = jnp.min(jnp.where(el2 == v2, sub, big), axis=0, keepdims=True)
    i1 = gsel * E8 + j1
    i2 = gsel * E8 + j2
    e21 = jnp.exp(v2 - v1)
    gate1 = p_grp / (1.0 + e21)
    gate2 = p_grp * e21 / (1.0 + e21)

    erow = lax.broadcasted_iota(jnp.int32, (N_EXPERTS, tm), 0).astype(jnp.float32)
    hit1 = erow == i1
    hit2 = erow == i2
    onehot = jnp.where(hit1 | hit2, 1.0, 0.0)
    carry = carry_ref[:, 0:1]
    before = _bdot(onehot.astype(jnp.bfloat16), earlier_ref[...]) + carry
    rank1 = jnp.sum(jnp.where(hit1, before, 0.0), axis=0, keepdims=True)
    rank2 = jnp.sum(jnp.where(hit2, before, 0.0), axis=0, keepdims=True)
    total = carry + jnp.sum(onehot, axis=1, keepdims=True)
    carry_ref[...] = jnp.broadcast_to(total, carry_ref.shape)
    cnt_ref[...] = jnp.broadcast_to(total, cnt_ref.shape)

    info_t = jnp.where(sub == 0, i1, 0.0)
    info_t = jnp.where(sub == 1, i2, info_t)
    info_t = jnp.where(sub == 2, rank1, info_t)
    info_t = jnp.where(sub == 3, rank2, info_t)
    info_t = jnp.where(sub == 4, gate1, info_t)
    info_t = jnp.where(sub == 5, gate2, info_t)
    infot_ref[...] = info_t
    info_ref[...] = jnp.concatenate(
        [info_t, jnp.zeros((ROUTER_LANES - E8, tm), jnp.float32)], axis=0).T


def _router_kernel(a_ref, wo_ref, x_ref, gt_ref, g_ref, sc_ref, sh_ref, wh_ref, wl_ref, b_ref,
                   xo_ref, hn_ref, info_ref, infot_ref, cnt_ref, carry_ref, earlier_ref, *, tm):
    @pl.when(pl.program_id(0) == 0)
    def _():
        carry_ref[...] = jnp.zeros_like(carry_ref)
        r_idx = lax.broadcasted_iota(jnp.int32, (tm, tm), 0)
        c_idx = lax.broadcasted_iota(jnp.int32, (tm, tm), 1)
        earlier_ref[...] = jnp.where(r_idx < c_idx, 1.0, 0.0).astype(jnp.bfloat16)

    x_new = x_ref[...] + gt_ref[...] * _bdot(a_ref[...], wo_ref[...])
    xo_ref[...] = x_new
    _route_tile(x_new, g_ref, sc_ref, sh_ref, wh_ref, wl_ref, b_ref,
                hn_ref, info_ref, infot_ref, cnt_ref, carry_ref, earlier_ref, tm)


def _outproj_router(a, w_out, x2d, gt, g, sc, sh, w_hi, w_lo, bias, seq):
    t, d = x2d.shape
    tm = min(ROUTER_TILE, seq)
    tiles_per_batch = seq // tm
    per_batch = lambda i: (i // tiles_per_batch, 0, 0)
    const = lambda i: (0, 0)
    return pl.pallas_call(
        functools.partial(_router_kernel, tm=tm),
        grid=(t // tm,),
        in_specs=[
            pl.BlockSpec((tm, a.shape[1]), lambda i: (i, 0)),
            pl.BlockSpec(w_out.shape, const),
            pl.BlockSpec((tm, d), lambda i: (i, 0)),
            pl.BlockSpec((None, 1, d), per_batch),
            pl.BlockSpec((1, d), const),
            pl.BlockSpec((None, 1, d), per_batch),
            pl.BlockSpec((None, 1, d), per_batch),
            pl.BlockSpec((ROUTER_LANES, d), const),
            pl.BlockSpec((2 * ROUTER_LANES, d), const),
            pl.BlockSpec((ROUTER_LANES, 1), const),
        ],
        out_specs=[
            pl.BlockSpec((tm, d), lambda i: (i, 0)),
            pl.BlockSpec((tm, d // 2), lambda i: (i, 0)),
            pl.BlockSpec((tm, ROUTER_LANES), lambda i: (i, 0)),
            pl.BlockSpec((EXPERTS_PER_GROUP, tm), lambda i: (0, i)),
            pl.BlockSpec((N_EXPERTS, LANES), const),
        ],
        out_shape=[
            jax.ShapeDtypeStruct((t, d), jnp.float32),
            jax.ShapeDtypeStruct((t, d // 2), jnp.uint32),
            jax.ShapeDtypeStruct((t, ROUTER_LANES), jnp.float32),
            jax.ShapeDtypeStruct((EXPERTS_PER_GROUP, t), jnp.float32),
            jax.ShapeDtypeStruct((N_EXPERTS, LANES), jnp.float32),
        ],
        scratch_shapes=[pltpu.VMEM((N_EXPERTS, LANES), jnp.float32),
                        pltpu.VMEM((tm, tm), jnp.bfloat16)],
        compiler_params=_params(("arbitrary",)),
        name="outproj_router",
    )(a, w_out, x2d, gt, g.reshape(1, d), sc, sh, w_hi, jnp.concatenate([w_hi, w_lo], axis=0), bias)


def _sc_mesh():
    return plsc.VectorSubcoreMesh(core_axis_name="c", subcore_axis_name="s")


def _sc_dispatch(rows, d0, d1, pad_idx, n_slots):
    t, w = rows.shape
    n_pad = pad_idx.shape[1]
    zeros = jnp.zeros((SC_ROWS, w), rows.dtype)
    sem = (pltpu.PARALLEL, pltpu.ARBITRARY)
    parts = SC_WINDOW // SC_ROWS

    @pl.kernel(out_type=jax.ShapeDtypeStruct((n_slots + SC_WINDOW, w), rows.dtype), mesh=_sc_mesh(),
               scratch_types=[pltpu.SemaphoreType.DMA, pltpu.SemaphoreType.DMA])
    def dispatch(x_hbm, d0_hbm, d1_hbm, z_hbm, p_hbm, o_hbm, sem0, sem1):
        def scatter_rows(x_vmem, i0_vmem, i1_vmem):
            part = pl.ds(pl.program_id(1) * SC_ROWS, SC_ROWS)
            first = pltpu.async_copy(x_vmem, o_hbm.at[i0_vmem.at[0, part]], sem0)
            second = pltpu.async_copy(x_vmem, o_hbm.at[i1_vmem.at[0, part]], sem1)
            first.wait()
            second.wait()

        pltpu.emit_pipeline(
            scatter_rows,
            grid=(t // SC_WINDOW, parts),
            in_specs=[pl.BlockSpec((SC_ROWS, w), lambda i, j: (parts * i + j, 0)),
                      pl.BlockSpec((1, SC_WINDOW), lambda i, j: (0, i)),
                      pl.BlockSpec((1, SC_WINDOW), lambda i, j: (0, i))],
            out_specs=[],
            core_axis_name=("c", "s"),
            dimension_semantics=sem,
        )(x_hbm, d0_hbm, d1_hbm)

        def scatter_zeros(z_vmem, p_vmem):
            part = pl.ds(pl.program_id(1) * SC_ROWS, SC_ROWS)
            pltpu.sync_copy(z_vmem, o_hbm.at[p_vmem.at[0, part]])

        pltpu.emit_pipeline(
            scatter_zeros,
            grid=(n_pad // SC_WINDOW, parts),
            in_specs=[pl.BlockSpec((SC_ROWS, w), lambda i, j: (0, 0)),
                      pl.BlockSpec((1, SC_WINDOW), lambda i, j: (0, i))],
            out_specs=[],
            core_axis_name=("c", "s"),
            dimension_semantics=sem,
        )(z_hbm, p_hbm)

    return dispatch(rows, d0, d1, zeros, pad_idx)


def _sc_gather(src, idx):
    n_out = idx.shape[1]
    w = src.shape[1]
    parts = SC_WINDOW // SC_ROWS

    @pl.kernel(out_type=jax.ShapeDtypeStruct((n_out, w), src.dtype), mesh=_sc_mesh())
    def gather(x_hbm, i_hbm, o_hbm):
        def gather_rows(i_vmem, o_vmem):
            part = pl.ds(pl.program_id(1) * SC_ROWS, SC_ROWS)
            pltpu.sync_copy(x_hbm.at[i_vmem.at[0, part]], o_vmem)

        pltpu.emit_pipeline(
            gather_rows,
            grid=(n_out // SC_WINDOW, parts),
            in_specs=[pl.BlockSpec((1, SC_WINDOW), lambda i, j: (0, i))],
            out_specs=[pl.BlockSpec((SC_ROWS, w), lambda i, j: (parts * i + j, 0))],
            core_axis_name=("c", "s"),
            dimension_semantics=(pltpu.PARALLEL, pltpu.ARBITRARY),
        )(i_hbm, o_hbm)

    return gather(src, idx)


def _expert_kernel(meta_ref, x_ref, w1_hbm, w3_hbm, w2_hbm, y_ref,
                   w1_buf, w3_buf, w2_buf, w1_c, w3_c, w2_c, sems, *, layer, nb):
    def weight_copies(expert, s):
        return (pltpu.make_async_copy(w1_hbm.at[layer, expert], w1_buf.at[s], sems.at[s, 0]),
                pltpu.make_async_copy(w3_hbm.at[layer, expert], w3_buf.at[s], sems.at[s, 1]),
                pltpu.make_async_copy(w2_hbm.at[layer, expert], w2_buf.at[s], sems.at[s, 2]))

    @pl.when(pl.program_id(0) == 0)
    def _():
        for cp in weight_copies(meta_ref[0], 0):
            cp.start()
        second = meta_ref[3 * nb + 1]

        @pl.when(second >= 0)
        def _():
            for cp in weight_copies(second, 1):
                cp.start()

    for j in range(EXPERT_BLOCKS_PER_STEP):
        i = pl.program_id(0) * EXPERT_BLOCKS_PER_STEP + j
        rows = slice(j * MOE_BLOCK, (j + 1) * MOE_BLOCK)
        e = meta_ref[i]
        slot = meta_ref[nb + i]
        nxt = meta_ref[2 * nb + i]
        used = i < meta_ref[3 * nb]
        first = used & ((i == 0) | (e != meta_ref[jnp.maximum(i - 1, 0)]))

        @pl.when(first)
        def _():
            for cp in weight_copies(e, slot):
                cp.wait()

            @pl.when(nxt >= 0)
            def _():
                for cp in weight_copies(nxt, (slot + EXPERT_STAGES - 1) % EXPERT_STAGES):
                    cp.start()

            w1_c[...] = w1_buf[slot].astype(jnp.bfloat16)
            w3_c[...] = w3_buf[slot].astype(jnp.bfloat16)
            w2_c[...] = w2_buf[slot].astype(jnp.bfloat16)

        @pl.when(used)
        def _():
            x_hi, x_lo = _unpack_bf16_pairs(x_ref[rows, :])
            xb = jnp.concatenate([x_hi, x_lo], axis=-1).astype(jnp.bfloat16)
            full = 256
            tail = jnp.concatenate([w1_c[:, full:], w3_c[:, full:]], axis=-1)
            half_rows = MOE_BLOCK // 2
            h_tail = jnp.concatenate([_bdot(xb[:half_rows], tail), _bdot(xb[half_rows:], tail)], axis=0)
            h1 = jnp.concatenate([_bdot(xb, w1_c[:, :full]), h_tail[:, :LANES]], axis=-1)
            h3 = jnp.concatenate([_bdot(xb, w3_c[:, :full]), h_tail[:, LANES:]], axis=-1)
            act = (h1 * jax.nn.sigmoid(h1) * h3).astype(jnp.bfloat16)
            y_ref[rows, :] = _pack_bf16_pairs(_bdot(act, w2_c[...]))

        @pl.when(jnp.logical_not(used))
        def _():
            y_ref[rows, :] = jnp.zeros((MOE_BLOCK, y_ref.shape[1]), y_ref.dtype)


def _experts(x_slots, blk_meta, w1, w3, w2, layer, nb):
    dp = x_slots.shape[1]
    d, de = w1.shape[-2:]
    step_rows = EXPERT_BLOCKS_PER_STEP * MOE_BLOCK
    assert nb % EXPERT_BLOCKS_PER_STEP == 0
    last_used_step = lambda s: (s[3 * nb] - 1) // EXPERT_BLOCKS_PER_STEP
    grid_spec = pltpu.PrefetchScalarGridSpec(
        num_scalar_prefetch=1,
        grid=(nb // EXPERT_BLOCKS_PER_STEP,),
        in_specs=[
            pl.BlockSpec((step_rows, dp), lambda i, s: (jnp.minimum(i, last_used_step(s)), 0)),
            pl.BlockSpec(memory_space=pl.ANY),
            pl.BlockSpec(memory_space=pl.ANY),
            pl.BlockSpec(memory_space=pl.ANY),
        ],
        out_specs=pl.BlockSpec((step_rows, dp), lambda i, s: (i, 0)),
        scratch_shapes=[
            pltpu.VMEM((EXPERT_STAGES, d, de), jnp.float32),
            pltpu.VMEM((EXPERT_STAGES, d, de), jnp.float32),
            pltpu.VMEM((EXPERT_STAGES, de, d), jnp.float32),
            pltpu.VMEM((d, de), jnp.bfloat16),
            pltpu.VMEM((d, de), jnp.bfloat16),
            pltpu.VMEM((de, d), jnp.bfloat16),
            pltpu.SemaphoreType.DMA((EXPERT_STAGES, 3)),
        ],
    )
    return pl.pallas_call(
        functools.partial(_expert_kernel, layer=layer, nb=nb),
        grid_spec=grid_spec,
        out_shape=jax.ShapeDtypeStruct((nb * MOE_BLOCK, dp), jnp.uint32),
        compiler_params=_params(("arbitrary",)),
        name="moe_experts",
    )(blk_meta, x_slots, w1, w3, w2)


def _combine_kernel(x_ref, y1_ref, y2_ref, info_ref, gt_ref, o_ref):
    o_ref[...] = _moe_combined(x_ref[...], y1_ref, y2_ref, info_ref, gt_ref)


def _combine(x2d, y_pairs, info, gt, seq):
    t, d = x2d.shape
    tm = min(COMBINE_TILE, seq)
    tiles_per_batch = seq // tm
    second = t // tm
    return pl.pallas_call(
        _combine_kernel,
        grid=(t // tm,),
        in_specs=[
            pl.BlockSpec((tm, d), lambda i: (i, 0)),
            pl.BlockSpec((tm, d // 2), lambda i: (i, 0)),
            pl.BlockSpec((tm, d // 2), lambda i: (i + second, 0)),
            pl.BlockSpec((tm, ROUTER_LANES), lambda i: (i, 0)),
            pl.BlockSpec((None, 1, d), lambda i: (i // tiles_per_batch, 0, 0)),
        ],
        out_specs=pl.BlockSpec((tm, d), lambda i: (i, 0)),
        out_shape=jax.ShapeDtypeStruct((t, d), jnp.float32),
        compiler_params=_params(("arbitrary",)),
        name="moe_combine",
    )(x2d, y_pairs, y_pairs, info, gt)


def _slot_plan(info_t, cnt, t):
    counts = cnt[:, 0].astype(jnp.int32)
    padded = (counts + MOE_BLOCK - 1) // MOE_BLOCK * MOE_BLOCK
    pad_ends = jnp.cumsum(padded)
    pad_starts = pad_ends - padded
    nb = -(-(2 * t) // MOE_BLOCK) + N_EXPERTS
    n_slots = nb * MOE_BLOCK
    it = info_t.astype(jnp.int32)
    onehot_start = lambda e: jnp.sum(
        jnp.where(e[None, :] == jnp.arange(N_EXPERTS, dtype=jnp.int32)[:, None],
                  pad_starts[:, None], 0), axis=0)
    dest1 = (onehot_start(it[0]) + it[2]).reshape(1, t)
    dest2 = (onehot_start(it[1]) + it[3]).reshape(1, t)
    lane = jnp.arange(MOE_BLOCK, dtype=jnp.int32)[None, :]
    n_padding = (padded - counts)[:, None]
    wrapped = (pad_starts + counts)[:, None] + lane % jnp.maximum(n_padding, 1)
    pad_idx = jnp.where(n_padding > 0, wrapped, n_slots + lane % SC_WINDOW).reshape(-1)
    n_real = pad_ends[-1] // MOE_BLOCK
    n_used = -(-n_real // EXPERT_BLOCKS_PER_STEP) * EXPERT_BLOCKS_PER_STEP
    tail = jnp.arange((EXPERT_BLOCKS_PER_STEP - 1) * MOE_BLOCK, dtype=jnp.int32)
    tail_idx = jnp.where(tail < (n_used - n_real) * MOE_BLOCK, pad_ends[-1] + tail,
                         n_slots + tail % SC_WINDOW)
    pad_idx = jnp.concatenate([pad_idx, tail_idx]).reshape(1, -1)
    experts = jnp.arange(N_EXPERTS, dtype=jnp.int32)
    blk = jnp.arange(nb, dtype=jnp.int32)
    blk_exp = jnp.minimum(
        jnp.sum((pad_ends[None, :] <= (blk * MOE_BLOCK)[:, None]).astype(jnp.int32), axis=1),
        N_EXPERTS - 1)
    blk_exp = jnp.where(blk >= n_real, jnp.max(jnp.where(padded > 0, experts, 0)), blk_exp)
    prev_exp = jnp.concatenate([jnp.full((1,), -1, jnp.int32), blk_exp[:-1]])
    is_first = (blk < n_used) & (blk_exp != prev_exp)
    blk_slot = (jnp.cumsum(is_first.astype(jnp.int32)) + EXPERT_STAGES - 1) % EXPERT_STAGES
    later = (experts[None, :] > experts[:, None]) & (padded[None, :] > 0)
    nxt_of = jnp.min(jnp.where(later, experts[None, :], N_EXPERTS), axis=1)
    nxt_of = jnp.where(nxt_of == N_EXPERTS, -1, nxt_of)
    lookup = lambda table, e: jnp.sum(
        jnp.where(e[:, None] == experts[None, :], table[None, :], 0), axis=1)
    nxt2_of = jnp.where(nxt_of >= 0, lookup(nxt_of, jnp.maximum(nxt_of, 0)), -1)
    blk_nxt = lookup(nxt2_of, blk_exp)
    second = lookup(nxt_of, blk_exp[:1])
    blk_meta = jnp.concatenate([blk_exp, blk_slot, blk_nxt, n_used[None], second]).astype(jnp.int32)
    return dest1, dest2, pad_idx, blk_meta, nb, n_slots


def _mixer_out_and_moe(a, w_out, x2d, gt1, g, sc, sh, w_group, b_group, w_router, b_router,
                       w1, w3, w2, layer, seq):
    t, d = x2d.shape
    w_cat = jnp.zeros((ROUTER_LANES, d), jnp.float32)
    w_cat = w_cat.at[:N_EXPERTS].set(w_router.T).at[N_EXPERTS:N_EXPERTS + N_GROUPS].set(w_group.T)
    b_cat = jnp.zeros((ROUTER_LANES, 1), jnp.float32)
    b_cat = b_cat.at[:N_EXPERTS, 0].set(b_router).at[N_EXPERTS:N_EXPERTS + N_GROUPS, 0].set(b_group)
    w_hi, w_lo = _split_hi_lo(w_cat)
    x_new, hn, info, info_t, cnt = _outproj_router(a, w_out, x2d, gt1, g, sc, sh, w_hi, w_lo,
                                                   b_cat, seq)
    dest1, dest2, pad_idx, blk_meta, nb, n_slots = _slot_plan(info_t, cnt, t)
    x_slots = _sc_dispatch(hn, dest1, dest2, pad_idx, n_slots)
    y_slots = _experts(x_slots, blk_meta, w1, w3, w2, layer, nb)
    y_pairs = _sc_gather(y_slots, jnp.concatenate([dest1, dest2], axis=1))
    return x_new, y_pairs, info


def kernel(x, c, w_ada, b_ada, norm1_g, norm2_g, ml_w_in, ml_b_gate, ml_g_out, ml_w_out,
           sw_w_in, sw_g_q, sw_g_k, sw_sinks, sw_w_out, moe_w_group, moe_b_group,
           moe_w_router, moe_b_router, moe_w1, moe_w3, moe_w2):
    bsz, seq, d = x.shape
    depth = w_ada.shape[0]
    bf = jnp.bfloat16
    mod = _ada_mod(c, w_ada, b_ada)
    x2d = x.reshape(bsz * seq, d)
    pending = None
    for layer in range(depth):
        sh1, sc1, gt1, sh2, sc2, gt2 = [
            mod[layer, :, i * d:(i + 1) * d].reshape(bsz, 1, d) for i in range(6)]
        j = layer // 2
        if layer % 2 == 0:
            w = ml_w_in[j]
            q_w, k_w = w[:, :ML_QK], w[:, ML_QK:2 * ML_QK]
            v_w = w[:, 2 * ML_QK:2 * ML_QK + ML_V]
            o_w = w[:, 2 * ML_QK + ML_V:2 * ML_QK + 2 * ML_V]
            g_w = w[:, 2 * ML_QK + 2 * ML_V:]
            w_main = jnp.concatenate([v_w, o_w, q_w], axis=1).astype(bf)
            wg_t = jnp.zeros((ML_GATE_ROWS, d), jnp.float32).at[:2 * ML_HEADS].set(g_w.T)
            wg_hi, wg_lo = _split_hi_lo(wg_t)
            wk_t = k_w.T.astype(bf)
            w_out = ml_w_out[j].astype(bf)
            outs = _inproj(x2d, norm1_g[layer], sc1, sh1, w_main, seq,
                           ml_extra=(wk_t, wg_hi, wg_lo),
                           q_cols=(2 * ML_V, 2 * ML_V + ML_QK), q_scale=ML_DK ** -0.5,
                           gate_cols=(ML_V, 2 * ML_V), pending=pending)
            if pending is not None:
                x2d, outs = outs[0], outs[1:]
            main, k_t, g_t = outs
            a = _mlstm_core(main, k_t, g_t, ml_b_gate[j], ml_g_out[j], bsz, seq)
        else:
            w = sw_w_in[j]
            dq = SW_Q_HEADS * SW_DH
            dkv = SW_KV_HEADS * SW_DH
            dup = lambda m: jnp.concatenate(
                [m.reshape(d, SW_KV_HEADS, 1, SW_DH)] * 2, axis=2).reshape(d, 2 * dkv)
            w_main = jnp.concatenate(
                [w[:, :dq], dup(w[:, dq:dq + dkv]), dup(w[:, dq + dkv:])], axis=1).astype(bf)
            w_out = sw_w_out[j].astype(bf)
            gq = jnp.concatenate([sw_g_q[j], sw_g_q[j]]).reshape(1, LANES) * (SW_DH ** -0.5 * LOG2E)
            gk = jnp.concatenate([sw_g_k[j], sw_g_k[j]]).reshape(1, LANES)
            outs = _inproj(x2d, norm1_g[layer], sc1, sh1, w_main, seq, pending=pending,
                           qk_norm=(dq, dq + 2 * dkv, gq, gk))
            if pending is not None:
                x2d, outs = outs[0], outs[1:]
            a = _swa_core(outs[0], sw_sinks[j], bsz, seq)
        x2d, y_pairs, info = _mixer_out_and_moe(
            a, w_out, x2d, gt1, norm2_g[layer], sc2, sh2, moe_w_group[layer], moe_b_group[layer],
            moe_w_router[layer], moe_b_router[layer], moe_w1, moe_w3, moe_w2, layer, seq)
        pending = (y_pairs, info, gt2)
    y_pairs, info, gt2 = pending
    return _combine(x2d, y_pairs, info, gt2, seq).reshape(bsz, seq, d)
```

```python
import functools

import jax
import jax.numpy as jnp
from jax import lax
from jax.experimental import pallas as pl
from jax.experimental.pallas import tpu as pltpu
from jax.experimental.pallas import tpu_sc as plsc

EPS = 1e-6
GATE_CAP = 15.0
LOG2E = 1.4426950408889634

ML_HEADS = 4
ML_DK = 128
ML_DV = 256
ML_QK = ML_HEADS * ML_DK
ML_V = ML_HEADS * ML_DV
ML_GATE_ROWS = 16

SW_Q_HEADS = 16
SW_KV_HEADS = 4
SW_GROUP = SW_Q_HEADS // SW_KV_HEADS
SW_DH = 64
SW_WINDOW = 128
LANES = 128

N_GROUPS = 8
EXPERTS_PER_GROUP = 8
N_EXPERTS = N_GROUPS * EXPERTS_PER_GROUP
MOE_BLOCK = 256
ROUTER_LANES = 128
SC_WINDOW = 128
SC_ROWS = 64
EXPERT_BLOCKS_PER_STEP = 4
EXPERT_STAGES = 3

SUBLANES = 8
TOKEN_TILE = 512
ROUTER_TILE = 1024
COMBINE_TILE = 1024
PROJ_COL_CHUNK = 1024
ML_CHUNK = 256
ADA_COL_TILE = 768
V7X_VMEM_BYTES = 64 * 1024 * 1024
VMEM_LIMIT = V7X_VMEM_BYTES - 8 * 1024 * 1024

_NT = (((1,), (1,)), ((), ()))


def _bdot(a, b):
    return jnp.dot(a, b, preferred_element_type=jnp.float32)


def _bdot_nt(a, b):
    return lax.dot_general(a, b, _NT, preferred_element_type=jnp.float32)


def _split_hi_lo(a):
    hi = a.astype(jnp.bfloat16)
    lo = (a - hi.astype(jnp.float32)).astype(jnp.bfloat16)
    return hi, lo


def _params(sem):
    return pltpu.CompilerParams(dimension_semantics=sem, vmem_limit_bytes=VMEM_LIMIT)


def _ada_kernel(c_ref, w_ref, b_ref, o_ref):
    c = c_ref[...]
    cond = c * jax.nn.sigmoid(c)
    c_hi, c_lo = _split_hi_lo(cond)
    w_hi, w_lo = _split_hi_lo(w_ref[...])
    acc = _bdot(c_hi, w_hi) + (_bdot(c_lo, w_hi) + _bdot(c_hi, w_lo))
    o_ref[...] = acc + b_ref[...]


def _ada_mod(c, w_ada, b_ada):
    depth, d, n = w_ada.shape
    bsz = c.shape[0]
    rows = SUBLANES
    tn = ADA_COL_TILE
    c_pad = jnp.zeros((rows, d), jnp.float32).at[:bsz].set(c)
    out = pl.pallas_call(
        _ada_kernel,
        grid=(depth, n // tn),
        in_specs=[
            pl.BlockSpec((rows, d), lambda l, j: (0, 0)),
            pl.BlockSpec((None, d, tn), lambda l, j: (l, 0, j)),
            pl.BlockSpec((None, 1, tn), lambda l, j: (l, 0, j)),
        ],
        out_specs=pl.BlockSpec((None, rows, tn), lambda l, j: (l, 0, j)),
        out_shape=jax.ShapeDtypeStruct((depth, rows, n), jnp.float32),
        compiler_params=_params(("arbitrary", "arbitrary")),
        name="ada_mod",
    )(c_pad, w_ada, b_ada.reshape(depth, 1, n))
    return out[:, :bsz]


def _modulated_norm(x, g, sc, sh):
    y = x * lax.rsqrt(jnp.mean(x * x, axis=-1, keepdims=True) + EPS)
    return y * (g * (1.0 + sc)) + sh


def _moe_combined(x, y1_ref, y2_ref, info_ref, gt_ref):
    info = info_ref[...]
    g1 = info[:, 4:5]
    g2 = info[:, 5:6]
    y1_hi, y1_lo = _unpack_bf16_pairs(y1_ref[...])
    y2_hi, y2_lo = _unpack_bf16_pairs(y2_ref[...])
    y = jnp.concatenate([g1 * y1_hi + g2 * y2_hi, g1 * y1_lo + g2 * y2_lo], axis=-1)
    return x + gt_ref[...] * y


def _qk_head_norm(acc, c0, qk_norm, gq, gk):
    q_hi, k_hi = qk_norm
    low = lax.broadcasted_iota(jnp.int32, (acc.shape[0], LANES), 1) < SW_DH
    slabs = []
    for j in range(acc.shape[1] // LANES):
        slab = acc[:, j * LANES:(j + 1) * LANES]
        if c0 + j * LANES < q_hi:
            sq = slab * slab
            ss_lo = jnp.sum(jnp.where(low, sq, 0.0), axis=-1, keepdims=True)
            ss_hi = jnp.sum(jnp.where(low, 0.0, sq), axis=-1, keepdims=True)
            rs = jnp.where(low, lax.rsqrt(ss_lo / SW_DH + EPS), lax.rsqrt(ss_hi / SW_DH + EPS))
            slab = slab * rs * gq
        elif c0 + j * LANES < k_hi:
            slab = slab * lax.rsqrt(jnp.mean(slab * slab, axis=-1, keepdims=True) + EPS) * gk
        slabs.append(slab)
    return jnp.concatenate(slabs, axis=-1)


def _inproj_kernel(*refs, n_main, chunk, q_cols, q_scale, gate_cols, qk_norm, with_ml,
                   with_combine):
    refs = list(refs)
    n_in = (5 + (4 if with_combine else 0) + (2 if with_ml else 0)
            + (2 if qk_norm is not None else 0))
    n_scratch = 4 if with_ml else 2
    ins, outs, scratch = refs[:n_in], refs[n_in:-n_scratch], refs[-n_scratch:]
    x_ref = ins.pop(0)
    if with_combine:
        y1_ref, y2_ref, info_ref, gtp_ref = ins[:4]
        ins = ins[4:]
        xo_ref = outs.pop(0)
    g_ref, sc_ref, sh_ref, w_ref = ins[:4]
    o_ref = outs[0]
    if with_ml:
        wk_ref, wgh_ref = ins[4:]
        kt_ref, gt_ref = outs[1:]
    if qk_norm is not None:
        gq_ref, gk_ref = ins[4:]

    def normalise(hb_dst, lo_dst):
        x = x_ref[...]
        if with_combine:
            x = _moe_combined(x, y1_ref, y2_ref, info_ref, gtp_ref)
            xo_ref[...] = x
        hn = _modulated_norm(x, g_ref[...], sc_ref[...], sh_ref[...])
        hb = hn.astype(jnp.bfloat16)
        hb_dst[...] = hb
        if with_ml:
            lo_dst[...] = (hn - hb.astype(jnp.float32)).astype(jnp.bfloat16)

    def project(hb_src, lo_src):
        hb = hb_src[...]
        for c0 in range(0, n_main, chunk):
            c1 = min(c0 + chunk, n_main)
            acc = _bdot(hb, w_ref[:, c0:c1])
            if q_cols is not None and q_cols[0] <= c0 < q_cols[1]:
                acc = acc * q_scale
            if gate_cols is not None and gate_cols[0] <= c0 < gate_cols[1]:
                acc = jax.nn.sigmoid(acc)
            if qk_norm is not None and c0 < qk_norm[1]:
                acc = _qk_head_norm(acc, c0, qk_norm, gq_ref[...], gk_ref[...])
            o_ref[:, c0:c1] = acc.astype(o_ref.dtype)
        if with_ml:
            nk = kt_ref.shape[0]
            stacked = _bdot_nt(wk_ref[...], hb)
            kt_ref[...] = stacked[:nk].astype(kt_ref.dtype)
            gt_ref[...] = (stacked[nk:nk + ML_GATE_ROWS]
                           + (_bdot_nt(wgh_ref[...], lo_src[...]) + stacked[nk + ML_GATE_ROWS:]))

    hb_a, hb_b = scratch[:2]
    lo_a, lo_b = scratch[2:] if with_ml else (None, None)
    s = pl.program_id(0)

    @pl.when(s == 0)
    def _():
        hb_b[...] = jnp.zeros_like(hb_b)
        if with_ml:
            lo_b[...] = jnp.zeros_like(lo_b)

    @pl.when(s % 2 == 0)
    def _():
        normalise(hb_a, lo_a)
        project(hb_b, lo_b)

    @pl.when(s % 2 == 1)
    def _():
        normalise(hb_b, lo_b)
        project(hb_a, lo_a)


def _inproj(x2d, g, sc, sh, w_main, seq, *, ml_extra=None, q_cols=None, q_scale=1.0,
            gate_cols=None, qk_norm=None, pending=None):
    t, d = x2d.shape
    tm = TOKEN_TILE if ml_extra is not None else min(2 * TOKEN_TILE, seq)
    n_main = w_main.shape[1]
    tiles_per_batch = seq // tm
    n_tiles = t // tm
    resident = pl.Buffered(1)
    norm_tile = lambda s: jnp.minimum(s, n_tiles - 1)
    proj_tile = lambda s: jnp.maximum(s - 1, 0)
    row = lambda s: (norm_tile(s), 0)
    per_batch = lambda s: (norm_tile(s) // tiles_per_batch, 0, 0)
    const = lambda s: (0, 0)
    in_specs = [pl.BlockSpec((tm, d), row)]
    args = [x2d]
    out_specs, out_shape = [], []
    if pending is not None:
        y_pairs, info, gt_prev = pending
        in_specs += [pl.BlockSpec((tm, d // 2), row),
                     pl.BlockSpec((tm, d // 2), lambda s: (norm_tile(s) + n_tiles, 0)),
                     pl.BlockSpec((tm, ROUTER_LANES), row),
                     pl.BlockSpec((None, 1, d), per_batch)]
        args += [y_pairs, y_pairs, info, gt_prev]
        out_specs += [pl.BlockSpec((tm, d), row)]
        out_shape += [jax.ShapeDtypeStruct((t, d), jnp.float32)]
    in_specs += [
        pl.BlockSpec((1, d), const),
        pl.BlockSpec((None, 1, d), per_batch),
        pl.BlockSpec((None, 1, d), per_batch),
        pl.BlockSpec((d, n_main), const, pipeline_mode=resident),
    ]
    args += [g.reshape(1, d), sc, sh, w_main]
    out_specs += [pl.BlockSpec((tm, n_main), lambda s: (proj_tile(s), 0))]
    out_shape += [jax.ShapeDtypeStruct((t, n_main), jnp.bfloat16)]
    scratch = [pltpu.VMEM((tm, d), jnp.bfloat16), pltpu.VMEM((tm, d), jnp.bfloat16)]
    if ml_extra is not None:
        wk_t, wg_hi, wg_lo = ml_extra
        nk = wk_t.shape[0]
        stacked = jnp.concatenate([wk_t, wg_hi, wg_lo], axis=0)
        in_specs += [pl.BlockSpec(stacked.shape, const, pipeline_mode=resident),
                     pl.BlockSpec(wg_hi.shape, const, pipeline_mode=resident)]
        args += [stacked, wg_hi]
        out_specs += [pl.BlockSpec((nk, tm), lambda s: (0, proj_tile(s))),
                      pl.BlockSpec((ML_GATE_ROWS, tm), lambda s: (0, proj_tile(s)))]
        out_shape += [jax.ShapeDtypeStruct((nk, t), jnp.bfloat16),
                      jax.ShapeDtypeStruct((ML_GATE_ROWS, t), jnp.float32)]
        scratch += [pltpu.VMEM((tm, d), jnp.bfloat16), pltpu.VMEM((tm, d), jnp.bfloat16)]
    if qk_norm is not None:
        in_specs += [pl.BlockSpec((1, LANES), const), pl.BlockSpec((1, LANES), const)]
        args += [qk_norm[2], qk_norm[3]]
    kern = functools.partial(_inproj_kernel, n_main=n_main, chunk=PROJ_COL_CHUNK, q_cols=q_cols,
                             q_scale=q_scale, gate_cols=gate_cols,
                             qk_norm=None if qk_norm is None else qk_norm[:2],
                             with_ml=ml_extra is not None, with_combine=pending is not None)
    return pl.pallas_call(
        kern,
        grid=(n_tiles + 1,),
        in_specs=in_specs,
        out_specs=out_specs,
        out_shape=out_shape,
        scratch_shapes=scratch,
        compiler_params=_params(("arbitrary",)),
        name="inproj_ml" if ml_extra is not None else "inproj_sw",
    )(*args)


def _mlstm_gate_terms(graw, bias, upper):
    H = ML_HEADS
    L = graw.shape[1]
    z = graw + bias
    gates = GATE_CAP * jnp.tanh(z / GATE_CAP)
    log_f = jnp.minimum(gates, 0.0) - jnp.log1p(jnp.exp(-jnp.abs(gates)))
    row = lax.broadcasted_iota(jnp.int32, (ML_GATE_ROWS, L), 0)
    lane = lax.broadcasted_iota(jnp.int32, (ML_GATE_ROWS, L), 1)
    is_i = row < H
    slab = jnp.where(is_i, gates, log_f)
    a1 = slab.astype(jnp.bfloat16)
    r1 = slab - a1.astype(jnp.float32)
    a2 = r1.astype(jnp.bfloat16)
    a3 = (r1 - a2.astype(jnp.float32)).astype(jnp.bfloat16)
    cum = _bdot(a1, upper) + (_bdot(a2, upper) + _bdot(a3, upper))
    ib = jnp.where(is_i, gates, cum) * LOG2E
    b = pltpu.roll(ib, ML_GATE_ROWS - H, 0)
    u = ib - b
    cm = u
    shift = 1
    while shift < L:
        cm = jnp.maximum(cm, jnp.where(lane >= shift, pltpu.roll(cm, shift, 1), -jnp.inf))
        shift *= 2
    return b, u, cm


def _mlstm_kernel(v_ref, o_ref, q_ref, kt_ref, gt_ref, gtn_ref, bg_ref, gout_ref, out_ref,
                  c_ref, m_ref, b_ref, u_ref, cm_ref, *, chunk):
    L = chunk
    H, dk, dv = ML_HEADS, ML_DK, ML_DV
    r_idx = lax.broadcasted_iota(jnp.int32, (L, L), 0)
    c_idx = lax.broadcasted_iota(jnp.int32, (L, L), 1)
    upper = jnp.where(r_idx <= c_idx, 1.0, 0.0).astype(jnp.bfloat16)
    causal = r_idx >= c_idx
    row = lax.broadcasted_iota(jnp.int32, (ML_GATE_ROWS, L), 0)
    ones_col = jnp.where(lax.broadcasted_iota(jnp.int32, (L, LANES), 1) == 0, 1.0, 0.0
                         ).astype(jnp.bfloat16)

    @pl.when(pl.program_id(1) == 0)
    def _():
        c_ref[...] = jnp.zeros_like(c_ref)
        m_ref[...] = jnp.zeros_like(m_ref)
        b0, u0, cm0 = _mlstm_gate_terms(gt_ref[...], bg_ref[...], upper)
        b_ref[...] = b0
        u_ref[...] = u0
        cm_ref[...] = cm0

    b16 = b_ref[...]
    u16 = u_ref[...]
    cm16 = cm_ref[...]
    b_n, u_n, cm_n = _mlstm_gate_terms(gtn_ref[...], bg_ref[...], upper)
    b_ref[...] = b_n
    u_ref[...] = u_n
    cm_ref[...] = cm_n

    m_prev = m_ref[:, 0:1]
    z16 = jnp.maximum(m_prev, cm16)
    w_inter16 = jnp.exp2(m_prev - z16)
    e_negm16 = jnp.exp2(-(b16 + z16))
    z_last = z16[:, L - 1:L]
    w_state16 = jnp.exp2(u16 - z_last)
    decay16 = jnp.exp2(m_prev - z_last)
    m_ref[...] = jnp.broadcast_to(b16[:, L - 1:L] + z_last, m_ref.shape)
    stacked = jnp.where(row < H, z16,
                        jnp.where(row < 2 * H, pltpu.roll(w_inter16, H, 0),
                                  pltpu.roll(e_negm16, 2 * H, 0)))
    cols = jnp.concatenate(
        [stacked, jnp.zeros((LANES - ML_GATE_ROWS, L), jnp.float32)], axis=0).T

    for h in range(H):
        u_r = u16[h:h + 1, :]
        z_c = cols[:, h:h + 1]
        w_inter = cols[:, H + h:H + h + 1]
        e_negm = cols[:, 2 * H + h:2 * H + h + 1]
        c_ext = c_ref[h]
        q = q_ref[:, h * dk:(h + 1) * dk]
        kt = kt_ref[h * dk:(h + 1) * dk, :]
        v_ext = jnp.concatenate([v_ref[:, h * dv:(h + 1) * dv], ones_col], axis=-1)

        w_intra = jnp.exp2(jnp.where(causal, u_r - z_c, -jnp.inf))
        s = (_bdot(q, kt) * w_intra).astype(jnp.bfloat16)
        qw = (q.astype(jnp.float32) * w_inter).astype(jnp.bfloat16)
        nd = _bdot(qw, c_ext.astype(jnp.bfloat16)) + _bdot(s, v_ext)
        den = nd[:, dv:dv + 1]
        hb = nd[:, :dv] * (1.0 / jnp.maximum(jnp.abs(den), e_negm))

        kw = (kt.astype(jnp.float32) * w_state16[h:h + 1, :]).astype(jnp.bfloat16)
        c_ref[h] = decay16[h:h + 1, :] * c_ext + _bdot(kw, v_ext)

        y = hb * lax.rsqrt(jnp.mean(hb * hb, axis=-1, keepdims=True) + EPS)
        y = y * gout_ref[:, h * dv:(h + 1) * dv]
        og = o_ref[:, h * dv:(h + 1) * dv].astype(jnp.float32)
        out_ref[:, h * dv:(h + 1) * dv] = (y * og).astype(out_ref.dtype)


def _mlstm_core(main, k_t, g_t, b_gate, g_out, bsz, seq):
    chunk = ML_CHUNK
    t = main.shape[0]
    nc = seq // chunk
    blk = lambda b, c: b * nc + c
    bg = jnp.zeros((ML_GATE_ROWS, 1), jnp.float32).at[:2 * ML_HEADS, 0].set(b_gate)
    return pl.pallas_call(
        functools.partial(_mlstm_kernel, chunk=chunk),
        grid=(bsz, nc),
        in_specs=[
            pl.BlockSpec((chunk, ML_V), lambda b, c: (blk(b, c), 0)),
            pl.BlockSpec((chunk, ML_V), lambda b, c: (blk(b, c), 1)),
            pl.BlockSpec((chunk, ML_QK), lambda b, c: (blk(b, c), 4)),
            pl.BlockSpec((ML_QK, chunk), lambda b, c: (0, blk(b, c))),
            pl.BlockSpec((ML_GATE_ROWS, chunk), lambda b, c: (0, blk(b, c))),
            pl.BlockSpec((ML_GATE_ROWS, chunk), lambda b, c: (0, blk(b, jnp.minimum(c + 1, nc - 1)))),
            pl.BlockSpec((ML_GATE_ROWS, 1), lambda b, c: (0, 0)),
            pl.BlockSpec((1, ML_V), lambda b, c: (0, 0)),
        ],
        out_specs=pl.BlockSpec((chunk, ML_V), lambda b, c: (blk(b, c), 0)),
        out_shape=jax.ShapeDtypeStruct((t, ML_V), jnp.bfloat16),
        scratch_shapes=[
            pltpu.VMEM((ML_HEADS, ML_DK, ML_DV + LANES), jnp.float32),
            pltpu.VMEM((ML_GATE_ROWS, LANES), jnp.float32),
            pltpu.VMEM((ML_GATE_ROWS, chunk), jnp.float32),
            pltpu.VMEM((ML_GATE_ROWS, chunk), jnp.float32),
            pltpu.VMEM((ML_GATE_ROWS, chunk), jnp.float32),
        ],
        compiler_params=_params(("arbitrary", "arbitrary")),
        name="mlstm_core",
    )(main, main, main, k_t, g_t, g_t, bg, g_out.reshape(1, ML_V))


def _swa_kernel(q_ref, kc_ref, kp_ref, vc_ref, vp_ref, bias_ref, o_ref):
    W = SW_WINDOW
    lane = lax.broadcasted_iota(jnp.int32, (W, LANES), 1)
    low = lane < SW_DH
    not_first_row = lax.broadcasted_iota(jnp.int32, (W, LANES), 0) > 0
    variant = jnp.minimum(pl.program_id(1), 1)
    for g in range(SW_KV_HEADS):
        sl = slice(g * LANES, (g + 1) * LANES)
        k_prev = jnp.where(not_first_row, kp_ref[:, sl].astype(jnp.float32), 0.0)
        v_prev = jnp.where(not_first_row, vp_ref[:, sl].astype(jnp.float32), 0.0)
        kn = jnp.concatenate([k_prev.astype(jnp.bfloat16), kc_ref[:, sl]], axis=0)
        v2 = jnp.concatenate([v_prev.astype(jnp.bfloat16), vc_ref[:, sl]], axis=0)
        parts = []
        for p in range(2):
            c0 = g * SW_GROUP * SW_DH + p * LANES
            qp = q_ref[:, c0:c0 + LANES].astype(jnp.float32)
            parts.append(jnp.where(low, qp, 0.0).astype(jnp.bfloat16))
            parts.append(jnp.where(low, 0.0, qp).astype(jnp.bfloat16))
        q4 = jnp.concatenate(parts, axis=0)
        scores = _bdot_nt(q4, kn) + bias_ref[variant, g]
        m = jnp.max(scores, axis=-1, keepdims=True)
        pexp = jnp.exp2(scores - m)
        denom = jnp.sum(pexp, axis=-1, keepdims=True)
        o4 = _bdot(pexp.astype(jnp.bfloat16), v2) * (1.0 / denom)
        for p in range(2):
            oa = o4[(2 * p) * W:(2 * p + 1) * W]
            ob = o4[(2 * p + 1) * W:(2 * p + 2) * W]
            c0 = g * SW_GROUP * SW_DH + p * LANES
            o_ref[:, c0:c0 + LANES] = jnp.where(low, oa, ob).astype(o_ref.dtype)


def _swa_core(proj, sinks, bsz, seq):
    t = proj.shape[0]
    W = SW_WINDOW
    nb = seq // W
    dq = SW_Q_HEADS * SW_DH
    kv_w = SW_KV_HEADS * LANES
    k_blk = dq // kv_w
    v_blk = k_blk + 1
    cur = lambda b, n: b * nb + n
    prev = lambda b, n: b * nb + jnp.maximum(n - 1, 0)
    qi = (jnp.arange(SW_GROUP * W) % W)[:, None]
    ki = jnp.arange(2 * W)[None, :]
    rel = qi + W - ki
    in_win = (rel >= 0) & (rel < W)
    window = jnp.stack([jnp.where(in_win & (ki >= W), 0.0, -jnp.inf),
                        jnp.where(in_win, 0.0, -jnp.inf)]).astype(jnp.float32)
    sink_rows = jnp.repeat(sinks.astype(jnp.float32).reshape(SW_KV_HEADS, SW_GROUP) * LOG2E, W, axis=1)
    bias = jnp.where(ki[None, None] == 0, sink_rows[None, :, :, None], window[:, None])
    return pl.pallas_call(
        _swa_kernel,
        grid=(bsz, nb),
        in_specs=[
            pl.BlockSpec((W, dq), lambda b, n: (cur(b, n), 0)),
            pl.BlockSpec((W, kv_w), lambda b, n: (cur(b, n), k_blk)),
            pl.BlockSpec((W, kv_w), lambda b, n: (prev(b, n), k_blk)),
            pl.BlockSpec((W, kv_w), lambda b, n: (cur(b, n), v_blk)),
            pl.BlockSpec((W, kv_w), lambda b, n: (prev(b, n), v_blk)),
            pl.BlockSpec(bias.shape, lambda b, n: (0, 0, 0, 0)),
        ],
        out_specs=pl.BlockSpec((W, dq), lambda b, n: (cur(b, n), 0)),
        out_shape=jax.ShapeDtypeStruct((t, dq), jnp.bfloat16),
        compiler_params=_params(("arbitrary", "arbitrary")),
        name="swa_core",
    )(proj, proj, proj, proj, proj, bias)


def _pack_rounded_pairs(r):
    k = r.shape[1] // 2
    hi = lax.bitcast_convert_type(r[:, :k], jnp.uint32)
    lo = lax.bitcast_convert_type(r[:, k:], jnp.uint32)
    return hi | (lo >> 16)


def _pack_bf16_pairs(a):
    return _pack_rounded_pairs(a.astype(jnp.bfloat16).astype(jnp.float32))


def _unpack_bf16_pairs(u):
    hi = lax.bitcast_convert_type(u & jnp.uint32(0xFFFF0000), jnp.float32)
    lo = lax.bitcast_convert_type(u << 16, jnp.float32)
    return hi, lo


def _route_tile(x_new, g_ref, sc_ref, sh_ref, wh_ref, wl_ref, b_ref,
                hn_ref, info_ref, infot_ref, cnt_ref, carry_ref, earlier_ref, tm):
    hn = _modulated_norm(x_new, g_ref[...], sc_ref[...], sh_ref[...])
    h_hi = hn.astype(jnp.bfloat16)
    hi_f32 = h_hi.astype(jnp.float32)
    h_lo = (hn - hi_f32).astype(jnp.bfloat16)
    hn_ref[...] = _pack_rounded_pairs(hi_f32)
    wide = _bdot_nt(wl_ref[...], h_hi)
    logits = (wide[:ROUTER_LANES] + (_bdot_nt(wh_ref[...], h_lo) + wide[ROUTER_LANES:])
              + b_ref[...])
    E8 = EXPERTS_PER_GROUP
    sub = lax.broadcasted_iota(jnp.int32, (E8, tm), 0).astype(jnp.float32)
    big = float(ROUTER_LANES)
    neg = -jnp.inf

    gl = logits[N_EXPERTS:N_EXPERTS + N_GROUPS]
    gmax = jnp.max(gl, axis=0, keepdims=True)
    gsel = jnp.min(jnp.where(gl == gmax, sub, big), axis=0, keepdims=True)
    p_grp = 1.0 / jnp.sum(jnp.exp(gl - gmax), axis=0, keepdims=True)

    el = logits[0:E8]
    for grp in range(1, N_GROUPS):
        el = jnp.where(gsel == grp, logits[grp * E8:(grp + 1) * E8], el)
    v1 = jnp.max(el, axis=0, keepdims=True)
    j1 = jnp.min(jnp.where(el == v1, sub, big), axis=0, keepdims=True)
    el2 = jnp.where(sub == j1, neg, el)
    v2 = jnp.max(el2, axis=0, keepdims=True)
    j2 = jnp.min(jnp.where(el2 == v2, sub, big), axis=0, keepdims=True)
    i1 = gsel * E8 + j1
    i2 = gsel * E8 + j2
    e21 = jnp.exp(v2 - v1)
    gate1 = p_grp / (1.0 + e21)
    gate2 = p_grp * e21 / (1.0 + e21)

    erow = lax.broadcasted_iota(jnp.int32, (N_EXPERTS, tm), 0).astype(jnp.float32)
    hit1 = erow == i1
    hit2 = erow == i2
    onehot = jnp.where(hit1 | hit2, 1.0, 0.0)
    carry = carry_ref[:, 0:1]
    before = _bdot(onehot.astype(jnp.bfloat16), earlier_ref[...]) + carry
    rank1 = jnp.sum(jnp.where(hit1, before, 0.0), axis=0, keepdims=True)
    rank2 = jnp.sum(jnp.where(hit2, before, 0.0), axis=0, keepdims=True)
    total = carry + jnp.sum(onehot, axis=1, keepdims=True)
    carry_ref[...] = jnp.broadcast_to(total, carry_ref.shape)
    cnt_ref[...] = jnp.broadcast_to(total, cnt_ref.shape)

    info_t = jnp.where(sub == 0, i1, 0.0)
    info_t = jnp.where(sub == 1, i2, info_t)
    info_t = jnp.where(sub == 2, rank1, info_t)
    info_t = jnp.where(sub == 3, rank2, info_t)
    info_t = jnp.where(sub == 4, gate1, info_t)
    info_t = jnp.where(sub == 5, gate2, info_t)
    infot_ref[...] = info_t
    info_ref[...] = jnp.concatenate(
        [info_t, jnp.zeros((ROUTER_LANES - E8, tm), jnp.float32)], axis=0).T


def _router_kernel(a_ref, wo_ref, x_ref, gt_ref, g_ref, sc_ref, sh_ref, wh_ref, wl_ref, b_ref,
                   xo_ref, hn_ref, info_ref, infot_ref, cnt_ref, carry_ref, earlier_ref, *, tm):
    @pl.when(pl.program_id(0) == 0)
    def _():
        carry_ref[...] = jnp.zeros_like(carry_ref)
        r_idx = lax.broadcasted_iota(jnp.int32, (tm, tm), 0)
        c_idx = lax.broadcasted_iota(jnp.int32, (tm, tm), 1)
        earlier_ref[...] = jnp.where(r_idx < c_idx, 1.0, 0.0).astype(jnp.bfloat16)

    x_new = x_ref[...] + gt_ref[...] * _bdot(a_ref[...], wo_ref[...])
    xo_ref[...] = x_new
    _route_tile(x_new, g_ref, sc_ref, sh_ref, wh_ref, wl_ref, b_ref,
                hn_ref, info_ref, infot_ref, cnt_ref, carry_ref, earlier_ref, tm)


def _outproj_router(a, w_out, x2d, gt, g, sc, sh, w_hi, w_lo, bias, seq):
    t, d = x2d.shape
    tm = min(ROUTER_TILE, seq)
    tiles_per_batch = seq // tm
    per_batch = lambda i: (i // tiles_per_batch, 0, 0)
    const = lambda i: (0, 0)
    return pl.pallas_call(
        functools.partial(_router_kernel, tm=tm),
        grid=(t // tm,),
        in_specs=[
            pl.BlockSpec((tm, a.shape[1]), lambda i: (i, 0)),
            pl.BlockSpec(w_out.shape, const),
            pl.BlockSpec((tm, d), lambda i: (i, 0)),
            pl.BlockSpec((None, 1, d), per_batch),
            pl.BlockSpec((1, d), const),
            pl.BlockSpec((None, 1, d), per_batch),
            pl.BlockSpec((None, 1, d), per_batch),
            pl.BlockSpec((ROUTER_LANES, d), const),
            pl.BlockSpec((2 * ROUTER_LANES, d), const),
            pl.BlockSpec((ROUTER_LANES, 1), const),
        ],
        out_specs=[
            pl.BlockSpec((tm, d), lambda i: (i, 0)),
            pl.BlockSpec((tm, d // 2), lambda i: (i, 0)),
            pl.BlockSpec((tm, ROUTER_LANES), lambda i: (i, 0)),
            pl.BlockSpec((EXPERTS_PER_GROUP, tm), lambda i: (0, i)),
            pl.BlockSpec((N_EXPERTS, LANES), const),
        ],
        out_shape=[
            jax.ShapeDtypeStruct((t, d), jnp.float32),
            jax.ShapeDtypeStruct((t, d // 2), jnp.uint32),
            jax.ShapeDtypeStruct((t, ROUTER_LANES), jnp.float32),
            jax.ShapeDtypeStruct((EXPERTS_PER_GROUP, t), jnp.float32),
            jax.ShapeDtypeStruct((N_EXPERTS, LANES), jnp.float32),
        ],
        scratch_shapes=[pltpu.VMEM((N_EXPERTS, LANES), jnp.float32),
                        pltpu.VMEM((tm, tm), jnp.bfloat16)],
        compiler_params=_params(("arbitrary",)),
        name="outproj_router",
    )(a, w_out, x2d, gt, g.reshape(1, d), sc, sh, w_hi, jnp.concatenate([w_hi, w_lo], axis=0), bias)


def _sc_mesh():
    return plsc.VectorSubcoreMesh(core_axis_name="c", subcore_axis_name="s")


def _sc_dispatch(rows, d0, d1, pad_idx, n_slots):
    t, w = rows.shape
    n_pad = pad_idx.shape[1]
    zeros = jnp.zeros((SC_ROWS, w), rows.dtype)
    sem = (pltpu.PARALLEL, pltpu.ARBITRARY)
    parts = SC_WINDOW // SC_ROWS

    @pl.kernel(out_type=jax.ShapeDtypeStruct((n_slots + SC_WINDOW, w), rows.dtype), mesh=_sc_mesh(),
               scratch_types=[pltpu.SemaphoreType.DMA, pltpu.SemaphoreType.DMA])
    def dispatch(x_hbm, d0_hbm, d1_hbm, z_hbm, p_hbm, o_hbm, sem0, sem1):
        def scatter_rows(x_vmem, i0_vmem, i1_vmem):
            part = pl.ds(pl.program_id(1) * SC_ROWS, SC_ROWS)
            first = pltpu.async_copy(x_vmem, o_hbm.at[i0_vmem.at[0, part]], sem0)
            second = pltpu.async_copy(x_vmem, o_hbm.at[i1_vmem.at[0, part]], sem1)
            first.wait()
            second.wait()

        pltpu.emit_pipeline(
            scatter_rows,
            grid=(t // SC_WINDOW, parts),
            in_specs=[pl.BlockSpec((SC_ROWS, w), lambda i, j: (parts * i + j, 0)),
                      pl.BlockSpec((1, SC_WINDOW), lambda i, j: (0, i)),
                      pl.BlockSpec((1, SC_WINDOW), lambda i, j: (0, i))],
            out_specs=[],
            core_axis_name=("c", "s"),
            dimension_semantics=sem,
        )(x_hbm, d0_hbm, d1_hbm)

        def scatter_zeros(z_vmem, p_vmem):
            part = pl.ds(pl.program_id(1) * SC_ROWS, SC_ROWS)
            pltpu.sync_copy(z_vmem, o_hbm.at[p_vmem.at[0, part]])

        pltpu.emit_pipeline(
            scatter_zeros,
            grid=(n_pad // SC_WINDOW, parts),
            in_specs=[pl.BlockSpec((SC_ROWS, w), lambda i, j: (0, 0)),
                      pl.BlockSpec((1, SC_WINDOW), lambda i, j: (0, i))],
            out_specs=[],
            core_axis_name=("c", "s"),
            dimension_semantics=sem,
        )(z_hbm, p_hbm)

    return dispatch(rows, d0, d1, zeros, pad_idx)


def _sc_gather(src, idx):
    n_out = idx.shape[1]
    w = src.shape[1]
    parts = SC_WINDOW // SC_ROWS

    @pl.kernel(out_type=jax.ShapeDtypeStruct((n_out, w), src.dtype), mesh=_sc_mesh())
    def gather(x_hbm, i_hbm, o_hbm):
        def gather_rows(i_vmem, o_vmem):
            part = pl.ds(pl.program_id(1) * SC_ROWS, SC_ROWS)
            pltpu.sync_copy(x_hbm.at[i_vmem.at[0, part]], o_vmem)

        pltpu.emit_pipeline(
            gather_rows,
            grid=(n_out // SC_WINDOW, parts),
            in_specs=[pl.BlockSpec((1, SC_WINDOW), lambda i, j: (0, i))],
            out_specs=[pl.BlockSpec((SC_ROWS, w), lambda i, j: (parts * i + j, 0))],
            core_axis_name=("c", "s"),
            dimension_semantics=(pltpu.PARALLEL, pltpu.ARBITRARY),
        )(i_hbm, o_hbm)

    return gather(src, idx)


def _expert_kernel(meta_ref, x_ref, w1_hbm, w3_hbm, w2_hbm, y_ref,
                   w1_buf, w3_buf, w2_buf, w1_c, w3_c, w2_c, sems, *, layer, nb):
    def weight_copies(expert, s):
        return (pltpu.make_async_copy(w1_hbm.at[layer, expert], w1_buf.at[s], sems.at[s, 0]),
                pltpu.make_async_copy(w3_hbm.at[layer, expert], w3_buf.at[s], sems.at[s, 1]),
                pltpu.make_async_copy(w2_hbm.at[layer, expert], w2_buf.at[s], sems.at[s, 2]))

    @pl.when(pl.program_id(0) == 0)
    def _():
        for cp in weight_copies(meta_ref[0], 0):
            cp.start()
        second = meta_ref[3 * nb + 1]

        @pl.when(second >= 0)
        def _():
            for cp in weight_copies(second, 1):
                cp.start()

    for j in range(EXPERT_BLOCKS_PER_STEP):
        i = pl.program_id(0) * EXPERT_BLOCKS_PER_STEP + j
        rows = slice(j * MOE_BLOCK, (j + 1) * MOE_BLOCK)
        e = meta_ref[i]
        slot = meta_ref[nb + i]
        nxt = meta_ref[2 * nb + i]
        used = i < meta_ref[3 * nb]
        first = used & ((i == 0) | (e != meta_ref[jnp.maximum(i - 1, 0)]))

        @pl.when(first)
        def _():
            for cp in weight_copies(e, slot):
                cp.wait()

            @pl.when(nxt >= 0)
            def _():
                for cp in weight_copies(nxt, (slot + EXPERT_STAGES - 1) % EXPERT_STAGES):
                    cp.start()

            w1_c[...] = w1_buf[slot].astype(jnp.bfloat16)
            w3_c[...] = w3_buf[slot].astype(jnp.bfloat16)
            w2_c[...] = w2_buf[slot].astype(jnp.bfloat16)

        @pl.when(used)
        def _():
            x_hi, x_lo = _unpack_bf16_pairs(x_ref[rows, :])
            xb = jnp.concatenate([x_hi, x_lo], axis=-1).astype(jnp.bfloat16)
            full = 256
            tail = jnp.concatenate([w1_c[:, full:], w3_c[:, full:]], axis=-1)
            half_rows = MOE_BLOCK // 2
            h_tail = jnp.concatenate([_bdot(xb[:half_rows], tail), _bdot(xb[half_rows:], tail)], axis=0)
            h1 = jnp.concatenate([_bdot(xb, w1_c[:, :full]), h_tail[:, :LANES]], axis=-1)
            h3 = jnp.concatenate([_bdot(xb, w3_c[:, :full]), h_tail[:, LANES:]], axis=-1)
            act = (h1 * jax.nn.sigmoid(h1) * h3).astype(jnp.bfloat16)
            y_ref[rows, :] = _pack_bf16_pairs(_bdot(act, w2_c[...]))

        @pl.when(jnp.logical_not(used))
        def _():
            y_ref[rows, :] = jnp.zeros((MOE_BLOCK, y_ref.shape[1]), y_ref.dtype)


def _experts(x_slots, blk_meta, w1, w3, w2, layer, nb):
    dp = x_slots.shape[1]
    d, de = w1.shape[-2:]
    step_rows = EXPERT_BLOCKS_PER_STEP * MOE_BLOCK
    assert nb % EXPERT_BLOCKS_PER_STEP == 0
    last_used_step = lambda s: (s[3 * nb] - 1) // EXPERT_BLOCKS_PER_STEP
    grid_spec = pltpu.PrefetchScalarGridSpec(
        num_scalar_prefetch=1,
        grid=(nb // EXPERT_BLOCKS_PER_STEP,),
        in_specs=[
            pl.BlockSpec((step_rows, dp), lambda i, s: (jnp.minimum(i, last_used_step(s)), 0)),
            pl.BlockSpec(memory_space=pl.ANY),
            pl.BlockSpec(memory_space=pl.ANY),
            pl.BlockSpec(memory_space=pl.ANY),
        ],
        out_specs=pl.BlockSpec((step_rows, dp), lambda i, s: (i, 0)),
        scratch_shapes=[
            pltpu.VMEM((EXPERT_STAGES, d, de), jnp.float32),
            pltpu.VMEM((EXPERT_STAGES, d, de), jnp.float32),
            pltpu.VMEM((EXPERT_STAGES, de, d), jnp.float32),
            pltpu.VMEM((d, de), jnp.bfloat16),
            pltpu.VMEM((d, de), jnp.bfloat16),
            pltpu.VMEM((de, d), jnp.bfloat16),
            pltpu.SemaphoreType.DMA((EXPERT_STAGES, 3)),
        ],
    )
    return pl.pallas_call(
        functools.partial(_expert_kernel, layer=layer, nb=nb),
        grid_spec=grid_spec,
        out_shape=jax.ShapeDtypeStruct((nb * MOE_BLOCK, dp), jnp.uint32),
        compiler_params=_params(("arbitrary",)),
        name="moe_experts",
    )(blk_meta, x_slots, w1, w3, w2)


def _combine_kernel(x_ref, y1_ref, y2_ref, info_ref, gt_ref, o_ref):
    o_ref[...] = _moe_combined(x_ref[...], y1_ref, y2_ref, info_ref, gt_ref)


def _combine(x2d, y_pairs, info, gt, seq):
    t, d = x2d.shape
    tm = min(COMBINE_TILE, seq)
    tiles_per_batch = seq // tm
    second = t // tm
    return pl.pallas_call(
        _combine_kernel,
        grid=(t // tm,),
        in_specs=[
            pl.BlockSpec((tm, d), lambda i: (i, 0)),
            pl.BlockSpec((tm, d // 2), lambda i: (i, 0)),
            pl.BlockSpec((tm, d // 2), lambda i: (i + second, 0)),
            pl.BlockSpec((tm, ROUTER_LANES), lambda i: (i, 0)),
            pl.BlockSpec((None, 1, d), lambda i: (i // tiles_per_batch, 0, 0)),
        ],
        out_specs=pl.BlockSpec((tm, d), lambda i: (i, 0)),
        out_shape=jax.ShapeDtypeStruct((t, d), jnp.float32),
        compiler_params=_params(("arbitrary",)),
        name="moe_combine",
    )(x2d, y_pairs, y_pairs, info, gt)


def _slot_plan(info_t, cnt, t):
    counts = cnt[:, 0].astype(jnp.int32)
    padded = (counts + MOE_BLOCK - 1) // MOE_BLOCK * MOE_BLOCK
    pad_ends = jnp.cumsum(padded)
    pad_starts = pad_ends - padded
    nb = -(-(2 * t) // MOE_BLOCK) + N_EXPERTS
    n_slots = nb * MOE_BLOCK
    it = info_t.astype(jnp.int32)
    onehot_start = lambda e: jnp.sum(
        jnp.where(e[None, :] == jnp.arange(N_EXPERTS, dtype=jnp.int32)[:, None],
                  pad_starts[:, None], 0), axis=0)
    dest1 = (onehot_start(it[0]) + it[2]).reshape(1, t)
    dest2 = (onehot_start(it[1]) + it[3]).reshape(1, t)
    lane = jnp.arange(MOE_BLOCK, dtype=jnp.int32)[None, :]
    n_padding = (padded - counts)[:, None]
    wrapped = (pad_starts + counts)[:, None] + lane % jnp.maximum(n_padding, 1)
    pad_idx = jnp.where(n_padding > 0, wrapped, n_slots + lane % SC_WINDOW).reshape(-1)
    n_real = pad_ends[-1] // MOE_BLOCK
    n_used = -(-n_real // EXPERT_BLOCKS_PER_STEP) * EXPERT_BLOCKS_PER_STEP
    tail = jnp.arange((EXPERT_BLOCKS_PER_STEP - 1) * MOE_BLOCK, dtype=jnp.int32)
    tail_idx = jnp.where(tail < (n_used - n_real) * MOE_BLOCK, pad_ends[-1] + tail,
                         n_slots + tail % SC_WINDOW)
    pad_idx = jnp.concatenate([pad_idx, tail_idx]).reshape(1, -1)
    experts = jnp.arange(N_EXPERTS, dtype=jnp.int32)
    blk = jnp.arange(nb, dtype=jnp.int32)
    blk_exp = jnp.minimum(
        jnp.sum((pad_ends[None, :] <= (blk * MOE_BLOCK)[:, None]).astype(jnp.int32), axis=1),
        N_EXPERTS - 1)
    blk_exp = jnp.where(blk >= n_real, jnp.max(jnp.where(padded > 0, experts, 0)), blk_exp)
    prev_exp = jnp.concatenate([jnp.full((1,), -1, jnp.int32), blk_exp[:-1]])
    is_first = (blk < n_used) & (blk_exp != prev_exp)
    blk_slot = (jnp.cumsum(is_first.astype(jnp.int32)) + EXPERT_STAGES - 1) % EXPERT_STAGES
    later = (experts[None, :] > experts[:, None]) & (padded[None, :] > 0)
    nxt_of = jnp.min(jnp.where(later, experts[None, :], N_EXPERTS), axis=1)
    nxt_of = jnp.where(nxt_of == N_EXPERTS, -1, nxt_of)
    lookup = lambda table, e: jnp.sum(
        jnp.where(e[:, None] == experts[None, :], table[None, :], 0), axis=1)
    nxt2_of = jnp.where(nxt_of >= 0, lookup(nxt_of, jnp.maximum(nxt_of, 0)), -1)
    blk_nxt = lookup(nxt2_of, blk_exp)
    second = lookup(nxt_of, blk_exp[:1])
    blk_meta = jnp.concatenate([blk_exp, blk_slot, blk_nxt, n_used[None], second]).astype(jnp.int32)
    return dest1, dest2, pad_idx, blk_meta, nb, n_slots


def _mixer_out_and_moe(a, w_out, x2d, gt1, g, sc, sh, w_group, b_group, w_router, b_router,
                       w1, w3, w2, layer, seq):
    t, d = x2d.shape
    w_cat = jnp.zeros((ROUTER_LANES, d), jnp.float32)
    w_cat = w_cat.at[:N_EXPERTS].set(w_router.T).at[N_EXPERTS:N_EXPERTS + N_GROUPS].set(w_group.T)
    b_cat = jnp.zeros((ROUTER_LANES, 1), jnp.float32)
    b_cat = b_cat.at[:N_EXPERTS, 0].set(b_router).at[N_EXPERTS:N_EXPERTS + N_GROUPS, 0].set(b_group)
    w_hi, w_lo = _split_hi_lo(w_cat)
    x_new, hn, info, info_t, cnt = _outproj_router(a, w_out, x2d, gt1, g, sc, sh, w_hi, w_lo,
                                                   b_cat, seq)
    dest1, dest2, pad_idx, blk_meta, nb, n_slots = _slot_plan(info_t, cnt, t)
    x_slots = _sc_dispatch(hn, dest1, dest2, pad_idx, n_slots)
    y_slots = _experts(x_slots, blk_meta, w1, w3, w2, layer, nb)
    y_pairs = _sc_gather(y_slots, jnp.concatenate([dest1, dest2], axis=1))
    return x_new, y_pairs, info


def kernel(x, c, w_ada, b_ada, norm1_g, norm2_g, ml_w_in, ml_b_gate, ml_g_out, ml_w_out,
           sw_w_in, sw_g_q, sw_g_k, sw_sinks, sw_w_out, moe_w_group, moe_b_group,
           moe_w_router, moe_b_router, moe_w1, moe_w3, moe_w2):
    bsz, seq, d = x.shape
    depth = w_ada.shape[0]
    bf = jnp.bfloat16
    mod = _ada_mod(c, w_ada, b_ada)
    x2d = x.reshape(bsz * seq, d)
    pending = None
    for layer in range(depth):
        sh1, sc1, gt1, sh2, sc2, gt2 = [
            mod[layer, :, i * d:(i + 1) * d].reshape(bsz, 1, d) for i in range(6)]
        j = layer // 2
        if layer % 2 == 0:
            w = ml_w_in[j]
            q_w, k_w = w[:, :ML_QK], w[:, ML_QK:2 * ML_QK]
            v_w = w[:, 2 * ML_QK:2 * ML_QK + ML_V]
            o_w = w[:, 2 * ML_QK + ML_V:2 * ML_QK + 2 * ML_V]
            g_w = w[:, 2 * ML_QK + 2 * ML_V:]
            w_main = jnp.concatenate([v_w, o_w, q_w], axis=1).astype(bf)
            wg_t = jnp.zeros((ML_GATE_ROWS, d), jnp.float32).at[:2 * ML_HEADS].set(g_w.T)
            wg_hi, wg_lo = _split_hi_lo(wg_t)
            wk_t = k_w.T.astype(bf)
            w_out = ml_w_out[j].astype(bf)
            outs = _inproj(x2d, norm1_g[layer], sc1, sh1, w_main, seq,
                           ml_extra=(wk_t, wg_hi, wg_lo),
                           q_cols=(2 * ML_V, 2 * ML_V + ML_QK), q_scale=ML_DK ** -0.5,
                           gate_cols=(ML_V, 2 * ML_V), pending=pending)
            if pending is not None:
                x2d, outs = outs[0], outs[1:]
            main, k_t, g_t = outs
            a = _mlstm_core(main, k_t, g_t, ml_b_gate[j], ml_g_out[j], bsz, seq)
        else:
            w = sw_w_in[j]
            dq = SW_Q_HEADS * SW_DH
            dkv = SW_KV_HEADS * SW_DH
            dup = lambda m: jnp.concatenate(
                [m.reshape(d, SW_KV_HEADS, 1, SW_DH)] * 2, axis=2).reshape(d, 2 * dkv)
            w_main = jnp.concatenate(
                [w[:, :dq], dup(w[:, dq:dq + dkv]), dup(w[:, dq + dkv:])], axis=1).astype(bf)
            w_out = sw_w_out[j].astype(bf)
            gq = jnp.concatenate([sw_g_q[j], sw_g_q[j]]).reshape(1, LANES) * (SW_DH ** -0.5 * LOG2E)
            gk = jnp.concatenate([sw_g_k[j], sw_g_k[j]]).reshape(1, LANES)
            outs = _inproj(x2d, norm1_g[layer], sc1, sh1, w_main, seq, pending=pending,
                           qk_norm=(dq, dq + 2 * dkv, gq, gk))
            if pending is not None:
                x2d, outs = outs[0], outs[1:]
            a = _swa_core(outs[0], sw_sinks[j], bsz, seq)
        x2d, y_pairs, info = _mixer_out_and_moe(
            a, w_out, x2d, gt1, norm2_g[layer], sc2, sh2, moe_w_group[layer], moe_b_group[layer],
            moe_w_router[layer], moe_b_router[layer], moe_w1, moe_w3, moe_w2, layer, seq)
        pending = (y_pairs, info, gt2)
    y_pairs, info, gt2 = pending
    return _combine(x2d, y_pairs, info, gt2, seq).reshape(bsz, seq, d)
```

```python
import functools

import jax
import jax.numpy as jnp
from jax import lax
from jax.experimental import pallas as pl
from jax.experimental.pallas import tpu as pltpu
from jax.experimental.pallas import tpu_sc as plsc

EPS = 1e-6
GATE_CAP = 15.0
LOG2E = 1.4426950408889634

ML_HEADS = 4
ML_DK = 128
ML_DV = 256
ML_QK = ML_HEADS * ML_DK
ML_V = ML_HEADS * ML_DV
ML_GATE_ROWS = 16

SW_Q_HEADS = 16
SW_KV_HEADS = 4
SW_GROUP = SW_Q_HEADS // SW_KV_HEADS
SW_DH = 64
SW_WINDOW = 128
LANES = 128

N_GROUPS = 8
EXPERTS_PER_GROUP = 8
N_EXPERTS = N_GROUPS * EXPERTS_PER_GROUP
MOE_BLOCK = 256
ROUTER_LANES = 128
SC_WINDOW = 128
SC_ROWS = 64
EXPERT_BLOCKS_PER_STEP = 4
EXPERT_STAGES = 3

SUBLANES = 8
TOKEN_TILE = 512
ROUTER_TILE = 1024
COMBINE_TILE = 1024
PROJ_COL_CHUNK = 1024
ML_CHUNK = 256
ADA_COL_TILE = 768
V7X_VMEM_BYTES = 64 * 1024 * 1024
VMEM_LIMIT = V7X_VMEM_BYTES - 8 * 1024 * 1024

_NT = (((1,), (1,)), ((), ()))


def _bdot(a, b):
    return jnp.dot(a, b, preferred_element_type=jnp.float32)


def _bdot_nt(a, b):
    return lax.dot_general(a, b, _NT, preferred_element_type=jnp.float32)


def _split_hi_lo(a):
    hi = a.astype(jnp.bfloat16)
    lo = (a - hi.astype(jnp.float32)).astype(jnp.bfloat16)
    return hi, lo


def _params(sem):
    return pltpu.CompilerParams(dimension_semantics=sem, vmem_limit_bytes=VMEM_LIMIT)


def _ada_kernel(c_ref, w_ref, b_ref, o_ref):
    c = c_ref[...]
    cond = c * jax.nn.sigmoid(c)
    c_hi, c_lo = _split_hi_lo(cond)
    w_hi, w_lo = _split_hi_lo(w_ref[...])
    acc = _bdot(c_hi, w_hi) + (_bdot(c_lo, w_hi) + _bdot(c_hi, w_lo))
    o_ref[...] = acc + b_ref[...]


def _ada_mod(c, w_ada, b_ada):
    depth, d, n = w_ada.shape
    bsz = c.shape[0]
    rows = SUBLANES
    tn = ADA_COL_TILE
    c_pad = jnp.zeros((rows, d), jnp.float32).at[:bsz].set(c)
    out = pl.pallas_call(
        _ada_kernel,
        grid=(depth, n // tn),
        in_specs=[
            pl.BlockSpec((rows, d), lambda l, j: (0, 0)),
            pl.BlockSpec((None, d, tn), lambda l, j: (l, 0, j)),
            pl.BlockSpec((None, 1, tn), lambda l, j: (l, 0, j)),
        ],
        out_specs=pl.BlockSpec((None, rows, tn), lambda l, j: (l, 0, j)),
        out_shape=jax.ShapeDtypeStruct((depth, rows, n), jnp.float32),
        compiler_params=_params(("arbitrary", "arbitrary")),
        name="ada_mod",
    )(c_pad, w_ada, b_ada.reshape(depth, 1, n))
    return out[:, :bsz]


def _modulated_norm(x, g, sc, sh):
    y = x * lax.rsqrt(jnp.mean(x * x, axis=-1, keepdims=True) + EPS)
    return y * (g * (1.0 + sc)) + sh


def _moe_combined(x, y1_ref, y2_ref, info_ref, gt_ref):
    info = info_ref[...]
    g1 = info[:, 4:5]
    g2 = info[:, 5:6]
    y1_hi, y1_lo = _unpack_bf16_pairs(y1_ref[...])
    y2_hi, y2_lo = _unpack_bf16_pairs(y2_ref[...])
    y = jnp.concatenate([g1 * y1_hi + g2 * y2_hi, g1 * y1_lo + g2 * y2_lo], axis=-1)
    return x + gt_ref[...] * y


def _qk_head_norm(acc, c0, qk_norm, gq, gk):
    q_hi, k_hi = qk_norm
    low = lax.broadcasted_iota(jnp.int32, (acc.shape[0], LANES), 1) < SW_DH
    slabs = []
    for j in range(acc.shape[1] // LANES):
        slab = acc[:, j * LANES:(j + 1) * LANES]
        if c0 + j * LANES < q_hi:
            sq = slab * slab
            ss_lo = jnp.sum(jnp.where(low, sq, 0.0), axis=-1, keepdims=True)
            ss_hi = jnp.sum(jnp.where(low, 0.0, sq), axis=-1, keepdims=True)
            rs = jnp.where(low, lax.rsqrt(ss_lo / SW_DH + EPS), lax.rsqrt(ss_hi / SW_DH + EPS))
            slab = slab * rs * gq
        elif c0 + j * LANES < k_hi:
            slab = slab * lax.rsqrt(jnp.mean(slab * slab, axis=-1, keepdims=True) + EPS) * gk
        slabs.append(slab)
    return jnp.concatenate(slabs, axis=-1)


def _inproj_kernel(*refs, n_main, chunk, q_cols, q_scale, gate_cols, qk_norm, with_ml,
                   with_combine):
    refs = list(refs)
    n_in = (5 + (4 if with_combine else 0) + (2 if with_ml else 0)
            + (2 if qk_norm is not None else 0))
    n_scratch = 4 if with_ml else 2
    ins, outs, scratch = refs[:n_in], refs[n_in:-n_scratch], refs[-n_scratch:]
    x_ref = ins.pop(0)
    if with_combine:
        y1_ref, y2_ref, info_ref, gtp_ref = ins[:4]
        ins = ins[4:]
        xo_ref = outs.pop(0)
    g_ref, sc_ref, sh_ref, w_ref = ins[:4]
    o_ref = outs[0]
    if with_ml:
        wk_ref, wgh_ref = ins[4:]
        kt_ref, gt_ref = outs[1:]
    if qk_norm is not None:
        gq_ref, gk_ref = ins[4:]

    def normalise(hb_dst, lo_dst):
        x = x_ref[...]
        if with_combine:
            x = _moe_combined(x, y1_ref, y2_ref, info_ref, gtp_ref)
            xo_ref[...] = x
        hn = _modulated_norm(x, g_ref[...], sc_ref[...], sh_ref[...])
        hb = hn.astype(jnp.bfloat16)
        hb_dst[...] = hb
        if with_ml:
            lo_dst[...] = (hn - hb.astype(jnp.float32)).astype(jnp.bfloat16)

    def project(hb_src, lo_src):
        hb = hb_src[...]
        for c0 in range(0, n_main, chunk):
            c1 = min(c0 + chunk, n_main)
            acc = _bdot(hb, w_ref[:, c0:c1])
            if q_cols is not None and q_cols[0] <= c0 < q_cols[1]:
                acc = acc * q_scale
            if gate_cols is not None and gate_cols[0] <= c0 < gate_cols[1]:
                acc = jax.nn.sigmoid(acc)
            if qk_norm is not None and c0 < qk_norm[1]:
                acc = _qk_head_norm(acc, c0, qk_norm, gq_ref[...], gk_ref[...])
            o_ref[:, c0:c1] = acc.astype(o_ref.dtype)
        if with_ml:
            nk = kt_ref.shape[0]
            stacked = _bdot_nt(wk_ref[...], hb)
            kt_ref[...] = stacked[:nk].astype(kt_ref.dtype)
            gt_ref[...] = (stacked[nk:nk + ML_GATE_ROWS]
                           + (_bdot_nt(wgh_ref[...], lo_src[...]) + stacked[nk + ML_GATE_ROWS:]))

    hb_a, hb_b = scratch[:2]
    lo_a, lo_b = scratch[2:] if with_ml else (None, None)
    s = pl.program_id(0)

    @pl.when(s == 0)
    def _():
        hb_b[...] = jnp.zeros_like(hb_b)
        if with_ml:
            lo_b[...] = jnp.zeros_like(lo_b)

    @pl.when(s % 2 == 0)
    def _():
        normalise(hb_a, lo_a)
        project(hb_b, lo_b)

    @pl.when(s % 2 == 1)
    def _():
        normalise(hb_b, lo_b)
        project(hb_a, lo_a)


def _inproj(x2d, g, sc, sh, w_main, seq, *, ml_extra=None, q_cols=None, q_scale=1.0,
            gate_cols=None, qk_norm=None, pending=None):
    t, d = x2d.shape
    tm = TOKEN_TILE if ml_extra is not None else min(2 * TOKEN_TILE, seq)
    n_main = w_main.shape[1]
    tiles_per_batch = seq // tm
    n_tiles = t // tm
    resident = pl.Buffered(1)
    norm_tile = lambda s: jnp.minimum(s, n_tiles - 1)
    proj_tile = lambda s: jnp.maximum(s - 1, 0)
    row = lambda s: (norm_tile(s), 0)
    per_batch = lambda s: (norm_tile(s) // tiles_per_batch, 0, 0)
    const = lambda s: (0, 0)
    in_specs = [pl.BlockSpec((tm, d), row)]
    args = [x2d]
    out_specs, out_shape = [], []
    if pending is not None:
        y_pairs, info, gt_prev = pending
        in_specs += [pl.BlockSpec((tm, d // 2), row),
                     pl.BlockSpec((tm, d // 2), lambda s: (norm_tile(s) + n_tiles, 0)),
                     pl.BlockSpec((tm, ROUTER_LANES), row),
                     pl.BlockSpec((None, 1, d), per_batch)]
        args += [y_pairs, y_pairs, info, gt_prev]
        out_specs += [pl.BlockSpec((tm, d), row)]
        out_shape += [jax.ShapeDtypeStruct((t, d), jnp.float32)]
    in_specs += [
        pl.BlockSpec((1, d), const),
        pl.BlockSpec((None, 1, d), per_batch),
        pl.BlockSpec((None, 1, d), per_batch),
        pl.BlockSpec((d, n_main), const, pipeline_mode=resident),
    ]
    args += [g.reshape(1, d), sc, sh, w_main]
    out_specs += [pl.BlockSpec((tm, n_main), lambda s: (proj_tile(s), 0))]
    out_shape += [jax.ShapeDtypeStruct((t, n_main), jnp.bfloat16)]
    scratch = [pltpu.VMEM((tm, d), jnp.bfloat16), pltpu.VMEM((tm, d), jnp.bfloat16)]
    if ml_extra is not None:
        wk_t, wg_hi, wg_lo = ml_extra
        nk = wk_t.shape[0]
        stacked = jnp.concatenate([wk_t, wg_hi, wg_lo], axis=0)
        in_specs += [pl.BlockSpec(stacked.shape, const, pipeline_mode=resident),
                     pl.BlockSpec(wg_hi.shape, const, pipeline_mode=resident)]
        args += [stacked, wg_hi]
        out_specs += [pl.BlockSpec((nk, tm), lambda s: (0, proj_tile(s))),
                      pl.BlockSpec((ML_GATE_ROWS, tm), lambda s: (0, proj_tile(s)))]
        out_shape += [jax.ShapeDtypeStruct((nk, t), jnp.bfloat16),
                      jax.ShapeDtypeStruct((ML_GATE_ROWS, t), jnp.float32)]
        scratch += [pltpu.VMEM((tm, d), jnp.bfloat16), pltpu.VMEM((tm, d), jnp.bfloat16)]
    if qk_norm is not None:
        in_specs += [pl.BlockSpec((1, LANES), const), pl.BlockSpec((1, LANES), const)]
        args += [qk_norm[2], qk_norm[3]]
    kern = functools.partial(_inproj_kernel, n_main=n_main, chunk=PROJ_COL_CHUNK, q_cols=q_cols,
                             q_scale=q_scale, gate_cols=gate_cols,
                             qk_norm=None if qk_norm is None else qk_norm[:2],
                             with_ml=ml_extra is not None, with_combine=pending is not None)
    return pl.pallas_call(
        kern,
        grid=(n_tiles + 1,),
        in_specs=in_specs,
        out_specs=out_specs,
        out_shape=out_shape,
        scratch_shapes=scratch,
        compiler_params=_params(("arbitrary",)),
        name="inproj_ml" if ml_extra is not None else "inproj_sw",
    )(*args)


def _mlstm_gate_terms(graw, bias, upper):
    H = ML_HEADS
    L = graw.shape[1]
    z = graw + bias
    gates = GATE_CAP * jnp.tanh(z / GATE_CAP)
    log_f = jnp.minimum(gates, 0.0) - jnp.log1p(jnp.exp(-jnp.abs(gates)))
    row = lax.broadcasted_iota(jnp.int32, (ML_GATE_ROWS, L), 0)
    lane = lax.broadcasted_iota(jnp.int32, (ML_GATE_ROWS, L), 1)
    is_i = row < H
    slab = jnp.where(is_i, gates, log_f)
    a1 = slab.astype(jnp.bfloat16)
    r1 = slab - a1.astype(jnp.float32)
    a2 = r1.astype(jnp.bfloat16)
    a3 = (r1 - a2.astype(jnp.float32)).astype(jnp.bfloat16)
    cum = _bdot(a1, upper) + (_bdot(a2, upper) + _bdot(a3, upper))
    ib = jnp.where(is_i, gates, cum) * LOG2E
    b = pltpu.roll(ib, ML_GATE_ROWS - H, 0)
    u = ib - b
    cm = u
    shift = 1
    while shift < L:
        cm = jnp.maximum(cm, jnp.where(lane >= shift, pltpu.roll(cm, shift, 1), -jnp.inf))
        shift *= 2
    return b, u, cm


def _mlstm_kernel(v_ref, o_ref, q_ref, kt_ref, gt_ref, gtn_ref, bg_ref, gout_ref, out_ref,
                  c_ref, m_ref, b_ref, u_ref, cm_ref, *, chunk):
    L = chunk
    H, dk, dv = ML_HEADS, ML_DK, ML_DV
    r_idx = lax.broadcasted_iota(jnp.int32, (L, L), 0)
    c_idx = lax.broadcasted_iota(jnp.int32, (L, L), 1)
    upper = jnp.where(r_idx <= c_idx, 1.0, 0.0).astype(jnp.bfloat16)
    causal = r_idx >= c_idx
    row = lax.broadcasted_iota(jnp.int32, (ML_GATE_ROWS, L), 0)
    ones_col = jnp.where(lax.broadcasted_iota(jnp.int32, (L, LANES), 1) == 0, 1.0, 0.0
                         ).astype(jnp.bfloat16)

    @pl.when(pl.program_id(1) == 0)
    def _():
        c_ref[...] = jnp.zeros_like(c_ref)
        m_ref[...] = jnp.zeros_like(m_ref)
        b0, u0, cm0 = _mlstm_gate_terms(gt_ref[...], bg_ref[...], upper)
        b_ref[...] = b0
        u_ref[...] = u0
        cm_ref[...] = cm0

    b16 = b_ref[...]
    u16 = u_ref[...]
    cm16 = cm_ref[...]
    b_n, u_n, cm_n = _mlstm_gate_terms(gtn_ref[...], bg_ref[...], upper)
    b_ref[...] = b_n
    u_ref[...] = u_n
    cm_ref[...] = cm_n

    m_prev = m_ref[:, 0:1]
    z16 = jnp.maximum(m_prev, cm16)
    w_inter16 = jnp.exp2(m_prev - z16)
    e_negm16 = jnp.exp2(-(b16 + z16))
    z_last = z16[:, L - 1:L]
    w_state16 = jnp.exp2(u16 - z_last)
    decay16 = jnp.exp2(m_prev - z_last)
    m_ref[...] = jnp.broadcast_to(b16[:, L - 1:L] + z_last, m_ref.shape)
    stacked = jnp.where(row < H, z16,
                        jnp.where(row < 2 * H, pltpu.roll(w_inter16, H, 0),
                                  pltpu.roll(e_negm16, 2 * H, 0)))
    cols = jnp.concatenate(
        [stacked, jnp.zeros((LANES - ML_GATE_ROWS, L), jnp.float32)], axis=0).T

    for h in range(H):
        u_r = u16[h:h + 1, :]
        z_c = cols[:, h:h + 1]
        w_inter = cols[:, H + h:H + h + 1]
        e_negm = cols[:, 2 * H + h:2 * H + h + 1]
        c_ext = c_ref[h]
        q = q_ref[:, h * dk:(h + 1) * dk]
        kt = kt_ref[h * dk:(h + 1) * dk, :]
        v_ext = jnp.concatenate([v_ref[:, h * dv:(h + 1) * dv], ones_col], axis=-1)

        w_intra = jnp.exp2(jnp.where(causal, u_r - z_c, -jnp.inf))
        s = (_bdot(q, kt) * w_intra).astype(jnp.bfloat16)
        qw = (q.astype(jnp.float32) * w_inter).astype(jnp.bfloat16)
        nd = _bdot(qw, c_ext.astype(jnp.bfloat16)) + _bdot(s, v_ext)
        den = nd[:, dv:dv + 1]
        hb = nd[:, :dv] * (1.0 / jnp.maximum(jnp.abs(den), e_negm))

        kw = (kt.astype(jnp.float32) * w_state16[h:h + 1, :]).astype(jnp.bfloat16)
        c_ref[h] = decay16[h:h + 1, :] * c_ext + _bdot(kw, v_ext)

        y = hb * lax.rsqrt(jnp.mean(hb * hb, axis=-1, keepdims=True) + EPS)
        y = y * gout_ref[:, h * dv:(h + 1) * dv]
        og = o_ref[:, h * dv:(h + 1) * dv].astype(jnp.float32)
        out_ref[:, h * dv:(h + 1) * dv] = (y * og).astype(out_ref.dtype)


def _mlstm_core(main, k_t, g_t, b_gate, g_out, bsz, seq):
    chunk = ML_CHUNK
    t = main.shape[0]
    nc = seq // chunk
    blk = lambda b, c: b * nc + c
    bg = jnp.zeros((ML_GATE_ROWS, 1), jnp.float32).at[:2 * ML_HEADS, 0].set(b_gate)
    return pl.pallas_call(
        functools.partial(_mlstm_kernel, chunk=chunk),
        grid=(bsz, nc),
        in_specs=[
            pl.BlockSpec((chunk, ML_V), lambda b, c: (blk(b, c), 0)),
            pl.BlockSpec((chunk, ML_V), lambda b, c: (blk(b, c), 1)),
            pl.BlockSpec((chunk, ML_QK), lambda b, c: (blk(b, c), 4)),
            pl.BlockSpec((ML_QK, chunk), lambda b, c: (0, blk(b, c))),
            pl.BlockSpec((ML_GATE_ROWS, chunk), lambda b, c: (0, blk(b, c))),
            pl.BlockSpec((ML_GATE_ROWS, chunk), lambda b, c: (0, blk(b, jnp.minimum(c + 1, nc - 1)))),
            pl.BlockSpec((ML_GATE_ROWS, 1), lambda b, c: (0, 0)),
            pl.BlockSpec((1, ML_V), lambda b, c: (0, 0)),
        ],
        out_specs=pl.BlockSpec((chunk, ML_V), lambda b, c: (blk(b, c), 0)),
        out_shape=jax.ShapeDtypeStruct((t, ML_V), jnp.bfloat16),
        scratch_shapes=[
            pltpu.VMEM((ML_HEADS, ML_DK, ML_DV + LANES), jnp.float32),
            pltpu.VMEM((ML_GATE_ROWS, LANES), jnp.float32),
            pltpu.VMEM((ML_GATE_ROWS, chunk), jnp.float32),
            pltpu.VMEM((ML_GATE_ROWS, chunk), jnp.float32),
            pltpu.VMEM((ML_GATE_ROWS, chunk), jnp.float32),
        ],
        compiler_params=_params(("arbitrary", "arbitrary")),
        name="mlstm_core",
    )(main, main, main, k_t, g_t, g_t, bg, g_out.reshape(1, ML_V))


def _swa_kernel(q_ref, kc_ref, kp_ref, vc_ref, vp_ref, bias_ref, o_ref):
    W = SW_WINDOW
    lane = lax.broadcasted_iota(jnp.int32, (W, LANES), 1)
    low = lane < SW_DH
    not_first_row = lax.broadcasted_iota(jnp.int32, (W, LANES), 0) > 0
    variant = jnp.minimum(pl.program_id(1), 1)
    for g in range(SW_KV_HEADS):
        sl = slice(g * LANES, (g + 1) * LANES)
        k_prev = jnp.where(not_first_row, kp_ref[:, sl].astype(jnp.float32), 0.0)
        v_prev = jnp.where(not_first_row, vp_ref[:, sl].astype(jnp.float32), 0.0)
        kn = jnp.concatenate([k_prev.astype(jnp.bfloat16), kc_ref[:, sl]], axis=0)
        v2 = jnp.concatenate([v_prev.astype(jnp.bfloat16), vc_ref[:, sl]], axis=0)
        parts = []
        for p in range(2):
            c0 = g * SW_GROUP * SW_DH + p * LANES
            qp = q_ref[:, c0:c0 + LANES].astype(jnp.float32)
            parts.append(jnp.where(low, qp, 0.0).astype(jnp.bfloat16))
            parts.append(jnp.where(low, 0.0, qp).astype(jnp.bfloat16))
        q4 = jnp.concatenate(parts, axis=0)
        scores = _bdot_nt(q4, kn) + bias_ref[variant, g]
        m = jnp.max(scores, axis=-1, keepdims=True)
        pexp = jnp.exp2(scores - m)
        denom = jnp.sum(pexp, axis=-1, keepdims=True)
        o4 = _bdot(pexp.astype(jnp.bfloat16), v2) * (1.0 / denom)
        for p in range(2):
            oa = o4[(2 * p) * W:(2 * p + 1) * W]
            ob = o4[(2 * p + 1) * W:(2 * p + 2) * W]
            c0 = g * SW_GROUP * SW_DH + p * LANES
            o_ref[:, c0:c0 + LANES] = jnp.where(low, oa, ob).astype(o_ref.dtype)


def _swa_core(proj, sinks, bsz, seq):
    t = proj.shape[0]
    W = SW_WINDOW
    nb = seq // W
    dq = SW_Q_HEADS * SW_DH
    kv_w = SW_KV_HEADS * LANES
    k_blk = dq // kv_w
    v_blk = k_blk + 1
    cur = lambda b, n: b * nb + n
    prev = lambda b, n: b * nb + jnp.maximum(n - 1, 0)
    qi = (jnp.arange(SW_GROUP * W) % W)[:, None]
    ki = jnp.arange(2 * W)[None, :]
    rel = qi + W - ki
    in_win = (rel >= 0) & (rel < W)
    window = jnp.stack([jnp.where(in_win & (ki >= W), 0.0, -jnp.inf),
                        jnp.where(in_win, 0.0, -jnp.inf)]).astype(jnp.float32)
    sink_rows = jnp.repeat(sinks.astype(jnp.float32).reshape(SW_KV_HEADS, SW_GROUP) * LOG2E, W, axis=1)
    bias = jnp.where(ki[None, None] == 0, sink_rows[None, :, :, None], window[:, None])
    return pl.pallas_call(
        _swa_kernel,
        grid=(bsz, nb),
        in_specs=[
            pl.BlockSpec((W, dq), lambda b, n: (cur(b, n), 0)),
            pl.BlockSpec((W, kv_w), lambda b, n: (cur(b, n), k_blk)),
            pl.BlockSpec((W, kv_w), lambda b, n: (prev(b, n), k_blk)),
            pl.BlockSpec((W, kv_w), lambda b, n: (cur(b, n), v_blk)),
            pl.BlockSpec((W, kv_w), lambda b, n: (prev(b, n), v_blk)),
            pl.BlockSpec(bias.shape, lambda b, n: (0, 0, 0, 0)),
        ],
        out_specs=pl.BlockSpec((W, dq), lambda b, n: (cur(b, n), 0)),
        out_shape=jax.ShapeDtypeStruct((t, dq), jnp.bfloat16),
        compiler_params=_params(("arbitrary", "arbitrary")),
        name="swa_core",
    )(proj, proj, proj, proj, proj, bias)


def _pack_rounded_pairs(r):
    k = r.shape[1] // 2
    hi = lax.bitcast_convert_type(r[:, :k], jnp.uint32)
    lo = lax.bitcast_convert_type(r[:, k:], jnp.uint32)
    return hi | (lo >> 16)


def _pack_bf16_pairs(a):
    return _pack_rounded_pairs(a.astype(jnp.bfloat16).astype(jnp.float32))


def _unpack_bf16_pairs(u):
    hi = lax.bitcast_convert_type(u & jnp.uint32(0xFFFF0000), jnp.float32)
    lo = lax.bitcast_convert_type(u << 16, jnp.float32)
    return hi, lo


def _route_tile(x_new, g_ref, sc_ref, sh_ref, wh_ref, wl_ref, b_ref,
                hn_ref, info_ref, infot_ref, cnt_ref, carry_ref, earlier_ref, tm):
    hn = _modulated_norm(x_new, g_ref[...], sc_ref[...], sh_ref[...])
    h_hi = hn.astype(jnp.bfloat16)
    hi_f32 = h_hi.astype(jnp.float32)
    h_lo = (hn - hi_f32).astype(jnp.bfloat16)
    hn_ref[...] = _pack_rounded_pairs(hi_f32)
    wide = _bdot_nt(wl_ref[...], h_hi)
    logits = (wide[:ROUTER_LANES] + (_bdot_nt(wh_ref[...], h_lo) + wide[ROUTER_LANES:])
              + b_ref[...])
    E8 = EXPERTS_PER_GROUP
    sub = lax.broadcasted_iota(jnp.int32, (E8, tm), 0).astype(jnp.float32)
    big = float(ROUTER_LANES)
    neg = -jnp.inf

    gl = logits[N_EXPERTS:N_EXPERTS + N_GROUPS]
    gmax = jnp.max(gl, axis=0, keepdims=True)
    gsel = jnp.min(jnp.where(gl == gmax, sub, big), axis=0, keepdims=True)
    p_grp = 1.0 / jnp.sum(jnp.exp(gl - gmax), axis=0, keepdims=True)

    el = logits[0:E8]
    for grp in range(1, N_GROUPS):
        el = jnp.where(gsel == grp, logits[grp * E8:(grp + 1) * E8], el)
    v1 = jnp.max(el, axis=0, keepdims=True)
    j1 = jnp.min(jnp.where(el == v1, sub, big), axis=0, keepdims=True)
    el2 = jnp.where(sub == j1, neg, el)
    v2 = jnp.max(el2, axis=0, keepdims=True)
    j2 = jnp.min(jnp.where(el2 == v2, sub, big), axis=0, keepdims=True)
    i1 = gsel * E8 + j1
    i2 = gsel * E8 + j2
    e21 = jnp.exp(v2 - v1)
    gate1 = p_grp / (1.0 + e21)
    gate2 = p_grp * e21 / (1.0 + e21)

    erow = lax.broadcasted_iota(jnp.int32, (N_EXPERTS, tm), 0).astype(jnp.float32)
    hit1 = erow == i1
    hit2 = erow == i2
    onehot = jnp.where(hit1 | hit2, 1.0, 0.0)
    carry = carry_ref[:, 0:1]
    before = _bdot(onehot.astype(jnp.bfloat16), earlier_ref[...]) + carry
    rank1 = jnp.sum(jnp.where(hit1, before, 0.0), axis=0, keepdims=True)
    rank2 = jnp.sum(jnp.where(hit2, before, 0.0), axis=0, keepdims=True)
    total = carry + jnp.sum(onehot, axis=1, keepdims=True)
    carry_ref[...] = jnp.broadcast_to(total, carry_ref.shape)
    cnt_ref[...] = jnp.broadcast_to(total, cnt_ref.shape)

    info_t = jnp.where(sub == 0, i1, 0.0)
    info_t = jnp.where(sub == 1, i2, info_t)
    info_t = jnp.where(sub == 2, rank1, info_t)
    info_t = jnp.where(sub == 3, rank2, info_t)
    info_t = jnp.where(sub == 4, gate1, info_t)
    info_t = jnp.where(sub == 5, gate2, info_t)
    infot_ref[...] = info_t
    info_ref[...] = jnp.concatenate(
        [info_t, jnp.zeros((ROUTER_LANES - E8, tm), jnp.float32)], axis=0).T


def _router_kernel(a_ref, wo_ref, x_ref, gt_ref, g_ref, sc_ref, sh_ref, wh_ref, wl_ref, b_ref,
                   xo_ref, hn_ref, info_ref, infot_ref, cnt_ref, carry_ref, earlier_ref, *, tm):
    @pl.when(pl.program_id(0) == 0)
    def _():
        carry_ref[...] = jnp.zeros_like(carry_ref)
        r_idx = lax.broadcasted_iota(jnp.int32, (tm, tm), 0)
        c_idx = lax.broadcasted_iota(jnp.int32, (tm, tm), 1)
        earlier_ref[...] = jnp.where(r_idx < c_idx, 1.0, 0.0).astype(jnp.bfloat16)

    x_new = x_ref[...] + gt_ref[...] * _bdot(a_ref[...], wo_ref[...])
    xo_ref[...] = x_new
    _route_tile(x_new, g_ref, sc_ref, sh_ref, wh_ref, wl_ref, b_ref,
                hn_ref, info_ref, infot_ref, cnt_ref, carry_ref, earlier_ref, tm)


def _outproj_router(a, w_out, x2d, gt, g, sc, sh, w_hi, w_lo, bias, seq):
    t, d = x2d.shape
    tm = min(ROUTER_TILE, seq)
    tiles_per_batch = seq // tm
    per_batch = lambda i: (i // tiles_per_batch, 0, 0)
    const = lambda i: (0, 0)
    return pl.pallas_call(
        functools.partial(_router_kernel, tm=tm),
        grid=(t // tm,),
        in_specs=[
            pl.BlockSpec((tm, a.shape[1]), lambda i: (i, 0)),
            pl.BlockSpec(w_out.shape, const),
            pl.BlockSpec((tm, d), lambda i: (i, 0)),
            pl.BlockSpec((None, 1, d), per_batch),
            pl.BlockSpec((1, d), const),
            pl.BlockSpec((None, 1, d), per_batch),
            pl.BlockSpec((None, 1, d), per_batch),
            pl.BlockSpec((ROUTER_LANES, d), const),
            pl.BlockSpec((2 * ROUTER_LANES, d), const),
            pl.BlockSpec((ROUTER_LANES, 1), const),
        ],
        out_specs=[
            pl.BlockSpec((tm, d), lambda i: (i, 0)),
            pl.BlockSpec((tm, d // 2), lambda i: (i, 0)),
            pl.BlockSpec((tm, ROUTER_LANES), lambda i: (i, 0)),
            pl.BlockSpec((EXPERTS_PER_GROUP, tm), lambda i: (0, i)),
            pl.BlockSpec((N_EXPERTS, LANES), const),
        ],
        out_shape=[
            jax.ShapeDtypeStruct((t, d), jnp.float32),
            jax.ShapeDtypeStruct((t, d // 2), jnp.uint32),
            jax.ShapeDtypeStruct((t, ROUTER_LANES), jnp.float32),
            jax.ShapeDtypeStruct((EXPERTS_PER_GROUP, t), jnp.float32),
            jax.ShapeDtypeStruct((N_EXPERTS, LANES), jnp.float32),
        ],
        scratch_shapes=[pltpu.VMEM((N_EXPERTS, LANES), jnp.float32),
                        pltpu.VMEM((tm, tm), jnp.bfloat16)],
        compiler_params=_params(("arbitrary",)),
        name="outproj_router",
    )(a, w_out, x2d, gt, g.reshape(1, d), sc, sh, w_hi, jnp.concatenate([w_hi, w_lo], axis=0), bias)


def _sc_mesh():
    return plsc.VectorSubcoreMesh(core_axis_name="c", subcore_axis_name="s")


def _sc_dispatch(rows, d0, d1, pad_idx, n_slots):
    t, w = rows.shape
    n_pad = pad_idx.shape[1]
    zeros = jnp.zeros((SC_ROWS, w), rows.dtype)
    sem = (pltpu.PARALLEL, pltpu.ARBITRARY)
    parts = SC_WINDOW // SC_ROWS

    @pl.kernel(out_type=jax.ShapeDtypeStruct((n_slots + SC_WINDOW, w), rows.dtype), mesh=_sc_mesh(),
               scratch_types=[pltpu.SemaphoreType.DMA, pltpu.SemaphoreType.DMA])
    def dispatch(x_hbm, d0_hbm, d1_hbm, z_hbm, p_hbm, o_hbm, sem0, sem1):
        def scatter_rows(x_vmem, i0_vmem, i1_vmem):
            part = pl.ds(pl.program_id(1) * SC_ROWS, SC_ROWS)
            first = pltpu.async_copy(x_vmem, o_hbm.at[i0_vmem.at[0, part]], sem0)
            second = pltpu.async_copy(x_vmem, o_hbm.at[i1_vmem.at[0, part]], sem1)
            first.wait()
            second.wait()

        pltpu.emit_pipeline(
            scatter_rows,
            grid=(t // SC_WINDOW, parts),
            in_specs=[pl.BlockSpec((SC_ROWS, w), lambda i, j: (parts * i + j, 0)),
                      pl.BlockSpec((1, SC_WINDOW), lambda i, j: (0, i)),
                      pl.BlockSpec((1, SC_WINDOW), lambda i, j: (0, i))],
            out_specs=[],
            core_axis_name=("c", "s"),
            dimension_semantics=sem,
        )(x_hbm, d0_hbm, d1_hbm)

        def scatter_zeros(z_vmem, p_vmem):
            part = pl.ds(pl.program_id(1) * SC_ROWS, SC_ROWS)
            pltpu.sync_copy(z_vmem, o_hbm.at[p_vmem.at[0, part]])

        pltpu.emit_pipeline(
            scatter_zeros,
            grid=(n_pad // SC_WINDOW, parts),
            in_specs=[pl.BlockSpec((SC_ROWS, w), lambda i, j: (0, 0)),
                      pl.BlockSpec((1, SC_WINDOW), lambda i, j: (0, i))],
            out_specs=[],
            core_axis_name=("c", "s"),
            dimension_semantics=sem,
        )(z_hbm, p_hbm)

    return dispatch(rows, d0, d1, zeros, pad_idx)


def _sc_gather(src, idx):
    n_out = idx.shape[1]
    w = src.shape[1]
    parts = SC_WINDOW // SC_ROWS

    @pl.kernel(out_type=jax.ShapeDtypeStruct((n_out, w), src.dtype), mesh=_sc_mesh())
    def gather(x_hbm, i_hbm, o_hbm):
        def gather_rows(i_vmem, o_vmem):
            part = pl.ds(pl.program_id(1) * SC_ROWS, SC_ROWS)
            pltpu.sync_copy(x_hbm.at[i_vmem.at[0, part]], o_vmem)

        pltpu.emit_pipeline(
            gather_rows,
            grid=(n_out // SC_WINDOW, parts),
            in_specs=[pl.BlockSpec((1, SC_WINDOW), lambda i, j: (0, i))],
            out_specs=[pl.BlockSpec((SC_ROWS, w), lambda i, j: (parts * i + j, 0))],
            core_axis_name=("c", "s"),
            dimension_semantics=(pltpu.PARALLEL, pltpu.ARBITRARY),
        )(i_hbm, o_hbm)

    return gather(src, idx)


def _expert_kernel(meta_ref, x_ref, w1_hbm, w3_hbm, w2_hbm, y_ref,
                   w1_buf, w3_buf, w2_buf, w1_c, w3_c, w2_c, sems, *, layer, nb):
    def weight_copies(expert, s):
        return (pltpu.make_async_copy(w1_hbm.at[layer, expert], w1_buf.at[s], sems.at[s, 0]),
                pltpu.make_async_copy(w3_hbm.at[layer, expert], w3_buf.at[s], sems.at[s, 1]),
                pltpu.make_async_copy(w2_hbm.at[layer, expert], w2_buf.at[s], sems.at[s, 2]))

    def start_weights(expert, s):
        for cp, priority in zip(weight_copies(expert, s), (1, 1, 0)):
            cp.start(priority=priority)

    @pl.when(pl.program_id(0) == 0)
    def _():
        start_weights(meta_ref[0], 0)
        second = meta_ref[3 * nb + 1]

        @pl.when(second >= 0)
        def _():
            start_weights(second, 1)

    for j in range(EXPERT_BLOCKS_PER_STEP):
        i = pl.program_id(0) * EXPERT_BLOCKS_PER_STEP + j
        rows = slice(j * MOE_BLOCK, (j + 1) * MOE_BLOCK)
        e = meta_ref[i]
        slot = meta_ref[nb + i]
        nxt = meta_ref[2 * nb + i]
        used = i < meta_ref[3 * nb]
        first = used & ((i == 0) | (e != meta_ref[jnp.maximum(i - 1, 0)]))

        @pl.when(first)
        def _():
            for cp in weight_copies(e, slot):
                cp.wait()

            @pl.when(nxt >= 0)
            def _():
                start_weights(nxt, (slot + EXPERT_STAGES - 1) % EXPERT_STAGES)

            w1_c[...] = w1_buf[slot].astype(jnp.bfloat16)
            w3_c[...] = w3_buf[slot].astype(jnp.bfloat16)
            w2_c[...] = w2_buf[slot].astype(jnp.bfloat16)

        @pl.when(used)
        def _():
            x_hi, x_lo = _unpack_bf16_pairs(x_ref[rows, :])
            xb = jnp.concatenate([x_hi, x_lo], axis=-1).astype(jnp.bfloat16)
            full = 256
            tail = jnp.concatenate([w1_c[:, full:], w3_c[:, full:]], axis=-1)
            half_rows = MOE_BLOCK // 2
            h_tail = jnp.concatenate([_bdot(xb[:half_rows], tail), _bdot(xb[half_rows:], tail)], axis=0)
            h1 = jnp.concatenate([_bdot(xb, w1_c[:, :full]), h_tail[:, :LANES]], axis=-1)
            h3 = jnp.concatenate([_bdot(xb, w3_c[:, :full]), h_tail[:, LANES:]], axis=-1)
            act = (h1 * jax.nn.sigmoid(h1) * h3).astype(jnp.bfloat16)
            y_ref[rows, :] = _pack_bf16_pairs(_bdot(act, w2_c[...]))

        @pl.when(jnp.logical_not(used))
        def _():
            y_ref[rows, :] = jnp.zeros((MOE_BLOCK, y_ref.shape[1]), y_ref.dtype)


def _experts(x_slots, blk_meta, w1, w3, w2, layer, nb):
    dp = x_slots.shape[1]
    d, de = w1.shape[-2:]
    step_rows = EXPERT_BLOCKS_PER_STEP * MOE_BLOCK
    assert nb % EXPERT_BLOCKS_PER_STEP == 0
    last_used_step = lambda s: (s[3 * nb] - 1) // EXPERT_BLOCKS_PER_STEP
    grid_spec = pltpu.PrefetchScalarGridSpec(
        num_scalar_prefetch=1,
        grid=(nb // EXPERT_BLOCKS_PER_STEP,),
        in_specs=[
            pl.BlockSpec((step_rows, dp), lambda i, s: (jnp.minimum(i, last_used_step(s)), 0)),
            pl.BlockSpec(memory_space=pl.ANY),
            pl.BlockSpec(memory_space=pl.ANY),
            pl.BlockSpec(memory_space=pl.ANY),
        ],
        out_specs=pl.BlockSpec((step_rows, dp), lambda i, s: (i, 0)),
        scratch_shapes=[
            pltpu.VMEM((EXPERT_STAGES, d, de), jnp.float32),
            pltpu.VMEM((EXPERT_STAGES, d, de), jnp.float32),
            pltpu.VMEM((EXPERT_STAGES, de, d), jnp.float32),
            pltpu.VMEM((d, de), jnp.bfloat16),
            pltpu.VMEM((d, de), jnp.bfloat16),
            pltpu.VMEM((de, d), jnp.bfloat16),
            pltpu.SemaphoreType.DMA((EXPERT_STAGES, 3)),
        ],
    )
    return pl.pallas_call(
        functools.partial(_expert_kernel, layer=layer, nb=nb),
        grid_spec=grid_spec,
        out_shape=jax.ShapeDtypeStruct((nb * MOE_BLOCK, dp), jnp.uint32),
        compiler_params=_params(("arbitrary",)),
        name="moe_experts",
    )(blk_meta, x_slots, w1, w3, w2)


def _combine_kernel(x_ref, y1_ref, y2_ref, info_ref, gt_ref, o_ref):
    o_ref[...] = _moe_combined(x_ref[...], y1_ref, y2_ref, info_ref, gt_ref)


def _combine(x2d, y_pairs, info, gt, seq):
    t, d = x2d.shape
    tm = min(COMBINE_TILE, seq)
    tiles_per_batch = seq // tm
    second = t // tm
    return pl.pallas_call(
        _combine_kernel,
        grid=(t // tm,),
        in_specs=[
            pl.BlockSpec((tm, d), lambda i: (i, 0)),
            pl.BlockSpec((tm, d // 2), lambda i: (i, 0)),
            pl.BlockSpec((tm, d // 2), lambda i: (i + second, 0)),
            pl.BlockSpec((tm, ROUTER_LANES), lambda i: (i, 0)),
            pl.BlockSpec((None, 1, d), lambda i: (i // tiles_per_batch, 0, 0)),
        ],
        out_specs=pl.BlockSpec((tm, d), lambda i: (i, 0)),
        out_shape=jax.ShapeDtypeStruct((t, d), jnp.float32),
        compiler_params=_params(("arbitrary",)),
        name="moe_combine",
    )(x2d, y_pairs, y_pairs, info, gt)


def _slot_plan(info_t, cnt, t):
    counts = cnt[:, 0].astype(jnp.int32)
    padded = (counts + MOE_BLOCK - 1) // MOE_BLOCK * MOE_BLOCK
    pad_ends = jnp.cumsum(padded)
    pad_starts = pad_ends - padded
    nb = -(-(2 * t) // MOE_BLOCK) + N_EXPERTS
    n_slots = nb * MOE_BLOCK
    it = info_t.astype(jnp.int32)
    onehot_start = lambda e: jnp.sum(
        jnp.where(e[None, :] == jnp.arange(N_EXPERTS, dtype=jnp.int32)[:, None],
                  pad_starts[:, None], 0), axis=0)
    dest1 = (onehot_start(it[0]) + it[2]).reshape(1, t)
    dest2 = (onehot_start(it[1]) + it[3]).reshape(1, t)
    lane = jnp.arange(MOE_BLOCK, dtype=jnp.int32)[None, :]
    n_padding = (padded - counts)[:, None]
    wrapped = (pad_starts + counts)[:, None] + lane % jnp.maximum(n_padding, 1)
    pad_idx = jnp.where(n_padding > 0, wrapped, n_slots + lane % SC_WINDOW).reshape(-1)
    n_real = pad_ends[-1] // MOE_BLOCK
    n_used = -(-n_real // EXPERT_BLOCKS_PER_STEP) * EXPERT_BLOCKS_PER_STEP
    tail = jnp.arange((EXPERT_BLOCKS_PER_STEP - 1) * MOE_BLOCK, dtype=jnp.int32)
    tail_idx = jnp.where(tail < (n_used - n_real) * MOE_BLOCK, pad_ends[-1] + tail,
                         n_slots + tail % SC_WINDOW)
    pad_idx = jnp.concatenate([pad_idx, tail_idx]).reshape(1, -1)
    experts = jnp.arange(N_EXPERTS, dtype=jnp.int32)
    blk = jnp.arange(nb, dtype=jnp.int32)
    blk_exp = jnp.minimum(
        jnp.sum((pad_ends[None, :] <= (blk * MOE_BLOCK)[:, None]).astype(jnp.int32), axis=1),
        N_EXPERTS - 1)
    blk_exp = jnp.where(blk >= n_real, jnp.max(jnp.where(padded > 0, experts, 0)), blk_exp)
    prev_exp = jnp.concatenate([jnp.full((1,), -1, jnp.int32), blk_exp[:-1]])
    is_first = (blk < n_used) & (blk_exp != prev_exp)
    blk_slot = (jnp.cumsum(is_first.astype(jnp.int32)) + EXPERT_STAGES - 1) % EXPERT_STAGES
    later = (experts[None, :] > experts[:, None]) & (padded[None, :] > 0)
    nxt_of = jnp.min(jnp.where(later, experts[None, :], N_EXPERTS), axis=1)
    nxt_of = jnp.where(nxt_of == N_EXPERTS, -1, nxt_of)
    lookup = lambda table, e: jnp.sum(
        jnp.where(e[:, None] == experts[None, :], table[None, :], 0), axis=1)
    nxt2_of = jnp.where(nxt_of >= 0, lookup(nxt_of, jnp.maximum(nxt_of, 0)), -1)
    blk_nxt = lookup(nxt2_of, blk_exp)
    second = lookup(nxt_of, blk_exp[:1])
    blk_meta = jnp.concatenate([blk_exp, blk_slot, blk_nxt, n_used[None], second]).astype(jnp.int32)
    return dest1, dest2, pad_idx, blk_meta, nb, n_slots


def _mixer_out_and_moe(a, w_out, x2d, gt1, g, sc, sh, w_group, b_group, w_router, b_router,
                       w1, w3, w2, layer, seq):
    t, d = x2d.shape
    w_cat = jnp.zeros((ROUTER_LANES, d), jnp.float32)
    w_cat = w_cat.at[:N_EXPERTS].set(w_router.T).at[N_EXPERTS:N_EXPERTS + N_GROUPS].set(w_group.T)
    b_cat = jnp.zeros((ROUTER_LANES, 1), jnp.float32)
    b_cat = b_cat.at[:N_EXPERTS, 0].set(b_router).at[N_EXPERTS:N_EXPERTS + N_GROUPS, 0].set(b_group)
    w_hi, w_lo = _split_hi_lo(w_cat)
    x_new, hn, info, info_t, cnt = _outproj_router(a, w_out, x2d, gt1, g, sc, sh, w_hi, w_lo,
                                                   b_cat, seq)
    dest1, dest2, pad_idx, blk_meta, nb, n_slots = _slot_plan(info_t, cnt, t)
    x_slots = _sc_dispatch(hn, dest1, dest2, pad_idx, n_slots)
    y_slots = _experts(x_slots, blk_meta, w1, w3, w2, layer, nb)
    y_pairs = _sc_gather(y_slots, jnp.concatenate([dest1, dest2], axis=1))
    return x_new, y_pairs, info


def kernel(x, c, w_ada, b_ada, norm1_g, norm2_g, ml_w_in, ml_b_gate, ml_g_out, ml_w_out,
           sw_w_in, sw_g_q, sw_g_k, sw_sinks, sw_w_out, moe_w_group, moe_b_group,
           moe_w_router, moe_b_router, moe_w1, moe_w3, moe_w2):
    bsz, seq, d = x.shape
    depth = w_ada.shape[0]
    bf = jnp.bfloat16
    mod = _ada_mod(c, w_ada, b_ada)
    x2d = x.reshape(bsz * seq, d)
    pending = None
    for layer in range(depth):
        sh1, sc1, gt1, sh2, sc2, gt2 = [
            mod[layer, :, i * d:(i + 1) * d].reshape(bsz, 1, d) for i in range(6)]
        j = layer // 2
        if layer % 2 == 0:
            w = ml_w_in[j]
            q_w, k_w = w[:, :ML_QK], w[:, ML_QK:2 * ML_QK]
            v_w = w[:, 2 * ML_QK:2 * ML_QK + ML_V]
            o_w = w[:, 2 * ML_QK + ML_V:2 * ML_QK + 2 * ML_V]
            g_w = w[:, 2 * ML_QK + 2 * ML_V:]
            w_main = jnp.concatenate([v_w, o_w, q_w], axis=1).astype(bf)
            wg_t = jnp.zeros((ML_GATE_ROWS, d), jnp.float32).at[:2 * ML_HEADS].set(g_w.T)
            wg_hi, wg_lo = _split_hi_lo(wg_t)
            wk_t = k_w.T.astype(bf)
            w_out = ml_w_out[j].astype(bf)
            outs = _inproj(x2d, norm1_g[layer], sc1, sh1, w_main, seq,
                           ml_extra=(wk_t, wg_hi, wg_lo),
                           q_cols=(2 * ML_V, 2 * ML_V + ML_QK), q_scale=ML_DK ** -0.5,
                           gate_cols=(ML_V, 2 * ML_V), pending=pending)
            if pending is not None:
                x2d, outs = outs[0], outs[1:]
            main, k_t, g_t = outs
            a = _mlstm_core(main, k_t, g_t, ml_b_gate[j], ml_g_out[j], bsz, seq)
        else:
            w = sw_w_in[j]
            dq = SW_Q_HEADS * SW_DH
            dkv = SW_KV_HEADS * SW_DH
            dup = lambda m: jnp.concatenate(
                [m.reshape(d, SW_KV_HEADS, 1, SW_DH)] * 2, axis=2).reshape(d, 2 * dkv)
            w_main = jnp.concatenate(
                [w[:, :dq], dup(w[:, dq:dq + dkv]), dup(w[:, dq + dkv:])], axis=1).astype(bf)
            w_out = sw_w_out[j].astype(bf)
            gq = jnp.concatenate([sw_g_q[j], sw_g_q[j]]).reshape(1, LANES) * (SW_DH ** -0.5 * LOG2E)
            gk = jnp.concatenate([sw_g_k[j], sw_g_k[j]]).reshape(1, LANES)
            outs = _inproj(x2d, norm1_g[layer], sc1, sh1, w_main, seq, pending=pending,
                           qk_norm=(dq, dq + 2 * dkv, gq, gk))
            if pending is not None:
                x2d, outs = outs[0], outs[1:]
            a = _swa_core(outs[0], sw_sinks[j], bsz, seq)
        x2d, y_pairs, info = _mixer_out_and_moe(
            a, w_out, x2d, gt1, norm2_g[layer], sc2, sh2, moe_w_group[layer], moe_b_group[layer],
            moe_w_router[layer], moe_b_router[layer], moe_w1, moe_w3, moe_w2, layer, seq)
        pending = (y_pairs, info, gt2)
    y_pairs, info, gt2 = pending
    return _combine(x2d, y_pairs, info, gt2, seq).reshape(bsz, seq, d)
```
